```python
import math
import jax, jax.numpy as jnp
from jax import lax
import numpy as np

D_MODEL = 1024
BATCH = 16
SEQ = 4096
DEPTH = 2

N_EVEN = (DEPTH + 1) // 2
N_ODD = DEPTH // 2

SSM_WIDTH = D_MODEL // 4
SSM_GROUP = 16
SSM_GROUPS = SSM_WIDTH // SSM_GROUP
SSM_STATE = 64
GMLP_WIDTH = D_MODEL - SSM_WIDTH
GMLP_HEAD = 128
GMLP_HEADS = GMLP_WIDTH // GMLP_HEAD
CHUNK = 128
EVEN_IN = SSM_WIDTH + 2 * GMLP_WIDTH

CONV_WIDTH = 3
D_FF = 2816
EPS = 1e-6
DT_MIN = 1e-3
DT_MAX = 1e-1
LAMBDA_RE_MAX = -1e-4
RESID_SCALE = (2 * DEPTH) ** -0.5

kernel_name = "hybrid_s5_gmlp_shortconv_convffn"


def rmsnorm(x, g):
    xf = x.astype(jnp.float32)
    y = xf * lax.rsqrt(jnp.mean(xf * xf, axis=-1, keepdims=True) + EPS)
    return (y * g.astype(jnp.float32)).astype(x.dtype)


def causal_dwconv(x, w, b):
    k_w = w.shape[0]
    s = x.shape[1]
    xp = jnp.pad(x, ((0, 0), (k_w - 1, 0), (0, 0)))
    return b + sum(w[k] * xp[:, k:k + s] for k in range(k_w))


def s5_mixer(u, lam_re, lam_im, log_dt, b_re, b_im, c_re, c_im, d_skip, w_glu, b_glu):
    bsz, s, _ = u.shape
    uf = u.astype(jnp.float32).reshape(bsz, s, SSM_GROUPS, SSM_GROUP)
    lr = jnp.minimum(lam_re.astype(jnp.float32), LAMBDA_RE_MAX)
    li = lam_im.astype(jnp.float32)
    dt = jnp.exp(log_dt.astype(jnp.float32))[:, None]
    mag = jnp.exp(lr * dt)
    ab_re = mag * jnp.cos(li * dt)
    ab_im = mag * jnp.sin(li * dt)
    den = lr * lr + li * li
    nr = ab_re - 1.0
    ni = ab_im
    z_re = ((nr * lr + ni * li) / den)[..., None]
    z_im = ((ni * lr - nr * li) / den)[..., None]
    br = b_re.astype(jnp.float32)
    bi = b_im.astype(jnp.float32)
    bb_re = z_re * br - z_im * bi
    bb_im = z_re * bi + z_im * br
    x_re = jnp.einsum('bsgh,gph->bsgp', uf, bb_re)
    x_im = jnp.einsum('bsgh,gph->bsgp', uf, bb_im)
    a_re = jnp.broadcast_to(ab_re, (1, s) + ab_re.shape)
    a_im = jnp.broadcast_to(ab_im, (1, s) + ab_im.shape)

    def combine(left, right):
        a1r, a1i, b1r, b1i = left
        a2r, a2i, b2r, b2i = right
        return (a2r * a1r - a2i * a1i,
                a2r * a1i + a2i * a1r,
                a2r * b1r - a2i * b1i + b2r,
                a2r * b1i + a2i * b1r + b2i)

    _, _, h_re, h_im = lax.associative_scan(combine, (a_re, a_im, x_re, x_im), axis=1)
    y = (jnp.einsum('bsgp,ghp->bsgh', h_re, c_re.astype(jnp.float32))
         - jnp.einsum('bsgp,ghp->bsgh', h_im, c_im.astype(jnp.float32)))
    y = (y + d_skip.astype(jnp.float32).reshape(SSM_GROUPS, SSM_GROUP) * uf).reshape(bsz, s, SSM_WIDTH)
    y = jax.nn.gelu(y)
    y = y * jax.nn.sigmoid(y @ w_glu.astype(jnp.float32) + b_glu.astype(jnp.float32))
    return y.astype(u.dtype)


def gmlp_mixer(uv, w_s, b_s, g_v):
    bsz, s, _ = uv.shape
    u, v = jnp.split(jax.nn.gelu(uv), 2, axis=-1)
    v = rmsnorm(v, g_v).reshape(bsz, s // CHUNK, CHUNK, GMLP_HEADS, GMLP_HEAD)
    mask = jnp.tril(jnp.ones((CHUNK, CHUNK), dtype=bool))
    w = jnp.where(mask, w_s, 0)
    gate = jnp.einsum('hts,bnshc->bnthc', w, v) + b_s.T[None, None, :, :, None]
    return u * gate.reshape(bsz, s, GMLP_WIDTH)


def shortconv_mixer(p, w_conv, b_conv):
    bg, cg, hx = jnp.split(p, 3, axis=-1)
    return bg * causal_dwconv(cg * hx, w_conv, b_conv)


def conv_ffn(x, w_up, w_conv, b_conv, w_down):
    h = causal_dwconv(x @ w_up, w_conv, b_conv)
    gate, val = jnp.split(h, 2, axis=-1)
    return (jax.nn.silu(gate) * val) @ w_down


def _fwd_setup_inputs(seed: int = 0) -> dict:
    key = jax.random.key(seed)
    ks = iter(jax.random.split(key, 32))
    f32 = jnp.float32
    nrm = lambda shape, std: std * jax.random.normal(next(ks), shape, f32)
    d = D_MODEL
    inp = {}
    inp["x"] = nrm((BATCH, SEQ, d), 1.0)
    inp["mix_norm_g"] = 1.0 + nrm((DEPTH, d), 0.02)
    inp["ffn_norm_g"] = 1.0 + nrm((DEPTH, d), 0.02)
    inp["final_norm_g"] = 1.0 + nrm((d,), 0.02)
    inp["ev_w_in"] = nrm((N_EVEN, d, EVEN_IN), d ** -0.5)
    inp["ev_w_out"] = nrm((N_EVEN, d, d), d ** -0.5 * RESID_SCALE)
    inp["s5_lam_re"] = -0.5 + nrm((N_EVEN, SSM_GROUPS, SSM_STATE), 0.01)
    n_idx = jnp.arange(SSM_STATE, dtype=f32)
    inp["s5_lam_im"] = math.pi * n_idx + nrm((N_EVEN, SSM_GROUPS, SSM_STATE), 0.01)
    inp["s5_log_dt"] = jax.random.uniform(next(ks), (N_EVEN, SSM_GROUPS), f32,
                                          math.log(DT_MIN), math.log(DT_MAX))
    inp["s5_b_re"] = nrm((N_EVEN, SSM_GROUPS, SSM_STATE, SSM_GROUP), (2 * SSM_GROUP) ** -0.5)
    inp["s5_b_im"] = nrm((N_EVEN, SSM_GROUPS, SSM_STATE, SSM_GROUP), (2 * SSM_GROUP) ** -0.5)
    inp["s5_c_re"] = nrm((N_EVEN, SSM_GROUPS, SSM_GROUP, SSM_STATE), SSM_STATE ** -0.5)
    inp["s5_c_im"] = nrm((N_EVEN, SSM_GROUPS, SSM_GROUP, SSM_STATE), SSM_STATE ** -0.5)
    inp["s5_d"] = nrm((N_EVEN, SSM_WIDTH), 1.0)
    inp["s5_w_glu"] = nrm((N_EVEN, SSM_WIDTH, SSM_WIDTH), SSM_WIDTH ** -0.5)
    inp["s5_b_glu"] = nrm((N_EVEN, SSM_WIDTH), 0.01)
    inp["gm_w_s"] = nrm((N_EVEN, GMLP_HEADS, CHUNK, CHUNK), CHUNK ** -0.5)
    inp["gm_b_s"] = 1.0 + nrm((N_EVEN, GMLP_HEADS, CHUNK), 0.01)
    inp["gm_v_g"] = 1.0 + nrm((N_EVEN, GMLP_WIDTH), 0.02)
    inp["od_w_in"] = nrm((N_ODD, d, 3 * d), d ** -0.5)
    inp["od_conv_w"] = nrm((N_ODD, CONV_WIDTH, d), CONV_WIDTH ** -0.5)
    inp["od_conv_b"] = nrm((N_ODD, d), 0.01)
    inp["od_w_out"] = nrm((N_ODD, d, d), d ** -0.5 * RESID_SCALE)
    inp["ffn_w_up"] = nrm((DEPTH, d, 2 * D_FF), d ** -0.5)
    inp["ffn_conv_w"] = nrm((DEPTH, CONV_WIDTH, 2 * D_FF), CONV_WIDTH ** -0.5)
    inp["ffn_conv_b"] = nrm((DEPTH, 2 * D_FF), 0.01)
    inp["ffn_w_down"] = nrm((DEPTH, D_FF, d), D_FF ** -0.5 * RESID_SCALE)
    return inp


def _fwd_reference(x, mix_norm_g, ffn_norm_g, final_norm_g,
              ev_w_in, ev_w_out, s5_lam_re, s5_lam_im, s5_log_dt,
              s5_b_re, s5_b_im, s5_c_re, s5_c_im, s5_d, s5_w_glu, s5_b_glu,
              gm_w_s, gm_b_s, gm_v_g,
              od_w_in, od_conv_w, od_conv_b, od_w_out,
              ffn_w_up, ffn_conv_w, ffn_conv_b, ffn_w_down):
    h = x
    for layer in range(DEPTH):
        y = rmsnorm(h, mix_norm_g[layer])
        if layer % 2 == 0:
            e = layer // 2
            p = y @ ev_w_in[e]
            a_out = s5_mixer(p[..., :SSM_WIDTH], s5_lam_re[e], s5_lam_im[e], s5_log_dt[e],
                             s5_b_re[e], s5_b_im[e], s5_c_re[e], s5_c_im[e],
                             s5_d[e], s5_w_glu[e], s5_b_glu[e])
            b_out = gmlp_mixer(p[..., SSM_WIDTH:], gm_w_s[e], gm_b_s[e], gm_v_g[e])
            mix = jnp.concatenate([a_out, b_out], axis=-1) @ ev_w_out[e]
        else:
            o = layer // 2
            mix = shortconv_mixer(y @ od_w_in[o], od_conv_w[o], od_conv_b[o]) @ od_w_out[o]
        h = h + mix
        h = h + conv_ffn(rmsnorm(h, ffn_norm_g[layer]), ffn_w_up[layer], ffn_conv_w[layer],
                         ffn_conv_b[layer], ffn_w_down[layer])
    return rmsnorm(h, final_norm_g)


import jax as _jax
import jax.numpy as _jnp

TWIN_FORMAT = 'train_step'
FWD_PARAMS = ['x', 'mix_norm_g', 'ffn_norm_g', 'final_norm_g', 'ev_w_in', 'ev_w_out', 's5_lam_re', 's5_lam_im', 's5_log_dt', 's5_b_re', 's5_b_im', 's5_c_re', 's5_c_im', 's5_d', 's5_w_glu', 's5_b_glu', 'gm_w_s', 'gm_b_s', 'gm_v_g', 'od_w_in', 'od_conv_w', 'od_conv_b', 'od_w_out', 'ffn_w_up', 'ffn_conv_w', 'ffn_conv_b', 'ffn_w_down']
TWIN_WEIGHTS = ['mix_norm_g', 'ffn_norm_g', 'final_norm_g', 'ev_w_in', 'ev_w_out', 's5_lam_re', 's5_lam_im', 's5_log_dt', 's5_b_re', 's5_b_im', 's5_c_re', 's5_c_im', 's5_d', 's5_w_glu', 's5_b_glu', 'gm_w_s', 'gm_b_s', 'gm_v_g', 'od_w_in', 'od_conv_w', 'od_conv_b', 'od_w_out', 'ffn_w_up', 'ffn_conv_w', 'ffn_conv_b', 'ffn_w_down']
TWIN_DIFF_INPUT = 'x'
TWIN_INPUTS = ['x', 'mix_norm_g', 'ffn_norm_g', 'final_norm_g', 'ev_w_in', 'ev_w_out', 's5_lam_re', 's5_lam_im', 's5_log_dt', 's5_b_re', 's5_b_im', 's5_c_re', 's5_c_im', 's5_d', 's5_w_glu', 's5_b_glu', 'gm_w_s', 'gm_b_s', 'gm_v_g', 'od_w_in', 'od_conv_w', 'od_conv_b', 'od_w_out', 'ffn_w_up', 'ffn_conv_w', 'ffn_conv_b', 'ffn_w_down', 'loss_target', 'm_mix_norm_g', 'm_ffn_norm_g', 'm_final_norm_g', 'm_ev_w_in', 'm_ev_w_out', 'm_s5_lam_re', 'm_s5_lam_im', 'm_s5_log_dt', 'm_s5_b_re', 'm_s5_b_im', 'm_s5_c_re', 'm_s5_c_im', 'm_s5_d', 'm_s5_w_glu', 'm_s5_b_glu', 'm_gm_w_s', 'm_gm_b_s', 'm_gm_v_g', 'm_od_w_in', 'm_od_conv_w', 'm_od_conv_b', 'm_od_w_out', 'm_ffn_w_up', 'm_ffn_conv_w', 'm_ffn_conv_b', 'm_ffn_w_down', 'v_mix_norm_g', 'v_ffn_norm_g', 'v_final_norm_g', 'v_ev_w_in', 'v_ev_w_out', 'v_s5_lam_re', 'v_s5_lam_im', 'v_s5_log_dt', 'v_s5_b_re', 'v_s5_b_im', 'v_s5_c_re', 'v_s5_c_im', 'v_s5_d', 'v_s5_w_glu', 'v_s5_b_glu', 'v_gm_w_s', 'v_gm_b_s', 'v_gm_v_g', 'v_od_w_in', 'v_od_conv_w', 'v_od_conv_b', 'v_od_w_out', 'v_ffn_w_up', 'v_ffn_conv_w', 'v_ffn_conv_b', 'v_ffn_w_down']
TWIN_OUTPUTS = ['loss', 'grad_x', 'grad_mix_norm_g', 'grad_ffn_norm_g', 'grad_final_norm_g', 'grad_ev_w_in', 'grad_ev_w_out', 'grad_s5_lam_re', 'grad_s5_lam_im', 'grad_s5_log_dt', 'grad_s5_b_re', 'grad_s5_b_im', 'grad_s5_c_re', 'grad_s5_c_im', 'grad_s5_d', 'grad_s5_w_glu', 'grad_s5_b_glu', 'grad_gm_w_s', 'grad_gm_b_s', 'grad_gm_v_g', 'grad_od_w_in', 'grad_od_conv_w', 'grad_od_conv_b', 'grad_od_w_out', 'grad_ffn_w_up', 'grad_ffn_conv_w', 'grad_ffn_conv_b', 'grad_ffn_w_down', 'delta_mix_norm_g', 'delta_ffn_norm_g', 'delta_final_norm_g', 'delta_ev_w_in', 'delta_ev_w_out', 'delta_s5_lam_re', 'delta_s5_lam_im', 'delta_s5_log_dt', 'delta_s5_b_re', 'delta_s5_b_im', 'delta_s5_c_re', 'delta_s5_c_im', 'delta_s5_d', 'delta_s5_w_glu', 'delta_s5_b_glu', 'delta_gm_w_s', 'delta_gm_b_s', 'delta_gm_v_g', 'delta_od_w_in', 'delta_od_conv_w', 'delta_od_conv_b', 'delta_od_w_out', 'delta_ffn_w_up', 'delta_ffn_conv_w', 'delta_ffn_conv_b', 'delta_ffn_w_down', 'new_m_mix_norm_g', 'new_m_ffn_norm_g', 'new_m_final_norm_g', 'new_m_ev_w_in', 'new_m_ev_w_out', 'new_m_s5_lam_re', 'new_m_s5_lam_im', 'new_m_s5_log_dt', 'new_m_s5_b_re', 'new_m_s5_b_im', 'new_m_s5_c_re', 'new_m_s5_c_im', 'new_m_s5_d', 'new_m_s5_w_glu', 'new_m_s5_b_glu', 'new_m_gm_w_s', 'new_m_gm_b_s', 'new_m_gm_v_g', 'new_m_od_w_in', 'new_m_od_conv_w', 'new_m_od_conv_b', 'new_m_od_w_out', 'new_m_ffn_w_up', 'new_m_ffn_conv_w', 'new_m_ffn_conv_b', 'new_m_ffn_w_down', 'new_v_mix_norm_g', 'new_v_ffn_norm_g', 'new_v_final_norm_g', 'new_v_ev_w_in', 'new_v_ev_w_out', 'new_v_s5_lam_re', 'new_v_s5_lam_im', 'new_v_s5_log_dt', 'new_v_s5_b_re', 'new_v_s5_b_im', 'new_v_s5_c_re', 'new_v_s5_c_im', 'new_v_s5_d', 'new_v_s5_w_glu', 'new_v_s5_b_glu', 'new_v_gm_w_s', 'new_v_gm_b_s', 'new_v_gm_v_g', 'new_v_od_w_in', 'new_v_od_conv_w', 'new_v_od_conv_b', 'new_v_od_w_out', 'new_v_ffn_w_up', 'new_v_ffn_conv_w', 'new_v_ffn_conv_b', 'new_v_ffn_w_down']
TWIN_LEAF_KINDS = {'loss': 'loss', 'grad_x': 'grad_x', 'grad_mix_norm_g': 'grad_w', 'grad_ffn_norm_g': 'grad_w', 'grad_final_norm_g': 'grad_w', 'grad_ev_w_in': 'grad_w', 'grad_ev_w_out': 'grad_w', 'grad_s5_lam_re': 'grad_w', 'grad_s5_lam_im': 'grad_w', 'grad_s5_log_dt': 'grad_w', 'grad_s5_b_re': 'grad_w', 'grad_s5_b_im': 'grad_w', 'grad_s5_c_re': 'grad_w', 'grad_s5_c_im': 'grad_w', 'grad_s5_d': 'grad_w', 'grad_s5_w_glu': 'grad_w', 'grad_s5_b_glu': 'grad_w', 'grad_gm_w_s': 'grad_w', 'grad_gm_b_s': 'grad_w', 'grad_gm_v_g': 'grad_w', 'grad_od_w_in': 'grad_w', 'grad_od_conv_w': 'grad_w', 'grad_od_conv_b': 'grad_w', 'grad_od_w_out': 'grad_w', 'grad_ffn_w_up': 'grad_w', 'grad_ffn_conv_w': 'grad_w', 'grad_ffn_conv_b': 'grad_w', 'grad_ffn_w_down': 'grad_w', 'delta_mix_norm_g': 'delta_w', 'delta_ffn_norm_g': 'delta_w', 'delta_final_norm_g': 'delta_w', 'delta_ev_w_in': 'delta_w', 'delta_ev_w_out': 'delta_w', 'delta_s5_lam_re': 'delta_w', 'delta_s5_lam_im': 'delta_w', 'delta_s5_log_dt': 'delta_w', 'delta_s5_b_re': 'delta_w', 'delta_s5_b_im': 'delta_w', 'delta_s5_c_re': 'delta_w', 'delta_s5_c_im': 'delta_w', 'delta_s5_d': 'delta_w', 'delta_s5_w_glu': 'delta_w', 'delta_s5_b_glu': 'delta_w', 'delta_gm_w_s': 'delta_w', 'delta_gm_b_s': 'delta_w', 'delta_gm_v_g': 'delta_w', 'delta_od_w_in': 'delta_w', 'delta_od_conv_w': 'delta_w', 'delta_od_conv_b': 'delta_w', 'delta_od_w_out': 'delta_w', 'delta_ffn_w_up': 'delta_w', 'delta_ffn_conv_w': 'delta_w', 'delta_ffn_conv_b': 'delta_w', 'delta_ffn_w_down': 'delta_w', 'new_m_mix_norm_g': 'new_m', 'new_m_ffn_norm_g': 'new_m', 'new_m_final_norm_g': 'new_m', 'new_m_ev_w_in': 'new_m', 'new_m_ev_w_out': 'new_m', 'new_m_s5_lam_re': 'new_m', 'new_m_s5_lam_im': 'new_m', 'new_m_s5_log_dt': 'new_m', 'new_m_s5_b_re': 'new_m', 'new_m_s5_b_im': 'new_m', 'new_m_s5_c_re': 'new_m', 'new_m_s5_c_im': 'new_m', 'new_m_s5_d': 'new_m', 'new_m_s5_w_glu': 'new_m', 'new_m_s5_b_glu': 'new_m', 'new_m_gm_w_s': 'new_m', 'new_m_gm_b_s': 'new_m', 'new_m_gm_v_g': 'new_m', 'new_m_od_w_in': 'new_m', 'new_m_od_conv_w': 'new_m', 'new_m_od_conv_b': 'new_m', 'new_m_od_w_out': 'new_m', 'new_m_ffn_w_up': 'new_m', 'new_m_ffn_conv_w': 'new_m', 'new_m_ffn_conv_b': 'new_m', 'new_m_ffn_w_down': 'new_m', 'new_v_mix_norm_g': 'new_v', 'new_v_ffn_norm_g': 'new_v', 'new_v_final_norm_g': 'new_v', 'new_v_ev_w_in': 'new_v', 'new_v_ev_w_out': 'new_v', 'new_v_s5_lam_re': 'new_v', 'new_v_s5_lam_im': 'new_v', 'new_v_s5_log_dt': 'new_v', 'new_v_s5_b_re': 'new_v', 'new_v_s5_b_im': 'new_v', 'new_v_s5_c_re': 'new_v', 'new_v_s5_c_im': 'new_v', 'new_v_s5_d': 'new_v', 'new_v_s5_w_glu': 'new_v', 'new_v_s5_b_glu': 'new_v', 'new_v_gm_w_s': 'new_v', 'new_v_gm_b_s': 'new_v', 'new_v_gm_v_g': 'new_v', 'new_v_od_w_in': 'new_v', 'new_v_od_conv_w': 'new_v', 'new_v_od_conv_b': 'new_v', 'new_v_od_w_out': 'new_v', 'new_v_ffn_w_up': 'new_v', 'new_v_ffn_conv_w': 'new_v', 'new_v_ffn_conv_b': 'new_v', 'new_v_ffn_w_down': 'new_v'}


def _forward(args):
    return _fwd_reference(*[args[k] for k in FWD_PARAMS])


def _output_shape():
    out = _jax.eval_shape(lambda: _forward(_fwd_setup_inputs(0)))
    return out.shape, out.dtype

N_MICROBATCH = 1
ADAM_LR = 0.001
ADAM_B1 = 0.9
ADAM_B2 = 0.999
ADAM_EPS = 1e-08
ADAM_WD = 0.01
ADAM_STEP = 10
PER_EXAMPLE_BATCH_AXIS = {'x': 0, 'loss_target': 0}
SHARED_INPUTS = []
_WEIGHT_DTYPES = {'mix_norm_g': _jnp.float32, 'ffn_norm_g': _jnp.float32, 'final_norm_g': _jnp.float32, 'ev_w_in': _jnp.float32, 'ev_w_out': _jnp.float32, 's5_lam_re': _jnp.float32, 's5_lam_im': _jnp.float32, 's5_log_dt': _jnp.float32, 's5_b_re': _jnp.float32, 's5_b_im': _jnp.float32, 's5_c_re': _jnp.float32, 's5_c_im': _jnp.float32, 's5_d': _jnp.float32, 's5_w_glu': _jnp.float32, 's5_b_glu': _jnp.float32, 'gm_w_s': _jnp.float32, 'gm_b_s': _jnp.float32, 'gm_v_g': _jnp.float32, 'od_w_in': _jnp.float32, 'od_conv_w': _jnp.float32, 'od_conv_b': _jnp.float32, 'od_w_out': _jnp.float32, 'ffn_w_up': _jnp.float32, 'ffn_conv_w': _jnp.float32, 'ffn_conv_b': _jnp.float32, 'ffn_w_down': _jnp.float32}
MOMENT_SCALE = {'mix_norm_g': 1.601308e-01, 'ffn_norm_g': 1.087929e-01, 'final_norm_g': 6.392622e+01, 'ev_w_in': 9.428723e-02, 'ev_w_out': 2.154424e-01, 's5_lam_re': 4.168935e-03, 's5_lam_im': 5.662345e-03, 's5_log_dt': 3.359128e+00, 's5_b_re': 2.479561e-03, 's5_b_im': 2.366771e-03, 's5_c_re': 3.576649e-03, 's5_c_im': 3.380052e-03, 's5_d': 6.487550e-02, 's5_w_glu': 1.699073e-02, 's5_b_glu': 2.398208e-02, 'gm_w_s': 6.707085e-02, 'gm_b_s': 9.887909e-02, 'gm_v_g': 6.746102e-02, 'od_w_in': 1.064295e-01, 'od_conv_w': 1.100644e-01, 'od_conv_b': 1.074700e-01, 'od_w_out': 2.124039e-01, 'ffn_w_up': 4.474713e-02, 'ffn_conv_w': 4.574553e-02, 'ffn_conv_b': 4.445127e-02, 'ffn_w_down': 1.456842e-01}


def _to_microbatches(a, axis):
    t = _jnp.moveaxis(a, axis, 0)
    t = t.reshape((N_MICROBATCH, t.shape[0] // N_MICROBATCH) + t.shape[1:])
    return _jnp.moveaxis(t, 1, axis + 1)


def setup_inputs(seed: int = 0) -> dict:
    inp = _fwd_setup_inputs(seed)
    key = _jax.random.fold_in(_jax.random.key(seed), 7919)
    shape, _ = _output_shape()
    out = dict(inp)
    out["loss_target"] = _jax.random.normal(_jax.random.fold_in(key, 0), shape, _jnp.float32)
    for i, name in enumerate(TWIN_WEIGHTS):
        w = inp[name].astype(_jnp.float32)
        if MOMENT_SCALE is None:
            s = _jnp.sqrt(_jnp.mean(_jnp.square(w)) + 1e-30)
        else:
            s = MOMENT_SCALE[name]
        km, kv = _jax.random.split(_jax.random.fold_in(key, i + 1))
        out[name] = w
        out["m_" + name] = s * _jax.random.normal(km, w.shape, _jnp.float32)
        out["v_" + name] = (s * s) * _jax.random.uniform(kv, w.shape, _jnp.float32, 0.5, 1.5)
    if N_MICROBATCH > 1:
        for name, axis in PER_EXAMPLE_BATCH_AXIS.items():
            out[name] = _to_microbatches(out[name], axis)
    return {'x': out['x'], 'mix_norm_g': out['mix_norm_g'], 'ffn_norm_g': out['ffn_norm_g'], 'final_norm_g': out['final_norm_g'], 'ev_w_in': out['ev_w_in'], 'ev_w_out': out['ev_w_out'], 's5_lam_re': out['s5_lam_re'], 's5_lam_im': out['s5_lam_im'], 's5_log_dt': out['s5_log_dt'], 's5_b_re': out['s5_b_re'], 's5_b_im': out['s5_b_im'], 's5_c_re': out['s5_c_re'], 's5_c_im': out['s5_c_im'], 's5_d': out['s5_d'], 's5_w_glu': out['s5_w_glu'], 's5_b_glu': out['s5_b_glu'], 'gm_w_s': out['gm_w_s'], 'gm_b_s': out['gm_b_s'], 'gm_v_g': out['gm_v_g'], 'od_w_in': out['od_w_in'], 'od_conv_w': out['od_conv_w'], 'od_conv_b': out['od_conv_b'], 'od_w_out': out['od_w_out'], 'ffn_w_up': out['ffn_w_up'], 'ffn_conv_w': out['ffn_conv_w'], 'ffn_conv_b': out['ffn_conv_b'], 'ffn_w_down': out['ffn_w_down'], 'loss_target': out['loss_target'], 'm_mix_norm_g': out['m_mix_norm_g'], 'm_ffn_norm_g': out['m_ffn_norm_g'], 'm_final_norm_g': out['m_final_norm_g'], 'm_ev_w_in': out['m_ev_w_in'], 'm_ev_w_out': out['m_ev_w_out'], 'm_s5_lam_re': out['m_s5_lam_re'], 'm_s5_lam_im': out['m_s5_lam_im'], 'm_s5_log_dt': out['m_s5_log_dt'], 'm_s5_b_re': out['m_s5_b_re'], 'm_s5_b_im': out['m_s5_b_im'], 'm_s5_c_re': out['m_s5_c_re'], 'm_s5_c_im': out['m_s5_c_im'], 'm_s5_d': out['m_s5_d'], 'm_s5_w_glu': out['m_s5_w_glu'], 'm_s5_b_glu': out['m_s5_b_glu'], 'm_gm_w_s': out['m_gm_w_s'], 'm_gm_b_s': out['m_gm_b_s'], 'm_gm_v_g': out['m_gm_v_g'], 'm_od_w_in': out['m_od_w_in'], 'm_od_conv_w': out['m_od_conv_w'], 'm_od_conv_b': out['m_od_conv_b'], 'm_od_w_out': out['m_od_w_out'], 'm_ffn_w_up': out['m_ffn_w_up'], 'm_ffn_conv_w': out['m_ffn_conv_w'], 'm_ffn_conv_b': out['m_ffn_conv_b'], 'm_ffn_w_down': out['m_ffn_w_down'], 'v_mix_norm_g': out['v_mix_norm_g'], 'v_ffn_norm_g': out['v_ffn_norm_g'], 'v_final_norm_g': out['v_final_norm_g'], 'v_ev_w_in': out['v_ev_w_in'], 'v_ev_w_out': out['v_ev_w_out'], 'v_s5_lam_re': out['v_s5_lam_re'], 'v_s5_lam_im': out['v_s5_lam_im'], 'v_s5_log_dt': out['v_s5_log_dt'], 'v_s5_b_re': out['v_s5_b_re'], 'v_s5_b_im': out['v_s5_b_im'], 'v_s5_c_re': out['v_s5_c_re'], 'v_s5_c_im': out['v_s5_c_im'], 'v_s5_d': out['v_s5_d'], 'v_s5_w_glu': out['v_s5_w_glu'], 'v_s5_b_glu': out['v_s5_b_glu'], 'v_gm_w_s': out['v_gm_w_s'], 'v_gm_b_s': out['v_gm_b_s'], 'v_gm_v_g': out['v_gm_v_g'], 'v_od_w_in': out['v_od_w_in'], 'v_od_conv_w': out['v_od_conv_w'], 'v_od_conv_b': out['v_od_conv_b'], 'v_od_w_out': out['v_od_w_out'], 'v_ffn_w_up': out['v_ffn_w_up'], 'v_ffn_conv_w': out['v_ffn_conv_w'], 'v_ffn_conv_b': out['v_ffn_conv_b'], 'v_ffn_w_down': out['v_ffn_w_down']}


def _loss(weights, diff, rest, loss_target):
    with _jax.named_scope("forward"):
        args = {**rest, TWIN_DIFF_INPUT: diff, **{k: w.astype(_WEIGHT_DTYPES[k]) for k, w in weights.items()}}
        y = _forward(args)
    with _jax.named_scope("loss_head"):
        err = _jnp.square(y.astype(_jnp.float32) - loss_target)
        return 0.5 * _jnp.sum(_jnp.mean(err, axis=-1)) if err.ndim else 0.5 * err


def _adamw(w, g, m, v):
    m = ADAM_B1 * m + (1.0 - ADAM_B1) * g
    v = ADAM_B2 * v + (1.0 - ADAM_B2) * _jnp.square(g)
    m_hat = m / (1.0 - ADAM_B1 ** ADAM_STEP)
    v_hat = v / (1.0 - ADAM_B2 ** ADAM_STEP)
    delta = -ADAM_LR * (m_hat / (_jnp.sqrt(v_hat) + ADAM_EPS) + ADAM_WD * w)
    return delta, m, v


def reference(x, mix_norm_g, ffn_norm_g, final_norm_g, ev_w_in, ev_w_out, s5_lam_re, s5_lam_im, s5_log_dt, s5_b_re, s5_b_im, s5_c_re, s5_c_im, s5_d, s5_w_glu, s5_b_glu, gm_w_s, gm_b_s, gm_v_g, od_w_in, od_conv_w, od_conv_b, od_w_out, ffn_w_up, ffn_conv_w, ffn_conv_b, ffn_w_down, loss_target, m_mix_norm_g, m_ffn_norm_g, m_final_norm_g, m_ev_w_in, m_ev_w_out, m_s5_lam_re, m_s5_lam_im, m_s5_log_dt, m_s5_b_re, m_s5_b_im, m_s5_c_re, m_s5_c_im, m_s5_d, m_s5_w_glu, m_s5_b_glu, m_gm_w_s, m_gm_b_s, m_gm_v_g, m_od_w_in, m_od_conv_w, m_od_conv_b, m_od_w_out, m_ffn_w_up, m_ffn_conv_w, m_ffn_conv_b, m_ffn_w_down, v_mix_norm_g, v_ffn_norm_g, v_final_norm_g, v_ev_w_in, v_ev_w_out, v_s5_lam_re, v_s5_lam_im, v_s5_log_dt, v_s5_b_re, v_s5_b_im, v_s5_c_re, v_s5_c_im, v_s5_d, v_s5_w_glu, v_s5_b_glu, v_gm_w_s, v_gm_b_s, v_gm_v_g, v_od_w_in, v_od_conv_w, v_od_conv_b, v_od_w_out, v_ffn_w_up, v_ffn_conv_w, v_ffn_conv_b, v_ffn_w_down):
    given = dict(x=x, mix_norm_g=mix_norm_g, ffn_norm_g=ffn_norm_g, final_norm_g=final_norm_g, ev_w_in=ev_w_in, ev_w_out=ev_w_out, s5_lam_re=s5_lam_re, s5_lam_im=s5_lam_im, s5_log_dt=s5_log_dt, s5_b_re=s5_b_re, s5_b_im=s5_b_im, s5_c_re=s5_c_re, s5_c_im=s5_c_im, s5_d=s5_d, s5_w_glu=s5_w_glu, s5_b_glu=s5_b_glu, gm_w_s=gm_w_s, gm_b_s=gm_b_s, gm_v_g=gm_v_g, od_w_in=od_w_in, od_conv_w=od_conv_w, od_conv_b=od_conv_b, od_w_out=od_w_out, ffn_w_up=ffn_w_up, ffn_conv_w=ffn_conv_w, ffn_conv_b=ffn_conv_b, ffn_w_down=ffn_w_down, loss_target=loss_target, m_mix_norm_g=m_mix_norm_g, m_ffn_norm_g=m_ffn_norm_g, m_final_norm_g=m_final_norm_g, m_ev_w_in=m_ev_w_in, m_ev_w_out=m_ev_w_out, m_s5_lam_re=m_s5_lam_re, m_s5_lam_im=m_s5_lam_im, m_s5_log_dt=m_s5_log_dt, m_s5_b_re=m_s5_b_re, m_s5_b_im=m_s5_b_im, m_s5_c_re=m_s5_c_re, m_s5_c_im=m_s5_c_im, m_s5_d=m_s5_d, m_s5_w_glu=m_s5_w_glu, m_s5_b_glu=m_s5_b_glu, m_gm_w_s=m_gm_w_s, m_gm_b_s=m_gm_b_s, m_gm_v_g=m_gm_v_g, m_od_w_in=m_od_w_in, m_od_conv_w=m_od_conv_w, m_od_conv_b=m_od_conv_b, m_od_w_out=m_od_w_out, m_ffn_w_up=m_ffn_w_up, m_ffn_conv_w=m_ffn_conv_w, m_ffn_conv_b=m_ffn_conv_b, m_ffn_w_down=m_ffn_w_down, v_mix_norm_g=v_mix_norm_g, v_ffn_norm_g=v_ffn_norm_g, v_final_norm_g=v_final_norm_g, v_ev_w_in=v_ev_w_in, v_ev_w_out=v_ev_w_out, v_s5_lam_re=v_s5_lam_re, v_s5_lam_im=v_s5_lam_im, v_s5_log_dt=v_s5_log_dt, v_s5_b_re=v_s5_b_re, v_s5_b_im=v_s5_b_im, v_s5_c_re=v_s5_c_re, v_s5_c_im=v_s5_c_im, v_s5_d=v_s5_d, v_s5_w_glu=v_s5_w_glu, v_s5_b_glu=v_s5_b_glu, v_gm_w_s=v_gm_w_s, v_gm_b_s=v_gm_b_s, v_gm_v_g=v_gm_v_g, v_od_w_in=v_od_w_in, v_od_conv_w=v_od_conv_w, v_od_conv_b=v_od_conv_b, v_od_w_out=v_od_w_out, v_ffn_w_up=v_ffn_w_up, v_ffn_conv_w=v_ffn_conv_w, v_ffn_conv_b=v_ffn_conv_b, v_ffn_w_down=v_ffn_w_down)
    weights = {n: given[n] for n in TWIN_WEIGHTS}
    shared = {n: given[n] for n in SHARED_INPUTS}
    per_example = {n: given[n] for n in ['x']}
    grad_fn = _jax.value_and_grad(_loss, argnums=(0, 1))

    def one_microbatch(ex, loss_target):
        ex = dict(ex)
        diff = ex.pop(TWIN_DIFF_INPUT)
        return grad_fn(weights, diff, {**shared, **ex}, loss_target)

    if N_MICROBATCH == 1:
        loss, (grad_w, grad_x) = one_microbatch(per_example, given["loss_target"])
    else:
        def body(carry, xs):
            loss_sum, grad_sum = carry
            l_k, (gw_k, gx_k) = one_microbatch(xs[0], xs[1])
            with _jax.named_scope("update"):
                return (loss_sum + l_k, _jax.tree.map(_jnp.add, grad_sum, gw_k)), gx_k

        init = (_jnp.zeros((), _jnp.float32), _jax.tree.map(_jnp.zeros_like, weights))
        (loss, grad_w), grad_x = _jax.lax.scan(body, init, (per_example, given["loss_target"]))
    with _jax.named_scope("update"):
        delta_w, new_m, new_v = {}, {}, {}
        for n in TWIN_WEIGHTS:
            delta_w[n], new_m[n], new_v[n] = _adamw(weights[n], grad_w[n], given["m_" + n], given["v_" + n])
    return (loss, grad_x, *[grad_w[n] for n in TWIN_WEIGHTS], *[delta_w[n] for n in TWIN_WEIGHTS],
            *[new_m[n] for n in TWIN_WEIGHTS], *[new_v[n] for n in TWIN_WEIGHTS])
```

```python
import functools
import math

import jax
import jax.numpy as jnp
import numpy as np
from jax import lax
from jax.experimental import pallas as pl
from jax.experimental.pallas import tpu as pltpu

F32 = jnp.float32
BF = jnp.bfloat16

D = 1024
DFF = 2816
NDEV = 8
SSM_W = 256
SSM_G = 16
SSM_H = 16
SSM_P = 64
NST = SSM_G * SSM_P
GM_W = 768
GM_HEADS = 6
CHUNK = 128
EPS = 1e-6
LAM_MAX = -1e-4
FB = 256
VMEM_LIMIT = 48 * 2**20
PACK_COLS = 1024
MESH_T = pl.DeviceIdType.MESH

ADAM_LR = 0.001
ADAM_B1 = 0.9
ADAM_B2 = 0.999
ADAM_EPS = 1e-08
ADAM_WD = 0.01
ADAM_STEP = 10

WEIGHT_ORDER = ['mix_norm_g', 'ffn_norm_g', 'final_norm_g', 'ev_w_in', 'ev_w_out', 's5_lam_re', 's5_lam_im',
                's5_log_dt', 's5_b_re', 's5_b_im', 's5_c_re', 's5_c_im', 's5_d', 's5_w_glu', 's5_b_glu', 'gm_w_s',
                'gm_b_s', 'gm_v_g', 'od_w_in', 'od_conv_w', 'od_conv_b', 'od_w_out', 'ffn_w_up', 'ffn_conv_w',
                'ffn_conv_b', 'ffn_w_down']
SHARDED = {'ev_w_in': ((1, 1024, 1792), 2), 'ev_w_out': ((1, 1024, 1024), 1), 's5_w_glu': ((1, 256, 256), 1),
           'od_w_in': ((1, 1024, 3072), 2), 'od_conv_w': ((1, 3, 1024), 2), 'od_conv_b': ((1, 1024), 1),
           'od_w_out': ((1, 1024, 1024), 1), 'ffn_w_up': ((2, 1024, 5632), 2), 'ffn_conv_w': ((2, 3, 5632), 2),
           'ffn_w_down': ((2, 2816, 1024), 1)}
SHARDED_ORDER = [n for n in WEIGHT_ORDER if n in SHARDED]
REPL_ORDER = [n for n in WEIGHT_ORDER if n not in SHARDED]


def _cp(sem):
    return pltpu.CompilerParams(dimension_semantics=sem, vmem_limit_bytes=VMEM_LIMIT)


def _sigmoid(x):
    return 1.0 / (1.0 + jnp.exp(-x))


_GELU_K = math.sqrt(2.0 / math.pi)


def _gelu(x):
    return 0.5 * x * (1.0 + jnp.tanh(_GELU_K * (x + 0.044715 * x * x * x)))


def _gelu_grad(x):
    t = jnp.tanh(_GELU_K * (x + 0.044715 * x * x * x))
    return 0.5 * (1.0 + t) + 0.5 * x * (1.0 - t * t) * _GELU_K * (1.0 + 3.0 * 0.044715 * x * x)


def _colsum(x):
    return jnp.sum(x, axis=0, keepdims=True)


def _accumulate(ref, first, part):
    @pl.when(first)
    def _():
        ref[...] = part

    @pl.when(jnp.logical_not(first))
    def _():
        ref[...] += part


_DIMS = {'nn': (((1,), (0,)), ((), ())), 'nt': (((1,), (1,)), ((), ())), 'tn': (((0,), (0,)), ((), ()))}


def _matmul(a, b, mode, tm, tn, tk, name, resid=None, out_dtype=F32):
    if mode == 'tn':
        kdim, m = a.shape
    else:
        m, kdim = a.shape
    n = b.shape[0] if mode == 'nt' else b.shape[1]
    tm, tn, tk = min(tm, m), min(tn, n), min(tk, kdim)
    assert m % tm == 0 and n % tn == 0 and kdim % tk == 0, (name, m, n, kdim, tm, tn, tk)
    nk = kdim // tk
    dims = _DIMS[mode]
    a_spec = (pl.BlockSpec((tk, tm), lambda i, j, k: (k, i)) if mode == 'tn'
              else pl.BlockSpec((tm, tk), lambda i, j, k: (i, k)))
    b_spec = (pl.BlockSpec((tn, tk), lambda i, j, k: (j, k)) if mode == 'nt'
              else pl.BlockSpec((tk, tn), lambda i, j, k: (k, j)))
    o_spec = pl.BlockSpec((tm, tn), lambda i, j, k: (i, j))
    has_resid = resid is not None

    def body(*refs):
        if has_resid:
            a_ref, b_ref, r_ref, o_ref = refs[:4]
        else:
            a_ref, b_ref, o_ref = refs[:3]
            r_ref = None
        part = lax.dot_general(a_ref[...].astype(BF), b_ref[...].astype(BF), dims, preferred_element_type=F32)
        if nk == 1:
            if has_resid:
                part = part + r_ref[...]
            o_ref[...] = part.astype(out_dtype)
        else:
            acc = refs[-1]
            k = pl.program_id(2)

            @pl.when(k == 0)
            def _():
                acc[...] = part

            @pl.when(k > 0)
            def _():
                acc[...] += part

            @pl.when(k == nk - 1)
            def _():
                tot = acc[...]
                if has_resid:
                    tot = tot + r_ref[...]
                o_ref[...] = tot.astype(out_dtype)

    operands = [a, b] + ([resid] if has_resid else [])
    in_specs = [a_spec, b_spec] + ([o_spec] if has_resid else [])
    return pl.pallas_call(
        body, name=name, grid=(m // tm, n // tn, nk), in_specs=in_specs, out_specs=o_spec,
        out_shape=jax.ShapeDtypeStruct((m, n), out_dtype),
        scratch_shapes=[pltpu.VMEM((tm, tn), F32)] if nk > 1 else [],
        compiler_params=_cp(("parallel", "parallel", "arbitrary")))(*operands)


def _rmsnorm_fwd(x, g, name):
    n = x.shape[0]
    tm = min(512, n)

    def body(x_ref, g_ref, o_ref):
        xv = x_ref[...]
        r = lax.rsqrt(jnp.mean(xv * xv, axis=-1, keepdims=True) + EPS)
        o_ref[...] = (xv * r * g_ref[...]).astype(BF)

    return pl.pallas_call(
        body, name=name, grid=(n // tm,),
        in_specs=[pl.BlockSpec((tm, D), lambda i: (i, 0)), pl.BlockSpec((1, D), lambda i: (0, 0))],
        out_specs=pl.BlockSpec((tm, D), lambda i: (i, 0)),
        out_shape=jax.ShapeDtypeStruct((n, D), BF), compiler_params=_cp(("parallel",)))(x, g)


def _rmsnorm_bwd(x, g, dy, dres, name):
    n = x.shape[0]
    tm = min(512, n)

    def body(x_ref, g_ref, dy_ref, dr_ref, dx_ref, dg_ref):
        xv = x_ref[...]
        r = lax.rsqrt(jnp.mean(xv * xv, axis=-1, keepdims=True) + EPS)
        xh = xv * r
        dyv = dy_ref[...]
        dyg = dyv * g_ref[...]
        dx_ref[...] = dr_ref[...] + r * (dyg - xh * jnp.mean(dyg * xh, axis=-1, keepdims=True))
        _accumulate(dg_ref, pl.program_id(0) == 0, _colsum(dyv * xh))

    row = pl.BlockSpec((tm, D), lambda i: (i, 0))
    vec = pl.BlockSpec((1, D), lambda i: (0, 0))
    return pl.pallas_call(
        body, name=name, grid=(n // tm,), in_specs=[row, vec, row, row], out_specs=[row, vec],
        out_shape=[jax.ShapeDtypeStruct((n, D), F32), jax.ShapeDtypeStruct((1, D), F32)],
        compiler_params=_cp(("arbitrary",)))(x, g, dy, dres)


def _final_loss(h, g, tgt, name):
    n = h.shape[0]
    tm = min(512, n)

    def body(x_ref, g_ref, t_ref, loss_ref, dx_ref, dg_ref):
        first = pl.program_id(0) == 0
        xv = x_ref[...]
        gv = g_ref[...]
        r = lax.rsqrt(jnp.mean(xv * xv, axis=-1, keepdims=True) + EPS)
        xh = xv * r
        err = xh * gv - t_ref[...]
        part = 0.5 * jnp.sum(jnp.mean(err * err, axis=-1, keepdims=True), axis=0, keepdims=True)
        _accumulate(loss_ref, first, jnp.broadcast_to(part, (1, 128)))
        dyv = err * (1.0 / D)
        dyg = dyv * gv
        dx_ref[...] = r * (dyg - xh * jnp.mean(dyg * xh, axis=-1, keepdims=True))
        _accumulate(dg_ref, first, _colsum(dyv * xh))

    row = pl.BlockSpec((tm, D), lambda i: (i, 0))
    vec = pl.BlockSpec((1, D), lambda i: (0, 0))
    return pl.pallas_call(
        body, name=name, grid=(n // tm,), in_specs=[row, vec, row],
        out_specs=[pl.BlockSpec((1, 128), lambda i: (0, 0)), row, vec],
        out_shape=[jax.ShapeDtypeStruct((1, 128), F32), jax.ShapeDtypeStruct((n, D), F32),
                   jax.ShapeDtypeStruct((1, D), F32)],
        compiler_params=_cp(("arbitrary",)))(h, g, tgt)


def _prev_rows(x, halo_ref, lanes, scale, row):
    h7 = halo_ref[7:8, lanes] * scale
    h6 = halo_ref[6:7, lanes] * scale
    p1 = jnp.where(row == 0, h7, pltpu.roll(x, 1, 0))
    p2 = jnp.where(row == 0, h6, jnp.where(row == 1, h7, pltpu.roll(x, 2, 0)))
    return p1, p2


def _halo_maps(tm, n_rows):
    r8 = tm // 8
    last = n_rows // 8 - 1
    prev = lambda i: jnp.maximum(i * r8 - 1, 0)
    nxt = lambda i: jnp.minimum((i + 1) * r8, last)
    return prev, nxt


def _ffn_conv_fwd(up, cw, cb, seq, name):
    n = up.shape[0]
    tm = min(256, seq)
    prev, _ = _halo_maps(tm, n)

    def body(u_ref, h_ref, w_ref, b_ref, o_ref):
        i = pl.program_id(1)
        scale = jnp.where(lax.rem(i * tm, seq) == 0, 0.0, 1.0)
        x = u_ref[...]
        row = lax.broadcasted_iota(jnp.int32, x.shape, 0)
        p1, p2 = _prev_rows(x, h_ref, slice(None), scale, row)
        hc = b_ref[...] + w_ref[0:1, :] * p2 + w_ref[1:2, :] * p1 + w_ref[2:3, :] * x
        gate = hc[:, :FB]
        o_ref[...] = (gate * _sigmoid(gate) * hc[:, FB:]).astype(BF)

    return pl.pallas_call(
        body, name=name, grid=(DFF // FB, n // tm),
        in_specs=[pl.BlockSpec((tm, 2 * FB), lambda j, i: (i, j)),
                  pl.BlockSpec((8, 2 * FB), lambda j, i: (prev(i), j)),
                  pl.BlockSpec((3, 2 * FB), lambda j, i: (0, j)),
                  pl.BlockSpec((1, 2 * FB), lambda j, i: (0, j))],
        out_specs=pl.BlockSpec((tm, FB), lambda j, i: (i, j)),
        out_shape=jax.ShapeDtypeStruct((n, DFF), BF), compiler_params=_cp(("parallel", "parallel")))(up, up, cw, cb)


def _ffn_conv_bwd(up, dact, cw, cb, seq, name):
    n = up.shape[0]
    tm = min(256, seq)
    ext = tm + 16
    prev, nxt = _halo_maps(tm, n)

    def body(u_ref, up_ref, un_ref, da_ref, dn_ref, w_ref, b_ref, du_ref, dw_ref, db_ref, ux, dx):
        i = pl.program_id(1)
        sp = jnp.where(lax.rem(i * tm, seq) == 0, 0.0, 1.0)
        sn = jnp.where(lax.rem((i + 1) * tm, seq) == 0, 0.0, 1.0)
        ux[0:8, :] = up_ref[...] * sp
        ux[8:8 + tm, :] = u_ref[...]
        ux[8 + tm:, :] = un_ref[...]
        dx[0:8, :] = jnp.zeros((8, FB), F32)
        dx[8:8 + tm, :] = da_ref[...]
        dx[8 + tm:, :] = dn_ref[...] * sn
        x0 = ux[...]
        x1 = pltpu.roll(x0, 1, 0)
        x2 = pltpu.roll(x0, 2, 0)
        hc = b_ref[...] + w_ref[0:1, :] * x2 + w_ref[1:2, :] * x1 + w_ref[2:3, :] * x0
        gate, val = hc[:, :FB], hc[:, FB:]
        s = _sigmoid(gate)
        da = dx[...]
        dval = da * (gate * s)
        dgate = da * val * (s * (1.0 + gate * (1.0 - s)))
        first = i == 0
        main = slice(8, 8 + tm)
        for half, dh in ((slice(0, FB), dgate), (slice(FB, 2 * FB), dval)):
            dup = (w_ref[2:3, half] * dh + w_ref[1:2, half] * pltpu.roll(dh, ext - 1, 0)
                   + w_ref[0:1, half] * pltpu.roll(dh, ext - 2, 0))
            du_ref[:, half] = dup[main].astype(BF)
            dm = dh[main]
            _accumulate(dw_ref.at[0:1, half], first, _colsum(dm * x2[main, half]))
            _accumulate(dw_ref.at[1:2, half], first, _colsum(dm * x1[main, half]))
            _accumulate(dw_ref.at[2:3, half], first, _colsum(dm * x0[main, half]))
            _accumulate(db_ref.at[:, half], first, _colsum(dm))

    return pl.pallas_call(
        body, name=name, grid=(DFF // FB, n // tm),
        in_specs=[pl.BlockSpec((tm, 2 * FB), lambda j, i: (i, j)),
                  pl.BlockSpec((8, 2 * FB), lambda j, i: (prev(i), j)),
                  pl.BlockSpec((8, 2 * FB), lambda j, i: (nxt(i), j)),
                  pl.BlockSpec((tm, FB), lambda j, i: (i, j)),
                  pl.BlockSpec((8, FB), lambda j, i: (nxt(i), j)),
                  pl.BlockSpec((3, 2 * FB), lambda j, i: (0, j)),
                  pl.BlockSpec((1, 2 * FB), lambda j, i: (0, j))],
        out_specs=[pl.BlockSpec((tm, 2 * FB), lambda j, i: (i, j)),
                   pl.BlockSpec((3, 2 * FB), lambda j, i: (0, j)),
                   pl.BlockSpec((1, 2 * FB), lambda j, i: (0, j))],
        out_shape=[jax.ShapeDtypeStruct((n, 2 * DFF), BF), jax.ShapeDtypeStruct((3, 2 * DFF), F32),
                   jax.ShapeDtypeStruct((1, 2 * DFF), F32)],
        scratch_shapes=[pltpu.VMEM((ext, 2 * FB), F32), pltpu.VMEM((ext, FB), F32)],
        compiler_params=_cp(("parallel", "arbitrary")))(up, up, up, dact, dact, cw, cb)


def _shortconv_fwd(p, cw, cb, seq, name):
    n = p.shape[0]
    tm = min(256, seq)
    prev, _ = _halo_maps(tm, n)

    def body(p_ref, h_ref, w_ref, b_ref, o_ref):
        i = pl.program_id(1)
        scale = jnp.where(lax.rem(i * tm, seq) == 0, 0.0, 1.0)
        q = p_ref[:, FB:2 * FB] * p_ref[:, 2 * FB:]
        row = lax.broadcasted_iota(jnp.int32, q.shape, 0)
        h7 = h_ref[7:8, FB:2 * FB] * h_ref[7:8, 2 * FB:] * scale
        h6 = h_ref[6:7, FB:2 * FB] * h_ref[6:7, 2 * FB:] * scale
        p1 = jnp.where(row == 0, h7, pltpu.roll(q, 1, 0))
        p2 = jnp.where(row == 0, h6, jnp.where(row == 1, h7, pltpu.roll(q, 2, 0)))
        conv = b_ref[...] + w_ref[0:1, :] * p2 + w_ref[1:2, :] * p1 + w_ref[2:3, :] * q
        o_ref[...] = (p_ref[:, :FB] * conv).astype(BF)

    return pl.pallas_call(
        body, name=name, grid=(D // FB, n // tm),
        in_specs=[pl.BlockSpec((tm, 3 * FB), lambda j, i: (i, j)),
                  pl.BlockSpec((8, 3 * FB), lambda j, i: (prev(i), j)),
                  pl.BlockSpec((3, FB), lambda j, i: (0, j)),
                  pl.BlockSpec((1, FB), lambda j, i: (0, j))],
        out_specs=pl.BlockSpec((tm, FB), lambda j, i: (i, j)),
        out_shape=jax.ShapeDtypeStruct((n, D), BF), compiler_params=_cp(("parallel", "parallel")))(p, p, cw, cb)


def _shortconv_bwd(p, dmix, cw, cb, seq, name):
    n = p.shape[0]
    tm = min(256, seq)
    ext = tm + 16
    prev, nxt = _halo_maps(tm, n)

    def body(p_ref, pp_ref, pn_ref, dm_ref, dn_ref, w_ref, b_ref, dp_ref, dw_ref, db_ref, qx, cx):
        i = pl.program_id(1)
        sp = jnp.where(lax.rem(i * tm, seq) == 0, 0.0, 1.0)
        sn = jnp.where(lax.rem((i + 1) * tm, seq) == 0, 0.0, 1.0)
        bg, cg, hx = p_ref[:, :FB], p_ref[:, FB:2 * FB], p_ref[:, 2 * FB:]
        dm = dm_ref[...]
        qx[0:8, :] = pp_ref[:, FB:2 * FB] * pp_ref[:, 2 * FB:] * sp
        qx[8:8 + tm, :] = cg * hx
        qx[8 + tm:, :] = jnp.zeros((8, FB), F32)
        cx[0:8, :] = jnp.zeros((8, FB), F32)
        cx[8:8 + tm, :] = dm * bg
        cx[8 + tm:, :] = dn_ref[...] * pn_ref[:, :FB] * sn
        q0 = qx[...]
        q1 = pltpu.roll(q0, 1, 0)
        q2 = pltpu.roll(q0, 2, 0)
        main = slice(8, 8 + tm)
        conv = b_ref[...] + w_ref[0:1, :] * q2[main] + w_ref[1:2, :] * q1[main] + w_ref[2:3, :] * q0[main]
        dc = cx[...]
        dq = (w_ref[2:3, :] * dc + w_ref[1:2, :] * pltpu.roll(dc, ext - 1, 0)
              + w_ref[0:1, :] * pltpu.roll(dc, ext - 2, 0))[main]
        dp_ref[:, :FB] = (dm * conv).astype(BF)
        dp_ref[:, FB:2 * FB] = (dq * hx).astype(BF)
        dp_ref[:, 2 * FB:] = (dq * cg).astype(BF)
        first = i == 0
        dcm = dc[main]
        _accumulate(dw_ref.at[0:1, :], first, _colsum(dcm * q2[main]))
        _accumulate(dw_ref.at[1:2, :], first, _colsum(dcm * q1[main]))
        _accumulate(dw_ref.at[2:3, :], first, _colsum(dcm * q0[main]))
        _accumulate(db_ref, first, _colsum(dcm))

    return pl.pallas_call(
        body, name=name, grid=(D // FB, n // tm),
        in_specs=[pl.BlockSpec((tm, 3 * FB), lambda j, i: (i, j)),
                  pl.BlockSpec((8, 3 * FB), lambda j, i: (prev(i), j)),
                  pl.BlockSpec((8, 3 * FB), lambda j, i: (nxt(i), j)),
                  pl.BlockSpec((tm, FB), lambda j, i: (i, j)),
                  pl.BlockSpec((8, FB), lambda j, i: (nxt(i), j)),
                  pl.BlockSpec((3, FB), lambda j, i: (0, j)),
                  pl.BlockSpec((1, FB), lambda j, i: (0, j))],
        out_specs=[pl.BlockSpec((tm, 3 * FB), lambda j, i: (i, j)),
                   pl.BlockSpec((3, FB), lambda j, i: (0, j)),
                   pl.BlockSpec((1, FB), lambda j, i: (0, j))],
        out_shape=[jax.ShapeDtypeStruct((n, 3 * D), BF), jax.ShapeDtypeStruct((3, D), F32),
                   jax.ShapeDtypeStruct((1, D), F32)],
        scratch_shapes=[pltpu.VMEM((ext, FB), F32), pltpu.VMEM((ext, FB), F32)],
        compiler_params=_cp(("parallel", "arbitrary")))(p, p, p, dmix, dmix, cw, cb)


def _gmlp_fwd(uv, wm, bst, gv, seq, name):
    n = uv.shape[0]
    tm = min(256, seq)

    def body(x_ref, w_ref, b_ref, g_ref, o_ref):
        ge_v = _gelu(x_ref[:, GM_W:])
        r = lax.rsqrt(jnp.mean(ge_v * ge_v, axis=-1, keepdims=True) + EPS)
        vn = (ge_v * r * g_ref[...]).astype(BF)
        for c in range(tm // CHUNK):
            rows = slice(c * CHUNK, (c + 1) * CHUNK)
            for h in range(GM_HEADS):
                cols = slice(h * CHUNK, (h + 1) * CHUNK)
                gate = jnp.dot(w_ref[h], vn[rows, cols], preferred_element_type=F32) + b_ref[:, h:h + 1]
                o_ref[rows, cols] = (_gelu(x_ref[rows, cols]) * gate).astype(BF)

    return pl.pallas_call(
        body, name=name, grid=(n // tm,),
        in_specs=[pl.BlockSpec((tm, 2 * GM_W), lambda i: (i, 0)),
                  pl.BlockSpec((GM_HEADS, CHUNK, CHUNK), lambda i: (0, 0, 0)),
                  pl.BlockSpec((CHUNK, GM_HEADS), lambda i: (0, 0)),
                  pl.BlockSpec((1, GM_W), lambda i: (0, 0))],
        out_specs=pl.BlockSpec((tm, GM_W), lambda i: (i, 0)),
        out_shape=jax.ShapeDtypeStruct((n, GM_W), BF), compiler_params=_cp(("parallel",)))(uv, wm, bst, gv)


def _gmlp_bwd(uv, dout, wm, wmt, bst, gv, seq, name):
    n = uv.shape[0]
    tm = min(256, seq)

    def body(x_ref, do_ref, w_ref, wt_ref, b_ref, g_ref, dx_ref, dw_ref, db_ref, dg_ref, dvn_scr):
        first = pl.program_id(0) == 0
        ge_v = _gelu(x_ref[:, GM_W:])
        r = lax.rsqrt(jnp.mean(ge_v * ge_v, axis=-1, keepdims=True) + EPS)
        vh = ge_v * r
        vn = (vh * g_ref[...]).astype(BF)
        tril = (lax.broadcasted_iota(jnp.int32, (CHUNK, CHUNK), 0)
                >= lax.broadcasted_iota(jnp.int32, (CHUNK, CHUNK), 1))
        for h in range(GM_HEADS):
            cols = slice(h * CHUNK, (h + 1) * CHUNK)
            dw = jnp.zeros((CHUNK, CHUNK), F32)
            dbs = jnp.zeros((CHUNK, 1), F32)
            for c in range(tm // CHUNK):
                rows = slice(c * CHUNK, (c + 1) * CHUNK)
                blk = vn[rows, cols]
                gate = jnp.dot(w_ref[h], blk, preferred_element_type=F32) + b_ref[:, h:h + 1]
                xu = x_ref[rows, cols]
                do = do_ref[rows, cols]
                dx_ref[rows, cols] = (do * gate * _gelu_grad(xu)).astype(BF)
                dgate = do * _gelu(xu)
                dgb = dgate.astype(BF)
                dw = dw + lax.dot_general(dgb, blk, _DIMS['nt'], preferred_element_type=F32)
                dbs = dbs + jnp.sum(dgate, axis=1, keepdims=True)
                dvn_scr[rows, cols] = jnp.dot(wt_ref[h], dgb, preferred_element_type=F32)
            _accumulate(dw_ref.at[h], first, jnp.where(tril, dw, 0.0))
            _accumulate(db_ref.at[h], first, dbs)
        dvn = dvn_scr[...]
        _accumulate(dg_ref, first, _colsum(dvn * vh))
        dvh = dvn * g_ref[...]
        dv = r * (dvh - vh * jnp.mean(dvh * vh, axis=-1, keepdims=True))
        dx_ref[:, GM_W:] = (dv * _gelu_grad(x_ref[:, GM_W:])).astype(BF)

    full3 = pl.BlockSpec((GM_HEADS, CHUNK, CHUNK), lambda i: (0, 0, 0))
    return pl.pallas_call(
        body, name=name, grid=(n // tm,),
        in_specs=[pl.BlockSpec((tm, 2 * GM_W), lambda i: (i, 0)), pl.BlockSpec((tm, GM_W), lambda i: (i, 0)),
                  full3, full3, pl.BlockSpec((CHUNK, GM_HEADS), lambda i: (0, 0)),
                  pl.BlockSpec((1, GM_W), lambda i: (0, 0))],
        out_specs=[pl.BlockSpec((tm, 2 * GM_W), lambda i: (i, 0)), full3,
                   pl.BlockSpec((GM_HEADS, CHUNK, 1), lambda i: (0, 0, 0)),
                   pl.BlockSpec((1, GM_W), lambda i: (0, 0))],
        out_shape=[jax.ShapeDtypeStruct((n, 2 * GM_W), BF), jax.ShapeDtypeStruct((GM_HEADS, CHUNK, CHUNK), F32),
                   jax.ShapeDtypeStruct((GM_HEADS, CHUNK, 1), F32), jax.ShapeDtypeStruct((1, GM_W), F32)],
        scratch_shapes=[pltpu.VMEM((tm, GM_W), F32)],
        compiler_params=_cp(("arbitrary",)))(uv, dout, wm, wmt, bst, gv)


def _s5_disc(lam_re, lam_im, log_dt, b_re, b_im):
    lr = jnp.minimum(lam_re, LAM_MAX)
    li = lam_im
    dt = jnp.exp(log_dt)
    mag = jnp.exp(lr * dt)
    ab_re = mag * jnp.cos(li * dt)
    ab_im = mag * jnp.sin(li * dt)
    den = lr * lr + li * li
    nr = ab_re - 1.0
    ni = ab_im
    z_re = (nr * lr + ni * li) / den
    z_im = (ni * lr - nr * li) / den
    return ab_re, ab_im, z_re * b_re - z_im * b_im, z_re * b_im + z_im * b_re


def _s5_disc_fwd(args, name):
    shp = jax.ShapeDtypeStruct(args[0].shape, F32)

    def body(*refs):
        outs = _s5_disc(*[r[...] for r in refs[:5]])
        for o_ref, o in zip(refs[5:], outs):
            o_ref[...] = o

    return pl.pallas_call(body, name=name, out_shape=[shp] * 4)(*args)


def _s5_disc_bwd(args, cts, name):
    shp = jax.ShapeDtypeStruct(args[0].shape, F32)

    def body(*refs):
        _, vjp = jax.vjp(_s5_disc, *[r[...] for r in refs[:5]])
        grads = vjp(tuple(r[...] for r in refs[5:9]))
        for o_ref, o in zip(refs[9:], grads):
            o_ref[...] = o

    return pl.pallas_call(body, name=name, out_shape=[shp] * 5)(*args, *cts)


def _cmul(a, b):
    return a[0] * b[0] - a[1] * b[1], a[0] * b[1] + a[1] * b[0]


def _scan_tables(ar, ai, reverse):
    if reverse:
        ai = -ai
    a1 = (ar, ai)
    a2 = _cmul(a1, a1)
    a3 = _cmul(a2, a1)
    a4 = _cmul(a2, a2)
    powers = [a1, a2, a3, a4, _cmul(a4, a1), _cmul(a4, a2), _cmul(a4, a3), _cmul(a4, a4)]
    row = lax.broadcasted_iota(jnp.int32, (8, NST), 0)
    zero = jnp.zeros((8, NST), F32)
    pr, pi = zero, zero
    for r in range(8):
        pw = powers[7 - r] if reverse else powers[r]
        pr = jnp.where(row == r, pw[0], pr)
        pi = jnp.where(row == r, pw[1], pi)
    levels = []
    for d, pw in ((1, a1), (2, a2), (4, a4)):
        ok = (row <= 7 - d) if reverse else (row >= d)
        levels.append((d, jnp.where(ok, pw[0], zero), jnp.where(ok, pw[1], zero)))
    return (pr, pi), levels


def _scan_block(src, dst, car, tables, n_tiles, reverse):
    (pr, pi), levels = tables
    row = lax.broadcasted_iota(jnp.int32, (8, NST), 0)
    out_row = 0 if reverse else 7

    def step(t, carry):
        cr, ci = carry
        tile = (n_tiles - 1 - t) if reverse else t
        rows = pl.ds(pl.multiple_of(tile * 8, 8), 8)
        xr = src[rows, 0:NST]
        xi = src[rows, NST:2 * NST]
        for d, dr, di in levels:
            shift = 8 - d if reverse else d
            rr = pltpu.roll(xr, shift, 0)
            ri = pltpu.roll(xi, shift, 0)
            xr, xi = xr + dr * rr - di * ri, xi + dr * ri + di * rr
        hr = xr + pr * cr - pi * ci
        hi = xi + pr * ci + pi * cr
        dst[rows, 0:NST] = hr
        dst[rows, NST:2 * NST] = hi
        return (_colsum(jnp.where(row == out_row, hr, 0.0)), _colsum(jnp.where(row == out_row, hi, 0.0)))

    cr, ci = lax.fori_loop(0, n_tiles, step, (car[0:1, 0:NST], car[0:1, NST:2 * NST]))
    car[0:1, 0:NST] = cr
    car[0:1, NST:2 * NST] = ci


def _s5_fwd(u, ab, bbt, cmat, dvec, wglu, bglu, seq, name):
    n = u.shape[0]
    tm = min(256, seq)

    def body(u_ref, ab_ref, bb_ref, c_ref, d_ref, w_ref, b_ref, h_ref, o_ref, xs, car):
        i = pl.program_id(0)

        @pl.when(lax.rem(i * tm, seq) == 0)
        def _():
            car[...] = jnp.zeros(car.shape, F32)

        uv = u_ref[...]
        xs[...] = jnp.dot(uv.astype(BF), bb_ref[...], preferred_element_type=F32)
        tables = _scan_tables(ab_ref[0:1, 0:NST], ab_ref[0:1, NST:2 * NST], False)
        _scan_block(xs, h_ref, car, tables, tm // 8, False)
        y = jnp.dot(h_ref[...].astype(BF), c_ref[...], preferred_element_type=F32) + d_ref[...] * uv
        g1 = _gelu(y)
        z = jnp.dot(g1.astype(BF), w_ref[...], preferred_element_type=F32) + b_ref[...]
        o_ref[...] = (g1 * _sigmoid(z)).astype(BF)

    const = lambda shape: pl.BlockSpec(shape, lambda i: (0, 0))
    return pl.pallas_call(
        body, name=name, grid=(n // tm,),
        in_specs=[pl.BlockSpec((tm, SSM_W), lambda i: (i, 0)), const((1, 2 * NST)), const((SSM_W, 2 * NST)),
                  const((2 * NST, SSM_W)), const((1, SSM_W)), const((SSM_W, SSM_W)), const((1, SSM_W))],
        out_specs=[pl.BlockSpec((tm, 2 * NST), lambda i: (i, 0)), pl.BlockSpec((tm, SSM_W), lambda i: (i, 0))],
        out_shape=[jax.ShapeDtypeStruct((n, 2 * NST), F32), jax.ShapeDtypeStruct((n, SSM_W), BF)],
        scratch_shapes=[pltpu.VMEM((tm, 2 * NST), F32), pltpu.VMEM((8, 2 * NST), F32)],
        compiler_params=_cp(("arbitrary",)))(u, ab, bbt, cmat, dvec, wglu, bglu)


def _s5_bwd(da, u, hst, ab, bbt, cmat, dvec, wglu, bglu, seq, name):
    n = u.shape[0]
    tm = min(256, seq)
    nb = n // tm
    blk = lambda r: nb - 1 - r
    prev, _ = _halo_maps(tm, n)

    def body(da_ref, u_ref, h_ref, hp_ref, ab_ref, bb_ref, c_ref, d_ref, w_ref, b_ref,
             du_ref, dw_ref, dbg_ref, dd_ref, dc_ref, dbb_ref, dab_ref, gs, car):
        r = pl.program_id(0)
        i = blk(r)
        first = r == 0

        @pl.when(lax.rem((i + 1) * tm, seq) == 0)
        def _():
            car[...] = jnp.zeros(car.shape, F32)

        uv = u_ref[...]
        dav = da_ref[...]
        hb = h_ref[...]
        hb16 = hb.astype(BF)
        dvv = d_ref[...]
        y = jnp.dot(hb16, c_ref[...], preferred_element_type=F32) + dvv * uv
        g1 = _gelu(y)
        g16 = g1.astype(BF)
        s = _sigmoid(jnp.dot(g16, w_ref[...], preferred_element_type=F32) + b_ref[...])
        dz = dav * g1 * s * (1.0 - s)
        dz16 = dz.astype(BF)
        dg1 = dav * s + lax.dot_general(dz16, w_ref[...], _DIMS['nt'], preferred_element_type=F32)
        _accumulate(dw_ref, first, lax.dot_general(g16, dz16, _DIMS['tn'], preferred_element_type=F32))
        _accumulate(dbg_ref, first, _colsum(dz))
        dy = dg1 * _gelu_grad(y)
        dy16 = dy.astype(BF)
        _accumulate(dd_ref, first, _colsum(dy * uv))
        _accumulate(dc_ref, first, lax.dot_general(hb16, dy16, _DIMS['tn'], preferred_element_type=F32))
        gs[...] = lax.dot_general(dy16, c_ref[...], _DIMS['nt'], preferred_element_type=F32)
        tables = _scan_tables(ab_ref[0:1, 0:NST], ab_ref[0:1, NST:2 * NST], True)
        _scan_block(gs, gs, car, tables, tm // 8, True)
        g = gs[...]
        g16b = g.astype(BF)
        sp = jnp.where(lax.rem(i * tm, seq) == 0, 0.0, 1.0)
        row = lax.broadcasted_iota(jnp.int32, hb.shape, 0)
        hprev = jnp.where(row == 0, hp_ref[7:8, :] * sp, pltpu.roll(hb, 1, 0))
        gr, gi = g[:, :NST], g[:, NST:]
        hr, hi = hprev[:, :NST], hprev[:, NST:]
        _accumulate(dab_ref.at[:, 0:NST], first, _colsum(gr * hr + gi * hi))
        _accumulate(dab_ref.at[:, NST:2 * NST], first, _colsum(gi * hr - gr * hi))
        _accumulate(dbb_ref, first, lax.dot_general(uv.astype(BF), g16b, _DIMS['tn'], preferred_element_type=F32))
        du = dy * dvv + lax.dot_general(g16b, bb_ref[...], _DIMS['nt'], preferred_element_type=F32)
        du_ref[...] = du.astype(BF)

    const = lambda shape: pl.BlockSpec(shape, lambda r: (0, 0))
    rowspec = lambda w: pl.BlockSpec((tm, w), lambda r: (blk(r), 0))
    return pl.pallas_call(
        body, name=name, grid=(nb,),
        in_specs=[rowspec(SSM_W), rowspec(SSM_W), rowspec(2 * NST),
                  pl.BlockSpec((8, 2 * NST), lambda r: (prev(blk(r)), 0)),
                  const((1, 2 * NST)), const((SSM_W, 2 * NST)), const((2 * NST, SSM_W)), const((1, SSM_W)),
                  const((SSM_W, SSM_W)), const((1, SSM_W))],
        out_specs=[rowspec(SSM_W), const((SSM_W, SSM_W)), const((1, SSM_W)), const((1, SSM_W)),
                   const((2 * NST, SSM_W)), const((SSM_W, 2 * NST)), const((1, 2 * NST))],
        out_shape=[jax.ShapeDtypeStruct((n, SSM_W), BF), jax.ShapeDtypeStruct((SSM_W, SSM_W), F32),
                   jax.ShapeDtypeStruct((1, SSM_W), F32), jax.ShapeDtypeStruct((1, SSM_W), F32),
                   jax.ShapeDtypeStruct((2 * NST, SSM_W), F32), jax.ShapeDtypeStruct((SSM_W, 2 * NST), F32),
                   jax.ShapeDtypeStruct((1, 2 * NST), F32)],
        scratch_shapes=[pltpu.VMEM((tm, 2 * NST), F32), pltpu.VMEM((8, 2 * NST), F32)],
        compiler_params=_cp(("arbitrary",)))(da, u, hst, hst, ab, bbt, cmat, dvec, wglu, bglu)


def _s5_rows(lam_re, lam_im, log_dt, b_re, b_im):
    rep = lambda a: jnp.broadcast_to(a[:, None, :], (SSM_G, SSM_H, SSM_P)).reshape(SSM_W, SSM_P)
    dt = jnp.broadcast_to(log_dt[:, None, None], (SSM_G, SSM_H, SSM_P)).reshape(SSM_W, SSM_P)
    tr = lambda b: b.transpose(0, 2, 1).reshape(SSM_W, SSM_P)
    return rep(lam_re), rep(lam_im), dt, tr(b_re), tr(b_im)


def _block_diag(rows_gp, inner):
    eye = jnp.eye(SSM_G, dtype=rows_gp.dtype)
    return (rows_gp[:, :, None, :] * eye[:, None, :, None]).reshape(SSM_G * inner, SSM_G * SSM_P)


def _diag_blocks(mat, inner):
    m4 = mat.reshape(SSM_G, inner, SSM_G, SSM_P)
    return jnp.stack([m4[g, :, g, :] for g in range(SSM_G)])


def _interleave(w, parts):
    lead = w.shape[:-1]
    nb = w.shape[-1] // (parts * FB)
    return jnp.swapaxes(w.reshape(lead + (parts, nb, FB)), -3, -2).reshape(w.shape)


def _deinterleave(w, parts):
    lead = w.shape[:-1]
    nb = w.shape[-1] // (parts * FB)
    return jnp.swapaxes(w.reshape(lead + (nb, parts, FB)), -3, -2).reshape(w.shape)


def _ffn_fwd(h, g, w_up, cw, cb, w_down, seq, tag):
    f = _rmsnorm_fwd(h, g, f"{tag}_norm")
    up = _matmul(f, w_up, 'nn', 1024, 512, D, f"{tag}_up")
    act = _ffn_conv_fwd(up, cw, cb, seq, f"{tag}_conv")
    out = _matmul(act, w_down, 'nn', 1024, 512, DFF, f"{tag}_down", resid=h)
    return out, (f, up, act)


def _ffn_bwd(dh, h, g, w_up, cw, cb, w_down, saved, seq, tag):
    f, up, act = saved
    dact = _matmul(dh, w_down, 'nt', 1024, 256, D, f"{tag}_ddown_x")
    dw_down = _matmul(act, dh, 'tn', DFF // 2, 512, 1024, f"{tag}_ddown_w")
    dup, dcw, dcb = _ffn_conv_bwd(up, dact, cw, cb, seq, f"{tag}_dconv")
    df = _matmul(dup, w_up, 'nt', 1024, D, 512, f"{tag}_dup_x")
    dw_up = _matmul(f, dup, 'tn', D, 512, 1024, f"{tag}_dup_w")
    dh_in, dg = _rmsnorm_bwd(h, g, df, dh, f"{tag}_dnorm")
    return dh_in, dict(g=dg, w_up=dw_up, cw=dcw, cb=dcb, w_down=dw_down)


def _local_step(x, tgt, w, seq):
    bf = lambda a: a.astype(BF)
    row = lambda a: a.reshape(1, -1).astype(F32)
    w_ev = bf(w['ev_w_in'][0])
    w_ev_s5, w_ev_gm = w_ev[:, :SSM_W], w_ev[:, SSM_W:]
    w_evo = bf(w['ev_w_out'][0])
    w_od = bf(_interleave(w['od_w_in'][0], 3))
    w_odo = bf(w['od_w_out'][0])
    w_up = [bf(_interleave(w['ffn_w_up'][l], 2)) for l in range(2)]
    f_cw = [_interleave(w['ffn_conv_w'][l], 2) for l in range(2)]
    f_cb = [_interleave(row(w['ffn_conv_b'][l]), 2) for l in range(2)]
    w_dn = [bf(w['ffn_w_down'][l]) for l in range(2)]
    od_cw, od_cb = w['od_conv_w'][0], row(w['od_conv_b'][0])
    tril = jnp.tril(jnp.ones((CHUNK, CHUNK), dtype=bool))
    gm_w = jnp.where(tril, w['gm_w_s'][0], 0.0)
    gm_wm, gm_wmt = bf(gm_w), bf(jnp.swapaxes(gm_w, 1, 2))
    gm_bt = w['gm_b_s'][0].T
    gm_gv = row(w['gm_v_g'][0])
    s5_in = _s5_rows(w['s5_lam_re'][0], w['s5_lam_im'][0], w['s5_log_dt'][0], w['s5_b_re'][0], w['s5_b_im'][0])
    ab_re, ab_im, bb_re, bb_im = _s5_disc_fwd(s5_in, "s5_disc")
    first_h = lambda a: a.reshape(SSM_G, SSM_H, SSM_P)[:, 0, :].reshape(1, NST)
    s5_ab = jnp.concatenate([first_h(ab_re), first_h(ab_im)], axis=1)
    to_gp = lambda a: a.reshape(SSM_G, SSM_H, SSM_P)
    s5_bbt = bf(jnp.concatenate([_block_diag(to_gp(bb_re), SSM_H), _block_diag(to_gp(bb_im), SSM_H)], axis=1))
    s5_cmat = bf(jnp.concatenate([_block_diag(w['s5_c_re'][0], SSM_H).T, -_block_diag(w['s5_c_im'][0], SSM_H).T],
                                 axis=0))
    s5_d, s5_bg, s5_wg = row(w['s5_d'][0]), row(w['s5_b_glu'][0]), bf(w['s5_w_glu'][0])
    g_mix = [row(w['mix_norm_g'][l]) for l in range(2)]
    g_ffn = [row(w['ffn_norm_g'][l]) for l in range(2)]
    g_fin = row(w['final_norm_g'])

    h0 = x
    y0 = _rmsnorm_fwd(h0, g_mix[0], "ev_norm")
    p_s5 = _matmul(y0, w_ev_s5, 'nn', 1024, 256, D, "ev_in_s5")
    p_gm = _matmul(y0, w_ev_gm, 'nn', 1024, 512, D, "ev_in_gm")
    hst, a_out = _s5_fwd(p_s5, s5_ab, s5_bbt, s5_cmat, s5_d, s5_wg, s5_bg, seq, "s5_fwd")
    b_out = _gmlp_fwd(p_gm, gm_wm, gm_bt, gm_gv, seq, "gmlp_fwd")
    mixcat = jnp.concatenate([a_out, b_out], axis=1)
    h1 = _matmul(mixcat, w_evo, 'nn', 1024, 512, D, "ev_out", resid=h0)
    h2, ffn0 = _ffn_fwd(h1, g_ffn[0], w_up[0], f_cw[0], f_cb[0], w_dn[0], seq, "ffn0")
    y1 = _rmsnorm_fwd(h2, g_mix[1], "od_norm")
    p_od = _matmul(y1, w_od, 'nn', 1024, 512, D, "od_in")
    mixin = _shortconv_fwd(p_od, od_cw, od_cb, seq, "od_conv")
    h3 = _matmul(mixin, w_odo, 'nn', 1024, 512, D, "od_out", resid=h2)
    h4, ffn1 = _ffn_fwd(h3, g_ffn[1], w_up[1], f_cw[1], f_cb[1], w_dn[1], seq, "ffn1")
    loss, dh4, dg_fin = _final_loss(h4, g_fin, tgt, "final_loss")

    dh3, gf1 = _ffn_bwd(dh4, h3, g_ffn[1], w_up[1], f_cw[1], f_cb[1], w_dn[1], ffn1, seq, "ffn1")
    dmixin = _matmul(dh3, w_odo, 'nt', 1024, 512, D, "od_dout_x")
    dw_odo = _matmul(mixin, dh3, 'tn', D, 512, 1024, "od_dout_w")
    dp_od, d_od_cw, d_od_cb = _shortconv_bwd(p_od, dmixin, od_cw, od_cb, seq, "od_dconv")
    dy1 = _matmul(dp_od, w_od, 'nt', 1024, D, 512, "od_din_x")
    dw_od = _matmul(y1, dp_od, 'tn', D, 512, 1024, "od_din_w")
    dh2, dg_mix1 = _rmsnorm_bwd(h2, g_mix[1], dy1, dh3, "od_dnorm")
    dh1, gf0 = _ffn_bwd(dh2, h1, g_ffn[0], w_up[0], f_cw[0], f_cb[0], w_dn[0], ffn0, seq, "ffn0")
    dmix_a = _matmul(dh1, w_evo[:SSM_W], 'nt', 1024, SSM_W, D, "ev_dout_xa")
    dmix_b = _matmul(dh1, w_evo[SSM_W:], 'nt', 1024, GM_W, D, "ev_dout_xb")
    dw_evo = _matmul(mixcat, dh1, 'tn', D, 512, 1024, "ev_dout_w")
    dp_s5, d_wg, d_bg, d_d, d_cmat, d_bbt, d_ab = _s5_bwd(dmix_a, p_s5, hst, s5_ab, s5_bbt, s5_cmat, s5_d, s5_wg,
                                                           s5_bg, seq, "s5_bwd")
    dp_gm, d_gmw, d_gmb, d_gmg = _gmlp_bwd(p_gm, dmix_b, gm_wm, gm_wmt, gm_bt, gm_gv, seq, "gmlp_bwd")
    dy0 = _matmul(dp_gm, w_ev_gm, 'nt', 1024, D, 512, "ev_din_xb")
    dy0 = _matmul(dp_s5, w_ev_s5, 'nt', 1024, D, SSM_W, "ev_din_xa", resid=dy0)
    dw_ev = jnp.concatenate([_matmul(y0, dp_s5, 'tn', D, SSM_W, 1024, "ev_din_wa"),
                             _matmul(y0, dp_gm, 'tn', D, 512, 1024, "ev_din_wb")], axis=1)
    grad_x, dg_mix0 = _rmsnorm_bwd(h0, g_mix[0], dy0, dh1, "ev_dnorm")

    put_h0 = lambda a: jnp.zeros((SSM_G, SSM_H, SSM_P), F32).at[:, 0, :].set(a.reshape(SSM_G, SSM_P)).reshape(
        SSM_W, SSM_P)
    ct = (put_h0(d_ab[:, :NST]), put_h0(d_ab[:, NST:]),
          _diag_blocks(d_bbt[:, :NST], SSM_H).reshape(SSM_W, SSM_P),
          _diag_blocks(d_bbt[:, NST:], SSM_H).reshape(SSM_W, SSM_P))
    d_lre, d_lim, d_ldt, d_bre, d_bim = _s5_disc_bwd(s5_in, ct, "s5_ddisc")
    over_h = lambda a: a.reshape(SSM_G, SSM_H, SSM_P).sum(axis=1)
    un_tr = lambda a: a.reshape(SSM_G, SSM_H, SSM_P).transpose(0, 2, 1)
    d_cre = _diag_blocks(d_cmat[:NST].T, SSM_H)
    d_cim = -_diag_blocks(d_cmat[NST:].T, SSM_H)

    grads = {
        'mix_norm_g': jnp.concatenate([dg_mix0, dg_mix1], axis=0),
        'ffn_norm_g': jnp.concatenate([gf0['g'], gf1['g']], axis=0),
        'final_norm_g': dg_fin.reshape(D),
        'ev_w_in': dw_ev[None], 'ev_w_out': dw_evo[None],
        's5_lam_re': over_h(d_lre)[None], 's5_lam_im': over_h(d_lim)[None],
        's5_log_dt': over_h(d_ldt).sum(axis=1)[None],
        's5_b_re': un_tr(d_bre)[None], 's5_b_im': un_tr(d_bim)[None],
        's5_c_re': d_cre[None], 's5_c_im': d_cim[None],
        's5_d': d_d, 's5_w_glu': d_wg[None], 's5_b_glu': d_bg,
        'gm_w_s': d_gmw[None], 'gm_b_s': d_gmb.reshape(1, GM_HEADS, CHUNK), 'gm_v_g': d_gmg,
        'od_w_in': _deinterleave(dw_od, 3)[None], 'od_conv_w': d_od_cw[None], 'od_conv_b': d_od_cb,
        'od_w_out': dw_odo[None],
        'ffn_w_up': jnp.stack([_deinterleave(gf0['w_up'], 2), _deinterleave(gf1['w_up'], 2)]),
        'ffn_conv_w': jnp.stack([_deinterleave(gf0['cw'], 2), _deinterleave(gf1['cw'], 2)]),
        'ffn_conv_b': jnp.concatenate([_deinterleave(gf0['cb'], 2), _deinterleave(gf1['cb'], 2)], axis=0),
        'ffn_w_down': jnp.stack([gf0['w_down'], gf1['w_down']]),
    }
    return loss, grad_x, grads


def _shard_shape(name):
    shape, axis = SHARDED[name]
    return tuple(s // NDEV if a == axis else s for a, s in enumerate(shape))


def _pack(arrays, rows):
    flat = jnp.concatenate([a.reshape(-1).astype(F32) for a in arrays])
    return jnp.pad(flat, (0, rows * PACK_COLS - flat.shape[0])).reshape(rows, PACK_COLS)


def _unpack(buf, shapes):
    flat = buf.reshape(-1)
    out, off = [], 0
    for shp in shapes:
        size = int(np.prod(shp))
        out.append(flat[off:off + size].reshape(shp))
        off += size
    return out


SHARD_ELEMS = sum(int(np.prod(_shard_shape(n))) for n in SHARDED_ORDER)
SHARD_ROWS = -(-SHARD_ELEMS // (PACK_COLS * 256)) * 256
REPL_SHAPES = {'mix_norm_g': (2, 1024), 'ffn_norm_g': (2, 1024), 'final_norm_g': (1024,), 's5_lam_re': (1, 16, 64),
               's5_lam_im': (1, 16, 64), 's5_log_dt': (1, 16), 's5_b_re': (1, 16, 64, 16), 's5_b_im': (1, 16, 64, 16),
               's5_c_re': (1, 16, 16, 64), 's5_c_im': (1, 16, 16, 64), 's5_d': (1, 256), 's5_b_glu': (1, 256),
               'gm_w_s': (1, 6, 128, 128), 'gm_b_s': (1, 6, 128), 'gm_v_g': (1, 768), 'ffn_conv_b': (2, 5632)}
REPL_ELEMS = sum(int(np.prod(REPL_SHAPES[n])) for n in REPL_ORDER)
REPL_ROWS = -(-REPL_ELEMS // (PACK_COLS * 8)) * 8


def _gathered_to_full(buf):
    per_dev = buf.reshape(NDEV, SHARD_ROWS * PACK_COLS)
    full, off = {}, 0
    for name in SHARDED_ORDER:
        shp = _shard_shape(name)
        size = int(np.prod(shp))
        axis = SHARDED[name][1]
        parts = per_dev[:, off:off + size].reshape((NDEV,) + shp)
        full[name] = jnp.moveaxis(parts, 0, axis).reshape(
            shp[:axis] + (NDEV * shp[axis],) + shp[axis + 1:])
        off += size
    return full


def _full_to_scatter(grads):
    cols = []
    for name in SHARDED_ORDER:
        shp = _shard_shape(name)
        axis = SHARDED[name][1]
        g = grads[name].reshape(shp[:axis] + (NDEV, shp[axis]) + shp[axis + 1:])
        cols.append(jnp.moveaxis(g, axis, 0).reshape(NDEV, -1))
    flat = jnp.concatenate(cols, axis=1)
    flat = jnp.pad(flat, ((0, 0), (0, SHARD_ROWS * PACK_COLS - flat.shape[1])))
    return flat.reshape(NDEV, SHARD_ROWS, PACK_COLS)


HBM_SPEC = pl.BlockSpec(memory_space=pltpu.HBM)


def _all_gather(shard, name):
    m_per, n = shard.shape

    def body(x_ref, out_ref, send_sems, recv_sems, local_sem):
        x, y, c = lax.axis_index("x"), lax.axis_index("y"), lax.axis_index("c")
        me, sibling = (x, y, c), (x, y, 1 - c)
        chips = [(1 - x, y), (x, 1 - y), (1 - x, 1 - y)]

        def rows(px, py, pc):
            return out_ref.at[pl.ds((4 * px + 2 * py + pc) * m_per, m_per), :]

        def copy(k, block, to, src=None):
            return pltpu.make_async_remote_copy(
                src_ref=rows(*block) if src is None else src, dst_ref=rows(*block),
                send_sem=send_sems.at[k], recv_sem=recv_sems.at[k], device_id=to, device_id_type=MESH_T)

        mine = pltpu.make_async_copy(x_ref, rows(*me), local_sem)
        mine.start()
        first = [copy(0, me, sibling, src=x_ref)]
        first += [copy(1 + j, me, (*chip, c), src=x_ref) for j, chip in enumerate(chips)]
        for cp in first:
            cp.start()
        passed = [copy(4 + j, (*chip, c), sibling) for j, chip in enumerate(chips)]
        for j, chip in enumerate(chips):
            copy(1 + j, (*chip, c), me).wait_recv()
            passed[j].start()
        copy(0, sibling, me).wait_recv()
        for j, chip in enumerate(chips):
            copy(4 + j, (*chip, 1 - c), me).wait_recv()
        for cp in first + passed:
            cp.wait_send()
        mine.wait()

    return pl.pallas_call(
        body, name=name, out_shape=jax.ShapeDtypeStruct((NDEV * m_per, n), shard.dtype),
        in_specs=[HBM_SPEC], out_specs=HBM_SPEC,
        scratch_shapes=[pltpu.SemaphoreType.DMA((7,)), pltpu.SemaphoreType.DMA((7,)), pltpu.SemaphoreType.DMA])(shard)


def _exchange_sibling(parts, name):
    _, r, c_ = parts.shape

    def body(g_ref, r_ref, send_sems, recv_sems):
        x, y, c = lax.axis_index("x"), lax.axis_index("y"), lax.axis_index("c")
        copies = [pltpu.make_async_remote_copy(
            src_ref=g_ref.at[2 * chip + (1 - c)], dst_ref=r_ref.at[chip], send_sem=send_sems.at[chip],
            recv_sem=recv_sems.at[chip], device_id=(x, y, 1 - c), device_id_type=MESH_T) for chip in range(4)]
        for cp in copies:
            cp.start()
        for cp in copies:
            cp.wait()

    return pl.pallas_call(
        body, name=name, out_shape=jax.ShapeDtypeStruct((4, r, c_), parts.dtype),
        in_specs=[HBM_SPEC], out_specs=HBM_SPEC,
        scratch_shapes=[pltpu.SemaphoreType.DMA((4,)), pltpu.SemaphoreType.DMA((4,))])(parts)


def _add_sibling(parts, got, core, name):
    _, r, c_ = parts.shape
    tr = 256

    def body(core_ref, a_ref, b_ref, o_ref):
        o_ref[...] = a_ref[...] + b_ref[...]

    return pl.pallas_call(
        body, name=name, out_shape=jax.ShapeDtypeStruct((4, r, c_), F32),
        grid_spec=pltpu.PrefetchScalarGridSpec(
            num_scalar_prefetch=1, grid=(4, r // tr),
            in_specs=[pl.BlockSpec((None, tr, c_), lambda ch, i, core_ref: (2 * ch + core_ref[0], i, 0)),
                      pl.BlockSpec((None, tr, c_), lambda ch, i, core_ref: (ch, i, 0))],
            out_specs=pl.BlockSpec((None, tr, c_), lambda ch, i, core_ref: (ch, i, 0))),
        compiler_params=_cp(("parallel", "parallel")))(core, parts, got)


def _exchange_chips(sums, name):
    _, r, c_ = sums.shape

    def body(s_ref, r_ref, send_sems, recv_sems, local_sem):
        x, y, c = lax.axis_index("x"), lax.axis_index("y"), lax.axis_index("c")
        my_chip = 2 * x + y
        chips = [(1 - x, y), (x, 1 - y), (1 - x, 1 - y)]
        mine = pltpu.make_async_copy(s_ref.at[my_chip], r_ref.at[my_chip], local_sem)
        mine.start()

        def copy(k, chip, src_slot, dst_slot):
            return pltpu.make_async_remote_copy(
                src_ref=s_ref.at[src_slot], dst_ref=r_ref.at[dst_slot], send_sem=send_sems.at[k],
                recv_sem=recv_sems.at[k], device_id=(*chip, c), device_id_type=MESH_T)

        sent = [copy(k, chip, 2 * chip[0] + chip[1], my_chip) for k, chip in enumerate(chips)]
        for cp in sent:
            cp.start()
        for k, chip in enumerate(chips):
            slot = 2 * chip[0] + chip[1]
            copy(k, chip, slot, slot).wait_recv()
        for cp in sent:
            cp.wait_send()
        mine.wait()

    return pl.pallas_call(
        body, name=name, out_shape=jax.ShapeDtypeStruct((4, r, c_), sums.dtype),
        in_specs=[HBM_SPEC], out_specs=HBM_SPEC,
        scratch_shapes=[pltpu.SemaphoreType.DMA((3,)), pltpu.SemaphoreType.DMA((3,)), pltpu.SemaphoreType.DMA])(sums)


def _adamw(w, m, v, gparts, name):
    parts, rows, cols = gparts.shape
    tr = min(256, rows)
    bc1 = 1.0 - ADAM_B1 ** ADAM_STEP
    bc2 = 1.0 - ADAM_B2 ** ADAM_STEP

    def body(w_ref, m_ref, v_ref, g_ref, go_ref, d_ref, mo_ref, vo_ref):
        g = g_ref[0]
        for k in range(1, parts):
            g = g + g_ref[k]
        mn = ADAM_B1 * m_ref[...] + (1.0 - ADAM_B1) * g
        vn = ADAM_B2 * v_ref[...] + (1.0 - ADAM_B2) * (g * g)
        go_ref[...] = g
        mo_ref[...] = mn
        vo_ref[...] = vn
        d_ref[...] = -ADAM_LR * ((mn / bc1) / (jnp.sqrt(vn / bc2) + ADAM_EPS) + ADAM_WD * w_ref[...])

    blk = pl.BlockSpec((tr, cols), lambda i: (i, 0))
    shp = jax.ShapeDtypeStruct((rows, cols), F32)
    return pl.pallas_call(
        body, name=name, grid=(rows // tr,),
        in_specs=[blk, blk, blk, pl.BlockSpec((parts, tr, cols), lambda i: (0, i, 0))],
        out_specs=[blk] * 4, out_shape=[shp] * 4, compiler_params=_cp(("parallel",)))(w, m, v, gparts)


def kernel(x, mix_norm_g, ffn_norm_g, final_norm_g, ev_w_in, ev_w_out, s5_lam_re, s5_lam_im, s5_log_dt, s5_b_re, s5_b_im, s5_c_re, s5_c_im, s5_d, s5_w_glu, s5_b_glu, gm_w_s, gm_b_s, gm_v_g, od_w_in, od_conv_w, od_conv_b, od_w_out, ffn_w_up, ffn_conv_w, ffn_conv_b, ffn_w_down, loss_target, m_mix_norm_g, m_ffn_norm_g, m_final_norm_g, m_ev_w_in, m_ev_w_out, m_s5_lam_re, m_s5_lam_im, m_s5_log_dt, m_s5_b_re, m_s5_b_im, m_s5_c_re, m_s5_c_im, m_s5_d, m_s5_w_glu, m_s5_b_glu, m_gm_w_s, m_gm_b_s, m_gm_v_g, m_od_w_in, m_od_conv_w, m_od_conv_b, m_od_w_out, m_ffn_w_up, m_ffn_conv_w, m_ffn_conv_b, m_ffn_w_down, v_mix_norm_g, v_ffn_norm_g, v_final_norm_g, v_ev_w_in, v_ev_w_out, v_s5_lam_re, v_s5_lam_im, v_s5_log_dt, v_s5_b_re, v_s5_b_im, v_s5_c_re, v_s5_c_im, v_s5_d, v_s5_w_glu, v_s5_b_glu, v_gm_w_s, v_gm_b_s, v_gm_v_g, v_od_w_in, v_od_conv_w, v_od_conv_b, v_od_w_out, v_ffn_w_up, v_ffn_conv_w, v_ffn_conv_b, v_ffn_w_down):
    given = dict(locals())
    weights = {n: given[n] for n in WEIGHT_ORDER}
    nseq, seq, _ = x.shape

    w_pack = _pack([weights[n] for n in SHARDED_ORDER], SHARD_ROWS)
    full = dict(weights)
    full.update(_gathered_to_full(_all_gather(w_pack, "gather_weights")))

    loss_row, grad_x, grads = _local_step(x.reshape(nseq * seq, D), loss_target.reshape(nseq * seq, D), full, seq)
    loss = lax.psum(loss_row[0, 0], ("x", "y", "c"))

    core = lax.axis_index("c").astype(jnp.int32).reshape(1)
    scatter = _full_to_scatter(grads)
    from_sibling = _exchange_sibling(scatter, "reduce_sibling")
    chip_sums = _add_sibling(scatter, from_sibling, core, "reduce_add")
    by_chip = _exchange_chips(chip_sums, "reduce_chips")
    repl_parts = _all_gather(_pack([grads[n] for n in REPL_ORDER], REPL_ROWS), "gather_small_grads")

    m_pack = _pack([given["m_" + n] for n in SHARDED_ORDER], SHARD_ROWS)
    v_pack = _pack([given["v_" + n] for n in SHARDED_ORDER], SHARD_ROWS)
    sh_out = _adamw(w_pack, m_pack, v_pack, by_chip, "adamw_sharded")
    rw = _pack([weights[n] for n in REPL_ORDER], REPL_ROWS)
    rm = _pack([given["m_" + n] for n in REPL_ORDER], REPL_ROWS)
    rv = _pack([given["v_" + n] for n in REPL_ORDER], REPL_ROWS)
    rp_out = _adamw(rw, rm, rv, repl_parts.reshape(NDEV, REPL_ROWS, PACK_COLS), "adamw_replicated")

    sh_shapes = [weights[n].shape for n in SHARDED_ORDER]
    rp_shapes = [weights[n].shape for n in REPL_ORDER]
    results = []
    for k in range(4):
        by_name = dict(zip(SHARDED_ORDER, _unpack(sh_out[k], sh_shapes)))
        by_name.update(zip(REPL_ORDER, _unpack(rp_out[k], rp_shapes)))
        results.append([by_name[n] for n in WEIGHT_ORDER])
    grad_w, delta_w, new_m, new_v = results
    return (loss, grad_x.reshape(nseq, seq, D), *grad_w, *delta_w, *new_m, *new_v)
```

```python
import math

import jax
import jax.numpy as jnp
import numpy as np
from jax import lax
from jax.experimental import pallas as pl
from jax.experimental.pallas import tpu as pltpu

F32 = jnp.float32
BF = jnp.bfloat16

D = 1024
DFF = 2816
NDEV = 8
SSM_W = 256
SSM_G = 16
SSM_H = 16
SSM_P = 64
NST = SSM_G * SSM_P
GM_W = 768
GM_HEADS = 6
CHUNK = 128
EPS = 1e-6
LAM_MAX = -1e-4
FB = 256
FSH = 2 * DFF // NDEV
VMEM_LIMIT = 48 * 2**20
PACK_COLS = 1024
MESH_T = pl.DeviceIdType.MESH

ADAM_LR = 0.001
ADAM_B1 = 0.9
ADAM_B2 = 0.999
ADAM_EPS = 1e-08
ADAM_WD = 0.01
ADAM_STEP = 10

WEIGHT_ORDER = ['mix_norm_g', 'ffn_norm_g', 'final_norm_g', 'ev_w_in', 'ev_w_out', 's5_lam_re', 's5_lam_im',
                's5_log_dt', 's5_b_re', 's5_b_im', 's5_c_re', 's5_c_im', 's5_d', 's5_w_glu', 's5_b_glu', 'gm_w_s',
                'gm_b_s', 'gm_v_g', 'od_w_in', 'od_conv_w', 'od_conv_b', 'od_w_out', 'ffn_w_up', 'ffn_conv_w',
                'ffn_conv_b', 'ffn_w_down']
SHARDED = {'ev_w_in': ((1, 1024, 1792), 2), 'ev_w_out': ((1, 1024, 1024), 1), 's5_w_glu': ((1, 256, 256), 1),
           'od_w_in': ((1, 1024, 3072), 2), 'od_conv_w': ((1, 3, 1024), 2), 'od_conv_b': ((1, 1024), 1),
           'od_w_out': ((1, 1024, 1024), 1), 'ffn_w_up': ((2, 1024, 5632), 2), 'ffn_conv_w': ((2, 3, 5632), 2),
           'ffn_w_down': ((2, 2816, 1024), 1)}
SHARDED_ORDER = [n for n in WEIGHT_ORDER if n in SHARDED]
REPL_ORDER = [n for n in WEIGHT_ORDER if n not in SHARDED]


def _cp(sem):
    return pltpu.CompilerParams(dimension_semantics=sem, vmem_limit_bytes=VMEM_LIMIT)


def _sigmoid(x):
    return 1.0 / (1.0 + jnp.exp(-x))


_GELU_K = math.sqrt(2.0 / math.pi)


def _gelu(x):
    return 0.5 * x * (1.0 + jnp.tanh(_GELU_K * (x + 0.044715 * x * x * x)))


def _gelu_grad(x):
    t = jnp.tanh(_GELU_K * (x + 0.044715 * x * x * x))
    return 0.5 * (1.0 + t) + 0.5 * x * (1.0 - t * t) * _GELU_K * (1.0 + 3.0 * 0.044715 * x * x)


def _colsum(x):
    return jnp.sum(x, axis=0, keepdims=True)


def _accumulate(ref, first, part):
    @pl.when(first)
    def _():
        ref[...] = part

    @pl.when(jnp.logical_not(first))
    def _():
        ref[...] += part


_DIMS = {'nn': (((1,), (0,)), ((), ())), 'nt': (((1,), (1,)), ((), ())), 'tn': (((0,), (0,)), ((), ()))}


def _matmul(a, b, mode, tm, tn, tk, name, resid=None, out_dtype=F32):
    if mode == 'tn':
        kdim, m = a.shape
    else:
        m, kdim = a.shape
    n = b.shape[0] if mode == 'nt' else b.shape[1]
    tm, tn, tk = min(tm, m), min(tn, n), min(tk, kdim)
    assert m % tm == 0 and n % tn == 0 and kdim % tk == 0, (name, m, n, kdim, tm, tn, tk)
    a_spec = (pl.BlockSpec((tk, tm), lambda i, j, k: (k, i)) if mode == 'tn'
              else pl.BlockSpec((tm, tk), lambda i, j, k: (i, k)))
    b_spec = (pl.BlockSpec((tn, tk), lambda i, j, k: (j, k)) if mode == 'nt'
              else pl.BlockSpec((tk, tn), lambda i, j, k: (k, j)))
    o_spec = pl.BlockSpec((tm, tn), lambda i, j, k: (i, j))
    return _matmul_spec(a, b, mode, (m // tm, n // tn, kdim // tk), a_spec, b_spec, o_spec, (m, n), name,
                        resid=resid, out_dtype=out_dtype)


def _matmul_spec(a, b, mode, grid, a_spec, b_spec, o_spec, out_shape, name, resid=None, out_dtype=F32):
    nk = grid[2]
    tm, tn = o_spec.block_shape[-2:]
    dims = _DIMS[mode]
    has_resid = resid is not None

    def body(*refs):
        if has_resid:
            a_ref, b_ref, r_ref, o_ref = refs[:4]
        else:
            a_ref, b_ref, o_ref = refs[:3]
            r_ref = None
        part = lax.dot_general(a_ref[...].astype(BF), b_ref[...].astype(BF), dims, preferred_element_type=F32)
        if nk == 1:
            if has_resid:
                part = part + r_ref[...]
            o_ref[...] = part.astype(out_dtype)
        else:
            acc = refs[-1]
            k = pl.program_id(2)

            @pl.when(k == 0)
            def _():
                acc[...] = part

            @pl.when(k > 0)
            def _():
                acc[...] += part

            @pl.when(k == nk - 1)
            def _():
                tot = acc[...]
                if has_resid:
                    tot = tot + r_ref[...]
                o_ref[...] = tot.astype(out_dtype)

    operands = [a, b] + ([resid] if has_resid else [])
    in_specs = [a_spec, b_spec] + ([o_spec] if has_resid else [])
    return pl.pallas_call(
        body, name=name, grid=grid, in_specs=in_specs, out_specs=o_spec,
        out_shape=jax.ShapeDtypeStruct(out_shape, out_dtype),
        scratch_shapes=[pltpu.VMEM((tm, tn), F32)] if nk > 1 else [],
        compiler_params=_cp(("parallel", "parallel", "arbitrary")))(*operands)


def _rmsnorm_fwd(x, g, name):
    n = x.shape[0]
    tm = min(512, n)

    def body(x_ref, g_ref, o_ref):
        xv = x_ref[...]
        r = lax.rsqrt(jnp.mean(xv * xv, axis=-1, keepdims=True) + EPS)
        o_ref[...] = (xv * r * g_ref[...]).astype(BF)

    return pl.pallas_call(
        body, name=name, grid=(n // tm,),
        in_specs=[pl.BlockSpec((tm, D), lambda i: (i, 0)), pl.BlockSpec((1, D), lambda i: (0, 0))],
        out_specs=pl.BlockSpec((tm, D), lambda i: (i, 0)),
        out_shape=jax.ShapeDtypeStruct((n, D), BF), compiler_params=_cp(("parallel",)))(x, g)


def _rmsnorm_bwd(x, g, dy, dres, name):
    n = x.shape[0]
    tm = min(512, n)

    def body(x_ref, g_ref, dy_ref, dr_ref, dx_ref, dg_ref):
        xv = x_ref[...]
        r = lax.rsqrt(jnp.mean(xv * xv, axis=-1, keepdims=True) + EPS)
        xh = xv * r
        dyv = dy_ref[...]
        dyg = dyv * g_ref[...]
        dx_ref[...] = dr_ref[...] + r * (dyg - xh * jnp.mean(dyg * xh, axis=-1, keepdims=True))
        _accumulate(dg_ref, pl.program_id(0) == 0, _colsum(dyv * xh))

    row = pl.BlockSpec((tm, D), lambda i: (i, 0))
    vec = pl.BlockSpec((1, D), lambda i: (0, 0))
    return pl.pallas_call(
        body, name=name, grid=(n // tm,), in_specs=[row, vec, row, row], out_specs=[row, vec],
        out_shape=[jax.ShapeDtypeStruct((n, D), F32), jax.ShapeDtypeStruct((1, D), F32)],
        compiler_params=_cp(("arbitrary",)))(x, g, dy, dres)


def _final_loss(h, g, tgt, name):
    n = h.shape[0]
    tm = min(512, n)

    def body(x_ref, g_ref, t_ref, loss_ref, dx_ref, dg_ref):
        first = pl.program_id(0) == 0
        xv = x_ref[...]
        gv = g_ref[...]
        r = lax.rsqrt(jnp.mean(xv * xv, axis=-1, keepdims=True) + EPS)
        xh = xv * r
        err = xh * gv - t_ref[...]
        part = 0.5 * jnp.sum(jnp.mean(err * err, axis=-1, keepdims=True), axis=0, keepdims=True)
        _accumulate(loss_ref, first, jnp.broadcast_to(part, (1, 128)))
        dyv = err * (1.0 / D)
        dyg = dyv * gv
        dx_ref[...] = r * (dyg - xh * jnp.mean(dyg * xh, axis=-1, keepdims=True))
        _accumulate(dg_ref, first, _colsum(dyv * xh))

    row = pl.BlockSpec((tm, D), lambda i: (i, 0))
    vec = pl.BlockSpec((1, D), lambda i: (0, 0))
    return pl.pallas_call(
        body, name=name, grid=(n // tm,), in_specs=[row, vec, row],
        out_specs=[pl.BlockSpec((1, 128), lambda i: (0, 0)), row, vec],
        out_shape=[jax.ShapeDtypeStruct((1, 128), F32), jax.ShapeDtypeStruct((n, D), F32),
                   jax.ShapeDtypeStruct((1, D), F32)],
        compiler_params=_cp(("arbitrary",)))(h, g, tgt)


def _prev_rows(x, halo_ref, lanes, scale, row):
    h7 = halo_ref[7:8, lanes] * scale
    h6 = halo_ref[6:7, lanes] * scale
    p1 = jnp.where(row == 0, h7, pltpu.roll(x, 1, 0))
    p2 = jnp.where(row == 0, h6, jnp.where(row == 1, h7, pltpu.roll(x, 2, 0)))
    return p1, p2


def _halo_maps(tm, n_rows):
    r8 = tm // 8
    last = n_rows // 8 - 1
    prev = lambda i: jnp.maximum(i * r8 - 1, 0)
    nxt = lambda i: jnp.minimum((i + 1) * r8, last)
    return prev, nxt


def _ffn_conv_fwd(up, cw, cb, seq, name):
    n = up.shape[2]
    tm = min(256, seq)
    prev, _ = _halo_maps(tm, n)

    def body(u_ref, h_ref, w_ref, b_ref, o_ref):
        i = pl.program_id(1)
        scale = jnp.where(lax.rem(i * tm, seq) == 0, 0.0, 1.0)
        row = lax.broadcasted_iota(jnp.int32, (tm, FSH), 0)
        hc = []
        for g in range(2):
            x = u_ref[g]
            p1, p2 = _prev_rows(x, h_ref.at[g], slice(None), scale, row)
            hc.append(b_ref[g] + w_ref[g, 0:1, :] * p2 + w_ref[g, 1:2, :] * p1 + w_ref[g, 2:3, :] * x)
        o_ref[...] = (hc[0] * _sigmoid(hc[0]) * hc[1]).astype(BF)

    return pl.pallas_call(
        body, name=name, grid=(4, n // tm),
        in_specs=[pl.BlockSpec((2, None, tm, FSH), lambda j, i: (0, j, i, 0)),
                  pl.BlockSpec((2, None, 8, FSH), lambda j, i: (0, j, prev(i), 0)),
                  pl.BlockSpec((2, None, 3, FSH), lambda j, i: (0, j, 0, 0)),
                  pl.BlockSpec((2, None, 1, FSH), lambda j, i: (0, j, 0, 0))],
        out_specs=pl.BlockSpec((None, tm, FSH), lambda j, i: (j, i, 0)),
        out_shape=jax.ShapeDtypeStruct((4, n, FSH), BF), compiler_params=_cp(("parallel", "parallel")))(up, up, cw, cb)


def _ffn_conv_bwd(up, dact, cw, cb, seq, name):
    n = up.shape[2]
    tm = min(256, seq)
    ext = tm + 16
    prev, nxt = _halo_maps(tm, n)

    def body(u_ref, up_ref, un_ref, da_ref, dn_ref, w_ref, b_ref, du_ref, dw_ref, db_ref, ux, dx):
        i = pl.program_id(1)
        sp = jnp.where(lax.rem(i * tm, seq) == 0, 0.0, 1.0)
        sn = jnp.where(lax.rem((i + 1) * tm, seq) == 0, 0.0, 1.0)
        main = slice(8, 8 + tm)
        dx[0:8, :] = jnp.zeros((8, FSH), F32)
        dx[main, :] = da_ref[...]
        dx[8 + tm:, :] = dn_ref[...] * sn
        x0, x1, x2, hc = [], [], [], []
        for g in range(2):
            ux[g, 0:8, :] = up_ref[g] * sp
            ux[g, main, :] = u_ref[g]
            ux[g, 8 + tm:, :] = un_ref[g]
            x0.append(ux[g])
            x1.append(pltpu.roll(x0[g], 1, 0))
            x2.append(pltpu.roll(x0[g], 2, 0))
            hc.append(b_ref[g] + w_ref[g, 0:1, :] * x2[g] + w_ref[g, 1:2, :] * x1[g] + w_ref[g, 2:3, :] * x0[g])
        s = _sigmoid(hc[0])
        da = dx[...]
        dhc = (da * hc[1] * (s * (1.0 + hc[0] * (1.0 - s))), da * (hc[0] * s))
        first = i == 0
        for g in range(2):
            dh = dhc[g]
            dup = (w_ref[g, 2:3, :] * dh + w_ref[g, 1:2, :] * pltpu.roll(dh, ext - 1, 0)
                   + w_ref[g, 0:1, :] * pltpu.roll(dh, ext - 2, 0))
            du_ref[g] = dup[main].astype(BF)
            dm = dh[main]
            _accumulate(dw_ref.at[g, 0:1, :], first, _colsum(dm * x2[g][main]))
            _accumulate(dw_ref.at[g, 1:2, :], first, _colsum(dm * x1[g][main]))
            _accumulate(dw_ref.at[g, 2:3, :], first, _colsum(dm * x0[g][main]))
            _accumulate(db_ref.at[g], first, _colsum(dm))

    return pl.pallas_call(
        body, name=name, grid=(4, n // tm),
        in_specs=[pl.BlockSpec((2, None, tm, FSH), lambda j, i: (0, j, i, 0)),
                  pl.BlockSpec((2, None, 8, FSH), lambda j, i: (0, j, prev(i), 0)),
                  pl.BlockSpec((2, None, 8, FSH), lambda j, i: (0, j, nxt(i), 0)),
                  pl.BlockSpec((None, tm, FSH), lambda j, i: (j, i, 0)),
                  pl.BlockSpec((None, 8, FSH), lambda j, i: (j, nxt(i), 0)),
                  pl.BlockSpec((2, None, 3, FSH), lambda j, i: (0, j, 0, 0)),
                  pl.BlockSpec((2, None, 1, FSH), lambda j, i: (0, j, 0, 0))],
        out_specs=[pl.BlockSpec((2, None, tm, FSH), lambda j, i: (0, j, i, 0)),
                   pl.BlockSpec((2, None, 3, FSH), lambda j, i: (0, j, 0, 0)),
                   pl.BlockSpec((2, None, 1, FSH), lambda j, i: (0, j, 0, 0))],
        out_shape=[jax.ShapeDtypeStruct((2, 4, n, FSH), BF), jax.ShapeDtypeStruct((2, 4, 3, FSH), F32),
                   jax.ShapeDtypeStruct((2, 4, 1, FSH), F32)],
        scratch_shapes=[pltpu.VMEM((2, ext, FSH), F32), pltpu.VMEM((ext, FSH), F32)],
        compiler_params=_cp(("parallel", "arbitrary")))(up, up, up, dact, dact, cw, cb)


def _shortconv_fwd(p, cw, cb, seq, name):
    n = p.shape[0]
    tm = min(256, seq)
    prev, _ = _halo_maps(tm, n)

    def body(p_ref, h_ref, w_ref, b_ref, o_ref):
        i = pl.program_id(1)
        scale = jnp.where(lax.rem(i * tm, seq) == 0, 0.0, 1.0)
        q = p_ref[:, FB:2 * FB] * p_ref[:, 2 * FB:]
        row = lax.broadcasted_iota(jnp.int32, q.shape, 0)
        h7 = h_ref[7:8, FB:2 * FB] * h_ref[7:8, 2 * FB:] * scale
        h6 = h_ref[6:7, FB:2 * FB] * h_ref[6:7, 2 * FB:] * scale
        p1 = jnp.where(row == 0, h7, pltpu.roll(q, 1, 0))
        p2 = jnp.where(row == 0, h6, jnp.where(row == 1, h7, pltpu.roll(q, 2, 0)))
        conv = b_ref[...] + w_ref[0:1, :] * p2 + w_ref[1:2, :] * p1 + w_ref[2:3, :] * q
        o_ref[...] = (p_ref[:, :FB] * conv).astype(BF)

    return pl.pallas_call(
        body, name=name, grid=(D // FB, n // tm),
        in_specs=[pl.BlockSpec((tm, 3 * FB), lambda j, i: (i, j)),
                  pl.BlockSpec((8, 3 * FB), lambda j, i: (prev(i), j)),
                  pl.BlockSpec((3, FB), lambda j, i: (0, j)),
                  pl.BlockSpec((1, FB), lambda j, i: (0, j))],
        out_specs=pl.BlockSpec((tm, FB), lambda j, i: (i, j)),
        out_shape=jax.ShapeDtypeStruct((n, D), BF), compiler_params=_cp(("parallel", "parallel")))(p, p, cw, cb)


def _shortconv_bwd(p, dmix, cw, cb, seq, name):
    n = p.shape[0]
    tm = min(256, seq)
    ext = tm + 16
    prev, nxt = _halo_maps(tm, n)

    def body(p_ref, pp_ref, pn_ref, dm_ref, dn_ref, w_ref, b_ref, dp_ref, dw_ref, db_ref, qx, cx):
        i = pl.program_id(1)
        sp = jnp.where(lax.rem(i * tm, seq) == 0, 0.0, 1.0)
        sn = jnp.where(lax.rem((i + 1) * tm, seq) == 0, 0.0, 1.0)
        bg, cg, hx = p_ref[:, :FB], p_ref[:, FB:2 * FB], p_ref[:, 2 * FB:]
        dm = dm_ref[...]
        qx[0:8, :] = pp_ref[:, FB:2 * FB] * pp_ref[:, 2 * FB:] * sp
        qx[8:8 + tm, :] = cg * hx
        qx[8 + tm:, :] = jnp.zeros((8, FB), F32)
        cx[0:8, :] = jnp.zeros((8, FB), F32)
        cx[8:8 + tm, :] = dm * bg
        cx[8 + tm:, :] = dn_ref[...] * pn_ref[:, :FB] * sn
        q0 = qx[...]
        q1 = pltpu.roll(q0, 1, 0)
        q2 = pltpu.roll(q0, 2, 0)
        main = slice(8, 8 + tm)
        conv = b_ref[...] + w_ref[0:1, :] * q2[main] + w_ref[1:2, :] * q1[main] + w_ref[2:3, :] * q0[main]
        dc = cx[...]
        dq = (w_ref[2:3, :] * dc + w_ref[1:2, :] * pltpu.roll(dc, ext - 1, 0)
              + w_ref[0:1, :] * pltpu.roll(dc, ext - 2, 0))[main]
        dp_ref[:, :FB] = (dm * conv).astype(BF)
        dp_ref[:, FB:2 * FB] = (dq * hx).astype(BF)
        dp_ref[:, 2 * FB:] = (dq * cg).astype(BF)
        first = i == 0
        dcm = dc[main]
        _accumulate(dw_ref.at[0:1, :], first, _colsum(dcm * q2[main]))
        _accumulate(dw_ref.at[1:2, :], first, _colsum(dcm * q1[main]))
        _accumulate(dw_ref.at[2:3, :], first, _colsum(dcm * q0[main]))
        _accumulate(db_ref, first, _colsum(dcm))

    return pl.pallas_call(
        body, name=name, grid=(D // FB, n // tm),
        in_specs=[pl.BlockSpec((tm, 3 * FB), lambda j, i: (i, j)),
                  pl.BlockSpec((8, 3 * FB), lambda j, i: (prev(i), j)),
                  pl.BlockSpec((8, 3 * FB), lambda j, i: (nxt(i), j)),
                  pl.BlockSpec((tm, FB), lambda j, i: (i, j)),
                  pl.BlockSpec((8, FB), lambda j, i: (nxt(i), j)),
                  pl.BlockSpec((3, FB), lambda j, i: (0, j)),
                  pl.BlockSpec((1, FB), lambda j, i: (0, j))],
        out_specs=[pl.BlockSpec((tm, 3 * FB), lambda j, i: (i, j)),
                   pl.BlockSpec((3, FB), lambda j, i: (0, j)),
                   pl.BlockSpec((1, FB), lambda j, i: (0, j))],
        out_shape=[jax.ShapeDtypeStruct((n, 3 * D), BF), jax.ShapeDtypeStruct((3, D), F32),
                   jax.ShapeDtypeStruct((1, D), F32)],
        scratch_shapes=[pltpu.VMEM((ext, FB), F32), pltpu.VMEM((ext, FB), F32)],
        compiler_params=_cp(("parallel", "arbitrary")))(p, p, p, dmix, dmix, cw, cb)


def _gmlp_fwd(uv, wm, bst, gv, seq, name):
    n = uv.shape[0]
    tm = min(256, seq)

    def body(x_ref, w_ref, b_ref, g_ref, o_ref):
        ge_v = _gelu(x_ref[:, GM_W:])
        r = lax.rsqrt(jnp.mean(ge_v * ge_v, axis=-1, keepdims=True) + EPS)
        vn = (ge_v * r * g_ref[...]).astype(BF)
        for c in range(tm // CHUNK):
            rows = slice(c * CHUNK, (c + 1) * CHUNK)
            for h in range(GM_HEADS):
                cols = slice(h * CHUNK, (h + 1) * CHUNK)
                gate = jnp.dot(w_ref[h], vn[rows, cols], preferred_element_type=F32) + b_ref[:, h:h + 1]
                o_ref[rows, cols] = (_gelu(x_ref[rows, cols]) * gate).astype(BF)

    return pl.pallas_call(
        body, name=name, grid=(n // tm,),
        in_specs=[pl.BlockSpec((tm, 2 * GM_W), lambda i: (i, 0)),
                  pl.BlockSpec((GM_HEADS, CHUNK, CHUNK), lambda i: (0, 0, 0)),
                  pl.BlockSpec((CHUNK, GM_HEADS), lambda i: (0, 0)),
                  pl.BlockSpec((1, GM_W), lambda i: (0, 0))],
        out_specs=pl.BlockSpec((tm, GM_W), lambda i: (i, 0)),
        out_shape=jax.ShapeDtypeStruct((n, GM_W), BF), compiler_params=_cp(("parallel",)))(uv, wm, bst, gv)


def _gmlp_bwd(uv, dout, wm, wmt, bst, gv, seq, name):
    n = uv.shape[0]
    tm = min(256, seq)

    def body(x_ref, do_ref, w_ref, wt_ref, b_ref, g_ref, dx_ref, dw_ref, db_ref, dg_ref, dvn_scr):
        first = pl.program_id(0) == 0
        ge_v = _gelu(x_ref[:, GM_W:])
        r = lax.rsqrt(jnp.mean(ge_v * ge_v, axis=-1, keepdims=True) + EPS)
        vh = ge_v * r
        vn = (vh * g_ref[...]).astype(BF)
        tril = (lax.broadcasted_iota(jnp.int32, (CHUNK, CHUNK), 0)
                >= lax.broadcasted_iota(jnp.int32, (CHUNK, CHUNK), 1))
        for h in range(GM_HEADS):
            cols = slice(h * CHUNK, (h + 1) * CHUNK)
            dw = jnp.zeros((CHUNK, CHUNK), F32)
            dbs = jnp.zeros((CHUNK, 1), F32)
            for c in range(tm // CHUNK):
                rows = slice(c * CHUNK, (c + 1) * CHUNK)
                blk = vn[rows, cols]
                gate = jnp.dot(w_ref[h], blk, preferred_element_type=F32) + b_ref[:, h:h + 1]
                xu = x_ref[rows, cols]
                do = do_ref[rows, cols]
                dx_ref[rows, cols] = (do * gate * _gelu_grad(xu)).astype(BF)
                dgate = do * _gelu(xu)
                dgb = dgate.astype(BF)
                dw = dw + lax.dot_general(dgb, blk, _DIMS['nt'], preferred_element_type=F32)
                dbs = dbs + jnp.sum(dgate, axis=1, keepdims=True)
                dvn_scr[rows, cols] = jnp.dot(wt_ref[h], dgb, preferred_element_type=F32)
            _accumulate(dw_ref.at[h], first, jnp.where(tril, dw, 0.0))
            _accumulate(db_ref.at[h], first, dbs)
        dvn = dvn_scr[...]
        _accumulate(dg_ref, first, _colsum(dvn * vh))
        dvh = dvn * g_ref[...]
        dv = r * (dvh - vh * jnp.mean(dvh * vh, axis=-1, keepdims=True))
        dx_ref[:, GM_W:] = (dv * _gelu_grad(x_ref[:, GM_W:])).astype(BF)

    full3 = pl.BlockSpec((GM_HEADS, CHUNK, CHUNK), lambda i: (0, 0, 0))
    return pl.pallas_call(
        body, name=name, grid=(n // tm,),
        in_specs=[pl.BlockSpec((tm, 2 * GM_W), lambda i: (i, 0)), pl.BlockSpec((tm, GM_W), lambda i: (i, 0)),
                  full3, full3, pl.BlockSpec((CHUNK, GM_HEADS), lambda i: (0, 0)),
                  pl.BlockSpec((1, GM_W), lambda i: (0, 0))],
        out_specs=[pl.BlockSpec((tm, 2 * GM_W), lambda i: (i, 0)), full3,
                   pl.BlockSpec((GM_HEADS, CHUNK, 1), lambda i: (0, 0, 0)),
                   pl.BlockSpec((1, GM_W), lambda i: (0, 0))],
        out_shape=[jax.ShapeDtypeStruct((n, 2 * GM_W), BF), jax.ShapeDtypeStruct((GM_HEADS, CHUNK, CHUNK), F32),
                   jax.ShapeDtypeStruct((GM_HEADS, CHUNK, 1), F32), jax.ShapeDtypeStruct((1, GM_W), F32)],
        scratch_shapes=[pltpu.VMEM((tm, GM_W), F32)],
        compiler_params=_cp(("arbitrary",)))(uv, dout, wm, wmt, bst, gv)


def _s5_disc(lam_re, lam_im, log_dt, b_re, b_im):
    lr = jnp.minimum(lam_re, LAM_MAX)
    li = lam_im
    dt = jnp.exp(log_dt)
    mag = jnp.exp(lr * dt)
    ab_re = mag * jnp.cos(li * dt)
    ab_im = mag * jnp.sin(li * dt)
    den = lr * lr + li * li
    nr = ab_re - 1.0
    ni = ab_im
    z_re = (nr * lr + ni * li) / den
    z_im = (ni * lr - nr * li) / den
    return ab_re, ab_im, z_re * b_re - z_im * b_im, z_re * b_im + z_im * b_re


def _s5_disc_fwd(args, name):
    shp = jax.ShapeDtypeStruct(args[0].shape, F32)

    def body(*refs):
        outs = _s5_disc(*[r[...] for r in refs[:5]])
        for o_ref, o in zip(refs[5:], outs):
            o_ref[...] = o

    return pl.pallas_call(body, name=name, out_shape=[shp] * 4)(*args)


def _s5_disc_bwd(args, cts, name):
    shp = jax.ShapeDtypeStruct(args[0].shape, F32)

    def body(*refs):
        _, vjp = jax.vjp(_s5_disc, *[r[...] for r in refs[:5]])
        grads = vjp(tuple(r[...] for r in refs[5:9]))
        for o_ref, o in zip(refs[9:], grads):
            o_ref[...] = o

    return pl.pallas_call(body, name=name, out_shape=[shp] * 5)(*args, *cts)


def _cmul(a, b):
    return a[0] * b[0] - a[1] * b[1], a[0] * b[1] + a[1] * b[0]


def _scan_tables(ar, ai, reverse):
    if reverse:
        ai = -ai
    a1 = (ar, ai)
    a2 = _cmul(a1, a1)
    a3 = _cmul(a2, a1)
    a4 = _cmul(a2, a2)
    powers = [a1, a2, a3, a4, _cmul(a4, a1), _cmul(a4, a2), _cmul(a4, a3), _cmul(a4, a4)]
    row = lax.broadcasted_iota(jnp.int32, (8, NST), 0)
    zero = jnp.zeros((8, NST), F32)
    pr, pi = zero, zero
    for r in range(8):
        pw = powers[7 - r] if reverse else powers[r]
        pr = jnp.where(row == r, pw[0], pr)
        pi = jnp.where(row == r, pw[1], pi)
    levels = []
    for d, pw in ((1, a1), (2, a2), (4, a4)):
        ok = (row <= 7 - d) if reverse else (row >= d)
        levels.append((d, jnp.where(ok, pw[0], zero), jnp.where(ok, pw[1], zero)))
    return (pr, pi), levels


def _scan_block(src, dst, car, tables, n_tiles, reverse):
    (pr, pi), levels = tables
    row = lax.broadcasted_iota(jnp.int32, (8, NST), 0)
    out_row = 0 if reverse else 7

    def step(t, carry):
        cr, ci = carry
        tile = (n_tiles - 1 - t) if reverse else t
        rows = pl.ds(pl.multiple_of(tile * 8, 8), 8)
        xr = src[rows, 0:NST]
        xi = src[rows, NST:2 * NST]
        for d, dr, di in levels:
            shift = 8 - d if reverse else d
            rr = pltpu.roll(xr, shift, 0)
            ri = pltpu.roll(xi, shift, 0)
            xr, xi = xr + dr * rr - di * ri, xi + dr * ri + di * rr
        hr = xr + pr * cr - pi * ci
        hi = xi + pr * ci + pi * cr
        dst[rows, 0:NST] = hr
        dst[rows, NST:2 * NST] = hi
        return (_colsum(jnp.where(row == out_row, hr, 0.0)), _colsum(jnp.where(row == out_row, hi, 0.0)))

    cr, ci = lax.fori_loop(0, n_tiles, step, (car[0:1, 0:NST], car[0:1, NST:2 * NST]))
    car[0:1, 0:NST] = cr
    car[0:1, NST:2 * NST] = ci


def _s5_fwd(u, ab, bbt, cmat, dvec, wglu, bglu, seq, name):
    n = u.shape[0]
    tm = min(256, seq)

    def body(u_ref, ab_ref, bb_ref, c_ref, d_ref, w_ref, b_ref, h_ref, o_ref, xs, car):
        i = pl.program_id(0)

        @pl.when(lax.rem(i * tm, seq) == 0)
        def _():
            car[...] = jnp.zeros(car.shape, F32)

        uv = u_ref[...]
        xs[...] = jnp.dot(uv.astype(BF), bb_ref[...], preferred_element_type=F32)
        tables = _scan_tables(ab_ref[0:1, 0:NST], ab_ref[0:1, NST:2 * NST], False)
        _scan_block(xs, h_ref, car, tables, tm // 8, False)
        y = jnp.dot(h_ref[...].astype(BF), c_ref[...], preferred_element_type=F32) + d_ref[...] * uv
        g1 = _gelu(y)
        z = jnp.dot(g1.astype(BF), w_ref[...], preferred_element_type=F32) + b_ref[...]
        o_ref[...] = (g1 * _sigmoid(z)).astype(BF)

    const = lambda shape: pl.BlockSpec(shape, lambda i: (0, 0))
    return pl.pallas_call(
        body, name=name, grid=(n // tm,),
        in_specs=[pl.BlockSpec((tm, SSM_W), lambda i: (i, 0)), const((1, 2 * NST)), const((SSM_W, 2 * NST)),
                  const((2 * NST, SSM_W)), const((1, SSM_W)), const((SSM_W, SSM_W)), const((1, SSM_W))],
        out_specs=[pl.BlockSpec((tm, 2 * NST), lambda i: (i, 0)), pl.BlockSpec((tm, SSM_W), lambda i: (i, 0))],
        out_shape=[jax.ShapeDtypeStruct((n, 2 * NST), F32), jax.ShapeDtypeStruct((n, SSM_W), BF)],
        scratch_shapes=[pltpu.VMEM((tm, 2 * NST), F32), pltpu.VMEM((8, 2 * NST), F32)],
        compiler_params=_cp(("arbitrary",)))(u, ab, bbt, cmat, dvec, wglu, bglu)


def _s5_bwd(da, u, hst, ab, bbt, cmat, dvec, wglu, bglu, seq, name):
    n = u.shape[0]
    tm = min(256, seq)
    nb = n // tm
    blk = lambda r: nb - 1 - r
    prev, _ = _halo_maps(tm, n)

    def body(da_ref, u_ref, h_ref, hp_ref, ab_ref, bb_ref, c_ref, d_ref, w_ref, b_ref,
             du_ref, dw_ref, dbg_ref, dd_ref, dc_ref, dbb_ref, dab_ref, gs, car):
        r = pl.program_id(0)
        i = blk(r)
        first = r == 0

        @pl.when(lax.rem((i + 1) * tm, seq) == 0)
        def _():
            car[...] = jnp.zeros(car.shape, F32)

        uv = u_ref[...]
        dav = da_ref[...]
        hb = h_ref[...]
        hb16 = hb.astype(BF)
        dvv = d_ref[...]
        y = jnp.dot(hb16, c_ref[...], preferred_element_type=F32) + dvv * uv
        g1 = _gelu(y)
        g16 = g1.astype(BF)
        s = _sigmoid(jnp.dot(g16, w_ref[...], preferred_element_type=F32) + b_ref[...])
        dz = dav * g1 * s * (1.0 - s)
        dz16 = dz.astype(BF)
        dg1 = dav * s + lax.dot_general(dz16, w_ref[...], _DIMS['nt'], preferred_element_type=F32)
        _accumulate(dw_ref, first, lax.dot_general(g16, dz16, _DIMS['tn'], preferred_element_type=F32))
        _accumulate(dbg_ref, first, _colsum(dz))
        dy = dg1 * _gelu_grad(y)
        dy16 = dy.astype(BF)
        _accumulate(dd_ref, first, _colsum(dy * uv))
        _accumulate(dc_ref, first, lax.dot_general(hb16, dy16, _DIMS['tn'], preferred_element_type=F32))
        gs[...] = lax.dot_general(dy16, c_ref[...], _DIMS['nt'], preferred_element_type=F32)
        tables = _scan_tables(ab_ref[0:1, 0:NST], ab_ref[0:1, NST:2 * NST], True)
        _scan_block(gs, gs, car, tables, tm // 8, True)
        g = gs[...]
        g16b = g.astype(BF)
        sp = jnp.where(lax.rem(i * tm, seq) == 0, 0.0, 1.0)
        row = lax.broadcasted_iota(jnp.int32, hb.shape, 0)
        hprev = jnp.where(row == 0, hp_ref[7:8, :] * sp, pltpu.roll(hb, 1, 0))
        gr, gi = g[:, :NST], g[:, NST:]
        hr, hi = hprev[:, :NST], hprev[:, NST:]
        _accumulate(dab_ref.at[:, 0:NST], first, _colsum(gr * hr + gi * hi))
        _accumulate(dab_ref.at[:, NST:2 * NST], first, _colsum(gi * hr - gr * hi))
        _accumulate(dbb_ref, first, lax.dot_general(uv.astype(BF), g16b, _DIMS['tn'], preferred_element_type=F32))
        du = dy * dvv + lax.dot_general(g16b, bb_ref[...], _DIMS['nt'], preferred_element_type=F32)
        du_ref[...] = du.astype(BF)

    const = lambda shape: pl.BlockSpec(shape, lambda r: (0, 0))
    rowspec = lambda w: pl.BlockSpec((tm, w), lambda r: (blk(r), 0))
    return pl.pallas_call(
        body, name=name, grid=(nb,),
        in_specs=[rowspec(SSM_W), rowspec(SSM_W), rowspec(2 * NST),
                  pl.BlockSpec((8, 2 * NST), lambda r: (prev(blk(r)), 0)),
                  const((1, 2 * NST)), const((SSM_W, 2 * NST)), const((2 * NST, SSM_W)), const((1, SSM_W)),
                  const((SSM_W, SSM_W)), const((1, SSM_W))],
        out_specs=[rowspec(SSM_W), const((SSM_W, SSM_W)), const((1, SSM_W)), const((1, SSM_W)),
                   const((2 * NST, SSM_W)), const((SSM_W, 2 * NST)), const((1, 2 * NST))],
        out_shape=[jax.ShapeDtypeStruct((n, SSM_W), BF), jax.ShapeDtypeStruct((SSM_W, SSM_W), F32),
                   jax.ShapeDtypeStruct((1, SSM_W), F32), jax.ShapeDtypeStruct((1, SSM_W), F32),
                   jax.ShapeDtypeStruct((2 * NST, SSM_W), F32), jax.ShapeDtypeStruct((SSM_W, 2 * NST), F32),
                   jax.ShapeDtypeStruct((1, 2 * NST), F32)],
        scratch_shapes=[pltpu.VMEM((tm, 2 * NST), F32), pltpu.VMEM((8, 2 * NST), F32)],
        compiler_params=_cp(("arbitrary",)))(da, u, hst, hst, ab, bbt, cmat, dvec, wglu, bglu)


def _s5_rows(lam_re, lam_im, log_dt, b_re, b_im):
    rep = lambda a: jnp.broadcast_to(a[:, None, :], (SSM_G, SSM_H, SSM_P)).reshape(SSM_W, SSM_P)
    dt = jnp.broadcast_to(log_dt[:, None, None], (SSM_G, SSM_H, SSM_P)).reshape(SSM_W, SSM_P)
    tr = lambda b: b.transpose(0, 2, 1).reshape(SSM_W, SSM_P)
    return rep(lam_re), rep(lam_im), dt, tr(b_re), tr(b_im)


def _block_diag(rows_gp, inner):
    eye = jnp.eye(SSM_G, dtype=rows_gp.dtype)
    return (rows_gp[:, :, None, :] * eye[:, None, :, None]).reshape(SSM_G * inner, SSM_G * SSM_P)


def _diag_blocks(mat, inner):
    m4 = mat.reshape(SSM_G, inner, SSM_G, SSM_P)
    return jnp.stack([m4[g, :, g, :] for g in range(SSM_G)])


def _interleave(w, parts):
    lead = w.shape[:-1]
    nb = w.shape[-1] // (parts * FB)
    return jnp.swapaxes(w.reshape(lead + (parts, nb, FB)), -3, -2).reshape(w.shape)


def _deinterleave(w, parts):
    lead = w.shape[:-1]
    nb = w.shape[-1] // (parts * FB)
    return jnp.swapaxes(w.reshape(lead + (nb, parts, FB)), -3, -2).reshape(w.shape)


def _ffn_fwd(h, g, w_up, w_down, cw, cb, layer, seq, tag):
    n = h.shape[0]
    tm = min(1024, n)
    ni = n // tm
    f = _rmsnorm_fwd(h, g, f"{tag}_norm")
    up = _matmul_spec(
        f, w_up, 'nn', (NDEV, ni, 1),
        pl.BlockSpec((tm, D), lambda s, i, k: (i, 0)),
        pl.BlockSpec((D, FSH), lambda s, i, k: (layer * NDEV + s, 0)),
        pl.BlockSpec((tm, FSH), lambda s, i, k: (s * ni + i, 0)), (NDEV * n, FSH), f"{tag}_up")
    up = up.reshape(2, 4, n, FSH)
    act = _ffn_conv_fwd(up, cw, cb, seq, f"{tag}_conv")
    tn = 512
    out = _matmul_spec(
        act.reshape(4 * n, FSH), w_down, 'nn', (ni, D // tn, 4),
        pl.BlockSpec((tm, FSH), lambda i, j, k: (k * ni + i, 0)),
        pl.BlockSpec((FSH, tn), lambda i, j, k: (layer * 4 + k, j)),
        pl.BlockSpec((tm, tn), lambda i, j, k: (i, j)), (n, D), f"{tag}_down", resid=h)
    return out, (f, up, act)


def _ffn_bwd(dh, h, g, w_up, w_down, cw, cb, saved, layer, seq, tag):
    f, up, act = saved
    n = h.shape[0]
    tm = min(1024, n)
    ni = n // tm
    tk = min(1024, n)
    nk = n // tk
    dact = _matmul_spec(
        dh, w_down, 'nt', (4, ni, 1),
        pl.BlockSpec((tm, D), lambda j, i, k: (i, 0)),
        pl.BlockSpec((FSH, D), lambda j, i, k: (layer * 4 + j, 0)),
        pl.BlockSpec((tm, FSH), lambda j, i, k: (j * ni + i, 0)), (4 * n, FSH), f"{tag}_ddown_x")
    tn = 512
    dw_down = _matmul_spec(
        act.reshape(4 * n, FSH), dh, 'tn', (4, D // tn, nk),
        pl.BlockSpec((tk, FSH), lambda j, c, k: (j * nk + k, 0)),
        pl.BlockSpec((tk, tn), lambda j, c, k: (k, c)),
        pl.BlockSpec((FSH, tn), lambda j, c, k: (j, c)), (DFF, D), f"{tag}_ddown_w")
    dup, dcw, dcb = _ffn_conv_bwd(up, dact.reshape(4, n, FSH), cw, cb, seq, f"{tag}_dconv")
    dup2 = dup.reshape(NDEV * n, FSH)
    df = _matmul_spec(
        dup2, w_up, 'nt', (ni, 1, NDEV),
        pl.BlockSpec((tm, FSH), lambda i, j, k: (k * ni + i, 0)),
        pl.BlockSpec((D, FSH), lambda i, j, k: (layer * NDEV + k, 0)),
        pl.BlockSpec((tm, D), lambda i, j, k: (i, 0)), (n, D), f"{tag}_dup_x")
    dw_up = _matmul_spec(
        f, dup2, 'tn', (NDEV, 1, nk),
        pl.BlockSpec((tk, D), lambda s, j, k: (k, 0)),
        pl.BlockSpec((tk, FSH), lambda s, j, k: (s * nk + k, 0)),
        pl.BlockSpec((D, FSH), lambda s, j, k: (s, 0)), (NDEV * D, FSH), f"{tag}_dup_w")
    dh_in, dg = _rmsnorm_bwd(h, g, df, dh, f"{tag}_dnorm")
    grads = dict(g=dg, w_up=dw_up.reshape(NDEV, D, FSH), w_down=dw_down.reshape(NDEV, DFF // NDEV, D),
                 cw=dcw.reshape(NDEV, 3, FSH), cb=dcb.reshape(2 * DFF))
    return dh_in, grads


def _col_shards(w, width):
    return w.reshape(w.shape[0], NDEV, width).transpose(1, 0, 2)


def _local_step(x, tgt, w, gw, seq):
    bf = lambda a: a.astype(BF)
    row = lambda a: a.reshape(1, -1).astype(F32)
    w_ev = gw['ev_w_in'].transpose(1, 0, 2).reshape(D, 1792)
    w_ev_s5, w_ev_gm = w_ev[:, :SSM_W], w_ev[:, SSM_W:]
    w_evo = gw['ev_w_out'].reshape(D, D)
    w_od = _interleave(gw['od_w_in'].transpose(1, 0, 2).reshape(D, 3 * D), 3)
    w_odo = gw['od_w_out'].reshape(D, D)
    od_cw = gw['od_conv_w'].transpose(1, 0, 2).reshape(3, D)
    od_cb = gw['od_conv_b'].reshape(1, D)
    w_up = gw['ffn_w_up'].reshape(2 * NDEV * D, FSH)
    w_dn = gw['ffn_w_down'].reshape(2 * DFF, D)
    f_cw = [gw['ffn_conv_w'][l].reshape(2, 4, 3, FSH) for l in range(2)]
    f_cb = [w['ffn_conv_b'][l].reshape(2, 4, 1, FSH) for l in range(2)]
    tril = jnp.tril(jnp.ones((CHUNK, CHUNK), dtype=bool))
    gm_w = jnp.where(tril, w['gm_w_s'][0], 0.0)
    gm_wm, gm_wmt = bf(gm_w), bf(jnp.swapaxes(gm_w, 1, 2))
    gm_bt = w['gm_b_s'][0].T
    gm_gv = row(w['gm_v_g'][0])
    s5_in = _s5_rows(w['s5_lam_re'][0], w['s5_lam_im'][0], w['s5_log_dt'][0], w['s5_b_re'][0], w['s5_b_im'][0])
    ab_re, ab_im, bb_re, bb_im = _s5_disc_fwd(s5_in, "s5_disc")
    first_h = lambda a: a.reshape(SSM_G, SSM_H, SSM_P)[:, 0, :].reshape(1, NST)
    s5_ab = jnp.concatenate([first_h(ab_re), first_h(ab_im)], axis=1)
    to_gp = lambda a: a.reshape(SSM_G, SSM_H, SSM_P)
    s5_bbt = bf(jnp.concatenate([_block_diag(to_gp(bb_re), SSM_H), _block_diag(to_gp(bb_im), SSM_H)], axis=1))
    s5_cmat = bf(jnp.concatenate([_block_diag(w['s5_c_re'][0], SSM_H).T, -_block_diag(w['s5_c_im'][0], SSM_H).T],
                                 axis=0))
    s5_d, s5_bg, s5_wg = row(w['s5_d'][0]), row(w['s5_b_glu'][0]), gw['s5_w_glu'].reshape(SSM_W, SSM_W)
    g_mix = [row(w['mix_norm_g'][l]) for l in range(2)]
    g_ffn = [row(w['ffn_norm_g'][l]) for l in range(2)]
    g_fin = row(w['final_norm_g'])

    h0 = x
    y0 = _rmsnorm_fwd(h0, g_mix[0], "ev_norm")
    p_s5 = _matmul(y0, w_ev_s5, 'nn', 1024, 256, D, "ev_in_s5")
    p_gm = _matmul(y0, w_ev_gm, 'nn', 1024, 512, D, "ev_in_gm")
    hst, a_out = _s5_fwd(p_s5, s5_ab, s5_bbt, s5_cmat, s5_d, s5_wg, s5_bg, seq, "s5_fwd")
    b_out = _gmlp_fwd(p_gm, gm_wm, gm_bt, gm_gv, seq, "gmlp_fwd")
    mixcat = jnp.concatenate([a_out, b_out], axis=1)
    h1 = _matmul(mixcat, w_evo, 'nn', 1024, 512, D, "ev_out", resid=h0)
    h2, ffn0 = _ffn_fwd(h1, g_ffn[0], w_up, w_dn, f_cw[0], f_cb[0], 0, seq, "ffn0")
    y1 = _rmsnorm_fwd(h2, g_mix[1], "od_norm")
    p_od = _matmul(y1, w_od, 'nn', 1024, 512, D, "od_in")
    mixin = _shortconv_fwd(p_od, od_cw, od_cb, seq, "od_conv")
    h3 = _matmul(mixin, w_odo, 'nn', 1024, 512, D, "od_out", resid=h2)
    h4, ffn1 = _ffn_fwd(h3, g_ffn[1], w_up, w_dn, f_cw[1], f_cb[1], 1, seq, "ffn1")
    loss, dh4, dg_fin = _final_loss(h4, g_fin, tgt, "final_loss")

    dh3, gf1 = _ffn_bwd(dh4, h3, g_ffn[1], w_up, w_dn, f_cw[1], f_cb[1], ffn1, 1, seq, "ffn1")
    dmixin = _matmul(dh3, w_odo, 'nt', 1024, 512, D, "od_dout_x")
    dw_odo = _matmul(mixin, dh3, 'tn', D, 512, 1024, "od_dout_w")
    dp_od, d_od_cw, d_od_cb = _shortconv_bwd(p_od, dmixin, od_cw, od_cb, seq, "od_dconv")
    dy1 = _matmul(dp_od, w_od, 'nt', 1024, D, 512, "od_din_x")
    dw_od = _matmul(y1, dp_od, 'tn', D, 512, 1024, "od_din_w")
    dh2, dg_mix1 = _rmsnorm_bwd(h2, g_mix[1], dy1, dh3, "od_dnorm")
    dh1, gf0 = _ffn_bwd(dh2, h1, g_ffn[0], w_up, w_dn, f_cw[0], f_cb[0], ffn0, 0, seq, "ffn0")
    dmix_a = _matmul(dh1, w_evo[:SSM_W], 'nt', 1024, SSM_W, D, "ev_dout_xa")
    dmix_b = _matmul(dh1, w_evo[SSM_W:], 'nt', 1024, GM_W, D, "ev_dout_xb")
    dw_evo = _matmul(mixcat, dh1, 'tn', D, 512, 1024, "ev_dout_w")
    dp_s5, d_wg, d_bg, d_d, d_cmat, d_bbt, d_ab = _s5_bwd(dmix_a, p_s5, hst, s5_ab, s5_bbt, s5_cmat, s5_d, s5_wg,
                                                           s5_bg, seq, "s5_bwd")
    dp_gm, d_gmw, d_gmb, d_gmg = _gmlp_bwd(p_gm, dmix_b, gm_wm, gm_wmt, gm_bt, gm_gv, seq, "gmlp_bwd")
    dy0 = _matmul(dp_gm, w_ev_gm, 'nt', 1024, D, 512, "ev_din_xb")
    dy0 = _matmul(dp_s5, w_ev_s5, 'nt', 1024, D, SSM_W, "ev_din_xa", resid=dy0)
    dw_ev = jnp.concatenate([_matmul(y0, dp_s5, 'tn', D, SSM_W, 1024, "ev_din_wa"),
                             _matmul(y0, dp_gm, 'tn', D, 512, 1024, "ev_din_wb")], axis=1)
    grad_x, dg_mix0 = _rmsnorm_bwd(h0, g_mix[0], dy0, dh1, "ev_dnorm")

    put_h0 = lambda a: jnp.zeros((SSM_G, SSM_H, SSM_P), F32).at[:, 0, :].set(a.reshape(SSM_G, SSM_P)).reshape(
        SSM_W, SSM_P)
    ct = (put_h0(d_ab[:, :NST]), put_h0(d_ab[:, NST:]),
          _diag_blocks(d_bbt[:, :NST], SSM_H).reshape(SSM_W, SSM_P),
          _diag_blocks(d_bbt[:, NST:], SSM_H).reshape(SSM_W, SSM_P))
    d_lre, d_lim, d_ldt, d_bre, d_bim = _s5_disc_bwd(s5_in, ct, "s5_ddisc")
    over_h = lambda a: a.reshape(SSM_G, SSM_H, SSM_P).sum(axis=1)
    un_tr = lambda a: a.reshape(SSM_G, SSM_H, SSM_P).transpose(0, 2, 1)
    d_cre = _diag_blocks(d_cmat[:NST].T, SSM_H)
    d_cim = -_diag_blocks(d_cmat[NST:].T, SSM_H)

    repl = {
        'mix_norm_g': jnp.concatenate([dg_mix0, dg_mix1], axis=0),
        'ffn_norm_g': jnp.concatenate([gf0['g'], gf1['g']], axis=0),
        'final_norm_g': dg_fin.reshape(D),
        's5_lam_re': over_h(d_lre)[None], 's5_lam_im': over_h(d_lim)[None],
        's5_log_dt': over_h(d_ldt).sum(axis=1)[None],
        's5_b_re': un_tr(d_bre)[None], 's5_b_im': un_tr(d_bim)[None],
        's5_c_re': d_cre[None], 's5_c_im': d_cim[None],
        's5_d': d_d, 's5_b_glu': d_bg,
        'gm_w_s': d_gmw[None], 'gm_b_s': d_gmb.reshape(1, GM_HEADS, CHUNK), 'gm_v_g': d_gmg,
        'ffn_conv_b': jnp.stack([gf0['cb'], gf1['cb']]),
    }
    sharded = {
        'ev_w_in': _col_shards(dw_ev, 224), 'ev_w_out': dw_evo.reshape(NDEV, D // NDEV, D),
        's5_w_glu': d_wg.reshape(NDEV, SSM_W // NDEV, SSM_W),
        'od_w_in': _col_shards(_deinterleave(dw_od, 3), 384), 'od_conv_w': _col_shards(d_od_cw, D // NDEV),
        'od_conv_b': d_od_cb.reshape(NDEV, 1, D // NDEV), 'od_w_out': dw_odo.reshape(NDEV, D // NDEV, D),
        'ffn_w_up0': gf0['w_up'], 'ffn_w_up1': gf1['w_up'], 'ffn_conv_w0': gf0['cw'], 'ffn_conv_w1': gf1['cw'],
        'ffn_w_down0': gf0['w_down'], 'ffn_w_down1': gf1['w_down'],
    }
    return loss, grad_x, repl, sharded


HBM_SPEC = pl.BlockSpec(memory_space=pltpu.HBM)


def _at_axis(ref, pos, index):
    return ref.at[(slice(None),) * pos + (index,)]


def _all_gather(shards, positions, name):
    n = len(shards)

    def body(*refs):
        xs, outs = refs[:n], refs[n:2 * n]
        send_sems, recv_sems, local_sems = refs[2 * n:]
        x, y, c = lax.axis_index("x"), lax.axis_index("y"), lax.axis_index("c")
        me, sibling = (x, y, c), (x, y, 1 - c)
        chips = [(1 - x, y), (x, 1 - y), (1 - x, 1 - y)]

        def block(p, dev):
            return _at_axis(outs[p], positions[p], 4 * dev[0] + 2 * dev[1] + dev[2])

        def copy(p, k, dev, to, src=None):
            return pltpu.make_async_remote_copy(
                src_ref=block(p, dev) if src is None else src, dst_ref=block(p, dev),
                send_sem=send_sems.at[p, k], recv_sem=recv_sems.at[p, k], device_id=to, device_id_type=MESH_T)

        mine = [pltpu.make_async_copy(xs[p], block(p, me), local_sems.at[p]) for p in range(n)]
        for cp in mine:
            cp.start()
        first = [copy(p, 0, me, sibling, src=xs[p]) for p in range(n)]
        first += [copy(p, 1 + j, me, (*chip, c), src=xs[p]) for j, chip in enumerate(chips) for p in range(n)]
        for cp in first:
            cp.start()
        passed = []
        for j, chip in enumerate(chips):
            for p in range(n):
                copy(p, 1 + j, (*chip, c), me).wait_recv()
                fwd = copy(p, 4 + j, (*chip, c), sibling)
                fwd.start()
                passed.append(fwd)
        for p in range(n):
            copy(p, 0, sibling, me).wait_recv()
        for j, chip in enumerate(chips):
            for p in range(n):
                copy(p, 4 + j, (*chip, 1 - c), me).wait_recv()
        for cp in first + passed:
            cp.wait_send()
        for cp in mine:
            cp.wait()

    out_shape = [jax.ShapeDtypeStruct(s.shape[:pos] + (NDEV,) + s.shape[pos:], s.dtype)
                 for s, pos in zip(shards, positions)]
    return pl.pallas_call(
        body, name=name, out_shape=out_shape, in_specs=[HBM_SPEC] * n, out_specs=[HBM_SPEC] * n,
        scratch_shapes=[pltpu.SemaphoreType.DMA((n, 7)), pltpu.SemaphoreType.DMA((n, 7)),
                        pltpu.SemaphoreType.DMA((n,))])(*shards)


def _exchange_sibling(parts, name):
    n = len(parts)

    def body(*refs):
        gs, rs = refs[:n], refs[n:2 * n]
        send_sems, recv_sems = refs[2 * n:]
        x, y, c = lax.axis_index("x"), lax.axis_index("y"), lax.axis_index("c")
        copies = [pltpu.make_async_remote_copy(
            src_ref=gs[p].at[2 * chip + (1 - c)], dst_ref=rs[p].at[chip], send_sem=send_sems.at[p, chip],
            recv_sem=recv_sems.at[p, chip], device_id=(x, y, 1 - c), device_id_type=MESH_T)
            for p in range(n) for chip in range(4)]
        for cp in copies:
            cp.start()
        for cp in copies:
            cp.wait()

    return pl.pallas_call(
        body, name=name, out_shape=[jax.ShapeDtypeStruct((4,) + g.shape[1:], g.dtype) for g in parts],
        in_specs=[HBM_SPEC] * n, out_specs=[HBM_SPEC] * n,
        scratch_shapes=[pltpu.SemaphoreType.DMA((n, 4)), pltpu.SemaphoreType.DMA((n, 4))])(*parts)


def _exchange_chips(sums, name):
    n = len(sums)

    def body(*refs):
        ss, rs = refs[:n], refs[n:2 * n]
        send_sems, recv_sems, local_sems = refs[2 * n:]
        x, y, c = lax.axis_index("x"), lax.axis_index("y"), lax.axis_index("c")
        my_chip = 2 * x + y
        chips = [(1 - x, y), (x, 1 - y), (1 - x, 1 - y)]
        mine = [pltpu.make_async_copy(ss[p].at[my_chip], rs[p].at[my_chip], local_sems.at[p]) for p in range(n)]
        for cp in mine:
            cp.start()

        def copy(p, k, chip, src_slot, dst_slot):
            return pltpu.make_async_remote_copy(
                src_ref=ss[p].at[src_slot], dst_ref=rs[p].at[dst_slot], send_sem=send_sems.at[p, k],
                recv_sem=recv_sems.at[p, k], device_id=(*chip, c), device_id_type=MESH_T)

        sent = [copy(p, k, chip, 2 * chip[0] + chip[1], my_chip) for k, chip in enumerate(chips) for p in range(n)]
        for cp in sent:
            cp.start()
        for k, chip in enumerate(chips):
            slot = 2 * chip[0] + chip[1]
            for p in range(n):
                copy(p, k, chip, slot, slot).wait_recv()
        for cp in sent:
            cp.wait_send()
        for cp in mine:
            cp.wait()

    return pl.pallas_call(
        body, name=name, out_shape=[jax.ShapeDtypeStruct(s.shape, s.dtype) for s in sums],
        in_specs=[HBM_SPEC] * n, out_specs=[HBM_SPEC] * n,
        scratch_shapes=[pltpu.SemaphoreType.DMA((n, 3)), pltpu.SemaphoreType.DMA((n, 3)),
                        pltpu.SemaphoreType.DMA((n,))])(*sums)


def _row_block(rows, cols, itemsize=4, target=2**20):
    best = None
    for tr in range(16, rows + 1, 16):
        if rows % tr == 0 and tr * cols * itemsize <= target:
            best = tr
    return best or rows


def _as_rows(a, lead):
    return a.reshape(a.shape[:lead] + (-1, a.shape[-1]))


def _add_sibling(parts, got, core, name):
    _, r, c_ = parts.shape
    tr = _row_block(r, c_)

    def body(core_ref, a_ref, b_ref, o_ref):
        o_ref[...] = (a_ref[...] + b_ref[...]).astype(BF)

    return pl.pallas_call(
        body, name=name, out_shape=jax.ShapeDtypeStruct((4, r, c_), BF),
        grid_spec=pltpu.PrefetchScalarGridSpec(
            num_scalar_prefetch=1, grid=(4, r // tr),
            in_specs=[pl.BlockSpec((None, tr, c_), lambda ch, i, core_ref: (2 * ch + core_ref[0], i, 0)),
                      pl.BlockSpec((None, tr, c_), lambda ch, i, core_ref: (ch, i, 0))],
            out_specs=pl.BlockSpec((None, tr, c_), lambda ch, i, core_ref: (ch, i, 0))),
        compiler_params=_cp(("parallel", "parallel")))(core, parts, got)


def _adamw(w, m, v, gparts, name):
    parts, rows, cols = gparts.shape
    tr = _row_block(rows, cols, target=2**19)
    bc1 = 1.0 - ADAM_B1 ** ADAM_STEP
    bc2 = 1.0 - ADAM_B2 ** ADAM_STEP

    def body(w_ref, m_ref, v_ref, g_ref, go_ref, d_ref, mo_ref, vo_ref):
        g = g_ref[0].astype(F32)
        for k in range(1, parts):
            g = g + g_ref[k].astype(F32)
        mn = ADAM_B1 * m_ref[...] + (1.0 - ADAM_B1) * g
        vn = ADAM_B2 * v_ref[...] + (1.0 - ADAM_B2) * (g * g)
        go_ref[...] = g
        mo_ref[...] = mn
        vo_ref[...] = vn
        d_ref[...] = -ADAM_LR * ((mn / bc1) / (jnp.sqrt(vn / bc2) + ADAM_EPS) + ADAM_WD * w_ref[...])

    blk = pl.BlockSpec((tr, cols), lambda i: (i, 0))
    shp = jax.ShapeDtypeStruct((rows, cols), F32)
    return pl.pallas_call(
        body, name=name, grid=(rows // tr,),
        in_specs=[blk, blk, blk, pl.BlockSpec((parts, tr, cols), lambda i: (0, i, 0))],
        out_specs=[blk] * 4, out_shape=[shp] * 4, compiler_params=_cp(("parallel",)))(w, m, v, gparts)


def _pack(arrays, rows):
    flat = jnp.concatenate([a.reshape(-1).astype(F32) for a in arrays])
    return jnp.pad(flat, (0, rows * PACK_COLS - flat.shape[0])).reshape(rows, PACK_COLS)


def _unpack(buf, shapes):
    flat = buf.reshape(-1)
    out, off = [], 0
    for shp in shapes:
        size = int(np.prod(shp))
        out.append(flat[off:off + size].reshape(shp))
        off += size
    return out


REPL_SHAPES = {'mix_norm_g': (2, 1024), 'ffn_norm_g': (2, 1024), 'final_norm_g': (1024,), 's5_lam_re': (1, 16, 64),
               's5_lam_im': (1, 16, 64), 's5_log_dt': (1, 16), 's5_b_re': (1, 16, 64, 16), 's5_b_im': (1, 16, 64, 16),
               's5_c_re': (1, 16, 16, 64), 's5_c_im': (1, 16, 16, 64), 's5_d': (1, 256), 's5_b_glu': (1, 256),
               'gm_w_s': (1, 6, 128, 128), 'gm_b_s': (1, 6, 128), 'gm_v_g': (1, 768), 'ffn_conv_b': (2, 5632)}
REPL_ELEMS = sum(int(np.prod(REPL_SHAPES[n])) for n in REPL_ORDER)
REPL_ROWS = -(-REPL_ELEMS // (PACK_COLS * 8)) * 8

GATHER_PLAN = {'ev_w_in': (0, BF), 'ev_w_out': (0, BF), 's5_w_glu': (0, BF), 'od_w_in': (0, BF), 'od_conv_w': (0, F32),
               'od_conv_b': (0, F32), 'od_w_out': (0, BF), 'ffn_w_up': (1, BF), 'ffn_conv_w': (1, F32),
               'ffn_w_down': (1, BF)}
GRAD_PIECES = [('ev_w_in', 'ev_w_in', None), ('ev_w_out', 'ev_w_out', None), ('s5_w_glu', 's5_w_glu', None),
               ('od_w_in', 'od_w_in', None), ('od_conv_w', 'od_conv_w', None), ('od_conv_b', 'od_conv_b', None),
               ('od_w_out', 'od_w_out', None), ('ffn_w_up0', 'ffn_w_up', 0), ('ffn_w_up1', 'ffn_w_up', 1),
               ('ffn_conv_w0', 'ffn_conv_w', 0), ('ffn_conv_w1', 'ffn_conv_w', 1), ('ffn_w_down0', 'ffn_w_down', 0),
               ('ffn_w_down1', 'ffn_w_down', 1)]


def _squeeze_lead(a):
    return a.reshape(a.shape[1:]) if a.shape[0] == 1 and a.ndim > 2 else a


def kernel(x, mix_norm_g, ffn_norm_g, final_norm_g, ev_w_in, ev_w_out, s5_lam_re, s5_lam_im, s5_log_dt, s5_b_re, s5_b_im, s5_c_re, s5_c_im, s5_d, s5_w_glu, s5_b_glu, gm_w_s, gm_b_s, gm_v_g, od_w_in, od_conv_w, od_conv_b, od_w_out, ffn_w_up, ffn_conv_w, ffn_conv_b, ffn_w_down, loss_target, m_mix_norm_g, m_ffn_norm_g, m_final_norm_g, m_ev_w_in, m_ev_w_out, m_s5_lam_re, m_s5_lam_im, m_s5_log_dt, m_s5_b_re, m_s5_b_im, m_s5_c_re, m_s5_c_im, m_s5_d, m_s5_w_glu, m_s5_b_glu, m_gm_w_s, m_gm_b_s, m_gm_v_g, m_od_w_in, m_od_conv_w, m_od_conv_b, m_od_w_out, m_ffn_w_up, m_ffn_conv_w, m_ffn_conv_b, m_ffn_w_down, v_mix_norm_g, v_ffn_norm_g, v_final_norm_g, v_ev_w_in, v_ev_w_out, v_s5_lam_re, v_s5_lam_im, v_s5_log_dt, v_s5_b_re, v_s5_b_im, v_s5_c_re, v_s5_c_im, v_s5_d, v_s5_w_glu, v_s5_b_glu, v_gm_w_s, v_gm_b_s, v_gm_v_g, v_od_w_in, v_od_conv_w, v_od_conv_b, v_od_w_out, v_ffn_w_up, v_ffn_conv_w, v_ffn_conv_b, v_ffn_w_down):
    given = dict(locals())
    weights = {n: given[n] for n in WEIGHT_ORDER}
    nseq, seq, _ = x.shape

    shards, positions = [], []
    for name in SHARDED_ORDER:
        pos, dtype = GATHER_PLAN[name]
        a = weights[name] if pos else _squeeze_lead(weights[name])
        shards.append(a.astype(dtype))
        positions.append(pos)
    gathered = dict(zip(SHARDED_ORDER, _all_gather(shards, positions, "gather_weights")))

    loss_row, grad_x, g_repl, g_shard = _local_step(
        x.reshape(nseq * seq, D), loss_target.reshape(nseq * seq, D), weights, gathered, seq)
    loss = lax.psum(loss_row[0, 0], ("x", "y", "c"))

    core = lax.axis_index("c").astype(jnp.int32).reshape(1)
    pieces = [g_shard[p[0]] for p in GRAD_PIECES]
    from_sibling = _exchange_sibling(pieces, "reduce_sibling")
    chip_sums = [_add_sibling(_as_rows(a, 1), _as_rows(b, 1), core, f"reduce_add_{p[0]}").reshape((4,) + a.shape[1:])
                 for p, a, b in zip(GRAD_PIECES, pieces, from_sibling)]
    by_chip = dict(zip([p[0] for p in GRAD_PIECES], _exchange_chips(chip_sums, "reduce_chips")))
    repl_parts = _all_gather([_pack([g_repl[n] for n in REPL_ORDER], REPL_ROWS)], [0], "gather_small_grads")[0]

    out = {}
    for name in SHARDED_ORDER:
        w = weights[name]
        if name + '0' in by_chip:
            gp = jnp.stack([by_chip[name + '0'], by_chip[name + '1']], axis=1)
        else:
            gp = by_chip[name]
        to_rows = lambda a: a.reshape(-1, w.shape[-1])
        res = _adamw(to_rows(w), to_rows(given["m_" + name]), to_rows(given["v_" + name]),
                     gp.reshape(4, -1, w.shape[-1]), f"adamw_{name}")
        out[name] = [r.reshape(w.shape) for r in res]
    rp = _adamw(_pack([weights[n] for n in REPL_ORDER], REPL_ROWS),
                _pack([given["m_" + n] for n in REPL_ORDER], REPL_ROWS),
                _pack([given["v_" + n] for n in REPL_ORDER], REPL_ROWS), repl_parts, "adamw_replicated")
    rp_shapes = [weights[n].shape for n in REPL_ORDER]
    for k in range(4):
        for name, a in zip(REPL_ORDER, _unpack(rp[k], rp_shapes)):
            out.setdefault(name, [None] * 4)[k] = a
    results = [[out[n][k] for n in WEIGHT_ORDER] for k in range(4)]
    grad_w, delta_w, new_m, new_v = results
    return (loss, grad_x.reshape(nseq, seq, D), *grad_w, *delta_w, *new_m, *new_v)
```

```python
import math

import jax
import jax.numpy as jnp
import numpy as np
from jax import lax
from jax.experimental import pallas as pl
from jax.experimental.pallas import tpu as pltpu

F32 = jnp.float32
BF = jnp.bfloat16

D = 1024
DFF = 2816
NDEV = 8
SSM_W = 256
SSM_G = 16
SSM_H = 16
SSM_P = 64
NST = SSM_G * SSM_P
GM_W = 768
GM_HEADS = 6
CHUNK = 128
EPS = 1e-6
LAM_MAX = -1e-4
FB = 256
FSH = 2 * DFF // NDEV
VMEM_LIMIT = 48 * 2**20
PACK_COLS = 1024
MESH_T = pl.DeviceIdType.MESH

ADAM_LR = 0.001
ADAM_B1 = 0.9
ADAM_B2 = 0.999
ADAM_EPS = 1e-08
ADAM_WD = 0.01
ADAM_STEP = 10

WEIGHT_ORDER = ['mix_norm_g', 'ffn_norm_g', 'final_norm_g', 'ev_w_in', 'ev_w_out', 's5_lam_re', 's5_lam_im',
                's5_log_dt', 's5_b_re', 's5_b_im', 's5_c_re', 's5_c_im', 's5_d', 's5_w_glu', 's5_b_glu', 'gm_w_s',
                'gm_b_s', 'gm_v_g', 'od_w_in', 'od_conv_w', 'od_conv_b', 'od_w_out', 'ffn_w_up', 'ffn_conv_w',
                'ffn_conv_b', 'ffn_w_down']
SHARDED = {'ev_w_in': ((1, 1024, 1792), 2), 'ev_w_out': ((1, 1024, 1024), 1), 's5_w_glu': ((1, 256, 256), 1),
           'od_w_in': ((1, 1024, 3072), 2), 'od_conv_w': ((1, 3, 1024), 2), 'od_conv_b': ((1, 1024), 1),
           'od_w_out': ((1, 1024, 1024), 1), 'ffn_w_up': ((2, 1024, 5632), 2), 'ffn_conv_w': ((2, 3, 5632), 2),
           'ffn_w_down': ((2, 2816, 1024), 1)}
SHARDED_ORDER = [n for n in WEIGHT_ORDER if n in SHARDED]
REPL_ORDER = [n for n in WEIGHT_ORDER if n not in SHARDED]


def _cp(sem):
    return pltpu.CompilerParams(dimension_semantics=sem, vmem_limit_bytes=VMEM_LIMIT)


def _sigmoid(x):
    return 1.0 / (1.0 + jnp.exp(-x))


_GELU_K = math.sqrt(2.0 / math.pi)


def _gelu(x):
    return 0.5 * x * (1.0 + jnp.tanh(_GELU_K * (x + 0.044715 * x * x * x)))


def _gelu_grad(x):
    t = jnp.tanh(_GELU_K * (x + 0.044715 * x * x * x))
    return 0.5 * (1.0 + t) + 0.5 * x * (1.0 - t * t) * _GELU_K * (1.0 + 3.0 * 0.044715 * x * x)


def _colsum(x):
    return jnp.sum(x, axis=0, keepdims=True)


def _accumulate(ref, first, part):
    @pl.when(first)
    def _():
        ref[...] = part

    @pl.when(jnp.logical_not(first))
    def _():
        ref[...] += part


_DIMS = {'nn': (((1,), (0,)), ((), ())), 'nt': (((1,), (1,)), ((), ())), 'tn': (((0,), (0,)), ((), ()))}


def _matmul(a, b, mode, tm, tn, tk, name, resid=None, out_dtype=F32):
    if mode == 'tn':
        kdim, m = a.shape
    else:
        m, kdim = a.shape
    n = b.shape[0] if mode == 'nt' else b.shape[1]
    tm, tn, tk = min(tm, m), min(tn, n), min(tk, kdim)
    assert m % tm == 0 and n % tn == 0 and kdim % tk == 0, (name, m, n, kdim, tm, tn, tk)
    a_spec = (pl.BlockSpec((tk, tm), lambda i, j, k: (k, i)) if mode == 'tn'
              else pl.BlockSpec((tm, tk), lambda i, j, k: (i, k)))
    b_spec = (pl.BlockSpec((tn, tk), lambda i, j, k: (j, k)) if mode == 'nt'
              else pl.BlockSpec((tk, tn), lambda i, j, k: (k, j)))
    o_spec = pl.BlockSpec((tm, tn), lambda i, j, k: (i, j))
    return _matmul_spec(a, b, mode, (m // tm, n // tn, kdim // tk), a_spec, b_spec, o_spec, (m, n), name,
                        resid=resid, out_dtype=out_dtype)


def _matmul_spec(a, b, mode, grid, a_spec, b_spec, o_spec, out_shape, name, resid=None, out_dtype=F32):
    nk = grid[2]
    tm, tn = o_spec.block_shape[-2:]
    dims = _DIMS[mode]
    has_resid = resid is not None

    def body(*refs):
        if has_resid:
            a_ref, b_ref, r_ref, o_ref = refs[:4]
        else:
            a_ref, b_ref, o_ref = refs[:3]
            r_ref = None
        part = lax.dot_general(a_ref[...].astype(BF), b_ref[...].astype(BF), dims, preferred_element_type=F32)
        if nk == 1:
            if has_resid:
                part = part + r_ref[...]
            o_ref[...] = part.astype(out_dtype)
        else:
            acc = refs[-1]
            k = pl.program_id(2)

            @pl.when(k == 0)
            def _():
                acc[...] = part

            @pl.when(k > 0)
            def _():
                acc[...] += part

            @pl.when(k == nk - 1)
            def _():
                tot = acc[...]
                if has_resid:
                    tot = tot + r_ref[...]
                o_ref[...] = tot.astype(out_dtype)

    operands = [a, b] + ([resid] if has_resid else [])
    in_specs = [a_spec, b_spec] + ([o_spec] if has_resid else [])
    return pl.pallas_call(
        body, name=name, grid=grid, in_specs=in_specs, out_specs=o_spec,
        out_shape=jax.ShapeDtypeStruct(out_shape, out_dtype),
        scratch_shapes=[pltpu.VMEM((tm, tn), F32)] if nk > 1 else [],
        compiler_params=_cp(("parallel", "parallel", "arbitrary")))(*operands)


def _rmsnorm_fwd(x, g, name):
    n = x.shape[0]
    tm = min(512, n)

    def body(x_ref, g_ref, o_ref):
        xv = x_ref[...]
        r = lax.rsqrt(jnp.mean(xv * xv, axis=-1, keepdims=True) + EPS)
        o_ref[...] = (xv * r * g_ref[...]).astype(BF)

    return pl.pallas_call(
        body, name=name, grid=(n // tm,),
        in_specs=[pl.BlockSpec((tm, D), lambda i: (i, 0)), pl.BlockSpec((1, D), lambda i: (0, 0))],
        out_specs=pl.BlockSpec((tm, D), lambda i: (i, 0)),
        out_shape=jax.ShapeDtypeStruct((n, D), BF), compiler_params=_cp(("parallel",)))(x, g)


def _rmsnorm_bwd(x, g, dy, dres, name):
    n = x.shape[0]
    tm = min(512, n)

    def body(x_ref, g_ref, dy_ref, dr_ref, dx_ref, dg_ref):
        xv = x_ref[...]
        r = lax.rsqrt(jnp.mean(xv * xv, axis=-1, keepdims=True) + EPS)
        xh = xv * r
        dyv = dy_ref[...]
        dyg = dyv * g_ref[...]
        dx_ref[...] = dr_ref[...] + r * (dyg - xh * jnp.mean(dyg * xh, axis=-1, keepdims=True))
        _accumulate(dg_ref, pl.program_id(0) == 0, _colsum(dyv * xh))

    row = pl.BlockSpec((tm, D), lambda i: (i, 0))
    vec = pl.BlockSpec((1, D), lambda i: (0, 0))
    return pl.pallas_call(
        body, name=name, grid=(n // tm,), in_specs=[row, vec, row, row], out_specs=[row, vec],
        out_shape=[jax.ShapeDtypeStruct((n, D), F32), jax.ShapeDtypeStruct((1, D), F32)],
        compiler_params=_cp(("arbitrary",)))(x, g, dy, dres)


def _final_loss(h, g, tgt, name):
    n = h.shape[0]
    tm = min(512, n)

    def body(x_ref, g_ref, t_ref, loss_ref, dx_ref, dg_ref):
        first = pl.program_id(0) == 0
        xv = x_ref[...]
        gv = g_ref[...]
        r = lax.rsqrt(jnp.mean(xv * xv, axis=-1, keepdims=True) + EPS)
        xh = xv * r
        err = xh * gv - t_ref[...]
        part = 0.5 * jnp.sum(jnp.mean(err * err, axis=-1, keepdims=True), axis=0, keepdims=True)
        _accumulate(loss_ref, first, jnp.broadcast_to(part, (1, 128)))
        dyv = err * (1.0 / D)
        dyg = dyv * gv
        dx_ref[...] = r * (dyg - xh * jnp.mean(dyg * xh, axis=-1, keepdims=True))
        _accumulate(dg_ref, first, _colsum(dyv * xh))

    row = pl.BlockSpec((tm, D), lambda i: (i, 0))
    vec = pl.BlockSpec((1, D), lambda i: (0, 0))
    return pl.pallas_call(
        body, name=name, grid=(n // tm,), in_specs=[row, vec, row],
        out_specs=[pl.BlockSpec((1, 128), lambda i: (0, 0)), row, vec],
        out_shape=[jax.ShapeDtypeStruct((1, 128), F32), jax.ShapeDtypeStruct((n, D), F32),
                   jax.ShapeDtypeStruct((1, D), F32)],
        compiler_params=_cp(("arbitrary",)))(h, g, tgt)


def _prev_rows(x, halo_ref, lanes, scale, row):
    h7 = halo_ref[7:8, lanes] * scale
    h6 = halo_ref[6:7, lanes] * scale
    p1 = jnp.where(row == 0, h7, pltpu.roll(x, 1, 0))
    p2 = jnp.where(row == 0, h6, jnp.where(row == 1, h7, pltpu.roll(x, 2, 0)))
    return p1, p2


def _halo_maps(tm, n_rows):
    r8 = tm // 8
    last = n_rows // 8 - 1
    prev = lambda i: jnp.maximum(i * r8 - 1, 0)
    nxt = lambda i: jnp.minimum((i + 1) * r8, last)
    return prev, nxt


def _ffn_conv_fwd(up, cw, cb, seq, name):
    n = up.shape[2]
    tm = min(256, seq)
    prev, _ = _halo_maps(tm, n)

    def body(u_ref, h_ref, w_ref, b_ref, o_ref):
        i = pl.program_id(1)
        scale = jnp.where(lax.rem(i * tm, seq) == 0, 0.0, 1.0)
        row = lax.broadcasted_iota(jnp.int32, (tm, FSH), 0)
        hc = []
        for g in range(2):
            x = u_ref[g]
            p1, p2 = _prev_rows(x, h_ref.at[g], slice(None), scale, row)
            hc.append(b_ref[g] + w_ref[g, 0:1, :] * p2 + w_ref[g, 1:2, :] * p1 + w_ref[g, 2:3, :] * x)
        o_ref[...] = (hc[0] * _sigmoid(hc[0]) * hc[1]).astype(BF)

    return pl.pallas_call(
        body, name=name, grid=(4, n // tm),
        in_specs=[pl.BlockSpec((2, None, tm, FSH), lambda j, i: (0, j, i, 0)),
                  pl.BlockSpec((2, None, 8, FSH), lambda j, i: (0, j, prev(i), 0)),
                  pl.BlockSpec((2, None, 3, FSH), lambda j, i: (0, j, 0, 0)),
                  pl.BlockSpec((2, None, 1, FSH), lambda j, i: (0, j, 0, 0))],
        out_specs=pl.BlockSpec((None, tm, FSH), lambda j, i: (j, i, 0)),
        out_shape=jax.ShapeDtypeStruct((4, n, FSH), BF), compiler_params=_cp(("parallel", "parallel")))(up, up, cw, cb)


def _ffn_conv_bwd(up, dact, cw, cb, seq, name):
    n = up.shape[2]
    tm = min(256, seq)
    ext = tm + 16
    prev, nxt = _halo_maps(tm, n)

    def body(u_ref, up_ref, un_ref, da_ref, dn_ref, w_ref, b_ref, du_ref, dw_ref, db_ref, ux, dx):
        i = pl.program_id(1)
        sp = jnp.where(lax.rem(i * tm, seq) == 0, 0.0, 1.0)
        sn = jnp.where(lax.rem((i + 1) * tm, seq) == 0, 0.0, 1.0)
        main = slice(8, 8 + tm)
        dx[0:8, :] = jnp.zeros((8, FSH), F32)
        dx[main, :] = da_ref[...]
        dx[8 + tm:, :] = dn_ref[...] * sn
        x0, x1, x2, hc = [], [], [], []
        for g in range(2):
            ux[g, 0:8, :] = up_ref[g] * sp
            ux[g, main, :] = u_ref[g]
            ux[g, 8 + tm:, :] = un_ref[g]
            x0.append(ux[g])
            x1.append(pltpu.roll(x0[g], 1, 0))
            x2.append(pltpu.roll(x0[g], 2, 0))
            hc.append(b_ref[g] + w_ref[g, 0:1, :] * x2[g] + w_ref[g, 1:2, :] * x1[g] + w_ref[g, 2:3, :] * x0[g])
        s = _sigmoid(hc[0])
        da = dx[...]
        dhc = (da * hc[1] * (s * (1.0 + hc[0] * (1.0 - s))), da * (hc[0] * s))
        first = i == 0
        for g in range(2):
            dh = dhc[g]
            dup = (w_ref[g, 2:3, :] * dh + w_ref[g, 1:2, :] * pltpu.roll(dh, ext - 1, 0)
                   + w_ref[g, 0:1, :] * pltpu.roll(dh, ext - 2, 0))
            du_ref[g] = dup[main].astype(BF)
            dm = dh[main]
            _accumulate(dw_ref.at[g, 0:1, :], first, _colsum(dm * x2[g][main]))
            _accumulate(dw_ref.at[g, 1:2, :], first, _colsum(dm * x1[g][main]))
            _accumulate(dw_ref.at[g, 2:3, :], first, _colsum(dm * x0[g][main]))
            _accumulate(db_ref.at[g], first, _colsum(dm))

    return pl.pallas_call(
        body, name=name, grid=(4, n // tm),
        in_specs=[pl.BlockSpec((2, None, tm, FSH), lambda j, i: (0, j, i, 0)),
                  pl.BlockSpec((2, None, 8, FSH), lambda j, i: (0, j, prev(i), 0)),
                  pl.BlockSpec((2, None, 8, FSH), lambda j, i: (0, j, nxt(i), 0)),
                  pl.BlockSpec((None, tm, FSH), lambda j, i: (j, i, 0)),
                  pl.BlockSpec((None, 8, FSH), lambda j, i: (j, nxt(i), 0)),
                  pl.BlockSpec((2, None, 3, FSH), lambda j, i: (0, j, 0, 0)),
                  pl.BlockSpec((2, None, 1, FSH), lambda j, i: (0, j, 0, 0))],
        out_specs=[pl.BlockSpec((2, None, tm, FSH), lambda j, i: (0, j, i, 0)),
                   pl.BlockSpec((2, None, 3, FSH), lambda j, i: (0, j, 0, 0)),
                   pl.BlockSpec((2, None, 1, FSH), lambda j, i: (0, j, 0, 0))],
        out_shape=[jax.ShapeDtypeStruct((2, 4, n, FSH), BF), jax.ShapeDtypeStruct((2, 4, 3, FSH), F32),
                   jax.ShapeDtypeStruct((2, 4, 1, FSH), F32)],
        scratch_shapes=[pltpu.VMEM((2, ext, FSH), F32), pltpu.VMEM((ext, FSH), F32)],
        compiler_params=_cp(("parallel", "arbitrary")))(up, up, up, dact, dact, cw, cb)


def _shortconv_fwd(p, cw, cb, seq, name):
    n = p.shape[0]
    tm = min(256, seq)
    prev, _ = _halo_maps(tm, n)

    def body(p_ref, h_ref, w_ref, b_ref, o_ref):
        i = pl.program_id(1)
        scale = jnp.where(lax.rem(i * tm, seq) == 0, 0.0, 1.0)
        q = p_ref[:, FB:2 * FB] * p_ref[:, 2 * FB:]
        row = lax.broadcasted_iota(jnp.int32, q.shape, 0)
        h7 = h_ref[7:8, FB:2 * FB] * h_ref[7:8, 2 * FB:] * scale
        h6 = h_ref[6:7, FB:2 * FB] * h_ref[6:7, 2 * FB:] * scale
        p1 = jnp.where(row == 0, h7, pltpu.roll(q, 1, 0))
        p2 = jnp.where(row == 0, h6, jnp.where(row == 1, h7, pltpu.roll(q, 2, 0)))
        conv = b_ref[...] + w_ref[0:1, :] * p2 + w_ref[1:2, :] * p1 + w_ref[2:3, :] * q
        o_ref[...] = (p_ref[:, :FB] * conv).astype(BF)

    return pl.pallas_call(
        body, name=name, grid=(D // FB, n // tm),
        in_specs=[pl.BlockSpec((tm, 3 * FB), lambda j, i: (i, j)),
                  pl.BlockSpec((8, 3 * FB), lambda j, i: (prev(i), j)),
                  pl.BlockSpec((3, FB), lambda j, i: (0, j)),
                  pl.BlockSpec((1, FB), lambda j, i: (0, j))],
        out_specs=pl.BlockSpec((tm, FB), lambda j, i: (i, j)),
        out_shape=jax.ShapeDtypeStruct((n, D), BF), compiler_params=_cp(("parallel", "parallel")))(p, p, cw, cb)


def _shortconv_bwd(p, dmix, cw, cb, seq, name):
    n = p.shape[0]
    tm = min(256, seq)
    ext = tm + 16
    prev, nxt = _halo_maps(tm, n)

    def body(p_ref, pp_ref, pn_ref, dm_ref, dn_ref, w_ref, b_ref, dp_ref, dw_ref, db_ref, qx, cx):
        i = pl.program_id(1)
        sp = jnp.where(lax.rem(i * tm, seq) == 0, 0.0, 1.0)
        sn = jnp.where(lax.rem((i + 1) * tm, seq) == 0, 0.0, 1.0)
        bg, cg, hx = p_ref[:, :FB], p_ref[:, FB:2 * FB], p_ref[:, 2 * FB:]
        dm = dm_ref[...]
        qx[0:8, :] = pp_ref[:, FB:2 * FB] * pp_ref[:, 2 * FB:] * sp
        qx[8:8 + tm, :] = cg * hx
        qx[8 + tm:, :] = jnp.zeros((8, FB), F32)
        cx[0:8, :] = jnp.zeros((8, FB), F32)
        cx[8:8 + tm, :] = dm * bg
        cx[8 + tm:, :] = dn_ref[...] * pn_ref[:, :FB] * sn
        q0 = qx[...]
        q1 = pltpu.roll(q0, 1, 0)
        q2 = pltpu.roll(q0, 2, 0)
        main = slice(8, 8 + tm)
        conv = b_ref[...] + w_ref[0:1, :] * q2[main] + w_ref[1:2, :] * q1[main] + w_ref[2:3, :] * q0[main]
        dc = cx[...]
        dq = (w_ref[2:3, :] * dc + w_ref[1:2, :] * pltpu.roll(dc, ext - 1, 0)
              + w_ref[0:1, :] * pltpu.roll(dc, ext - 2, 0))[main]
        dp_ref[:, :FB] = (dm * conv).astype(BF)
        dp_ref[:, FB:2 * FB] = (dq * hx).astype(BF)
        dp_ref[:, 2 * FB:] = (dq * cg).astype(BF)
        first = i == 0
        dcm = dc[main]
        _accumulate(dw_ref.at[0:1, :], first, _colsum(dcm * q2[main]))
        _accumulate(dw_ref.at[1:2, :], first, _colsum(dcm * q1[main]))
        _accumulate(dw_ref.at[2:3, :], first, _colsum(dcm * q0[main]))
        _accumulate(db_ref, first, _colsum(dcm))

    return pl.pallas_call(
        body, name=name, grid=(D // FB, n // tm),
        in_specs=[pl.BlockSpec((tm, 3 * FB), lambda j, i: (i, j)),
                  pl.BlockSpec((8, 3 * FB), lambda j, i: (prev(i), j)),
                  pl.BlockSpec((8, 3 * FB), lambda j, i: (nxt(i), j)),
                  pl.BlockSpec((tm, FB), lambda j, i: (i, j)),
                  pl.BlockSpec((8, FB), lambda j, i: (nxt(i), j)),
                  pl.BlockSpec((3, FB), lambda j, i: (0, j)),
                  pl.BlockSpec((1, FB), lambda j, i: (0, j))],
        out_specs=[pl.BlockSpec((tm, 3 * FB), lambda j, i: (i, j)),
                   pl.BlockSpec((3, FB), lambda j, i: (0, j)),
                   pl.BlockSpec((1, FB), lambda j, i: (0, j))],
        out_shape=[jax.ShapeDtypeStruct((n, 3 * D), BF), jax.ShapeDtypeStruct((3, D), F32),
                   jax.ShapeDtypeStruct((1, D), F32)],
        scratch_shapes=[pltpu.VMEM((ext, FB), F32), pltpu.VMEM((ext, FB), F32)],
        compiler_params=_cp(("parallel", "arbitrary")))(p, p, p, dmix, dmix, cw, cb)


def _gmlp_fwd(uv, wm, bst, gv, seq, name):
    n = uv.shape[0]
    tm = min(256, seq)

    def body(x_ref, w_ref, b_ref, g_ref, o_ref):
        ge_v = _gelu(x_ref[:, GM_W:])
        r = lax.rsqrt(jnp.mean(ge_v * ge_v, axis=-1, keepdims=True) + EPS)
        vn = (ge_v * r * g_ref[...]).astype(BF)
        for c in range(tm // CHUNK):
            rows = slice(c * CHUNK, (c + 1) * CHUNK)
            for h in range(GM_HEADS):
                cols = slice(h * CHUNK, (h + 1) * CHUNK)
                gate = jnp.dot(w_ref[h], vn[rows, cols], preferred_element_type=F32) + b_ref[:, h:h + 1]
                o_ref[rows, cols] = (_gelu(x_ref[rows, cols]) * gate).astype(BF)

    return pl.pallas_call(
        body, name=name, grid=(n // tm,),
        in_specs=[pl.BlockSpec((tm, 2 * GM_W), lambda i: (i, 0)),
                  pl.BlockSpec((GM_HEADS, CHUNK, CHUNK), lambda i: (0, 0, 0)),
                  pl.BlockSpec((CHUNK, GM_HEADS), lambda i: (0, 0)),
                  pl.BlockSpec((1, GM_W), lambda i: (0, 0))],
        out_specs=pl.BlockSpec((tm, GM_W), lambda i: (i, 0)),
        out_shape=jax.ShapeDtypeStruct((n, GM_W), BF), compiler_params=_cp(("parallel",)))(uv, wm, bst, gv)


def _gmlp_bwd(uv, dout, wm, wmt, bst, gv, seq, name):
    n = uv.shape[0]
    tm = min(256, seq)

    def body(x_ref, do_ref, w_ref, wt_ref, b_ref, g_ref, dx_ref, dw_ref, db_ref, dg_ref, dvn_scr):
        first = pl.program_id(0) == 0
        ge_v = _gelu(x_ref[:, GM_W:])
        r = lax.rsqrt(jnp.mean(ge_v * ge_v, axis=-1, keepdims=True) + EPS)
        vh = ge_v * r
        vn = (vh * g_ref[...]).astype(BF)
        tril = (lax.broadcasted_iota(jnp.int32, (CHUNK, CHUNK), 0)
                >= lax.broadcasted_iota(jnp.int32, (CHUNK, CHUNK), 1))
        for h in range(GM_HEADS):
            cols = slice(h * CHUNK, (h + 1) * CHUNK)
            dw = jnp.zeros((CHUNK, CHUNK), F32)
            dbs = jnp.zeros((CHUNK, 1), F32)
            for c in range(tm // CHUNK):
                rows = slice(c * CHUNK, (c + 1) * CHUNK)
                blk = vn[rows, cols]
                gate = jnp.dot(w_ref[h], blk, preferred_element_type=F32) + b_ref[:, h:h + 1]
                xu = x_ref[rows, cols]
                do = do_ref[rows, cols]
                dx_ref[rows, cols] = (do * gate * _gelu_grad(xu)).astype(BF)
                dgate = do * _gelu(xu)
                dgb = dgate.astype(BF)
                dw = dw + lax.dot_general(dgb, blk, _DIMS['nt'], preferred_element_type=F32)
                dbs = dbs + jnp.sum(dgate, axis=1, keepdims=True)
                dvn_scr[rows, cols] = jnp.dot(wt_ref[h], dgb, preferred_element_type=F32)
            _accumulate(dw_ref.at[h], first, jnp.where(tril, dw, 0.0))
            _accumulate(db_ref.at[h], first, dbs)
        dvn = dvn_scr[...]
        _accumulate(dg_ref, first, _colsum(dvn * vh))
        dvh = dvn * g_ref[...]
        dv = r * (dvh - vh * jnp.mean(dvh * vh, axis=-1, keepdims=True))
        dx_ref[:, GM_W:] = (dv * _gelu_grad(x_ref[:, GM_W:])).astype(BF)

    full3 = pl.BlockSpec((GM_HEADS, CHUNK, CHUNK), lambda i: (0, 0, 0))
    return pl.pallas_call(
        body, name=name, grid=(n // tm,),
        in_specs=[pl.BlockSpec((tm, 2 * GM_W), lambda i: (i, 0)), pl.BlockSpec((tm, GM_W), lambda i: (i, 0)),
                  full3, full3, pl.BlockSpec((CHUNK, GM_HEADS), lambda i: (0, 0)),
                  pl.BlockSpec((1, GM_W), lambda i: (0, 0))],
        out_specs=[pl.BlockSpec((tm, 2 * GM_W), lambda i: (i, 0)), full3,
                   pl.BlockSpec((GM_HEADS, CHUNK, 1), lambda i: (0, 0, 0)),
                   pl.BlockSpec((1, GM_W), lambda i: (0, 0))],
        out_shape=[jax.ShapeDtypeStruct((n, 2 * GM_W), BF), jax.ShapeDtypeStruct((GM_HEADS, CHUNK, CHUNK), F32),
                   jax.ShapeDtypeStruct((GM_HEADS, CHUNK, 1), F32), jax.ShapeDtypeStruct((1, GM_W), F32)],
        scratch_shapes=[pltpu.VMEM((tm, GM_W), F32)],
        compiler_params=_cp(("arbitrary",)))(uv, dout, wm, wmt, bst, gv)


def _s5_disc(lam_re, lam_im, log_dt, b_re, b_im):
    lr = jnp.minimum(lam_re, LAM_MAX)
    li = lam_im
    dt = jnp.exp(log_dt)
    mag = jnp.exp(lr * dt)
    ab_re = mag * jnp.cos(li * dt)
    ab_im = mag * jnp.sin(li * dt)
    den = lr * lr + li * li
    nr = ab_re - 1.0
    ni = ab_im
    z_re = (nr * lr + ni * li) / den
    z_im = (ni * lr - nr * li) / den
    return ab_re, ab_im, z_re * b_re - z_im * b_im, z_re * b_im + z_im * b_re


def _s5_disc_fwd(args, name):
    shp = jax.ShapeDtypeStruct(args[0].shape, F32)

    def body(*refs):
        outs = _s5_disc(*[r[...] for r in refs[:5]])
        for o_ref, o in zip(refs[5:], outs):
            o_ref[...] = o

    return pl.pallas_call(body, name=name, out_shape=[shp] * 4)(*args)


def _s5_disc_bwd(args, cts, name):
    shp = jax.ShapeDtypeStruct(args[0].shape, F32)

    def body(*refs):
        _, vjp = jax.vjp(_s5_disc, *[r[...] for r in refs[:5]])
        grads = vjp(tuple(r[...] for r in refs[5:9]))
        for o_ref, o in zip(refs[9:], grads):
            o_ref[...] = o

    return pl.pallas_call(body, name=name, out_shape=[shp] * 5)(*args, *cts)


def _cmul(a, b):
    return a[0] * b[0] - a[1] * b[1], a[0] * b[1] + a[1] * b[0]


def _scan_tables(ar, ai, reverse):
    if reverse:
        ai = -ai
    a1 = (ar, ai)
    a2 = _cmul(a1, a1)
    a3 = _cmul(a2, a1)
    a4 = _cmul(a2, a2)
    powers = [a1, a2, a3, a4, _cmul(a4, a1), _cmul(a4, a2), _cmul(a4, a3), _cmul(a4, a4)]
    row = lax.broadcasted_iota(jnp.int32, (8, NST), 0)
    zero = jnp.zeros((8, NST), F32)
    pr, pi = zero, zero
    for r in range(8):
        pw = powers[7 - r] if reverse else powers[r]
        pr = jnp.where(row == r, pw[0], pr)
        pi = jnp.where(row == r, pw[1], pi)
    levels = []
    for d, pw in ((1, a1), (2, a2), (4, a4)):
        ok = (row <= 7 - d) if reverse else (row >= d)
        levels.append((d, jnp.where(ok, pw[0], zero), jnp.where(ok, pw[1], zero)))
    return (pr, pi), levels


def _scan_block(src, dst, car, tables, n_tiles, reverse):
    (pr, pi), levels = tables
    row = lax.broadcasted_iota(jnp.int32, (8, NST), 0)
    out_row = 0 if reverse else 7

    def step(t, carry):
        cr, ci = carry
        tile = (n_tiles - 1 - t) if reverse else t
        rows = pl.ds(pl.multiple_of(tile * 8, 8), 8)
        xr = src[rows, 0:NST]
        xi = src[rows, NST:2 * NST]
        for d, dr, di in levels:
            shift = 8 - d if reverse else d
            rr = pltpu.roll(xr, shift, 0)
            ri = pltpu.roll(xi, shift, 0)
            xr, xi = xr + dr * rr - di * ri, xi + dr * ri + di * rr
        hr = xr + pr * cr - pi * ci
        hi = xi + pr * ci + pi * cr
        dst[rows, 0:NST] = hr
        dst[rows, NST:2 * NST] = hi
        return (_colsum(jnp.where(row == out_row, hr, 0.0)), _colsum(jnp.where(row == out_row, hi, 0.0)))

    cr, ci = lax.fori_loop(0, n_tiles, step, (car[0:1, 0:NST], car[0:1, NST:2 * NST]))
    car[0:1, 0:NST] = cr
    car[0:1, NST:2 * NST] = ci


def _s5_fwd(u, ab, bbt, cmat, dvec, wglu, bglu, seq, name):
    n = u.shape[0]
    tm = min(256, seq)

    def body(u_ref, ab_ref, bb_ref, c_ref, d_ref, w_ref, b_ref, h_ref, o_ref, xs, car):
        i = pl.program_id(0)

        @pl.when(lax.rem(i * tm, seq) == 0)
        def _():
            car[...] = jnp.zeros(car.shape, F32)

        uv = u_ref[...]
        xs[...] = jnp.dot(uv.astype(BF), bb_ref[...], preferred_element_type=F32)
        tables = _scan_tables(ab_ref[0:1, 0:NST], ab_ref[0:1, NST:2 * NST], False)
        _scan_block(xs, h_ref, car, tables, tm // 8, False)
        y = jnp.dot(h_ref[...].astype(BF), c_ref[...], preferred_element_type=F32) + d_ref[...] * uv
        g1 = _gelu(y)
        z = jnp.dot(g1.astype(BF), w_ref[...], preferred_element_type=F32) + b_ref[...]
        o_ref[...] = (g1 * _sigmoid(z)).astype(BF)

    const = lambda shape: pl.BlockSpec(shape, lambda i: (0, 0))
    return pl.pallas_call(
        body, name=name, grid=(n // tm,),
        in_specs=[pl.BlockSpec((tm, SSM_W), lambda i: (i, 0)), const((1, 2 * NST)), const((SSM_W, 2 * NST)),
                  const((2 * NST, SSM_W)), const((1, SSM_W)), const((SSM_W, SSM_W)), const((1, SSM_W))],
        out_specs=[pl.BlockSpec((tm, 2 * NST), lambda i: (i, 0)), pl.BlockSpec((tm, SSM_W), lambda i: (i, 0))],
        out_shape=[jax.ShapeDtypeStruct((n, 2 * NST), F32), jax.ShapeDtypeStruct((n, SSM_W), BF)],
        scratch_shapes=[pltpu.VMEM((tm, 2 * NST), F32), pltpu.VMEM((8, 2 * NST), F32)],
        compiler_params=_cp(("arbitrary",)))(u, ab, bbt, cmat, dvec, wglu, bglu)


def _s5_bwd(da, u, hst, ab, bbt, cmat, dvec, wglu, bglu, seq, name):
    n = u.shape[0]
    tm = min(256, seq)
    nb = n // tm
    blk = lambda r: nb - 1 - r
    prev, _ = _halo_maps(tm, n)

    def body(da_ref, u_ref, h_ref, hp_ref, ab_ref, bb_ref, c_ref, d_ref, w_ref, b_ref,
             du_ref, dw_ref, dbg_ref, dd_ref, dc_ref, dbb_ref, dab_ref, gs, car):
        r = pl.program_id(0)
        i = blk(r)
        first = r == 0

        @pl.when(lax.rem((i + 1) * tm, seq) == 0)
        def _():
            car[...] = jnp.zeros(car.shape, F32)

        uv = u_ref[...]
        dav = da_ref[...]
        hb = h_ref[...]
        hb16 = hb.astype(BF)
        dvv = d_ref[...]
        y = jnp.dot(hb16, c_ref[...], preferred_element_type=F32) + dvv * uv
        g1 = _gelu(y)
        g16 = g1.astype(BF)
        s = _sigmoid(jnp.dot(g16, w_ref[...], preferred_element_type=F32) + b_ref[...])
        dz = dav * g1 * s * (1.0 - s)
        dz16 = dz.astype(BF)
        dg1 = dav * s + lax.dot_general(dz16, w_ref[...], _DIMS['nt'], preferred_element_type=F32)
        _accumulate(dw_ref, first, lax.dot_general(g16, dz16, _DIMS['tn'], preferred_element_type=F32))
        _accumulate(dbg_ref, first, _colsum(dz))
        dy = dg1 * _gelu_grad(y)
        dy16 = dy.astype(BF)
        _accumulate(dd_ref, first, _colsum(dy * uv))
        _accumulate(dc_ref, first, lax.dot_general(hb16, dy16, _DIMS['tn'], preferred_element_type=F32))
        gs[...] = lax.dot_general(dy16, c_ref[...], _DIMS['nt'], preferred_element_type=F32)
        tables = _scan_tables(ab_ref[0:1, 0:NST], ab_ref[0:1, NST:2 * NST], True)
        _scan_block(gs, gs, car, tables, tm // 8, True)
        g = gs[...]
        g16b = g.astype(BF)
        sp = jnp.where(lax.rem(i * tm, seq) == 0, 0.0, 1.0)
        row = lax.broadcasted_iota(jnp.int32, hb.shape, 0)
        hprev = jnp.where(row == 0, hp_ref[7:8, :] * sp, pltpu.roll(hb, 1, 0))
        gr, gi = g[:, :NST], g[:, NST:]
        hr, hi = hprev[:, :NST], hprev[:, NST:]
        _accumulate(dab_ref.at[:, 0:NST], first, _colsum(gr * hr + gi * hi))
        _accumulate(dab_ref.at[:, NST:2 * NST], first, _colsum(gi * hr - gr * hi))
        _accumulate(dbb_ref, first, lax.dot_general(uv.astype(BF), g16b, _DIMS['tn'], preferred_element_type=F32))
        du = dy * dvv + lax.dot_general(g16b, bb_ref[...], _DIMS['nt'], preferred_element_type=F32)
        du_ref[...] = du.astype(BF)

    const = lambda shape: pl.BlockSpec(shape, lambda r: (0, 0))
    rowspec = lambda w: pl.BlockSpec((tm, w), lambda r: (blk(r), 0))
    return pl.pallas_call(
        body, name=name, grid=(nb,),
        in_specs=[rowspec(SSM_W), rowspec(SSM_W), rowspec(2 * NST),
                  pl.BlockSpec((8, 2 * NST), lambda r: (prev(blk(r)), 0)),
                  const((1, 2 * NST)), const((SSM_W, 2 * NST)), const((2 * NST, SSM_W)), const((1, SSM_W)),
                  const((SSM_W, SSM_W)), const((1, SSM_W))],
        out_specs=[rowspec(SSM_W), const((SSM_W, SSM_W)), const((1, SSM_W)), const((1, SSM_W)),
                   const((2 * NST, SSM_W)), const((SSM_W, 2 * NST)), const((1, 2 * NST))],
        out_shape=[jax.ShapeDtypeStruct((n, SSM_W), BF), jax.ShapeDtypeStruct((SSM_W, SSM_W), F32),
                   jax.ShapeDtypeStruct((1, SSM_W), F32), jax.ShapeDtypeStruct((1, SSM_W), F32),
                   jax.ShapeDtypeStruct((2 * NST, SSM_W), F32), jax.ShapeDtypeStruct((SSM_W, 2 * NST), F32),
                   jax.ShapeDtypeStruct((1, 2 * NST), F32)],
        scratch_shapes=[pltpu.VMEM((tm, 2 * NST), F32), pltpu.VMEM((8, 2 * NST), F32)],
        compiler_params=_cp(("arbitrary",)))(da, u, hst, hst, ab, bbt, cmat, dvec, wglu, bglu)


def _s5_rows(lam_re, lam_im, log_dt, b_re, b_im):
    rep = lambda a: jnp.broadcast_to(a[:, None, :], (SSM_G, SSM_H, SSM_P)).reshape(SSM_W, SSM_P)
    dt = jnp.broadcast_to(log_dt[:, None, None], (SSM_G, SSM_H, SSM_P)).reshape(SSM_W, SSM_P)
    tr = lambda b: b.transpose(0, 2, 1).reshape(SSM_W, SSM_P)
    return rep(lam_re), rep(lam_im), dt, tr(b_re), tr(b_im)


def _block_diag(rows_gp, inner):
    eye = jnp.eye(SSM_G, dtype=rows_gp.dtype)
    return (rows_gp[:, :, None, :] * eye[:, None, :, None]).reshape(SSM_G * inner, SSM_G * SSM_P)


def _diag_blocks(mat, inner):
    m4 = mat.reshape(SSM_G, inner, SSM_G, SSM_P)
    return jnp.stack([m4[g, :, g, :] for g in range(SSM_G)])


def _interleave(w, parts):
    lead = w.shape[:-1]
    nb = w.shape[-1] // (parts * FB)
    return jnp.swapaxes(w.reshape(lead + (parts, nb, FB)), -3, -2).reshape(w.shape)


def _deinterleave(w, parts):
    lead = w.shape[:-1]
    nb = w.shape[-1] // (parts * FB)
    return jnp.swapaxes(w.reshape(lead + (nb, parts, FB)), -3, -2).reshape(w.shape)


def _ffn_fwd(h, g, w_up, w_down, cw, cb, seq, tag):
    n = h.shape[0]
    tm = min(1024, n)
    ni = n // tm
    f = _rmsnorm_fwd(h, g, f"{tag}_norm")
    up = _matmul_spec(
        f, w_up, 'nn', (NDEV, ni, 1),
        pl.BlockSpec((tm, D), lambda s, i, k: (i, 0)),
        pl.BlockSpec((D, FSH), lambda s, i, k: (s, 0)),
        pl.BlockSpec((tm, FSH), lambda s, i, k: (s * ni + i, 0)), (NDEV * n, FSH), f"{tag}_up")
    up = up.reshape(2, 4, n, FSH)
    act = _ffn_conv_fwd(up, cw, cb, seq, f"{tag}_conv")
    tn = 512
    out = _matmul_spec(
        act.reshape(4 * n, FSH), w_down, 'nn', (ni, D // tn, 4),
        pl.BlockSpec((tm, FSH), lambda i, j, k: (k * ni + i, 0)),
        pl.BlockSpec((FSH, tn), lambda i, j, k: (k, j)),
        pl.BlockSpec((tm, tn), lambda i, j, k: (i, j)), (n, D), f"{tag}_down", resid=h)
    return out, (f, up, act)


def _ffn_bwd(dh, h, g, w_up, w_down, cw, cb, saved, seq, tag):
    f, up, act = saved
    n = h.shape[0]
    tm = min(1024, n)
    ni = n // tm
    tk = min(1024, n)
    nk = n // tk
    dact = _matmul_spec(
        dh, w_down, 'nt', (4, ni, 1),
        pl.BlockSpec((tm, D), lambda j, i, k: (i, 0)),
        pl.BlockSpec((FSH, D), lambda j, i, k: (j, 0)),
        pl.BlockSpec((tm, FSH), lambda j, i, k: (j * ni + i, 0)), (4 * n, FSH), f"{tag}_ddown_x")
    tn = 512
    dw_down = _matmul_spec(
        act.reshape(4 * n, FSH), dh, 'tn', (4, D // tn, nk),
        pl.BlockSpec((tk, FSH), lambda j, c, k: (j * nk + k, 0)),
        pl.BlockSpec((tk, tn), lambda j, c, k: (k, c)),
        pl.BlockSpec((FSH, tn), lambda j, c, k: (j, c)), (DFF, D), f"{tag}_ddown_w")
    dup, dcw, dcb = _ffn_conv_bwd(up, dact.reshape(4, n, FSH), cw, cb, seq, f"{tag}_dconv")
    dup2 = dup.reshape(NDEV * n, FSH)
    df = _matmul_spec(
        dup2, w_up, 'nt', (ni, 1, NDEV),
        pl.BlockSpec((tm, FSH), lambda i, j, k: (k * ni + i, 0)),
        pl.BlockSpec((D, FSH), lambda i, j, k: (k, 0)),
        pl.BlockSpec((tm, D), lambda i, j, k: (i, 0)), (n, D), f"{tag}_dup_x")
    dw_up = _matmul_spec(
        f, dup2, 'tn', (NDEV, 1, nk),
        pl.BlockSpec((tk, D), lambda s, j, k: (k, 0)),
        pl.BlockSpec((tk, FSH), lambda s, j, k: (s * nk + k, 0)),
        pl.BlockSpec((D, FSH), lambda s, j, k: (s, 0)), (NDEV * D, FSH), f"{tag}_dup_w")
    dh_in, dg = _rmsnorm_bwd(h, g, df, dh, f"{tag}_dnorm")
    grads = dict(g=dg, w_up=dw_up.reshape(NDEV, D, FSH), w_down=dw_down.reshape(NDEV, DFF // NDEV, D),
                 cw=dcw.reshape(NDEV, 3, FSH), cb=dcb.reshape(2 * DFF))
    return dh_in, grads


def _col_shards(w, width):
    return w.reshape(w.shape[0], NDEV, width).transpose(1, 0, 2)


def _local_step(x, tgt, w, gw, wait_ffn0, wait_rest, token, seq):
    bf = lambda a: a.astype(BF)
    row = lambda a: a.reshape(1, -1).astype(F32)
    w_ev = gw['ev_w_in'].transpose(1, 0, 2).reshape(D, 1792)
    w_ev_s5, w_ev_gm = w_ev[:, :SSM_W], w_ev[:, SSM_W:]
    w_evo = gw['ev_w_out'].reshape(D, D)
    f_cb = [w['ffn_conv_b'][l].reshape(2, 4, 1, FSH) for l in range(2)]
    tril = jnp.tril(jnp.ones((CHUNK, CHUNK), dtype=bool))
    gm_w = jnp.where(tril, w['gm_w_s'][0], 0.0)
    gm_wm, gm_wmt = bf(gm_w), bf(jnp.swapaxes(gm_w, 1, 2))
    gm_bt = w['gm_b_s'][0].T
    gm_gv = row(w['gm_v_g'][0])
    s5_in = _s5_rows(w['s5_lam_re'][0], w['s5_lam_im'][0], w['s5_log_dt'][0], w['s5_b_re'][0], w['s5_b_im'][0])
    ab_re, ab_im, bb_re, bb_im = _s5_disc_fwd(s5_in, "s5_disc")
    first_h = lambda a: a.reshape(SSM_G, SSM_H, SSM_P)[:, 0, :].reshape(1, NST)
    s5_ab = jnp.concatenate([first_h(ab_re), first_h(ab_im)], axis=1)
    to_gp = lambda a: a.reshape(SSM_G, SSM_H, SSM_P)
    s5_bbt = bf(jnp.concatenate([_block_diag(to_gp(bb_re), SSM_H), _block_diag(to_gp(bb_im), SSM_H)], axis=1))
    s5_cmat = bf(jnp.concatenate([_block_diag(w['s5_c_re'][0], SSM_H).T, -_block_diag(w['s5_c_im'][0], SSM_H).T],
                                 axis=0))
    s5_d, s5_bg, s5_wg = row(w['s5_d'][0]), row(w['s5_b_glu'][0]), gw['s5_w_glu'].reshape(SSM_W, SSM_W)
    g_mix = [row(w['mix_norm_g'][0]) + token[0:1, 0:1], row(w['mix_norm_g'][1])]
    g_ffn = [row(w['ffn_norm_g'][l]) for l in range(2)]
    g_fin = row(w['final_norm_g'])

    h0 = x
    y0 = _rmsnorm_fwd(h0, g_mix[0], "ev_norm")
    p_s5 = _matmul(y0, w_ev_s5, 'nn', 1024, 256, D, "ev_in_s5")
    p_gm = _matmul(y0, w_ev_gm, 'nn', 1024, 512, D, "ev_in_gm")
    hst, a_out = _s5_fwd(p_s5, s5_ab, s5_bbt, s5_cmat, s5_d, s5_wg, s5_bg, seq, "s5_fwd")
    b_out = _gmlp_fwd(p_gm, gm_wm, gm_bt, gm_gv, seq, "gmlp_fwd")
    mixcat = jnp.concatenate([a_out, b_out], axis=1)
    h1 = _matmul(mixcat, w_evo, 'nn', 1024, 512, D, "ev_out", resid=h0)
    g0 = wait_ffn0(mixcat)
    w_up0, w_dn0 = g0['ffn_w_up0'].reshape(NDEV * D, FSH), g0['ffn_w_down0'].reshape(DFF, D)
    f_cw0 = g0['ffn_conv_w0'].reshape(2, 4, 3, FSH)
    h2, ffn0 = _ffn_fwd(h1, g_ffn[0], w_up0, w_dn0, f_cw0, f_cb[0], seq, "ffn0")
    g1 = wait_rest(h2)
    w_od = _interleave(g1['od_w_in'].transpose(1, 0, 2).reshape(D, 3 * D), 3)
    w_odo = g1['od_w_out'].reshape(D, D)
    od_cw = g1['od_conv_w'].transpose(1, 0, 2).reshape(3, D)
    od_cb = g1['od_conv_b'].reshape(1, D)
    w_up1, w_dn1 = g1['ffn_w_up1'].reshape(NDEV * D, FSH), g1['ffn_w_down1'].reshape(DFF, D)
    f_cw1 = g1['ffn_conv_w1'].reshape(2, 4, 3, FSH)
    y1 = _rmsnorm_fwd(h2, g_mix[1], "od_norm")
    p_od = _matmul(y1, w_od, 'nn', 1024, 512, D, "od_in")
    mixin = _shortconv_fwd(p_od, od_cw, od_cb, seq, "od_conv")
    h3 = _matmul(mixin, w_odo, 'nn', 1024, 512, D, "od_out", resid=h2)
    h4, ffn1 = _ffn_fwd(h3, g_ffn[1], w_up1, w_dn1, f_cw1, f_cb[1], seq, "ffn1")
    loss, dh4, dg_fin = _final_loss(h4, g_fin, tgt, "final_loss")

    dh3, gf1 = _ffn_bwd(dh4, h3, g_ffn[1], w_up1, w_dn1, f_cw1, f_cb[1], ffn1, seq, "ffn1")
    dmixin = _matmul(dh3, w_odo, 'nt', 1024, 512, D, "od_dout_x")
    dw_odo = _matmul(mixin, dh3, 'tn', D, 512, 1024, "od_dout_w")
    dp_od, d_od_cw, d_od_cb = _shortconv_bwd(p_od, dmixin, od_cw, od_cb, seq, "od_dconv")
    dy1 = _matmul(dp_od, w_od, 'nt', 1024, D, 512, "od_din_x")
    dw_od = _matmul(y1, dp_od, 'tn', D, 512, 1024, "od_din_w")
    dh2, dg_mix1 = _rmsnorm_bwd(h2, g_mix[1], dy1, dh3, "od_dnorm")
    dh1, gf0 = _ffn_bwd(dh2, h1, g_ffn[0], w_up0, w_dn0, f_cw0, f_cb[0], ffn0, seq, "ffn0")
    dmix_a = _matmul(dh1, w_evo[:SSM_W], 'nt', 1024, SSM_W, D, "ev_dout_xa")
    dmix_b = _matmul(dh1, w_evo[SSM_W:], 'nt', 1024, GM_W, D, "ev_dout_xb")
    dw_evo = _matmul(mixcat, dh1, 'tn', D, 512, 1024, "ev_dout_w")
    dp_s5, d_wg, d_bg, d_d, d_cmat, d_bbt, d_ab = _s5_bwd(dmix_a, p_s5, hst, s5_ab, s5_bbt, s5_cmat, s5_d, s5_wg,
                                                           s5_bg, seq, "s5_bwd")
    dp_gm, d_gmw, d_gmb, d_gmg = _gmlp_bwd(p_gm, dmix_b, gm_wm, gm_wmt, gm_bt, gm_gv, seq, "gmlp_bwd")
    dy0 = _matmul(dp_gm, w_ev_gm, 'nt', 1024, D, 512, "ev_din_xb")
    dy0 = _matmul(dp_s5, w_ev_s5, 'nt', 1024, D, SSM_W, "ev_din_xa", resid=dy0)
    dw_ev = jnp.concatenate([_matmul(y0, dp_s5, 'tn', D, SSM_W, 1024, "ev_din_wa"),
                             _matmul(y0, dp_gm, 'tn', D, 512, 1024, "ev_din_wb")], axis=1)
    grad_x, dg_mix0 = _rmsnorm_bwd(h0, g_mix[0], dy0, dh1, "ev_dnorm")

    put_h0 = lambda a: jnp.zeros((SSM_G, SSM_H, SSM_P), F32).at[:, 0, :].set(a.reshape(SSM_G, SSM_P)).reshape(
        SSM_W, SSM_P)
    ct = (put_h0(d_ab[:, :NST]), put_h0(d_ab[:, NST:]),
          _diag_blocks(d_bbt[:, :NST], SSM_H).reshape(SSM_W, SSM_P),
          _diag_blocks(d_bbt[:, NST:], SSM_H).reshape(SSM_W, SSM_P))
    d_lre, d_lim, d_ldt, d_bre, d_bim = _s5_disc_bwd(s5_in, ct, "s5_ddisc")
    over_h = lambda a: a.reshape(SSM_G, SSM_H, SSM_P).sum(axis=1)
    un_tr = lambda a: a.reshape(SSM_G, SSM_H, SSM_P).transpose(0, 2, 1)
    d_cre = _diag_blocks(d_cmat[:NST].T, SSM_H)
    d_cim = -_diag_blocks(d_cmat[NST:].T, SSM_H)

    repl = {
        'mix_norm_g': jnp.concatenate([dg_mix0, dg_mix1], axis=0),
        'ffn_norm_g': jnp.concatenate([gf0['g'], gf1['g']], axis=0),
        'final_norm_g': dg_fin.reshape(D),
        's5_lam_re': over_h(d_lre)[None], 's5_lam_im': over_h(d_lim)[None],
        's5_log_dt': over_h(d_ldt).sum(axis=1)[None],
        's5_b_re': un_tr(d_bre)[None], 's5_b_im': un_tr(d_bim)[None],
        's5_c_re': d_cre[None], 's5_c_im': d_cim[None],
        's5_d': d_d, 's5_b_glu': d_bg,
        'gm_w_s': d_gmw[None], 'gm_b_s': d_gmb.reshape(1, GM_HEADS, CHUNK), 'gm_v_g': d_gmg,
        'ffn_conv_b': jnp.stack([gf0['cb'], gf1['cb']]),
    }
    sharded = {
        'ev_w_in': _col_shards(dw_ev, 224), 'ev_w_out': dw_evo.reshape(NDEV, D // NDEV, D),
        's5_w_glu': d_wg.reshape(NDEV, SSM_W // NDEV, SSM_W),
        'od_w_in': _col_shards(_deinterleave(dw_od, 3), 384), 'od_conv_w': _col_shards(d_od_cw, D // NDEV),
        'od_conv_b': d_od_cb.reshape(NDEV, 1, D // NDEV), 'od_w_out': dw_odo.reshape(NDEV, D // NDEV, D),
        'ffn_w_up0': gf0['w_up'], 'ffn_w_up1': gf1['w_up'], 'ffn_conv_w0': gf0['cw'], 'ffn_conv_w1': gf1['cw'],
        'ffn_w_down0': gf0['w_down'], 'ffn_w_down1': gf1['w_down'],
    }
    return loss, grad_x, repl, sharded


HBM_SPEC = pl.BlockSpec(memory_space=pltpu.HBM)


def _at_axis(ref, pos, index):
    return ref.at[(slice(None),) * pos + (index,)]


def _all_gather(shards, positions, name):
    n = len(shards)

    def body(*refs):
        xs, outs = refs[:n], refs[n:2 * n]
        send_sems, recv_sems, local_sems = refs[2 * n:]
        x, y, c = lax.axis_index("x"), lax.axis_index("y"), lax.axis_index("c")
        me, sibling = (x, y, c), (x, y, 1 - c)
        chips = [(1 - x, y), (x, 1 - y), (1 - x, 1 - y)]

        def block(p, dev):
            return _at_axis(outs[p], positions[p], 4 * dev[0] + 2 * dev[1] + dev[2])

        def copy(p, k, dev, to, src=None):
            return pltpu.make_async_remote_copy(
                src_ref=block(p, dev) if src is None else src, dst_ref=block(p, dev),
                send_sem=send_sems.at[p, k], recv_sem=recv_sems.at[p, k], device_id=to, device_id_type=MESH_T)

        mine = [pltpu.make_async_copy(xs[p], block(p, me), local_sems.at[p]) for p in range(n)]
        for cp in mine:
            cp.start()
        first = [copy(p, 0, me, sibling, src=xs[p]) for p in range(n)]
        first += [copy(p, 1 + j, me, (*chip, c), src=xs[p]) for j, chip in enumerate(chips) for p in range(n)]
        for cp in first:
            cp.start()
        passed = []
        for j, chip in enumerate(chips):
            for p in range(n):
                copy(p, 1 + j, (*chip, c), me).wait_recv()
                fwd = copy(p, 4 + j, (*chip, c), sibling)
                fwd.start()
                passed.append(fwd)
        for p in range(n):
            copy(p, 0, sibling, me).wait_recv()
        for j, chip in enumerate(chips):
            for p in range(n):
                copy(p, 4 + j, (*chip, 1 - c), me).wait_recv()
        for cp in first + passed:
            cp.wait_send()
        for cp in mine:
            cp.wait()

    out_shape = [jax.ShapeDtypeStruct(s.shape[:pos] + (NDEV,) + s.shape[pos:], s.dtype)
                 for s, pos in zip(shards, positions)]
    return pl.pallas_call(
        body, name=name, out_shape=out_shape, in_specs=[HBM_SPEC] * n, out_specs=[HBM_SPEC] * n,
        scratch_shapes=[pltpu.SemaphoreType.DMA((n, 7)), pltpu.SemaphoreType.DMA((n, 7)),
                        pltpu.SemaphoreType.DMA((n,))])(*shards)


def _other_devices(x, y, c):
    flip = lambda v, bit: 1 - v if bit else v
    return [(flip(x, k >> 2 & 1), flip(y, k >> 1 & 1), flip(c, k & 1)) for k in range(1, NDEV)]


SEM_SPEC = pl.BlockSpec(memory_space=pltpu.SEMAPHORE)
START_EFFECT = pltpu.SideEffectType.DATAFLOW_SIDE_EFFECTING


def _gather_start(shards, name):
    n = len(shards)
    lands = [pltpu.with_memory_space_constraint(lax.empty((NDEV,) + s.shape, s.dtype), pltpu.HBM) for s in shards]

    def body(*refs):
        xs, ls = refs[:n], refs[n:2 * n]
        send_sems, recv_sems = refs[2 * n], refs[2 * n + 1]
        token, local_sems = refs[4 * n + 2], refs[4 * n + 3]
        x, y, c = lax.axis_index("x"), lax.axis_index("y"), lax.axis_index("c")
        me = 4 * x + 2 * y + c
        mine = [pltpu.make_async_copy(xs[p], ls[p].at[me], local_sems.at[p]) for p in range(n)]
        for cp in mine:
            cp.start()
        for k, peer in enumerate(_other_devices(x, y, c)):
            for p in range(n):
                pltpu.make_async_remote_copy(
                    src_ref=xs[p], dst_ref=ls[p].at[me], send_sem=send_sems.at[p * (NDEV - 1) + k],
                    recv_sem=recv_sems.at[p * (NDEV - 1) + k],
                    device_id=peer, device_id_type=MESH_T).start()
        for cp in mine:
            cp.wait()
        token[...] = jnp.zeros(token.shape, F32)

    sems = pltpu.SemaphoreType.DMA((n * (NDEV - 1),))
    out_shape = ([sems, sems] + [pltpu.HBM(s.shape, s.dtype) for s in shards]
                 + [pltpu.HBM(l.shape, l.dtype) for l in lands] + [jax.ShapeDtypeStruct((8, 128), F32)])
    res = pl.pallas_call(
        body, name=name, out_shape=out_shape, in_specs=[HBM_SPEC] * (2 * n),
        out_specs=[SEM_SPEC, SEM_SPEC] + [HBM_SPEC] * (2 * n) + [pl.BlockSpec(memory_space=pltpu.VMEM)],
        input_output_aliases={i: 2 + i for i in range(2 * n)},
        scratch_shapes=[pltpu.SemaphoreType.DMA((n,))],
        compiler_params=pltpu.CompilerParams(has_side_effects=START_EFFECT))(
            *[pltpu.with_memory_space_constraint(s, pltpu.HBM) for s in shards], *lands)
    return res[0], res[1], res[2:2 + n], res[2 + n:2 + 2 * n], res[2 + 2 * n]


def _gather_wait(send_sems, recv_sems, shards, lands, after, name):
    n = len(shards)

    def body(*refs):
        xs, ls = refs[:n], refs[n:2 * n]
        send, recv = refs[2 * n], refs[2 * n + 1]
        x, y, c = lax.axis_index("x"), lax.axis_index("y"), lax.axis_index("c")
        for k, peer in enumerate(_other_devices(x, y, c)):
            for p in range(n):
                cp = pltpu.make_async_remote_copy(
                    src_ref=xs[p], dst_ref=ls[p].at[4 * peer[0] + 2 * peer[1] + peer[2]],
                    send_sem=send.at[p * (NDEV - 1) + k], recv_sem=recv.at[p * (NDEV - 1) + k], device_id=peer,
                    device_id_type=MESH_T)
                cp.wait_send()
                cp.wait_recv()

    res = pl.pallas_call(
        body, name=name, out_shape=[pltpu.HBM(a.shape, a.dtype) for a in list(shards) + list(lands)],
        in_specs=[HBM_SPEC] * (2 * n) + [SEM_SPEC, SEM_SPEC, pl.BlockSpec(memory_space=pl.ANY)],
        out_specs=[HBM_SPEC] * (2 * n), input_output_aliases={i: i for i in range(2 * n)},
        compiler_params=pltpu.CompilerParams(has_side_effects=START_EFFECT))(
            *shards, *lands, send_sems, recv_sems, after)
    return res[n:]


def _exchange_sibling(parts, name):
    n = len(parts)

    def body(*refs):
        gs, rs = refs[:n], refs[n:2 * n]
        send_sems, recv_sems = refs[2 * n:]
        x, y, c = lax.axis_index("x"), lax.axis_index("y"), lax.axis_index("c")
        copies = [pltpu.make_async_remote_copy(
            src_ref=gs[p].at[2 * chip + (1 - c)], dst_ref=rs[p].at[chip], send_sem=send_sems.at[p, chip],
            recv_sem=recv_sems.at[p, chip], device_id=(x, y, 1 - c), device_id_type=MESH_T)
            for p in range(n) for chip in range(4)]
        for cp in copies:
            cp.start()
        for cp in copies:
            cp.wait()

    return pl.pallas_call(
        body, name=name, out_shape=[jax.ShapeDtypeStruct((4,) + g.shape[1:], g.dtype) for g in parts],
        in_specs=[HBM_SPEC] * n, out_specs=[HBM_SPEC] * n,
        scratch_shapes=[pltpu.SemaphoreType.DMA((n, 4)), pltpu.SemaphoreType.DMA((n, 4))])(*parts)


def _exchange_chips(sums, name):
    n = len(sums)

    def body(*refs):
        ss, rs = refs[:n], refs[n:2 * n]
        send_sems, recv_sems, local_sems = refs[2 * n:]
        x, y, c = lax.axis_index("x"), lax.axis_index("y"), lax.axis_index("c")
        my_chip = 2 * x + y
        chips = [(1 - x, y), (x, 1 - y), (1 - x, 1 - y)]
        mine = [pltpu.make_async_copy(ss[p].at[my_chip], rs[p].at[my_chip], local_sems.at[p]) for p in range(n)]
        for cp in mine:
            cp.start()

        def copy(p, k, chip, src_slot, dst_slot):
            return pltpu.make_async_remote_copy(
                src_ref=ss[p].at[src_slot], dst_ref=rs[p].at[dst_slot], send_sem=send_sems.at[p, k],
                recv_sem=recv_sems.at[p, k], device_id=(*chip, c), device_id_type=MESH_T)

        sent = [copy(p, k, chip, 2 * chip[0] + chip[1], my_chip) for k, chip in enumerate(chips) for p in range(n)]
        for cp in sent:
            cp.start()
        for k, chip in enumerate(chips):
            slot = 2 * chip[0] + chip[1]
            for p in range(n):
                copy(p, k, chip, slot, slot).wait_recv()
        for cp in sent:
            cp.wait_send()
        for cp in mine:
            cp.wait()

    return pl.pallas_call(
        body, name=name, out_shape=[jax.ShapeDtypeStruct(s.shape, s.dtype) for s in sums],
        in_specs=[HBM_SPEC] * n, out_specs=[HBM_SPEC] * n,
        scratch_shapes=[pltpu.SemaphoreType.DMA((n, 3)), pltpu.SemaphoreType.DMA((n, 3)),
                        pltpu.SemaphoreType.DMA((n,))])(*sums)


def _row_block(rows, cols, itemsize=4, target=2**20):
    best = None
    for tr in range(16, rows + 1, 16):
        if rows % tr == 0 and tr * cols * itemsize <= target:
            best = tr
    return best or rows


def _as_rows(a, lead):
    return a.reshape(a.shape[:lead] + (-1, a.shape[-1]))


def _add_sibling(parts, got, core, name):
    _, r, c_ = parts.shape
    tr = _row_block(r, c_)

    def body(core_ref, a_ref, b_ref, o_ref):
        o_ref[...] = (a_ref[...] + b_ref[...]).astype(BF)

    return pl.pallas_call(
        body, name=name, out_shape=jax.ShapeDtypeStruct((4, r, c_), BF),
        grid_spec=pltpu.PrefetchScalarGridSpec(
            num_scalar_prefetch=1, grid=(4, r // tr),
            in_specs=[pl.BlockSpec((None, tr, c_), lambda ch, i, core_ref: (2 * ch + core_ref[0], i, 0)),
                      pl.BlockSpec((None, tr, c_), lambda ch, i, core_ref: (ch, i, 0))],
            out_specs=pl.BlockSpec((None, tr, c_), lambda ch, i, core_ref: (ch, i, 0))),
        compiler_params=_cp(("parallel", "parallel")))(core, parts, got)


def _adamw(w, m, v, gparts, name):
    parts, rows, cols = gparts.shape
    tr = _row_block(rows, cols, target=2**19)
    bc1 = 1.0 - ADAM_B1 ** ADAM_STEP
    bc2 = 1.0 - ADAM_B2 ** ADAM_STEP

    def body(w_ref, m_ref, v_ref, g_ref, go_ref, d_ref, mo_ref, vo_ref):
        g = g_ref[0].astype(F32)
        for k in range(1, parts):
            g = g + g_ref[k].astype(F32)
        mn = ADAM_B1 * m_ref[...] + (1.0 - ADAM_B1) * g
        vn = ADAM_B2 * v_ref[...] + (1.0 - ADAM_B2) * (g * g)
        go_ref[...] = g
        mo_ref[...] = mn
        vo_ref[...] = vn
        d_ref[...] = -ADAM_LR * ((mn / bc1) / (jnp.sqrt(vn / bc2) + ADAM_EPS) + ADAM_WD * w_ref[...])

    blk = pl.BlockSpec((tr, cols), lambda i: (i, 0))
    shp = jax.ShapeDtypeStruct((rows, cols), F32)
    return pl.pallas_call(
        body, name=name, grid=(rows // tr,),
        in_specs=[blk, blk, blk, pl.BlockSpec((parts, tr, cols), lambda i: (0, i, 0))],
        out_specs=[blk] * 4, out_shape=[shp] * 4, compiler_params=_cp(("parallel",)))(w, m, v, gparts)


def _pack(arrays, rows):
    flat = jnp.concatenate([a.reshape(-1).astype(F32) for a in arrays])
    return jnp.pad(flat, (0, rows * PACK_COLS - flat.shape[0])).reshape(rows, PACK_COLS)


def _unpack(buf, shapes):
    flat = buf.reshape(-1)
    out, off = [], 0
    for shp in shapes:
        size = int(np.prod(shp))
        out.append(flat[off:off + size].reshape(shp))
        off += size
    return out


REPL_SHAPES = {'mix_norm_g': (2, 1024), 'ffn_norm_g': (2, 1024), 'final_norm_g': (1024,), 's5_lam_re': (1, 16, 64),
               's5_lam_im': (1, 16, 64), 's5_log_dt': (1, 16), 's5_b_re': (1, 16, 64, 16), 's5_b_im': (1, 16, 64, 16),
               's5_c_re': (1, 16, 16, 64), 's5_c_im': (1, 16, 16, 64), 's5_d': (1, 256), 's5_b_glu': (1, 256),
               'gm_w_s': (1, 6, 128, 128), 'gm_b_s': (1, 6, 128), 'gm_v_g': (1, 768), 'ffn_conv_b': (2, 5632)}
REPL_ELEMS = sum(int(np.prod(REPL_SHAPES[n])) for n in REPL_ORDER)
REPL_ROWS = -(-REPL_ELEMS // (PACK_COLS * 8)) * 8

GATHER_DTYPE = {'ev_w_in': BF, 'ev_w_out': BF, 's5_w_glu': BF, 'od_w_in': BF, 'od_conv_w': F32, 'od_conv_b': F32,
                'od_w_out': BF, 'ffn_w_up': BF, 'ffn_conv_w': F32, 'ffn_w_down': BF}
GATHER_EVEN = ['ev_w_in', 'ev_w_out', 's5_w_glu']
GATHER_FFN0 = ['ffn_w_up0', 'ffn_conv_w0', 'ffn_w_down0']
GATHER_REST = ['od_w_in', 'od_conv_w', 'od_conv_b', 'od_w_out', 'ffn_w_up1', 'ffn_conv_w1', 'ffn_w_down1']
GRAD_PIECES = [('ev_w_in', 'ev_w_in', None), ('ev_w_out', 'ev_w_out', None), ('s5_w_glu', 's5_w_glu', None),
               ('od_w_in', 'od_w_in', None), ('od_conv_w', 'od_conv_w', None), ('od_conv_b', 'od_conv_b', None),
               ('od_w_out', 'od_w_out', None), ('ffn_w_up0', 'ffn_w_up', 0), ('ffn_w_up1', 'ffn_w_up', 1),
               ('ffn_conv_w0', 'ffn_conv_w', 0), ('ffn_conv_w1', 'ffn_conv_w', 1), ('ffn_w_down0', 'ffn_w_down', 0),
               ('ffn_w_down1', 'ffn_w_down', 1)]


def _squeeze_lead(a):
    return a.reshape(a.shape[1:]) if a.shape[0] == 1 and a.ndim > 2 else a


def kernel(x, mix_norm_g, ffn_norm_g, final_norm_g, ev_w_in, ev_w_out, s5_lam_re, s5_lam_im, s5_log_dt, s5_b_re, s5_b_im, s5_c_re, s5_c_im, s5_d, s5_w_glu, s5_b_glu, gm_w_s, gm_b_s, gm_v_g, od_w_in, od_conv_w, od_conv_b, od_w_out, ffn_w_up, ffn_conv_w, ffn_conv_b, ffn_w_down, loss_target, m_mix_norm_g, m_ffn_norm_g, m_final_norm_g, m_ev_w_in, m_ev_w_out, m_s5_lam_re, m_s5_lam_im, m_s5_log_dt, m_s5_b_re, m_s5_b_im, m_s5_c_re, m_s5_c_im, m_s5_d, m_s5_w_glu, m_s5_b_glu, m_gm_w_s, m_gm_b_s, m_gm_v_g, m_od_w_in, m_od_conv_w, m_od_conv_b, m_od_w_out, m_ffn_w_up, m_ffn_conv_w, m_ffn_conv_b, m_ffn_w_down, v_mix_norm_g, v_ffn_norm_g, v_final_norm_g, v_ev_w_in, v_ev_w_out, v_s5_lam_re, v_s5_lam_im, v_s5_log_dt, v_s5_b_re, v_s5_b_im, v_s5_c_re, v_s5_c_im, v_s5_d, v_s5_w_glu, v_s5_b_glu, v_gm_w_s, v_gm_b_s, v_gm_v_g, v_od_w_in, v_od_conv_w, v_od_conv_b, v_od_w_out, v_ffn_w_up, v_ffn_conv_w, v_ffn_conv_b, v_ffn_w_down):
    given = dict(locals())
    weights = {n: given[n] for n in WEIGHT_ORDER}
    nseq, seq, _ = x.shape

    send = {}
    for name in SHARDED_ORDER:
        a = weights[name].astype(GATHER_DTYPE[name])
        if a.shape[0] == 2:
            send[name + '0'], send[name + '1'] = a[0], a[1]
        else:
            send[name] = _squeeze_lead(a)
    groups = []
    for tag, names in (("ffn0", GATHER_FFN0), ("rest", GATHER_REST)):
        started = _gather_start([send[n] for n in names], f"gather_{tag}_start")
        groups.append((names, started))
    token = groups[0][1][4] + groups[1][1][4]

    def waiter(tag, names, started):
        def wait(after):
            lands = _gather_wait(started[0], started[1], started[2], started[3], after, f"gather_{tag}_wait")
            return dict(zip(names, lands))
        return wait

    gathered = dict(zip(GATHER_EVEN, _all_gather([send[n] for n in GATHER_EVEN], [0] * len(GATHER_EVEN),
                                                 "gather_even")))

    loss_row, grad_x, g_repl, g_shard = _local_step(
        x.reshape(nseq * seq, D), loss_target.reshape(nseq * seq, D), weights, gathered,
        waiter("ffn0", *groups[0]), waiter("rest", *groups[1]), token, seq)
    loss = lax.psum(loss_row[0, 0], ("x", "y", "c"))

    core = lax.axis_index("c").astype(jnp.int32).reshape(1)
    pieces = [g_shard[p[0]] for p in GRAD_PIECES]
    from_sibling = _exchange_sibling(pieces, "reduce_sibling")
    chip_sums = [_add_sibling(_as_rows(a, 1), _as_rows(b, 1), core, f"reduce_add_{p[0]}").reshape((4,) + a.shape[1:])
                 for p, a, b in zip(GRAD_PIECES, pieces, from_sibling)]
    by_chip = dict(zip([p[0] for p in GRAD_PIECES], _exchange_chips(chip_sums, "reduce_chips")))
    repl_parts = _all_gather([_pack([g_repl[n] for n in REPL_ORDER], REPL_ROWS)], [0], "gather_small_grads")[0]

    out = {}
    for name in SHARDED_ORDER:
        w = weights[name]
        if name + '0' in by_chip:
            gp = jnp.stack([by_chip[name + '0'], by_chip[name + '1']], axis=1)
        else:
            gp = by_chip[name]
        to_rows = lambda a: a.reshape(-1, w.shape[-1])
        res = _adamw(to_rows(w), to_rows(given["m_" + name]), to_rows(given["v_" + name]),
                     gp.reshape(4, -1, w.shape[-1]), f"adamw_{name}")
        out[name] = [r.reshape(w.shape) for r in res]
    rp = _adamw(_pack([weights[n] for n in REPL_ORDER], REPL_ROWS),
                _pack([given["m_" + n] for n in REPL_ORDER], REPL_ROWS),
                _pack([given["v_" + n] for n in REPL_ORDER], REPL_ROWS), repl_parts, "adamw_replicated")
    rp_shapes = [weights[n].shape for n in REPL_ORDER]
    for k in range(4):
        for name, a in zip(REPL_ORDER, _unpack(rp[k], rp_shapes)):
            out.setdefault(name, [None] * 4)[k] = a
    results = [[out[n][k] for n in WEIGHT_ORDER] for k in range(4)]
    grad_w, delta_w, new_m, new_v = results
    return (loss, grad_x.reshape(nseq, seq, D), *grad_w, *delta_w, *new_m, *new_v)
```

```python
import math

import jax
import jax.numpy as jnp
import numpy as np
from jax import lax
from jax.experimental import pallas as pl
from jax.experimental.pallas import tpu as pltpu

F32 = jnp.float32
BF = jnp.bfloat16

D = 1024
DFF = 2816
NDEV = 8
SSM_W = 256
SSM_G = 16
SSM_H = 16
SSM_P = 64
NST = SSM_G * SSM_P
GM_W = 768
GM_HEADS = 6
CHUNK = 128
EPS = 1e-6
LAM_MAX = -1e-4
FB = 256
FSH = 2 * DFF // NDEV
VMEM_LIMIT = 48 * 2**20
PACK_COLS = 1024
MESH_T = pl.DeviceIdType.MESH

ADAM_LR = 0.001
ADAM_B1 = 0.9
ADAM_B2 = 0.999
ADAM_EPS = 1e-08
ADAM_WD = 0.01
ADAM_STEP = 10

WEIGHT_ORDER = ['mix_norm_g', 'ffn_norm_g', 'final_norm_g', 'ev_w_in', 'ev_w_out', 's5_lam_re', 's5_lam_im',
                's5_log_dt', 's5_b_re', 's5_b_im', 's5_c_re', 's5_c_im', 's5_d', 's5_w_glu', 's5_b_glu', 'gm_w_s',
                'gm_b_s', 'gm_v_g', 'od_w_in', 'od_conv_w', 'od_conv_b', 'od_w_out', 'ffn_w_up', 'ffn_conv_w',
                'ffn_conv_b', 'ffn_w_down']
SHARDED = {'ev_w_in': ((1, 1024, 1792), 2), 'ev_w_out': ((1, 1024, 1024), 1), 's5_w_glu': ((1, 256, 256), 1),
           'od_w_in': ((1, 1024, 3072), 2), 'od_conv_w': ((1, 3, 1024), 2), 'od_conv_b': ((1, 1024), 1),
           'od_w_out': ((1, 1024, 1024), 1), 'ffn_w_up': ((2, 1024, 5632), 2), 'ffn_conv_w': ((2, 3, 5632), 2),
           'ffn_w_down': ((2, 2816, 1024), 1)}
SHARDED_ORDER = [n for n in WEIGHT_ORDER if n in SHARDED]
REPL_ORDER = [n for n in WEIGHT_ORDER if n not in SHARDED]


def _cp(sem):
    return pltpu.CompilerParams(dimension_semantics=sem, vmem_limit_bytes=VMEM_LIMIT)


def _sigmoid(x):
    return 1.0 / (1.0 + jnp.exp(-x))


_GELU_K = math.sqrt(2.0 / math.pi)


def _gelu(x):
    return 0.5 * x * (1.0 + jnp.tanh(_GELU_K * (x + 0.044715 * x * x * x)))


def _gelu_grad(x):
    t = jnp.tanh(_GELU_K * (x + 0.044715 * x * x * x))
    return 0.5 * (1.0 + t) + 0.5 * x * (1.0 - t * t) * _GELU_K * (1.0 + 3.0 * 0.044715 * x * x)


def _colsum(x):
    return jnp.sum(x, axis=0, keepdims=True)


def _accumulate(ref, first, part):
    @pl.when(first)
    def _():
        ref[...] = part

    @pl.when(jnp.logical_not(first))
    def _():
        ref[...] += part


_DIMS = {'nn': (((1,), (0,)), ((), ())), 'nt': (((1,), (1,)), ((), ())), 'tn': (((0,), (0,)), ((), ()))}


def _matmul(a, b, mode, tm, tn, tk, name, resid=None, out_dtype=F32):
    if mode == 'tn':
        kdim, m = a.shape
    else:
        m, kdim = a.shape
    n = b.shape[0] if mode == 'nt' else b.shape[1]
    tm, tn, tk = min(tm, m), min(tn, n), min(tk, kdim)
    assert m % tm == 0 and n % tn == 0 and kdim % tk == 0, (name, m, n, kdim, tm, tn, tk)
    a_spec = (pl.BlockSpec((tk, tm), lambda i, j, k: (k, i)) if mode == 'tn'
              else pl.BlockSpec((tm, tk), lambda i, j, k: (i, k)))
    b_spec = (pl.BlockSpec((tn, tk), lambda i, j, k: (j, k)) if mode == 'nt'
              else pl.BlockSpec((tk, tn), lambda i, j, k: (k, j)))
    o_spec = pl.BlockSpec((tm, tn), lambda i, j, k: (i, j))
    return _matmul_spec(a, b, mode, (m // tm, n // tn, kdim // tk), a_spec, b_spec, o_spec, (m, n), name,
                        resid=resid, out_dtype=out_dtype)


def _matmul_spec(a, b, mode, grid, a_spec, b_spec, o_spec, out_shape, name, resid=None, out_dtype=F32):
    nk = grid[2]
    tm, tn = o_spec.block_shape[-2:]
    dims = _DIMS[mode]
    has_resid = resid is not None

    def body(*refs):
        if has_resid:
            a_ref, b_ref, r_ref, o_ref = refs[:4]
        else:
            a_ref, b_ref, o_ref = refs[:3]
            r_ref = None
        part = lax.dot_general(a_ref[...].astype(BF), b_ref[...].astype(BF), dims, preferred_element_type=F32)
        if nk == 1:
            if has_resid:
                part = part + r_ref[...]
            o_ref[...] = part.astype(out_dtype)
        else:
            acc = refs[-1]
            k = pl.program_id(2)

            @pl.when(k == 0)
            def _():
                acc[...] = part

            @pl.when(k > 0)
            def _():
                acc[...] += part

            @pl.when(k == nk - 1)
            def _():
                tot = acc[...]
                if has_resid:
                    tot = tot + r_ref[...]
                o_ref[...] = tot.astype(out_dtype)

    operands = [a, b] + ([resid] if has_resid else [])
    in_specs = [a_spec, b_spec] + ([o_spec] if has_resid else [])
    return pl.pallas_call(
        body, name=name, grid=grid, in_specs=in_specs, out_specs=o_spec,
        out_shape=jax.ShapeDtypeStruct(out_shape, out_dtype),
        scratch_shapes=[pltpu.VMEM((tm, tn), F32)] if nk > 1 else [],
        compiler_params=_cp(("parallel", "parallel", "arbitrary")))(*operands)


def _rmsnorm_fwd(x, g, name):
    n = x.shape[0]
    tm = min(512, n)

    def body(x_ref, g_ref, o_ref):
        xv = x_ref[...]
        r = lax.rsqrt(jnp.mean(xv * xv, axis=-1, keepdims=True) + EPS)
        o_ref[...] = (xv * r * g_ref[...]).astype(BF)

    return pl.pallas_call(
        body, name=name, grid=(n // tm,),
        in_specs=[pl.BlockSpec((tm, D), lambda i: (i, 0)), pl.BlockSpec((1, D), lambda i: (0, 0))],
        out_specs=pl.BlockSpec((tm, D), lambda i: (i, 0)),
        out_shape=jax.ShapeDtypeStruct((n, D), BF), compiler_params=_cp(("parallel",)))(x, g)


def _rmsnorm_bwd(x, g, dy, dres, name):
    n = x.shape[0]
    tm = min(512, n)

    def body(x_ref, g_ref, dy_ref, dr_ref, dx_ref, dg_ref):
        xv = x_ref[...]
        r = lax.rsqrt(jnp.mean(xv * xv, axis=-1, keepdims=True) + EPS)
        xh = xv * r
        dyv = dy_ref[...]
        dyg = dyv * g_ref[...]
        dx_ref[...] = dr_ref[...] + r * (dyg - xh * jnp.mean(dyg * xh, axis=-1, keepdims=True))
        _accumulate(dg_ref, pl.program_id(0) == 0, _colsum(dyv * xh))

    row = pl.BlockSpec((tm, D), lambda i: (i, 0))
    vec = pl.BlockSpec((1, D), lambda i: (0, 0))
    return pl.pallas_call(
        body, name=name, grid=(n // tm,), in_specs=[row, vec, row, row], out_specs=[row, vec],
        out_shape=[jax.ShapeDtypeStruct((n, D), F32), jax.ShapeDtypeStruct((1, D), F32)],
        compiler_params=_cp(("arbitrary",)))(x, g, dy, dres)


def _final_loss(h, g, tgt, name):
    n = h.shape[0]
    tm = min(512, n)

    def body(x_ref, g_ref, t_ref, loss_ref, dx_ref, dg_ref):
        first = pl.program_id(0) == 0
        xv = x_ref[...]
        gv = g_ref[...]
        r = lax.rsqrt(jnp.mean(xv * xv, axis=-1, keepdims=True) + EPS)
        xh = xv * r
        err = xh * gv - t_ref[...]
        part = 0.5 * jnp.sum(jnp.mean(err * err, axis=-1, keepdims=True), axis=0, keepdims=True)
        _accumulate(loss_ref, first, jnp.broadcast_to(part, (1, 128)))
        dyv = err * (1.0 / D)
        dyg = dyv * gv
        dx_ref[...] = r * (dyg - xh * jnp.mean(dyg * xh, axis=-1, keepdims=True))
        _accumulate(dg_ref, first, _colsum(dyv * xh))

    row = pl.BlockSpec((tm, D), lambda i: (i, 0))
    vec = pl.BlockSpec((1, D), lambda i: (0, 0))
    return pl.pallas_call(
        body, name=name, grid=(n // tm,), in_specs=[row, vec, row],
        out_specs=[pl.BlockSpec((1, 128), lambda i: (0, 0)), row, vec],
        out_shape=[jax.ShapeDtypeStruct((1, 128), F32), jax.ShapeDtypeStruct((n, D), F32),
                   jax.ShapeDtypeStruct((1, D), F32)],
        compiler_params=_cp(("arbitrary",)))(h, g, tgt)


def _prev_rows(x, halo_ref, lanes, scale, row):
    h7 = halo_ref[7:8, lanes] * scale
    h6 = halo_ref[6:7, lanes] * scale
    p1 = jnp.where(row == 0, h7, pltpu.roll(x, 1, 0))
    p2 = jnp.where(row == 0, h6, jnp.where(row == 1, h7, pltpu.roll(x, 2, 0)))
    return p1, p2


def _halo_maps(tm, n_rows):
    r8 = tm // 8
    last = n_rows // 8 - 1
    prev = lambda i: jnp.maximum(i * r8 - 1, 0)
    nxt = lambda i: jnp.minimum((i + 1) * r8, last)
    return prev, nxt


def _ffn_conv_fwd(up, cw, cb, seq, name):
    n = up.shape[2]
    tm = min(256, seq)
    prev, _ = _halo_maps(tm, n)

    def body(u_ref, h_ref, w_ref, b_ref, o_ref):
        i = pl.program_id(1)
        scale = jnp.where(lax.rem(i * tm, seq) == 0, 0.0, 1.0)
        row = lax.broadcasted_iota(jnp.int32, (tm, FSH), 0)
        hc = []
        for g in range(2):
            x = u_ref[g]
            p1, p2 = _prev_rows(x, h_ref.at[g], slice(None), scale, row)
            hc.append(b_ref[g] + w_ref[g, 0:1, :] * p2 + w_ref[g, 1:2, :] * p1 + w_ref[g, 2:3, :] * x)
        o_ref[...] = (hc[0] * _sigmoid(hc[0]) * hc[1]).astype(BF)

    return pl.pallas_call(
        body, name=name, grid=(4, n // tm),
        in_specs=[pl.BlockSpec((2, None, tm, FSH), lambda j, i: (0, j, i, 0)),
                  pl.BlockSpec((2, None, 8, FSH), lambda j, i: (0, j, prev(i), 0)),
                  pl.BlockSpec((2, None, 3, FSH), lambda j, i: (0, j, 0, 0)),
                  pl.BlockSpec((2, None, 1, FSH), lambda j, i: (0, j, 0, 0))],
        out_specs=pl.BlockSpec((None, tm, FSH), lambda j, i: (j, i, 0)),
        out_shape=jax.ShapeDtypeStruct((4, n, FSH), BF), compiler_params=_cp(("parallel", "parallel")))(up, up, cw, cb)


def _ffn_conv_bwd(up, dact, cw, cb, seq, name):
    n = up.shape[2]
    tm = min(256, seq)
    ext = tm + 16
    prev, nxt = _halo_maps(tm, n)

    def body(u_ref, up_ref, un_ref, da_ref, dn_ref, w_ref, b_ref, du_ref, dw_ref, db_ref, ux, dx):
        i = pl.program_id(1)
        sp = jnp.where(lax.rem(i * tm, seq) == 0, 0.0, 1.0)
        sn = jnp.where(lax.rem((i + 1) * tm, seq) == 0, 0.0, 1.0)
        main = slice(8, 8 + tm)
        dx[0:8, :] = jnp.zeros((8, FSH), F32)
        dx[main, :] = da_ref[...]
        dx[8 + tm:, :] = dn_ref[...] * sn
        x0, x1, x2, hc = [], [], [], []
        for g in range(2):
            ux[g, 0:8, :] = up_ref[g] * sp
            ux[g, main, :] = u_ref[g]
            ux[g, 8 + tm:, :] = un_ref[g]
            x0.append(ux[g])
            x1.append(pltpu.roll(x0[g], 1, 0))
            x2.append(pltpu.roll(x0[g], 2, 0))
            hc.append(b_ref[g] + w_ref[g, 0:1, :] * x2[g] + w_ref[g, 1:2, :] * x1[g] + w_ref[g, 2:3, :] * x0[g])
        s = _sigmoid(hc[0])
        da = dx[...]
        dhc = (da * hc[1] * (s * (1.0 + hc[0] * (1.0 - s))), da * (hc[0] * s))
        first = i == 0
        for g in range(2):
            dh = dhc[g]
            dup = (w_ref[g, 2:3, :] * dh + w_ref[g, 1:2, :] * pltpu.roll(dh, ext - 1, 0)
                   + w_ref[g, 0:1, :] * pltpu.roll(dh, ext - 2, 0))
            du_ref[g] = dup[main].astype(BF)
            dm = dh[main]
            _accumulate(dw_ref.at[g, 0:1, :], first, _colsum(dm * x2[g][main]))
            _accumulate(dw_ref.at[g, 1:2, :], first, _colsum(dm * x1[g][main]))
            _accumulate(dw_ref.at[g, 2:3, :], first, _colsum(dm * x0[g][main]))
            _accumulate(db_ref.at[g], first, _colsum(dm))

    return pl.pallas_call(
        body, name=name, grid=(4, n // tm),
        in_specs=[pl.BlockSpec((2, None, tm, FSH), lambda j, i: (0, j, i, 0)),
                  pl.BlockSpec((2, None, 8, FSH), lambda j, i: (0, j, prev(i), 0)),
                  pl.BlockSpec((2, None, 8, FSH), lambda j, i: (0, j, nxt(i), 0)),
                  pl.BlockSpec((None, tm, FSH), lambda j, i: (j, i, 0)),
                  pl.BlockSpec((None, 8, FSH), lambda j, i: (j, nxt(i), 0)),
                  pl.BlockSpec((2, None, 3, FSH), lambda j, i: (0, j, 0, 0)),
                  pl.BlockSpec((2, None, 1, FSH), lambda j, i: (0, j, 0, 0))],
        out_specs=[pl.BlockSpec((2, None, tm, FSH), lambda j, i: (0, j, i, 0)),
                   pl.BlockSpec((2, None, 3, FSH), lambda j, i: (0, j, 0, 0)),
                   pl.BlockSpec((2, None, 1, FSH), lambda j, i: (0, j, 0, 0))],
        out_shape=[jax.ShapeDtypeStruct((2, 4, n, FSH), BF), jax.ShapeDtypeStruct((2, 4, 3, FSH), F32),
                   jax.ShapeDtypeStruct((2, 4, 1, FSH), F32)],
        scratch_shapes=[pltpu.VMEM((2, ext, FSH), F32), pltpu.VMEM((ext, FSH), F32)],
        compiler_params=_cp(("parallel", "arbitrary")))(up, up, up, dact, dact, cw, cb)


def _shortconv_fwd(p, cw, cb, seq, name):
    n = p.shape[0]
    tm = min(256, seq)
    prev, _ = _halo_maps(tm, n)

    def body(p_ref, h_ref, w_ref, b_ref, o_ref):
        i = pl.program_id(1)
        scale = jnp.where(lax.rem(i * tm, seq) == 0, 0.0, 1.0)
        q = p_ref[:, FB:2 * FB] * p_ref[:, 2 * FB:]
        row = lax.broadcasted_iota(jnp.int32, q.shape, 0)
        h7 = h_ref[7:8, FB:2 * FB] * h_ref[7:8, 2 * FB:] * scale
        h6 = h_ref[6:7, FB:2 * FB] * h_ref[6:7, 2 * FB:] * scale
        p1 = jnp.where(row == 0, h7, pltpu.roll(q, 1, 0))
        p2 = jnp.where(row == 0, h6, jnp.where(row == 1, h7, pltpu.roll(q, 2, 0)))
        conv = b_ref[...] + w_ref[0:1, :] * p2 + w_ref[1:2, :] * p1 + w_ref[2:3, :] * q
        o_ref[...] = (p_ref[:, :FB] * conv).astype(BF)

    return pl.pallas_call(
        body, name=name, grid=(D // FB, n // tm),
        in_specs=[pl.BlockSpec((tm, 3 * FB), lambda j, i: (i, j)),
                  pl.BlockSpec((8, 3 * FB), lambda j, i: (prev(i), j)),
                  pl.BlockSpec((3, FB), lambda j, i: (0, j)),
                  pl.BlockSpec((1, FB), lambda j, i: (0, j))],
        out_specs=pl.BlockSpec((tm, FB), lambda j, i: (i, j)),
        out_shape=jax.ShapeDtypeStruct((n, D), BF), compiler_params=_cp(("parallel", "parallel")))(p, p, cw, cb)


def _shortconv_bwd(p, dmix, cw, cb, seq, name):
    n = p.shape[0]
    tm = min(256, seq)
    ext = tm + 16
    prev, nxt = _halo_maps(tm, n)

    def body(p_ref, pp_ref, pn_ref, dm_ref, dn_ref, w_ref, b_ref, dp_ref, dw_ref, db_ref, qx, cx):
        i = pl.program_id(1)
        sp = jnp.where(lax.rem(i * tm, seq) == 0, 0.0, 1.0)
        sn = jnp.where(lax.rem((i + 1) * tm, seq) == 0, 0.0, 1.0)
        bg, cg, hx = p_ref[:, :FB], p_ref[:, FB:2 * FB], p_ref[:, 2 * FB:]
        dm = dm_ref[...]
        qx[0:8, :] = pp_ref[:, FB:2 * FB] * pp_ref[:, 2 * FB:] * sp
        qx[8:8 + tm, :] = cg * hx
        qx[8 + tm:, :] = jnp.zeros((8, FB), F32)
        cx[0:8, :] = jnp.zeros((8, FB), F32)
        cx[8:8 + tm, :] = dm * bg
        cx[8 + tm:, :] = dn_ref[...] * pn_ref[:, :FB] * sn
        q0 = qx[...]
        q1 = pltpu.roll(q0, 1, 0)
        q2 = pltpu.roll(q0, 2, 0)
        main = slice(8, 8 + tm)
        conv = b_ref[...] + w_ref[0:1, :] * q2[main] + w_ref[1:2, :] * q1[main] + w_ref[2:3, :] * q0[main]
        dc = cx[...]
        dq = (w_ref[2:3, :] * dc + w_ref[1:2, :] * pltpu.roll(dc, ext - 1, 0)
              + w_ref[0:1, :] * pltpu.roll(dc, ext - 2, 0))[main]
        dp_ref[:, :FB] = (dm * conv).astype(BF)
        dp_ref[:, FB:2 * FB] = (dq * hx).astype(BF)
        dp_ref[:, 2 * FB:] = (dq * cg).astype(BF)
        first = i == 0
        dcm = dc[main]
        _accumulate(dw_ref.at[0:1, :], first, _colsum(dcm * q2[main]))
        _accumulate(dw_ref.at[1:2, :], first, _colsum(dcm * q1[main]))
        _accumulate(dw_ref.at[2:3, :], first, _colsum(dcm * q0[main]))
        _accumulate(db_ref, first, _colsum(dcm))

    return pl.pallas_call(
        body, name=name, grid=(D // FB, n // tm),
        in_specs=[pl.BlockSpec((tm, 3 * FB), lambda j, i: (i, j)),
                  pl.BlockSpec((8, 3 * FB), lambda j, i: (prev(i), j)),
                  pl.BlockSpec((8, 3 * FB), lambda j, i: (nxt(i), j)),
                  pl.BlockSpec((tm, FB), lambda j, i: (i, j)),
                  pl.BlockSpec((8, FB), lambda j, i: (nxt(i), j)),
                  pl.BlockSpec((3, FB), lambda j, i: (0, j)),
                  pl.BlockSpec((1, FB), lambda j, i: (0, j))],
        out_specs=[pl.BlockSpec((tm, 3 * FB), lambda j, i: (i, j)),
                   pl.BlockSpec((3, FB), lambda j, i: (0, j)),
                   pl.BlockSpec((1, FB), lambda j, i: (0, j))],
        out_shape=[jax.ShapeDtypeStruct((n, 3 * D), BF), jax.ShapeDtypeStruct((3, D), F32),
                   jax.ShapeDtypeStruct((1, D), F32)],
        scratch_shapes=[pltpu.VMEM((ext, FB), F32), pltpu.VMEM((ext, FB), F32)],
        compiler_params=_cp(("parallel", "arbitrary")))(p, p, p, dmix, dmix, cw, cb)


def _gmlp_fwd(uv, wm, bst, gv, seq, name):
    n = uv.shape[0]
    tm = min(256, seq)

    def body(x_ref, w_ref, b_ref, g_ref, o_ref):
        ge_v = _gelu(x_ref[:, GM_W:])
        r = lax.rsqrt(jnp.mean(ge_v * ge_v, axis=-1, keepdims=True) + EPS)
        vn = (ge_v * r * g_ref[...]).astype(BF)
        for c in range(tm // CHUNK):
            rows = slice(c * CHUNK, (c + 1) * CHUNK)
            for h in range(GM_HEADS):
                cols = slice(h * CHUNK, (h + 1) * CHUNK)
                gate = jnp.dot(w_ref[h], vn[rows, cols], preferred_element_type=F32) + b_ref[:, h:h + 1]
                o_ref[rows, cols] = (_gelu(x_ref[rows, cols]) * gate).astype(BF)

    return pl.pallas_call(
        body, name=name, grid=(n // tm,),
        in_specs=[pl.BlockSpec((tm, 2 * GM_W), lambda i: (i, 0)),
                  pl.BlockSpec((GM_HEADS, CHUNK, CHUNK), lambda i: (0, 0, 0)),
                  pl.BlockSpec((CHUNK, GM_HEADS), lambda i: (0, 0)),
                  pl.BlockSpec((1, GM_W), lambda i: (0, 0))],
        out_specs=pl.BlockSpec((tm, GM_W), lambda i: (i, 0)),
        out_shape=jax.ShapeDtypeStruct((n, GM_W), BF), compiler_params=_cp(("parallel",)))(uv, wm, bst, gv)


def _gmlp_bwd(uv, dout, wm, wmt, bst, gv, seq, name):
    n = uv.shape[0]
    tm = min(256, seq)

    def body(x_ref, do_ref, w_ref, wt_ref, b_ref, g_ref, dx_ref, dw_ref, db_ref, dg_ref, dvn_scr):
        first = pl.program_id(0) == 0
        ge_v = _gelu(x_ref[:, GM_W:])
        r = lax.rsqrt(jnp.mean(ge_v * ge_v, axis=-1, keepdims=True) + EPS)
        vh = ge_v * r
        vn = (vh * g_ref[...]).astype(BF)
        tril = (lax.broadcasted_iota(jnp.int32, (CHUNK, CHUNK), 0)
                >= lax.broadcasted_iota(jnp.int32, (CHUNK, CHUNK), 1))
        for h in range(GM_HEADS):
            cols = slice(h * CHUNK, (h + 1) * CHUNK)
            dw = jnp.zeros((CHUNK, CHUNK), F32)
            dbs = jnp.zeros((CHUNK, 1), F32)
            for c in range(tm // CHUNK):
                rows = slice(c * CHUNK, (c + 1) * CHUNK)
                blk = vn[rows, cols]
                gate = jnp.dot(w_ref[h], blk, preferred_element_type=F32) + b_ref[:, h:h + 1]
                xu = x_ref[rows, cols]
                do = do_ref[rows, cols]
                dx_ref[rows, cols] = (do * gate * _gelu_grad(xu)).astype(BF)
                dgate = do * _gelu(xu)
                dgb = dgate.astype(BF)
                dw = dw + lax.dot_general(dgb, blk, _DIMS['nt'], preferred_element_type=F32)
                dbs = dbs + jnp.sum(dgate, axis=1, keepdims=True)
                dvn_scr[rows, cols] = jnp.dot(wt_ref[h], dgb, preferred_element_type=F32)
            _accumulate(dw_ref.at[h], first, jnp.where(tril, dw, 0.0))
            _accumulate(db_ref.at[h], first, dbs)
        dvn = dvn_scr[...]
        _accumulate(dg_ref, first, _colsum(dvn * vh))
        dvh = dvn * g_ref[...]
        dv = r * (dvh - vh * jnp.mean(dvh * vh, axis=-1, keepdims=True))
        dx_ref[:, GM_W:] = (dv * _gelu_grad(x_ref[:, GM_W:])).astype(BF)

    full3 = pl.BlockSpec((GM_HEADS, CHUNK, CHUNK), lambda i: (0, 0, 0))
    return pl.pallas_call(
        body, name=name, grid=(n // tm,),
        in_specs=[pl.BlockSpec((tm, 2 * GM_W), lambda i: (i, 0)), pl.BlockSpec((tm, GM_W), lambda i: (i, 0)),
                  full3, full3, pl.BlockSpec((CHUNK, GM_HEADS), lambda i: (0, 0)),
                  pl.BlockSpec((1, GM_W), lambda i: (0, 0))],
        out_specs=[pl.BlockSpec((tm, 2 * GM_W), lambda i: (i, 0)), full3,
                   pl.BlockSpec((GM_HEADS, CHUNK, 1), lambda i: (0, 0, 0)),
                   pl.BlockSpec((1, GM_W), lambda i: (0, 0))],
        out_shape=[jax.ShapeDtypeStruct((n, 2 * GM_W), BF), jax.ShapeDtypeStruct((GM_HEADS, CHUNK, CHUNK), F32),
                   jax.ShapeDtypeStruct((GM_HEADS, CHUNK, 1), F32), jax.ShapeDtypeStruct((1, GM_W), F32)],
        scratch_shapes=[pltpu.VMEM((tm, GM_W), F32)],
        compiler_params=_cp(("arbitrary",)))(uv, dout, wm, wmt, bst, gv)


def _s5_disc(lam_re, lam_im, log_dt, b_re, b_im):
    lr = jnp.minimum(lam_re, LAM_MAX)
    li = lam_im
    dt = jnp.exp(log_dt)
    mag = jnp.exp(lr * dt)
    ab_re = mag * jnp.cos(li * dt)
    ab_im = mag * jnp.sin(li * dt)
    den = lr * lr + li * li
    nr = ab_re - 1.0
    ni = ab_im
    z_re = (nr * lr + ni * li) / den
    z_im = (ni * lr - nr * li) / den
    return ab_re, ab_im, z_re * b_re - z_im * b_im, z_re * b_im + z_im * b_re


def _s5_disc_fwd(args, name):
    shp = jax.ShapeDtypeStruct(args[0].shape, F32)

    def body(*refs):
        outs = _s5_disc(*[r[...] for r in refs[:5]])
        for o_ref, o in zip(refs[5:], outs):
            o_ref[...] = o

    return pl.pallas_call(body, name=name, out_shape=[shp] * 4)(*args)


def _s5_disc_bwd(args, cts, name):
    shp = jax.ShapeDtypeStruct(args[0].shape, F32)

    def body(*refs):
        _, vjp = jax.vjp(_s5_disc, *[r[...] for r in refs[:5]])
        grads = vjp(tuple(r[...] for r in refs[5:9]))
        for o_ref, o in zip(refs[9:], grads):
            o_ref[...] = o

    return pl.pallas_call(body, name=name, out_shape=[shp] * 5)(*args, *cts)


def _cmul(a, b):
    return a[0] * b[0] - a[1] * b[1], a[0] * b[1] + a[1] * b[0]


def _scan_tables(ar, ai, reverse):
    if reverse:
        ai = -ai
    a1 = (ar, ai)
    a2 = _cmul(a1, a1)
    a3 = _cmul(a2, a1)
    a4 = _cmul(a2, a2)
    powers = [a1, a2, a3, a4, _cmul(a4, a1), _cmul(a4, a2), _cmul(a4, a3), _cmul(a4, a4)]
    row = lax.broadcasted_iota(jnp.int32, (8, NST), 0)
    zero = jnp.zeros((8, NST), F32)
    pr, pi = zero, zero
    for r in range(8):
        pw = powers[7 - r] if reverse else powers[r]
        pr = jnp.where(row == r, pw[0], pr)
        pi = jnp.where(row == r, pw[1], pi)
    levels = []
    for d, pw in ((1, a1), (2, a2), (4, a4)):
        ok = (row <= 7 - d) if reverse else (row >= d)
        levels.append((d, jnp.where(ok, pw[0], zero), jnp.where(ok, pw[1], zero)))
    return (pr, pi), levels


def _scan_block(src, dst, car, tables, n_tiles, reverse):
    (pr, pi), levels = tables
    row = lax.broadcasted_iota(jnp.int32, (8, NST), 0)
    out_row = 0 if reverse else 7

    def step(t, carry):
        cr, ci = carry
        tile = (n_tiles - 1 - t) if reverse else t
        rows = pl.ds(pl.multiple_of(tile * 8, 8), 8)
        xr = src[rows, 0:NST]
        xi = src[rows, NST:2 * NST]
        for d, dr, di in levels:
            shift = 8 - d if reverse else d
            rr = pltpu.roll(xr, shift, 0)
            ri = pltpu.roll(xi, shift, 0)
            xr, xi = xr + dr * rr - di * ri, xi + dr * ri + di * rr
        hr = xr + pr * cr - pi * ci
        hi = xi + pr * ci + pi * cr
        dst[rows, 0:NST] = hr
        dst[rows, NST:2 * NST] = hi
        return (_colsum(jnp.where(row == out_row, hr, 0.0)), _colsum(jnp.where(row == out_row, hi, 0.0)))

    cr, ci = lax.fori_loop(0, n_tiles, step, (car[0:1, 0:NST], car[0:1, NST:2 * NST]))
    car[0:1, 0:NST] = cr
    car[0:1, NST:2 * NST] = ci


def _s5_fwd(u, ab, bbt, cmat, dvec, wglu, bglu, seq, name):
    n = u.shape[0]
    tm = min(256, seq)

    def body(u_ref, ab_ref, bb_ref, c_ref, d_ref, w_ref, b_ref, h_ref, o_ref, xs, car):
        i = pl.program_id(0)

        @pl.when(lax.rem(i * tm, seq) == 0)
        def _():
            car[...] = jnp.zeros(car.shape, F32)

        uv = u_ref[...]
        xs[...] = jnp.dot(uv.astype(BF), bb_ref[...], preferred_element_type=F32)
        tables = _scan_tables(ab_ref[0:1, 0:NST], ab_ref[0:1, NST:2 * NST], False)
        _scan_block(xs, h_ref, car, tables, tm // 8, False)
        y = jnp.dot(h_ref[...].astype(BF), c_ref[...], preferred_element_type=F32) + d_ref[...] * uv
        g1 = _gelu(y)
        z = jnp.dot(g1.astype(BF), w_ref[...], preferred_element_type=F32) + b_ref[...]
        o_ref[...] = (g1 * _sigmoid(z)).astype(BF)

    const = lambda shape: pl.BlockSpec(shape, lambda i: (0, 0))
    return pl.pallas_call(
        body, name=name, grid=(n // tm,),
        in_specs=[pl.BlockSpec((tm, SSM_W), lambda i: (i, 0)), const((1, 2 * NST)), const((SSM_W, 2 * NST)),
                  const((2 * NST, SSM_W)), const((1, SSM_W)), const((SSM_W, SSM_W)), const((1, SSM_W))],
        out_specs=[pl.BlockSpec((tm, 2 * NST), lambda i: (i, 0)), pl.BlockSpec((tm, SSM_W), lambda i: (i, 0))],
        out_shape=[jax.ShapeDtypeStruct((n, 2 * NST), F32), jax.ShapeDtypeStruct((n, SSM_W), BF)],
        scratch_shapes=[pltpu.VMEM((tm, 2 * NST), F32), pltpu.VMEM((8, 2 * NST), F32)],
        compiler_params=_cp(("arbitrary",)))(u, ab, bbt, cmat, dvec, wglu, bglu)


def _s5_bwd(da, u, hst, ab, bbt, cmat, dvec, wglu, bglu, seq, name):
    n = u.shape[0]
    tm = min(256, seq)
    nb = n // tm
    blk = lambda r: nb - 1 - r
    prev, _ = _halo_maps(tm, n)

    def body(da_ref, u_ref, h_ref, hp_ref, ab_ref, bb_ref, c_ref, d_ref, w_ref, b_ref,
             du_ref, dw_ref, dbg_ref, dd_ref, dc_ref, dbb_ref, dab_ref, gs, car):
        r = pl.program_id(0)
        i = blk(r)
        first = r == 0

        @pl.when(lax.rem((i + 1) * tm, seq) == 0)
        def _():
            car[...] = jnp.zeros(car.shape, F32)

        uv = u_ref[...]
        dav = da_ref[...]
        hb = h_ref[...]
        hb16 = hb.astype(BF)
        dvv = d_ref[...]
        y = jnp.dot(hb16, c_ref[...], preferred_element_type=F32) + dvv * uv
        g1 = _gelu(y)
        g16 = g1.astype(BF)
        s = _sigmoid(jnp.dot(g16, w_ref[...], preferred_element_type=F32) + b_ref[...])
        dz = dav * g1 * s * (1.0 - s)
        dz16 = dz.astype(BF)
        dg1 = dav * s + lax.dot_general(dz16, w_ref[...], _DIMS['nt'], preferred_element_type=F32)
        _accumulate(dw_ref, first, lax.dot_general(g16, dz16, _DIMS['tn'], preferred_element_type=F32))
        _accumulate(dbg_ref, first, _colsum(dz))
        dy = dg1 * _gelu_grad(y)
        dy16 = dy.astype(BF)
        _accumulate(dd_ref, first, _colsum(dy * uv))
        _accumulate(dc_ref, first, lax.dot_general(hb16, dy16, _DIMS['tn'], preferred_element_type=F32))
        gs[...] = lax.dot_general(dy16, c_ref[...], _DIMS['nt'], preferred_element_type=F32)
        tables = _scan_tables(ab_ref[0:1, 0:NST], ab_ref[0:1, NST:2 * NST], True)
        _scan_block(gs, gs, car, tables, tm // 8, True)
        g = gs[...]
        g16b = g.astype(BF)
        sp = jnp.where(lax.rem(i * tm, seq) == 0, 0.0, 1.0)
        row = lax.broadcasted_iota(jnp.int32, hb.shape, 0)
        hprev = jnp.where(row == 0, hp_ref[7:8, :] * sp, pltpu.roll(hb, 1, 0))
        gr, gi = g[:, :NST], g[:, NST:]
        hr, hi = hprev[:, :NST], hprev[:, NST:]
        _accumulate(dab_ref.at[:, 0:NST], first, _colsum(gr * hr + gi * hi))
        _accumulate(dab_ref.at[:, NST:2 * NST], first, _colsum(gi * hr - gr * hi))
        _accumulate(dbb_ref, first, lax.dot_general(uv.astype(BF), g16b, _DIMS['tn'], preferred_element_type=F32))
        du = dy * dvv + lax.dot_general(g16b, bb_ref[...], _DIMS['nt'], preferred_element_type=F32)
        du_ref[...] = du.astype(BF)

    const = lambda shape: pl.BlockSpec(shape, lambda r: (0, 0))
    rowspec = lambda w: pl.BlockSpec((tm, w), lambda r: (blk(r), 0))
    return pl.pallas_call(
        body, name=name, grid=(nb,),
        in_specs=[rowspec(SSM_W), rowspec(SSM_W), rowspec(2 * NST),
                  pl.BlockSpec((8, 2 * NST), lambda r: (prev(blk(r)), 0)),
                  const((1, 2 * NST)), const((SSM_W, 2 * NST)), const((2 * NST, SSM_W)), const((1, SSM_W)),
                  const((SSM_W, SSM_W)), const((1, SSM_W))],
        out_specs=[rowspec(SSM_W), const((SSM_W, SSM_W)), const((1, SSM_W)), const((1, SSM_W)),
                   const((2 * NST, SSM_W)), const((SSM_W, 2 * NST)), const((1, 2 * NST))],
        out_shape=[jax.ShapeDtypeStruct((n, SSM_W), BF), jax.ShapeDtypeStruct((SSM_W, SSM_W), F32),
                   jax.ShapeDtypeStruct((1, SSM_W), F32), jax.ShapeDtypeStruct((1, SSM_W), F32),
                   jax.ShapeDtypeStruct((2 * NST, SSM_W), F32), jax.ShapeDtypeStruct((SSM_W, 2 * NST), F32),
                   jax.ShapeDtypeStruct((1, 2 * NST), F32)],
        scratch_shapes=[pltpu.VMEM((tm, 2 * NST), F32), pltpu.VMEM((8, 2 * NST), F32)],
        compiler_params=_cp(("arbitrary",)))(da, u, hst, hst, ab, bbt, cmat, dvec, wglu, bglu)


def _s5_rows(lam_re, lam_im, log_dt, b_re, b_im):
    rep = lambda a: jnp.broadcast_to(a[:, None, :], (SSM_G, SSM_H, SSM_P)).reshape(SSM_W, SSM_P)
    dt = jnp.broadcast_to(log_dt[:, None, None], (SSM_G, SSM_H, SSM_P)).reshape(SSM_W, SSM_P)
    tr = lambda b: b.transpose(0, 2, 1).reshape(SSM_W, SSM_P)
    return rep(lam_re), rep(lam_im), dt, tr(b_re), tr(b_im)


def _block_diag(rows_gp, inner):
    eye = jnp.eye(SSM_G, dtype=rows_gp.dtype)
    return (rows_gp[:, :, None, :] * eye[:, None, :, None]).reshape(SSM_G * inner, SSM_G * SSM_P)


def _diag_blocks(mat, inner):
    m4 = mat.reshape(SSM_G, inner, SSM_G, SSM_P)
    return jnp.stack([m4[g, :, g, :] for g in range(SSM_G)])


def _interleave(w, parts):
    lead = w.shape[:-1]
    nb = w.shape[-1] // (parts * FB)
    return jnp.swapaxes(w.reshape(lead + (parts, nb, FB)), -3, -2).reshape(w.shape)


def _deinterleave(w, parts):
    lead = w.shape[:-1]
    nb = w.shape[-1] // (parts * FB)
    return jnp.swapaxes(w.reshape(lead + (nb, parts, FB)), -3, -2).reshape(w.shape)


def _ffn_fwd(h, g, w_up, w_down, cw, cb, seq, tag):
    n = h.shape[0]
    tm = min(1024, n)
    ni = n // tm
    f = _rmsnorm_fwd(h, g, f"{tag}_norm")
    up = _matmul_spec(
        f, w_up, 'nn', (NDEV, ni, 1),
        pl.BlockSpec((tm, D), lambda s, i, k: (i, 0)),
        pl.BlockSpec((D, FSH), lambda s, i, k: (s, 0)),
        pl.BlockSpec((tm, FSH), lambda s, i, k: (s * ni + i, 0)), (NDEV * n, FSH), f"{tag}_up")
    up = up.reshape(2, 4, n, FSH)
    act = _ffn_conv_fwd(up, cw, cb, seq, f"{tag}_conv")
    tn = 512
    out = _matmul_spec(
        act.reshape(4 * n, FSH), w_down, 'nn', (ni, D // tn, 4),
        pl.BlockSpec((tm, FSH), lambda i, j, k: (k * ni + i, 0)),
        pl.BlockSpec((FSH, tn), lambda i, j, k: (k, j)),
        pl.BlockSpec((tm, tn), lambda i, j, k: (i, j)), (n, D), f"{tag}_down", resid=h)
    return out, (f, up, act)


def _ffn_bwd(dh, h, g, w_up, w_down, cw, cb, saved, seq, tag):
    f, up, act = saved
    n = h.shape[0]
    tm = min(1024, n)
    ni = n // tm
    tk = min(1024, n)
    nk = n // tk
    dact = _matmul_spec(
        dh, w_down, 'nt', (4, ni, 1),
        pl.BlockSpec((tm, D), lambda j, i, k: (i, 0)),
        pl.BlockSpec((FSH, D), lambda j, i, k: (j, 0)),
        pl.BlockSpec((tm, FSH), lambda j, i, k: (j * ni + i, 0)), (4 * n, FSH), f"{tag}_ddown_x")
    tn = 512
    dw_down = _matmul_spec(
        act.reshape(4 * n, FSH), dh, 'tn', (4, D // tn, nk),
        pl.BlockSpec((tk, FSH), lambda j, c, k: (j * nk + k, 0)),
        pl.BlockSpec((tk, tn), lambda j, c, k: (k, c)),
        pl.BlockSpec((FSH, tn), lambda j, c, k: (j, c)), (DFF, D), f"{tag}_ddown_w", out_dtype=BF)
    dup, dcw, dcb = _ffn_conv_bwd(up, dact.reshape(4, n, FSH), cw, cb, seq, f"{tag}_dconv")
    dup2 = dup.reshape(NDEV * n, FSH)
    df = _matmul_spec(
        dup2, w_up, 'nt', (ni, 1, NDEV),
        pl.BlockSpec((tm, FSH), lambda i, j, k: (k * ni + i, 0)),
        pl.BlockSpec((D, FSH), lambda i, j, k: (k, 0)),
        pl.BlockSpec((tm, D), lambda i, j, k: (i, 0)), (n, D), f"{tag}_dup_x")
    dw_up = _matmul_spec(
        f, dup2, 'tn', (NDEV, 1, nk),
        pl.BlockSpec((tk, D), lambda s, j, k: (k, 0)),
        pl.BlockSpec((tk, FSH), lambda s, j, k: (s * nk + k, 0)),
        pl.BlockSpec((D, FSH), lambda s, j, k: (s, 0)), (NDEV * D, FSH), f"{tag}_dup_w", out_dtype=BF)
    dh_in, dg = _rmsnorm_bwd(h, g, df, dh, f"{tag}_dnorm")
    grads = dict(g=dg, w_up=dw_up.reshape(NDEV, D, FSH), w_down=dw_down.reshape(NDEV, DFF // NDEV, D),
                 cw=dcw.reshape(NDEV, 3, FSH), cb=dcb.reshape(2 * DFF))
    return dh_in, grads


def _col_shards(w, width):
    return w.reshape(w.shape[0], NDEV, width).transpose(1, 0, 2)


def _local_step(x, tgt, w, gw, wait_ffn0, wait_rest, token, scatter, seq):
    bf = lambda a: a.astype(BF)
    row = lambda a: a.reshape(1, -1).astype(F32)
    w_ev = gw['ev_w_in'].transpose(1, 0, 2).reshape(D, 1792)
    w_ev_s5, w_ev_gm = w_ev[:, :SSM_W], w_ev[:, SSM_W:]
    w_evo = gw['ev_w_out'].reshape(D, D)
    f_cb = [w['ffn_conv_b'][l].reshape(2, 4, 1, FSH) for l in range(2)]
    tril = jnp.tril(jnp.ones((CHUNK, CHUNK), dtype=bool))
    gm_w = jnp.where(tril, w['gm_w_s'][0], 0.0)
    gm_wm, gm_wmt = bf(gm_w), bf(jnp.swapaxes(gm_w, 1, 2))
    gm_bt = w['gm_b_s'][0].T
    gm_gv = row(w['gm_v_g'][0])
    s5_in = _s5_rows(w['s5_lam_re'][0], w['s5_lam_im'][0], w['s5_log_dt'][0], w['s5_b_re'][0], w['s5_b_im'][0])
    ab_re, ab_im, bb_re, bb_im = _s5_disc_fwd(s5_in, "s5_disc")
    first_h = lambda a: a.reshape(SSM_G, SSM_H, SSM_P)[:, 0, :].reshape(1, NST)
    s5_ab = jnp.concatenate([first_h(ab_re), first_h(ab_im)], axis=1)
    to_gp = lambda a: a.reshape(SSM_G, SSM_H, SSM_P)
    s5_bbt = bf(jnp.concatenate([_block_diag(to_gp(bb_re), SSM_H), _block_diag(to_gp(bb_im), SSM_H)], axis=1))
    s5_cmat = bf(jnp.concatenate([_block_diag(w['s5_c_re'][0], SSM_H).T, -_block_diag(w['s5_c_im'][0], SSM_H).T],
                                 axis=0))
    s5_d, s5_bg, s5_wg = row(w['s5_d'][0]), row(w['s5_b_glu'][0]), gw['s5_w_glu'].reshape(SSM_W, SSM_W)
    g_mix = [row(w['mix_norm_g'][0]) + token[0:1, 0:1], row(w['mix_norm_g'][1])]
    g_ffn = [row(w['ffn_norm_g'][l]) for l in range(2)]
    g_fin = row(w['final_norm_g'])

    h0 = x
    y0 = _rmsnorm_fwd(h0, g_mix[0], "ev_norm")
    p_s5 = _matmul(y0, w_ev_s5, 'nn', 1024, 256, D, "ev_in_s5")
    p_gm = _matmul(y0, w_ev_gm, 'nn', 1024, 512, D, "ev_in_gm")
    hst, a_out = _s5_fwd(p_s5, s5_ab, s5_bbt, s5_cmat, s5_d, s5_wg, s5_bg, seq, "s5_fwd")
    b_out = _gmlp_fwd(p_gm, gm_wm, gm_bt, gm_gv, seq, "gmlp_fwd")
    mixcat = jnp.concatenate([a_out, b_out], axis=1)
    h1 = _matmul(mixcat, w_evo, 'nn', 1024, 512, D, "ev_out", resid=h0)
    g0 = wait_ffn0(mixcat)
    w_up0, w_dn0 = g0['ffn_w_up0'].reshape(NDEV * D, FSH), g0['ffn_w_down0'].reshape(DFF, D)
    f_cw0 = g0['ffn_conv_w0'].reshape(2, 4, 3, FSH)
    h2, ffn0 = _ffn_fwd(h1, g_ffn[0], w_up0, w_dn0, f_cw0, f_cb[0], seq, "ffn0")
    g1 = wait_rest(h2)
    w_od = _interleave(g1['od_w_in'].transpose(1, 0, 2).reshape(D, 3 * D), 3)
    w_odo = g1['od_w_out'].reshape(D, D)
    od_cw = g1['od_conv_w'].transpose(1, 0, 2).reshape(3, D)
    od_cb = g1['od_conv_b'].reshape(1, D)
    w_up1, w_dn1 = g1['ffn_w_up1'].reshape(NDEV * D, FSH), g1['ffn_w_down1'].reshape(DFF, D)
    f_cw1 = g1['ffn_conv_w1'].reshape(2, 4, 3, FSH)
    y1 = _rmsnorm_fwd(h2, g_mix[1], "od_norm")
    p_od = _matmul(y1, w_od, 'nn', 1024, 512, D, "od_in")
    mixin = _shortconv_fwd(p_od, od_cw, od_cb, seq, "od_conv")
    h3 = _matmul(mixin, w_odo, 'nn', 1024, 512, D, "od_out", resid=h2)
    h4, ffn1 = _ffn_fwd(h3, g_ffn[1], w_up1, w_dn1, f_cw1, f_cb[1], seq, "ffn1")
    loss, dh4, dg_fin = _final_loss(h4, g_fin, tgt, "final_loss")

    dh3, gf1 = _ffn_bwd(dh4, h3, g_ffn[1], w_up1, w_dn1, f_cw1, f_cb[1], ffn1, seq, "ffn1")
    dmixin = _matmul(dh3, w_odo, 'nt', 1024, 512, D, "od_dout_x")
    dw_odo = _matmul(mixin, dh3, 'tn', D, 512, 1024, "od_dout_w", out_dtype=BF)
    dp_od, d_od_cw, d_od_cb = _shortconv_bwd(p_od, dmixin, od_cw, od_cb, seq, "od_dconv")
    dy1 = _matmul(dp_od, w_od, 'nt', 1024, D, 512, "od_din_x")
    dw_od = _matmul(y1, dp_od, 'tn', D, 512, 1024, "od_din_w", out_dtype=BF)
    sent = scatter("scatter_layer1", {
        'od_w_in': _col_shards(_deinterleave(dw_od, 3), 384), 'od_conv_w': _col_shards(d_od_cw, D // NDEV),
        'od_conv_b': d_od_cb.reshape(NDEV, 1, D // NDEV), 'od_w_out': dw_odo.reshape(NDEV, D // NDEV, D),
        'ffn_w_up1': gf1['w_up'], 'ffn_conv_w1': gf1['cw'], 'ffn_w_down1': gf1['w_down']})
    dh2, dg_mix1 = _rmsnorm_bwd(h2, g_mix[1] + sent[0:1, 0:1], dy1, dh3, "od_dnorm")
    dh1, gf0 = _ffn_bwd(dh2, h1, g_ffn[0], w_up0, w_dn0, f_cw0, f_cb[0], ffn0, seq, "ffn0")
    dmix_a = _matmul(dh1, w_evo[:SSM_W], 'nt', 1024, SSM_W, D, "ev_dout_xa")
    dmix_b = _matmul(dh1, w_evo[SSM_W:], 'nt', 1024, GM_W, D, "ev_dout_xb")
    dw_evo = _matmul(mixcat, dh1, 'tn', D, 512, 1024, "ev_dout_w", out_dtype=BF)
    sent = scatter("scatter_ffn0", {'ffn_w_up0': gf0['w_up'], 'ffn_conv_w0': gf0['cw'], 'ffn_w_down0': gf0['w_down']})
    dp_s5, d_wg, d_bg, d_d, d_cmat, d_bbt, d_ab = _s5_bwd(dmix_a, p_s5, hst, s5_ab, s5_bbt, s5_cmat,
                                                           s5_d + sent[0:1, 0:1], s5_wg, s5_bg, seq, "s5_bwd")
    dp_gm, d_gmw, d_gmb, d_gmg = _gmlp_bwd(p_gm, dmix_b, gm_wm, gm_wmt, gm_bt, gm_gv, seq, "gmlp_bwd")
    dy0 = _matmul(dp_gm, w_ev_gm, 'nt', 1024, D, 512, "ev_din_xb")
    dy0 = _matmul(dp_s5, w_ev_s5, 'nt', 1024, D, SSM_W, "ev_din_xa", resid=dy0)
    dw_ev = jnp.concatenate([_matmul(y0, dp_s5, 'tn', D, SSM_W, 1024, "ev_din_wa", out_dtype=BF),
                             _matmul(y0, dp_gm, 'tn', D, 512, 1024, "ev_din_wb", out_dtype=BF)], axis=1)
    grad_x, dg_mix0 = _rmsnorm_bwd(h0, g_mix[0], dy0, dh1, "ev_dnorm")

    put_h0 = lambda a: jnp.zeros((SSM_G, SSM_H, SSM_P), F32).at[:, 0, :].set(a.reshape(SSM_G, SSM_P)).reshape(
        SSM_W, SSM_P)
    ct = (put_h0(d_ab[:, :NST]), put_h0(d_ab[:, NST:]),
          _diag_blocks(d_bbt[:, :NST], SSM_H).reshape(SSM_W, SSM_P),
          _diag_blocks(d_bbt[:, NST:], SSM_H).reshape(SSM_W, SSM_P))
    d_lre, d_lim, d_ldt, d_bre, d_bim = _s5_disc_bwd(s5_in, ct, "s5_ddisc")
    over_h = lambda a: a.reshape(SSM_G, SSM_H, SSM_P).sum(axis=1)
    un_tr = lambda a: a.reshape(SSM_G, SSM_H, SSM_P).transpose(0, 2, 1)
    d_cre = _diag_blocks(d_cmat[:NST].T, SSM_H)
    d_cim = -_diag_blocks(d_cmat[NST:].T, SSM_H)

    repl = {
        'mix_norm_g': jnp.concatenate([dg_mix0, dg_mix1], axis=0),
        'ffn_norm_g': jnp.concatenate([gf0['g'], gf1['g']], axis=0),
        'final_norm_g': dg_fin.reshape(D),
        's5_lam_re': over_h(d_lre)[None], 's5_lam_im': over_h(d_lim)[None],
        's5_log_dt': over_h(d_ldt).sum(axis=1)[None],
        's5_b_re': un_tr(d_bre)[None], 's5_b_im': un_tr(d_bim)[None],
        's5_c_re': d_cre[None], 's5_c_im': d_cim[None],
        's5_d': d_d, 's5_b_glu': d_bg,
        'gm_w_s': d_gmw[None], 'gm_b_s': d_gmb.reshape(1, GM_HEADS, CHUNK), 'gm_v_g': d_gmg,
        'ffn_conv_b': jnp.stack([gf0['cb'], gf1['cb']]),
    }
    scatter("scatter_even", {'ev_w_in': _col_shards(dw_ev, 224), 'ev_w_out': dw_evo.reshape(NDEV, D // NDEV, D),
                             's5_w_glu': d_wg.reshape(NDEV, SSM_W // NDEV, SSM_W)})
    return loss, grad_x, repl


HBM_SPEC = pl.BlockSpec(memory_space=pltpu.HBM)


def _at_axis(ref, pos, index):
    return ref.at[(slice(None),) * pos + (index,)]


def _all_gather(shards, positions, name):
    n = len(shards)

    def body(*refs):
        xs, outs = refs[:n], refs[n:2 * n]
        send_sems, recv_sems, local_sems = refs[2 * n:]
        x, y, c = lax.axis_index("x"), lax.axis_index("y"), lax.axis_index("c")
        me, sibling = (x, y, c), (x, y, 1 - c)
        chips = [(1 - x, y), (x, 1 - y), (1 - x, 1 - y)]

        def block(p, dev):
            return _at_axis(outs[p], positions[p], 4 * dev[0] + 2 * dev[1] + dev[2])

        def copy(p, k, dev, to, src=None):
            return pltpu.make_async_remote_copy(
                src_ref=block(p, dev) if src is None else src, dst_ref=block(p, dev),
                send_sem=send_sems.at[p, k], recv_sem=recv_sems.at[p, k], device_id=to, device_id_type=MESH_T)

        mine = [pltpu.make_async_copy(xs[p], block(p, me), local_sems.at[p]) for p in range(n)]
        for cp in mine:
            cp.start()
        first = [copy(p, 0, me, sibling, src=xs[p]) for p in range(n)]
        first += [copy(p, 1 + j, me, (*chip, c), src=xs[p]) for j, chip in enumerate(chips) for p in range(n)]
        for cp in first:
            cp.start()
        passed = []
        for j, chip in enumerate(chips):
            for p in range(n):
                copy(p, 1 + j, (*chip, c), me).wait_recv()
                fwd = copy(p, 4 + j, (*chip, c), sibling)
                fwd.start()
                passed.append(fwd)
        for p in range(n):
            copy(p, 0, sibling, me).wait_recv()
        for j, chip in enumerate(chips):
            for p in range(n):
                copy(p, 4 + j, (*chip, 1 - c), me).wait_recv()
        for cp in first + passed:
            cp.wait_send()
        for cp in mine:
            cp.wait()

    out_shape = [jax.ShapeDtypeStruct(s.shape[:pos] + (NDEV,) + s.shape[pos:], s.dtype)
                 for s, pos in zip(shards, positions)]
    return pl.pallas_call(
        body, name=name, out_shape=out_shape, in_specs=[HBM_SPEC] * n, out_specs=[HBM_SPEC] * n,
        scratch_shapes=[pltpu.SemaphoreType.DMA((n, 7)), pltpu.SemaphoreType.DMA((n, 7)),
                        pltpu.SemaphoreType.DMA((n,))])(*shards)


def _other_devices(x, y, c):
    flip = lambda v, bit: 1 - v if bit else v
    return [(flip(x, k >> 2 & 1), flip(y, k >> 1 & 1), flip(c, k & 1)) for k in range(1, NDEV)]


SEM_SPEC = pl.BlockSpec(memory_space=pltpu.SEMAPHORE)
START_EFFECT = pltpu.SideEffectType.DATAFLOW_SIDE_EFFECTING


def _own_slots(arrays, scatter, name):
    n = len(arrays)

    def body(*refs):
        xs, ls, sems = refs[:n], refs[n:2 * n], refs[2 * n]
        me = 4 * lax.axis_index("x") + 2 * lax.axis_index("y") + lax.axis_index("c")
        copies = [pltpu.make_async_copy(xs[p].at[me] if scatter else xs[p], ls[p].at[me], sems.at[p])
                  for p in range(n)]
        for cp in copies:
            cp.start()
        for cp in copies:
            cp.wait()

    out_shape = [jax.ShapeDtypeStruct((NDEV,) + (a.shape[1:] if scatter else a.shape), a.dtype) for a in arrays]
    return pl.pallas_call(
        body, name=name, out_shape=out_shape, in_specs=[HBM_SPEC] * n, out_specs=[HBM_SPEC] * n,
        scratch_shapes=[pltpu.SemaphoreType.DMA((n,))])(*arrays)


def _send_start(arrays, lands, scatter, name):
    n = len(arrays)

    def body(*refs):
        xs, ls = refs[:n], refs[n:2 * n]
        send_sems, recv_sems, token = refs[2 * n], refs[2 * n + 1], refs[4 * n + 2]
        x, y, c = lax.axis_index("x"), lax.axis_index("y"), lax.axis_index("c")
        me = 4 * x + 2 * y + c
        for k, peer in enumerate(_other_devices(x, y, c)):
            for p in range(n):
                src = xs[p].at[4 * peer[0] + 2 * peer[1] + peer[2]] if scatter else xs[p]
                pltpu.make_async_remote_copy(
                    src_ref=src, dst_ref=ls[p].at[me], send_sem=send_sems.at[p * (NDEV - 1) + k],
                    recv_sem=recv_sems.at[p * (NDEV - 1) + k], device_id=peer, device_id_type=MESH_T).start()
        token[...] = jnp.zeros(token.shape, F32)

    sems = pltpu.SemaphoreType.DMA((n * (NDEV - 1),))
    out_shape = ([sems, sems] + [pltpu.HBM(a.shape, a.dtype) for a in list(arrays) + list(lands)]
                 + [jax.ShapeDtypeStruct((8, 128), F32)])
    res = pl.pallas_call(
        body, name=name, out_shape=out_shape, in_specs=[HBM_SPEC] * (2 * n),
        out_specs=[SEM_SPEC, SEM_SPEC] + [HBM_SPEC] * (2 * n) + [pl.BlockSpec(memory_space=pltpu.VMEM)],
        input_output_aliases={i: 2 + i for i in range(2 * n)},
        compiler_params=pltpu.CompilerParams(has_side_effects=START_EFFECT))(
            *[pltpu.with_memory_space_constraint(a, pltpu.HBM) for a in list(arrays) + list(lands)])
    return res[0], res[1], res[2:2 + n], res[2 + n:2 + 2 * n], res[2 + 2 * n]


def _send_wait(started, scatter, after, name):
    send_sems, recv_sems, arrays, lands, _ = started
    n = len(arrays)

    def body(*refs):
        xs, ls = refs[:n], refs[n:2 * n]
        send, recv = refs[2 * n], refs[2 * n + 1]
        x, y, c = lax.axis_index("x"), lax.axis_index("y"), lax.axis_index("c")
        for k, peer in enumerate(_other_devices(x, y, c)):
            slot = 4 * peer[0] + 2 * peer[1] + peer[2]
            for p in range(n):
                cp = pltpu.make_async_remote_copy(
                    src_ref=xs[p].at[slot] if scatter else xs[p], dst_ref=ls[p].at[slot],
                    send_sem=send.at[p * (NDEV - 1) + k], recv_sem=recv.at[p * (NDEV - 1) + k], device_id=peer,
                    device_id_type=MESH_T)
                cp.wait_send()
                cp.wait_recv()

    res = pl.pallas_call(
        body, name=name, out_shape=[pltpu.HBM(a.shape, a.dtype) for a in list(arrays) + list(lands)],
        in_specs=[HBM_SPEC] * (2 * n) + [SEM_SPEC, SEM_SPEC, pl.BlockSpec(memory_space=pl.ANY)],
        out_specs=[HBM_SPEC] * (2 * n), input_output_aliases={i: i for i in range(2 * n)},
        compiler_params=pltpu.CompilerParams(has_side_effects=START_EFFECT))(
            *arrays, *lands, send_sems, recv_sems, after)
    return res[n:]


def _exchange(arrays, scatter, tag):
    lands = _own_slots(arrays, scatter, f"{tag}_own")
    started = _send_start(arrays, lands, scatter, f"{tag}_start")
    return started, started[4]


def _row_block(rows, cols, itemsize=4, target=2**20):
    best = None
    for tr in range(16, rows + 1, 16):
        if rows % tr == 0 and tr * cols * itemsize <= target:
            best = tr
    return best or rows


def _adamw(w, m, v, gparts, name):
    parts, rows, cols = gparts.shape
    tr = _row_block(rows, cols, target=2**19)
    bc1 = 1.0 - ADAM_B1 ** ADAM_STEP
    bc2 = 1.0 - ADAM_B2 ** ADAM_STEP

    def body(w_ref, m_ref, v_ref, g_ref, go_ref, d_ref, mo_ref, vo_ref):
        g = g_ref[0].astype(F32)
        for k in range(1, parts):
            g = g + g_ref[k].astype(F32)
        mn = ADAM_B1 * m_ref[...] + (1.0 - ADAM_B1) * g
        vn = ADAM_B2 * v_ref[...] + (1.0 - ADAM_B2) * (g * g)
        go_ref[...] = g
        mo_ref[...] = mn
        vo_ref[...] = vn
        d_ref[...] = -ADAM_LR * ((mn / bc1) / (jnp.sqrt(vn / bc2) + ADAM_EPS) + ADAM_WD * w_ref[...])

    blk = pl.BlockSpec((tr, cols), lambda i: (i, 0))
    shp = jax.ShapeDtypeStruct((rows, cols), F32)
    return pl.pallas_call(
        body, name=name, grid=(rows // tr,),
        in_specs=[blk, blk, blk, pl.BlockSpec((parts, tr, cols), lambda i: (0, i, 0))],
        out_specs=[blk] * 4, out_shape=[shp] * 4, compiler_params=_cp(("parallel",)))(w, m, v, gparts)


def _pack(arrays, rows):
    flat = jnp.concatenate([a.reshape(-1).astype(F32) for a in arrays])
    return jnp.pad(flat, (0, rows * PACK_COLS - flat.shape[0])).reshape(rows, PACK_COLS)


def _unpack(buf, shapes):
    flat = buf.reshape(-1)
    out, off = [], 0
    for shp in shapes:
        size = int(np.prod(shp))
        out.append(flat[off:off + size].reshape(shp))
        off += size
    return out


REPL_SHAPES = {'mix_norm_g': (2, 1024), 'ffn_norm_g': (2, 1024), 'final_norm_g': (1024,), 's5_lam_re': (1, 16, 64),
               's5_lam_im': (1, 16, 64), 's5_log_dt': (1, 16), 's5_b_re': (1, 16, 64, 16), 's5_b_im': (1, 16, 64, 16),
               's5_c_re': (1, 16, 16, 64), 's5_c_im': (1, 16, 16, 64), 's5_d': (1, 256), 's5_b_glu': (1, 256),
               'gm_w_s': (1, 6, 128, 128), 'gm_b_s': (1, 6, 128), 'gm_v_g': (1, 768), 'ffn_conv_b': (2, 5632)}
REPL_ELEMS = sum(int(np.prod(REPL_SHAPES[n])) for n in REPL_ORDER)
REPL_ROWS = -(-REPL_ELEMS // (PACK_COLS * 8)) * 8

GATHER_DTYPE = {'ev_w_in': BF, 'ev_w_out': BF, 's5_w_glu': BF, 'od_w_in': BF, 'od_conv_w': F32, 'od_conv_b': F32,
                'od_w_out': BF, 'ffn_w_up': BF, 'ffn_conv_w': F32, 'ffn_w_down': BF}
GATHER_EVEN = ['ev_w_in', 'ev_w_out', 's5_w_glu']
GATHER_FFN0 = ['ffn_w_up0', 'ffn_conv_w0', 'ffn_w_down0']
GATHER_REST = ['od_w_in', 'od_conv_w', 'od_conv_b', 'od_w_out', 'ffn_w_up1', 'ffn_conv_w1', 'ffn_w_down1']

def _squeeze_lead(a):
    return a.reshape(a.shape[1:]) if a.shape[0] == 1 and a.ndim > 2 else a


def kernel(x, mix_norm_g, ffn_norm_g, final_norm_g, ev_w_in, ev_w_out, s5_lam_re, s5_lam_im, s5_log_dt, s5_b_re, s5_b_im, s5_c_re, s5_c_im, s5_d, s5_w_glu, s5_b_glu, gm_w_s, gm_b_s, gm_v_g, od_w_in, od_conv_w, od_conv_b, od_w_out, ffn_w_up, ffn_conv_w, ffn_conv_b, ffn_w_down, loss_target, m_mix_norm_g, m_ffn_norm_g, m_final_norm_g, m_ev_w_in, m_ev_w_out, m_s5_lam_re, m_s5_lam_im, m_s5_log_dt, m_s5_b_re, m_s5_b_im, m_s5_c_re, m_s5_c_im, m_s5_d, m_s5_w_glu, m_s5_b_glu, m_gm_w_s, m_gm_b_s, m_gm_v_g, m_od_w_in, m_od_conv_w, m_od_conv_b, m_od_w_out, m_ffn_w_up, m_ffn_conv_w, m_ffn_conv_b, m_ffn_w_down, v_mix_norm_g, v_ffn_norm_g, v_final_norm_g, v_ev_w_in, v_ev_w_out, v_s5_lam_re, v_s5_lam_im, v_s5_log_dt, v_s5_b_re, v_s5_b_im, v_s5_c_re, v_s5_c_im, v_s5_d, v_s5_w_glu, v_s5_b_glu, v_gm_w_s, v_gm_b_s, v_gm_v_g, v_od_w_in, v_od_conv_w, v_od_conv_b, v_od_w_out, v_ffn_w_up, v_ffn_conv_w, v_ffn_conv_b, v_ffn_w_down):
    given = dict(locals())
    weights = {n: given[n] for n in WEIGHT_ORDER}
    nseq, seq, _ = x.shape

    send = {}
    for name in SHARDED_ORDER:
        a = weights[name].astype(GATHER_DTYPE[name])
        if a.shape[0] == 2:
            send[name + '0'], send[name + '1'] = a[0], a[1]
        else:
            send[name] = _squeeze_lead(a)
    gathers = [_exchange([send[n] for n in names], False, f"gather_{tag}")
               for tag, names in (("ffn0", GATHER_FFN0), ("rest", GATHER_REST))]
    token = gathers[0][1] + gathers[1][1]

    def waiter(tag, names, started):
        return lambda after: dict(zip(names, _send_wait(started, False, after, f"gather_{tag}_wait")))

    gathered = dict(zip(GATHER_EVEN, _all_gather([send[n] for n in GATHER_EVEN], [0] * len(GATHER_EVEN),
                                                 "gather_even")))

    scatters = []

    def scatter(tag, grads):
        names = list(grads)
        started, sent = _exchange([grads[n].astype(BF) for n in names], True, tag)
        scatters.append((tag, names, started))
        return sent

    loss_row, grad_x, g_repl = _local_step(
        x.reshape(nseq * seq, D), loss_target.reshape(nseq * seq, D), weights, gathered,
        waiter("ffn0", GATHER_FFN0, gathers[0][0]), waiter("rest", GATHER_REST, gathers[1][0]), token, scatter, seq)
    loss = lax.psum(loss_row[0, 0], ("x", "y", "c"))

    parts = {}
    for tag, names, started in scatters:
        parts.update(zip(names, _send_wait(started, True, grad_x, f"{tag}_wait")))
    repl_parts = _all_gather([_pack([g_repl[n] for n in REPL_ORDER], REPL_ROWS)], [0], "gather_small_grads")[0]

    out = {}
    for name in SHARDED_ORDER:
        w = weights[name]
        if name + '0' in parts:
            gp = jnp.stack([parts[name + '0'], parts[name + '1']], axis=1)
        else:
            gp = parts[name]
        to_rows = lambda a: a.reshape(-1, w.shape[-1])
        res = _adamw(to_rows(w), to_rows(given["m_" + name]), to_rows(given["v_" + name]),
                     gp.reshape(NDEV, -1, w.shape[-1]), f"adamw_{name}")
        out[name] = [r.reshape(w.shape) for r in res]
    rp = _adamw(_pack([weights[n] for n in REPL_ORDER], REPL_ROWS),
                _pack([given["m_" + n] for n in REPL_ORDER], REPL_ROWS),
                _pack([given["v_" + n] for n in REPL_ORDER], REPL_ROWS), repl_parts, "adamw_replicated")
    rp_shapes = [weights[n].shape for n in REPL_ORDER]
    for k in range(4):
        for name, a in zip(REPL_ORDER, _unpack(rp[k], rp_shapes)):
            out.setdefault(name, [None] * 4)[k] = a
    results = [[out[n][k] for n in WEIGHT_ORDER] for k in range(4)]
    grad_w, delta_w, new_m, new_v = results
    return (loss, grad_x.reshape(nseq, seq, D), *grad_w, *delta_w, *new_m, *new_v)
```

```python
import math

import jax
import jax.numpy as jnp
import numpy as np
from jax import lax
from jax.experimental import pallas as pl
from jax.experimental.pallas import tpu as pltpu

F32 = jnp.float32
BF = jnp.bfloat16

D = 1024
DFF = 2816
NDEV = 8
SSM_W = 256
SSM_G = 16
SSM_H = 16
SSM_P = 64
NST = SSM_G * SSM_P
GM_W = 768
GM_HEADS = 6
CHUNK = 128
EPS = 1e-6
LAM_MAX = -1e-4
FB = 256
FSH = 2 * DFF // NDEV
VMEM_LIMIT = 48 * 2**20
PACK_COLS = 1024
MESH_T = pl.DeviceIdType.MESH

ADAM_LR = 0.001
ADAM_B1 = 0.9
ADAM_B2 = 0.999
ADAM_EPS = 1e-08
ADAM_WD = 0.01
ADAM_STEP = 10

WEIGHT_ORDER = ['mix_norm_g', 'ffn_norm_g', 'final_norm_g', 'ev_w_in', 'ev_w_out', 's5_lam_re', 's5_lam_im',
                's5_log_dt', 's5_b_re', 's5_b_im', 's5_c_re', 's5_c_im', 's5_d', 's5_w_glu', 's5_b_glu', 'gm_w_s',
                'gm_b_s', 'gm_v_g', 'od_w_in', 'od_conv_w', 'od_conv_b', 'od_w_out', 'ffn_w_up', 'ffn_conv_w',
                'ffn_conv_b', 'ffn_w_down']
SHARDED = {'ev_w_in': ((1, 1024, 1792), 2), 'ev_w_out': ((1, 1024, 1024), 1), 's5_w_glu': ((1, 256, 256), 1),
           'od_w_in': ((1, 1024, 3072), 2), 'od_conv_w': ((1, 3, 1024), 2), 'od_conv_b': ((1, 1024), 1),
           'od_w_out': ((1, 1024, 1024), 1), 'ffn_w_up': ((2, 1024, 5632), 2), 'ffn_conv_w': ((2, 3, 5632), 2),
           'ffn_w_down': ((2, 2816, 1024), 1)}
SHARDED_ORDER = [n for n in WEIGHT_ORDER if n in SHARDED]
REPL_ORDER = [n for n in WEIGHT_ORDER if n not in SHARDED]


def _cp(sem):
    return pltpu.CompilerParams(dimension_semantics=sem, vmem_limit_bytes=VMEM_LIMIT)


def _sigmoid(x):
    return 1.0 / (1.0 + jnp.exp(-x))


_GELU_K = math.sqrt(2.0 / math.pi)


def _gelu(x):
    return 0.5 * x * (1.0 + jnp.tanh(_GELU_K * (x + 0.044715 * x * x * x)))


def _gelu_grad(x):
    t = jnp.tanh(_GELU_K * (x + 0.044715 * x * x * x))
    return 0.5 * (1.0 + t) + 0.5 * x * (1.0 - t * t) * _GELU_K * (1.0 + 3.0 * 0.044715 * x * x)


def _colsum(x):
    return jnp.sum(x, axis=0, keepdims=True)


def _accumulate(ref, first, part):
    @pl.when(first)
    def _():
        ref[...] = part

    @pl.when(jnp.logical_not(first))
    def _():
        ref[...] += part


_DIMS = {'nn': (((1,), (0,)), ((), ())), 'nt': (((1,), (1,)), ((), ())), 'tn': (((0,), (0,)), ((), ()))}


def _matmul(a, b, mode, tm, tn, tk, name, resid=None, out_dtype=F32):
    if mode == 'tn':
        kdim, m = a.shape
    else:
        m, kdim = a.shape
    n = b.shape[0] if mode == 'nt' else b.shape[1]
    tm, tn, tk = min(tm, m), min(tn, n), min(tk, kdim)
    assert m % tm == 0 and n % tn == 0 and kdim % tk == 0, (name, m, n, kdim, tm, tn, tk)
    a_spec = (pl.BlockSpec((tk, tm), lambda i, j, k: (k, i)) if mode == 'tn'
              else pl.BlockSpec((tm, tk), lambda i, j, k: (i, k)))
    b_spec = (pl.BlockSpec((tn, tk), lambda i, j, k: (j, k)) if mode == 'nt'
              else pl.BlockSpec((tk, tn), lambda i, j, k: (k, j)))
    o_spec = pl.BlockSpec((tm, tn), lambda i, j, k: (i, j))
    return _matmul_spec(a, b, mode, (m // tm, n // tn, kdim // tk), a_spec, b_spec, o_spec, (m, n), name,
                        resid=resid, out_dtype=out_dtype)


def _matmul_spec(a, b, mode, grid, a_spec, b_spec, o_spec, out_shape, name, resid=None, out_dtype=F32):
    nk = grid[2]
    tm, tn = o_spec.block_shape[-2:]
    dims = _DIMS[mode]
    has_resid = resid is not None

    def body(*refs):
        if has_resid:
            a_ref, b_ref, r_ref, o_ref = refs[:4]
        else:
            a_ref, b_ref, o_ref = refs[:3]
            r_ref = None
        part = lax.dot_general(a_ref[...].astype(BF), b_ref[...].astype(BF), dims, preferred_element_type=F32)
        if nk == 1:
            if has_resid:
                part = part + r_ref[...]
            o_ref[...] = part.astype(out_dtype)
        else:
            acc = refs[-1]
            k = pl.program_id(2)

            @pl.when(k == 0)
            def _():
                acc[...] = part

            @pl.when(k > 0)
            def _():
                acc[...] += part

            @pl.when(k == nk - 1)
            def _():
                tot = acc[...]
                if has_resid:
                    tot = tot + r_ref[...]
                o_ref[...] = tot.astype(out_dtype)

    operands = [a, b] + ([resid] if has_resid else [])
    in_specs = [a_spec, b_spec] + ([o_spec] if has_resid else [])
    return pl.pallas_call(
        body, name=name, grid=grid, in_specs=in_specs, out_specs=o_spec,
        out_shape=jax.ShapeDtypeStruct(out_shape, out_dtype),
        scratch_shapes=[pltpu.VMEM((tm, tn), F32)] if nk > 1 else [],
        compiler_params=_cp(("parallel", "parallel", "arbitrary")))(*operands)


def _rmsnorm_fwd(x, g, name):
    n = x.shape[0]
    tm = min(512, n)

    def body(x_ref, g_ref, o_ref):
        xv = x_ref[...]
        r = lax.rsqrt(jnp.mean(xv * xv, axis=-1, keepdims=True) + EPS)
        o_ref[...] = (xv * r * g_ref[...]).astype(BF)

    return pl.pallas_call(
        body, name=name, grid=(n // tm,),
        in_specs=[pl.BlockSpec((tm, D), lambda i: (i, 0)), pl.BlockSpec((1, D), lambda i: (0, 0))],
        out_specs=pl.BlockSpec((tm, D), lambda i: (i, 0)),
        out_shape=jax.ShapeDtypeStruct((n, D), BF), compiler_params=_cp(("parallel",)))(x, g)


def _rmsnorm_bwd(x, g, dy, dres, name):
    n = x.shape[0]
    tm = min(512, n)

    def body(x_ref, g_ref, dy_ref, dr_ref, dx_ref, dg_ref):
        xv = x_ref[...]
        r = lax.rsqrt(jnp.mean(xv * xv, axis=-1, keepdims=True) + EPS)
        xh = xv * r
        dyv = dy_ref[...]
        dyg = dyv * g_ref[...]
        dx_ref[...] = dr_ref[...] + r * (dyg - xh * jnp.mean(dyg * xh, axis=-1, keepdims=True))
        _accumulate(dg_ref, pl.program_id(0) == 0, _colsum(dyv * xh))

    row = pl.BlockSpec((tm, D), lambda i: (i, 0))
    vec = pl.BlockSpec((1, D), lambda i: (0, 0))
    return pl.pallas_call(
        body, name=name, grid=(n // tm,), in_specs=[row, vec, row, row], out_specs=[row, vec],
        out_shape=[jax.ShapeDtypeStruct((n, D), F32), jax.ShapeDtypeStruct((1, D), F32)],
        compiler_params=_cp(("arbitrary",)))(x, g, dy, dres)


def _final_loss(h, g, tgt, name):
    n = h.shape[0]
    tm = min(512, n)

    def body(x_ref, g_ref, t_ref, loss_ref, dx_ref, dg_ref):
        first = pl.program_id(0) == 0
        xv = x_ref[...]
        gv = g_ref[...]
        r = lax.rsqrt(jnp.mean(xv * xv, axis=-1, keepdims=True) + EPS)
        xh = xv * r
        err = xh * gv - t_ref[...]
        part = 0.5 * jnp.sum(jnp.mean(err * err, axis=-1, keepdims=True), axis=0, keepdims=True)
        _accumulate(loss_ref, first, jnp.broadcast_to(part, (1, 128)))
        dyv = err * (1.0 / D)
        dyg = dyv * gv
        dx_ref[...] = r * (dyg - xh * jnp.mean(dyg * xh, axis=-1, keepdims=True))
        _accumulate(dg_ref, first, _colsum(dyv * xh))

    row = pl.BlockSpec((tm, D), lambda i: (i, 0))
    vec = pl.BlockSpec((1, D), lambda i: (0, 0))
    return pl.pallas_call(
        body, name=name, grid=(n // tm,), in_specs=[row, vec, row],
        out_specs=[pl.BlockSpec((1, 128), lambda i: (0, 0)), row, vec],
        out_shape=[jax.ShapeDtypeStruct((1, 128), F32), jax.ShapeDtypeStruct((n, D), F32),
                   jax.ShapeDtypeStruct((1, D), F32)],
        compiler_params=_cp(("arbitrary",)))(h, g, tgt)


def _prev_rows(x, halo_ref, lanes, scale, row):
    h7 = halo_ref[7:8, lanes] * scale
    h6 = halo_ref[6:7, lanes] * scale
    p1 = jnp.where(row == 0, h7, pltpu.roll(x, 1, 0))
    p2 = jnp.where(row == 0, h6, jnp.where(row == 1, h7, pltpu.roll(x, 2, 0)))
    return p1, p2


def _halo_maps(tm, n_rows):
    r8 = tm // 8
    last = n_rows // 8 - 1
    prev = lambda i: jnp.maximum(i * r8 - 1, 0)
    nxt = lambda i: jnp.minimum((i + 1) * r8, last)
    return prev, nxt


def _ffn_conv_fwd(up, cw, cb, seq, name):
    n = up.shape[2]
    tm = min(256, seq)
    prev, _ = _halo_maps(tm, n)

    def body(u_ref, h_ref, w_ref, b_ref, o_ref):
        i = pl.program_id(1)
        scale = jnp.where(lax.rem(i * tm, seq) == 0, 0.0, 1.0)
        row = lax.broadcasted_iota(jnp.int32, (tm, FSH), 0)
        hc = []
        for g in range(2):
            x = u_ref[g]
            p1, p2 = _prev_rows(x, h_ref.at[g], slice(None), scale, row)
            hc.append(b_ref[g] + w_ref[g, 0:1, :] * p2 + w_ref[g, 1:2, :] * p1 + w_ref[g, 2:3, :] * x)
        o_ref[...] = (hc[0] * _sigmoid(hc[0]) * hc[1]).astype(BF)

    return pl.pallas_call(
        body, name=name, grid=(4, n // tm),
        in_specs=[pl.BlockSpec((2, None, tm, FSH), lambda j, i: (0, j, i, 0)),
                  pl.BlockSpec((2, None, 8, FSH), lambda j, i: (0, j, prev(i), 0)),
                  pl.BlockSpec((2, None, 3, FSH), lambda j, i: (0, j, 0, 0)),
                  pl.BlockSpec((2, None, 1, FSH), lambda j, i: (0, j, 0, 0))],
        out_specs=pl.BlockSpec((None, tm, FSH), lambda j, i: (j, i, 0)),
        out_shape=jax.ShapeDtypeStruct((4, n, FSH), BF), compiler_params=_cp(("parallel", "parallel")))(up, up, cw, cb)


def _ffn_conv_bwd(up, dact, cw, cb, seq, name):
    n = up.shape[2]
    tm = min(256, seq)
    ext = tm + 16
    prev, nxt = _halo_maps(tm, n)

    def body(u_ref, up_ref, un_ref, da_ref, dn_ref, w_ref, b_ref, du_ref, dw_ref, db_ref, ux, dx):
        i = pl.program_id(1)
        sp = jnp.where(lax.rem(i * tm, seq) == 0, 0.0, 1.0)
        sn = jnp.where(lax.rem((i + 1) * tm, seq) == 0, 0.0, 1.0)
        main = slice(8, 8 + tm)
        dx[0:8, :] = jnp.zeros((8, FSH), F32)
        dx[main, :] = da_ref[...]
        dx[8 + tm:, :] = dn_ref[...] * sn
        x0, x1, x2, hc = [], [], [], []
        for g in range(2):
            ux[g, 0:8, :] = up_ref[g] * sp
            ux[g, main, :] = u_ref[g]
            ux[g, 8 + tm:, :] = un_ref[g]
            x0.append(ux[g])
            x1.append(pltpu.roll(x0[g], 1, 0))
            x2.append(pltpu.roll(x0[g], 2, 0))
            hc.append(b_ref[g] + w_ref[g, 0:1, :] * x2[g] + w_ref[g, 1:2, :] * x1[g] + w_ref[g, 2:3, :] * x0[g])
        s = _sigmoid(hc[0])
        da = dx[...]
        dhc = (da * hc[1] * (s * (1.0 + hc[0] * (1.0 - s))), da * (hc[0] * s))
        first = i == 0
        for g in range(2):
            dh = dhc[g]
            dup = (w_ref[g, 2:3, :] * dh + w_ref[g, 1:2, :] * pltpu.roll(dh, ext - 1, 0)
                   + w_ref[g, 0:1, :] * pltpu.roll(dh, ext - 2, 0))
            du_ref[g] = dup[main].astype(BF)
            dm = dh[main]
            _accumulate(dw_ref.at[g, 0:1, :], first, _colsum(dm * x2[g][main]))
            _accumulate(dw_ref.at[g, 1:2, :], first, _colsum(dm * x1[g][main]))
            _accumulate(dw_ref.at[g, 2:3, :], first, _colsum(dm * x0[g][main]))
            _accumulate(db_ref.at[g], first, _colsum(dm))

    return pl.pallas_call(
        body, name=name, grid=(4, n // tm),
        in_specs=[pl.BlockSpec((2, None, tm, FSH), lambda j, i: (0, j, i, 0)),
                  pl.BlockSpec((2, None, 8, FSH), lambda j, i: (0, j, prev(i), 0)),
                  pl.BlockSpec((2, None, 8, FSH), lambda j, i: (0, j, nxt(i), 0)),
                  pl.BlockSpec((None, tm, FSH), lambda j, i: (j, i, 0)),
                  pl.BlockSpec((None, 8, FSH), lambda j, i: (j, nxt(i), 0)),
                  pl.BlockSpec((2, None, 3, FSH), lambda j, i: (0, j, 0, 0)),
                  pl.BlockSpec((2, None, 1, FSH), lambda j, i: (0, j, 0, 0))],
        out_specs=[pl.BlockSpec((2, None, tm, FSH), lambda j, i: (0, j, i, 0)),
                   pl.BlockSpec((2, None, 3, FSH), lambda j, i: (0, j, 0, 0)),
                   pl.BlockSpec((2, None, 1, FSH), lambda j, i: (0, j, 0, 0))],
        out_shape=[jax.ShapeDtypeStruct((2, 4, n, FSH), BF), jax.ShapeDtypeStruct((2, 4, 3, FSH), F32),
                   jax.ShapeDtypeStruct((2, 4, 1, FSH), F32)],
        scratch_shapes=[pltpu.VMEM((2, ext, FSH), F32), pltpu.VMEM((ext, FSH), F32)],
        compiler_params=_cp(("parallel", "arbitrary")))(up, up, up, dact, dact, cw, cb)


def _shortconv_fwd(p, cw, cb, seq, name):
    n = p.shape[0]
    tm = min(256, seq)
    prev, _ = _halo_maps(tm, n)

    def body(p_ref, h_ref, w_ref, b_ref, o_ref):
        i = pl.program_id(1)
        scale = jnp.where(lax.rem(i * tm, seq) == 0, 0.0, 1.0)
        q = p_ref[:, FB:2 * FB] * p_ref[:, 2 * FB:]
        row = lax.broadcasted_iota(jnp.int32, q.shape, 0)
        h7 = h_ref[7:8, FB:2 * FB] * h_ref[7:8, 2 * FB:] * scale
        h6 = h_ref[6:7, FB:2 * FB] * h_ref[6:7, 2 * FB:] * scale
        p1 = jnp.where(row == 0, h7, pltpu.roll(q, 1, 0))
        p2 = jnp.where(row == 0, h6, jnp.where(row == 1, h7, pltpu.roll(q, 2, 0)))
        conv = b_ref[...] + w_ref[0:1, :] * p2 + w_ref[1:2, :] * p1 + w_ref[2:3, :] * q
        o_ref[...] = (p_ref[:, :FB] * conv).astype(BF)

    return pl.pallas_call(
        body, name=name, grid=(D // FB, n // tm),
        in_specs=[pl.BlockSpec((tm, 3 * FB), lambda j, i: (i, j)),
                  pl.BlockSpec((8, 3 * FB), lambda j, i: (prev(i), j)),
                  pl.BlockSpec((3, FB), lambda j, i: (0, j)),
                  pl.BlockSpec((1, FB), lambda j, i: (0, j))],
        out_specs=pl.BlockSpec((tm, FB), lambda j, i: (i, j)),
        out_shape=jax.ShapeDtypeStruct((n, D), BF), compiler_params=_cp(("parallel", "parallel")))(p, p, cw, cb)


def _shortconv_bwd(p, dmix, cw, cb, seq, name):
    n = p.shape[0]
    tm = min(256, seq)
    ext = tm + 16
    prev, nxt = _halo_maps(tm, n)

    def body(p_ref, pp_ref, pn_ref, dm_ref, dn_ref, w_ref, b_ref, dp_ref, dw_ref, db_ref, qx, cx):
        i = pl.program_id(1)
        sp = jnp.where(lax.rem(i * tm, seq) == 0, 0.0, 1.0)
        sn = jnp.where(lax.rem((i + 1) * tm, seq) == 0, 0.0, 1.0)
        bg, cg, hx = p_ref[:, :FB], p_ref[:, FB:2 * FB], p_ref[:, 2 * FB:]
        dm = dm_ref[...]
        qx[0:8, :] = pp_ref[:, FB:2 * FB] * pp_ref[:, 2 * FB:] * sp
        qx[8:8 + tm, :] = cg * hx
        qx[8 + tm:, :] = jnp.zeros((8, FB), F32)
        cx[0:8, :] = jnp.zeros((8, FB), F32)
        cx[8:8 + tm, :] = dm * bg
        cx[8 + tm:, :] = dn_ref[...] * pn_ref[:, :FB] * sn
        q0 = qx[...]
        q1 = pltpu.roll(q0, 1, 0)
        q2 = pltpu.roll(q0, 2, 0)
        main = slice(8, 8 + tm)
        conv = b_ref[...] + w_ref[0:1, :] * q2[main] + w_ref[1:2, :] * q1[main] + w_ref[2:3, :] * q0[main]
        dc = cx[...]
        dq = (w_ref[2:3, :] * dc + w_ref[1:2, :] * pltpu.roll(dc, ext - 1, 0)
              + w_ref[0:1, :] * pltpu.roll(dc, ext - 2, 0))[main]
        dp_ref[:, :FB] = (dm * conv).astype(BF)
        dp_ref[:, FB:2 * FB] = (dq * hx).astype(BF)
        dp_ref[:, 2 * FB:] = (dq * cg).astype(BF)
        first = i == 0
        dcm = dc[main]
        _accumulate(dw_ref.at[0:1, :], first, _colsum(dcm * q2[main]))
        _accumulate(dw_ref.at[1:2, :], first, _colsum(dcm * q1[main]))
        _accumulate(dw_ref.at[2:3, :], first, _colsum(dcm * q0[main]))
        _accumulate(db_ref, first, _colsum(dcm))

    return pl.pallas_call(
        body, name=name, grid=(D // FB, n // tm),
        in_specs=[pl.BlockSpec((tm, 3 * FB), lambda j, i: (i, j)),
                  pl.BlockSpec((8, 3 * FB), lambda j, i: (prev(i), j)),
                  pl.BlockSpec((8, 3 * FB), lambda j, i: (nxt(i), j)),
                  pl.BlockSpec((tm, FB), lambda j, i: (i, j)),
                  pl.BlockSpec((8, FB), lambda j, i: (nxt(i), j)),
                  pl.BlockSpec((3, FB), lambda j, i: (0, j)),
                  pl.BlockSpec((1, FB), lambda j, i: (0, j))],
        out_specs=[pl.BlockSpec((tm, 3 * FB), lambda j, i: (i, j)),
                   pl.BlockSpec((3, FB), lambda j, i: (0, j)),
                   pl.BlockSpec((1, FB), lambda j, i: (0, j))],
        out_shape=[jax.ShapeDtypeStruct((n, 3 * D), BF), jax.ShapeDtypeStruct((3, D), F32),
                   jax.ShapeDtypeStruct((1, D), F32)],
        scratch_shapes=[pltpu.VMEM((ext, FB), F32), pltpu.VMEM((ext, FB), F32)],
        compiler_params=_cp(("parallel", "arbitrary")))(p, p, p, dmix, dmix, cw, cb)


def _gmlp_fwd(uv, wm, bst, gv, seq, name):
    n = uv.shape[0]
    tm = min(256, seq)

    def body(x_ref, w_ref, b_ref, g_ref, o_ref):
        ge_v = _gelu(x_ref[:, GM_W:])
        r = lax.rsqrt(jnp.mean(ge_v * ge_v, axis=-1, keepdims=True) + EPS)
        vn = (ge_v * r * g_ref[...]).astype(BF)
        for c in range(tm // CHUNK):
            rows = slice(c * CHUNK, (c + 1) * CHUNK)
            for h in range(GM_HEADS):
                cols = slice(h * CHUNK, (h + 1) * CHUNK)
                gate = jnp.dot(w_ref[h], vn[rows, cols], preferred_element_type=F32) + b_ref[:, h:h + 1]
                o_ref[rows, cols] = (_gelu(x_ref[rows, cols]) * gate).astype(BF)

    return pl.pallas_call(
        body, name=name, grid=(n // tm,),
        in_specs=[pl.BlockSpec((tm, 2 * GM_W), lambda i: (i, 0)),
                  pl.BlockSpec((GM_HEADS, CHUNK, CHUNK), lambda i: (0, 0, 0)),
                  pl.BlockSpec((CHUNK, GM_HEADS), lambda i: (0, 0)),
                  pl.BlockSpec((1, GM_W), lambda i: (0, 0))],
        out_specs=pl.BlockSpec((tm, GM_W), lambda i: (i, 0)),
        out_shape=jax.ShapeDtypeStruct((n, GM_W), BF), compiler_params=_cp(("parallel",)))(uv, wm, bst, gv)


def _gmlp_bwd(uv, dout, wm, wmt, bst, gv, seq, name):
    n = uv.shape[0]
    tm = min(256, seq)

    def body(x_ref, do_ref, w_ref, wt_ref, b_ref, g_ref, dx_ref, dw_ref, db_ref, dg_ref, dvn_scr):
        first = pl.program_id(0) == 0
        ge_v = _gelu(x_ref[:, GM_W:])
        r = lax.rsqrt(jnp.mean(ge_v * ge_v, axis=-1, keepdims=True) + EPS)
        vh = ge_v * r
        vn = (vh * g_ref[...]).astype(BF)
        tril = (lax.broadcasted_iota(jnp.int32, (CHUNK, CHUNK), 0)
                >= lax.broadcasted_iota(jnp.int32, (CHUNK, CHUNK), 1))
        for h in range(GM_HEADS):
            cols = slice(h * CHUNK, (h + 1) * CHUNK)
            dw = jnp.zeros((CHUNK, CHUNK), F32)
            dbs = jnp.zeros((CHUNK, 1), F32)
            for c in range(tm // CHUNK):
                rows = slice(c * CHUNK, (c + 1) * CHUNK)
                blk = vn[rows, cols]
                gate = jnp.dot(w_ref[h], blk, preferred_element_type=F32) + b_ref[:, h:h + 1]
                xu = x_ref[rows, cols]
                do = do_ref[rows, cols]
                dx_ref[rows, cols] = (do * gate * _gelu_grad(xu)).astype(BF)
                dgate = do * _gelu(xu)
                dgb = dgate.astype(BF)
                dw = dw + lax.dot_general(dgb, blk, _DIMS['nt'], preferred_element_type=F32)
                dbs = dbs + jnp.sum(dgate, axis=1, keepdims=True)
                dvn_scr[rows, cols] = jnp.dot(wt_ref[h], dgb, preferred_element_type=F32)
            _accumulate(dw_ref.at[h], first, jnp.where(tril, dw, 0.0))
            _accumulate(db_ref.at[h], first, dbs)
        dvn = dvn_scr[...]
        _accumulate(dg_ref, first, _colsum(dvn * vh))
        dvh = dvn * g_ref[...]
        dv = r * (dvh - vh * jnp.mean(dvh * vh, axis=-1, keepdims=True))
        dx_ref[:, GM_W:] = (dv * _gelu_grad(x_ref[:, GM_W:])).astype(BF)

    full3 = pl.BlockSpec((GM_HEADS, CHUNK, CHUNK), lambda i: (0, 0, 0))
    return pl.pallas_call(
        body, name=name, grid=(n // tm,),
        in_specs=[pl.BlockSpec((tm, 2 * GM_W), lambda i: (i, 0)), pl.BlockSpec((tm, GM_W), lambda i: (i, 0)),
                  full3, full3, pl.BlockSpec((CHUNK, GM_HEADS), lambda i: (0, 0)),
                  pl.BlockSpec((1, GM_W), lambda i: (0, 0))],
        out_specs=[pl.BlockSpec((tm, 2 * GM_W), lambda i: (i, 0)), full3,
                   pl.BlockSpec((GM_HEADS, CHUNK, 1), lambda i: (0, 0, 0)),
                   pl.BlockSpec((1, GM_W), lambda i: (0, 0))],
        out_shape=[jax.ShapeDtypeStruct((n, 2 * GM_W), BF), jax.ShapeDtypeStruct((GM_HEADS, CHUNK, CHUNK), F32),
                   jax.ShapeDtypeStruct((GM_HEADS, CHUNK, 1), F32), jax.ShapeDtypeStruct((1, GM_W), F32)],
        scratch_shapes=[pltpu.VMEM((tm, GM_W), F32)],
        compiler_params=_cp(("arbitrary",)))(uv, dout, wm, wmt, bst, gv)


def _s5_disc(lam_re, lam_im, log_dt, b_re, b_im):
    lr = jnp.minimum(lam_re, LAM_MAX)
    li = lam_im
    dt = jnp.exp(log_dt)
    mag = jnp.exp(lr * dt)
    ab_re = mag * jnp.cos(li * dt)
    ab_im = mag * jnp.sin(li * dt)
    den = lr * lr + li * li
    nr = ab_re - 1.0
    ni = ab_im
    z_re = (nr * lr + ni * li) / den
    z_im = (ni * lr - nr * li) / den
    return ab_re, ab_im, z_re * b_re - z_im * b_im, z_re * b_im + z_im * b_re


def _s5_disc_fwd(args, name):
    shp = jax.ShapeDtypeStruct(args[0].shape, F32)

    def body(*refs):
        outs = _s5_disc(*[r[...] for r in refs[:5]])
        for o_ref, o in zip(refs[5:], outs):
            o_ref[...] = o

    return pl.pallas_call(body, name=name, out_shape=[shp] * 4)(*args)


def _s5_disc_bwd(args, cts, name):
    shp = jax.ShapeDtypeStruct(args[0].shape, F32)

    def body(*refs):
        _, vjp = jax.vjp(_s5_disc, *[r[...] for r in refs[:5]])
        grads = vjp(tuple(r[...] for r in refs[5:9]))
        for o_ref, o in zip(refs[9:], grads):
            o_ref[...] = o

    return pl.pallas_call(body, name=name, out_shape=[shp] * 5)(*args, *cts)


def _cmul(a, b):
    return a[0] * b[0] - a[1] * b[1], a[0] * b[1] + a[1] * b[0]


def _scan_tables(ar, ai, reverse):
    if reverse:
        ai = -ai
    a1 = (ar, ai)
    a2 = _cmul(a1, a1)
    a3 = _cmul(a2, a1)
    a4 = _cmul(a2, a2)
    powers = [a1, a2, a3, a4, _cmul(a4, a1), _cmul(a4, a2), _cmul(a4, a3), _cmul(a4, a4)]
    row = lax.broadcasted_iota(jnp.int32, (8, NST), 0)
    zero = jnp.zeros((8, NST), F32)
    pr, pi = zero, zero
    for r in range(8):
        pw = powers[7 - r] if reverse else powers[r]
        pr = jnp.where(row == r, pw[0], pr)
        pi = jnp.where(row == r, pw[1], pi)
    levels = []
    for d, pw in ((1, a1), (2, a2), (4, a4)):
        ok = (row <= 7 - d) if reverse else (row >= d)
        levels.append((d, jnp.where(ok, pw[0], zero), jnp.where(ok, pw[1], zero)))
    return (pr, pi), levels


def _scan_block(src, dst, car, tables, n_tiles, reverse):
    (pr, pi), levels = tables
    row = lax.broadcasted_iota(jnp.int32, (8, NST), 0)
    out_row = 0 if reverse else 7

    def step(t, carry):
        cr, ci = carry
        tile = (n_tiles - 1 - t) if reverse else t
        rows = pl.ds(pl.multiple_of(tile * 8, 8), 8)
        xr = src[rows, 0:NST]
        xi = src[rows, NST:2 * NST]
        for d, dr, di in levels:
            shift = 8 - d if reverse else d
            rr = pltpu.roll(xr, shift, 0)
            ri = pltpu.roll(xi, shift, 0)
            xr, xi = xr + dr * rr - di * ri, xi + dr * ri + di * rr
        hr = xr + pr * cr - pi * ci
        hi = xi + pr * ci + pi * cr
        dst[rows, 0:NST] = hr
        dst[rows, NST:2 * NST] = hi
        return (_colsum(jnp.where(row == out_row, hr, 0.0)), _colsum(jnp.where(row == out_row, hi, 0.0)))

    cr, ci = lax.fori_loop(0, n_tiles, step, (car[0:1, 0:NST], car[0:1, NST:2 * NST]))
    car[0:1, 0:NST] = cr
    car[0:1, NST:2 * NST] = ci


def _s5_fwd(u, ab, bbt, cmat, dvec, wglu, bglu, seq, name):
    n = u.shape[0]
    tm = min(256, seq)

    def body(u_ref, ab_ref, bb_ref, c_ref, d_ref, w_ref, b_ref, h_ref, o_ref, xs, car):
        i = pl.program_id(0)

        @pl.when(lax.rem(i * tm, seq) == 0)
        def _():
            car[...] = jnp.zeros(car.shape, F32)

        uv = u_ref[...]
        xs[...] = jnp.dot(uv.astype(BF), bb_ref[...], preferred_element_type=F32)
        tables = _scan_tables(ab_ref[0:1, 0:NST], ab_ref[0:1, NST:2 * NST], False)
        _scan_block(xs, h_ref, car, tables, tm // 8, False)
        y = jnp.dot(h_ref[...].astype(BF), c_ref[...], preferred_element_type=F32) + d_ref[...] * uv
        g1 = _gelu(y)
        z = jnp.dot(g1.astype(BF), w_ref[...], preferred_element_type=F32) + b_ref[...]
        o_ref[...] = (g1 * _sigmoid(z)).astype(BF)

    const = lambda shape: pl.BlockSpec(shape, lambda i: (0, 0))
    return pl.pallas_call(
        body, name=name, grid=(n // tm,),
        in_specs=[pl.BlockSpec((tm, SSM_W), lambda i: (i, 0)), const((1, 2 * NST)), const((SSM_W, 2 * NST)),
                  const((2 * NST, SSM_W)), const((1, SSM_W)), const((SSM_W, SSM_W)), const((1, SSM_W))],
        out_specs=[pl.BlockSpec((tm, 2 * NST), lambda i: (i, 0)), pl.BlockSpec((tm, SSM_W), lambda i: (i, 0))],
        out_shape=[jax.ShapeDtypeStruct((n, 2 * NST), F32), jax.ShapeDtypeStruct((n, SSM_W), BF)],
        scratch_shapes=[pltpu.VMEM((tm, 2 * NST), F32), pltpu.VMEM((8, 2 * NST), F32)],
        compiler_params=_cp(("arbitrary",)))(u, ab, bbt, cmat, dvec, wglu, bglu)


def _s5_bwd(da, u, hst, ab, bbt, cmat, dvec, wglu, bglu, seq, name):
    n = u.shape[0]
    tm = min(256, seq)
    nb = n // tm
    blk = lambda r: nb - 1 - r
    prev, _ = _halo_maps(tm, n)

    def body(da_ref, u_ref, h_ref, hp_ref, ab_ref, bb_ref, c_ref, d_ref, w_ref, b_ref,
             du_ref, dw_ref, dbg_ref, dd_ref, dc_ref, dbb_ref, dab_ref, gs, car):
        r = pl.program_id(0)
        i = blk(r)
        first = r == 0

        @pl.when(lax.rem((i + 1) * tm, seq) == 0)
        def _():
            car[...] = jnp.zeros(car.shape, F32)

        uv = u_ref[...]
        dav = da_ref[...]
        hb = h_ref[...]
        hb16 = hb.astype(BF)
        dvv = d_ref[...]
        y = jnp.dot(hb16, c_ref[...], preferred_element_type=F32) + dvv * uv
        g1 = _gelu(y)
        g16 = g1.astype(BF)
        s = _sigmoid(jnp.dot(g16, w_ref[...], preferred_element_type=F32) + b_ref[...])
        dz = dav * g1 * s * (1.0 - s)
        dz16 = dz.astype(BF)
        dg1 = dav * s + lax.dot_general(dz16, w_ref[...], _DIMS['nt'], preferred_element_type=F32)
        _accumulate(dw_ref, first, lax.dot_general(g16, dz16, _DIMS['tn'], preferred_element_type=F32))
        _accumulate(dbg_ref, first, _colsum(dz))
        dy = dg1 * _gelu_grad(y)
        dy16 = dy.astype(BF)
        _accumulate(dd_ref, first, _colsum(dy * uv))
        _accumulate(dc_ref, first, lax.dot_general(hb16, dy16, _DIMS['tn'], preferred_element_type=F32))
        gs[...] = lax.dot_general(dy16, c_ref[...], _DIMS['nt'], preferred_element_type=F32)
        tables = _scan_tables(ab_ref[0:1, 0:NST], ab_ref[0:1, NST:2 * NST], True)
        _scan_block(gs, gs, car, tables, tm // 8, True)
        g = gs[...]
        g16b = g.astype(BF)
        sp = jnp.where(lax.rem(i * tm, seq) == 0, 0.0, 1.0)
        row = lax.broadcasted_iota(jnp.int32, hb.shape, 0)
        hprev = jnp.where(row == 0, hp_ref[7:8, :] * sp, pltpu.roll(hb, 1, 0))
        gr, gi = g[:, :NST], g[:, NST:]
        hr, hi = hprev[:, :NST], hprev[:, NST:]
        _accumulate(dab_ref.at[:, 0:NST], first, _colsum(gr * hr + gi * hi))
        _accumulate(dab_ref.at[:, NST:2 * NST], first, _colsum(gi * hr - gr * hi))
        _accumulate(dbb_ref, first, lax.dot_general(uv.astype(BF), g16b, _DIMS['tn'], preferred_element_type=F32))
        du = dy * dvv + lax.dot_general(g16b, bb_ref[...], _DIMS['nt'], preferred_element_type=F32)
        du_ref[...] = du.astype(BF)

    const = lambda shape: pl.BlockSpec(shape, lambda r: (0, 0))
    rowspec = lambda w: pl.BlockSpec((tm, w), lambda r: (blk(r), 0))
    return pl.pallas_call(
        body, name=name, grid=(nb,),
        in_specs=[rowspec(SSM_W), rowspec(SSM_W), rowspec(2 * NST),
                  pl.BlockSpec((8, 2 * NST), lambda r: (prev(blk(r)), 0)),
                  const((1, 2 * NST)), const((SSM_W, 2 * NST)), const((2 * NST, SSM_W)), const((1, SSM_W)),
                  const((SSM_W, SSM_W)), const((1, SSM_W))],
        out_specs=[rowspec(SSM_W), const((SSM_W, SSM_W)), const((1, SSM_W)), const((1, SSM_W)),
                   const((2 * NST, SSM_W)), const((SSM_W, 2 * NST)), const((1, 2 * NST))],
        out_shape=[jax.ShapeDtypeStruct((n, SSM_W), BF), jax.ShapeDtypeStruct((SSM_W, SSM_W), F32),
                   jax.ShapeDtypeStruct((1, SSM_W), F32), jax.ShapeDtypeStruct((1, SSM_W), F32),
                   jax.ShapeDtypeStruct((2 * NST, SSM_W), F32), jax.ShapeDtypeStruct((SSM_W, 2 * NST), F32),
                   jax.ShapeDtypeStruct((1, 2 * NST), F32)],
        scratch_shapes=[pltpu.VMEM((tm, 2 * NST), F32), pltpu.VMEM((8, 2 * NST), F32)],
        compiler_params=_cp(("arbitrary",)))(da, u, hst, hst, ab, bbt, cmat, dvec, wglu, bglu)


def _s5_rows(lam_re, lam_im, log_dt, b_re, b_im):
    rep = lambda a: jnp.broadcast_to(a[:, None, :], (SSM_G, SSM_H, SSM_P)).reshape(SSM_W, SSM_P)
    dt = jnp.broadcast_to(log_dt[:, None, None], (SSM_G, SSM_H, SSM_P)).reshape(SSM_W, SSM_P)
    tr = lambda b: b.transpose(0, 2, 1).reshape(SSM_W, SSM_P)
    return rep(lam_re), rep(lam_im), dt, tr(b_re), tr(b_im)


def _block_diag(rows_gp, inner):
    eye = jnp.eye(SSM_G, dtype=rows_gp.dtype)
    return (rows_gp[:, :, None, :] * eye[:, None, :, None]).reshape(SSM_G * inner, SSM_G * SSM_P)


def _diag_blocks(mat, inner):
    m4 = mat.reshape(SSM_G, inner, SSM_G, SSM_P)
    return jnp.stack([m4[g, :, g, :] for g in range(SSM_G)])


def _interleave(w, parts):
    lead = w.shape[:-1]
    nb = w.shape[-1] // (parts * FB)
    return jnp.swapaxes(w.reshape(lead + (parts, nb, FB)), -3, -2).reshape(w.shape)


def _deinterleave(w, parts):
    lead = w.shape[:-1]
    nb = w.shape[-1] // (parts * FB)
    return jnp.swapaxes(w.reshape(lead + (nb, parts, FB)), -3, -2).reshape(w.shape)


def _ffn_fwd(h, g, w_up, w_down, cw, cb, seq, tag):
    n = h.shape[0]
    tm = min(1024, n)
    ni = n // tm
    f = _rmsnorm_fwd(h, g, f"{tag}_norm")
    up = _matmul_spec(
        f, w_up, 'nn', (NDEV, ni, 1),
        pl.BlockSpec((tm, D), lambda s, i, k: (i, 0)),
        pl.BlockSpec((D, FSH), lambda s, i, k: (s, 0)),
        pl.BlockSpec((tm, FSH), lambda s, i, k: (s * ni + i, 0)), (NDEV * n, FSH), f"{tag}_up")
    up = up.reshape(2, 4, n, FSH)
    act = _ffn_conv_fwd(up, cw, cb, seq, f"{tag}_conv")
    tn = 512
    out = _matmul_spec(
        act.reshape(4 * n, FSH), w_down, 'nn', (ni, D // tn, 4),
        pl.BlockSpec((tm, FSH), lambda i, j, k: (k * ni + i, 0)),
        pl.BlockSpec((FSH, tn), lambda i, j, k: (k, j)),
        pl.BlockSpec((tm, tn), lambda i, j, k: (i, j)), (n, D), f"{tag}_down", resid=h)
    return out, (f, up, act)


def _ffn_bwd(dh, h, g, w_up, w_down, cw, cb, saved, seq, tag):
    f, up, act = saved
    n = h.shape[0]
    tm = min(1024, n)
    ni = n // tm
    tk = min(1024, n)
    nk = n // tk
    dact = _matmul_spec(
        dh, w_down, 'nt', (4, ni, 1),
        pl.BlockSpec((tm, D), lambda j, i, k: (i, 0)),
        pl.BlockSpec((FSH, D), lambda j, i, k: (j, 0)),
        pl.BlockSpec((tm, FSH), lambda j, i, k: (j * ni + i, 0)), (4 * n, FSH), f"{tag}_ddown_x")
    tn = 512
    dw_down = _matmul_spec(
        act.reshape(4 * n, FSH), dh, 'tn', (4, D // tn, nk),
        pl.BlockSpec((tk, FSH), lambda j, c, k: (j * nk + k, 0)),
        pl.BlockSpec((tk, tn), lambda j, c, k: (k, c)),
        pl.BlockSpec((FSH, tn), lambda j, c, k: (j, c)), (DFF, D), f"{tag}_ddown_w", out_dtype=BF)
    dup, dcw, dcb = _ffn_conv_bwd(up, dact.reshape(4, n, FSH), cw, cb, seq, f"{tag}_dconv")
    dup2 = dup.reshape(NDEV * n, FSH)
    df = _matmul_spec(
        dup2, w_up, 'nt', (ni, 1, NDEV),
        pl.BlockSpec((tm, FSH), lambda i, j, k: (k * ni + i, 0)),
        pl.BlockSpec((D, FSH), lambda i, j, k: (k, 0)),
        pl.BlockSpec((tm, D), lambda i, j, k: (i, 0)), (n, D), f"{tag}_dup_x")
    dw_up = _matmul_spec(
        f, dup2, 'tn', (NDEV, 1, nk),
        pl.BlockSpec((tk, D), lambda s, j, k: (k, 0)),
        pl.BlockSpec((tk, FSH), lambda s, j, k: (s * nk + k, 0)),
        pl.BlockSpec((D, FSH), lambda s, j, k: (s, 0)), (NDEV * D, FSH), f"{tag}_dup_w", out_dtype=BF)
    dh_in, dg = _rmsnorm_bwd(h, g, df, dh, f"{tag}_dnorm")
    grads = dict(g=dg, w_up=dw_up.reshape(NDEV, D, FSH), w_down=dw_down.reshape(NDEV, DFF // NDEV, D),
                 cw=dcw.reshape(NDEV, 3, FSH), cb=dcb.reshape(2 * DFF))
    return dh_in, grads


def _col_shards(w, width):
    return w.reshape(w.shape[0], NDEV, width).transpose(1, 0, 2)


def _local_step(x, tgt, w, gw, wait_ffn0, wait_rest, token, scatter, seq):
    bf = lambda a: a.astype(BF)
    row = lambda a: a.reshape(1, -1).astype(F32)
    w_ev = gw['ev_w_in'].transpose(1, 0, 2).reshape(D, 1792)
    w_ev_s5, w_ev_gm = w_ev[:, :SSM_W], w_ev[:, SSM_W:]
    w_evo = gw['ev_w_out'].reshape(D, D)
    f_cb = [w['ffn_conv_b'][l].reshape(2, 4, 1, FSH) for l in range(2)]
    tril = jnp.tril(jnp.ones((CHUNK, CHUNK), dtype=bool))
    gm_w = jnp.where(tril, w['gm_w_s'][0], 0.0)
    gm_wm, gm_wmt = bf(gm_w), bf(jnp.swapaxes(gm_w, 1, 2))
    gm_bt = w['gm_b_s'][0].T
    gm_gv = row(w['gm_v_g'][0])
    s5_in = _s5_rows(w['s5_lam_re'][0], w['s5_lam_im'][0], w['s5_log_dt'][0], w['s5_b_re'][0], w['s5_b_im'][0])
    ab_re, ab_im, bb_re, bb_im = _s5_disc_fwd(s5_in, "s5_disc")
    first_h = lambda a: a.reshape(SSM_G, SSM_H, SSM_P)[:, 0, :].reshape(1, NST)
    s5_ab = jnp.concatenate([first_h(ab_re), first_h(ab_im)], axis=1)
    to_gp = lambda a: a.reshape(SSM_G, SSM_H, SSM_P)
    s5_bbt = bf(jnp.concatenate([_block_diag(to_gp(bb_re), SSM_H), _block_diag(to_gp(bb_im), SSM_H)], axis=1))
    s5_cmat = bf(jnp.concatenate([_block_diag(w['s5_c_re'][0], SSM_H).T, -_block_diag(w['s5_c_im'][0], SSM_H).T],
                                 axis=0))
    s5_d, s5_bg, s5_wg = row(w['s5_d'][0]), row(w['s5_b_glu'][0]), gw['s5_w_glu'].reshape(SSM_W, SSM_W)
    g_mix = [row(w['mix_norm_g'][0]) + token[0:1, 0:1], row(w['mix_norm_g'][1])]
    g_ffn = [row(w['ffn_norm_g'][l]) for l in range(2)]
    g_fin = row(w['final_norm_g'])

    h0 = x
    y0 = _rmsnorm_fwd(h0, g_mix[0], "ev_norm")
    p_s5 = _matmul(y0, w_ev_s5, 'nn', 1024, 256, D, "ev_in_s5")
    p_gm = _matmul(y0, w_ev_gm, 'nn', 1024, 512, D, "ev_in_gm")
    hst, a_out = _s5_fwd(p_s5, s5_ab, s5_bbt, s5_cmat, s5_d, s5_wg, s5_bg, seq, "s5_fwd")
    b_out = _gmlp_fwd(p_gm, gm_wm, gm_bt, gm_gv, seq, "gmlp_fwd")
    mixcat = jnp.concatenate([a_out, b_out], axis=1)
    h1 = _matmul(mixcat, w_evo, 'nn', 1024, 512, D, "ev_out", resid=h0)
    g0 = wait_ffn0(mixcat)
    w_up0, w_dn0 = g0['ffn_w_up0'].reshape(NDEV * D, FSH), g0['ffn_w_down0'].reshape(DFF, D)
    f_cw0 = g0['ffn_conv_w0'].reshape(2, 4, 3, FSH)
    h2, ffn0 = _ffn_fwd(h1, g_ffn[0], w_up0, w_dn0, f_cw0, f_cb[0], seq, "ffn0")
    g1 = wait_rest(h2)
    w_od = _interleave(g1['od_w_in'].transpose(1, 0, 2).reshape(D, 3 * D), 3)
    w_odo = g1['od_w_out'].reshape(D, D)
    od_cw = g1['od_conv_w'].transpose(1, 0, 2).reshape(3, D)
    od_cb = g1['od_conv_b'].reshape(1, D)
    w_up1, w_dn1 = g1['ffn_w_up1'].reshape(NDEV * D, FSH), g1['ffn_w_down1'].reshape(DFF, D)
    f_cw1 = g1['ffn_conv_w1'].reshape(2, 4, 3, FSH)
    y1 = _rmsnorm_fwd(h2, g_mix[1], "od_norm")
    p_od = _matmul(y1, w_od, 'nn', 1024, 512, D, "od_in")
    mixin = _shortconv_fwd(p_od, od_cw, od_cb, seq, "od_conv")
    h3 = _matmul(mixin, w_odo, 'nn', 1024, 512, D, "od_out", resid=h2)
    h4, ffn1 = _ffn_fwd(h3, g_ffn[1], w_up1, w_dn1, f_cw1, f_cb[1], seq, "ffn1")
    loss, dh4, dg_fin = _final_loss(h4, g_fin, tgt, "final_loss")

    dh3, gf1 = _ffn_bwd(dh4, h3, g_ffn[1], w_up1, w_dn1, f_cw1, f_cb[1], ffn1, seq, "ffn1")
    dmixin = _matmul(dh3, w_odo, 'nt', 1024, 512, D, "od_dout_x")
    dw_odo = _matmul(mixin, dh3, 'tn', D, 512, 1024, "od_dout_w", out_dtype=BF)
    dp_od, d_od_cw, d_od_cb = _shortconv_bwd(p_od, dmixin, od_cw, od_cb, seq, "od_dconv")
    dy1 = _matmul(dp_od, w_od, 'nt', 1024, D, 512, "od_din_x")
    dw_od = _matmul(y1, dp_od, 'tn', D, 512, 1024, "od_din_w", out_dtype=BF)
    sent = scatter("scatter_layer1", {
        'od_w_in': _col_shards(_deinterleave(dw_od, 3), 384), 'od_conv_w': _col_shards(d_od_cw, D // NDEV),
        'od_conv_b': d_od_cb.reshape(NDEV, 1, D // NDEV), 'od_w_out': dw_odo.reshape(NDEV, D // NDEV, D),
        'ffn_w_up1': gf1['w_up'], 'ffn_conv_w1': gf1['cw'], 'ffn_w_down1': gf1['w_down']})
    dh2, dg_mix1 = _rmsnorm_bwd(h2, g_mix[1] + sent[0:1, 0:1], dy1, dh3, "od_dnorm")
    dh1, gf0 = _ffn_bwd(dh2, h1, g_ffn[0], w_up0, w_dn0, f_cw0, f_cb[0], ffn0, seq, "ffn0")
    dmix_a = _matmul(dh1, w_evo[:SSM_W], 'nt', 1024, SSM_W, D, "ev_dout_xa")
    dmix_b = _matmul(dh1, w_evo[SSM_W:], 'nt', 1024, GM_W, D, "ev_dout_xb")
    dw_evo = _matmul(mixcat, dh1, 'tn', D, 512, 1024, "ev_dout_w", out_dtype=BF)
    sent = scatter("scatter_ffn0", {'ffn_w_up0': gf0['w_up'], 'ffn_conv_w0': gf0['cw'], 'ffn_w_down0': gf0['w_down'],
                                    'ev_w_out': dw_evo.reshape(NDEV, D // NDEV, D)})
    dp_s5, d_wg, d_bg, d_d, d_cmat, d_bbt, d_ab = _s5_bwd(dmix_a, p_s5, hst, s5_ab, s5_bbt, s5_cmat,
                                                           s5_d + sent[0:1, 0:1], s5_wg, s5_bg, seq, "s5_bwd")
    dp_gm, d_gmw, d_gmb, d_gmg = _gmlp_bwd(p_gm, dmix_b, gm_wm, gm_wmt, gm_bt, gm_gv, seq, "gmlp_bwd")
    dw_ev = jnp.concatenate([_matmul(y0, dp_s5, 'tn', D, SSM_W, 1024, "ev_din_wa", out_dtype=BF),
                             _matmul(y0, dp_gm, 'tn', D, 512, 1024, "ev_din_wb", out_dtype=BF)], axis=1)
    sent = scatter("scatter_even", {'ev_w_in': _col_shards(dw_ev, 224),
                                    's5_w_glu': d_wg.reshape(NDEV, SSM_W // NDEV, SSM_W)})
    dy0 = _matmul(dp_gm, w_ev_gm, 'nt', 1024, D, 512, "ev_din_xb")
    dy0 = _matmul(dp_s5, w_ev_s5, 'nt', 1024, D, SSM_W, "ev_din_xa", resid=dy0)
    grad_x, dg_mix0 = _rmsnorm_bwd(h0, g_mix[0] + sent[0:1, 0:1], dy0, dh1, "ev_dnorm")

    put_h0 = lambda a: jnp.zeros((SSM_G, SSM_H, SSM_P), F32).at[:, 0, :].set(a.reshape(SSM_G, SSM_P)).reshape(
        SSM_W, SSM_P)
    ct = (put_h0(d_ab[:, :NST]), put_h0(d_ab[:, NST:]),
          _diag_blocks(d_bbt[:, :NST], SSM_H).reshape(SSM_W, SSM_P),
          _diag_blocks(d_bbt[:, NST:], SSM_H).reshape(SSM_W, SSM_P))
    d_lre, d_lim, d_ldt, d_bre, d_bim = _s5_disc_bwd(s5_in, ct, "s5_ddisc")
    over_h = lambda a: a.reshape(SSM_G, SSM_H, SSM_P).sum(axis=1)
    un_tr = lambda a: a.reshape(SSM_G, SSM_H, SSM_P).transpose(0, 2, 1)
    d_cre = _diag_blocks(d_cmat[:NST].T, SSM_H)
    d_cim = -_diag_blocks(d_cmat[NST:].T, SSM_H)

    repl = {
        'mix_norm_g': jnp.concatenate([dg_mix0, dg_mix1], axis=0),
        'ffn_norm_g': jnp.concatenate([gf0['g'], gf1['g']], axis=0),
        'final_norm_g': dg_fin.reshape(D),
        's5_lam_re': over_h(d_lre)[None], 's5_lam_im': over_h(d_lim)[None],
        's5_log_dt': over_h(d_ldt).sum(axis=1)[None],
        's5_b_re': un_tr(d_bre)[None], 's5_b_im': un_tr(d_bim)[None],
        's5_c_re': d_cre[None], 's5_c_im': d_cim[None],
        's5_d': d_d, 's5_b_glu': d_bg,
        'gm_w_s': d_gmw[None], 'gm_b_s': d_gmb.reshape(1, GM_HEADS, CHUNK), 'gm_v_g': d_gmg,
        'ffn_conv_b': jnp.stack([gf0['cb'], gf1['cb']]),
    }
    return loss, grad_x, repl


HBM_SPEC = pl.BlockSpec(memory_space=pltpu.HBM)


def _at_axis(ref, pos, index):
    return ref.at[(slice(None),) * pos + (index,)]


def _all_gather(shards, positions, name):
    n = len(shards)

    def body(*refs):
        xs, outs = refs[:n], refs[n:2 * n]
        send_sems, recv_sems, local_sems = refs[2 * n:]
        x, y, c = lax.axis_index("x"), lax.axis_index("y"), lax.axis_index("c")
        me, sibling = (x, y, c), (x, y, 1 - c)
        chips = [(1 - x, y), (x, 1 - y), (1 - x, 1 - y)]

        def block(p, dev):
            return _at_axis(outs[p], positions[p], 4 * dev[0] + 2 * dev[1] + dev[2])

        def copy(p, k, dev, to, src=None):
            return pltpu.make_async_remote_copy(
                src_ref=block(p, dev) if src is None else src, dst_ref=block(p, dev),
                send_sem=send_sems.at[p, k], recv_sem=recv_sems.at[p, k], device_id=to, device_id_type=MESH_T)

        mine = [pltpu.make_async_copy(xs[p], block(p, me), local_sems.at[p]) for p in range(n)]
        for cp in mine:
            cp.start()
        first = [copy(p, 0, me, sibling, src=xs[p]) for p in range(n)]
        first += [copy(p, 1 + j, me, (*chip, c), src=xs[p]) for j, chip in enumerate(chips) for p in range(n)]
        for cp in first:
            cp.start()
        passed = []
        for j, chip in enumerate(chips):
            for p in range(n):
                copy(p, 1 + j, (*chip, c), me).wait_recv()
                fwd = copy(p, 4 + j, (*chip, c), sibling)
                fwd.start()
                passed.append(fwd)
        for p in range(n):
            copy(p, 0, sibling, me).wait_recv()
        for j, chip in enumerate(chips):
            for p in range(n):
                copy(p, 4 + j, (*chip, 1 - c), me).wait_recv()
        for cp in first + passed:
            cp.wait_send()
        for cp in mine:
            cp.wait()

    out_shape = [jax.ShapeDtypeStruct(s.shape[:pos] + (NDEV,) + s.shape[pos:], s.dtype)
                 for s, pos in zip(shards, positions)]
    return pl.pallas_call(
        body, name=name, out_shape=out_shape, in_specs=[HBM_SPEC] * n, out_specs=[HBM_SPEC] * n,
        scratch_shapes=[pltpu.SemaphoreType.DMA((n, 7)), pltpu.SemaphoreType.DMA((n, 7)),
                        pltpu.SemaphoreType.DMA((n,))])(*shards)


def _other_devices(x, y, c):
    flip = lambda v, bit: 1 - v if bit else v
    return [(flip(x, k >> 2 & 1), flip(y, k >> 1 & 1), flip(c, k & 1)) for k in range(1, NDEV)]


SEM_SPEC = pl.BlockSpec(memory_space=pltpu.SEMAPHORE)
START_EFFECT = pltpu.SideEffectType.DATAFLOW_SIDE_EFFECTING


def _send_start(arrays, scatter, name):
    n = len(arrays)
    lands = [lax.empty((NDEV,) + (a.shape[1:] if scatter else a.shape), a.dtype) for a in arrays]

    def body(*refs):
        xs, ls = refs[:n], refs[n:2 * n]
        send_sems, recv_sems, own_sems, token = refs[2 * n], refs[2 * n + 1], refs[2 * n + 2], refs[4 * n + 3]
        x, y, c = lax.axis_index("x"), lax.axis_index("y"), lax.axis_index("c")
        me = 4 * x + 2 * y + c
        for k, peer in enumerate(_other_devices(x, y, c)):
            for p in range(n):
                src = xs[p].at[4 * peer[0] + 2 * peer[1] + peer[2]] if scatter else xs[p]
                pltpu.make_async_remote_copy(
                    src_ref=src, dst_ref=ls[p].at[me], send_sem=send_sems.at[p * (NDEV - 1) + k],
                    recv_sem=recv_sems.at[p * (NDEV - 1) + k], device_id=peer, device_id_type=MESH_T).start()
        for p in range(n):
            pltpu.make_async_copy(xs[p].at[me] if scatter else xs[p], ls[p].at[me], own_sems.at[p]).start()
        token[...] = jnp.zeros(token.shape, F32)

    sems = pltpu.SemaphoreType.DMA((n * (NDEV - 1),))
    out_shape = ([sems, sems, pltpu.SemaphoreType.DMA((n,))]
                 + [pltpu.HBM(a.shape, a.dtype) for a in list(arrays) + lands] + [jax.ShapeDtypeStruct((8, 128), F32)])
    res = pl.pallas_call(
        body, name=name, out_shape=out_shape, in_specs=[HBM_SPEC] * (2 * n),
        out_specs=[SEM_SPEC] * 3 + [HBM_SPEC] * (2 * n) + [pl.BlockSpec(memory_space=pltpu.VMEM)],
        input_output_aliases={i: 3 + i for i in range(2 * n)},
        compiler_params=pltpu.CompilerParams(has_side_effects=START_EFFECT))(
            *[pltpu.with_memory_space_constraint(a, pltpu.HBM) for a in list(arrays) + lands])
    return res[:3], res[3:3 + n], res[3 + n:3 + 2 * n], res[3 + 2 * n]


def _send_wait(started, scatter, after, name):
    sems, arrays, lands, _ = started
    n = len(arrays)

    def body(*refs):
        xs, ls = refs[:n], refs[n:2 * n]
        send, recv, own = refs[2 * n:2 * n + 3]
        x, y, c = lax.axis_index("x"), lax.axis_index("y"), lax.axis_index("c")
        me = 4 * x + 2 * y + c
        for p in range(n):
            pltpu.make_async_copy(xs[p].at[me] if scatter else xs[p], ls[p].at[me], own.at[p]).wait()
        for k, peer in enumerate(_other_devices(x, y, c)):
            slot = 4 * peer[0] + 2 * peer[1] + peer[2]
            for p in range(n):
                cp = pltpu.make_async_remote_copy(
                    src_ref=xs[p].at[slot] if scatter else xs[p], dst_ref=ls[p].at[slot],
                    send_sem=send.at[p * (NDEV - 1) + k], recv_sem=recv.at[p * (NDEV - 1) + k], device_id=peer,
                    device_id_type=MESH_T)
                cp.wait_send()
                cp.wait_recv()

    res = pl.pallas_call(
        body, name=name, out_shape=[pltpu.HBM(a.shape, a.dtype) for a in list(arrays) + list(lands)],
        in_specs=[HBM_SPEC] * (2 * n) + [SEM_SPEC] * 3 + [pl.BlockSpec(memory_space=pl.ANY)],
        out_specs=[HBM_SPEC] * (2 * n), input_output_aliases={i: i for i in range(2 * n)},
        compiler_params=pltpu.CompilerParams(has_side_effects=START_EFFECT))(
            *arrays, *lands, *sems, after)
    return res[n:]


def _row_block(rows, cols, itemsize=4, target=2**20):
    best = None
    for tr in range(16, rows + 1, 16):
        if rows % tr == 0 and tr * cols * itemsize <= target:
            best = tr
    return best or rows


def _adamw(w, m, v, gparts, name):
    parts, rows, cols = gparts.shape
    tr = _row_block(rows, cols, target=2**19)
    bc1 = 1.0 - ADAM_B1 ** ADAM_STEP
    bc2 = 1.0 - ADAM_B2 ** ADAM_STEP

    def body(w_ref, m_ref, v_ref, g_ref, go_ref, d_ref, mo_ref, vo_ref):
        g = g_ref[0].astype(F32)
        for k in range(1, parts):
            g = g + g_ref[k].astype(F32)
        mn = ADAM_B1 * m_ref[...] + (1.0 - ADAM_B1) * g
        vn = ADAM_B2 * v_ref[...] + (1.0 - ADAM_B2) * (g * g)
        go_ref[...] = g
        mo_ref[...] = mn
        vo_ref[...] = vn
        d_ref[...] = -ADAM_LR * ((mn / bc1) / (jnp.sqrt(vn / bc2) + ADAM_EPS) + ADAM_WD * w_ref[...])

    blk = pl.BlockSpec((tr, cols), lambda i: (i, 0))
    shp = jax.ShapeDtypeStruct((rows, cols), F32)
    return pl.pallas_call(
        body, name=name, grid=(rows // tr,),
        in_specs=[blk, blk, blk, pl.BlockSpec((parts, tr, cols), lambda i: (0, i, 0))],
        out_specs=[blk] * 4, out_shape=[shp] * 4, compiler_params=_cp(("parallel",)))(w, m, v, gparts)


def _pack(arrays, rows):
    flat = jnp.concatenate([a.reshape(-1).astype(F32) for a in arrays])
    return jnp.pad(flat, (0, rows * PACK_COLS - flat.shape[0])).reshape(rows, PACK_COLS)


def _unpack(buf, shapes):
    flat = buf.reshape(-1)
    out, off = [], 0
    for shp in shapes:
        size = int(np.prod(shp))
        out.append(flat[off:off + size].reshape(shp))
        off += size
    return out


REPL_SHAPES = {'mix_norm_g': (2, 1024), 'ffn_norm_g': (2, 1024), 'final_norm_g': (1024,), 's5_lam_re': (1, 16, 64),
               's5_lam_im': (1, 16, 64), 's5_log_dt': (1, 16), 's5_b_re': (1, 16, 64, 16), 's5_b_im': (1, 16, 64, 16),
               's5_c_re': (1, 16, 16, 64), 's5_c_im': (1, 16, 16, 64), 's5_d': (1, 256), 's5_b_glu': (1, 256),
               'gm_w_s': (1, 6, 128, 128), 'gm_b_s': (1, 6, 128), 'gm_v_g': (1, 768), 'ffn_conv_b': (2, 5632)}
REPL_ELEMS = sum(int(np.prod(REPL_SHAPES[n])) for n in REPL_ORDER)
REPL_ROWS = -(-REPL_ELEMS // (PACK_COLS * 8)) * 8

GATHER_DTYPE = {'ev_w_in': BF, 'ev_w_out': BF, 's5_w_glu': BF, 'od_w_in': BF, 'od_conv_w': F32, 'od_conv_b': F32,
                'od_w_out': BF, 'ffn_w_up': BF, 'ffn_conv_w': F32, 'ffn_w_down': BF}
GATHER_EVEN = ['ev_w_in', 'ev_w_out', 's5_w_glu']
GATHER_FFN0 = ['ffn_w_up0', 'ffn_conv_w0', 'ffn_w_down0']
GATHER_REST = ['od_w_in', 'od_conv_w', 'od_conv_b', 'od_w_out', 'ffn_w_up1', 'ffn_conv_w1', 'ffn_w_down1']

def _squeeze_lead(a):
    return a.reshape(a.shape[1:]) if a.shape[0] == 1 and a.ndim > 2 else a


def kernel(x, mix_norm_g, ffn_norm_g, final_norm_g, ev_w_in, ev_w_out, s5_lam_re, s5_lam_im, s5_log_dt, s5_b_re, s5_b_im, s5_c_re, s5_c_im, s5_d, s5_w_glu, s5_b_glu, gm_w_s, gm_b_s, gm_v_g, od_w_in, od_conv_w, od_conv_b, od_w_out, ffn_w_up, ffn_conv_w, ffn_conv_b, ffn_w_down, loss_target, m_mix_norm_g, m_ffn_norm_g, m_final_norm_g, m_ev_w_in, m_ev_w_out, m_s5_lam_re, m_s5_lam_im, m_s5_log_dt, m_s5_b_re, m_s5_b_im, m_s5_c_re, m_s5_c_im, m_s5_d, m_s5_w_glu, m_s5_b_glu, m_gm_w_s, m_gm_b_s, m_gm_v_g, m_od_w_in, m_od_conv_w, m_od_conv_b, m_od_w_out, m_ffn_w_up, m_ffn_conv_w, m_ffn_conv_b, m_ffn_w_down, v_mix_norm_g, v_ffn_norm_g, v_final_norm_g, v_ev_w_in, v_ev_w_out, v_s5_lam_re, v_s5_lam_im, v_s5_log_dt, v_s5_b_re, v_s5_b_im, v_s5_c_re, v_s5_c_im, v_s5_d, v_s5_w_glu, v_s5_b_glu, v_gm_w_s, v_gm_b_s, v_gm_v_g, v_od_w_in, v_od_conv_w, v_od_conv_b, v_od_w_out, v_ffn_w_up, v_ffn_conv_w, v_ffn_conv_b, v_ffn_w_down):
    given = dict(locals())
    weights = {n: given[n] for n in WEIGHT_ORDER}
    nseq, seq, _ = x.shape

    send = {}
    for name in SHARDED_ORDER:
        a = weights[name].astype(GATHER_DTYPE[name])
        if a.shape[0] == 2:
            send[name + '0'], send[name + '1'] = a[0], a[1]
        else:
            send[name] = _squeeze_lead(a)
    gathers = [_send_start([send[n] for n in names], False, f"gather_{tag}_start")
               for tag, names in (("ffn0", GATHER_FFN0), ("rest", GATHER_REST))]
    token = gathers[0][3] + gathers[1][3]

    def waiter(tag, names, started):
        return lambda after: dict(zip(names, _send_wait(started, False, after, f"gather_{tag}_wait")))

    gathered = dict(zip(GATHER_EVEN, _all_gather([send[n] for n in GATHER_EVEN], [0] * len(GATHER_EVEN),
                                                 "gather_even")))

    scatters = []

    def scatter(tag, grads):
        names = list(grads)
        started = _send_start([grads[n].astype(BF) for n in names], True, f"{tag}_start")
        scatters.append((tag, names, started))
        return started[3]

    loss_row, grad_x, g_repl = _local_step(
        x.reshape(nseq * seq, D), loss_target.reshape(nseq * seq, D), weights, gathered,
        waiter("ffn0", GATHER_FFN0, gathers[0]), waiter("rest", GATHER_REST, gathers[1]), token, scatter, seq)
    loss = lax.psum(loss_row[0, 0], ("x", "y", "c"))

    parts = {}
    for tag, names, started in scatters:
        parts.update(zip(names, _send_wait(started, True, grad_x, f"{tag}_wait")))
    repl_parts = _all_gather([_pack([g_repl[n] for n in REPL_ORDER], REPL_ROWS)], [0], "gather_small_grads")[0]

    out = {}
    for name in SHARDED_ORDER:
        w = weights[name]
        if name + '0' in parts:
            gp = jnp.stack([parts[name + '0'], parts[name + '1']], axis=1)
        else:
            gp = parts[name]
        to_rows = lambda a: a.reshape(-1, w.shape[-1])
        res = _adamw(to_rows(w), to_rows(given["m_" + name]), to_rows(given["v_" + name]),
                     gp.reshape(NDEV, -1, w.shape[-1]), f"adamw_{name}")
        out[name] = [r.reshape(w.shape) for r in res]
    rp = _adamw(_pack([weights[n] for n in REPL_ORDER], REPL_ROWS),
                _pack([given["m_" + n] for n in REPL_ORDER], REPL_ROWS),
                _pack([given["v_" + n] for n in REPL_ORDER], REPL_ROWS), repl_parts, "adamw_replicated")
    rp_shapes = [weights[n].shape for n in REPL_ORDER]
    for k in range(4):
        for name, a in zip(REPL_ORDER, _unpack(rp[k], rp_shapes)):
            out.setdefault(name, [None] * 4)[k] = a
    results = [[out[n][k] for n in WEIGHT_ORDER] for k in range(4)]
    grad_w, delta_w, new_m, new_v = results
    return (loss, grad_x.reshape(nseq, seq, D), *grad_w, *delta_w, *new_m, *new_v)
```

```python
import math

import jax
import jax.numpy as jnp
import numpy as np
from jax import lax
from jax.experimental import pallas as pl
from jax.experimental.pallas import tpu as pltpu

F32 = jnp.float32
BF = jnp.bfloat16

D = 1024
DFF = 2816
NDEV = 8
SSM_W = 256
SSM_G = 16
SSM_H = 16
SSM_P = 64
NST = SSM_G * SSM_P
GM_W = 768
GM_HEADS = 6
CHUNK = 128
EPS = 1e-6
LAM_MAX = -1e-4
FB = 256
FSH = 2 * DFF // NDEV
VMEM_LIMIT = 48 * 2**20
PACK_COLS = 1024
MESH_T = pl.DeviceIdType.MESH

ADAM_LR = 0.001
ADAM_B1 = 0.9
ADAM_B2 = 0.999
ADAM_EPS = 1e-08
ADAM_WD = 0.01
ADAM_STEP = 10

WEIGHT_ORDER = ['mix_norm_g', 'ffn_norm_g', 'final_norm_g', 'ev_w_in', 'ev_w_out', 's5_lam_re', 's5_lam_im',
                's5_log_dt', 's5_b_re', 's5_b_im', 's5_c_re', 's5_c_im', 's5_d', 's5_w_glu', 's5_b_glu', 'gm_w_s',
                'gm_b_s', 'gm_v_g', 'od_w_in', 'od_conv_w', 'od_conv_b', 'od_w_out', 'ffn_w_up', 'ffn_conv_w',
                'ffn_conv_b', 'ffn_w_down']
SHARDED = {'ev_w_in': ((1, 1024, 1792), 2), 'ev_w_out': ((1, 1024, 1024), 1), 's5_w_glu': ((1, 256, 256), 1),
           'od_w_in': ((1, 1024, 3072), 2), 'od_conv_w': ((1, 3, 1024), 2), 'od_conv_b': ((1, 1024), 1),
           'od_w_out': ((1, 1024, 1024), 1), 'ffn_w_up': ((2, 1024, 5632), 2), 'ffn_conv_w': ((2, 3, 5632), 2),
           'ffn_w_down': ((2, 2816, 1024), 1)}
SHARDED_ORDER = [n for n in WEIGHT_ORDER if n in SHARDED]
REPL_ORDER = [n for n in WEIGHT_ORDER if n not in SHARDED]


def _cp(sem):
    return pltpu.CompilerParams(dimension_semantics=sem, vmem_limit_bytes=VMEM_LIMIT)


def _sigmoid(x):
    return 1.0 / (1.0 + jnp.exp(-x))


_GELU_K = math.sqrt(2.0 / math.pi)


def _gelu(x):
    return 0.5 * x * (1.0 + jnp.tanh(_GELU_K * (x + 0.044715 * x * x * x)))


def _gelu_grad(x):
    t = jnp.tanh(_GELU_K * (x + 0.044715 * x * x * x))
    return 0.5 * (1.0 + t) + 0.5 * x * (1.0 - t * t) * _GELU_K * (1.0 + 3.0 * 0.044715 * x * x)


def _colsum(x):
    return jnp.sum(x, axis=0, keepdims=True)


def _accumulate(ref, first, part):
    @pl.when(first)
    def _():
        ref[...] = part

    @pl.when(jnp.logical_not(first))
    def _():
        ref[...] += part


_DIMS = {'nn': (((1,), (0,)), ((), ())), 'nt': (((1,), (1,)), ((), ())), 'tn': (((0,), (0,)), ((), ()))}


def _matmul(a, b, mode, tm, tn, tk, name, resid=None, out_dtype=F32):
    if mode == 'tn':
        kdim, m = a.shape
    else:
        m, kdim = a.shape
    n = b.shape[0] if mode == 'nt' else b.shape[1]
    tm, tn, tk = min(tm, m), min(tn, n), min(tk, kdim)
    assert m % tm == 0 and n % tn == 0 and kdim % tk == 0, (name, m, n, kdim, tm, tn, tk)
    a_spec = (pl.BlockSpec((tk, tm), lambda i, j, k: (k, i)) if mode == 'tn'
              else pl.BlockSpec((tm, tk), lambda i, j, k: (i, k)))
    b_spec = (pl.BlockSpec((tn, tk), lambda i, j, k: (j, k)) if mode == 'nt'
              else pl.BlockSpec((tk, tn), lambda i, j, k: (k, j)))
    o_spec = pl.BlockSpec((tm, tn), lambda i, j, k: (i, j))
    return _matmul_spec(a, b, mode, (m // tm, n // tn, kdim // tk), a_spec, b_spec, o_spec, (m, n), name,
                        resid=resid, out_dtype=out_dtype)


def _matmul_spec(a, b, mode, grid, a_spec, b_spec, o_spec, out_shape, name, resid=None, out_dtype=F32):
    nk = grid[2]
    tm, tn = o_spec.block_shape[-2:]
    dims = _DIMS[mode]
    has_resid = resid is not None

    def body(*refs):
        if has_resid:
            a_ref, b_ref, r_ref, o_ref = refs[:4]
        else:
            a_ref, b_ref, o_ref = refs[:3]
            r_ref = None
        part = lax.dot_general(a_ref[...].astype(BF), b_ref[...].astype(BF), dims, preferred_element_type=F32)
        if nk == 1:
            if has_resid:
                part = part + r_ref[...]
            o_ref[...] = part.astype(out_dtype)
        else:
            acc = refs[-1]
            k = pl.program_id(2)

            @pl.when(k == 0)
            def _():
                acc[...] = part

            @pl.when(k > 0)
            def _():
                acc[...] += part

            @pl.when(k == nk - 1)
            def _():
                tot = acc[...]
                if has_resid:
                    tot = tot + r_ref[...]
                o_ref[...] = tot.astype(out_dtype)

    operands = [a, b] + ([resid] if has_resid else [])
    in_specs = [a_spec, b_spec] + ([o_spec] if has_resid else [])
    return pl.pallas_call(
        body, name=name, grid=grid, in_specs=in_specs, out_specs=o_spec,
        out_shape=jax.ShapeDtypeStruct(out_shape, out_dtype),
        scratch_shapes=[pltpu.VMEM((tm, tn), F32)] if nk > 1 else [],
        compiler_params=_cp(("parallel", "parallel", "arbitrary")))(*operands)


def _rmsnorm_fwd(x, g, name):
    n = x.shape[0]
    tm = min(512, n)

    def body(x_ref, g_ref, o_ref):
        xv = x_ref[...]
        r = lax.rsqrt(jnp.mean(xv * xv, axis=-1, keepdims=True) + EPS)
        o_ref[...] = (xv * r * g_ref[...]).astype(BF)

    return pl.pallas_call(
        body, name=name, grid=(n // tm,),
        in_specs=[pl.BlockSpec((tm, D), lambda i: (i, 0)), pl.BlockSpec((1, D), lambda i: (0, 0))],
        out_specs=pl.BlockSpec((tm, D), lambda i: (i, 0)),
        out_shape=jax.ShapeDtypeStruct((n, D), BF), compiler_params=_cp(("parallel",)))(x, g)


def _rmsnorm_bwd(x, g, dy, dres, name):
    n = x.shape[0]
    tm = min(512, n)

    def body(x_ref, g_ref, dy_ref, dr_ref, dx_ref, dg_ref):
        xv = x_ref[...]
        r = lax.rsqrt(jnp.mean(xv * xv, axis=-1, keepdims=True) + EPS)
        xh = xv * r
        dyv = dy_ref[...]
        dyg = dyv * g_ref[...]
        dx_ref[...] = dr_ref[...] + r * (dyg - xh * jnp.mean(dyg * xh, axis=-1, keepdims=True))
        _accumulate(dg_ref, pl.program_id(0) == 0, _colsum(dyv * xh))

    row = pl.BlockSpec((tm, D), lambda i: (i, 0))
    vec = pl.BlockSpec((1, D), lambda i: (0, 0))
    return pl.pallas_call(
        body, name=name, grid=(n // tm,), in_specs=[row, vec, row, row], out_specs=[row, vec],
        out_shape=[jax.ShapeDtypeStruct((n, D), F32), jax.ShapeDtypeStruct((1, D), F32)],
        compiler_params=_cp(("arbitrary",)))(x, g, dy, dres)


def _final_loss(h, g, tgt, name):
    n = h.shape[0]
    tm = min(512, n)

    def body(x_ref, g_ref, t_ref, loss_ref, dx_ref, dg_ref):
        first = pl.program_id(0) == 0
        xv = x_ref[...]
        gv = g_ref[...]
        r = lax.rsqrt(jnp.mean(xv * xv, axis=-1, keepdims=True) + EPS)
        xh = xv * r
        err = xh * gv - t_ref[...]
        part = 0.5 * jnp.sum(jnp.mean(err * err, axis=-1, keepdims=True), axis=0, keepdims=True)
        _accumulate(loss_ref, first, jnp.broadcast_to(part, (1, 128)))
        dyv = err * (1.0 / D)
        dyg = dyv * gv
        dx_ref[...] = r * (dyg - xh * jnp.mean(dyg * xh, axis=-1, keepdims=True))
        _accumulate(dg_ref, first, _colsum(dyv * xh))

    row = pl.BlockSpec((tm, D), lambda i: (i, 0))
    vec = pl.BlockSpec((1, D), lambda i: (0, 0))
    return pl.pallas_call(
        body, name=name, grid=(n // tm,), in_specs=[row, vec, row],
        out_specs=[pl.BlockSpec((1, 128), lambda i: (0, 0)), row, vec],
        out_shape=[jax.ShapeDtypeStruct((1, 128), F32), jax.ShapeDtypeStruct((n, D), F32),
                   jax.ShapeDtypeStruct((1, D), F32)],
        compiler_params=_cp(("arbitrary",)))(h, g, tgt)


def _prev_rows(x, halo_ref, lanes, scale, row):
    h7 = halo_ref[7:8, lanes] * scale
    h6 = halo_ref[6:7, lanes] * scale
    p1 = jnp.where(row == 0, h7, pltpu.roll(x, 1, 0))
    p2 = jnp.where(row == 0, h6, jnp.where(row == 1, h7, pltpu.roll(x, 2, 0)))
    return p1, p2


def _halo_maps(tm, n_rows):
    r8 = tm // 8
    last = n_rows // 8 - 1
    prev = lambda i: jnp.maximum(i * r8 - 1, 0)
    nxt = lambda i: jnp.minimum((i + 1) * r8, last)
    return prev, nxt


def _lane_blocks(width):
    return [slice(lo, min(lo + 128, width)) for lo in range(0, width, 128)]


def _conv_taps(w_ref, b_ref, g, lanes):
    return w_ref[g, 0:1, lanes], w_ref[g, 1:2, lanes], w_ref[g, 2:3, lanes], b_ref[g, :, lanes]


def _conv_tile(x, prev1, prev2, taps, row):
    w0, w1, w2, b = taps
    r1 = pltpu.roll(x, 1, 0)
    r2 = pltpu.roll(x, 2, 0)
    x1 = jnp.where(row == 0, prev1, r1)
    x2 = jnp.where(row < 2, prev2, r2)
    return b + w0 * x2 + w1 * x1 + w2 * x, x1, x2, r1, r2


def _ffn_conv_fwd(up, cw, cb, seq, name):
    n = up.shape[2]
    tm = min(256, seq)
    prev, _ = _halo_maps(tm, n)

    def body(u_ref, h_ref, w_ref, b_ref, o_ref):
        i = pl.program_id(1)
        scale = jnp.where(lax.rem(i * tm, seq) == 0, 0.0, 1.0)
        for lanes in _lane_blocks(FSH):
            lw = lanes.stop - lanes.start
            row = lax.broadcasted_iota(jnp.int32, (8, lw), 0)
            taps = [_conv_taps(w_ref, b_ref, g, lanes) for g in range(2)]

            def tile(t, carry):
                rows = slice(t * 8, t * 8 + 8)
                hc, nxt = [], []
                for g in range(2):
                    conv, _, _, r1, r2 = _conv_tile(u_ref[g, rows, lanes], carry[2 * g], carry[2 * g + 1], taps[g], row)
                    hc.append(conv)
                    nxt += [r1, r2]
                return hc[0] * _sigmoid(hc[0]) * hc[1], tuple(nxt)

            def pair(m, carry):
                a, carry = tile(2 * m, carry)
                b, carry = tile(2 * m + 1, carry)
                o_ref[m * 16:m * 16 + 16, lanes] = jnp.concatenate([a, b], axis=0).astype(BF)
                return carry

            init = []
            for g in range(2):
                halo = h_ref[g, :, lanes] * scale
                init += [pltpu.roll(halo, 1, 0), pltpu.roll(halo, 2, 0)]
            carry = tuple(init)
            for m in range(tm // 16):
                carry = pair(m, carry)

    return pl.pallas_call(
        body, name=name, grid=(4, n // tm),
        in_specs=[pl.BlockSpec((2, None, tm, FSH), lambda j, i: (0, j, i, 0)),
                  pl.BlockSpec((2, None, 8, FSH), lambda j, i: (0, j, prev(i), 0)),
                  pl.BlockSpec((2, None, 3, FSH), lambda j, i: (0, j, 0, 0)),
                  pl.BlockSpec((2, None, 1, FSH), lambda j, i: (0, j, 0, 0))],
        out_specs=pl.BlockSpec((None, tm, FSH), lambda j, i: (j, i, 0)),
        out_shape=jax.ShapeDtypeStruct((4, n, FSH), BF), compiler_params=_cp(("parallel", "parallel")))(up, up, cw, cb)


def _ffn_conv_bwd(up, dact, cw, cb, seq, name):
    n = up.shape[2]
    tm = min(256, seq)
    nt = tm // 8
    prev, nxt = _halo_maps(tm, n)

    def body(u_ref, up_ref, un_ref, da_ref, dn_ref, w_ref, b_ref, du_ref, dw_ref, db_ref):
        i = pl.program_id(1)
        sp = jnp.where(lax.rem(i * tm, seq) == 0, 0.0, 1.0)
        sn = jnp.where(lax.rem((i + 1) * tm, seq) == 0, 0.0, 1.0)
        first = i == 0
        for lanes in _lane_blocks(FSH):
            lw = lanes.stop - lanes.start
            row = lax.broadcasted_iota(jnp.int32, (8, lw), 0)
            taps = [_conv_taps(w_ref, b_ref, g, lanes) for g in range(2)]

            def step(xs, da, st, accumulate):
                roll_st, sums = st
                hc, shifted, rolls = [], [], []
                for g in range(2):
                    conv, x1, x2, r1, r2 = _conv_tile(xs[g], roll_st[g][0], roll_st[g][1], taps[g], row)
                    hc.append(conv)
                    shifted.append((x2, x1, xs[g]))
                    rolls.append((r1, r2))
                s = _sigmoid(hc[0])
                dhc = (da * hc[1] * (s * (1.0 + hc[0] * (1.0 - s))), da * (hc[0] * s))
                dups, new_roll, new_sums = [], [], []
                for g in range(2):
                    w0, w1, w2, _ = taps[g]
                    _, _, last, last7, last6 = roll_st[g]
                    d7 = pltpu.roll(dhc[g], 7, 0)
                    d6 = pltpu.roll(dhc[g], 6, 0)
                    dups.append(w2 * last + w1 * jnp.where(row == 7, d7, last7) + w0 * jnp.where(row >= 6, d6, last6))
                    new_roll.append((rolls[g][0], rolls[g][1], dhc[g], d7, d6))
                    if accumulate:
                        new_sums.append(tuple(acc + dhc[g] * xk for acc, xk in zip(sums[g][:3], shifted[g]))
                                        + (sums[g][3] + dhc[g],))
                    else:
                        new_sums.append(sums[g])
                return dups, (tuple(new_roll), tuple(new_sums))

            def load(t):
                rows = slice(t * 8, t * 8 + 8)
                return [u_ref[g, rows, lanes] for g in range(2)], da_ref[rows, lanes]

            def store(m, dup_a, dup_b):
                for g in range(2):
                    du_ref[g, m * 16:m * 16 + 16, lanes] = jnp.concatenate(
                        [dup_a[g], dup_b[g]], axis=0).astype(BF)

            def pair(m, st):
                dup_a, st = step(*load(2 * m + 1), st, True)
                dup_b, st = step(*load(2 * m + 2), st, True)
                store(m, dup_a, dup_b)
                return st

            zero = jnp.zeros((8, lw), F32)
            roll0 = []
            for g in range(2):
                halo = up_ref[g, :, lanes] * sp
                roll0.append((pltpu.roll(halo, 1, 0), pltpu.roll(halo, 2, 0), zero, zero, zero))
            st = (tuple(roll0), ((zero,) * 4, (zero,) * 4))
            _, st = step(*load(0), st, True)
            for m in range(nt // 2 - 1):
                st = pair(m, st)
            dup_a, st = step(*load(nt - 1), st, True)
            dup_b, st = step([un_ref[g, :, lanes] for g in range(2)], dn_ref[:, lanes] * sn, st, False)
            store(nt // 2 - 1, dup_a, dup_b)
            for g in range(2):
                for k in range(3):
                    _accumulate(dw_ref.at[g, k:k + 1, lanes], first, _colsum(st[1][g][k]))
                _accumulate(db_ref.at[g, :, lanes], first, _colsum(st[1][g][3]))

    return pl.pallas_call(
        body, name=name, grid=(4, n // tm),
        in_specs=[pl.BlockSpec((2, None, tm, FSH), lambda j, i: (0, j, i, 0)),
                  pl.BlockSpec((2, None, 8, FSH), lambda j, i: (0, j, prev(i), 0)),
                  pl.BlockSpec((2, None, 8, FSH), lambda j, i: (0, j, nxt(i), 0)),
                  pl.BlockSpec((None, tm, FSH), lambda j, i: (j, i, 0)),
                  pl.BlockSpec((None, 8, FSH), lambda j, i: (j, nxt(i), 0)),
                  pl.BlockSpec((2, None, 3, FSH), lambda j, i: (0, j, 0, 0)),
                  pl.BlockSpec((2, None, 1, FSH), lambda j, i: (0, j, 0, 0))],
        out_specs=[pl.BlockSpec((2, None, tm, FSH), lambda j, i: (0, j, i, 0)),
                   pl.BlockSpec((2, None, 3, FSH), lambda j, i: (0, j, 0, 0)),
                   pl.BlockSpec((2, None, 1, FSH), lambda j, i: (0, j, 0, 0))],
        out_shape=[jax.ShapeDtypeStruct((2, 4, n, FSH), BF), jax.ShapeDtypeStruct((2, 4, 3, FSH), F32),
                   jax.ShapeDtypeStruct((2, 4, 1, FSH), F32)],
        compiler_params=_cp(("parallel", "arbitrary")))(up, up, up, dact, dact, cw, cb)


def _shortconv_fwd(p, cw, cb, seq, name):
    n = p.shape[0]
    tm = min(256, seq)
    prev, _ = _halo_maps(tm, n)

    def body(p_ref, h_ref, w_ref, b_ref, o_ref):
        i = pl.program_id(1)
        scale = jnp.where(lax.rem(i * tm, seq) == 0, 0.0, 1.0)
        q = p_ref[:, FB:2 * FB] * p_ref[:, 2 * FB:]
        row = lax.broadcasted_iota(jnp.int32, q.shape, 0)
        h7 = h_ref[7:8, FB:2 * FB] * h_ref[7:8, 2 * FB:] * scale
        h6 = h_ref[6:7, FB:2 * FB] * h_ref[6:7, 2 * FB:] * scale
        p1 = jnp.where(row == 0, h7, pltpu.roll(q, 1, 0))
        p2 = jnp.where(row == 0, h6, jnp.where(row == 1, h7, pltpu.roll(q, 2, 0)))
        conv = b_ref[...] + w_ref[0:1, :] * p2 + w_ref[1:2, :] * p1 + w_ref[2:3, :] * q
        o_ref[...] = (p_ref[:, :FB] * conv).astype(BF)

    return pl.pallas_call(
        body, name=name, grid=(D // FB, n // tm),
        in_specs=[pl.BlockSpec((tm, 3 * FB), lambda j, i: (i, j)),
                  pl.BlockSpec((8, 3 * FB), lambda j, i: (prev(i), j)),
                  pl.BlockSpec((3, FB), lambda j, i: (0, j)),
                  pl.BlockSpec((1, FB), lambda j, i: (0, j))],
        out_specs=pl.BlockSpec((tm, FB), lambda j, i: (i, j)),
        out_shape=jax.ShapeDtypeStruct((n, D), BF), compiler_params=_cp(("parallel", "parallel")))(p, p, cw, cb)


def _shortconv_bwd(p, dmix, cw, cb, seq, name):
    n = p.shape[0]
    tm = min(256, seq)
    ext = tm + 16
    prev, nxt = _halo_maps(tm, n)

    def body(p_ref, pp_ref, pn_ref, dm_ref, dn_ref, w_ref, b_ref, dp_ref, dw_ref, db_ref, qx, cx):
        i = pl.program_id(1)
        sp = jnp.where(lax.rem(i * tm, seq) == 0, 0.0, 1.0)
        sn = jnp.where(lax.rem((i + 1) * tm, seq) == 0, 0.0, 1.0)
        bg, cg, hx = p_ref[:, :FB], p_ref[:, FB:2 * FB], p_ref[:, 2 * FB:]
        dm = dm_ref[...]
        qx[0:8, :] = pp_ref[:, FB:2 * FB] * pp_ref[:, 2 * FB:] * sp
        qx[8:8 + tm, :] = cg * hx
        qx[8 + tm:, :] = jnp.zeros((8, FB), F32)
        cx[0:8, :] = jnp.zeros((8, FB), F32)
        cx[8:8 + tm, :] = dm * bg
        cx[8 + tm:, :] = dn_ref[...] * pn_ref[:, :FB] * sn
        q0 = qx[...]
        q1 = pltpu.roll(q0, 1, 0)
        q2 = pltpu.roll(q0, 2, 0)
        main = slice(8, 8 + tm)
        conv = b_ref[...] + w_ref[0:1, :] * q2[main] + w_ref[1:2, :] * q1[main] + w_ref[2:3, :] * q0[main]
        dc = cx[...]
        dq = (w_ref[2:3, :] * dc + w_ref[1:2, :] * pltpu.roll(dc, ext - 1, 0)
              + w_ref[0:1, :] * pltpu.roll(dc, ext - 2, 0))[main]
        dp_ref[:, :FB] = (dm * conv).astype(BF)
        dp_ref[:, FB:2 * FB] = (dq * hx).astype(BF)
        dp_ref[:, 2 * FB:] = (dq * cg).astype(BF)
        first = i == 0
        dcm = dc[main]
        _accumulate(dw_ref.at[0:1, :], first, _colsum(dcm * q2[main]))
        _accumulate(dw_ref.at[1:2, :], first, _colsum(dcm * q1[main]))
        _accumulate(dw_ref.at[2:3, :], first, _colsum(dcm * q0[main]))
        _accumulate(db_ref, first, _colsum(dcm))

    return pl.pallas_call(
        body, name=name, grid=(D // FB, n // tm),
        in_specs=[pl.BlockSpec((tm, 3 * FB), lambda j, i: (i, j)),
                  pl.BlockSpec((8, 3 * FB), lambda j, i: (prev(i), j)),
                  pl.BlockSpec((8, 3 * FB), lambda j, i: (nxt(i), j)),
                  pl.BlockSpec((tm, FB), lambda j, i: (i, j)),
                  pl.BlockSpec((8, FB), lambda j, i: (nxt(i), j)),
                  pl.BlockSpec((3, FB), lambda j, i: (0, j)),
                  pl.BlockSpec((1, FB), lambda j, i: (0, j))],
        out_specs=[pl.BlockSpec((tm, 3 * FB), lambda j, i: (i, j)),
                   pl.BlockSpec((3, FB), lambda j, i: (0, j)),
                   pl.BlockSpec((1, FB), lambda j, i: (0, j))],
        out_shape=[jax.ShapeDtypeStruct((n, 3 * D), BF), jax.ShapeDtypeStruct((3, D), F32),
                   jax.ShapeDtypeStruct((1, D), F32)],
        scratch_shapes=[pltpu.VMEM((ext, FB), F32), pltpu.VMEM((ext, FB), F32)],
        compiler_params=_cp(("parallel", "arbitrary")))(p, p, p, dmix, dmix, cw, cb)


def _gmlp_fwd(uv, wm, bst, gv, seq, name):
    n = uv.shape[0]
    tm = min(256, seq)

    def body(x_ref, w_ref, b_ref, g_ref, o_ref):
        ge_v = _gelu(x_ref[:, GM_W:])
        r = lax.rsqrt(jnp.mean(ge_v * ge_v, axis=-1, keepdims=True) + EPS)
        vn = (ge_v * r * g_ref[...]).astype(BF)
        for c in range(tm // CHUNK):
            rows = slice(c * CHUNK, (c + 1) * CHUNK)
            for h in range(GM_HEADS):
                cols = slice(h * CHUNK, (h + 1) * CHUNK)
                gate = jnp.dot(w_ref[h], vn[rows, cols], preferred_element_type=F32) + b_ref[:, h:h + 1]
                o_ref[rows, cols] = (_gelu(x_ref[rows, cols]) * gate).astype(BF)

    return pl.pallas_call(
        body, name=name, grid=(n // tm,),
        in_specs=[pl.BlockSpec((tm, 2 * GM_W), lambda i: (i, 0)),
                  pl.BlockSpec((GM_HEADS, CHUNK, CHUNK), lambda i: (0, 0, 0)),
                  pl.BlockSpec((CHUNK, GM_HEADS), lambda i: (0, 0)),
                  pl.BlockSpec((1, GM_W), lambda i: (0, 0))],
        out_specs=pl.BlockSpec((tm, GM_W), lambda i: (i, 0)),
        out_shape=jax.ShapeDtypeStruct((n, GM_W), BF), compiler_params=_cp(("parallel",)))(uv, wm, bst, gv)


def _gmlp_bwd(uv, dout, wm, wmt, bst, gv, seq, name):
    n = uv.shape[0]
    tm = min(256, seq)

    def body(x_ref, do_ref, w_ref, wt_ref, b_ref, g_ref, dx_ref, dw_ref, db_ref, dg_ref, dvn_scr):
        first = pl.program_id(0) == 0
        ge_v = _gelu(x_ref[:, GM_W:])
        r = lax.rsqrt(jnp.mean(ge_v * ge_v, axis=-1, keepdims=True) + EPS)
        vh = ge_v * r
        vn = (vh * g_ref[...]).astype(BF)
        tril = (lax.broadcasted_iota(jnp.int32, (CHUNK, CHUNK), 0)
                >= lax.broadcasted_iota(jnp.int32, (CHUNK, CHUNK), 1))
        for h in range(GM_HEADS):
            cols = slice(h * CHUNK, (h + 1) * CHUNK)
            dw = jnp.zeros((CHUNK, CHUNK), F32)
            dbs = jnp.zeros((CHUNK, 1), F32)
            for c in range(tm // CHUNK):
                rows = slice(c * CHUNK, (c + 1) * CHUNK)
                blk = vn[rows, cols]
                gate = jnp.dot(w_ref[h], blk, preferred_element_type=F32) + b_ref[:, h:h + 1]
                xu = x_ref[rows, cols]
                do = do_ref[rows, cols]
                dx_ref[rows, cols] = (do * gate * _gelu_grad(xu)).astype(BF)
                dgate = do * _gelu(xu)
                dgb = dgate.astype(BF)
                dw = dw + lax.dot_general(dgb, blk, _DIMS['nt'], preferred_element_type=F32)
                dbs = dbs + jnp.sum(dgate, axis=1, keepdims=True)
                dvn_scr[rows, cols] = jnp.dot(wt_ref[h], dgb, preferred_element_type=F32)
            _accumulate(dw_ref.at[h], first, jnp.where(tril, dw, 0.0))
            _accumulate(db_ref.at[h], first, dbs)
        dvn = dvn_scr[...]
        _accumulate(dg_ref, first, _colsum(dvn * vh))
        dvh = dvn * g_ref[...]
        dv = r * (dvh - vh * jnp.mean(dvh * vh, axis=-1, keepdims=True))
        dx_ref[:, GM_W:] = (dv * _gelu_grad(x_ref[:, GM_W:])).astype(BF)

    full3 = pl.BlockSpec((GM_HEADS, CHUNK, CHUNK), lambda i: (0, 0, 0))
    return pl.pallas_call(
        body, name=name, grid=(n // tm,),
        in_specs=[pl.BlockSpec((tm, 2 * GM_W), lambda i: (i, 0)), pl.BlockSpec((tm, GM_W), lambda i: (i, 0)),
                  full3, full3, pl.BlockSpec((CHUNK, GM_HEADS), lambda i: (0, 0)),
                  pl.BlockSpec((1, GM_W), lambda i: (0, 0))],
        out_specs=[pl.BlockSpec((tm, 2 * GM_W), lambda i: (i, 0)), full3,
                   pl.BlockSpec((GM_HEADS, CHUNK, 1), lambda i: (0, 0, 0)),
                   pl.BlockSpec((1, GM_W), lambda i: (0, 0))],
        out_shape=[jax.ShapeDtypeStruct((n, 2 * GM_W), BF), jax.ShapeDtypeStruct((GM_HEADS, CHUNK, CHUNK), F32),
                   jax.ShapeDtypeStruct((GM_HEADS, CHUNK, 1), F32), jax.ShapeDtypeStruct((1, GM_W), F32)],
        scratch_shapes=[pltpu.VMEM((tm, GM_W), F32)],
        compiler_params=_cp(("arbitrary",)))(uv, dout, wm, wmt, bst, gv)


def _s5_disc(lam_re, lam_im, log_dt, b_re, b_im):
    lr = jnp.minimum(lam_re, LAM_MAX)
    li = lam_im
    dt = jnp.exp(log_dt)
    mag = jnp.exp(lr * dt)
    ab_re = mag * jnp.cos(li * dt)
    ab_im = mag * jnp.sin(li * dt)
    den = lr * lr + li * li
    nr = ab_re - 1.0
    ni = ab_im
    z_re = (nr * lr + ni * li) / den
    z_im = (ni * lr - nr * li) / den
    return ab_re, ab_im, z_re * b_re - z_im * b_im, z_re * b_im + z_im * b_re


def _s5_disc_fwd(args, name):
    shp = jax.ShapeDtypeStruct(args[0].shape, F32)

    def body(*refs):
        outs = _s5_disc(*[r[...] for r in refs[:5]])
        for o_ref, o in zip(refs[5:], outs):
            o_ref[...] = o

    return pl.pallas_call(body, name=name, out_shape=[shp] * 4)(*args)


def _s5_disc_bwd(args, cts, name):
    shp = jax.ShapeDtypeStruct(args[0].shape, F32)

    def body(*refs):
        _, vjp = jax.vjp(_s5_disc, *[r[...] for r in refs[:5]])
        grads = vjp(tuple(r[...] for r in refs[5:9]))
        for o_ref, o in zip(refs[9:], grads):
            o_ref[...] = o

    return pl.pallas_call(body, name=name, out_shape=[shp] * 5)(*args, *cts)


def _cmul(a, b):
    return a[0] * b[0] - a[1] * b[1], a[0] * b[1] + a[1] * b[0]


def _scan_tables(ar, ai, reverse):
    if reverse:
        ai = -ai
    a1 = (ar, ai)
    a2 = _cmul(a1, a1)
    a3 = _cmul(a2, a1)
    a4 = _cmul(a2, a2)
    powers = [a1, a2, a3, a4, _cmul(a4, a1), _cmul(a4, a2), _cmul(a4, a3), _cmul(a4, a4)]
    row = lax.broadcasted_iota(jnp.int32, (8, NST), 0)
    zero = jnp.zeros((8, NST), F32)
    pr, pi = zero, zero
    for r in range(8):
        pw = powers[7 - r] if reverse else powers[r]
        pr = jnp.where(row == r, pw[0], pr)
        pi = jnp.where(row == r, pw[1], pi)
    levels = []
    for d, pw in ((1, a1), (2, a2), (4, a4)):
        ok = (row <= 7 - d) if reverse else (row >= d)
        levels.append((d, jnp.where(ok, pw[0], zero), jnp.where(ok, pw[1], zero)))
    return (pr, pi), levels


def _scan_block(src, dst, car, tables, n_tiles, reverse):
    (pr, pi), levels = tables
    row = lax.broadcasted_iota(jnp.int32, (8, NST), 0)
    out_row = 0 if reverse else 7

    def step(t, carry):
        cr, ci = carry
        tile = (n_tiles - 1 - t) if reverse else t
        rows = pl.ds(pl.multiple_of(tile * 8, 8), 8)
        xr = src[rows, 0:NST]
        xi = src[rows, NST:2 * NST]
        for d, dr, di in levels:
            shift = 8 - d if reverse else d
            rr = pltpu.roll(xr, shift, 0)
            ri = pltpu.roll(xi, shift, 0)
            xr, xi = xr + dr * rr - di * ri, xi + dr * ri + di * rr
        hr = xr + pr * cr - pi * ci
        hi = xi + pr * ci + pi * cr
        dst[rows, 0:NST] = hr
        dst[rows, NST:2 * NST] = hi
        return (_colsum(jnp.where(row == out_row, hr, 0.0)), _colsum(jnp.where(row == out_row, hi, 0.0)))

    cr, ci = lax.fori_loop(0, n_tiles, step, (car[0:1, 0:NST], car[0:1, NST:2 * NST]))
    car[0:1, 0:NST] = cr
    car[0:1, NST:2 * NST] = ci


def _s5_fwd(u, ab, bbt, cmat, dvec, wglu, bglu, seq, name):
    n = u.shape[0]
    tm = min(256, seq)

    def body(u_ref, ab_ref, bb_ref, c_ref, d_ref, w_ref, b_ref, h_ref, o_ref, xs, car):
        i = pl.program_id(0)

        @pl.when(lax.rem(i * tm, seq) == 0)
        def _():
            car[...] = jnp.zeros(car.shape, F32)

        uv = u_ref[...]
        xs[...] = jnp.dot(uv.astype(BF), bb_ref[...], preferred_element_type=F32)
        tables = _scan_tables(ab_ref[0:1, 0:NST], ab_ref[0:1, NST:2 * NST], False)
        _scan_block(xs, h_ref, car, tables, tm // 8, False)
        y = jnp.dot(h_ref[...].astype(BF), c_ref[...], preferred_element_type=F32) + d_ref[...] * uv
        g1 = _gelu(y)
        z = jnp.dot(g1.astype(BF), w_ref[...], preferred_element_type=F32) + b_ref[...]
        o_ref[...] = (g1 * _sigmoid(z)).astype(BF)

    const = lambda shape: pl.BlockSpec(shape, lambda i: (0, 0))
    return pl.pallas_call(
        body, name=name, grid=(n // tm,),
        in_specs=[pl.BlockSpec((tm, SSM_W), lambda i: (i, 0)), const((1, 2 * NST)), const((SSM_W, 2 * NST)),
                  const((2 * NST, SSM_W)), const((1, SSM_W)), const((SSM_W, SSM_W)), const((1, SSM_W))],
        out_specs=[pl.BlockSpec((tm, 2 * NST), lambda i: (i, 0)), pl.BlockSpec((tm, SSM_W), lambda i: (i, 0))],
        out_shape=[jax.ShapeDtypeStruct((n, 2 * NST), F32), jax.ShapeDtypeStruct((n, SSM_W), BF)],
        scratch_shapes=[pltpu.VMEM((tm, 2 * NST), F32), pltpu.VMEM((8, 2 * NST), F32)],
        compiler_params=_cp(("arbitrary",)))(u, ab, bbt, cmat, dvec, wglu, bglu)


def _s5_bwd(da, u, hst, ab, bbt, cmat, dvec, wglu, bglu, seq, name):
    n = u.shape[0]
    tm = min(256, seq)
    nb = n // tm
    blk = lambda r: nb - 1 - r
    prev, _ = _halo_maps(tm, n)

    def body(da_ref, u_ref, h_ref, hp_ref, ab_ref, bb_ref, c_ref, d_ref, w_ref, b_ref,
             du_ref, dw_ref, dbg_ref, dd_ref, dc_ref, dbb_ref, dab_ref, gs, car):
        r = pl.program_id(0)
        i = blk(r)
        first = r == 0

        @pl.when(lax.rem((i + 1) * tm, seq) == 0)
        def _():
            car[...] = jnp.zeros(car.shape, F32)

        uv = u_ref[...]
        dav = da_ref[...]
        hb = h_ref[...]
        hb16 = hb.astype(BF)
        dvv = d_ref[...]
        y = jnp.dot(hb16, c_ref[...], preferred_element_type=F32) + dvv * uv
        g1 = _gelu(y)
        g16 = g1.astype(BF)
        s = _sigmoid(jnp.dot(g16, w_ref[...], preferred_element_type=F32) + b_ref[...])
        dz = dav * g1 * s * (1.0 - s)
        dz16 = dz.astype(BF)
        dg1 = dav * s + lax.dot_general(dz16, w_ref[...], _DIMS['nt'], preferred_element_type=F32)
        _accumulate(dw_ref, first, lax.dot_general(g16, dz16, _DIMS['tn'], preferred_element_type=F32))
        _accumulate(dbg_ref, first, _colsum(dz))
        dy = dg1 * _gelu_grad(y)
        dy16 = dy.astype(BF)
        _accumulate(dd_ref, first, _colsum(dy * uv))
        _accumulate(dc_ref, first, lax.dot_general(hb16, dy16, _DIMS['tn'], preferred_element_type=F32))
        gs[...] = lax.dot_general(dy16, c_ref[...], _DIMS['nt'], preferred_element_type=F32)
        tables = _scan_tables(ab_ref[0:1, 0:NST], ab_ref[0:1, NST:2 * NST], True)
        _scan_block(gs, gs, car, tables, tm // 8, True)
        g = gs[...]
        g16b = g.astype(BF)
        sp = jnp.where(lax.rem(i * tm, seq) == 0, 0.0, 1.0)
        row = lax.broadcasted_iota(jnp.int32, hb.shape, 0)
        hprev = jnp.where(row == 0, hp_ref[7:8, :] * sp, pltpu.roll(hb, 1, 0))
        gr, gi = g[:, :NST], g[:, NST:]
        hr, hi = hprev[:, :NST], hprev[:, NST:]
        _accumulate(dab_ref.at[:, 0:NST], first, _colsum(gr * hr + gi * hi))
        _accumulate(dab_ref.at[:, NST:2 * NST], first, _colsum(gi * hr - gr * hi))
        _accumulate(dbb_ref, first, lax.dot_general(uv.astype(BF), g16b, _DIMS['tn'], preferred_element_type=F32))
        du = dy * dvv + lax.dot_general(g16b, bb_ref[...], _DIMS['nt'], preferred_element_type=F32)
        du_ref[...] = du.astype(BF)

    const = lambda shape: pl.BlockSpec(shape, lambda r: (0, 0))
    rowspec = lambda w: pl.BlockSpec((tm, w), lambda r: (blk(r), 0))
    return pl.pallas_call(
        body, name=name, grid=(nb,),
        in_specs=[rowspec(SSM_W), rowspec(SSM_W), rowspec(2 * NST),
                  pl.BlockSpec((8, 2 * NST), lambda r: (prev(blk(r)), 0)),
                  const((1, 2 * NST)), const((SSM_W, 2 * NST)), const((2 * NST, SSM_W)), const((1, SSM_W)),
                  const((SSM_W, SSM_W)), const((1, SSM_W))],
        out_specs=[rowspec(SSM_W), const((SSM_W, SSM_W)), const((1, SSM_W)), const((1, SSM_W)),
                   const((2 * NST, SSM_W)), const((SSM_W, 2 * NST)), const((1, 2 * NST))],
        out_shape=[jax.ShapeDtypeStruct((n, SSM_W), BF), jax.ShapeDtypeStruct((SSM_W, SSM_W), F32),
                   jax.ShapeDtypeStruct((1, SSM_W), F32), jax.ShapeDtypeStruct((1, SSM_W), F32),
                   jax.ShapeDtypeStruct((2 * NST, SSM_W), F32), jax.ShapeDtypeStruct((SSM_W, 2 * NST), F32),
                   jax.ShapeDtypeStruct((1, 2 * NST), F32)],
        scratch_shapes=[pltpu.VMEM((tm, 2 * NST), F32), pltpu.VMEM((8, 2 * NST), F32)],
        compiler_params=_cp(("arbitrary",)))(da, u, hst, hst, ab, bbt, cmat, dvec, wglu, bglu)


def _s5_rows(lam_re, lam_im, log_dt, b_re, b_im):
    rep = lambda a: jnp.broadcast_to(a[:, None, :], (SSM_G, SSM_H, SSM_P)).reshape(SSM_W, SSM_P)
    dt = jnp.broadcast_to(log_dt[:, None, None], (SSM_G, SSM_H, SSM_P)).reshape(SSM_W, SSM_P)
    tr = lambda b: b.transpose(0, 2, 1).reshape(SSM_W, SSM_P)
    return rep(lam_re), rep(lam_im), dt, tr(b_re), tr(b_im)


def _block_diag(rows_gp, inner):
    eye = jnp.eye(SSM_G, dtype=rows_gp.dtype)
    return (rows_gp[:, :, None, :] * eye[:, None, :, None]).reshape(SSM_G * inner, SSM_G * SSM_P)


def _diag_blocks(mat, inner):
    m4 = mat.reshape(SSM_G, inner, SSM_G, SSM_P)
    return jnp.stack([m4[g, :, g, :] for g in range(SSM_G)])


def _interleave(w, parts):
    lead = w.shape[:-1]
    nb = w.shape[-1] // (parts * FB)
    return jnp.swapaxes(w.reshape(lead + (parts, nb, FB)), -3, -2).reshape(w.shape)


def _deinterleave(w, parts):
    lead = w.shape[:-1]
    nb = w.shape[-1] // (parts * FB)
    return jnp.swapaxes(w.reshape(lead + (nb, parts, FB)), -3, -2).reshape(w.shape)


def _ffn_fwd(h, g, w_up, w_down, cw, cb, seq, tag):
    n = h.shape[0]
    tm = min(1024, n)
    ni = n // tm
    f = _rmsnorm_fwd(h, g, f"{tag}_norm")
    up = _matmul_spec(
        f, w_up, 'nn', (NDEV, ni, 1),
        pl.BlockSpec((tm, D), lambda s, i, k: (i, 0)),
        pl.BlockSpec((D, FSH), lambda s, i, k: (s, 0)),
        pl.BlockSpec((tm, FSH), lambda s, i, k: (s * ni + i, 0)), (NDEV * n, FSH), f"{tag}_up")
    up = up.reshape(2, 4, n, FSH)
    act = _ffn_conv_fwd(up, cw, cb, seq, f"{tag}_conv")
    tn = 512
    out = _matmul_spec(
        act.reshape(4 * n, FSH), w_down, 'nn', (ni, D // tn, 4),
        pl.BlockSpec((tm, FSH), lambda i, j, k: (k * ni + i, 0)),
        pl.BlockSpec((FSH, tn), lambda i, j, k: (k, j)),
        pl.BlockSpec((tm, tn), lambda i, j, k: (i, j)), (n, D), f"{tag}_down", resid=h)
    return out, (f, up, act)


def _ffn_bwd(dh, h, g, w_up, w_down, cw, cb, saved, seq, tag):
    f, up, act = saved
    n = h.shape[0]
    tm = min(1024, n)
    ni = n // tm
    tk = min(1024, n)
    nk = n // tk
    dact = _matmul_spec(
        dh, w_down, 'nt', (4, ni, 1),
        pl.BlockSpec((tm, D), lambda j, i, k: (i, 0)),
        pl.BlockSpec((FSH, D), lambda j, i, k: (j, 0)),
        pl.BlockSpec((tm, FSH), lambda j, i, k: (j * ni + i, 0)), (4 * n, FSH), f"{tag}_ddown_x")
    tn = 512
    dw_down = _matmul_spec(
        act.reshape(4 * n, FSH), dh, 'tn', (4, D // tn, nk),
        pl.BlockSpec((tk, FSH), lambda j, c, k: (j * nk + k, 0)),
        pl.BlockSpec((tk, tn), lambda j, c, k: (k, c)),
        pl.BlockSpec((FSH, tn), lambda j, c, k: (j, c)), (DFF, D), f"{tag}_ddown_w", out_dtype=BF)
    dup, dcw, dcb = _ffn_conv_bwd(up, dact.reshape(4, n, FSH), cw, cb, seq, f"{tag}_dconv")
    dup2 = dup.reshape(NDEV * n, FSH)
    df = _matmul_spec(
        dup2, w_up, 'nt', (ni, 1, NDEV),
        pl.BlockSpec((tm, FSH), lambda i, j, k: (k * ni + i, 0)),
        pl.BlockSpec((D, FSH), lambda i, j, k: (k, 0)),
        pl.BlockSpec((tm, D), lambda i, j, k: (i, 0)), (n, D), f"{tag}_dup_x")
    dw_up = _matmul_spec(
        f, dup2, 'tn', (NDEV, 1, nk),
        pl.BlockSpec((tk, D), lambda s, j, k: (k, 0)),
        pl.BlockSpec((tk, FSH), lambda s, j, k: (s * nk + k, 0)),
        pl.BlockSpec((D, FSH), lambda s, j, k: (s, 0)), (NDEV * D, FSH), f"{tag}_dup_w", out_dtype=BF)
    dh_in, dg = _rmsnorm_bwd(h, g, df, dh, f"{tag}_dnorm")
    grads = dict(g=dg, w_up=dw_up.reshape(NDEV, D, FSH), w_down=dw_down.reshape(NDEV, DFF // NDEV, D),
                 cw=dcw.reshape(NDEV, 3, FSH), cb=dcb.reshape(2 * DFF))
    return dh_in, grads


def _col_shards(w, width):
    return w.reshape(w.shape[0], NDEV, width).transpose(1, 0, 2)


def _local_step(x, tgt, w, gw, wait_ffn0, wait_rest, token, scatter, seq):
    bf = lambda a: a.astype(BF)
    row = lambda a: a.reshape(1, -1).astype(F32)
    w_ev = gw['ev_w_in'].transpose(1, 0, 2).reshape(D, 1792)
    w_ev_s5, w_ev_gm = w_ev[:, :SSM_W], w_ev[:, SSM_W:]
    w_evo = gw['ev_w_out'].reshape(D, D)
    f_cb = [w['ffn_conv_b'][l].reshape(2, 4, 1, FSH) for l in range(2)]
    tril = jnp.tril(jnp.ones((CHUNK, CHUNK), dtype=bool))
    gm_w = jnp.where(tril, w['gm_w_s'][0], 0.0)
    gm_wm, gm_wmt = bf(gm_w), bf(jnp.swapaxes(gm_w, 1, 2))
    gm_bt = w['gm_b_s'][0].T
    gm_gv = row(w['gm_v_g'][0])
    s5_in = _s5_rows(w['s5_lam_re'][0], w['s5_lam_im'][0], w['s5_log_dt'][0], w['s5_b_re'][0], w['s5_b_im'][0])
    ab_re, ab_im, bb_re, bb_im = _s5_disc_fwd(s5_in, "s5_disc")
    first_h = lambda a: a.reshape(SSM_G, SSM_H, SSM_P)[:, 0, :].reshape(1, NST)
    s5_ab = jnp.concatenate([first_h(ab_re), first_h(ab_im)], axis=1)
    to_gp = lambda a: a.reshape(SSM_G, SSM_H, SSM_P)
    s5_bbt = bf(jnp.concatenate([_block_diag(to_gp(bb_re), SSM_H), _block_diag(to_gp(bb_im), SSM_H)], axis=1))
    s5_cmat = bf(jnp.concatenate([_block_diag(w['s5_c_re'][0], SSM_H).T, -_block_diag(w['s5_c_im'][0], SSM_H).T],
                                 axis=0))
    s5_d, s5_bg, s5_wg = row(w['s5_d'][0]), row(w['s5_b_glu'][0]), gw['s5_w_glu'].reshape(SSM_W, SSM_W)
    g_mix = [row(w['mix_norm_g'][0]) + token[0:1, 0:1], row(w['mix_norm_g'][1])]
    g_ffn = [row(w['ffn_norm_g'][l]) for l in range(2)]
    g_fin = row(w['final_norm_g'])

    h0 = x
    y0 = _rmsnorm_fwd(h0, g_mix[0], "ev_norm")
    p_s5 = _matmul(y0, w_ev_s5, 'nn', 1024, 256, D, "ev_in_s5")
    p_gm = _matmul(y0, w_ev_gm, 'nn', 1024, 512, D, "ev_in_gm")
    hst, a_out = _s5_fwd(p_s5, s5_ab, s5_bbt, s5_cmat, s5_d, s5_wg, s5_bg, seq, "s5_fwd")
    b_out = _gmlp_fwd(p_gm, gm_wm, gm_bt, gm_gv, seq, "gmlp_fwd")
    mixcat = jnp.concatenate([a_out, b_out], axis=1)
    h1 = _matmul(mixcat, w_evo, 'nn', 1024, 512, D, "ev_out", resid=h0)
    g0 = wait_ffn0(mixcat)
    w_up0, w_dn0 = g0['ffn_w_up0'].reshape(NDEV * D, FSH), g0['ffn_w_down0'].reshape(DFF, D)
    f_cw0 = g0['ffn_conv_w0'].reshape(2, 4, 3, FSH)
    h2, ffn0 = _ffn_fwd(h1, g_ffn[0], w_up0, w_dn0, f_cw0, f_cb[0], seq, "ffn0")
    g1 = wait_rest(h2)
    w_od = _interleave(g1['od_w_in'].transpose(1, 0, 2).reshape(D, 3 * D), 3)
    w_odo = g1['od_w_out'].reshape(D, D)
    od_cw = g1['od_conv_w'].transpose(1, 0, 2).reshape(3, D)
    od_cb = g1['od_conv_b'].reshape(1, D)
    w_up1, w_dn1 = g1['ffn_w_up1'].reshape(NDEV * D, FSH), g1['ffn_w_down1'].reshape(DFF, D)
    f_cw1 = g1['ffn_conv_w1'].reshape(2, 4, 3, FSH)
    y1 = _rmsnorm_fwd(h2, g_mix[1], "od_norm")
    p_od = _matmul(y1, w_od, 'nn', 1024, 512, D, "od_in")
    mixin = _shortconv_fwd(p_od, od_cw, od_cb, seq, "od_conv")
    h3 = _matmul(mixin, w_odo, 'nn', 1024, 512, D, "od_out", resid=h2)
    h4, ffn1 = _ffn_fwd(h3, g_ffn[1], w_up1, w_dn1, f_cw1, f_cb[1], seq, "ffn1")
    loss, dh4, dg_fin = _final_loss(h4, g_fin, tgt, "final_loss")

    dh3, gf1 = _ffn_bwd(dh4, h3, g_ffn[1], w_up1, w_dn1, f_cw1, f_cb[1], ffn1, seq, "ffn1")
    dmixin = _matmul(dh3, w_odo, 'nt', 1024, 512, D, "od_dout_x")
    dw_odo = _matmul(mixin, dh3, 'tn', D, 512, 1024, "od_dout_w", out_dtype=BF)
    dp_od, d_od_cw, d_od_cb = _shortconv_bwd(p_od, dmixin, od_cw, od_cb, seq, "od_dconv")
    dy1 = _matmul(dp_od, w_od, 'nt', 1024, D, 512, "od_din_x")
    dw_od = _matmul(y1, dp_od, 'tn', D, 512, 1024, "od_din_w", out_dtype=BF)
    sent = scatter("scatter_layer1", {
        'od_w_in': _col_shards(_deinterleave(dw_od, 3), 384), 'od_conv_w': _col_shards(d_od_cw, D // NDEV),
        'od_conv_b': d_od_cb.reshape(NDEV, 1, D // NDEV), 'od_w_out': dw_odo.reshape(NDEV, D // NDEV, D),
        'ffn_w_up1': gf1['w_up'], 'ffn_conv_w1': gf1['cw'], 'ffn_w_down1': gf1['w_down']})
    dh2, dg_mix1 = _rmsnorm_bwd(h2, g_mix[1] + sent[0:1, 0:1], dy1, dh3, "od_dnorm")
    dh1, gf0 = _ffn_bwd(dh2, h1, g_ffn[0], w_up0, w_dn0, f_cw0, f_cb[0], ffn0, seq, "ffn0")
    dmix_a = _matmul(dh1, w_evo[:SSM_W], 'nt', 1024, SSM_W, D, "ev_dout_xa")
    dmix_b = _matmul(dh1, w_evo[SSM_W:], 'nt', 1024, GM_W, D, "ev_dout_xb")
    dw_evo = _matmul(mixcat, dh1, 'tn', D, 512, 1024, "ev_dout_w", out_dtype=BF)
    sent = scatter("scatter_ffn0", {'ffn_w_up0': gf0['w_up'], 'ffn_conv_w0': gf0['cw'], 'ffn_w_down0': gf0['w_down'],
                                    'ev_w_out': dw_evo.reshape(NDEV, D // NDEV, D)})
    dp_s5, d_wg, d_bg, d_d, d_cmat, d_bbt, d_ab = _s5_bwd(dmix_a, p_s5, hst, s5_ab, s5_bbt, s5_cmat,
                                                           s5_d + sent[0:1, 0:1], s5_wg, s5_bg, seq, "s5_bwd")
    dp_gm, d_gmw, d_gmb, d_gmg = _gmlp_bwd(p_gm, dmix_b, gm_wm, gm_wmt, gm_bt, gm_gv, seq, "gmlp_bwd")
    dw_ev = jnp.concatenate([_matmul(y0, dp_s5, 'tn', D, SSM_W, 1024, "ev_din_wa", out_dtype=BF),
                             _matmul(y0, dp_gm, 'tn', D, 512, 1024, "ev_din_wb", out_dtype=BF)], axis=1)
    sent = scatter("scatter_even", {'ev_w_in': _col_shards(dw_ev, 224),
                                    's5_w_glu': d_wg.reshape(NDEV, SSM_W // NDEV, SSM_W)})
    dy0 = _matmul(dp_gm, w_ev_gm, 'nt', 1024, D, 512, "ev_din_xb")
    dy0 = _matmul(dp_s5, w_ev_s5, 'nt', 1024, D, SSM_W, "ev_din_xa", resid=dy0)
    grad_x, dg_mix0 = _rmsnorm_bwd(h0, g_mix[0] + sent[0:1, 0:1], dy0, dh1, "ev_dnorm")

    put_h0 = lambda a: jnp.zeros((SSM_G, SSM_H, SSM_P), F32).at[:, 0, :].set(a.reshape(SSM_G, SSM_P)).reshape(
        SSM_W, SSM_P)
    ct = (put_h0(d_ab[:, :NST]), put_h0(d_ab[:, NST:]),
          _diag_blocks(d_bbt[:, :NST], SSM_H).reshape(SSM_W, SSM_P),
          _diag_blocks(d_bbt[:, NST:], SSM_H).reshape(SSM_W, SSM_P))
    d_lre, d_lim, d_ldt, d_bre, d_bim = _s5_disc_bwd(s5_in, ct, "s5_ddisc")
    over_h = lambda a: a.reshape(SSM_G, SSM_H, SSM_P).sum(axis=1)
    un_tr = lambda a: a.reshape(SSM_G, SSM_H, SSM_P).transpose(0, 2, 1)
    d_cre = _diag_blocks(d_cmat[:NST].T, SSM_H)
    d_cim = -_diag_blocks(d_cmat[NST:].T, SSM_H)

    repl = {
        'mix_norm_g': jnp.concatenate([dg_mix0, dg_mix1], axis=0),
        'ffn_norm_g': jnp.concatenate([gf0['g'], gf1['g']], axis=0),
        'final_norm_g': dg_fin.reshape(D),
        's5_lam_re': over_h(d_lre)[None], 's5_lam_im': over_h(d_lim)[None],
        's5_log_dt': over_h(d_ldt).sum(axis=1)[None],
        's5_b_re': un_tr(d_bre)[None], 's5_b_im': un_tr(d_bim)[None],
        's5_c_re': d_cre[None], 's5_c_im': d_cim[None],
        's5_d': d_d, 's5_b_glu': d_bg,
        'gm_w_s': d_gmw[None], 'gm_b_s': d_gmb.reshape(1, GM_HEADS, CHUNK), 'gm_v_g': d_gmg,
        'ffn_conv_b': jnp.stack([gf0['cb'], gf1['cb']]),
    }
    return loss, grad_x, repl


HBM_SPEC = pl.BlockSpec(memory_space=pltpu.HBM)


def _at_axis(ref, pos, index):
    return ref.at[(slice(None),) * pos + (index,)]


def _all_gather(shards, positions, name):
    n = len(shards)

    def body(*refs):
        xs, outs = refs[:n], refs[n:2 * n]
        send_sems, recv_sems, local_sems = refs[2 * n:]
        x, y, c = lax.axis_index("x"), lax.axis_index("y"), lax.axis_index("c")
        me, sibling = (x, y, c), (x, y, 1 - c)
        chips = [(1 - x, y), (x, 1 - y), (1 - x, 1 - y)]

        def block(p, dev):
            return _at_axis(outs[p], positions[p], 4 * dev[0] + 2 * dev[1] + dev[2])

        def copy(p, k, dev, to, src=None):
            return pltpu.make_async_remote_copy(
                src_ref=block(p, dev) if src is None else src, dst_ref=block(p, dev),
                send_sem=send_sems.at[p, k], recv_sem=recv_sems.at[p, k], device_id=to, device_id_type=MESH_T)

        mine = [pltpu.make_async_copy(xs[p], block(p, me), local_sems.at[p]) for p in range(n)]
        for cp in mine:
            cp.start()
        first = [copy(p, 0, me, sibling, src=xs[p]) for p in range(n)]
        first += [copy(p, 1 + j, me, (*chip, c), src=xs[p]) for j, chip in enumerate(chips) for p in range(n)]
        for cp in first:
            cp.start()
        passed = []
        for j, chip in enumerate(chips):
            for p in range(n):
                copy(p, 1 + j, (*chip, c), me).wait_recv()
                fwd = copy(p, 4 + j, (*chip, c), sibling)
                fwd.start()
                passed.append(fwd)
        for p in range(n):
            copy(p, 0, sibling, me).wait_recv()
        for j, chip in enumerate(chips):
            for p in range(n):
                copy(p, 4 + j, (*chip, 1 - c), me).wait_recv()
        for cp in first + passed:
            cp.wait_send()
        for cp in mine:
            cp.wait()

    out_shape = [jax.ShapeDtypeStruct(s.shape[:pos] + (NDEV,) + s.shape[pos:], s.dtype)
                 for s, pos in zip(shards, positions)]
    return pl.pallas_call(
        body, name=name, out_shape=out_shape, in_specs=[HBM_SPEC] * n, out_specs=[HBM_SPEC] * n,
        scratch_shapes=[pltpu.SemaphoreType.DMA((n, 7)), pltpu.SemaphoreType.DMA((n, 7)),
                        pltpu.SemaphoreType.DMA((n,))])(*shards)


def _other_devices(x, y, c):
    flip = lambda v, bit: 1 - v if bit else v
    return [(flip(x, k >> 2 & 1), flip(y, k >> 1 & 1), flip(c, k & 1)) for k in range(1, NDEV)]


SEM_SPEC = pl.BlockSpec(memory_space=pltpu.SEMAPHORE)
START_EFFECT = pltpu.SideEffectType.DATAFLOW_SIDE_EFFECTING


def _send_start(arrays, scatter, name):
    n = len(arrays)
    lands = [lax.empty((NDEV,) + (a.shape[1:] if scatter else a.shape), a.dtype) for a in arrays]

    def body(*refs):
        xs, ls = refs[:n], refs[n:2 * n]
        send_sems, recv_sems, own_sems, token = refs[2 * n], refs[2 * n + 1], refs[2 * n + 2], refs[4 * n + 3]
        x, y, c = lax.axis_index("x"), lax.axis_index("y"), lax.axis_index("c")
        me = 4 * x + 2 * y + c
        for k, peer in enumerate(_other_devices(x, y, c)):
            for p in range(n):
                src = xs[p].at[4 * peer[0] + 2 * peer[1] + peer[2]] if scatter else xs[p]
                pltpu.make_async_remote_copy(
                    src_ref=src, dst_ref=ls[p].at[me], send_sem=send_sems.at[p * (NDEV - 1) + k],
                    recv_sem=recv_sems.at[p * (NDEV - 1) + k], device_id=peer, device_id_type=MESH_T).start()
        for p in range(n):
            pltpu.make_async_copy(xs[p].at[me] if scatter else xs[p], ls[p].at[me], own_sems.at[p]).start()
        token[...] = jnp.zeros(token.shape, F32)

    sems = pltpu.SemaphoreType.DMA((n * (NDEV - 1),))
    out_shape = ([sems, sems, pltpu.SemaphoreType.DMA((n,))]
                 + [pltpu.HBM(a.shape, a.dtype) for a in list(arrays) + lands] + [jax.ShapeDtypeStruct((8, 128), F32)])
    res = pl.pallas_call(
        body, name=name, out_shape=out_shape, in_specs=[HBM_SPEC] * (2 * n),
        out_specs=[SEM_SPEC] * 3 + [HBM_SPEC] * (2 * n) + [pl.BlockSpec(memory_space=pltpu.VMEM)],
        input_output_aliases={i: 3 + i for i in range(2 * n)},
        compiler_params=pltpu.CompilerParams(has_side_effects=START_EFFECT))(
            *[pltpu.with_memory_space_constraint(a, pltpu.HBM) for a in list(arrays) + lands])
    return res[:3], res[3:3 + n], res[3 + n:3 + 2 * n], res[3 + 2 * n]


def _send_wait(started, scatter, after, name):
    sems, arrays, lands, _ = started
    n = len(arrays)

    def body(*refs):
        xs, ls = refs[:n], refs[n:2 * n]
        send, recv, own = refs[2 * n:2 * n + 3]
        x, y, c = lax.axis_index("x"), lax.axis_index("y"), lax.axis_index("c")
        me = 4 * x + 2 * y + c
        for p in range(n):
            pltpu.make_async_copy(xs[p].at[me] if scatter else xs[p], ls[p].at[me], own.at[p]).wait()
        for k, peer in enumerate(_other_devices(x, y, c)):
            slot = 4 * peer[0] + 2 * peer[1] + peer[2]
            for p in range(n):
                cp = pltpu.make_async_remote_copy(
                    src_ref=xs[p].at[slot] if scatter else xs[p], dst_ref=ls[p].at[slot],
                    send_sem=send.at[p * (NDEV - 1) + k], recv_sem=recv.at[p * (NDEV - 1) + k], device_id=peer,
                    device_id_type=MESH_T)
                cp.wait_send()
                cp.wait_recv()

    res = pl.pallas_call(
        body, name=name, out_shape=[pltpu.HBM(a.shape, a.dtype) for a in list(arrays) + list(lands)],
        in_specs=[HBM_SPEC] * (2 * n) + [SEM_SPEC] * 3 + [pl.BlockSpec(memory_space=pl.ANY)],
        out_specs=[HBM_SPEC] * (2 * n), input_output_aliases={i: i for i in range(2 * n)},
        compiler_params=pltpu.CompilerParams(has_side_effects=START_EFFECT))(
            *arrays, *lands, *sems, after)
    return res[n:]


def _row_block(rows, cols, itemsize=4, target=2**20):
    best = None
    for tr in range(16, rows + 1, 16):
        if rows % tr == 0 and tr * cols * itemsize <= target:
            best = tr
    return best or rows


def _adamw(w, m, v, gparts, name):
    parts, rows, cols = gparts.shape
    tr = _row_block(rows, cols, target=2**19)
    bc1 = 1.0 - ADAM_B1 ** ADAM_STEP
    bc2 = 1.0 - ADAM_B2 ** ADAM_STEP

    def body(w_ref, m_ref, v_ref, g_ref, go_ref, d_ref, mo_ref, vo_ref):
        g = g_ref[0].astype(F32)
        for k in range(1, parts):
            g = g + g_ref[k].astype(F32)
        mn = ADAM_B1 * m_ref[...] + (1.0 - ADAM_B1) * g
        vn = ADAM_B2 * v_ref[...] + (1.0 - ADAM_B2) * (g * g)
        go_ref[...] = g
        mo_ref[...] = mn
        vo_ref[...] = vn
        d_ref[...] = -ADAM_LR * ((mn / bc1) / (jnp.sqrt(vn / bc2) + ADAM_EPS) + ADAM_WD * w_ref[...])

    blk = pl.BlockSpec((tr, cols), lambda i: (i, 0))
    shp = jax.ShapeDtypeStruct((rows, cols), F32)
    return pl.pallas_call(
        body, name=name, grid=(rows // tr,),
        in_specs=[blk, blk, blk, pl.BlockSpec((parts, tr, cols), lambda i: (0, i, 0))],
        out_specs=[blk] * 4, out_shape=[shp] * 4, compiler_params=_cp(("parallel",)))(w, m, v, gparts)


def _pack(arrays, rows):
    flat = jnp.concatenate([a.reshape(-1).astype(F32) for a in arrays])
    return jnp.pad(flat, (0, rows * PACK_COLS - flat.shape[0])).reshape(rows, PACK_COLS)


def _unpack(buf, shapes):
    flat = buf.reshape(-1)
    out, off = [], 0
    for shp in shapes:
        size = int(np.prod(shp))
        out.append(flat[off:off + size].reshape(shp))
        off += size
    return out


REPL_SHAPES = {'mix_norm_g': (2, 1024), 'ffn_norm_g': (2, 1024), 'final_norm_g': (1024,), 's5_lam_re': (1, 16, 64),
               's5_lam_im': (1, 16, 64), 's5_log_dt': (1, 16), 's5_b_re': (1, 16, 64, 16), 's5_b_im': (1, 16, 64, 16),
               's5_c_re': (1, 16, 16, 64), 's5_c_im': (1, 16, 16, 64), 's5_d': (1, 256), 's5_b_glu': (1, 256),
               'gm_w_s': (1, 6, 128, 128), 'gm_b_s': (1, 6, 128), 'gm_v_g': (1, 768), 'ffn_conv_b': (2, 5632)}
REPL_ELEMS = sum(int(np.prod(REPL_SHAPES[n])) for n in REPL_ORDER)
REPL_ROWS = -(-REPL_ELEMS // (PACK_COLS * 8)) * 8

GATHER_DTYPE = {'ev_w_in': BF, 'ev_w_out': BF, 's5_w_glu': BF, 'od_w_in': BF, 'od_conv_w': F32, 'od_conv_b': F32,
                'od_w_out': BF, 'ffn_w_up': BF, 'ffn_conv_w': F32, 'ffn_w_down': BF}
GATHER_EVEN = ['ev_w_in', 'ev_w_out', 's5_w_glu']
GATHER_FFN0 = ['ffn_w_up0', 'ffn_conv_w0', 'ffn_w_down0']
GATHER_REST = ['od_w_in', 'od_conv_w', 'od_conv_b', 'od_w_out', 'ffn_w_up1', 'ffn_conv_w1', 'ffn_w_down1']

def _squeeze_lead(a):
    return a.reshape(a.shape[1:]) if a.shape[0] == 1 and a.ndim > 2 else a


def kernel(x, mix_norm_g, ffn_norm_g, final_norm_g, ev_w_in, ev_w_out, s5_lam_re, s5_lam_im, s5_log_dt, s5_b_re, s5_b_im, s5_c_re, s5_c_im, s5_d, s5_w_glu, s5_b_glu, gm_w_s, gm_b_s, gm_v_g, od_w_in, od_conv_w, od_conv_b, od_w_out, ffn_w_up, ffn_conv_w, ffn_conv_b, ffn_w_down, loss_target, m_mix_norm_g, m_ffn_norm_g, m_final_norm_g, m_ev_w_in, m_ev_w_out, m_s5_lam_re, m_s5_lam_im, m_s5_log_dt, m_s5_b_re, m_s5_b_im, m_s5_c_re, m_s5_c_im, m_s5_d, m_s5_w_glu, m_s5_b_glu, m_gm_w_s, m_gm_b_s, m_gm_v_g, m_od_w_in, m_od_conv_w, m_od_conv_b, m_od_w_out, m_ffn_w_up, m_ffn_conv_w, m_ffn_conv_b, m_ffn_w_down, v_mix_norm_g, v_ffn_norm_g, v_final_norm_g, v_ev_w_in, v_ev_w_out, v_s5_lam_re, v_s5_lam_im, v_s5_log_dt, v_s5_b_re, v_s5_b_im, v_s5_c_re, v_s5_c_im, v_s5_d, v_s5_w_glu, v_s5_b_glu, v_gm_w_s, v_gm_b_s, v_gm_v_g, v_od_w_in, v_od_conv_w, v_od_conv_b, v_od_w_out, v_ffn_w_up, v_ffn_conv_w, v_ffn_conv_b, v_ffn_w_down):
    given = dict(locals())
    weights = {n: given[n] for n in WEIGHT_ORDER}
    nseq, seq, _ = x.shape

    send = {}
    for name in SHARDED_ORDER:
        a = weights[name].astype(GATHER_DTYPE[name])
        if a.shape[0] == 2:
            send[name + '0'], send[name + '1'] = a[0], a[1]
        else:
            send[name] = _squeeze_lead(a)
    gathers = [_send_start([send[n] for n in names], False, f"gather_{tag}_start")
               for tag, names in (("ffn0", GATHER_FFN0), ("rest", GATHER_REST))]
    token = gathers[0][3] + gathers[1][3]

    def waiter(tag, names, started):
        return lambda after: dict(zip(names, _send_wait(started, False, after, f"gather_{tag}_wait")))

    gathered = dict(zip(GATHER_EVEN, _all_gather([send[n] for n in GATHER_EVEN], [0] * len(GATHER_EVEN),
                                                 "gather_even")))

    scatters = []

    def scatter(tag, grads):
        names = list(grads)
        started = _send_start([grads[n].astype(BF) for n in names], True, f"{tag}_start")
        scatters.append((tag, names, started))
        return started[3]

    loss_row, grad_x, g_repl = _local_step(
        x.reshape(nseq * seq, D), loss_target.reshape(nseq * seq, D), weights, gathered,
        waiter("ffn0", GATHER_FFN0, gathers[0]), waiter("rest", GATHER_REST, gathers[1]), token, scatter, seq)
    loss = lax.psum(loss_row[0, 0], ("x", "y", "c"))

    parts = {}
    for tag, names, started in scatters:
        parts.update(zip(names, _send_wait(started, True, grad_x, f"{tag}_wait")))
    repl_parts = _all_gather([_pack([g_repl[n] for n in REPL_ORDER], REPL_ROWS)], [0], "gather_small_grads")[0]

    out = {}
    for name in SHARDED_ORDER:
        w = weights[name]
        if name + '0' in parts:
            gp = jnp.stack([parts[name + '0'], parts[name + '1']], axis=1)
        else:
            gp = parts[name]
        to_rows = lambda a: a.reshape(-1, w.shape[-1])
        res = _adamw(to_rows(w), to_rows(given["m_" + name]), to_rows(given["v_" + name]),
                     gp.reshape(NDEV, -1, w.shape[-1]), f"adamw_{name}")
        out[name] = [r.reshape(w.shape) for r in res]
    rp = _adamw(_pack([weights[n] for n in REPL_ORDER], REPL_ROWS),
                _pack([given["m_" + n] for n in REPL_ORDER], REPL_ROWS),
                _pack([given["v_" + n] for n in REPL_ORDER], REPL_ROWS), repl_parts, "adamw_replicated")
    rp_shapes = [weights[n].shape for n in REPL_ORDER]
    for k in range(4):
        for name, a in zip(REPL_ORDER, _unpack(rp[k], rp_shapes)):
            out.setdefault(name, [None] * 4)[k] = a
    results = [[out[n][k] for n in WEIGHT_ORDER] for k in range(4)]
    grad_w, delta_w, new_m, new_v = results
    return (loss, grad_x.reshape(nseq, seq, D), *grad_w, *delta_w, *new_m, *new_v)
```

```python
import math

import jax
import jax.numpy as jnp
import numpy as np
from jax import lax
from jax.experimental import pallas as pl
from jax.experimental.pallas import tpu as pltpu

F32 = jnp.float32
BF = jnp.bfloat16

D = 1024
DFF = 2816
NDEV = 8
SSM_W = 256
SSM_G = 16
SSM_H = 16
SSM_P = 64
NST = SSM_G * SSM_P
GM_W = 768
GM_HEADS = 6
CHUNK = 128
EPS = 1e-6
LAM_MAX = -1e-4
FB = 256
FSH = 2 * DFF // NDEV
VMEM_LIMIT = 48 * 2**20
PACK_COLS = 1024
MESH_T = pl.DeviceIdType.MESH

ADAM_LR = 0.001
ADAM_B1 = 0.9
ADAM_B2 = 0.999
ADAM_EPS = 1e-08
ADAM_WD = 0.01
ADAM_STEP = 10

WEIGHT_ORDER = ['mix_norm_g', 'ffn_norm_g', 'final_norm_g', 'ev_w_in', 'ev_w_out', 's5_lam_re', 's5_lam_im',
                's5_log_dt', 's5_b_re', 's5_b_im', 's5_c_re', 's5_c_im', 's5_d', 's5_w_glu', 's5_b_glu', 'gm_w_s',
                'gm_b_s', 'gm_v_g', 'od_w_in', 'od_conv_w', 'od_conv_b', 'od_w_out', 'ffn_w_up', 'ffn_conv_w',
                'ffn_conv_b', 'ffn_w_down']
SHARDED = {'ev_w_in': ((1, 1024, 1792), 2), 'ev_w_out': ((1, 1024, 1024), 1), 's5_w_glu': ((1, 256, 256), 1),
           'od_w_in': ((1, 1024, 3072), 2), 'od_conv_w': ((1, 3, 1024), 2), 'od_conv_b': ((1, 1024), 1),
           'od_w_out': ((1, 1024, 1024), 1), 'ffn_w_up': ((2, 1024, 5632), 2), 'ffn_conv_w': ((2, 3, 5632), 2),
           'ffn_w_down': ((2, 2816, 1024), 1)}
SHARDED_ORDER = [n for n in WEIGHT_ORDER if n in SHARDED]
REPL_ORDER = [n for n in WEIGHT_ORDER if n not in SHARDED]


def _cp(sem):
    return pltpu.CompilerParams(dimension_semantics=sem, vmem_limit_bytes=VMEM_LIMIT)


def _sigmoid(x):
    return 1.0 / (1.0 + jnp.exp(-x))


_GELU_K = math.sqrt(2.0 / math.pi)


def _gelu(x):
    return 0.5 * x * (1.0 + jnp.tanh(_GELU_K * (x + 0.044715 * x * x * x)))


def _gelu_grad(x):
    t = jnp.tanh(_GELU_K * (x + 0.044715 * x * x * x))
    return 0.5 * (1.0 + t) + 0.5 * x * (1.0 - t * t) * _GELU_K * (1.0 + 3.0 * 0.044715 * x * x)


def _colsum(x):
    return jnp.sum(x, axis=0, keepdims=True)


def _accumulate(ref, first, part):
    @pl.when(first)
    def _():
        ref[...] = part

    @pl.when(jnp.logical_not(first))
    def _():
        ref[...] += part


_DIMS = {'nn': (((1,), (0,)), ((), ())), 'nt': (((1,), (1,)), ((), ())), 'tn': (((0,), (0,)), ((), ()))}


def _matmul(a, b, mode, tm, tn, tk, name, resid=None, out_dtype=F32):
    if mode == 'tn':
        kdim, m = a.shape
    else:
        m, kdim = a.shape
    n = b.shape[0] if mode == 'nt' else b.shape[1]
    tm, tn, tk = min(tm, m), min(tn, n), min(tk, kdim)
    assert m % tm == 0 and n % tn == 0 and kdim % tk == 0, (name, m, n, kdim, tm, tn, tk)
    a_spec = (pl.BlockSpec((tk, tm), lambda i, j, k: (k, i)) if mode == 'tn'
              else pl.BlockSpec((tm, tk), lambda i, j, k: (i, k)))
    b_spec = (pl.BlockSpec((tn, tk), lambda i, j, k: (j, k)) if mode == 'nt'
              else pl.BlockSpec((tk, tn), lambda i, j, k: (k, j)))
    o_spec = pl.BlockSpec((tm, tn), lambda i, j, k: (i, j))
    return _matmul_spec(a, b, mode, (m // tm, n // tn, kdim // tk), a_spec, b_spec, o_spec, (m, n), name,
                        resid=resid, out_dtype=out_dtype)


def _matmul_spec(a, b, mode, grid, a_spec, b_spec, o_spec, out_shape, name, resid=None, out_dtype=F32):
    nk = grid[2]
    tm, tn = o_spec.block_shape[-2:]
    dims = _DIMS[mode]
    has_resid = resid is not None

    def body(*refs):
        if has_resid:
            a_ref, b_ref, r_ref, o_ref = refs[:4]
        else:
            a_ref, b_ref, o_ref = refs[:3]
            r_ref = None
        part = lax.dot_general(a_ref[...].astype(BF), b_ref[...].astype(BF), dims, preferred_element_type=F32)
        if nk == 1:
            if has_resid:
                part = part + r_ref[...]
            o_ref[...] = part.astype(out_dtype)
        else:
            acc = refs[-1]
            k = pl.program_id(2)

            @pl.when(k == 0)
            def _():
                acc[...] = part

            @pl.when(k > 0)
            def _():
                acc[...] += part

            @pl.when(k == nk - 1)
            def _():
                tot = acc[...]
                if has_resid:
                    tot = tot + r_ref[...]
                o_ref[...] = tot.astype(out_dtype)

    operands = [a, b] + ([resid] if has_resid else [])
    in_specs = [a_spec, b_spec] + ([o_spec] if has_resid else [])
    return pl.pallas_call(
        body, name=name, grid=grid, in_specs=in_specs, out_specs=o_spec,
        out_shape=jax.ShapeDtypeStruct(out_shape, out_dtype),
        scratch_shapes=[pltpu.VMEM((tm, tn), F32)] if nk > 1 else [],
        compiler_params=_cp(("parallel", "parallel", "arbitrary")))(*operands)


def _rmsnorm_fwd(x, g, name):
    n = x.shape[0]
    tm = min(512, n)

    def body(x_ref, g_ref, o_ref):
        xv = x_ref[...]
        r = lax.rsqrt(jnp.mean(xv * xv, axis=-1, keepdims=True) + EPS)
        o_ref[...] = (xv * r * g_ref[...]).astype(BF)

    return pl.pallas_call(
        body, name=name, grid=(n // tm,),
        in_specs=[pl.BlockSpec((tm, D), lambda i: (i, 0)), pl.BlockSpec((1, D), lambda i: (0, 0))],
        out_specs=pl.BlockSpec((tm, D), lambda i: (i, 0)),
        out_shape=jax.ShapeDtypeStruct((n, D), BF), compiler_params=_cp(("parallel",)))(x, g)


def _rmsnorm_bwd(x, g, dy, dres, name):
    n = x.shape[0]
    tm = min(512, n)

    def body(x_ref, g_ref, dy_ref, dr_ref, dx_ref, dg_ref):
        xv = x_ref[...]
        r = lax.rsqrt(jnp.mean(xv * xv, axis=-1, keepdims=True) + EPS)
        xh = xv * r
        dyv = dy_ref[...]
        dyg = dyv * g_ref[...]
        dx_ref[...] = dr_ref[...] + r * (dyg - xh * jnp.mean(dyg * xh, axis=-1, keepdims=True))
        _accumulate(dg_ref, pl.program_id(0) == 0, _colsum(dyv * xh))

    row = pl.BlockSpec((tm, D), lambda i: (i, 0))
    vec = pl.BlockSpec((1, D), lambda i: (0, 0))
    return pl.pallas_call(
        body, name=name, grid=(n // tm,), in_specs=[row, vec, row, row], out_specs=[row, vec],
        out_shape=[jax.ShapeDtypeStruct((n, D), F32), jax.ShapeDtypeStruct((1, D), F32)],
        compiler_params=_cp(("arbitrary",)))(x, g, dy, dres)


def _final_loss(h, g, tgt, name):
    n = h.shape[0]
    tm = min(512, n)

    def body(x_ref, g_ref, t_ref, loss_ref, dx_ref, dg_ref):
        first = pl.program_id(0) == 0
        xv = x_ref[...]
        gv = g_ref[...]
        r = lax.rsqrt(jnp.mean(xv * xv, axis=-1, keepdims=True) + EPS)
        xh = xv * r
        err = xh * gv - t_ref[...]
        part = 0.5 * jnp.sum(jnp.mean(err * err, axis=-1, keepdims=True), axis=0, keepdims=True)
        _accumulate(loss_ref, first, jnp.broadcast_to(part, (1, 128)))
        dyv = err * (1.0 / D)
        dyg = dyv * gv
        dx_ref[...] = r * (dyg - xh * jnp.mean(dyg * xh, axis=-1, keepdims=True))
        _accumulate(dg_ref, first, _colsum(dyv * xh))

    row = pl.BlockSpec((tm, D), lambda i: (i, 0))
    vec = pl.BlockSpec((1, D), lambda i: (0, 0))
    return pl.pallas_call(
        body, name=name, grid=(n // tm,), in_specs=[row, vec, row],
        out_specs=[pl.BlockSpec((1, 128), lambda i: (0, 0)), row, vec],
        out_shape=[jax.ShapeDtypeStruct((1, 128), F32), jax.ShapeDtypeStruct((n, D), F32),
                   jax.ShapeDtypeStruct((1, D), F32)],
        compiler_params=_cp(("arbitrary",)))(h, g, tgt)


def _prev_rows(x, halo_ref, lanes, scale, row):
    h7 = halo_ref[7:8, lanes] * scale
    h6 = halo_ref[6:7, lanes] * scale
    p1 = jnp.where(row == 0, h7, pltpu.roll(x, 1, 0))
    p2 = jnp.where(row == 0, h6, jnp.where(row == 1, h7, pltpu.roll(x, 2, 0)))
    return p1, p2


def _halo_maps(tm, n_rows):
    r8 = tm // 8
    last = n_rows // 8 - 1
    prev = lambda i: jnp.maximum(i * r8 - 1, 0)
    nxt = lambda i: jnp.minimum((i + 1) * r8, last)
    return prev, nxt


def _lane_blocks(width):
    return [slice(lo, min(lo + 128, width)) for lo in range(0, width, 128)]


def _conv_taps(w_ref, b_ref, g, lanes):
    return w_ref[g, 0:1, lanes], w_ref[g, 1:2, lanes], w_ref[g, 2:3, lanes], b_ref[g, :, lanes]


def _conv_tile(x, prev1, prev2, taps, row):
    w0, w1, w2, b = taps
    r1 = pltpu.roll(x, 1, 0)
    r2 = pltpu.roll(x, 2, 0)
    x1 = jnp.where(row == 0, prev1, r1)
    x2 = jnp.where(row < 2, prev2, r2)
    return b + w0 * x2 + w1 * x1 + w2 * x, x1, x2, r1, r2


def _halo16_maps(tm, n_rows):
    r16 = tm // 16
    last = n_rows // 16 - 1
    return (lambda i: jnp.maximum(i * r16 - 1, 0)), (lambda i: jnp.minimum((i + 1) * r16, last))


def _ffn_conv_fwd(up, cw, cb, seq, name):
    n = up.shape[2]
    tm = min(256, seq)
    prev, _ = _halo16_maps(tm, n)

    def body(u_ref, h_ref, w_ref, b_ref, o_ref):
        i = pl.program_id(1)
        scale = jnp.where(lax.rem(i * tm, seq) == 0, 0.0, 1.0)
        for lanes in _lane_blocks(FSH):
            lw = lanes.stop - lanes.start
            row = lax.broadcasted_iota(jnp.int32, (8, lw), 0)
            taps = [_conv_taps(w_ref, b_ref, g, lanes) for g in range(2)]

            def tile(xs, carry):
                hc, nxt = [], []
                for g in range(2):
                    conv, _, _, r1, r2 = _conv_tile(xs[g], carry[2 * g], carry[2 * g + 1], taps[g], row)
                    hc.append(conv)
                    nxt += [r1, r2]
                return hc[0] * _sigmoid(hc[0]) * hc[1], tuple(nxt)

            carry = []
            for g in range(2):
                halo = h_ref[g, :, lanes].astype(F32)[8:] * scale
                carry += [pltpu.roll(halo, 1, 0), pltpu.roll(halo, 2, 0)]
            carry = tuple(carry)
            for m in range(tm // 16):
                x16 = [u_ref[g, m * 16:m * 16 + 16, lanes].astype(F32) for g in range(2)]
                a, carry = tile([x[:8] for x in x16], carry)
                b, carry = tile([x[8:] for x in x16], carry)
                o_ref[m * 16:m * 16 + 16, lanes] = jnp.concatenate([a, b], axis=0).astype(BF)

    return pl.pallas_call(
        body, name=name, grid=(4, n // tm),
        in_specs=[pl.BlockSpec((2, None, tm, FSH), lambda j, i: (0, j, i, 0)),
                  pl.BlockSpec((2, None, 16, FSH), lambda j, i: (0, j, prev(i), 0)),
                  pl.BlockSpec((2, None, 3, FSH), lambda j, i: (0, j, 0, 0)),
                  pl.BlockSpec((2, None, 1, FSH), lambda j, i: (0, j, 0, 0))],
        out_specs=pl.BlockSpec((None, tm, FSH), lambda j, i: (j, i, 0)),
        out_shape=jax.ShapeDtypeStruct((4, n, FSH), BF), compiler_params=_cp(("parallel", "parallel")))(up, up, cw, cb)


def _ffn_conv_bwd(up, dact, cw, cb, seq, name):
    n = up.shape[2]
    tm = min(256, seq)
    prev, nxt = _halo16_maps(tm, n)

    def body(u_ref, up_ref, un_ref, da_ref, dn_ref, w_ref, b_ref, du_ref, dw_ref, db_ref):
        i = pl.program_id(1)
        sp = jnp.where(lax.rem(i * tm, seq) == 0, 0.0, 1.0)
        sn = jnp.where(lax.rem((i + 1) * tm, seq) == 0, 0.0, 1.0)
        first = i == 0
        for lanes in _lane_blocks(FSH):
            lw = lanes.stop - lanes.start
            row = lax.broadcasted_iota(jnp.int32, (8, lw), 0)
            taps = [_conv_taps(w_ref, b_ref, g, lanes) for g in range(2)]

            def grads(xs, da, xroll, sums, accumulate):
                hc, shifted, rolls = [], [], []
                for g in range(2):
                    conv, x1, x2, r1, r2 = _conv_tile(xs[g], xroll[g][0], xroll[g][1], taps[g], row)
                    hc.append(conv)
                    shifted.append((x2, x1, xs[g]))
                    rolls.append((r1, r2))
                s = _sigmoid(hc[0])
                dhc = (da * hc[1] * (s * (1.0 + hc[0] * (1.0 - s))), da * (hc[0] * s))
                if accumulate:
                    sums = [tuple(acc + dhc[g] * xk for acc, xk in zip(sums[g][:3], shifted[g])) + (sums[g][3] + dhc[g],)
                            for g in range(2)]
                return [(dhc[g], pltpu.roll(dhc[g], 7, 0), pltpu.roll(dhc[g], 6, 0)) for g in range(2)], rolls, sums

            def dup_tile(g, cur, after):
                w0, w1, w2, _ = taps[g]
                return (w2 * cur[0] + w1 * jnp.where(row == 7, after[1], cur[1])
                        + w0 * jnp.where(row >= 6, after[2], cur[2]))

            def emit(m, pair_, after):
                for g in range(2):
                    du_ref[g, m * 16:m * 16 + 16, lanes] = jnp.concatenate(
                        [dup_tile(g, pair_[0][g], pair_[1][g]), dup_tile(g, pair_[1][g], after[g])],
                        axis=0).astype(BF)

            zero = jnp.zeros((8, lw), F32)
            xroll = []
            for g in range(2):
                halo = up_ref[g, :, lanes].astype(F32)[8:] * sp
                xroll.append((pltpu.roll(halo, 1, 0), pltpu.roll(halo, 2, 0)))
            sums = [(zero,) * 4, (zero,) * 4]
            held = None
            for m in range(tm // 16):
                x16 = [u_ref[g, m * 16:m * 16 + 16, lanes].astype(F32) for g in range(2)]
                d16 = da_ref[m * 16:m * 16 + 16, lanes].astype(F32)
                ta, xroll, sums = grads([x[:8] for x in x16], d16[:8], xroll, sums, True)
                tb, xroll, sums = grads([x[8:] for x in x16], d16[8:], xroll, sums, True)
                if held is not None:
                    emit(m - 1, held, ta)
                held = (ta, tb)
            xn = [un_ref[g, :, lanes].astype(F32)[:8] for g in range(2)]
            tn_, _, _ = grads(xn, dn_ref[:, lanes].astype(F32)[:8] * sn, xroll, sums, False)
            emit(tm // 16 - 1, held, tn_)
            for g in range(2):
                for k in range(3):
                    _accumulate(dw_ref.at[g, k:k + 1, lanes], first, _colsum(sums[g][k]))
                _accumulate(db_ref.at[g, :, lanes], first, _colsum(sums[g][3]))

    return pl.pallas_call(
        body, name=name, grid=(4, n // tm),
        in_specs=[pl.BlockSpec((2, None, tm, FSH), lambda j, i: (0, j, i, 0)),
                  pl.BlockSpec((2, None, 16, FSH), lambda j, i: (0, j, prev(i), 0)),
                  pl.BlockSpec((2, None, 16, FSH), lambda j, i: (0, j, nxt(i), 0)),
                  pl.BlockSpec((None, tm, FSH), lambda j, i: (j, i, 0)),
                  pl.BlockSpec((None, 16, FSH), lambda j, i: (j, nxt(i), 0)),
                  pl.BlockSpec((2, None, 3, FSH), lambda j, i: (0, j, 0, 0)),
                  pl.BlockSpec((2, None, 1, FSH), lambda j, i: (0, j, 0, 0))],
        out_specs=[pl.BlockSpec((2, None, tm, FSH), lambda j, i: (0, j, i, 0)),
                   pl.BlockSpec((2, None, 3, FSH), lambda j, i: (0, j, 0, 0)),
                   pl.BlockSpec((2, None, 1, FSH), lambda j, i: (0, j, 0, 0))],
        out_shape=[jax.ShapeDtypeStruct((2, 4, n, FSH), BF), jax.ShapeDtypeStruct((2, 4, 3, FSH), F32),
                   jax.ShapeDtypeStruct((2, 4, 1, FSH), F32)],
        compiler_params=_cp(("parallel", "arbitrary")))(up, up, up, dact, dact, cw, cb)


def _shortconv_fwd(p, cw, cb, seq, name):
    n = p.shape[0]
    tm = min(256, seq)
    prev, _ = _halo_maps(tm, n)

    def body(p_ref, h_ref, w_ref, b_ref, o_ref):
        i = pl.program_id(1)
        scale = jnp.where(lax.rem(i * tm, seq) == 0, 0.0, 1.0)
        q = p_ref[:, FB:2 * FB] * p_ref[:, 2 * FB:]
        row = lax.broadcasted_iota(jnp.int32, q.shape, 0)
        h7 = h_ref[7:8, FB:2 * FB] * h_ref[7:8, 2 * FB:] * scale
        h6 = h_ref[6:7, FB:2 * FB] * h_ref[6:7, 2 * FB:] * scale
        p1 = jnp.where(row == 0, h7, pltpu.roll(q, 1, 0))
        p2 = jnp.where(row == 0, h6, jnp.where(row == 1, h7, pltpu.roll(q, 2, 0)))
        conv = b_ref[...] + w_ref[0:1, :] * p2 + w_ref[1:2, :] * p1 + w_ref[2:3, :] * q
        o_ref[...] = (p_ref[:, :FB] * conv).astype(BF)

    return pl.pallas_call(
        body, name=name, grid=(D // FB, n // tm),
        in_specs=[pl.BlockSpec((tm, 3 * FB), lambda j, i: (i, j)),
                  pl.BlockSpec((8, 3 * FB), lambda j, i: (prev(i), j)),
                  pl.BlockSpec((3, FB), lambda j, i: (0, j)),
                  pl.BlockSpec((1, FB), lambda j, i: (0, j))],
        out_specs=pl.BlockSpec((tm, FB), lambda j, i: (i, j)),
        out_shape=jax.ShapeDtypeStruct((n, D), BF), compiler_params=_cp(("parallel", "parallel")))(p, p, cw, cb)


def _shortconv_bwd(p, dmix, cw, cb, seq, name):
    n = p.shape[0]
    tm = min(256, seq)
    ext = tm + 16
    prev, nxt = _halo_maps(tm, n)

    def body(p_ref, pp_ref, pn_ref, dm_ref, dn_ref, w_ref, b_ref, dp_ref, dw_ref, db_ref, qx, cx):
        i = pl.program_id(1)
        sp = jnp.where(lax.rem(i * tm, seq) == 0, 0.0, 1.0)
        sn = jnp.where(lax.rem((i + 1) * tm, seq) == 0, 0.0, 1.0)
        bg, cg, hx = p_ref[:, :FB], p_ref[:, FB:2 * FB], p_ref[:, 2 * FB:]
        dm = dm_ref[...]
        qx[0:8, :] = pp_ref[:, FB:2 * FB] * pp_ref[:, 2 * FB:] * sp
        qx[8:8 + tm, :] = cg * hx
        qx[8 + tm:, :] = jnp.zeros((8, FB), F32)
        cx[0:8, :] = jnp.zeros((8, FB), F32)
        cx[8:8 + tm, :] = dm * bg
        cx[8 + tm:, :] = dn_ref[...] * pn_ref[:, :FB] * sn
        q0 = qx[...]
        q1 = pltpu.roll(q0, 1, 0)
        q2 = pltpu.roll(q0, 2, 0)
        main = slice(8, 8 + tm)
        conv = b_ref[...] + w_ref[0:1, :] * q2[main] + w_ref[1:2, :] * q1[main] + w_ref[2:3, :] * q0[main]
        dc = cx[...]
        dq = (w_ref[2:3, :] * dc + w_ref[1:2, :] * pltpu.roll(dc, ext - 1, 0)
              + w_ref[0:1, :] * pltpu.roll(dc, ext - 2, 0))[main]
        dp_ref[:, :FB] = (dm * conv).astype(BF)
        dp_ref[:, FB:2 * FB] = (dq * hx).astype(BF)
        dp_ref[:, 2 * FB:] = (dq * cg).astype(BF)
        first = i == 0
        dcm = dc[main]
        _accumulate(dw_ref.at[0:1, :], first, _colsum(dcm * q2[main]))
        _accumulate(dw_ref.at[1:2, :], first, _colsum(dcm * q1[main]))
        _accumulate(dw_ref.at[2:3, :], first, _colsum(dcm * q0[main]))
        _accumulate(db_ref, first, _colsum(dcm))

    return pl.pallas_call(
        body, name=name, grid=(D // FB, n // tm),
        in_specs=[pl.BlockSpec((tm, 3 * FB), lambda j, i: (i, j)),
                  pl.BlockSpec((8, 3 * FB), lambda j, i: (prev(i), j)),
                  pl.BlockSpec((8, 3 * FB), lambda j, i: (nxt(i), j)),
                  pl.BlockSpec((tm, FB), lambda j, i: (i, j)),
                  pl.BlockSpec((8, FB), lambda j, i: (nxt(i), j)),
                  pl.BlockSpec((3, FB), lambda j, i: (0, j)),
                  pl.BlockSpec((1, FB), lambda j, i: (0, j))],
        out_specs=[pl.BlockSpec((tm, 3 * FB), lambda j, i: (i, j)),
                   pl.BlockSpec((3, FB), lambda j, i: (0, j)),
                   pl.BlockSpec((1, FB), lambda j, i: (0, j))],
        out_shape=[jax.ShapeDtypeStruct((n, 3 * D), BF), jax.ShapeDtypeStruct((3, D), F32),
                   jax.ShapeDtypeStruct((1, D), F32)],
        scratch_shapes=[pltpu.VMEM((ext, FB), F32), pltpu.VMEM((ext, FB), F32)],
        compiler_params=_cp(("parallel", "arbitrary")))(p, p, p, dmix, dmix, cw, cb)


def _gmlp_fwd(uv, wm, bst, gv, seq, name):
    n = uv.shape[0]
    tm = min(256, seq)

    def body(x_ref, w_ref, b_ref, g_ref, o_ref):
        ge_v = _gelu(x_ref[:, GM_W:])
        r = lax.rsqrt(jnp.mean(ge_v * ge_v, axis=-1, keepdims=True) + EPS)
        vn = (ge_v * r * g_ref[...]).astype(BF)
        for c in range(tm // CHUNK):
            rows = slice(c * CHUNK, (c + 1) * CHUNK)
            for h in range(GM_HEADS):
                cols = slice(h * CHUNK, (h + 1) * CHUNK)
                gate = jnp.dot(w_ref[h], vn[rows, cols], preferred_element_type=F32) + b_ref[:, h:h + 1]
                o_ref[rows, cols] = (_gelu(x_ref[rows, cols]) * gate).astype(BF)

    return pl.pallas_call(
        body, name=name, grid=(n // tm,),
        in_specs=[pl.BlockSpec((tm, 2 * GM_W), lambda i: (i, 0)),
                  pl.BlockSpec((GM_HEADS, CHUNK, CHUNK), lambda i: (0, 0, 0)),
                  pl.BlockSpec((CHUNK, GM_HEADS), lambda i: (0, 0)),
                  pl.BlockSpec((1, GM_W), lambda i: (0, 0))],
        out_specs=pl.BlockSpec((tm, GM_W), lambda i: (i, 0)),
        out_shape=jax.ShapeDtypeStruct((n, GM_W), BF), compiler_params=_cp(("parallel",)))(uv, wm, bst, gv)


def _gmlp_bwd(uv, dout, wm, wmt, bst, gv, seq, name):
    n = uv.shape[0]
    tm = min(256, seq)

    def body(x_ref, do_ref, w_ref, wt_ref, b_ref, g_ref, dx_ref, dw_ref, db_ref, dg_ref, dvn_scr):
        first = pl.program_id(0) == 0
        ge_v = _gelu(x_ref[:, GM_W:])
        r = lax.rsqrt(jnp.mean(ge_v * ge_v, axis=-1, keepdims=True) + EPS)
        vh = ge_v * r
        vn = (vh * g_ref[...]).astype(BF)
        tril = (lax.broadcasted_iota(jnp.int32, (CHUNK, CHUNK), 0)
                >= lax.broadcasted_iota(jnp.int32, (CHUNK, CHUNK), 1))
        for h in range(GM_HEADS):
            cols = slice(h * CHUNK, (h + 1) * CHUNK)
            dw = jnp.zeros((CHUNK, CHUNK), F32)
            dbs = jnp.zeros((CHUNK, 1), F32)
            for c in range(tm // CHUNK):
                rows = slice(c * CHUNK, (c + 1) * CHUNK)
                blk = vn[rows, cols]
                gate = jnp.dot(w_ref[h], blk, preferred_element_type=F32) + b_ref[:, h:h + 1]
                xu = x_ref[rows, cols]
                do = do_ref[rows, cols]
                dx_ref[rows, cols] = (do * gate * _gelu_grad(xu)).astype(BF)
                dgate = do * _gelu(xu)
                dgb = dgate.astype(BF)
                dw = dw + lax.dot_general(dgb, blk, _DIMS['nt'], preferred_element_type=F32)
                dbs = dbs + jnp.sum(dgate, axis=1, keepdims=True)
                dvn_scr[rows, cols] = jnp.dot(wt_ref[h], dgb, preferred_element_type=F32)
            _accumulate(dw_ref.at[h], first, jnp.where(tril, dw, 0.0))
            _accumulate(db_ref.at[h], first, dbs)
        dvn = dvn_scr[...]
        _accumulate(dg_ref, first, _colsum(dvn * vh))
        dvh = dvn * g_ref[...]
        dv = r * (dvh - vh * jnp.mean(dvh * vh, axis=-1, keepdims=True))
        dx_ref[:, GM_W:] = (dv * _gelu_grad(x_ref[:, GM_W:])).astype(BF)

    full3 = pl.BlockSpec((GM_HEADS, CHUNK, CHUNK), lambda i: (0, 0, 0))
    return pl.pallas_call(
        body, name=name, grid=(n // tm,),
        in_specs=[pl.BlockSpec((tm, 2 * GM_W), lambda i: (i, 0)), pl.BlockSpec((tm, GM_W), lambda i: (i, 0)),
                  full3, full3, pl.BlockSpec((CHUNK, GM_HEADS), lambda i: (0, 0)),
                  pl.BlockSpec((1, GM_W), lambda i: (0, 0))],
        out_specs=[pl.BlockSpec((tm, 2 * GM_W), lambda i: (i, 0)), full3,
                   pl.BlockSpec((GM_HEADS, CHUNK, 1), lambda i: (0, 0, 0)),
                   pl.BlockSpec((1, GM_W), lambda i: (0, 0))],
        out_shape=[jax.ShapeDtypeStruct((n, 2 * GM_W), BF), jax.ShapeDtypeStruct((GM_HEADS, CHUNK, CHUNK), F32),
                   jax.ShapeDtypeStruct((GM_HEADS, CHUNK, 1), F32), jax.ShapeDtypeStruct((1, GM_W), F32)],
        scratch_shapes=[pltpu.VMEM((tm, GM_W), F32)],
        compiler_params=_cp(("arbitrary",)))(uv, dout, wm, wmt, bst, gv)


def _s5_disc(lam_re, lam_im, log_dt, b_re, b_im):
    lr = jnp.minimum(lam_re, LAM_MAX)
    li = lam_im
    dt = jnp.exp(log_dt)
    mag = jnp.exp(lr * dt)
    ab_re = mag * jnp.cos(li * dt)
    ab_im = mag * jnp.sin(li * dt)
    den = lr * lr + li * li
    nr = ab_re - 1.0
    ni = ab_im
    z_re = (nr * lr + ni * li) / den
    z_im = (ni * lr - nr * li) / den
    return ab_re, ab_im, z_re * b_re - z_im * b_im, z_re * b_im + z_im * b_re


def _s5_disc_fwd(args, name):
    shp = jax.ShapeDtypeStruct(args[0].shape, F32)

    def body(*refs):
        outs = _s5_disc(*[r[...] for r in refs[:5]])
        for o_ref, o in zip(refs[5:], outs):
            o_ref[...] = o

    return pl.pallas_call(body, name=name, out_shape=[shp] * 4)(*args)


def _s5_disc_bwd(args, cts, name):
    shp = jax.ShapeDtypeStruct(args[0].shape, F32)

    def body(*refs):
        _, vjp = jax.vjp(_s5_disc, *[r[...] for r in refs[:5]])
        grads = vjp(tuple(r[...] for r in refs[5:9]))
        for o_ref, o in zip(refs[9:], grads):
            o_ref[...] = o

    return pl.pallas_call(body, name=name, out_shape=[shp] * 5)(*args, *cts)


def _cmul(a, b):
    return a[0] * b[0] - a[1] * b[1], a[0] * b[1] + a[1] * b[0]


def _scan_tables(ar, ai, reverse):
    if reverse:
        ai = -ai
    a1 = (ar, ai)
    a2 = _cmul(a1, a1)
    a3 = _cmul(a2, a1)
    a4 = _cmul(a2, a2)
    powers = [a1, a2, a3, a4, _cmul(a4, a1), _cmul(a4, a2), _cmul(a4, a3), _cmul(a4, a4)]
    row = lax.broadcasted_iota(jnp.int32, (8, NST), 0)
    zero = jnp.zeros((8, NST), F32)
    pr, pi = zero, zero
    for r in range(8):
        pw = powers[7 - r] if reverse else powers[r]
        pr = jnp.where(row == r, pw[0], pr)
        pi = jnp.where(row == r, pw[1], pi)
    levels = []
    for d, pw in ((1, a1), (2, a2), (4, a4)):
        ok = (row <= 7 - d) if reverse else (row >= d)
        levels.append((d, jnp.where(ok, pw[0], zero), jnp.where(ok, pw[1], zero)))
    return (pr, pi), levels


def _scan_block(src, dst, car, tables, n_tiles, reverse):
    (pr, pi), levels = tables
    row = lax.broadcasted_iota(jnp.int32, (8, NST), 0)
    out_row = 0 if reverse else 7

    def step(t, carry):
        cr, ci = carry
        tile = (n_tiles - 1 - t) if reverse else t
        rows = pl.ds(pl.multiple_of(tile * 8, 8), 8)
        xr = src[rows, 0:NST]
        xi = src[rows, NST:2 * NST]
        for d, dr, di in levels:
            shift = 8 - d if reverse else d
            rr = pltpu.roll(xr, shift, 0)
            ri = pltpu.roll(xi, shift, 0)
            xr, xi = xr + dr * rr - di * ri, xi + dr * ri + di * rr
        hr = xr + pr * cr - pi * ci
        hi = xi + pr * ci + pi * cr
        dst[rows, 0:NST] = hr
        dst[rows, NST:2 * NST] = hi
        return (_colsum(jnp.where(row == out_row, hr, 0.0)), _colsum(jnp.where(row == out_row, hi, 0.0)))

    cr, ci = lax.fori_loop(0, n_tiles, step, (car[0:1, 0:NST], car[0:1, NST:2 * NST]))
    car[0:1, 0:NST] = cr
    car[0:1, NST:2 * NST] = ci


def _s5_fwd(u, ab, bbt, cmat, dvec, wglu, bglu, seq, name):
    n = u.shape[0]
    tm = min(256, seq)

    def body(u_ref, ab_ref, bb_ref, c_ref, d_ref, w_ref, b_ref, h_ref, o_ref, xs, car):
        i = pl.program_id(0)

        @pl.when(lax.rem(i * tm, seq) == 0)
        def _():
            car[...] = jnp.zeros(car.shape, F32)

        uv = u_ref[...]
        xs[...] = jnp.dot(uv.astype(BF), bb_ref[...], preferred_element_type=F32)
        tables = _scan_tables(ab_ref[0:1, 0:NST], ab_ref[0:1, NST:2 * NST], False)
        _scan_block(xs, h_ref, car, tables, tm // 8, False)
        y = jnp.dot(h_ref[...].astype(BF), c_ref[...], preferred_element_type=F32) + d_ref[...] * uv
        g1 = _gelu(y)
        z = jnp.dot(g1.astype(BF), w_ref[...], preferred_element_type=F32) + b_ref[...]
        o_ref[...] = (g1 * _sigmoid(z)).astype(BF)

    const = lambda shape: pl.BlockSpec(shape, lambda i: (0, 0))
    return pl.pallas_call(
        body, name=name, grid=(n // tm,),
        in_specs=[pl.BlockSpec((tm, SSM_W), lambda i: (i, 0)), const((1, 2 * NST)), const((SSM_W, 2 * NST)),
                  const((2 * NST, SSM_W)), const((1, SSM_W)), const((SSM_W, SSM_W)), const((1, SSM_W))],
        out_specs=[pl.BlockSpec((tm, 2 * NST), lambda i: (i, 0)), pl.BlockSpec((tm, SSM_W), lambda i: (i, 0))],
        out_shape=[jax.ShapeDtypeStruct((n, 2 * NST), F32), jax.ShapeDtypeStruct((n, SSM_W), BF)],
        scratch_shapes=[pltpu.VMEM((tm, 2 * NST), F32), pltpu.VMEM((8, 2 * NST), F32)],
        compiler_params=_cp(("arbitrary",)))(u, ab, bbt, cmat, dvec, wglu, bglu)


def _s5_bwd(da, u, hst, ab, bbt, cmat, dvec, wglu, bglu, seq, name):
    n = u.shape[0]
    tm = min(256, seq)
    nb = n // tm
    blk = lambda r: nb - 1 - r
    prev, _ = _halo_maps(tm, n)

    def body(da_ref, u_ref, h_ref, hp_ref, ab_ref, bb_ref, c_ref, d_ref, w_ref, b_ref,
             du_ref, dw_ref, dbg_ref, dd_ref, dc_ref, dbb_ref, dab_ref, gs, car):
        r = pl.program_id(0)
        i = blk(r)
        first = r == 0

        @pl.when(lax.rem((i + 1) * tm, seq) == 0)
        def _():
            car[...] = jnp.zeros(car.shape, F32)

        uv = u_ref[...]
        dav = da_ref[...]
        hb = h_ref[...]
        hb16 = hb.astype(BF)
        dvv = d_ref[...]
        y = jnp.dot(hb16, c_ref[...], preferred_element_type=F32) + dvv * uv
        g1 = _gelu(y)
        g16 = g1.astype(BF)
        s = _sigmoid(jnp.dot(g16, w_ref[...], preferred_element_type=F32) + b_ref[...])
        dz = dav * g1 * s * (1.0 - s)
        dz16 = dz.astype(BF)
        dg1 = dav * s + lax.dot_general(dz16, w_ref[...], _DIMS['nt'], preferred_element_type=F32)
        _accumulate(dw_ref, first, lax.dot_general(g16, dz16, _DIMS['tn'], preferred_element_type=F32))
        _accumulate(dbg_ref, first, _colsum(dz))
        dy = dg1 * _gelu_grad(y)
        dy16 = dy.astype(BF)
        _accumulate(dd_ref, first, _colsum(dy * uv))
        _accumulate(dc_ref, first, lax.dot_general(hb16, dy16, _DIMS['tn'], preferred_element_type=F32))
        gs[...] = lax.dot_general(dy16, c_ref[...], _DIMS['nt'], preferred_element_type=F32)
        tables = _scan_tables(ab_ref[0:1, 0:NST], ab_ref[0:1, NST:2 * NST], True)
        _scan_block(gs, gs, car, tables, tm // 8, True)
        g = gs[...]
        g16b = g.astype(BF)
        sp = jnp.where(lax.rem(i * tm, seq) == 0, 0.0, 1.0)
        row = lax.broadcasted_iota(jnp.int32, hb.shape, 0)
        hprev = jnp.where(row == 0, hp_ref[7:8, :] * sp, pltpu.roll(hb, 1, 0))
        gr, gi = g[:, :NST], g[:, NST:]
        hr, hi = hprev[:, :NST], hprev[:, NST:]
        _accumulate(dab_ref.at[:, 0:NST], first, _colsum(gr * hr + gi * hi))
        _accumulate(dab_ref.at[:, NST:2 * NST], first, _colsum(gi * hr - gr * hi))
        _accumulate(dbb_ref, first, lax.dot_general(uv.astype(BF), g16b, _DIMS['tn'], preferred_element_type=F32))
        du = dy * dvv + lax.dot_general(g16b, bb_ref[...], _DIMS['nt'], preferred_element_type=F32)
        du_ref[...] = du.astype(BF)

    const = lambda shape: pl.BlockSpec(shape, lambda r: (0, 0))
    rowspec = lambda w: pl.BlockSpec((tm, w), lambda r: (blk(r), 0))
    return pl.pallas_call(
        body, name=name, grid=(nb,),
        in_specs=[rowspec(SSM_W), rowspec(SSM_W), rowspec(2 * NST),
                  pl.BlockSpec((8, 2 * NST), lambda r: (prev(blk(r)), 0)),
                  const((1, 2 * NST)), const((SSM_W, 2 * NST)), const((2 * NST, SSM_W)), const((1, SSM_W)),
                  const((SSM_W, SSM_W)), const((1, SSM_W))],
        out_specs=[rowspec(SSM_W), const((SSM_W, SSM_W)), const((1, SSM_W)), const((1, SSM_W)),
                   const((2 * NST, SSM_W)), const((SSM_W, 2 * NST)), const((1, 2 * NST))],
        out_shape=[jax.ShapeDtypeStruct((n, SSM_W), BF), jax.ShapeDtypeStruct((SSM_W, SSM_W), F32),
                   jax.ShapeDtypeStruct((1, SSM_W), F32), jax.ShapeDtypeStruct((1, SSM_W), F32),
                   jax.ShapeDtypeStruct((2 * NST, SSM_W), F32), jax.ShapeDtypeStruct((SSM_W, 2 * NST), F32),
                   jax.ShapeDtypeStruct((1, 2 * NST), F32)],
        scratch_shapes=[pltpu.VMEM((tm, 2 * NST), F32), pltpu.VMEM((8, 2 * NST), F32)],
        compiler_params=_cp(("arbitrary",)))(da, u, hst, hst, ab, bbt, cmat, dvec, wglu, bglu)


def _s5_rows(lam_re, lam_im, log_dt, b_re, b_im):
    rep = lambda a: jnp.broadcast_to(a[:, None, :], (SSM_G, SSM_H, SSM_P)).reshape(SSM_W, SSM_P)
    dt = jnp.broadcast_to(log_dt[:, None, None], (SSM_G, SSM_H, SSM_P)).reshape(SSM_W, SSM_P)
    tr = lambda b: b.transpose(0, 2, 1).reshape(SSM_W, SSM_P)
    return rep(lam_re), rep(lam_im), dt, tr(b_re), tr(b_im)


def _block_diag(rows_gp, inner):
    eye = jnp.eye(SSM_G, dtype=rows_gp.dtype)
    return (rows_gp[:, :, None, :] * eye[:, None, :, None]).reshape(SSM_G * inner, SSM_G * SSM_P)


def _diag_blocks(mat, inner):
    m4 = mat.reshape(SSM_G, inner, SSM_G, SSM_P)
    return jnp.stack([m4[g, :, g, :] for g in range(SSM_G)])


def _interleave(w, parts):
    lead = w.shape[:-1]
    nb = w.shape[-1] // (parts * FB)
    return jnp.swapaxes(w.reshape(lead + (parts, nb, FB)), -3, -2).reshape(w.shape)


def _deinterleave(w, parts):
    lead = w.shape[:-1]
    nb = w.shape[-1] // (parts * FB)
    return jnp.swapaxes(w.reshape(lead + (nb, parts, FB)), -3, -2).reshape(w.shape)


def _ffn_fwd(h, g, w_up, w_down, cw, cb, seq, tag):
    n = h.shape[0]
    tm = min(1024, n)
    ni = n // tm
    f = _rmsnorm_fwd(h, g, f"{tag}_norm")
    up = _matmul_spec(
        f, w_up, 'nn', (NDEV, ni, 1),
        pl.BlockSpec((tm, D), lambda s, i, k: (i, 0)),
        pl.BlockSpec((D, FSH), lambda s, i, k: (s, 0)),
        pl.BlockSpec((tm, FSH), lambda s, i, k: (s * ni + i, 0)), (NDEV * n, FSH), f"{tag}_up", out_dtype=BF)
    up = up.reshape(2, 4, n, FSH)
    act = _ffn_conv_fwd(up, cw, cb, seq, f"{tag}_conv")
    tn = 512
    out = _matmul_spec(
        act.reshape(4 * n, FSH), w_down, 'nn', (ni, D // tn, 4),
        pl.BlockSpec((tm, FSH), lambda i, j, k: (k * ni + i, 0)),
        pl.BlockSpec((FSH, tn), lambda i, j, k: (k, j)),
        pl.BlockSpec((tm, tn), lambda i, j, k: (i, j)), (n, D), f"{tag}_down", resid=h)
    return out, (f, up, act)


def _ffn_bwd(dh, h, g, w_up, w_down, cw, cb, saved, seq, tag):
    f, up, act = saved
    n = h.shape[0]
    tm = min(1024, n)
    ni = n // tm
    tk = min(2048, n)
    nk = n // tk
    dact = _matmul_spec(
        dh, w_down, 'nt', (4, ni, 1),
        pl.BlockSpec((tm, D), lambda j, i, k: (i, 0)),
        pl.BlockSpec((FSH, D), lambda j, i, k: (j, 0)),
        pl.BlockSpec((tm, FSH), lambda j, i, k: (j * ni + i, 0)), (4 * n, FSH), f"{tag}_ddown_x", out_dtype=BF)
    tn = 512
    dw_down = _matmul_spec(
        act.reshape(4 * n, FSH), dh, 'tn', (4, D // tn, nk),
        pl.BlockSpec((tk, FSH), lambda j, c, k: (j * nk + k, 0)),
        pl.BlockSpec((tk, tn), lambda j, c, k: (k, c)),
        pl.BlockSpec((FSH, tn), lambda j, c, k: (j, c)), (DFF, D), f"{tag}_ddown_w", out_dtype=BF)
    dup, dcw, dcb = _ffn_conv_bwd(up, dact.reshape(4, n, FSH), cw, cb, seq, f"{tag}_dconv")
    dup2 = dup.reshape(NDEV * n, FSH)
    df = _matmul_spec(
        dup2, w_up, 'nt', (ni, 1, NDEV),
        pl.BlockSpec((tm, FSH), lambda i, j, k: (k * ni + i, 0)),
        pl.BlockSpec((D, FSH), lambda i, j, k: (k, 0)),
        pl.BlockSpec((tm, D), lambda i, j, k: (i, 0)), (n, D), f"{tag}_dup_x")
    dw_up = _matmul_spec(
        f, dup2, 'tn', (NDEV, 1, nk),
        pl.BlockSpec((tk, D), lambda s, j, k: (k, 0)),
        pl.BlockSpec((tk, FSH), lambda s, j, k: (s * nk + k, 0)),
        pl.BlockSpec((D, FSH), lambda s, j, k: (s, 0)), (NDEV * D, FSH), f"{tag}_dup_w", out_dtype=BF)
    dh_in, dg = _rmsnorm_bwd(h, g, df, dh, f"{tag}_dnorm")
    grads = dict(g=dg, w_up=dw_up.reshape(NDEV, D, FSH), w_down=dw_down.reshape(NDEV, DFF // NDEV, D),
                 cw=dcw.reshape(NDEV, 3, FSH), cb=dcb.reshape(2 * DFF))
    return dh_in, grads


def _col_shards(w, width):
    return w.reshape(w.shape[0], NDEV, width).transpose(1, 0, 2)


def _local_step(x, tgt, w, gw, wait_ffn0, wait_rest, token, scatter, seq):
    bf = lambda a: a.astype(BF)
    row = lambda a: a.reshape(1, -1).astype(F32)
    w_ev = gw['ev_w_in'].transpose(1, 0, 2).reshape(D, 1792)
    w_ev_s5, w_ev_gm = w_ev[:, :SSM_W], w_ev[:, SSM_W:]
    w_evo = gw['ev_w_out'].reshape(D, D)
    f_cb = [w['ffn_conv_b'][l].reshape(2, 4, 1, FSH) for l in range(2)]
    tril = jnp.tril(jnp.ones((CHUNK, CHUNK), dtype=bool))
    gm_w = jnp.where(tril, w['gm_w_s'][0], 0.0)
    gm_wm, gm_wmt = bf(gm_w), bf(jnp.swapaxes(gm_w, 1, 2))
    gm_bt = w['gm_b_s'][0].T
    gm_gv = row(w['gm_v_g'][0])
    s5_in = _s5_rows(w['s5_lam_re'][0], w['s5_lam_im'][0], w['s5_log_dt'][0], w['s5_b_re'][0], w['s5_b_im'][0])
    ab_re, ab_im, bb_re, bb_im = _s5_disc_fwd(s5_in, "s5_disc")
    first_h = lambda a: a.reshape(SSM_G, SSM_H, SSM_P)[:, 0, :].reshape(1, NST)
    s5_ab = jnp.concatenate([first_h(ab_re), first_h(ab_im)], axis=1)
    to_gp = lambda a: a.reshape(SSM_G, SSM_H, SSM_P)
    s5_bbt = bf(jnp.concatenate([_block_diag(to_gp(bb_re), SSM_H), _block_diag(to_gp(bb_im), SSM_H)], axis=1))
    s5_cmat = bf(jnp.concatenate([_block_diag(w['s5_c_re'][0], SSM_H).T, -_block_diag(w['s5_c_im'][0], SSM_H).T],
                                 axis=0))
    s5_d, s5_bg, s5_wg = row(w['s5_d'][0]), row(w['s5_b_glu'][0]), gw['s5_w_glu'].reshape(SSM_W, SSM_W)
    g_mix = [row(w['mix_norm_g'][0]) + token[0:1, 0:1], row(w['mix_norm_g'][1])]
    g_ffn = [row(w['ffn_norm_g'][l]) for l in range(2)]
    g_fin = row(w['final_norm_g'])

    h0 = x
    y0 = _rmsnorm_fwd(h0, g_mix[0], "ev_norm")
    p_s5 = _matmul(y0, w_ev_s5, 'nn', 1024, 256, D, "ev_in_s5")
    p_gm = _matmul(y0, w_ev_gm, 'nn', 1024, 512, D, "ev_in_gm")
    hst, a_out = _s5_fwd(p_s5, s5_ab, s5_bbt, s5_cmat, s5_d, s5_wg, s5_bg, seq, "s5_fwd")
    b_out = _gmlp_fwd(p_gm, gm_wm, gm_bt, gm_gv, seq, "gmlp_fwd")
    mixcat = jnp.concatenate([a_out, b_out], axis=1)
    h1 = _matmul(mixcat, w_evo, 'nn', 1024, 512, D, "ev_out", resid=h0)
    g0 = wait_ffn0(mixcat)
    w_up0, w_dn0 = g0['ffn_w_up0'].reshape(NDEV * D, FSH), g0['ffn_w_down0'].reshape(DFF, D)
    f_cw0 = g0['ffn_conv_w0'].reshape(2, 4, 3, FSH)
    h2, ffn0 = _ffn_fwd(h1, g_ffn[0], w_up0, w_dn0, f_cw0, f_cb[0], seq, "ffn0")
    g1 = wait_rest(h2)
    w_od = _interleave(g1['od_w_in'].transpose(1, 0, 2).reshape(D, 3 * D), 3)
    w_odo = g1['od_w_out'].reshape(D, D)
    od_cw = g1['od_conv_w'].transpose(1, 0, 2).reshape(3, D)
    od_cb = g1['od_conv_b'].reshape(1, D)
    w_up1, w_dn1 = g1['ffn_w_up1'].reshape(NDEV * D, FSH), g1['ffn_w_down1'].reshape(DFF, D)
    f_cw1 = g1['ffn_conv_w1'].reshape(2, 4, 3, FSH)
    y1 = _rmsnorm_fwd(h2, g_mix[1], "od_norm")
    p_od = _matmul(y1, w_od, 'nn', 1024, 512, D, "od_in")
    mixin = _shortconv_fwd(p_od, od_cw, od_cb, seq, "od_conv")
    h3 = _matmul(mixin, w_odo, 'nn', 1024, 512, D, "od_out", resid=h2)
    h4, ffn1 = _ffn_fwd(h3, g_ffn[1], w_up1, w_dn1, f_cw1, f_cb[1], seq, "ffn1")
    loss, dh4, dg_fin = _final_loss(h4, g_fin, tgt, "final_loss")

    dh3, gf1 = _ffn_bwd(dh4, h3, g_ffn[1], w_up1, w_dn1, f_cw1, f_cb[1], ffn1, seq, "ffn1")
    dmixin = _matmul(dh3, w_odo, 'nt', 1024, 512, D, "od_dout_x")
    dw_odo = _matmul(mixin, dh3, 'tn', D, 512, 1024, "od_dout_w", out_dtype=BF)
    dp_od, d_od_cw, d_od_cb = _shortconv_bwd(p_od, dmixin, od_cw, od_cb, seq, "od_dconv")
    dy1 = _matmul(dp_od, w_od, 'nt', 1024, D, 512, "od_din_x")
    dw_od = _matmul(y1, dp_od, 'tn', D, 512, 1024, "od_din_w", out_dtype=BF)
    sent = scatter("scatter_layer1", {
        'od_w_in': _col_shards(_deinterleave(dw_od, 3), 384), 'od_conv_w': _col_shards(d_od_cw, D // NDEV),
        'od_conv_b': d_od_cb.reshape(NDEV, 1, D // NDEV), 'od_w_out': dw_odo.reshape(NDEV, D // NDEV, D),
        'ffn_w_up1': gf1['w_up'], 'ffn_conv_w1': gf1['cw'], 'ffn_w_down1': gf1['w_down']})
    dh2, dg_mix1 = _rmsnorm_bwd(h2, g_mix[1] + sent[0:1, 0:1], dy1, dh3, "od_dnorm")
    dh1, gf0 = _ffn_bwd(dh2, h1, g_ffn[0], w_up0, w_dn0, f_cw0, f_cb[0], ffn0, seq, "ffn0")
    dmix_a = _matmul(dh1, w_evo[:SSM_W], 'nt', 1024, SSM_W, D, "ev_dout_xa")
    dmix_b = _matmul(dh1, w_evo[SSM_W:], 'nt', 1024, GM_W, D, "ev_dout_xb")
    dw_evo = _matmul(mixcat, dh1, 'tn', D, 512, 1024, "ev_dout_w", out_dtype=BF)
    sent = scatter("scatter_ffn0", {'ffn_w_up0': gf0['w_up'], 'ffn_conv_w0': gf0['cw'], 'ffn_w_down0': gf0['w_down'],
                                    'ev_w_out': dw_evo.reshape(NDEV, D // NDEV, D)})
    dp_s5, d_wg, d_bg, d_d, d_cmat, d_bbt, d_ab = _s5_bwd(dmix_a, p_s5, hst, s5_ab, s5_bbt, s5_cmat,
                                                           s5_d + sent[0:1, 0:1], s5_wg, s5_bg, seq, "s5_bwd")
    dp_gm, d_gmw, d_gmb, d_gmg = _gmlp_bwd(p_gm, dmix_b, gm_wm, gm_wmt, gm_bt, gm_gv, seq, "gmlp_bwd")
    dw_ev = jnp.concatenate([_matmul(y0, dp_s5, 'tn', D, SSM_W, 1024, "ev_din_wa", out_dtype=BF),
                             _matmul(y0, dp_gm, 'tn', D, 512, 1024, "ev_din_wb", out_dtype=BF)], axis=1)
    sent = scatter("scatter_even", {'ev_w_in': _col_shards(dw_ev, 224),
                                    's5_w_glu': d_wg.reshape(NDEV, SSM_W // NDEV, SSM_W)})
    dy0 = _matmul(dp_gm, w_ev_gm, 'nt', 1024, D, 512, "ev_din_xb")
    dy0 = _matmul(dp_s5, w_ev_s5, 'nt', 1024, D, SSM_W, "ev_din_xa", resid=dy0)
    grad_x, dg_mix0 = _rmsnorm_bwd(h0, g_mix[0] + sent[0:1, 0:1], dy0, dh1, "ev_dnorm")

    put_h0 = lambda a: jnp.zeros((SSM_G, SSM_H, SSM_P), F32).at[:, 0, :].set(a.reshape(SSM_G, SSM_P)).reshape(
        SSM_W, SSM_P)
    ct = (put_h0(d_ab[:, :NST]), put_h0(d_ab[:, NST:]),
          _diag_blocks(d_bbt[:, :NST], SSM_H).reshape(SSM_W, SSM_P),
          _diag_blocks(d_bbt[:, NST:], SSM_H).reshape(SSM_W, SSM_P))
    d_lre, d_lim, d_ldt, d_bre, d_bim = _s5_disc_bwd(s5_in, ct, "s5_ddisc")
    over_h = lambda a: a.reshape(SSM_G, SSM_H, SSM_P).sum(axis=1)
    un_tr = lambda a: a.reshape(SSM_G, SSM_H, SSM_P).transpose(0, 2, 1)
    d_cre = _diag_blocks(d_cmat[:NST].T, SSM_H)
    d_cim = -_diag_blocks(d_cmat[NST:].T, SSM_H)

    repl = {
        'mix_norm_g': jnp.concatenate([dg_mix0, dg_mix1], axis=0),
        'ffn_norm_g': jnp.concatenate([gf0['g'], gf1['g']], axis=0),
        'final_norm_g': dg_fin.reshape(D),
        's5_lam_re': over_h(d_lre)[None], 's5_lam_im': over_h(d_lim)[None],
        's5_log_dt': over_h(d_ldt).sum(axis=1)[None],
        's5_b_re': un_tr(d_bre)[None], 's5_b_im': un_tr(d_bim)[None],
        's5_c_re': d_cre[None], 's5_c_im': d_cim[None],
        's5_d': d_d, 's5_b_glu': d_bg,
        'gm_w_s': d_gmw[None], 'gm_b_s': d_gmb.reshape(1, GM_HEADS, CHUNK), 'gm_v_g': d_gmg,
        'ffn_conv_b': jnp.stack([gf0['cb'], gf1['cb']]),
    }
    return loss, grad_x, repl


HBM_SPEC = pl.BlockSpec(memory_space=pltpu.HBM)


def _at_axis(ref, pos, index):
    return ref.at[(slice(None),) * pos + (index,)]


def _all_gather(shards, positions, name):
    n = len(shards)

    def body(*refs):
        xs, outs = refs[:n], refs[n:2 * n]
        send_sems, recv_sems, local_sems = refs[2 * n:]
        x, y, c = lax.axis_index("x"), lax.axis_index("y"), lax.axis_index("c")
        me, sibling = (x, y, c), (x, y, 1 - c)
        chips = [(1 - x, y), (x, 1 - y), (1 - x, 1 - y)]

        def block(p, dev):
            return _at_axis(outs[p], positions[p], 4 * dev[0] + 2 * dev[1] + dev[2])

        def copy(p, k, dev, to, src=None):
            return pltpu.make_async_remote_copy(
                src_ref=block(p, dev) if src is None else src, dst_ref=block(p, dev),
                send_sem=send_sems.at[p, k], recv_sem=recv_sems.at[p, k], device_id=to, device_id_type=MESH_T)

        mine = [pltpu.make_async_copy(xs[p], block(p, me), local_sems.at[p]) for p in range(n)]
        for cp in mine:
            cp.start()
        first = [copy(p, 0, me, sibling, src=xs[p]) for p in range(n)]
        first += [copy(p, 1 + j, me, (*chip, c), src=xs[p]) for j, chip in enumerate(chips) for p in range(n)]
        for cp in first:
            cp.start()
        passed = []
        for j, chip in enumerate(chips):
            for p in range(n):
                copy(p, 1 + j, (*chip, c), me).wait_recv()
                fwd = copy(p, 4 + j, (*chip, c), sibling)
                fwd.start()
                passed.append(fwd)
        for p in range(n):
            copy(p, 0, sibling, me).wait_recv()
        for j, chip in enumerate(chips):
            for p in range(n):
                copy(p, 4 + j, (*chip, 1 - c), me).wait_recv()
        for cp in first + passed:
            cp.wait_send()
        for cp in mine:
            cp.wait()

    out_shape = [jax.ShapeDtypeStruct(s.shape[:pos] + (NDEV,) + s.shape[pos:], s.dtype)
                 for s, pos in zip(shards, positions)]
    return pl.pallas_call(
        body, name=name, out_shape=out_shape, in_specs=[HBM_SPEC] * n, out_specs=[HBM_SPEC] * n,
        scratch_shapes=[pltpu.SemaphoreType.DMA((n, 7)), pltpu.SemaphoreType.DMA((n, 7)),
                        pltpu.SemaphoreType.DMA((n,))])(*shards)


def _other_devices(x, y, c):
    flip = lambda v, bit: 1 - v if bit else v
    return [(flip(x, k >> 2 & 1), flip(y, k >> 1 & 1), flip(c, k & 1)) for k in range(1, NDEV)]


SEM_SPEC = pl.BlockSpec(memory_space=pltpu.SEMAPHORE)
START_EFFECT = pltpu.SideEffectType.DATAFLOW_SIDE_EFFECTING


def _send_start(arrays, scatter, name):
    n = len(arrays)
    lands = [lax.empty((NDEV,) + (a.shape[1:] if scatter else a.shape), a.dtype) for a in arrays]

    def body(*refs):
        xs, ls = refs[:n], refs[n:2 * n]
        send_sems, recv_sems, own_sems, token = refs[2 * n], refs[2 * n + 1], refs[2 * n + 2], refs[4 * n + 3]
        x, y, c = lax.axis_index("x"), lax.axis_index("y"), lax.axis_index("c")
        me = 4 * x + 2 * y + c
        for k, peer in enumerate(_other_devices(x, y, c)):
            for p in range(n):
                src = xs[p].at[4 * peer[0] + 2 * peer[1] + peer[2]] if scatter else xs[p]
                pltpu.make_async_remote_copy(
                    src_ref=src, dst_ref=ls[p].at[me], send_sem=send_sems.at[p * (NDEV - 1) + k],
                    recv_sem=recv_sems.at[p * (NDEV - 1) + k], device_id=peer, device_id_type=MESH_T).start()
        for p in range(n):
            pltpu.make_async_copy(xs[p].at[me] if scatter else xs[p], ls[p].at[me], own_sems.at[p]).start()
        token[...] = jnp.zeros(token.shape, F32)

    sems = pltpu.SemaphoreType.DMA((n * (NDEV - 1),))
    out_shape = ([sems, sems, pltpu.SemaphoreType.DMA((n,))]
                 + [pltpu.HBM(a.shape, a.dtype) for a in list(arrays) + lands] + [jax.ShapeDtypeStruct((8, 128), F32)])
    res = pl.pallas_call(
        body, name=name, out_shape=out_shape, in_specs=[HBM_SPEC] * (2 * n),
        out_specs=[SEM_SPEC] * 3 + [HBM_SPEC] * (2 * n) + [pl.BlockSpec(memory_space=pltpu.VMEM)],
        input_output_aliases={i: 3 + i for i in range(2 * n)},
        compiler_params=pltpu.CompilerParams(has_side_effects=START_EFFECT))(
            *[pltpu.with_memory_space_constraint(a, pltpu.HBM) for a in list(arrays) + lands])
    return res[:3], res[3:3 + n], res[3 + n:3 + 2 * n], res[3 + 2 * n]


def _send_wait(started, scatter, after, name):
    sems, arrays, lands, _ = started
    n = len(arrays)

    def body(*refs):
        xs, ls = refs[:n], refs[n:2 * n]
        send, recv, own = refs[2 * n:2 * n + 3]
        x, y, c = lax.axis_index("x"), lax.axis_index("y"), lax.axis_index("c")
        me = 4 * x + 2 * y + c
        for p in range(n):
            pltpu.make_async_copy(xs[p].at[me] if scatter else xs[p], ls[p].at[me], own.at[p]).wait()
        for k, peer in enumerate(_other_devices(x, y, c)):
            slot = 4 * peer[0] + 2 * peer[1] + peer[2]
            for p in range(n):
                cp = pltpu.make_async_remote_copy(
                    src_ref=xs[p].at[slot] if scatter else xs[p], dst_ref=ls[p].at[slot],
                    send_sem=send.at[p * (NDEV - 1) + k], recv_sem=recv.at[p * (NDEV - 1) + k], device_id=peer,
                    device_id_type=MESH_T)
                cp.wait_send()
                cp.wait_recv()

    res = pl.pallas_call(
        body, name=name, out_shape=[pltpu.HBM(a.shape, a.dtype) for a in list(arrays) + list(lands)],
        in_specs=[HBM_SPEC] * (2 * n) + [SEM_SPEC] * 3 + [pl.BlockSpec(memory_space=pl.ANY)],
        out_specs=[HBM_SPEC] * (2 * n), input_output_aliases={i: i for i in range(2 * n)},
        compiler_params=pltpu.CompilerParams(has_side_effects=START_EFFECT))(
            *arrays, *lands, *sems, after)
    return res[n:]


def _row_block(rows, cols, itemsize=4, target=2**20):
    best = None
    for tr in range(16, rows + 1, 16):
        if rows % tr == 0 and tr * cols * itemsize <= target:
            best = tr
    return best or rows


def _adamw(w, m, v, gparts, name):
    parts, rows, cols = gparts.shape
    tr = _row_block(rows, cols, target=2**19)
    bc1 = 1.0 - ADAM_B1 ** ADAM_STEP
    bc2 = 1.0 - ADAM_B2 ** ADAM_STEP

    def body(w_ref, m_ref, v_ref, g_ref, go_ref, d_ref, mo_ref, vo_ref):
        g = g_ref[0].astype(F32)
        for k in range(1, parts):
            g = g + g_ref[k].astype(F32)
        mn = ADAM_B1 * m_ref[...] + (1.0 - ADAM_B1) * g
        vn = ADAM_B2 * v_ref[...] + (1.0 - ADAM_B2) * (g * g)
        go_ref[...] = g
        mo_ref[...] = mn
        vo_ref[...] = vn
        d_ref[...] = -ADAM_LR * ((mn / bc1) / (jnp.sqrt(vn / bc2) + ADAM_EPS) + ADAM_WD * w_ref[...])

    blk = pl.BlockSpec((tr, cols), lambda i: (i, 0))
    shp = jax.ShapeDtypeStruct((rows, cols), F32)
    return pl.pallas_call(
        body, name=name, grid=(rows // tr,),
        in_specs=[blk, blk, blk, pl.BlockSpec((parts, tr, cols), lambda i: (0, i, 0))],
        out_specs=[blk] * 4, out_shape=[shp] * 4, compiler_params=_cp(("parallel",)))(w, m, v, gparts)


def _pack(arrays, rows):
    flat = jnp.concatenate([a.reshape(-1).astype(F32) for a in arrays])
    return jnp.pad(flat, (0, rows * PACK_COLS - flat.shape[0])).reshape(rows, PACK_COLS)


def _unpack(buf, shapes):
    flat = buf.reshape(-1)
    out, off = [], 0
    for shp in shapes:
        size = int(np.prod(shp))
        out.append(flat[off:off + size].reshape(shp))
        off += size
    return out


REPL_SHAPES = {'mix_norm_g': (2, 1024), 'ffn_norm_g': (2, 1024), 'final_norm_g': (1024,), 's5_lam_re': (1, 16, 64),
               's5_lam_im': (1, 16, 64), 's5_log_dt': (1, 16), 's5_b_re': (1, 16, 64, 16), 's5_b_im': (1, 16, 64, 16),
               's5_c_re': (1, 16, 16, 64), 's5_c_im': (1, 16, 16, 64), 's5_d': (1, 256), 's5_b_glu': (1, 256),
               'gm_w_s': (1, 6, 128, 128), 'gm_b_s': (1, 6, 128), 'gm_v_g': (1, 768), 'ffn_conv_b': (2, 5632)}
REPL_ELEMS = sum(int(np.prod(REPL_SHAPES[n])) for n in REPL_ORDER)
REPL_ROWS = -(-REPL_ELEMS // (PACK_COLS * 8)) * 8

GATHER_DTYPE = {'ev_w_in': BF, 'ev_w_out': BF, 's5_w_glu': BF, 'od_w_in': BF, 'od_conv_w': F32, 'od_conv_b': F32,
                'od_w_out': BF, 'ffn_w_up': BF, 'ffn_conv_w': F32, 'ffn_w_down': BF}
GATHER_EVEN = ['ev_w_in', 'ev_w_out', 's5_w_glu']
GATHER_FFN0 = ['ffn_w_up0', 'ffn_conv_w0', 'ffn_w_down0']
GATHER_REST = ['od_w_in', 'od_conv_w', 'od_conv_b', 'od_w_out', 'ffn_w_up1', 'ffn_conv_w1', 'ffn_w_down1']

def _squeeze_lead(a):
    return a.reshape(a.shape[1:]) if a.shape[0] == 1 and a.ndim > 2 else a


def kernel(x, mix_norm_g, ffn_norm_g, final_norm_g, ev_w_in, ev_w_out, s5_lam_re, s5_lam_im, s5_log_dt, s5_b_re, s5_b_im, s5_c_re, s5_c_im, s5_d, s5_w_glu, s5_b_glu, gm_w_s, gm_b_s, gm_v_g, od_w_in, od_conv_w, od_conv_b, od_w_out, ffn_w_up, ffn_conv_w, ffn_conv_b, ffn_w_down, loss_target, m_mix_norm_g, m_ffn_norm_g, m_final_norm_g, m_ev_w_in, m_ev_w_out, m_s5_lam_re, m_s5_lam_im, m_s5_log_dt, m_s5_b_re, m_s5_b_im, m_s5_c_re, m_s5_c_im, m_s5_d, m_s5_w_glu, m_s5_b_glu, m_gm_w_s, m_gm_b_s, m_gm_v_g, m_od_w_in, m_od_conv_w, m_od_conv_b, m_od_w_out, m_ffn_w_up, m_ffn_conv_w, m_ffn_conv_b, m_ffn_w_down, v_mix_norm_g, v_ffn_norm_g, v_final_norm_g, v_ev_w_in, v_ev_w_out, v_s5_lam_re, v_s5_lam_im, v_s5_log_dt, v_s5_b_re, v_s5_b_im, v_s5_c_re, v_s5_c_im, v_s5_d, v_s5_w_glu, v_s5_b_glu, v_gm_w_s, v_gm_b_s, v_gm_v_g, v_od_w_in, v_od_conv_w, v_od_conv_b, v_od_w_out, v_ffn_w_up, v_ffn_conv_w, v_ffn_conv_b, v_ffn_w_down):
    given = dict(locals())
    weights = {n: given[n] for n in WEIGHT_ORDER}
    nseq, seq, _ = x.shape

    send = {}
    for name in SHARDED_ORDER:
        a = weights[name].astype(GATHER_DTYPE[name])
        if a.shape[0] == 2:
            send[name + '0'], send[name + '1'] = a[0], a[1]
        else:
            send[name] = _squeeze_lead(a)
    gathers = [_send_start([send[n] for n in names], False, f"gather_{tag}_start")
               for tag, names in (("ffn0", GATHER_FFN0), ("rest", GATHER_REST))]
    token = gathers[0][3] + gathers[1][3]

    def waiter(tag, names, started):
        return lambda after: dict(zip(names, _send_wait(started, False, after, f"gather_{tag}_wait")))

    gathered = dict(zip(GATHER_EVEN, _all_gather([send[n] for n in GATHER_EVEN], [0] * len(GATHER_EVEN),
                                                 "gather_even")))

    scatters = []

    def scatter(tag, grads):
        names = list(grads)
        started = _send_start([grads[n].astype(BF) for n in names], True, f"{tag}_start")
        scatters.append((tag, names, started))
        return started[3]

    loss_row, grad_x, g_repl = _local_step(
        x.reshape(nseq * seq, D), loss_target.reshape(nseq * seq, D), weights, gathered,
        waiter("ffn0", GATHER_FFN0, gathers[0]), waiter("rest", GATHER_REST, gathers[1]), token, scatter, seq)
    loss = lax.psum(loss_row[0, 0], ("x", "y", "c"))

    parts = {}
    for tag, names, started in scatters:
        parts.update(zip(names, _send_wait(started, True, grad_x, f"{tag}_wait")))
    repl_parts = _all_gather([_pack([g_repl[n] for n in REPL_ORDER], REPL_ROWS)], [0], "gather_small_grads")[0]

    out = {}
    for name in SHARDED_ORDER:
        w = weights[name]
        if name + '0' in parts:
            gp = jnp.stack([parts[name + '0'], parts[name + '1']], axis=1)
        else:
            gp = parts[name]
        to_rows = lambda a: a.reshape(-1, w.shape[-1])
        res = _adamw(to_rows(w), to_rows(given["m_" + name]), to_rows(given["v_" + name]),
                     gp.reshape(NDEV, -1, w.shape[-1]), f"adamw_{name}")
        out[name] = [r.reshape(w.shape) for r in res]
    rp = _adamw(_pack([weights[n] for n in REPL_ORDER], REPL_ROWS),
                _pack([given["m_" + n] for n in REPL_ORDER], REPL_ROWS),
                _pack([given["v_" + n] for n in REPL_ORDER], REPL_ROWS), repl_parts, "adamw_replicated")
    rp_shapes = [weights[n].shape for n in REPL_ORDER]
    for k in range(4):
        for name, a in zip(REPL_ORDER, _unpack(rp[k], rp_shapes)):
            out.setdefault(name, [None] * 4)[k] = a
    results = [[out[n][k] for n in WEIGHT_ORDER] for k in range(4)]
    grad_w, delta_w, new_m, new_v = results
    return (loss, grad_x.reshape(nseq, seq, D), *grad_w, *delta_w, *new_m, *new_v)
```

```python
import math

import jax
import jax.numpy as jnp
import numpy as np
from jax import lax
from jax.experimental import pallas as pl
from jax.experimental.pallas import tpu as pltpu

F32 = jnp.float32
BF = jnp.bfloat16

D = 1024
DFF = 2816
NDEV = 8
SSM_W = 256
SSM_G = 16
SSM_H = 16
SSM_P = 64
NST = SSM_G * SSM_P
GM_W = 768
GM_HEADS = 6
CHUNK = 128
EPS = 1e-6
LAM_MAX = -1e-4
FB = 256
FSH = 2 * DFF // NDEV
VMEM_LIMIT = 48 * 2**20
PACK_COLS = 1024
MESH_T = pl.DeviceIdType.MESH

ADAM_LR = 0.001
ADAM_B1 = 0.9
ADAM_B2 = 0.999
ADAM_EPS = 1e-08
ADAM_WD = 0.01
ADAM_STEP = 10

WEIGHT_ORDER = ['mix_norm_g', 'ffn_norm_g', 'final_norm_g', 'ev_w_in', 'ev_w_out', 's5_lam_re', 's5_lam_im',
                's5_log_dt', 's5_b_re', 's5_b_im', 's5_c_re', 's5_c_im', 's5_d', 's5_w_glu', 's5_b_glu', 'gm_w_s',
                'gm_b_s', 'gm_v_g', 'od_w_in', 'od_conv_w', 'od_conv_b', 'od_w_out', 'ffn_w_up', 'ffn_conv_w',
                'ffn_conv_b', 'ffn_w_down']
SHARDED = {'ev_w_in': ((1, 1024, 1792), 2), 'ev_w_out': ((1, 1024, 1024), 1), 's5_w_glu': ((1, 256, 256), 1),
           'od_w_in': ((1, 1024, 3072), 2), 'od_conv_w': ((1, 3, 1024), 2), 'od_conv_b': ((1, 1024), 1),
           'od_w_out': ((1, 1024, 1024), 1), 'ffn_w_up': ((2, 1024, 5632), 2), 'ffn_conv_w': ((2, 3, 5632), 2),
           'ffn_w_down': ((2, 2816, 1024), 1)}
SHARDED_ORDER = [n for n in WEIGHT_ORDER if n in SHARDED]
REPL_ORDER = [n for n in WEIGHT_ORDER if n not in SHARDED]


def _cp(sem):
    return pltpu.CompilerParams(dimension_semantics=sem, vmem_limit_bytes=VMEM_LIMIT)


def _sigmoid(x):
    return 1.0 / (1.0 + jnp.exp(-x))


_GELU_K = math.sqrt(2.0 / math.pi)


def _gelu(x):
    return 0.5 * x * (1.0 + jnp.tanh(_GELU_K * (x + 0.044715 * x * x * x)))


def _gelu_grad(x):
    t = jnp.tanh(_GELU_K * (x + 0.044715 * x * x * x))
    return 0.5 * (1.0 + t) + 0.5 * x * (1.0 - t * t) * _GELU_K * (1.0 + 3.0 * 0.044715 * x * x)


def _colsum(x):
    return jnp.sum(x, axis=0, keepdims=True)


def _accumulate(ref, first, part):
    @pl.when(first)
    def _():
        ref[...] = part

    @pl.when(jnp.logical_not(first))
    def _():
        ref[...] += part


_DIMS = {'nn': (((1,), (0,)), ((), ())), 'nt': (((1,), (1,)), ((), ())), 'tn': (((0,), (0,)), ((), ()))}


def _matmul(a, b, mode, tm, tn, tk, name, resid=None, out_dtype=F32):
    if mode == 'tn':
        kdim, m = a.shape
    else:
        m, kdim = a.shape
    n = b.shape[0] if mode == 'nt' else b.shape[1]
    tm, tn, tk = min(tm, m), min(tn, n), min(tk, kdim)
    assert m % tm == 0 and n % tn == 0 and kdim % tk == 0, (name, m, n, kdim, tm, tn, tk)
    a_spec = (pl.BlockSpec((tk, tm), lambda i, j, k: (k, i)) if mode == 'tn'
              else pl.BlockSpec((tm, tk), lambda i, j, k: (i, k)))
    b_spec = (pl.BlockSpec((tn, tk), lambda i, j, k: (j, k)) if mode == 'nt'
              else pl.BlockSpec((tk, tn), lambda i, j, k: (k, j)))
    o_spec = pl.BlockSpec((tm, tn), lambda i, j, k: (i, j))
    return _matmul_spec(a, b, mode, (m // tm, n // tn, kdim // tk), a_spec, b_spec, o_spec, (m, n), name,
                        resid=resid, out_dtype=out_dtype)


def _matmul_spec(a, b, mode, grid, a_spec, b_spec, o_spec, out_shape, name, resid=None, out_dtype=F32):
    nk = grid[2]
    tm, tn = o_spec.block_shape[-2:]
    dims = _DIMS[mode]
    has_resid = resid is not None

    def body(*refs):
        if has_resid:
            a_ref, b_ref, r_ref, o_ref = refs[:4]
        else:
            a_ref, b_ref, o_ref = refs[:3]
            r_ref = None
        part = lax.dot_general(a_ref[...].astype(BF), b_ref[...].astype(BF), dims, preferred_element_type=F32)
        if nk == 1:
            if has_resid:
                part = part + r_ref[...]
            o_ref[...] = part.astype(out_dtype)
        else:
            acc = refs[-1]
            k = pl.program_id(2)

            @pl.when(k == 0)
            def _():
                acc[...] = part

            @pl.when(k > 0)
            def _():
                acc[...] += part

            @pl.when(k == nk - 1)
            def _():
                tot = acc[...]
                if has_resid:
                    tot = tot + r_ref[...]
                o_ref[...] = tot.astype(out_dtype)

    operands = [a, b] + ([resid] if has_resid else [])
    in_specs = [a_spec, b_spec] + ([o_spec] if has_resid else [])
    return pl.pallas_call(
        body, name=name, grid=grid, in_specs=in_specs, out_specs=o_spec,
        out_shape=jax.ShapeDtypeStruct(out_shape, out_dtype),
        scratch_shapes=[pltpu.VMEM((tm, tn), F32)] if nk > 1 else [],
        compiler_params=_cp(("parallel", "parallel", "arbitrary")))(*operands)


def _matmul_shards(a, b, mode, tm, tn, name, resid=None, out_dtype=F32):
    shards, m, kdim = a.shape
    n = b.shape[2] if mode == 'nn' else b.shape[1]
    tm, tn = min(tm, m), min(tn, n)
    dims = _DIMS[mode]
    has_resid = resid is not None

    def body(*refs):
        a_ref, b_ref = refs[:2]
        acc = lax.dot_general(a_ref[0], b_ref[0], dims, preferred_element_type=F32)
        for s in range(1, shards):
            acc = acc + lax.dot_general(a_ref[s], b_ref[s], dims, preferred_element_type=F32)
        if has_resid:
            acc = acc + refs[2][...]
        refs[-1][...] = acc.astype(out_dtype)

    b_spec = (pl.BlockSpec((shards, kdim, tn), lambda i, j: (0, 0, j)) if mode == 'nn'
              else pl.BlockSpec((shards, tn, kdim), lambda i, j: (0, j, 0)))
    o_spec = pl.BlockSpec((tm, tn), lambda i, j: (i, j))
    return pl.pallas_call(
        body, name=name, grid=(m // tm, n // tn),
        in_specs=[pl.BlockSpec((shards, tm, kdim), lambda i, j: (0, i, 0)), b_spec] + ([o_spec] if has_resid else []),
        out_specs=o_spec, out_shape=jax.ShapeDtypeStruct((m, n), out_dtype),
        compiler_params=_cp(("parallel", "parallel")))(*([a, b] + ([resid] if has_resid else [])))


def _rmsnorm_fwd(x, g, name):
    n = x.shape[0]
    tm = min(512, n)

    def body(x_ref, g_ref, o_ref):
        xv = x_ref[...]
        r = lax.rsqrt(jnp.mean(xv * xv, axis=-1, keepdims=True) + EPS)
        o_ref[...] = (xv * r * g_ref[...]).astype(BF)

    return pl.pallas_call(
        body, name=name, grid=(n // tm,),
        in_specs=[pl.BlockSpec((tm, D), lambda i: (i, 0)), pl.BlockSpec((1, D), lambda i: (0, 0))],
        out_specs=pl.BlockSpec((tm, D), lambda i: (i, 0)),
        out_shape=jax.ShapeDtypeStruct((n, D), BF), compiler_params=_cp(("parallel",)))(x, g)


def _rmsnorm_bwd(x, g, dy, dres, name):
    n = x.shape[0]
    tm = min(512, n)

    def body(x_ref, g_ref, dy_ref, dr_ref, dx_ref, dxb_ref, dg_ref):
        xv = x_ref[...]
        r = lax.rsqrt(jnp.mean(xv * xv, axis=-1, keepdims=True) + EPS)
        xh = xv * r
        dyv = dy_ref[...]
        dyg = dyv * g_ref[...]
        dx = dr_ref[...] + r * (dyg - xh * jnp.mean(dyg * xh, axis=-1, keepdims=True))
        dx_ref[...] = dx
        dxb_ref[...] = dx.astype(BF)
        _accumulate(dg_ref, pl.program_id(0) == 0, _colsum(dyv * xh))

    row = pl.BlockSpec((tm, D), lambda i: (i, 0))
    vec = pl.BlockSpec((1, D), lambda i: (0, 0))
    return pl.pallas_call(
        body, name=name, grid=(n // tm,), in_specs=[row, vec, row, row], out_specs=[row, row, vec],
        out_shape=[jax.ShapeDtypeStruct((n, D), F32), jax.ShapeDtypeStruct((n, D), BF),
                   jax.ShapeDtypeStruct((1, D), F32)],
        compiler_params=_cp(("arbitrary",)))(x, g, dy, dres)


def _final_loss(h, g, tgt, name):
    n = h.shape[0]
    tm = min(512, n)

    def body(x_ref, g_ref, t_ref, loss_ref, dx_ref, dxb_ref, dg_ref):
        first = pl.program_id(0) == 0
        xv = x_ref[...]
        gv = g_ref[...]
        r = lax.rsqrt(jnp.mean(xv * xv, axis=-1, keepdims=True) + EPS)
        xh = xv * r
        err = xh * gv - t_ref[...]
        part = 0.5 * jnp.sum(jnp.mean(err * err, axis=-1, keepdims=True), axis=0, keepdims=True)
        _accumulate(loss_ref, first, jnp.broadcast_to(part, (1, 128)))
        dyv = err * (1.0 / D)
        dyg = dyv * gv
        dx = r * (dyg - xh * jnp.mean(dyg * xh, axis=-1, keepdims=True))
        dx_ref[...] = dx
        dxb_ref[...] = dx.astype(BF)
        _accumulate(dg_ref, first, _colsum(dyv * xh))

    row = pl.BlockSpec((tm, D), lambda i: (i, 0))
    vec = pl.BlockSpec((1, D), lambda i: (0, 0))
    return pl.pallas_call(
        body, name=name, grid=(n // tm,), in_specs=[row, vec, row],
        out_specs=[pl.BlockSpec((1, 128), lambda i: (0, 0)), row, row, vec],
        out_shape=[jax.ShapeDtypeStruct((1, 128), F32), jax.ShapeDtypeStruct((n, D), F32),
                   jax.ShapeDtypeStruct((n, D), BF), jax.ShapeDtypeStruct((1, D), F32)],
        compiler_params=_cp(("arbitrary",)))(h, g, tgt)


def _prev_rows(x, halo_ref, lanes, scale, row):
    h7 = halo_ref[7:8, lanes] * scale
    h6 = halo_ref[6:7, lanes] * scale
    p1 = jnp.where(row == 0, h7, pltpu.roll(x, 1, 0))
    p2 = jnp.where(row == 0, h6, jnp.where(row == 1, h7, pltpu.roll(x, 2, 0)))
    return p1, p2


def _halo_maps(tm, n_rows):
    r8 = tm // 8
    last = n_rows // 8 - 1
    prev = lambda i: jnp.maximum(i * r8 - 1, 0)
    nxt = lambda i: jnp.minimum((i + 1) * r8, last)
    return prev, nxt


def _lane_blocks(width):
    return [slice(lo, min(lo + 128, width)) for lo in range(0, width, 128)]


def _conv_taps(w_ref, b_ref, g, lanes):
    return w_ref[g, 0:1, lanes], w_ref[g, 1:2, lanes], w_ref[g, 2:3, lanes], b_ref[g, :, lanes]


def _conv_tile(x, prev1, prev2, taps, row):
    w0, w1, w2, b = taps
    r1 = pltpu.roll(x, 1, 0)
    r2 = pltpu.roll(x, 2, 0)
    x1 = jnp.where(row == 0, prev1, r1)
    x2 = jnp.where(row < 2, prev2, r2)
    return b + w0 * x2 + w1 * x1 + w2 * x, x1, x2, r1, r2


def _halo16_maps(tm, n_rows):
    r16 = tm // 16
    last = n_rows // 16 - 1
    return (lambda i: jnp.maximum(i * r16 - 1, 0)), (lambda i: jnp.minimum((i + 1) * r16, last))


def _ffn_conv_fwd(up, cw, cb, seq, name):
    n = up.shape[2]
    tm = min(256, seq)
    prev, _ = _halo16_maps(tm, n)

    def body(u_ref, h_ref, w_ref, b_ref, o_ref):
        i = pl.program_id(1)
        scale = jnp.where(lax.rem(i * tm, seq) == 0, 0.0, 1.0)
        for lanes in _lane_blocks(FSH):
            lw = lanes.stop - lanes.start
            row = lax.broadcasted_iota(jnp.int32, (8, lw), 0)
            taps = [_conv_taps(w_ref, b_ref, g, lanes) for g in range(2)]

            def tile(xs, carry):
                hc, nxt = [], []
                for g in range(2):
                    conv, _, _, r1, r2 = _conv_tile(xs[g], carry[2 * g], carry[2 * g + 1], taps[g], row)
                    hc.append(conv)
                    nxt += [r1, r2]
                return hc[0] * _sigmoid(hc[0]) * hc[1], tuple(nxt)

            carry = []
            for g in range(2):
                halo = h_ref[g, :, lanes].astype(F32)[8:] * scale
                carry += [pltpu.roll(halo, 1, 0), pltpu.roll(halo, 2, 0)]
            carry = tuple(carry)
            for m in range(tm // 16):
                x16 = [u_ref[g, m * 16:m * 16 + 16, lanes].astype(F32) for g in range(2)]
                a, carry = tile([x[:8] for x in x16], carry)
                b, carry = tile([x[8:] for x in x16], carry)
                o_ref[m * 16:m * 16 + 16, lanes] = jnp.concatenate([a, b], axis=0).astype(BF)

    return pl.pallas_call(
        body, name=name, grid=(4, n // tm),
        in_specs=[pl.BlockSpec((2, None, tm, FSH), lambda j, i: (0, j, i, 0)),
                  pl.BlockSpec((2, None, 16, FSH), lambda j, i: (0, j, prev(i), 0)),
                  pl.BlockSpec((2, None, 3, FSH), lambda j, i: (0, j, 0, 0)),
                  pl.BlockSpec((2, None, 1, FSH), lambda j, i: (0, j, 0, 0))],
        out_specs=pl.BlockSpec((None, tm, FSH), lambda j, i: (j, i, 0)),
        out_shape=jax.ShapeDtypeStruct((4, n, FSH), BF), compiler_params=_cp(("parallel", "parallel")))(up, up, cw, cb)


def _ffn_conv_bwd(up, dact, cw, cb, seq, name):
    n = up.shape[2]
    tm = min(256, seq)
    prev, nxt = _halo16_maps(tm, n)

    def body(u_ref, up_ref, un_ref, da_ref, dn_ref, w_ref, b_ref, du_ref, dw_ref, db_ref):
        i = pl.program_id(1)
        sp = jnp.where(lax.rem(i * tm, seq) == 0, 0.0, 1.0)
        sn = jnp.where(lax.rem((i + 1) * tm, seq) == 0, 0.0, 1.0)
        first = i == 0
        for lanes in _lane_blocks(FSH):
            lw = lanes.stop - lanes.start
            row = lax.broadcasted_iota(jnp.int32, (8, lw), 0)
            taps = [_conv_taps(w_ref, b_ref, g, lanes) for g in range(2)]

            def grads(xs, da, xroll, sums, accumulate):
                hc, shifted, rolls = [], [], []
                for g in range(2):
                    conv, x1, x2, r1, r2 = _conv_tile(xs[g], xroll[g][0], xroll[g][1], taps[g], row)
                    hc.append(conv)
                    shifted.append((x2, x1, xs[g]))
                    rolls.append((r1, r2))
                s = _sigmoid(hc[0])
                dhc = (da * hc[1] * (s * (1.0 + hc[0] * (1.0 - s))), da * (hc[0] * s))
                if accumulate:
                    sums = [tuple(acc + dhc[g] * xk for acc, xk in zip(sums[g][:3], shifted[g])) + (sums[g][3] + dhc[g],)
                            for g in range(2)]
                return [(dhc[g], pltpu.roll(dhc[g], 7, 0), pltpu.roll(dhc[g], 6, 0)) for g in range(2)], rolls, sums

            def dup_tile(g, cur, after):
                w0, w1, w2, _ = taps[g]
                return (w2 * cur[0] + w1 * jnp.where(row == 7, after[1], cur[1])
                        + w0 * jnp.where(row >= 6, after[2], cur[2]))

            def emit(m, pair_, after):
                for g in range(2):
                    du_ref[g, m * 16:m * 16 + 16, lanes] = jnp.concatenate(
                        [dup_tile(g, pair_[0][g], pair_[1][g]), dup_tile(g, pair_[1][g], after[g])],
                        axis=0).astype(BF)

            zero = jnp.zeros((8, lw), F32)
            xroll = []
            for g in range(2):
                halo = up_ref[g, :, lanes].astype(F32)[8:] * sp
                xroll.append((pltpu.roll(halo, 1, 0), pltpu.roll(halo, 2, 0)))
            sums = [(zero,) * 4, (zero,) * 4]
            held = None
            for m in range(tm // 16):
                x16 = [u_ref[g, m * 16:m * 16 + 16, lanes].astype(F32) for g in range(2)]
                d16 = da_ref[m * 16:m * 16 + 16, lanes].astype(F32)
                ta, xroll, sums = grads([x[:8] for x in x16], d16[:8], xroll, sums, True)
                tb, xroll, sums = grads([x[8:] for x in x16], d16[8:], xroll, sums, True)
                if held is not None:
                    emit(m - 1, held, ta)
                held = (ta, tb)
            xn = [un_ref[g, :, lanes].astype(F32)[:8] for g in range(2)]
            tn_, _, _ = grads(xn, dn_ref[:, lanes].astype(F32)[:8] * sn, xroll, sums, False)
            emit(tm // 16 - 1, held, tn_)
            for g in range(2):
                for k in range(3):
                    _accumulate(dw_ref.at[g, k:k + 1, lanes], first, _colsum(sums[g][k]))
                _accumulate(db_ref.at[g, :, lanes], first, _colsum(sums[g][3]))

    return pl.pallas_call(
        body, name=name, grid=(4, n // tm),
        in_specs=[pl.BlockSpec((2, None, tm, FSH), lambda j, i: (0, j, i, 0)),
                  pl.BlockSpec((2, None, 16, FSH), lambda j, i: (0, j, prev(i), 0)),
                  pl.BlockSpec((2, None, 16, FSH), lambda j, i: (0, j, nxt(i), 0)),
                  pl.BlockSpec((None, tm, FSH), lambda j, i: (j, i, 0)),
                  pl.BlockSpec((None, 16, FSH), lambda j, i: (j, nxt(i), 0)),
                  pl.BlockSpec((2, None, 3, FSH), lambda j, i: (0, j, 0, 0)),
                  pl.BlockSpec((2, None, 1, FSH), lambda j, i: (0, j, 0, 0))],
        out_specs=[pl.BlockSpec((2, None, tm, FSH), lambda j, i: (0, j, i, 0)),
                   pl.BlockSpec((2, None, 3, FSH), lambda j, i: (0, j, 0, 0)),
                   pl.BlockSpec((2, None, 1, FSH), lambda j, i: (0, j, 0, 0))],
        out_shape=[jax.ShapeDtypeStruct((2, 4, n, FSH), BF), jax.ShapeDtypeStruct((2, 4, 3, FSH), F32),
                   jax.ShapeDtypeStruct((2, 4, 1, FSH), F32)],
        compiler_params=_cp(("parallel", "arbitrary")))(up, up, up, dact, dact, cw, cb)


def _shortconv_fwd(p, cw, cb, seq, name):
    n = p.shape[0]
    tm = min(256, seq)
    prev, _ = _halo_maps(tm, n)

    def body(p_ref, h_ref, w_ref, b_ref, o_ref):
        i = pl.program_id(1)
        scale = jnp.where(lax.rem(i * tm, seq) == 0, 0.0, 1.0)
        q = p_ref[:, FB:2 * FB] * p_ref[:, 2 * FB:]
        row = lax.broadcasted_iota(jnp.int32, q.shape, 0)
        h7 = h_ref[7:8, FB:2 * FB] * h_ref[7:8, 2 * FB:] * scale
        h6 = h_ref[6:7, FB:2 * FB] * h_ref[6:7, 2 * FB:] * scale
        p1 = jnp.where(row == 0, h7, pltpu.roll(q, 1, 0))
        p2 = jnp.where(row == 0, h6, jnp.where(row == 1, h7, pltpu.roll(q, 2, 0)))
        conv = b_ref[...] + w_ref[0:1, :] * p2 + w_ref[1:2, :] * p1 + w_ref[2:3, :] * q
        o_ref[...] = (p_ref[:, :FB] * conv).astype(BF)

    return pl.pallas_call(
        body, name=name, grid=(D // FB, n // tm),
        in_specs=[pl.BlockSpec((tm, 3 * FB), lambda j, i: (i, j)),
                  pl.BlockSpec((8, 3 * FB), lambda j, i: (prev(i), j)),
                  pl.BlockSpec((3, FB), lambda j, i: (0, j)),
                  pl.BlockSpec((1, FB), lambda j, i: (0, j))],
        out_specs=pl.BlockSpec((tm, FB), lambda j, i: (i, j)),
        out_shape=jax.ShapeDtypeStruct((n, D), BF), compiler_params=_cp(("parallel", "parallel")))(p, p, cw, cb)


def _shortconv_bwd(p, dmix, cw, cb, seq, name):
    n = p.shape[0]
    tm = min(256, seq)
    ext = tm + 16
    prev, nxt = _halo_maps(tm, n)

    def body(p_ref, pp_ref, pn_ref, dm_ref, dn_ref, w_ref, b_ref, dp_ref, dw_ref, db_ref, qx, cx):
        i = pl.program_id(1)
        sp = jnp.where(lax.rem(i * tm, seq) == 0, 0.0, 1.0)
        sn = jnp.where(lax.rem((i + 1) * tm, seq) == 0, 0.0, 1.0)
        bg, cg, hx = p_ref[:, :FB], p_ref[:, FB:2 * FB], p_ref[:, 2 * FB:]
        dm = dm_ref[...]
        qx[0:8, :] = pp_ref[:, FB:2 * FB] * pp_ref[:, 2 * FB:] * sp
        qx[8:8 + tm, :] = cg * hx
        qx[8 + tm:, :] = jnp.zeros((8, FB), F32)
        cx[0:8, :] = jnp.zeros((8, FB), F32)
        cx[8:8 + tm, :] = dm * bg
        cx[8 + tm:, :] = dn_ref[...] * pn_ref[:, :FB] * sn
        q0 = qx[...]
        q1 = pltpu.roll(q0, 1, 0)
        q2 = pltpu.roll(q0, 2, 0)
        main = slice(8, 8 + tm)
        conv = b_ref[...] + w_ref[0:1, :] * q2[main] + w_ref[1:2, :] * q1[main] + w_ref[2:3, :] * q0[main]
        dc = cx[...]
        dq = (w_ref[2:3, :] * dc + w_ref[1:2, :] * pltpu.roll(dc, ext - 1, 0)
              + w_ref[0:1, :] * pltpu.roll(dc, ext - 2, 0))[main]
        dp_ref[:, :FB] = (dm * conv).astype(BF)
        dp_ref[:, FB:2 * FB] = (dq * hx).astype(BF)
        dp_ref[:, 2 * FB:] = (dq * cg).astype(BF)
        first = i == 0
        dcm = dc[main]
        _accumulate(dw_ref.at[0:1, :], first, _colsum(dcm * q2[main]))
        _accumulate(dw_ref.at[1:2, :], first, _colsum(dcm * q1[main]))
        _accumulate(dw_ref.at[2:3, :], first, _colsum(dcm * q0[main]))
        _accumulate(db_ref, first, _colsum(dcm))

    return pl.pallas_call(
        body, name=name, grid=(D // FB, n // tm),
        in_specs=[pl.BlockSpec((tm, 3 * FB), lambda j, i: (i, j)),
                  pl.BlockSpec((8, 3 * FB), lambda j, i: (prev(i), j)),
                  pl.BlockSpec((8, 3 * FB), lambda j, i: (nxt(i), j)),
                  pl.BlockSpec((tm, FB), lambda j, i: (i, j)),
                  pl.BlockSpec((8, FB), lambda j, i: (nxt(i), j)),
                  pl.BlockSpec((3, FB), lambda j, i: (0, j)),
                  pl.BlockSpec((1, FB), lambda j, i: (0, j))],
        out_specs=[pl.BlockSpec((tm, 3 * FB), lambda j, i: (i, j)),
                   pl.BlockSpec((3, FB), lambda j, i: (0, j)),
                   pl.BlockSpec((1, FB), lambda j, i: (0, j))],
        out_shape=[jax.ShapeDtypeStruct((n, 3 * D), BF), jax.ShapeDtypeStruct((3, D), F32),
                   jax.ShapeDtypeStruct((1, D), F32)],
        scratch_shapes=[pltpu.VMEM((ext, FB), F32), pltpu.VMEM((ext, FB), F32)],
        compiler_params=_cp(("parallel", "arbitrary")))(p, p, p, dmix, dmix, cw, cb)


def _gmlp_fwd(uv, wm, bst, gv, seq, name):
    n = uv.shape[0]
    tm = min(256, seq)

    def body(x_ref, w_ref, b_ref, g_ref, o_ref):
        ge_v = _gelu(x_ref[:, GM_W:])
        r = lax.rsqrt(jnp.mean(ge_v * ge_v, axis=-1, keepdims=True) + EPS)
        vn = (ge_v * r * g_ref[...]).astype(BF)
        for c in range(tm // CHUNK):
            rows = slice(c * CHUNK, (c + 1) * CHUNK)
            for h in range(GM_HEADS):
                cols = slice(h * CHUNK, (h + 1) * CHUNK)
                gate = jnp.dot(w_ref[h], vn[rows, cols], preferred_element_type=F32) + b_ref[:, h:h + 1]
                o_ref[rows, cols] = (_gelu(x_ref[rows, cols]) * gate).astype(BF)

    return pl.pallas_call(
        body, name=name, grid=(n // tm,),
        in_specs=[pl.BlockSpec((tm, 2 * GM_W), lambda i: (i, 0)),
                  pl.BlockSpec((GM_HEADS, CHUNK, CHUNK), lambda i: (0, 0, 0)),
                  pl.BlockSpec((CHUNK, GM_HEADS), lambda i: (0, 0)),
                  pl.BlockSpec((1, GM_W), lambda i: (0, 0))],
        out_specs=pl.BlockSpec((tm, GM_W), lambda i: (i, 0)),
        out_shape=jax.ShapeDtypeStruct((n, GM_W), BF), compiler_params=_cp(("parallel",)))(uv, wm, bst, gv)


def _gmlp_bwd(uv, dout, wm, wmt, bst, gv, seq, name):
    n = uv.shape[0]
    tm = min(256, seq)

    def body(x_ref, do_ref, w_ref, wt_ref, b_ref, g_ref, dx_ref, dw_ref, db_ref, dg_ref, dvn_scr):
        first = pl.program_id(0) == 0
        ge_v = _gelu(x_ref[:, GM_W:])
        r = lax.rsqrt(jnp.mean(ge_v * ge_v, axis=-1, keepdims=True) + EPS)
        vh = ge_v * r
        vn = (vh * g_ref[...]).astype(BF)
        tril = (lax.broadcasted_iota(jnp.int32, (CHUNK, CHUNK), 0)
                >= lax.broadcasted_iota(jnp.int32, (CHUNK, CHUNK), 1))
        for h in range(GM_HEADS):
            cols = slice(h * CHUNK, (h + 1) * CHUNK)
            dw = jnp.zeros((CHUNK, CHUNK), F32)
            dbs = jnp.zeros((CHUNK, 1), F32)
            for c in range(tm // CHUNK):
                rows = slice(c * CHUNK, (c + 1) * CHUNK)
                blk = vn[rows, cols]
                gate = jnp.dot(w_ref[h], blk, preferred_element_type=F32) + b_ref[:, h:h + 1]
                xu = x_ref[rows, cols]
                do = do_ref[rows, cols]
                dx_ref[rows, cols] = (do * gate * _gelu_grad(xu)).astype(BF)
                dgate = do * _gelu(xu)
                dgb = dgate.astype(BF)
                dw = dw + lax.dot_general(dgb, blk, _DIMS['nt'], preferred_element_type=F32)
                dbs = dbs + jnp.sum(dgate, axis=1, keepdims=True)
                dvn_scr[rows, cols] = jnp.dot(wt_ref[h], dgb, preferred_element_type=F32)
            _accumulate(dw_ref.at[h], first, jnp.where(tril, dw, 0.0))
            _accumulate(db_ref.at[h], first, dbs)
        dvn = dvn_scr[...]
        _accumulate(dg_ref, first, _colsum(dvn * vh))
        dvh = dvn * g_ref[...]
        dv = r * (dvh - vh * jnp.mean(dvh * vh, axis=-1, keepdims=True))
        dx_ref[:, GM_W:] = (dv * _gelu_grad(x_ref[:, GM_W:])).astype(BF)

    full3 = pl.BlockSpec((GM_HEADS, CHUNK, CHUNK), lambda i: (0, 0, 0))
    return pl.pallas_call(
        body, name=name, grid=(n // tm,),
        in_specs=[pl.BlockSpec((tm, 2 * GM_W), lambda i: (i, 0)), pl.BlockSpec((tm, GM_W), lambda i: (i, 0)),
                  full3, full3, pl.BlockSpec((CHUNK, GM_HEADS), lambda i: (0, 0)),
                  pl.BlockSpec((1, GM_W), lambda i: (0, 0))],
        out_specs=[pl.BlockSpec((tm, 2 * GM_W), lambda i: (i, 0)), full3,
                   pl.BlockSpec((GM_HEADS, CHUNK, 1), lambda i: (0, 0, 0)),
                   pl.BlockSpec((1, GM_W), lambda i: (0, 0))],
        out_shape=[jax.ShapeDtypeStruct((n, 2 * GM_W), BF), jax.ShapeDtypeStruct((GM_HEADS, CHUNK, CHUNK), F32),
                   jax.ShapeDtypeStruct((GM_HEADS, CHUNK, 1), F32), jax.ShapeDtypeStruct((1, GM_W), F32)],
        scratch_shapes=[pltpu.VMEM((tm, GM_W), F32)],
        compiler_params=_cp(("arbitrary",)))(uv, dout, wm, wmt, bst, gv)


def _s5_disc(lam_re, lam_im, log_dt, b_re, b_im):
    lr = jnp.minimum(lam_re, LAM_MAX)
    li = lam_im
    dt = jnp.exp(log_dt)
    mag = jnp.exp(lr * dt)
    ab_re = mag * jnp.cos(li * dt)
    ab_im = mag * jnp.sin(li * dt)
    den = lr * lr + li * li
    nr = ab_re - 1.0
    ni = ab_im
    z_re = (nr * lr + ni * li) / den
    z_im = (ni * lr - nr * li) / den
    return ab_re, ab_im, z_re * b_re - z_im * b_im, z_re * b_im + z_im * b_re


def _s5_disc_fwd(args, name):
    shp = jax.ShapeDtypeStruct(args[0].shape, F32)

    def body(*refs):
        outs = _s5_disc(*[r[...] for r in refs[:5]])
        for o_ref, o in zip(refs[5:], outs):
            o_ref[...] = o

    return pl.pallas_call(body, name=name, out_shape=[shp] * 4)(*args)


def _s5_disc_bwd(args, cts, name):
    shp = jax.ShapeDtypeStruct(args[0].shape, F32)

    def body(*refs):
        _, vjp = jax.vjp(_s5_disc, *[r[...] for r in refs[:5]])
        grads = vjp(tuple(r[...] for r in refs[5:9]))
        for o_ref, o in zip(refs[9:], grads):
            o_ref[...] = o

    return pl.pallas_call(body, name=name, out_shape=[shp] * 5)(*args, *cts)


def _cmul(a, b):
    return a[0] * b[0] - a[1] * b[1], a[0] * b[1] + a[1] * b[0]


def _scan_tables(ar, ai, reverse):
    if reverse:
        ai = -ai
    a1 = (ar, ai)
    a2 = _cmul(a1, a1)
    a3 = _cmul(a2, a1)
    a4 = _cmul(a2, a2)
    powers = [a1, a2, a3, a4, _cmul(a4, a1), _cmul(a4, a2), _cmul(a4, a3), _cmul(a4, a4)]
    row = lax.broadcasted_iota(jnp.int32, (8, NST), 0)
    zero = jnp.zeros((8, NST), F32)
    pr, pi = zero, zero
    for r in range(8):
        pw = powers[7 - r] if reverse else powers[r]
        pr = jnp.where(row == r, pw[0], pr)
        pi = jnp.where(row == r, pw[1], pi)
    levels = []
    for d, pw in ((1, a1), (2, a2), (4, a4)):
        ok = (row <= 7 - d) if reverse else (row >= d)
        levels.append((d, jnp.where(ok, pw[0], zero), jnp.where(ok, pw[1], zero)))
    return (pr, pi), levels


def _scan_block(src, dst, car, tables, n_tiles, reverse):
    (pr, pi), levels = tables
    row = lax.broadcasted_iota(jnp.int32, (8, NST), 0)
    out_row = 0 if reverse else 7

    def step(t, carry):
        cr, ci = carry
        tile = (n_tiles - 1 - t) if reverse else t
        rows = pl.ds(pl.multiple_of(tile * 8, 8), 8)
        xr = src[rows, 0:NST]
        xi = src[rows, NST:2 * NST]
        for d, dr, di in levels:
            shift = 8 - d if reverse else d
            rr = pltpu.roll(xr, shift, 0)
            ri = pltpu.roll(xi, shift, 0)
            xr, xi = xr + dr * rr - di * ri, xi + dr * ri + di * rr
        hr = xr + pr * cr - pi * ci
        hi = xi + pr * ci + pi * cr
        dst[rows, 0:NST] = hr
        dst[rows, NST:2 * NST] = hi
        return (_colsum(jnp.where(row == out_row, hr, 0.0)), _colsum(jnp.where(row == out_row, hi, 0.0)))

    cr, ci = lax.fori_loop(0, n_tiles, step, (car[0:1, 0:NST], car[0:1, NST:2 * NST]))
    car[0:1, 0:NST] = cr
    car[0:1, NST:2 * NST] = ci


def _s5_fwd(u, ab, bbt, cmat, dvec, wglu, bglu, seq, name):
    n = u.shape[0]
    tm = min(256, seq)

    def body(u_ref, ab_ref, bb_ref, c_ref, d_ref, w_ref, b_ref, h_ref, o_ref, xs, car):
        i = pl.program_id(0)

        @pl.when(lax.rem(i * tm, seq) == 0)
        def _():
            car[...] = jnp.zeros(car.shape, F32)

        uv = u_ref[...]
        xs[...] = jnp.dot(uv.astype(BF), bb_ref[...], preferred_element_type=F32)
        tables = _scan_tables(ab_ref[0:1, 0:NST], ab_ref[0:1, NST:2 * NST], False)
        _scan_block(xs, h_ref, car, tables, tm // 8, False)
        y = jnp.dot(h_ref[...].astype(BF), c_ref[...], preferred_element_type=F32) + d_ref[...] * uv
        g1 = _gelu(y)
        z = jnp.dot(g1.astype(BF), w_ref[...], preferred_element_type=F32) + b_ref[...]
        o_ref[...] = (g1 * _sigmoid(z)).astype(BF)

    const = lambda shape: pl.BlockSpec(shape, lambda i: (0, 0))
    return pl.pallas_call(
        body, name=name, grid=(n // tm,),
        in_specs=[pl.BlockSpec((tm, SSM_W), lambda i: (i, 0)), const((1, 2 * NST)), const((SSM_W, 2 * NST)),
                  const((2 * NST, SSM_W)), const((1, SSM_W)), const((SSM_W, SSM_W)), const((1, SSM_W))],
        out_specs=[pl.BlockSpec((tm, 2 * NST), lambda i: (i, 0)), pl.BlockSpec((tm, SSM_W), lambda i: (i, 0))],
        out_shape=[jax.ShapeDtypeStruct((n, 2 * NST), F32), jax.ShapeDtypeStruct((n, SSM_W), BF)],
        scratch_shapes=[pltpu.VMEM((tm, 2 * NST), F32), pltpu.VMEM((8, 2 * NST), F32)],
        compiler_params=_cp(("arbitrary",)))(u, ab, bbt, cmat, dvec, wglu, bglu)


def _s5_bwd(da, u, hst, ab, bbt, cmat, dvec, wglu, bglu, seq, name):
    n = u.shape[0]
    tm = min(256, seq)
    nb = n // tm
    blk = lambda r: nb - 1 - r
    prev, _ = _halo_maps(tm, n)

    def body(da_ref, u_ref, h_ref, hp_ref, ab_ref, bb_ref, c_ref, d_ref, w_ref, b_ref,
             du_ref, dw_ref, dbg_ref, dd_ref, dc_ref, dbb_ref, dab_ref, gs, car):
        r = pl.program_id(0)
        i = blk(r)
        first = r == 0

        @pl.when(lax.rem((i + 1) * tm, seq) == 0)
        def _():
            car[...] = jnp.zeros(car.shape, F32)

        uv = u_ref[...]
        dav = da_ref[...]
        hb = h_ref[...]
        hb16 = hb.astype(BF)
        dvv = d_ref[...]
        y = jnp.dot(hb16, c_ref[...], preferred_element_type=F32) + dvv * uv
        g1 = _gelu(y)
        g16 = g1.astype(BF)
        s = _sigmoid(jnp.dot(g16, w_ref[...], preferred_element_type=F32) + b_ref[...])
        dz = dav * g1 * s * (1.0 - s)
        dz16 = dz.astype(BF)
        dg1 = dav * s + lax.dot_general(dz16, w_ref[...], _DIMS['nt'], preferred_element_type=F32)
        _accumulate(dw_ref, first, lax.dot_general(g16, dz16, _DIMS['tn'], preferred_element_type=F32))
        _accumulate(dbg_ref, first, _colsum(dz))
        dy = dg1 * _gelu_grad(y)
        dy16 = dy.astype(BF)
        _accumulate(dd_ref, first, _colsum(dy * uv))
        _accumulate(dc_ref, first, lax.dot_general(hb16, dy16, _DIMS['tn'], preferred_element_type=F32))
        gs[...] = lax.dot_general(dy16, c_ref[...], _DIMS['nt'], preferred_element_type=F32)
        tables = _scan_tables(ab_ref[0:1, 0:NST], ab_ref[0:1, NST:2 * NST], True)
        _scan_block(gs, gs, car, tables, tm // 8, True)
        g = gs[...]
        g16b = g.astype(BF)
        sp = jnp.where(lax.rem(i * tm, seq) == 0, 0.0, 1.0)
        row = lax.broadcasted_iota(jnp.int32, hb.shape, 0)
        hprev = jnp.where(row == 0, hp_ref[7:8, :] * sp, pltpu.roll(hb, 1, 0))
        gr, gi = g[:, :NST], g[:, NST:]
        hr, hi = hprev[:, :NST], hprev[:, NST:]
        _accumulate(dab_ref.at[:, 0:NST], first, _colsum(gr * hr + gi * hi))
        _accumulate(dab_ref.at[:, NST:2 * NST], first, _colsum(gi * hr - gr * hi))
        _accumulate(dbb_ref, first, lax.dot_general(uv.astype(BF), g16b, _DIMS['tn'], preferred_element_type=F32))
        du = dy * dvv + lax.dot_general(g16b, bb_ref[...], _DIMS['nt'], preferred_element_type=F32)
        du_ref[...] = du.astype(BF)

    const = lambda shape: pl.BlockSpec(shape, lambda r: (0, 0))
    rowspec = lambda w: pl.BlockSpec((tm, w), lambda r: (blk(r), 0))
    return pl.pallas_call(
        body, name=name, grid=(nb,),
        in_specs=[rowspec(SSM_W), rowspec(SSM_W), rowspec(2 * NST),
                  pl.BlockSpec((8, 2 * NST), lambda r: (prev(blk(r)), 0)),
                  const((1, 2 * NST)), const((SSM_W, 2 * NST)), const((2 * NST, SSM_W)), const((1, SSM_W)),
                  const((SSM_W, SSM_W)), const((1, SSM_W))],
        out_specs=[rowspec(SSM_W), const((SSM_W, SSM_W)), const((1, SSM_W)), const((1, SSM_W)),
                   const((2 * NST, SSM_W)), const((SSM_W, 2 * NST)), const((1, 2 * NST))],
        out_shape=[jax.ShapeDtypeStruct((n, SSM_W), BF), jax.ShapeDtypeStruct((SSM_W, SSM_W), F32),
                   jax.ShapeDtypeStruct((1, SSM_W), F32), jax.ShapeDtypeStruct((1, SSM_W), F32),
                   jax.ShapeDtypeStruct((2 * NST, SSM_W), F32), jax.ShapeDtypeStruct((SSM_W, 2 * NST), F32),
                   jax.ShapeDtypeStruct((1, 2 * NST), F32)],
        scratch_shapes=[pltpu.VMEM((tm, 2 * NST), F32), pltpu.VMEM((8, 2 * NST), F32)],
        compiler_params=_cp(("arbitrary",)))(da, u, hst, hst, ab, bbt, cmat, dvec, wglu, bglu)


def _s5_rows(lam_re, lam_im, log_dt, b_re, b_im):
    rep = lambda a: jnp.broadcast_to(a[:, None, :], (SSM_G, SSM_H, SSM_P)).reshape(SSM_W, SSM_P)
    dt = jnp.broadcast_to(log_dt[:, None, None], (SSM_G, SSM_H, SSM_P)).reshape(SSM_W, SSM_P)
    tr = lambda b: b.transpose(0, 2, 1).reshape(SSM_W, SSM_P)
    return rep(lam_re), rep(lam_im), dt, tr(b_re), tr(b_im)


def _block_diag(rows_gp, inner):
    eye = jnp.eye(SSM_G, dtype=rows_gp.dtype)
    return (rows_gp[:, :, None, :] * eye[:, None, :, None]).reshape(SSM_G * inner, SSM_G * SSM_P)


def _diag_blocks(mat, inner):
    m4 = mat.reshape(SSM_G, inner, SSM_G, SSM_P)
    return jnp.stack([m4[g, :, g, :] for g in range(SSM_G)])


def _interleave(w, parts):
    lead = w.shape[:-1]
    nb = w.shape[-1] // (parts * FB)
    return jnp.swapaxes(w.reshape(lead + (parts, nb, FB)), -3, -2).reshape(w.shape)


def _deinterleave(w, parts):
    lead = w.shape[:-1]
    nb = w.shape[-1] // (parts * FB)
    return jnp.swapaxes(w.reshape(lead + (nb, parts, FB)), -3, -2).reshape(w.shape)


def _ffn_fwd(h, g, w_up, w_down, cw, cb, seq, tag):
    n = h.shape[0]
    tm = min(1024, n)
    ni = n // tm
    f = _rmsnorm_fwd(h, g, f"{tag}_norm")
    up = _matmul_spec(
        f, w_up, 'nn', (NDEV, ni, 1),
        pl.BlockSpec((tm, D), lambda s, i, k: (i, 0)),
        pl.BlockSpec((D, FSH), lambda s, i, k: (s, 0)),
        pl.BlockSpec((tm, FSH), lambda s, i, k: (s * ni + i, 0)), (NDEV * n, FSH), f"{tag}_up", out_dtype=BF)
    up = up.reshape(2, 4, n, FSH)
    act = _ffn_conv_fwd(up, cw, cb, seq, f"{tag}_conv")
    out = _matmul_shards(act, w_down.reshape(4, FSH, D), 'nn', 1024, 512, f"{tag}_down", resid=h)
    return out, (f, up, act)


def _ffn_bwd(dh, dhb, h, g, w_up, w_down, cw, cb, saved, seq, tag):
    f, up, act = saved
    n = h.shape[0]
    tm = min(1024, n)
    ni = n // tm
    tk = min(4096, n)
    nk = n // tk
    dact = _matmul_spec(
        dhb, w_down, 'nt', (4, ni, 1),
        pl.BlockSpec((tm, D), lambda j, i, k: (i, 0)),
        pl.BlockSpec((FSH, D), lambda j, i, k: (j, 0)),
        pl.BlockSpec((tm, FSH), lambda j, i, k: (j * ni + i, 0)), (4 * n, FSH), f"{tag}_ddown_x", out_dtype=BF)
    tn = 512
    dw_down = _matmul_spec(
        act.reshape(4 * n, FSH), dhb, 'tn', (4, D // tn, nk),
        pl.BlockSpec((tk, FSH), lambda j, c, k: (j * nk + k, 0)),
        pl.BlockSpec((tk, tn), lambda j, c, k: (k, c)),
        pl.BlockSpec((FSH, tn), lambda j, c, k: (j, c)), (DFF, D), f"{tag}_ddown_w", out_dtype=BF)
    dup, dcw, dcb = _ffn_conv_bwd(up, dact.reshape(4, n, FSH), cw, cb, seq, f"{tag}_dconv")
    dup2 = dup.reshape(NDEV * n, FSH)
    df = _matmul_shards(dup.reshape(NDEV, n, FSH), w_up.reshape(NDEV, D, FSH), 'nt', 256, D, f"{tag}_dup_x")
    dw_up = _matmul_spec(
        f, dup2, 'tn', (NDEV, 1, nk),
        pl.BlockSpec((tk, D), lambda s, j, k: (k, 0)),
        pl.BlockSpec((tk, FSH), lambda s, j, k: (s * nk + k, 0)),
        pl.BlockSpec((D, FSH), lambda s, j, k: (s, 0)), (NDEV * D, FSH), f"{tag}_dup_w", out_dtype=BF)
    dh_in, dhb_in, dg = _rmsnorm_bwd(h, g, df, dh, f"{tag}_dnorm")
    grads = dict(g=dg, w_up=dw_up.reshape(NDEV, D, FSH), w_down=dw_down.reshape(NDEV, DFF // NDEV, D),
                 cw=dcw.reshape(NDEV, 3, FSH), cb=dcb.reshape(2 * DFF))
    return dh_in, dhb_in, grads


def _col_shards(w, width):
    return w.reshape(w.shape[0], NDEV, width).transpose(1, 0, 2)


def _local_step(x, tgt, w, gw, wait_ffn0, wait_rest, token, scatter, seq):
    bf = lambda a: a.astype(BF)
    row = lambda a: a.reshape(1, -1).astype(F32)
    w_ev = gw['ev_w_in'].transpose(1, 0, 2).reshape(D, 1792)
    w_ev_s5, w_ev_gm = w_ev[:, :SSM_W], w_ev[:, SSM_W:]
    w_evo = gw['ev_w_out'].reshape(D, D)
    f_cb = [w['ffn_conv_b'][l].reshape(2, 4, 1, FSH) for l in range(2)]
    tril = jnp.tril(jnp.ones((CHUNK, CHUNK), dtype=bool))
    gm_w = jnp.where(tril, w['gm_w_s'][0], 0.0)
    gm_wm, gm_wmt = bf(gm_w), bf(jnp.swapaxes(gm_w, 1, 2))
    gm_bt = w['gm_b_s'][0].T
    gm_gv = row(w['gm_v_g'][0])
    s5_in = _s5_rows(w['s5_lam_re'][0], w['s5_lam_im'][0], w['s5_log_dt'][0], w['s5_b_re'][0], w['s5_b_im'][0])
    ab_re, ab_im, bb_re, bb_im = _s5_disc_fwd(s5_in, "s5_disc")
    first_h = lambda a: a.reshape(SSM_G, SSM_H, SSM_P)[:, 0, :].reshape(1, NST)
    s5_ab = jnp.concatenate([first_h(ab_re), first_h(ab_im)], axis=1)
    to_gp = lambda a: a.reshape(SSM_G, SSM_H, SSM_P)
    s5_bbt = bf(jnp.concatenate([_block_diag(to_gp(bb_re), SSM_H), _block_diag(to_gp(bb_im), SSM_H)], axis=1))
    s5_cmat = bf(jnp.concatenate([_block_diag(w['s5_c_re'][0], SSM_H).T, -_block_diag(w['s5_c_im'][0], SSM_H).T],
                                 axis=0))
    s5_d, s5_bg, s5_wg = row(w['s5_d'][0]), row(w['s5_b_glu'][0]), gw['s5_w_glu'].reshape(SSM_W, SSM_W)
    g_mix = [row(w['mix_norm_g'][0]) + token[0:1, 0:1], row(w['mix_norm_g'][1])]
    g_ffn = [row(w['ffn_norm_g'][l]) for l in range(2)]
    g_fin = row(w['final_norm_g'])

    h0 = x
    y0 = _rmsnorm_fwd(h0, g_mix[0], "ev_norm")
    p_s5 = _matmul(y0, w_ev_s5, 'nn', 1024, 256, D, "ev_in_s5")
    p_gm = _matmul(y0, w_ev_gm, 'nn', 1024, 512, D, "ev_in_gm")
    hst, a_out = _s5_fwd(p_s5, s5_ab, s5_bbt, s5_cmat, s5_d, s5_wg, s5_bg, seq, "s5_fwd")
    b_out = _gmlp_fwd(p_gm, gm_wm, gm_bt, gm_gv, seq, "gmlp_fwd")
    mixcat = jnp.concatenate([a_out, b_out], axis=1)
    h1 = _matmul(mixcat, w_evo, 'nn', 1024, 512, D, "ev_out", resid=h0)
    g0 = wait_ffn0(mixcat)
    w_up0, w_dn0 = g0['ffn_w_up0'].reshape(NDEV * D, FSH), g0['ffn_w_down0'].reshape(DFF, D)
    f_cw0 = g0['ffn_conv_w0'].reshape(2, 4, 3, FSH)
    h2, ffn0 = _ffn_fwd(h1, g_ffn[0], w_up0, w_dn0, f_cw0, f_cb[0], seq, "ffn0")
    g1 = wait_rest(h2)
    w_od = _interleave(g1['od_w_in'].transpose(1, 0, 2).reshape(D, 3 * D), 3)
    w_odo = g1['od_w_out'].reshape(D, D)
    od_cw = g1['od_conv_w'].transpose(1, 0, 2).reshape(3, D)
    od_cb = g1['od_conv_b'].reshape(1, D)
    w_up1, w_dn1 = g1['ffn_w_up1'].reshape(NDEV * D, FSH), g1['ffn_w_down1'].reshape(DFF, D)
    f_cw1 = g1['ffn_conv_w1'].reshape(2, 4, 3, FSH)
    y1 = _rmsnorm_fwd(h2, g_mix[1], "od_norm")
    p_od = _matmul(y1, w_od, 'nn', 1024, 512, D, "od_in")
    mixin = _shortconv_fwd(p_od, od_cw, od_cb, seq, "od_conv")
    h3 = _matmul(mixin, w_odo, 'nn', 1024, 512, D, "od_out", resid=h2)
    h4, ffn1 = _ffn_fwd(h3, g_ffn[1], w_up1, w_dn1, f_cw1, f_cb[1], seq, "ffn1")
    loss, dh4, dh4b, dg_fin = _final_loss(h4, g_fin, tgt, "final_loss")

    dh3, dh3b, gf1 = _ffn_bwd(dh4, dh4b, h3, g_ffn[1], w_up1, w_dn1, f_cw1, f_cb[1], ffn1, seq, "ffn1")
    dmixin = _matmul(dh3b, w_odo, 'nt', 1024, 512, D, "od_dout_x")
    dw_odo = _matmul(mixin, dh3b, 'tn', D, 512, 4096, "od_dout_w", out_dtype=BF)
    dp_od, d_od_cw, d_od_cb = _shortconv_bwd(p_od, dmixin, od_cw, od_cb, seq, "od_dconv")
    dy1 = _matmul(dp_od, w_od, 'nt', 256, D, 3 * D, "od_din_x")
    dw_od = _matmul(y1, dp_od, 'tn', D, 512, 4096, "od_din_w", out_dtype=BF)
    sent = scatter("scatter_layer1", {
        'od_w_in': _col_shards(_deinterleave(dw_od, 3), 384), 'od_conv_w': _col_shards(d_od_cw, D // NDEV),
        'od_conv_b': d_od_cb.reshape(NDEV, 1, D // NDEV), 'od_w_out': dw_odo.reshape(NDEV, D // NDEV, D),
        'ffn_w_up1': gf1['w_up'], 'ffn_conv_w1': gf1['cw'], 'ffn_w_down1': gf1['w_down']})
    dh2, dh2b, dg_mix1 = _rmsnorm_bwd(h2, g_mix[1] + sent[0:1, 0:1], dy1, dh3, "od_dnorm")
    dh1, dh1b, gf0 = _ffn_bwd(dh2, dh2b, h1, g_ffn[0], w_up0, w_dn0, f_cw0, f_cb[0], ffn0, seq, "ffn0")
    dmix_a = _matmul(dh1b, w_evo[:SSM_W], 'nt', 1024, SSM_W, D, "ev_dout_xa")
    dmix_b = _matmul(dh1b, w_evo[SSM_W:], 'nt', 1024, GM_W, D, "ev_dout_xb")
    dw_evo = _matmul(mixcat, dh1b, 'tn', D, 512, 4096, "ev_dout_w", out_dtype=BF)
    sent = scatter("scatter_ffn0", {'ffn_w_up0': gf0['w_up'], 'ffn_conv_w0': gf0['cw'], 'ffn_w_down0': gf0['w_down'],
                                    'ev_w_out': dw_evo.reshape(NDEV, D // NDEV, D)})
    dp_s5, d_wg, d_bg, d_d, d_cmat, d_bbt, d_ab = _s5_bwd(dmix_a, p_s5, hst, s5_ab, s5_bbt, s5_cmat,
                                                           s5_d + sent[0:1, 0:1], s5_wg, s5_bg, seq, "s5_bwd")
    dp_gm, d_gmw, d_gmb, d_gmg = _gmlp_bwd(p_gm, dmix_b, gm_wm, gm_wmt, gm_bt, gm_gv, seq, "gmlp_bwd")
    dw_ev = jnp.concatenate([_matmul(y0, dp_s5, 'tn', D, SSM_W, 4096, "ev_din_wa", out_dtype=BF),
                             _matmul(y0, dp_gm, 'tn', D, 512, 4096, "ev_din_wb", out_dtype=BF)], axis=1)
    sent = scatter("scatter_even", {'ev_w_in': _col_shards(dw_ev, 224),
                                    's5_w_glu': d_wg.reshape(NDEV, SSM_W // NDEV, SSM_W)})
    dy0 = _matmul(dp_gm, w_ev_gm, 'nt', 512, D, 2 * GM_W, "ev_din_xb")
    dy0 = _matmul(dp_s5, w_ev_s5, 'nt', 1024, D, SSM_W, "ev_din_xa", resid=dy0)
    grad_x, _, dg_mix0 = _rmsnorm_bwd(h0, g_mix[0] + sent[0:1, 0:1], dy0, dh1, "ev_dnorm")

    put_h0 = lambda a: jnp.zeros((SSM_G, SSM_H, SSM_P), F32).at[:, 0, :].set(a.reshape(SSM_G, SSM_P)).reshape(
        SSM_W, SSM_P)
    ct = (put_h0(d_ab[:, :NST]), put_h0(d_ab[:, NST:]),
          _diag_blocks(d_bbt[:, :NST], SSM_H).reshape(SSM_W, SSM_P),
          _diag_blocks(d_bbt[:, NST:], SSM_H).reshape(SSM_W, SSM_P))
    d_lre, d_lim, d_ldt, d_bre, d_bim = _s5_disc_bwd(s5_in, ct, "s5_ddisc")
    over_h = lambda a: a.reshape(SSM_G, SSM_H, SSM_P).sum(axis=1)
    un_tr = lambda a: a.reshape(SSM_G, SSM_H, SSM_P).transpose(0, 2, 1)
    d_cre = _diag_blocks(d_cmat[:NST].T, SSM_H)
    d_cim = -_diag_blocks(d_cmat[NST:].T, SSM_H)

    repl = {
        'mix_norm_g': jnp.concatenate([dg_mix0, dg_mix1], axis=0),
        'ffn_norm_g': jnp.concatenate([gf0['g'], gf1['g']], axis=0),
        'final_norm_g': dg_fin.reshape(D),
        's5_lam_re': over_h(d_lre)[None], 's5_lam_im': over_h(d_lim)[None],
        's5_log_dt': over_h(d_ldt).sum(axis=1)[None],
        's5_b_re': un_tr(d_bre)[None], 's5_b_im': un_tr(d_bim)[None],
        's5_c_re': d_cre[None], 's5_c_im': d_cim[None],
        's5_d': d_d, 's5_b_glu': d_bg,
        'gm_w_s': d_gmw[None], 'gm_b_s': d_gmb.reshape(1, GM_HEADS, CHUNK), 'gm_v_g': d_gmg,
        'ffn_conv_b': jnp.stack([gf0['cb'], gf1['cb']]),
    }
    return loss, grad_x, repl


HBM_SPEC = pl.BlockSpec(memory_space=pltpu.HBM)


def _at_axis(ref, pos, index):
    return ref.at[(slice(None),) * pos + (index,)]


def _all_gather(shards, positions, name):
    n = len(shards)

    def body(*refs):
        xs, outs = refs[:n], refs[n:2 * n]
        send_sems, recv_sems, local_sems = refs[2 * n:]
        x, y, c = lax.axis_index("x"), lax.axis_index("y"), lax.axis_index("c")
        me, sibling = (x, y, c), (x, y, 1 - c)
        chips = [(1 - x, y), (x, 1 - y), (1 - x, 1 - y)]

        def block(p, dev):
            return _at_axis(outs[p], positions[p], 4 * dev[0] + 2 * dev[1] + dev[2])

        def copy(p, k, dev, to, src=None):
            return pltpu.make_async_remote_copy(
                src_ref=block(p, dev) if src is None else src, dst_ref=block(p, dev),
                send_sem=send_sems.at[p, k], recv_sem=recv_sems.at[p, k], device_id=to, device_id_type=MESH_T)

        mine = [pltpu.make_async_copy(xs[p], block(p, me), local_sems.at[p]) for p in range(n)]
        for cp in mine:
            cp.start()
        first = [copy(p, 0, me, sibling, src=xs[p]) for p in range(n)]
        first += [copy(p, 1 + j, me, (*chip, c), src=xs[p]) for j, chip in enumerate(chips) for p in range(n)]
        for cp in first:
            cp.start()
        passed = []
        for j, chip in enumerate(chips):
            for p in range(n):
                copy(p, 1 + j, (*chip, c), me).wait_recv()
                fwd = copy(p, 4 + j, (*chip, c), sibling)
                fwd.start()
                passed.append(fwd)
        for p in range(n):
            copy(p, 0, sibling, me).wait_recv()
        for j, chip in enumerate(chips):
            for p in range(n):
                copy(p, 4 + j, (*chip, 1 - c), me).wait_recv()
        for cp in first + passed:
            cp.wait_send()
        for cp in mine:
            cp.wait()

    out_shape = [jax.ShapeDtypeStruct(s.shape[:pos] + (NDEV,) + s.shape[pos:], s.dtype)
                 for s, pos in zip(shards, positions)]
    return pl.pallas_call(
        body, name=name, out_shape=out_shape, in_specs=[HBM_SPEC] * n, out_specs=[HBM_SPEC] * n,
        scratch_shapes=[pltpu.SemaphoreType.DMA((n, 7)), pltpu.SemaphoreType.DMA((n, 7)),
                        pltpu.SemaphoreType.DMA((n,))])(*shards)


def _other_devices(x, y, c):
    flip = lambda v, bit: 1 - v if bit else v
    return [(flip(x, k >> 2 & 1), flip(y, k >> 1 & 1), flip(c, k & 1)) for k in range(1, NDEV)]


SEM_SPEC = pl.BlockSpec(memory_space=pltpu.SEMAPHORE)
START_EFFECT = pltpu.SideEffectType.DATAFLOW_SIDE_EFFECTING


def _send_start(arrays, scatter, name):
    n = len(arrays)
    lands = [lax.empty((NDEV,) + (a.shape[1:] if scatter else a.shape), a.dtype) for a in arrays]

    def body(*refs):
        xs, ls = refs[:n], refs[n:2 * n]
        send_sems, recv_sems, own_sems, token = refs[2 * n], refs[2 * n + 1], refs[2 * n + 2], refs[4 * n + 3]
        x, y, c = lax.axis_index("x"), lax.axis_index("y"), lax.axis_index("c")
        me = 4 * x + 2 * y + c
        for k, peer in enumerate(_other_devices(x, y, c)):
            for p in range(n):
                src = xs[p].at[4 * peer[0] + 2 * peer[1] + peer[2]] if scatter else xs[p]
                pltpu.make_async_remote_copy(
                    src_ref=src, dst_ref=ls[p].at[me], send_sem=send_sems.at[p * (NDEV - 1) + k],
                    recv_sem=recv_sems.at[p * (NDEV - 1) + k], device_id=peer, device_id_type=MESH_T).start()
        for p in range(n):
            pltpu.make_async_copy(xs[p].at[me] if scatter else xs[p], ls[p].at[me], own_sems.at[p]).start()
        token[...] = jnp.zeros(token.shape, F32)

    sems = pltpu.SemaphoreType.DMA((n * (NDEV - 1),))
    out_shape = ([sems, sems, pltpu.SemaphoreType.DMA((n,))]
                 + [pltpu.HBM(a.shape, a.dtype) for a in list(arrays) + lands] + [jax.ShapeDtypeStruct((8, 128), F32)])
    res = pl.pallas_call(
        body, name=name, out_shape=out_shape, in_specs=[HBM_SPEC] * (2 * n),
        out_specs=[SEM_SPEC] * 3 + [HBM_SPEC] * (2 * n) + [pl.BlockSpec(memory_space=pltpu.VMEM)],
        input_output_aliases={i: 3 + i for i in range(2 * n)},
        compiler_params=pltpu.CompilerParams(has_side_effects=START_EFFECT))(
            *[pltpu.with_memory_space_constraint(a, pltpu.HBM) for a in list(arrays) + lands])
    return res[:3], res[3:3 + n], res[3 + n:3 + 2 * n], res[3 + 2 * n]


def _send_wait(started, scatter, after, name):
    sems, arrays, lands, _ = started
    n = len(arrays)

    def body(*refs):
        xs, ls = refs[:n], refs[n:2 * n]
        send, recv, own = refs[2 * n:2 * n + 3]
        x, y, c = lax.axis_index("x"), lax.axis_index("y"), lax.axis_index("c")
        me = 4 * x + 2 * y + c
        for p in range(n):
            pltpu.make_async_copy(xs[p].at[me] if scatter else xs[p], ls[p].at[me], own.at[p]).wait()
        for k, peer in enumerate(_other_devices(x, y, c)):
            slot = 4 * peer[0] + 2 * peer[1] + peer[2]
            for p in range(n):
                cp = pltpu.make_async_remote_copy(
                    src_ref=xs[p].at[slot] if scatter else xs[p], dst_ref=ls[p].at[slot],
                    send_sem=send.at[p * (NDEV - 1) + k], recv_sem=recv.at[p * (NDEV - 1) + k], device_id=peer,
                    device_id_type=MESH_T)
                cp.wait_send()
                cp.wait_recv()

    res = pl.pallas_call(
        body, name=name, out_shape=[pltpu.HBM(a.shape, a.dtype) for a in list(arrays) + list(lands)],
        in_specs=[HBM_SPEC] * (2 * n) + [SEM_SPEC] * 3 + [pl.BlockSpec(memory_space=pl.ANY)],
        out_specs=[HBM_SPEC] * (2 * n), input_output_aliases={i: i for i in range(2 * n)},
        compiler_params=pltpu.CompilerParams(has_side_effects=START_EFFECT))(
            *arrays, *lands, *sems, after)
    return res[n:]


def _row_block(rows, cols, itemsize=4, target=2**20):
    best = None
    for tr in range(16, rows + 1, 16):
        if rows % tr == 0 and tr * cols * itemsize <= target:
            best = tr
    return best or rows


def _adamw(w, m, v, gparts, name):
    parts, rows, cols = gparts.shape
    tr = _row_block(rows, cols, target=2**19)
    bc1 = 1.0 - ADAM_B1 ** ADAM_STEP
    bc2 = 1.0 - ADAM_B2 ** ADAM_STEP

    def body(w_ref, m_ref, v_ref, g_ref, go_ref, d_ref, mo_ref, vo_ref):
        g = g_ref[0].astype(F32)
        for k in range(1, parts):
            g = g + g_ref[k].astype(F32)
        mn = ADAM_B1 * m_ref[...] + (1.0 - ADAM_B1) * g
        vn = ADAM_B2 * v_ref[...] + (1.0 - ADAM_B2) * (g * g)
        go_ref[...] = g
        mo_ref[...] = mn
        vo_ref[...] = vn
        d_ref[...] = -ADAM_LR * ((mn / bc1) / (jnp.sqrt(vn / bc2) + ADAM_EPS) + ADAM_WD * w_ref[...])

    blk = pl.BlockSpec((tr, cols), lambda i: (i, 0))
    shp = jax.ShapeDtypeStruct((rows, cols), F32)
    return pl.pallas_call(
        body, name=name, grid=(rows // tr,),
        in_specs=[blk, blk, blk, pl.BlockSpec((parts, tr, cols), lambda i: (0, i, 0))],
        out_specs=[blk] * 4, out_shape=[shp] * 4, compiler_params=_cp(("parallel",)))(w, m, v, gparts)


def _pack(arrays, rows):
    flat = jnp.concatenate([a.reshape(-1).astype(F32) for a in arrays])
    return jnp.pad(flat, (0, rows * PACK_COLS - flat.shape[0])).reshape(rows, PACK_COLS)


def _unpack(buf, shapes):
    flat = buf.reshape(-1)
    out, off = [], 0
    for shp in shapes:
        size = int(np.prod(shp))
        out.append(flat[off:off + size].reshape(shp))
        off += size
    return out


REPL_SHAPES = {'mix_norm_g': (2, 1024), 'ffn_norm_g': (2, 1024), 'final_norm_g': (1024,), 's5_lam_re': (1, 16, 64),
               's5_lam_im': (1, 16, 64), 's5_log_dt': (1, 16), 's5_b_re': (1, 16, 64, 16), 's5_b_im': (1, 16, 64, 16),
               's5_c_re': (1, 16, 16, 64), 's5_c_im': (1, 16, 16, 64), 's5_d': (1, 256), 's5_b_glu': (1, 256),
               'gm_w_s': (1, 6, 128, 128), 'gm_b_s': (1, 6, 128), 'gm_v_g': (1, 768), 'ffn_conv_b': (2, 5632)}
REPL_ELEMS = sum(int(np.prod(REPL_SHAPES[n])) for n in REPL_ORDER)
REPL_ROWS = -(-REPL_ELEMS // (PACK_COLS * 8)) * 8

GATHER_DTYPE = {'ev_w_in': BF, 'ev_w_out': BF, 's5_w_glu': BF, 'od_w_in': BF, 'od_conv_w': F32, 'od_conv_b': F32,
                'od_w_out': BF, 'ffn_w_up': BF, 'ffn_conv_w': F32, 'ffn_w_down': BF}
GATHER_EVEN = ['ev_w_in', 'ev_w_out', 's5_w_glu']
GATHER_FFN0 = ['ffn_w_up0', 'ffn_conv_w0', 'ffn_w_down0']
GATHER_REST = ['od_w_in', 'od_conv_w', 'od_conv_b', 'od_w_out', 'ffn_w_up1', 'ffn_conv_w1', 'ffn_w_down1']

def _squeeze_lead(a):
    return a.reshape(a.shape[1:]) if a.shape[0] == 1 and a.ndim > 2 else a


def kernel(x, mix_norm_g, ffn_norm_g, final_norm_g, ev_w_in, ev_w_out, s5_lam_re, s5_lam_im, s5_log_dt, s5_b_re, s5_b_im, s5_c_re, s5_c_im, s5_d, s5_w_glu, s5_b_glu, gm_w_s, gm_b_s, gm_v_g, od_w_in, od_conv_w, od_conv_b, od_w_out, ffn_w_up, ffn_conv_w, ffn_conv_b, ffn_w_down, loss_target, m_mix_norm_g, m_ffn_norm_g, m_final_norm_g, m_ev_w_in, m_ev_w_out, m_s5_lam_re, m_s5_lam_im, m_s5_log_dt, m_s5_b_re, m_s5_b_im, m_s5_c_re, m_s5_c_im, m_s5_d, m_s5_w_glu, m_s5_b_glu, m_gm_w_s, m_gm_b_s, m_gm_v_g, m_od_w_in, m_od_conv_w, m_od_conv_b, m_od_w_out, m_ffn_w_up, m_ffn_conv_w, m_ffn_conv_b, m_ffn_w_down, v_mix_norm_g, v_ffn_norm_g, v_final_norm_g, v_ev_w_in, v_ev_w_out, v_s5_lam_re, v_s5_lam_im, v_s5_log_dt, v_s5_b_re, v_s5_b_im, v_s5_c_re, v_s5_c_im, v_s5_d, v_s5_w_glu, v_s5_b_glu, v_gm_w_s, v_gm_b_s, v_gm_v_g, v_od_w_in, v_od_conv_w, v_od_conv_b, v_od_w_out, v_ffn_w_up, v_ffn_conv_w, v_ffn_conv_b, v_ffn_w_down):
    given = dict(locals())
    weights = {n: given[n] for n in WEIGHT_ORDER}
    nseq, seq, _ = x.shape

    send = {}
    for name in SHARDED_ORDER:
        a = weights[name].astype(GATHER_DTYPE[name])
        if a.shape[0] == 2:
            send[name + '0'], send[name + '1'] = a[0], a[1]
        else:
            send[name] = _squeeze_lead(a)
    gathers = [_send_start([send[n] for n in names], False, f"gather_{tag}_start")
               for tag, names in (("ffn0", GATHER_FFN0), ("rest", GATHER_REST))]
    token = gathers[0][3] + gathers[1][3]

    def waiter(tag, names, started):
        return lambda after: dict(zip(names, _send_wait(started, False, after, f"gather_{tag}_wait")))

    gathered = dict(zip(GATHER_EVEN, _all_gather([send[n] for n in GATHER_EVEN], [0] * len(GATHER_EVEN),
                                                 "gather_even")))

    scatters = []

    def scatter(tag, grads):
        names = list(grads)
        started = _send_start([grads[n].astype(BF) for n in names], True, f"{tag}_start")
        scatters.append((tag, names, started))
        return started[3]

    loss_row, grad_x, g_repl = _local_step(
        x.reshape(nseq * seq, D), loss_target.reshape(nseq * seq, D), weights, gathered,
        waiter("ffn0", GATHER_FFN0, gathers[0]), waiter("rest", GATHER_REST, gathers[1]), token, scatter, seq)
    loss = lax.psum(loss_row[0, 0], ("x", "y", "c"))

    parts = {}
    for tag, names, started in scatters:
        parts.update(zip(names, _send_wait(started, True, grad_x, f"{tag}_wait")))
    repl_parts = _all_gather([_pack([g_repl[n] for n in REPL_ORDER], REPL_ROWS)], [0], "gather_small_grads")[0]

    out = {}
    for name in SHARDED_ORDER:
        w = weights[name]
        if name + '0' in parts:
            gp = jnp.stack([parts[name + '0'], parts[name + '1']], axis=1)
        else:
            gp = parts[name]
        to_rows = lambda a: a.reshape(-1, w.shape[-1])
        res = _adamw(to_rows(w), to_rows(given["m_" + name]), to_rows(given["v_" + name]),
                     gp.reshape(NDEV, -1, w.shape[-1]), f"adamw_{name}")
        out[name] = [r.reshape(w.shape) for r in res]
    rp = _adamw(_pack([weights[n] for n in REPL_ORDER], REPL_ROWS),
                _pack([given["m_" + n] for n in REPL_ORDER], REPL_ROWS),
                _pack([given["v_" + n] for n in REPL_ORDER], REPL_ROWS), repl_parts, "adamw_replicated")
    rp_shapes = [weights[n].shape for n in REPL_ORDER]
    for k in range(4):
        for name, a in zip(REPL_ORDER, _unpack(rp[k], rp_shapes)):
            out.setdefault(name, [None] * 4)[k] = a
    results = [[out[n][k] for n in WEIGHT_ORDER] for k in range(4)]
    grad_w, delta_w, new_m, new_v = results
    return (loss, grad_x.reshape(nseq, seq, D), *grad_w, *delta_w, *new_m, *new_v)
```

```python
import math

import jax
import jax.numpy as jnp
import numpy as np
from jax import lax
from jax.experimental import pallas as pl
from jax.experimental.pallas import tpu as pltpu

F32 = jnp.float32
BF = jnp.bfloat16

D = 1024
DFF = 2816
NDEV = 8
SSM_W = 256
SSM_G = 16
SSM_H = 16
SSM_P = 64
NST = SSM_G * SSM_P
GM_W = 768
GM_HEADS = 6
CHUNK = 128
EPS = 1e-6
LAM_MAX = -1e-4
FB = 256
FSH = 2 * DFF // NDEV
VMEM_LIMIT = 48 * 2**20
PACK_COLS = 1024
MESH_T = pl.DeviceIdType.MESH

ADAM_LR = 0.001
ADAM_B1 = 0.9
ADAM_B2 = 0.999
ADAM_EPS = 1e-08
ADAM_WD = 0.01
ADAM_STEP = 10

WEIGHT_ORDER = ['mix_norm_g', 'ffn_norm_g', 'final_norm_g', 'ev_w_in', 'ev_w_out', 's5_lam_re', 's5_lam_im',
                's5_log_dt', 's5_b_re', 's5_b_im', 's5_c_re', 's5_c_im', 's5_d', 's5_w_glu', 's5_b_glu', 'gm_w_s',
                'gm_b_s', 'gm_v_g', 'od_w_in', 'od_conv_w', 'od_conv_b', 'od_w_out', 'ffn_w_up', 'ffn_conv_w',
                'ffn_conv_b', 'ffn_w_down']
SHARDED = {'ev_w_in': ((1, 1024, 1792), 2), 'ev_w_out': ((1, 1024, 1024), 1), 's5_w_glu': ((1, 256, 256), 1),
           'od_w_in': ((1, 1024, 3072), 2), 'od_conv_w': ((1, 3, 1024), 2), 'od_conv_b': ((1, 1024), 1),
           'od_w_out': ((1, 1024, 1024), 1), 'ffn_w_up': ((2, 1024, 5632), 2), 'ffn_conv_w': ((2, 3, 5632), 2),
           'ffn_w_down': ((2, 2816, 1024), 1)}
SHARDED_ORDER = [n for n in WEIGHT_ORDER if n in SHARDED]
REPL_ORDER = [n for n in WEIGHT_ORDER if n not in SHARDED]


def _cp(sem):
    return pltpu.CompilerParams(dimension_semantics=sem, vmem_limit_bytes=VMEM_LIMIT)


def _sigmoid(x):
    return 1.0 / (1.0 + jnp.exp(-x))


_GELU_K = math.sqrt(2.0 / math.pi)


def _gelu(x):
    return 0.5 * x * (1.0 + jnp.tanh(_GELU_K * (x + 0.044715 * x * x * x)))


def _gelu_grad(x):
    t = jnp.tanh(_GELU_K * (x + 0.044715 * x * x * x))
    return 0.5 * (1.0 + t) + 0.5 * x * (1.0 - t * t) * _GELU_K * (1.0 + 3.0 * 0.044715 * x * x)


def _colsum(x):
    return jnp.sum(x, axis=0, keepdims=True)


def _accumulate(ref, first, part):
    @pl.when(first)
    def _():
        ref[...] = part

    @pl.when(jnp.logical_not(first))
    def _():
        ref[...] += part


_DIMS = {'nn': (((1,), (0,)), ((), ())), 'nt': (((1,), (1,)), ((), ())), 'tn': (((0,), (0,)), ((), ()))}


def _matmul(a, b, mode, tm, tn, tk, name, resid=None, out_dtype=F32):
    if mode == 'tn':
        kdim, m = a.shape
    else:
        m, kdim = a.shape
    n = b.shape[0] if mode == 'nt' else b.shape[1]
    tm, tn, tk = min(tm, m), min(tn, n), min(tk, kdim)
    assert m % tm == 0 and n % tn == 0 and kdim % tk == 0, (name, m, n, kdim, tm, tn, tk)
    a_spec = (pl.BlockSpec((tk, tm), lambda i, j, k: (k, i)) if mode == 'tn'
              else pl.BlockSpec((tm, tk), lambda i, j, k: (i, k)))
    b_spec = (pl.BlockSpec((tn, tk), lambda i, j, k: (j, k)) if mode == 'nt'
              else pl.BlockSpec((tk, tn), lambda i, j, k: (k, j)))
    o_spec = pl.BlockSpec((tm, tn), lambda i, j, k: (i, j))
    return _matmul_spec(a, b, mode, (m // tm, n // tn, kdim // tk), a_spec, b_spec, o_spec, (m, n), name,
                        resid=resid, out_dtype=out_dtype)


def _matmul_spec(a, b, mode, grid, a_spec, b_spec, o_spec, out_shape, name, resid=None, out_dtype=F32):
    nk = grid[2]
    tm, tn = o_spec.block_shape[-2:]
    dims = _DIMS[mode]
    has_resid = resid is not None

    def body(*refs):
        if has_resid:
            a_ref, b_ref, r_ref, o_ref = refs[:4]
        else:
            a_ref, b_ref, o_ref = refs[:3]
            r_ref = None
        part = lax.dot_general(a_ref[...].astype(BF), b_ref[...].astype(BF), dims, preferred_element_type=F32)
        if nk == 1:
            if has_resid:
                part = part + r_ref[...]
            o_ref[...] = part.astype(out_dtype)
        else:
            acc = refs[-1]
            k = pl.program_id(2)

            @pl.when(k == 0)
            def _():
                acc[...] = part

            @pl.when(k > 0)
            def _():
                acc[...] += part

            @pl.when(k == nk - 1)
            def _():
                tot = acc[...]
                if has_resid:
                    tot = tot + r_ref[...]
                o_ref[...] = tot.astype(out_dtype)

    operands = [a, b] + ([resid] if has_resid else [])
    in_specs = [a_spec, b_spec] + ([o_spec] if has_resid else [])
    return pl.pallas_call(
        body, name=name, grid=grid, in_specs=in_specs, out_specs=o_spec,
        out_shape=jax.ShapeDtypeStruct(out_shape, out_dtype),
        scratch_shapes=[pltpu.VMEM((tm, tn), F32)] if nk > 1 else [],
        compiler_params=_cp(("parallel", "parallel", "arbitrary")))(*operands)


def _matmul_shards(a, b, mode, tm, tn, name, resid=None, out_dtype=F32):
    shards, m, kdim = a.shape
    n = b.shape[2] if mode == 'nn' else b.shape[1]
    tm, tn = min(tm, m), min(tn, n)
    dims = _DIMS[mode]
    has_resid = resid is not None

    def body(*refs):
        a_ref, b_ref = refs[:2]
        acc = lax.dot_general(a_ref[0], b_ref[0], dims, preferred_element_type=F32)
        for s in range(1, shards):
            acc = acc + lax.dot_general(a_ref[s], b_ref[s], dims, preferred_element_type=F32)
        if has_resid:
            acc = acc + refs[2][...]
        refs[-1][...] = acc.astype(out_dtype)

    b_spec = (pl.BlockSpec((shards, kdim, tn), lambda i, j: (0, 0, j)) if mode == 'nn'
              else pl.BlockSpec((shards, tn, kdim), lambda i, j: (0, j, 0)))
    o_spec = pl.BlockSpec((tm, tn), lambda i, j: (i, j))
    return pl.pallas_call(
        body, name=name, grid=(m // tm, n // tn),
        in_specs=[pl.BlockSpec((shards, tm, kdim), lambda i, j: (0, i, 0)), b_spec] + ([o_spec] if has_resid else []),
        out_specs=o_spec, out_shape=jax.ShapeDtypeStruct((m, n), out_dtype),
        compiler_params=_cp(("parallel", "parallel")))(*([a, b] + ([resid] if has_resid else [])))


def _rmsnorm_fwd(x, g, name):
    n = x.shape[0]
    tm = min(512, n)

    def body(x_ref, g_ref, o_ref):
        xv = x_ref[...]
        r = lax.rsqrt(jnp.mean(xv * xv, axis=-1, keepdims=True) + EPS)
        o_ref[...] = (xv * r * g_ref[...]).astype(BF)

    return pl.pallas_call(
        body, name=name, grid=(n // tm,),
        in_specs=[pl.BlockSpec((tm, D), lambda i: (i, 0)), pl.BlockSpec((1, D), lambda i: (0, 0))],
        out_specs=pl.BlockSpec((tm, D), lambda i: (i, 0)),
        out_shape=jax.ShapeDtypeStruct((n, D), BF), compiler_params=_cp(("parallel",)))(x, g)


def _rmsnorm_bwd(x, g, dy, dres, name):
    n = x.shape[0]
    tm = min(512, n)

    def body(x_ref, g_ref, dy_ref, dr_ref, dx_ref, dxb_ref, dg_ref):
        xv = x_ref[...]
        r = lax.rsqrt(jnp.mean(xv * xv, axis=-1, keepdims=True) + EPS)
        xh = xv * r
        dyv = dy_ref[...]
        dyg = dyv * g_ref[...]
        dx = dr_ref[...] + r * (dyg - xh * jnp.mean(dyg * xh, axis=-1, keepdims=True))
        dx_ref[...] = dx
        dxb_ref[...] = dx.astype(BF)
        _accumulate(dg_ref, pl.program_id(0) == 0, _colsum(dyv * xh))

    row = pl.BlockSpec((tm, D), lambda i: (i, 0))
    vec = pl.BlockSpec((1, D), lambda i: (0, 0))
    return pl.pallas_call(
        body, name=name, grid=(n // tm,), in_specs=[row, vec, row, row], out_specs=[row, row, vec],
        out_shape=[jax.ShapeDtypeStruct((n, D), F32), jax.ShapeDtypeStruct((n, D), BF),
                   jax.ShapeDtypeStruct((1, D), F32)],
        compiler_params=_cp(("arbitrary",)))(x, g, dy, dres)


def _final_loss(h, g, tgt, name):
    n = h.shape[0]
    tm = min(512, n)

    def body(x_ref, g_ref, t_ref, loss_ref, dx_ref, dxb_ref, dg_ref):
        first = pl.program_id(0) == 0
        xv = x_ref[...]
        gv = g_ref[...]
        r = lax.rsqrt(jnp.mean(xv * xv, axis=-1, keepdims=True) + EPS)
        xh = xv * r
        err = xh * gv - t_ref[...]
        part = 0.5 * jnp.sum(jnp.mean(err * err, axis=-1, keepdims=True), axis=0, keepdims=True)
        _accumulate(loss_ref, first, jnp.broadcast_to(part, (1, 128)))
        dyv = err * (1.0 / D)
        dyg = dyv * gv
        dx = r * (dyg - xh * jnp.mean(dyg * xh, axis=-1, keepdims=True))
        dx_ref[...] = dx
        dxb_ref[...] = dx.astype(BF)
        _accumulate(dg_ref, first, _colsum(dyv * xh))

    row = pl.BlockSpec((tm, D), lambda i: (i, 0))
    vec = pl.BlockSpec((1, D), lambda i: (0, 0))
    return pl.pallas_call(
        body, name=name, grid=(n // tm,), in_specs=[row, vec, row],
        out_specs=[pl.BlockSpec((1, 128), lambda i: (0, 0)), row, row, vec],
        out_shape=[jax.ShapeDtypeStruct((1, 128), F32), jax.ShapeDtypeStruct((n, D), F32),
                   jax.ShapeDtypeStruct((n, D), BF), jax.ShapeDtypeStruct((1, D), F32)],
        compiler_params=_cp(("arbitrary",)))(h, g, tgt)


def _prev_rows(x, halo_ref, lanes, scale, row):
    h7 = halo_ref[7:8, lanes] * scale
    h6 = halo_ref[6:7, lanes] * scale
    p1 = jnp.where(row == 0, h7, pltpu.roll(x, 1, 0))
    p2 = jnp.where(row == 0, h6, jnp.where(row == 1, h7, pltpu.roll(x, 2, 0)))
    return p1, p2


def _halo_maps(tm, n_rows):
    r8 = tm // 8
    last = n_rows // 8 - 1
    prev = lambda i: jnp.maximum(i * r8 - 1, 0)
    nxt = lambda i: jnp.minimum((i + 1) * r8, last)
    return prev, nxt


def _lane_blocks(width):
    return [slice(lo, min(lo + 128, width)) for lo in range(0, width, 128)]


def _conv_taps(w_ref, b_ref, g, lanes):
    return w_ref[g, 0:1, lanes], w_ref[g, 1:2, lanes], w_ref[g, 2:3, lanes], b_ref[g, :, lanes]


def _conv_tile(x, prev1, prev2, taps, row):
    w0, w1, w2, b = taps
    r1 = pltpu.roll(x, 1, 0)
    r2 = pltpu.roll(x, 2, 0)
    x1 = jnp.where(row == 0, prev1, r1)
    x2 = jnp.where(row < 2, prev2, r2)
    return b + w0 * x2 + w1 * x1 + w2 * x, x1, x2, r1, r2


def _halo16_maps(tm, n_rows):
    r16 = tm // 16
    last = n_rows // 16 - 1
    return (lambda i: jnp.maximum(i * r16 - 1, 0)), (lambda i: jnp.minimum((i + 1) * r16, last))


def _ffn_conv_fwd(up, cw, cb, seq, name):
    n = up.shape[2]
    tm = min(256, seq)
    prev, _ = _halo16_maps(tm, n)

    def body(u_ref, h_ref, w_ref, b_ref, o_ref, d_ref):
        i = pl.program_id(1)
        scale = jnp.where(lax.rem(i * tm, seq) == 0, 0.0, 1.0)
        for lanes in _lane_blocks(FSH):
            lw = lanes.stop - lanes.start
            row = lax.broadcasted_iota(jnp.int32, (8, lw), 0)
            taps = [_conv_taps(w_ref, b_ref, g, lanes) for g in range(2)]

            def tile(xs, carry):
                hc, nxt = [], []
                for g in range(2):
                    conv, _, _, r1, r2 = _conv_tile(xs[g], carry[2 * g], carry[2 * g + 1], taps[g], row)
                    hc.append(conv)
                    nxt += [r1, r2]
                s = _sigmoid(hc[0])
                silu = hc[0] * s
                return (silu * hc[1], hc[1] * (s * (1.0 + hc[0] * (1.0 - s))), silu), tuple(nxt)

            carry = []
            for g in range(2):
                halo = h_ref[g, :, lanes].astype(F32)[8:] * scale
                carry += [pltpu.roll(halo, 1, 0), pltpu.roll(halo, 2, 0)]
            carry = tuple(carry)
            for m in range(tm // 16):
                rows = slice(m * 16, m * 16 + 16)
                x16 = [u_ref[g, rows, lanes].astype(F32) for g in range(2)]
                a, carry = tile([x[:8] for x in x16], carry)
                b, carry = tile([x[8:] for x in x16], carry)
                o_ref[rows, lanes] = jnp.concatenate([a[0], b[0]], axis=0).astype(BF)
                d_ref[0, rows, lanes] = jnp.concatenate([a[1], b[1]], axis=0).astype(BF)
                d_ref[1, rows, lanes] = jnp.concatenate([a[2], b[2]], axis=0).astype(BF)

    return pl.pallas_call(
        body, name=name, grid=(4, n // tm),
        in_specs=[pl.BlockSpec((2, None, tm, FSH), lambda j, i: (0, j, i, 0)),
                  pl.BlockSpec((2, None, 16, FSH), lambda j, i: (0, j, prev(i), 0)),
                  pl.BlockSpec((2, None, 3, FSH), lambda j, i: (0, j, 0, 0)),
                  pl.BlockSpec((2, None, 1, FSH), lambda j, i: (0, j, 0, 0))],
        out_specs=[pl.BlockSpec((None, tm, FSH), lambda j, i: (j, i, 0)),
                   pl.BlockSpec((2, None, tm, FSH), lambda j, i: (0, j, i, 0))],
        out_shape=[jax.ShapeDtypeStruct((4, n, FSH), BF), jax.ShapeDtypeStruct((2, 4, n, FSH), BF)],
        compiler_params=_cp(("parallel", "parallel")))(up, up, cw, cb)


def _ffn_conv_bwd(up, dgate, dact, cw, seq, name):
    n = up.shape[2]
    tm = min(256, seq)
    _, nxt = _halo16_maps(tm, n)

    def body(u_ref, g_ref, gn_ref, da_ref, dn_ref, w_ref, du_ref, dw_ref, db_ref):
        i = pl.program_id(1)
        sn = jnp.where(lax.rem((i + 1) * tm, seq) == 0, 0.0, 1.0)
        first = i == 0
        for lanes in _lane_blocks(FSH):
            lw = lanes.stop - lanes.start
            row = lax.broadcasted_iota(jnp.int32, (8, lw), 0)
            taps = [(w_ref[g, 0:1, lanes], w_ref[g, 1:2, lanes], w_ref[g, 2:3, lanes]) for g in range(2)]

            def dconv(gs, da):
                ds = [gs[g] * da for g in range(2)]
                return [(d, pltpu.roll(d, 7, 0), pltpu.roll(d, 6, 0)) for d in ds]

            def finish(cur, after, xs, sums):
                dups, new_sums = [], []
                for g in range(2):
                    w0, w1, w2 = taps[g]
                    s1 = jnp.where(row == 7, after[g][1], cur[g][1])
                    s2 = jnp.where(row >= 6, after[g][2], cur[g][2])
                    dups.append(w2 * cur[g][0] + w1 * s1 + w0 * s2)
                    acc = sums[g]
                    new_sums.append((acc[0] + xs[g] * s2, acc[1] + xs[g] * s1, acc[2] + xs[g] * cur[g][0],
                                     acc[3] + cur[g][0]))
                return dups, new_sums

            def emit(m, held, after, sums):
                (ta, xa), (tb, xb) = held
                dup_a, sums = finish(ta, tb, xa, sums)
                dup_b, sums = finish(tb, after, xb, sums)
                for g in range(2):
                    du_ref[g, m * 16:m * 16 + 16, lanes] = jnp.concatenate([dup_a[g], dup_b[g]], axis=0).astype(BF)
                return sums

            zero = jnp.zeros((8, lw), F32)
            sums = [(zero,) * 4, (zero,) * 4]
            held = None
            for m in range(tm // 16):
                rows = slice(m * 16, m * 16 + 16)
                x16 = [u_ref[g, rows, lanes].astype(F32) for g in range(2)]
                g16 = [g_ref[g, rows, lanes].astype(F32) for g in range(2)]
                d16 = da_ref[rows, lanes].astype(F32)
                ta = dconv([a[:8] for a in g16], d16[:8])
                tb = dconv([a[8:] for a in g16], d16[8:])
                if held is not None:
                    sums = emit(m - 1, held, ta, sums)
                held = ((ta, [x[:8] for x in x16]), (tb, [x[8:] for x in x16]))
            tn_ = dconv([gn_ref[g, :, lanes].astype(F32)[:8] for g in range(2)], dn_ref[:, lanes].astype(F32)[:8] * sn)
            sums = emit(tm // 16 - 1, held, tn_, sums)
            for g in range(2):
                for k in range(3):
                    _accumulate(dw_ref.at[g, k:k + 1, lanes], first, _colsum(sums[g][k]))
                _accumulate(db_ref.at[g, :, lanes], first, _colsum(sums[g][3]))

    return pl.pallas_call(
        body, name=name, grid=(4, n // tm),
        in_specs=[pl.BlockSpec((2, None, tm, FSH), lambda j, i: (0, j, i, 0)),
                  pl.BlockSpec((2, None, tm, FSH), lambda j, i: (0, j, i, 0)),
                  pl.BlockSpec((2, None, 16, FSH), lambda j, i: (0, j, nxt(i), 0)),
                  pl.BlockSpec((None, tm, FSH), lambda j, i: (j, i, 0)),
                  pl.BlockSpec((None, 16, FSH), lambda j, i: (j, nxt(i), 0)),
                  pl.BlockSpec((2, None, 3, FSH), lambda j, i: (0, j, 0, 0))],
        out_specs=[pl.BlockSpec((2, None, tm, FSH), lambda j, i: (0, j, i, 0)),
                   pl.BlockSpec((2, None, 3, FSH), lambda j, i: (0, j, 0, 0)),
                   pl.BlockSpec((2, None, 1, FSH), lambda j, i: (0, j, 0, 0))],
        out_shape=[jax.ShapeDtypeStruct((2, 4, n, FSH), BF), jax.ShapeDtypeStruct((2, 4, 3, FSH), F32),
                   jax.ShapeDtypeStruct((2, 4, 1, FSH), F32)],
        compiler_params=_cp(("parallel", "arbitrary")))(up, dgate, dgate, dact, dact, cw)


def _shortconv_fwd(p, cw, cb, seq, name):
    n = p.shape[0]
    tm = min(256, seq)
    prev, _ = _halo_maps(tm, n)

    def body(p_ref, h_ref, w_ref, b_ref, o_ref):
        i = pl.program_id(1)
        scale = jnp.where(lax.rem(i * tm, seq) == 0, 0.0, 1.0)
        q = p_ref[:, FB:2 * FB] * p_ref[:, 2 * FB:]
        row = lax.broadcasted_iota(jnp.int32, q.shape, 0)
        h7 = h_ref[7:8, FB:2 * FB] * h_ref[7:8, 2 * FB:] * scale
        h6 = h_ref[6:7, FB:2 * FB] * h_ref[6:7, 2 * FB:] * scale
        p1 = jnp.where(row == 0, h7, pltpu.roll(q, 1, 0))
        p2 = jnp.where(row == 0, h6, jnp.where(row == 1, h7, pltpu.roll(q, 2, 0)))
        conv = b_ref[...] + w_ref[0:1, :] * p2 + w_ref[1:2, :] * p1 + w_ref[2:3, :] * q
        o_ref[...] = (p_ref[:, :FB] * conv).astype(BF)

    return pl.pallas_call(
        body, name=name, grid=(D // FB, n // tm),
        in_specs=[pl.BlockSpec((tm, 3 * FB), lambda j, i: (i, j)),
                  pl.BlockSpec((8, 3 * FB), lambda j, i: (prev(i), j)),
                  pl.BlockSpec((3, FB), lambda j, i: (0, j)),
                  pl.BlockSpec((1, FB), lambda j, i: (0, j))],
        out_specs=pl.BlockSpec((tm, FB), lambda j, i: (i, j)),
        out_shape=jax.ShapeDtypeStruct((n, D), BF), compiler_params=_cp(("parallel", "parallel")))(p, p, cw, cb)


def _shortconv_bwd(p, dmix, cw, cb, seq, name):
    n = p.shape[0]
    tm = min(256, seq)
    ext = tm + 16
    prev, nxt = _halo_maps(tm, n)

    def body(p_ref, pp_ref, pn_ref, dm_ref, dn_ref, w_ref, b_ref, dp_ref, dw_ref, db_ref, qx, cx):
        i = pl.program_id(1)
        sp = jnp.where(lax.rem(i * tm, seq) == 0, 0.0, 1.0)
        sn = jnp.where(lax.rem((i + 1) * tm, seq) == 0, 0.0, 1.0)
        bg, cg, hx = p_ref[:, :FB], p_ref[:, FB:2 * FB], p_ref[:, 2 * FB:]
        dm = dm_ref[...]
        qx[0:8, :] = pp_ref[:, FB:2 * FB] * pp_ref[:, 2 * FB:] * sp
        qx[8:8 + tm, :] = cg * hx
        qx[8 + tm:, :] = jnp.zeros((8, FB), F32)
        cx[0:8, :] = jnp.zeros((8, FB), F32)
        cx[8:8 + tm, :] = dm * bg
        cx[8 + tm:, :] = dn_ref[...] * pn_ref[:, :FB] * sn
        q0 = qx[...]
        q1 = pltpu.roll(q0, 1, 0)
        q2 = pltpu.roll(q0, 2, 0)
        main = slice(8, 8 + tm)
        conv = b_ref[...] + w_ref[0:1, :] * q2[main] + w_ref[1:2, :] * q1[main] + w_ref[2:3, :] * q0[main]
        dc = cx[...]
        dq = (w_ref[2:3, :] * dc + w_ref[1:2, :] * pltpu.roll(dc, ext - 1, 0)
              + w_ref[0:1, :] * pltpu.roll(dc, ext - 2, 0))[main]
        dp_ref[:, :FB] = (dm * conv).astype(BF)
        dp_ref[:, FB:2 * FB] = (dq * hx).astype(BF)
        dp_ref[:, 2 * FB:] = (dq * cg).astype(BF)
        first = i == 0
        dcm = dc[main]
        _accumulate(dw_ref.at[0:1, :], first, _colsum(dcm * q2[main]))
        _accumulate(dw_ref.at[1:2, :], first, _colsum(dcm * q1[main]))
        _accumulate(dw_ref.at[2:3, :], first, _colsum(dcm * q0[main]))
        _accumulate(db_ref, first, _colsum(dcm))

    return pl.pallas_call(
        body, name=name, grid=(D // FB, n // tm),
        in_specs=[pl.BlockSpec((tm, 3 * FB), lambda j, i: (i, j)),
                  pl.BlockSpec((8, 3 * FB), lambda j, i: (prev(i), j)),
                  pl.BlockSpec((8, 3 * FB), lambda j, i: (nxt(i), j)),
                  pl.BlockSpec((tm, FB), lambda j, i: (i, j)),
                  pl.BlockSpec((8, FB), lambda j, i: (nxt(i), j)),
                  pl.BlockSpec((3, FB), lambda j, i: (0, j)),
                  pl.BlockSpec((1, FB), lambda j, i: (0, j))],
        out_specs=[pl.BlockSpec((tm, 3 * FB), lambda j, i: (i, j)),
                   pl.BlockSpec((3, FB), lambda j, i: (0, j)),
                   pl.BlockSpec((1, FB), lambda j, i: (0, j))],
        out_shape=[jax.ShapeDtypeStruct((n, 3 * D), BF), jax.ShapeDtypeStruct((3, D), F32),
                   jax.ShapeDtypeStruct((1, D), F32)],
        scratch_shapes=[pltpu.VMEM((ext, FB), F32), pltpu.VMEM((ext, FB), F32)],
        compiler_params=_cp(("parallel", "arbitrary")))(p, p, p, dmix, dmix, cw, cb)


def _gmlp_fwd(uv, wm, bst, gv, seq, name):
    n = uv.shape[0]
    tm = min(256, seq)

    def body(x_ref, w_ref, b_ref, g_ref, o_ref):
        ge_v = _gelu(x_ref[:, GM_W:])
        r = lax.rsqrt(jnp.mean(ge_v * ge_v, axis=-1, keepdims=True) + EPS)
        vn = (ge_v * r * g_ref[...]).astype(BF)
        for c in range(tm // CHUNK):
            rows = slice(c * CHUNK, (c + 1) * CHUNK)
            for h in range(GM_HEADS):
                cols = slice(h * CHUNK, (h + 1) * CHUNK)
                gate = jnp.dot(w_ref[h], vn[rows, cols], preferred_element_type=F32) + b_ref[:, h:h + 1]
                o_ref[rows, cols] = (_gelu(x_ref[rows, cols]) * gate).astype(BF)

    return pl.pallas_call(
        body, name=name, grid=(n // tm,),
        in_specs=[pl.BlockSpec((tm, 2 * GM_W), lambda i: (i, 0)),
                  pl.BlockSpec((GM_HEADS, CHUNK, CHUNK), lambda i: (0, 0, 0)),
                  pl.BlockSpec((CHUNK, GM_HEADS), lambda i: (0, 0)),
                  pl.BlockSpec((1, GM_W), lambda i: (0, 0))],
        out_specs=pl.BlockSpec((tm, GM_W), lambda i: (i, 0)),
        out_shape=jax.ShapeDtypeStruct((n, GM_W), BF), compiler_params=_cp(("parallel",)))(uv, wm, bst, gv)


def _gmlp_bwd(uv, dout, wm, wmt, bst, gv, seq, name):
    n = uv.shape[0]
    tm = min(256, seq)

    def body(x_ref, do_ref, w_ref, wt_ref, b_ref, g_ref, dx_ref, dw_ref, db_ref, dg_ref, dvn_scr):
        first = pl.program_id(0) == 0
        ge_v = _gelu(x_ref[:, GM_W:])
        r = lax.rsqrt(jnp.mean(ge_v * ge_v, axis=-1, keepdims=True) + EPS)
        vh = ge_v * r
        vn = (vh * g_ref[...]).astype(BF)
        tril = (lax.broadcasted_iota(jnp.int32, (CHUNK, CHUNK), 0)
                >= lax.broadcasted_iota(jnp.int32, (CHUNK, CHUNK), 1))
        for h in range(GM_HEADS):
            cols = slice(h * CHUNK, (h + 1) * CHUNK)
            dw = jnp.zeros((CHUNK, CHUNK), F32)
            dbs = jnp.zeros((CHUNK, 1), F32)
            for c in range(tm // CHUNK):
                rows = slice(c * CHUNK, (c + 1) * CHUNK)
                blk = vn[rows, cols]
                gate = jnp.dot(w_ref[h], blk, preferred_element_type=F32) + b_ref[:, h:h + 1]
                xu = x_ref[rows, cols]
                do = do_ref[rows, cols]
                dx_ref[rows, cols] = (do * gate * _gelu_grad(xu)).astype(BF)
                dgate = do * _gelu(xu)
                dgb = dgate.astype(BF)
                dw = dw + lax.dot_general(dgb, blk, _DIMS['nt'], preferred_element_type=F32)
                dbs = dbs + jnp.sum(dgate, axis=1, keepdims=True)
                dvn_scr[rows, cols] = jnp.dot(wt_ref[h], dgb, preferred_element_type=F32)
            _accumulate(dw_ref.at[h], first, jnp.where(tril, dw, 0.0))
            _accumulate(db_ref.at[h], first, dbs)
        dvn = dvn_scr[...]
        _accumulate(dg_ref, first, _colsum(dvn * vh))
        dvh = dvn * g_ref[...]
        dv = r * (dvh - vh * jnp.mean(dvh * vh, axis=-1, keepdims=True))
        dx_ref[:, GM_W:] = (dv * _gelu_grad(x_ref[:, GM_W:])).astype(BF)

    full3 = pl.BlockSpec((GM_HEADS, CHUNK, CHUNK), lambda i: (0, 0, 0))
    return pl.pallas_call(
        body, name=name, grid=(n // tm,),
        in_specs=[pl.BlockSpec((tm, 2 * GM_W), lambda i: (i, 0)), pl.BlockSpec((tm, GM_W), lambda i: (i, 0)),
                  full3, full3, pl.BlockSpec((CHUNK, GM_HEADS), lambda i: (0, 0)),
                  pl.BlockSpec((1, GM_W), lambda i: (0, 0))],
        out_specs=[pl.BlockSpec((tm, 2 * GM_W), lambda i: (i, 0)), full3,
                   pl.BlockSpec((GM_HEADS, CHUNK, 1), lambda i: (0, 0, 0)),
                   pl.BlockSpec((1, GM_W), lambda i: (0, 0))],
        out_shape=[jax.ShapeDtypeStruct((n, 2 * GM_W), BF), jax.ShapeDtypeStruct((GM_HEADS, CHUNK, CHUNK), F32),
                   jax.ShapeDtypeStruct((GM_HEADS, CHUNK, 1), F32), jax.ShapeDtypeStruct((1, GM_W), F32)],
        scratch_shapes=[pltpu.VMEM((tm, GM_W), F32)],
        compiler_params=_cp(("arbitrary",)))(uv, dout, wm, wmt, bst, gv)


def _s5_disc(lam_re, lam_im, log_dt, b_re, b_im):
    lr = jnp.minimum(lam_re, LAM_MAX)
    li = lam_im
    dt = jnp.exp(log_dt)
    mag = jnp.exp(lr * dt)
    ab_re = mag * jnp.cos(li * dt)
    ab_im = mag * jnp.sin(li * dt)
    den = lr * lr + li * li
    nr = ab_re - 1.0
    ni = ab_im
    z_re = (nr * lr + ni * li) / den
    z_im = (ni * lr - nr * li) / den
    return ab_re, ab_im, z_re * b_re - z_im * b_im, z_re * b_im + z_im * b_re


def _s5_disc_fwd(args, name):
    shp = jax.ShapeDtypeStruct(args[0].shape, F32)

    def body(*refs):
        outs = _s5_disc(*[r[...] for r in refs[:5]])
        for o_ref, o in zip(refs[5:], outs):
            o_ref[...] = o

    return pl.pallas_call(body, name=name, out_shape=[shp] * 4)(*args)


def _s5_disc_bwd(args, cts, name):
    shp = jax.ShapeDtypeStruct(args[0].shape, F32)

    def body(*refs):
        _, vjp = jax.vjp(_s5_disc, *[r[...] for r in refs[:5]])
        grads = vjp(tuple(r[...] for r in refs[5:9]))
        for o_ref, o in zip(refs[9:], grads):
            o_ref[...] = o

    return pl.pallas_call(body, name=name, out_shape=[shp] * 5)(*args, *cts)


def _cmul(a, b):
    return a[0] * b[0] - a[1] * b[1], a[0] * b[1] + a[1] * b[0]


def _scan_tables(ar, ai, reverse):
    if reverse:
        ai = -ai
    a1 = (ar, ai)
    a2 = _cmul(a1, a1)
    a3 = _cmul(a2, a1)
    a4 = _cmul(a2, a2)
    powers = [a1, a2, a3, a4, _cmul(a4, a1), _cmul(a4, a2), _cmul(a4, a3), _cmul(a4, a4)]
    row = lax.broadcasted_iota(jnp.int32, (8, NST), 0)
    zero = jnp.zeros((8, NST), F32)
    pr, pi = zero, zero
    for r in range(8):
        pw = powers[7 - r] if reverse else powers[r]
        pr = jnp.where(row == r, pw[0], pr)
        pi = jnp.where(row == r, pw[1], pi)
    levels = []
    for d, pw in ((1, a1), (2, a2), (4, a4)):
        ok = (row <= 7 - d) if reverse else (row >= d)
        levels.append((d, jnp.where(ok, pw[0], zero), jnp.where(ok, pw[1], zero)))
    return (pr, pi), levels


def _scan_block(src, dst, car, tables, n_tiles, reverse):
    (pr, pi), levels = tables
    row = lax.broadcasted_iota(jnp.int32, (8, NST), 0)
    out_row = 0 if reverse else 7

    def step(t, carry):
        cr, ci = carry
        tile = (n_tiles - 1 - t) if reverse else t
        rows = pl.ds(pl.multiple_of(tile * 8, 8), 8)
        xr = src[rows, 0:NST]
        xi = src[rows, NST:2 * NST]
        for d, dr, di in levels:
            shift = 8 - d if reverse else d
            rr = pltpu.roll(xr, shift, 0)
            ri = pltpu.roll(xi, shift, 0)
            xr, xi = xr + dr * rr - di * ri, xi + dr * ri + di * rr
        hr = xr + pr * cr - pi * ci
        hi = xi + pr * ci + pi * cr
        dst[rows, 0:NST] = hr
        dst[rows, NST:2 * NST] = hi
        return (_colsum(jnp.where(row == out_row, hr, 0.0)), _colsum(jnp.where(row == out_row, hi, 0.0)))

    cr, ci = lax.fori_loop(0, n_tiles, step, (car[0:1, 0:NST], car[0:1, NST:2 * NST]))
    car[0:1, 0:NST] = cr
    car[0:1, NST:2 * NST] = ci


def _s5_fwd(u, ab, bbt, cmat, dvec, wglu, bglu, seq, name):
    n = u.shape[0]
    tm = min(256, seq)

    def body(u_ref, ab_ref, bb_ref, c_ref, d_ref, w_ref, b_ref, h_ref, o_ref, xs, car):
        i = pl.program_id(0)

        @pl.when(lax.rem(i * tm, seq) == 0)
        def _():
            car[...] = jnp.zeros(car.shape, F32)

        uv = u_ref[...]
        xs[...] = jnp.dot(uv.astype(BF), bb_ref[...], preferred_element_type=F32)
        tables = _scan_tables(ab_ref[0:1, 0:NST], ab_ref[0:1, NST:2 * NST], False)
        _scan_block(xs, h_ref, car, tables, tm // 8, False)
        y = jnp.dot(h_ref[...].astype(BF), c_ref[...], preferred_element_type=F32) + d_ref[...] * uv
        g1 = _gelu(y)
        z = jnp.dot(g1.astype(BF), w_ref[...], preferred_element_type=F32) + b_ref[...]
        o_ref[...] = (g1 * _sigmoid(z)).astype(BF)

    const = lambda shape: pl.BlockSpec(shape, lambda i: (0, 0))
    return pl.pallas_call(
        body, name=name, grid=(n // tm,),
        in_specs=[pl.BlockSpec((tm, SSM_W), lambda i: (i, 0)), const((1, 2 * NST)), const((SSM_W, 2 * NST)),
                  const((2 * NST, SSM_W)), const((1, SSM_W)), const((SSM_W, SSM_W)), const((1, SSM_W))],
        out_specs=[pl.BlockSpec((tm, 2 * NST), lambda i: (i, 0)), pl.BlockSpec((tm, SSM_W), lambda i: (i, 0))],
        out_shape=[jax.ShapeDtypeStruct((n, 2 * NST), F32), jax.ShapeDtypeStruct((n, SSM_W), BF)],
        scratch_shapes=[pltpu.VMEM((tm, 2 * NST), F32), pltpu.VMEM((8, 2 * NST), F32)],
        compiler_params=_cp(("arbitrary",)))(u, ab, bbt, cmat, dvec, wglu, bglu)


def _s5_bwd(da, u, hst, ab, bbt, cmat, dvec, wglu, bglu, seq, name):
    n = u.shape[0]
    tm = min(256, seq)
    nb = n // tm
    blk = lambda r: nb - 1 - r
    prev, _ = _halo_maps(tm, n)

    def body(da_ref, u_ref, h_ref, hp_ref, ab_ref, bb_ref, c_ref, d_ref, w_ref, b_ref,
             du_ref, dw_ref, dbg_ref, dd_ref, dc_ref, dbb_ref, dab_ref, gs, car):
        r = pl.program_id(0)
        i = blk(r)
        first = r == 0

        @pl.when(lax.rem((i + 1) * tm, seq) == 0)
        def _():
            car[...] = jnp.zeros(car.shape, F32)

        uv = u_ref[...]
        dav = da_ref[...]
        hb = h_ref[...]
        hb16 = hb.astype(BF)
        dvv = d_ref[...]
        y = jnp.dot(hb16, c_ref[...], preferred_element_type=F32) + dvv * uv
        g1 = _gelu(y)
        g16 = g1.astype(BF)
        s = _sigmoid(jnp.dot(g16, w_ref[...], preferred_element_type=F32) + b_ref[...])
        dz = dav * g1 * s * (1.0 - s)
        dz16 = dz.astype(BF)
        dg1 = dav * s + lax.dot_general(dz16, w_ref[...], _DIMS['nt'], preferred_element_type=F32)
        _accumulate(dw_ref, first, lax.dot_general(g16, dz16, _DIMS['tn'], preferred_element_type=F32))
        _accumulate(dbg_ref, first, _colsum(dz))
        dy = dg1 * _gelu_grad(y)
        dy16 = dy.astype(BF)
        _accumulate(dd_ref, first, _colsum(dy * uv))
        _accumulate(dc_ref, first, lax.dot_general(hb16, dy16, _DIMS['tn'], preferred_element_type=F32))
        gs[...] = lax.dot_general(dy16, c_ref[...], _DIMS['nt'], preferred_element_type=F32)
        tables = _scan_tables(ab_ref[0:1, 0:NST], ab_ref[0:1, NST:2 * NST], True)
        _scan_block(gs, gs, car, tables, tm // 8, True)
        g = gs[...]
        g16b = g.astype(BF)
        sp = jnp.where(lax.rem(i * tm, seq) == 0, 0.0, 1.0)
        row = lax.broadcasted_iota(jnp.int32, hb.shape, 0)
        hprev = jnp.where(row == 0, hp_ref[7:8, :] * sp, pltpu.roll(hb, 1, 0))
        gr, gi = g[:, :NST], g[:, NST:]
        hr, hi = hprev[:, :NST], hprev[:, NST:]
        _accumulate(dab_ref.at[:, 0:NST], first, _colsum(gr * hr + gi * hi))
        _accumulate(dab_ref.at[:, NST:2 * NST], first, _colsum(gi * hr - gr * hi))
        _accumulate(dbb_ref, first, lax.dot_general(uv.astype(BF), g16b, _DIMS['tn'], preferred_element_type=F32))
        du = dy * dvv + lax.dot_general(g16b, bb_ref[...], _DIMS['nt'], preferred_element_type=F32)
        du_ref[...] = du.astype(BF)

    const = lambda shape: pl.BlockSpec(shape, lambda r: (0, 0))
    rowspec = lambda w: pl.BlockSpec((tm, w), lambda r: (blk(r), 0))
    return pl.pallas_call(
        body, name=name, grid=(nb,),
        in_specs=[rowspec(SSM_W), rowspec(SSM_W), rowspec(2 * NST),
                  pl.BlockSpec((8, 2 * NST), lambda r: (prev(blk(r)), 0)),
                  const((1, 2 * NST)), const((SSM_W, 2 * NST)), const((2 * NST, SSM_W)), const((1, SSM_W)),
                  const((SSM_W, SSM_W)), const((1, SSM_W))],
        out_specs=[rowspec(SSM_W), const((SSM_W, SSM_W)), const((1, SSM_W)), const((1, SSM_W)),
                   const((2 * NST, SSM_W)), const((SSM_W, 2 * NST)), const((1, 2 * NST))],
        out_shape=[jax.ShapeDtypeStruct((n, SSM_W), BF), jax.ShapeDtypeStruct((SSM_W, SSM_W), F32),
                   jax.ShapeDtypeStruct((1, SSM_W), F32), jax.ShapeDtypeStruct((1, SSM_W), F32),
                   jax.ShapeDtypeStruct((2 * NST, SSM_W), F32), jax.ShapeDtypeStruct((SSM_W, 2 * NST), F32),
                   jax.ShapeDtypeStruct((1, 2 * NST), F32)],
        scratch_shapes=[pltpu.VMEM((tm, 2 * NST), F32), pltpu.VMEM((8, 2 * NST), F32)],
        compiler_params=_cp(("arbitrary",)))(da, u, hst, hst, ab, bbt, cmat, dvec, wglu, bglu)


def _s5_rows(lam_re, lam_im, log_dt, b_re, b_im):
    rep = lambda a: jnp.broadcast_to(a[:, None, :], (SSM_G, SSM_H, SSM_P)).reshape(SSM_W, SSM_P)
    dt = jnp.broadcast_to(log_dt[:, None, None], (SSM_G, SSM_H, SSM_P)).reshape(SSM_W, SSM_P)
    tr = lambda b: b.transpose(0, 2, 1).reshape(SSM_W, SSM_P)
    return rep(lam_re), rep(lam_im), dt, tr(b_re), tr(b_im)


def _block_diag(rows_gp, inner):
    eye = jnp.eye(SSM_G, dtype=rows_gp.dtype)
    return (rows_gp[:, :, None, :] * eye[:, None, :, None]).reshape(SSM_G * inner, SSM_G * SSM_P)


def _diag_blocks(mat, inner):
    m4 = mat.reshape(SSM_G, inner, SSM_G, SSM_P)
    return jnp.stack([m4[g, :, g, :] for g in range(SSM_G)])


def _interleave(w, parts):
    lead = w.shape[:-1]
    nb = w.shape[-1] // (parts * FB)
    return jnp.swapaxes(w.reshape(lead + (parts, nb, FB)), -3, -2).reshape(w.shape)


def _deinterleave(w, parts):
    lead = w.shape[:-1]
    nb = w.shape[-1] // (parts * FB)
    return jnp.swapaxes(w.reshape(lead + (nb, parts, FB)), -3, -2).reshape(w.shape)


def _ffn_fwd(h, g, w_up, w_down, cw, cb, seq, tag):
    n = h.shape[0]
    tm = min(1024, n)
    ni = n // tm
    f = _rmsnorm_fwd(h, g, f"{tag}_norm")
    up = _matmul_spec(
        f, w_up, 'nn', (NDEV, ni, 1),
        pl.BlockSpec((tm, D), lambda s, i, k: (i, 0)),
        pl.BlockSpec((D, FSH), lambda s, i, k: (s, 0)),
        pl.BlockSpec((tm, FSH), lambda s, i, k: (s * ni + i, 0)), (NDEV * n, FSH), f"{tag}_up", out_dtype=BF)
    up = up.reshape(2, 4, n, FSH)
    act, dgate = _ffn_conv_fwd(up, cw, cb, seq, f"{tag}_conv")
    out = _matmul_shards(act, w_down.reshape(4, FSH, D), 'nn', 1024, 512, f"{tag}_down", resid=h)
    return out, (f, up, act, dgate)


def _ffn_bwd(dh, dhb, h, g, w_up, w_down, cw, cb, saved, seq, tag):
    f, up, act, dgate = saved
    n = h.shape[0]
    tm = min(1024, n)
    ni = n // tm
    tk = min(4096, n)
    nk = n // tk
    dact = _matmul_spec(
        dhb, w_down, 'nt', (4, ni, 1),
        pl.BlockSpec((tm, D), lambda j, i, k: (i, 0)),
        pl.BlockSpec((FSH, D), lambda j, i, k: (j, 0)),
        pl.BlockSpec((tm, FSH), lambda j, i, k: (j * ni + i, 0)), (4 * n, FSH), f"{tag}_ddown_x", out_dtype=BF)
    tn = 512
    dw_down = _matmul_spec(
        act.reshape(4 * n, FSH), dhb, 'tn', (4, D // tn, nk),
        pl.BlockSpec((tk, FSH), lambda j, c, k: (j * nk + k, 0)),
        pl.BlockSpec((tk, tn), lambda j, c, k: (k, c)),
        pl.BlockSpec((FSH, tn), lambda j, c, k: (j, c)), (DFF, D), f"{tag}_ddown_w", out_dtype=BF)
    dup, dcw, dcb = _ffn_conv_bwd(up, dgate, dact.reshape(4, n, FSH), cw, seq, f"{tag}_dconv")
    dup2 = dup.reshape(NDEV * n, FSH)
    df = _matmul_shards(dup.reshape(NDEV, n, FSH), w_up.reshape(NDEV, D, FSH), 'nt', 256, D, f"{tag}_dup_x")
    dw_up = _matmul_spec(
        f, dup2, 'tn', (NDEV, 1, nk),
        pl.BlockSpec((tk, D), lambda s, j, k: (k, 0)),
        pl.BlockSpec((tk, FSH), lambda s, j, k: (s * nk + k, 0)),
        pl.BlockSpec((D, FSH), lambda s, j, k: (s, 0)), (NDEV * D, FSH), f"{tag}_dup_w", out_dtype=BF)
    dh_in, dhb_in, dg = _rmsnorm_bwd(h, g, df, dh, f"{tag}_dnorm")
    grads = dict(g=dg, w_up=dw_up.reshape(NDEV, D, FSH), w_down=dw_down.reshape(NDEV, DFF // NDEV, D),
                 cw=dcw.reshape(NDEV, 3, FSH), cb=dcb.reshape(2 * DFF))
    return dh_in, dhb_in, grads


def _col_shards(w, width):
    return w.reshape(w.shape[0], NDEV, width).transpose(1, 0, 2)


def _local_step(x, tgt, w, gw, wait_ffn0, wait_rest, token, scatter, seq):
    bf = lambda a: a.astype(BF)
    row = lambda a: a.reshape(1, -1).astype(F32)
    w_ev = gw['ev_w_in'].transpose(1, 0, 2).reshape(D, 1792)
    w_ev_s5, w_ev_gm = w_ev[:, :SSM_W], w_ev[:, SSM_W:]
    w_evo = gw['ev_w_out'].reshape(D, D)
    f_cb = [w['ffn_conv_b'][l].reshape(2, 4, 1, FSH) for l in range(2)]
    tril = jnp.tril(jnp.ones((CHUNK, CHUNK), dtype=bool))
    gm_w = jnp.where(tril, w['gm_w_s'][0], 0.0)
    gm_wm, gm_wmt = bf(gm_w), bf(jnp.swapaxes(gm_w, 1, 2))
    gm_bt = w['gm_b_s'][0].T
    gm_gv = row(w['gm_v_g'][0])
    s5_in = _s5_rows(w['s5_lam_re'][0], w['s5_lam_im'][0], w['s5_log_dt'][0], w['s5_b_re'][0], w['s5_b_im'][0])
    ab_re, ab_im, bb_re, bb_im = _s5_disc_fwd(s5_in, "s5_disc")
    first_h = lambda a: a.reshape(SSM_G, SSM_H, SSM_P)[:, 0, :].reshape(1, NST)
    s5_ab = jnp.concatenate([first_h(ab_re), first_h(ab_im)], axis=1)
    to_gp = lambda a: a.reshape(SSM_G, SSM_H, SSM_P)
    s5_bbt = bf(jnp.concatenate([_block_diag(to_gp(bb_re), SSM_H), _block_diag(to_gp(bb_im), SSM_H)], axis=1))
    s5_cmat = bf(jnp.concatenate([_block_diag(w['s5_c_re'][0], SSM_H).T, -_block_diag(w['s5_c_im'][0], SSM_H).T],
                                 axis=0))
    s5_d, s5_bg, s5_wg = row(w['s5_d'][0]), row(w['s5_b_glu'][0]), gw['s5_w_glu'].reshape(SSM_W, SSM_W)
    g_mix = [row(w['mix_norm_g'][0]) + token[0:1, 0:1], row(w['mix_norm_g'][1])]
    g_ffn = [row(w['ffn_norm_g'][l]) for l in range(2)]
    g_fin = row(w['final_norm_g'])

    h0 = x
    y0 = _rmsnorm_fwd(h0, g_mix[0], "ev_norm")
    p_s5 = _matmul(y0, w_ev_s5, 'nn', 1024, 256, D, "ev_in_s5")
    p_gm = _matmul(y0, w_ev_gm, 'nn', 1024, 512, D, "ev_in_gm")
    hst, a_out = _s5_fwd(p_s5, s5_ab, s5_bbt, s5_cmat, s5_d, s5_wg, s5_bg, seq, "s5_fwd")
    b_out = _gmlp_fwd(p_gm, gm_wm, gm_bt, gm_gv, seq, "gmlp_fwd")
    mixcat = jnp.concatenate([a_out, b_out], axis=1)
    h1 = _matmul(mixcat, w_evo, 'nn', 1024, 512, D, "ev_out", resid=h0)
    g0 = wait_ffn0(mixcat)
    w_up0, w_dn0 = g0['ffn_w_up0'].reshape(NDEV * D, FSH), g0['ffn_w_down0'].reshape(DFF, D)
    f_cw0 = g0['ffn_conv_w0'].reshape(2, 4, 3, FSH)
    h2, ffn0 = _ffn_fwd(h1, g_ffn[0], w_up0, w_dn0, f_cw0, f_cb[0], seq, "ffn0")
    g1 = wait_rest(h2)
    w_od = _interleave(g1['od_w_in'].transpose(1, 0, 2).reshape(D, 3 * D), 3)
    w_odo = g1['od_w_out'].reshape(D, D)
    od_cw = g1['od_conv_w'].transpose(1, 0, 2).reshape(3, D)
    od_cb = g1['od_conv_b'].reshape(1, D)
    w_up1, w_dn1 = g1['ffn_w_up1'].reshape(NDEV * D, FSH), g1['ffn_w_down1'].reshape(DFF, D)
    f_cw1 = g1['ffn_conv_w1'].reshape(2, 4, 3, FSH)
    y1 = _rmsnorm_fwd(h2, g_mix[1], "od_norm")
    p_od = _matmul(y1, w_od, 'nn', 1024, 512, D, "od_in")
    mixin = _shortconv_fwd(p_od, od_cw, od_cb, seq, "od_conv")
    h3 = _matmul(mixin, w_odo, 'nn', 1024, 512, D, "od_out", resid=h2)
    h4, ffn1 = _ffn_fwd(h3, g_ffn[1], w_up1, w_dn1, f_cw1, f_cb[1], seq, "ffn1")
    loss, dh4, dh4b, dg_fin = _final_loss(h4, g_fin, tgt, "final_loss")

    dh3, dh3b, gf1 = _ffn_bwd(dh4, dh4b, h3, g_ffn[1], w_up1, w_dn1, f_cw1, f_cb[1], ffn1, seq, "ffn1")
    dmixin = _matmul(dh3b, w_odo, 'nt', 1024, 512, D, "od_dout_x")
    dw_odo = _matmul(mixin, dh3b, 'tn', D, 512, 4096, "od_dout_w", out_dtype=BF)
    dp_od, d_od_cw, d_od_cb = _shortconv_bwd(p_od, dmixin, od_cw, od_cb, seq, "od_dconv")
    dy1 = _matmul(dp_od, w_od, 'nt', 256, D, 3 * D, "od_din_x")
    dw_od = _matmul(y1, dp_od, 'tn', D, 512, 4096, "od_din_w", out_dtype=BF)
    sent = scatter("scatter_layer1", {
        'od_w_in': _col_shards(_deinterleave(dw_od, 3), 384), 'od_conv_w': _col_shards(d_od_cw, D // NDEV),
        'od_conv_b': d_od_cb.reshape(NDEV, 1, D // NDEV), 'od_w_out': dw_odo.reshape(NDEV, D // NDEV, D),
        'ffn_w_up1': gf1['w_up'], 'ffn_conv_w1': gf1['cw'], 'ffn_w_down1': gf1['w_down']})
    dh2, dh2b, dg_mix1 = _rmsnorm_bwd(h2, g_mix[1] + sent[0:1, 0:1], dy1, dh3, "od_dnorm")
    dh1, dh1b, gf0 = _ffn_bwd(dh2, dh2b, h1, g_ffn[0], w_up0, w_dn0, f_cw0, f_cb[0], ffn0, seq, "ffn0")
    dmix_a = _matmul(dh1b, w_evo[:SSM_W], 'nt', 1024, SSM_W, D, "ev_dout_xa")
    dmix_b = _matmul(dh1b, w_evo[SSM_W:], 'nt', 1024, GM_W, D, "ev_dout_xb")
    dw_evo = _matmul(mixcat, dh1b, 'tn', D, 512, 4096, "ev_dout_w", out_dtype=BF)
    sent = scatter("scatter_ffn0", {'ffn_w_up0': gf0['w_up'], 'ffn_conv_w0': gf0['cw'], 'ffn_w_down0': gf0['w_down'],
                                    'ev_w_out': dw_evo.reshape(NDEV, D // NDEV, D)})
    dp_s5, d_wg, d_bg, d_d, d_cmat, d_bbt, d_ab = _s5_bwd(dmix_a, p_s5, hst, s5_ab, s5_bbt, s5_cmat,
                                                           s5_d + sent[0:1, 0:1], s5_wg, s5_bg, seq, "s5_bwd")
    dp_gm, d_gmw, d_gmb, d_gmg = _gmlp_bwd(p_gm, dmix_b, gm_wm, gm_wmt, gm_bt, gm_gv, seq, "gmlp_bwd")
    dw_ev = jnp.concatenate([_matmul(y0, dp_s5, 'tn', D, SSM_W, 4096, "ev_din_wa", out_dtype=BF),
                             _matmul(y0, dp_gm, 'tn', D, 512, 4096, "ev_din_wb", out_dtype=BF)], axis=1)
    sent = scatter("scatter_even", {'ev_w_in': _col_shards(dw_ev, 224),
                                    's5_w_glu': d_wg.reshape(NDEV, SSM_W // NDEV, SSM_W)})
    dy0 = _matmul(dp_gm, w_ev_gm, 'nt', 512, D, 2 * GM_W, "ev_din_xb")
    dy0 = _matmul(dp_s5, w_ev_s5, 'nt', 1024, D, SSM_W, "ev_din_xa", resid=dy0)
    grad_x, _, dg_mix0 = _rmsnorm_bwd(h0, g_mix[0] + sent[0:1, 0:1], dy0, dh1, "ev_dnorm")

    put_h0 = lambda a: jnp.zeros((SSM_G, SSM_H, SSM_P), F32).at[:, 0, :].set(a.reshape(SSM_G, SSM_P)).reshape(
        SSM_W, SSM_P)
    ct = (put_h0(d_ab[:, :NST]), put_h0(d_ab[:, NST:]),
          _diag_blocks(d_bbt[:, :NST], SSM_H).reshape(SSM_W, SSM_P),
          _diag_blocks(d_bbt[:, NST:], SSM_H).reshape(SSM_W, SSM_P))
    d_lre, d_lim, d_ldt, d_bre, d_bim = _s5_disc_bwd(s5_in, ct, "s5_ddisc")
    over_h = lambda a: a.reshape(SSM_G, SSM_H, SSM_P).sum(axis=1)
    un_tr = lambda a: a.reshape(SSM_G, SSM_H, SSM_P).transpose(0, 2, 1)
    d_cre = _diag_blocks(d_cmat[:NST].T, SSM_H)
    d_cim = -_diag_blocks(d_cmat[NST:].T, SSM_H)

    repl = {
        'mix_norm_g': jnp.concatenate([dg_mix0, dg_mix1], axis=0),
        'ffn_norm_g': jnp.concatenate([gf0['g'], gf1['g']], axis=0),
        'final_norm_g': dg_fin.reshape(D),
        's5_lam_re': over_h(d_lre)[None], 's5_lam_im': over_h(d_lim)[None],
        's5_log_dt': over_h(d_ldt).sum(axis=1)[None],
        's5_b_re': un_tr(d_bre)[None], 's5_b_im': un_tr(d_bim)[None],
        's5_c_re': d_cre[None], 's5_c_im': d_cim[None],
        's5_d': d_d, 's5_b_glu': d_bg,
        'gm_w_s': d_gmw[None], 'gm_b_s': d_gmb.reshape(1, GM_HEADS, CHUNK), 'gm_v_g': d_gmg,
        'ffn_conv_b': jnp.stack([gf0['cb'], gf1['cb']]),
    }
    return loss, grad_x, repl


HBM_SPEC = pl.BlockSpec(memory_space=pltpu.HBM)


def _at_axis(ref, pos, index):
    return ref.at[(slice(None),) * pos + (index,)]


def _all_gather(shards, positions, name):
    n = len(shards)

    def body(*refs):
        xs, outs = refs[:n], refs[n:2 * n]
        send_sems, recv_sems, local_sems = refs[2 * n:]
        x, y, c = lax.axis_index("x"), lax.axis_index("y"), lax.axis_index("c")
        me, sibling = (x, y, c), (x, y, 1 - c)
        chips = [(1 - x, y), (x, 1 - y), (1 - x, 1 - y)]

        def block(p, dev):
            return _at_axis(outs[p], positions[p], 4 * dev[0] + 2 * dev[1] + dev[2])

        def copy(p, k, dev, to, src=None):
            return pltpu.make_async_remote_copy(
                src_ref=block(p, dev) if src is None else src, dst_ref=block(p, dev),
                send_sem=send_sems.at[p, k], recv_sem=recv_sems.at[p, k], device_id=to, device_id_type=MESH_T)

        mine = [pltpu.make_async_copy(xs[p], block(p, me), local_sems.at[p]) for p in range(n)]
        for cp in mine:
            cp.start()
        first = [copy(p, 0, me, sibling, src=xs[p]) for p in range(n)]
        first += [copy(p, 1 + j, me, (*chip, c), src=xs[p]) for j, chip in enumerate(chips) for p in range(n)]
        for cp in first:
            cp.start()
        passed = []
        for j, chip in enumerate(chips):
            for p in range(n):
                copy(p, 1 + j, (*chip, c), me).wait_recv()
                fwd = copy(p, 4 + j, (*chip, c), sibling)
                fwd.start()
                passed.append(fwd)
        for p in range(n):
            copy(p, 0, sibling, me).wait_recv()
        for j, chip in enumerate(chips):
            for p in range(n):
                copy(p, 4 + j, (*chip, 1 - c), me).wait_recv()
        for cp in first + passed:
            cp.wait_send()
        for cp in mine:
            cp.wait()

    out_shape = [jax.ShapeDtypeStruct(s.shape[:pos] + (NDEV,) + s.shape[pos:], s.dtype)
                 for s, pos in zip(shards, positions)]
    return pl.pallas_call(
        body, name=name, out_shape=out_shape, in_specs=[HBM_SPEC] * n, out_specs=[HBM_SPEC] * n,
        scratch_shapes=[pltpu.SemaphoreType.DMA((n, 7)), pltpu.SemaphoreType.DMA((n, 7)),
                        pltpu.SemaphoreType.DMA((n,))])(*shards)


def _other_devices(x, y, c):
    flip = lambda v, bit: 1 - v if bit else v
    return [(flip(x, k >> 2 & 1), flip(y, k >> 1 & 1), flip(c, k & 1)) for k in range(1, NDEV)]


SEM_SPEC = pl.BlockSpec(memory_space=pltpu.SEMAPHORE)
START_EFFECT = pltpu.SideEffectType.DATAFLOW_SIDE_EFFECTING


def _send_start(arrays, scatter, name):
    n = len(arrays)
    lands = [lax.empty((NDEV,) + (a.shape[1:] if scatter else a.shape), a.dtype) for a in arrays]

    def body(*refs):
        xs, ls = refs[:n], refs[n:2 * n]
        send_sems, recv_sems, own_sems, token = refs[2 * n], refs[2 * n + 1], refs[2 * n + 2], refs[4 * n + 3]
        x, y, c = lax.axis_index("x"), lax.axis_index("y"), lax.axis_index("c")
        me = 4 * x + 2 * y + c
        for k, peer in enumerate(_other_devices(x, y, c)):
            for p in range(n):
                src = xs[p].at[4 * peer[0] + 2 * peer[1] + peer[2]] if scatter else xs[p]
                pltpu.make_async_remote_copy(
                    src_ref=src, dst_ref=ls[p].at[me], send_sem=send_sems.at[p * (NDEV - 1) + k],
                    recv_sem=recv_sems.at[p * (NDEV - 1) + k], device_id=peer, device_id_type=MESH_T).start()
        for p in range(n):
            pltpu.make_async_copy(xs[p].at[me] if scatter else xs[p], ls[p].at[me], own_sems.at[p]).start()
        token[...] = jnp.zeros(token.shape, F32)

    sems = pltpu.SemaphoreType.DMA((n * (NDEV - 1),))
    out_shape = ([sems, sems, pltpu.SemaphoreType.DMA((n,))]
                 + [pltpu.HBM(a.shape, a.dtype) for a in list(arrays) + lands] + [jax.ShapeDtypeStruct((8, 128), F32)])
    res = pl.pallas_call(
        body, name=name, out_shape=out_shape, in_specs=[HBM_SPEC] * (2 * n),
        out_specs=[SEM_SPEC] * 3 + [HBM_SPEC] * (2 * n) + [pl.BlockSpec(memory_space=pltpu.VMEM)],
        input_output_aliases={i: 3 + i for i in range(2 * n)},
        compiler_params=pltpu.CompilerParams(has_side_effects=START_EFFECT))(
            *[pltpu.with_memory_space_constraint(a, pltpu.HBM) for a in list(arrays) + lands])
    return res[:3], res[3:3 + n], res[3 + n:3 + 2 * n], res[3 + 2 * n]


def _send_wait(started, scatter, after, name):
    sems, arrays, lands, _ = started
    n = len(arrays)

    def body(*refs):
        xs, ls = refs[:n], refs[n:2 * n]
        send, recv, own = refs[2 * n:2 * n + 3]
        x, y, c = lax.axis_index("x"), lax.axis_index("y"), lax.axis_index("c")
        me = 4 * x + 2 * y + c
        for p in range(n):
            pltpu.make_async_copy(xs[p].at[me] if scatter else xs[p], ls[p].at[me], own.at[p]).wait()
        for k, peer in enumerate(_other_devices(x, y, c)):
            slot = 4 * peer[0] + 2 * peer[1] + peer[2]
            for p in range(n):
                cp = pltpu.make_async_remote_copy(
                    src_ref=xs[p].at[slot] if scatter else xs[p], dst_ref=ls[p].at[slot],
                    send_sem=send.at[p * (NDEV - 1) + k], recv_sem=recv.at[p * (NDEV - 1) + k], device_id=peer,
                    device_id_type=MESH_T)
                cp.wait_send()
                cp.wait_recv()

    res = pl.pallas_call(
        body, name=name, out_shape=[pltpu.HBM(a.shape, a.dtype) for a in list(arrays) + list(lands)],
        in_specs=[HBM_SPEC] * (2 * n) + [SEM_SPEC] * 3 + [pl.BlockSpec(memory_space=pl.ANY)],
        out_specs=[HBM_SPEC] * (2 * n), input_output_aliases={i: i for i in range(2 * n)},
        compiler_params=pltpu.CompilerParams(has_side_effects=START_EFFECT))(
            *arrays, *lands, *sems, after)
    return res[n:]


def _row_block(rows, cols, itemsize=4, target=2**20):
    best = None
    for tr in range(16, rows + 1, 16):
        if rows % tr == 0 and tr * cols * itemsize <= target:
            best = tr
    return best or rows


def _adamw(w, m, v, gparts, name):
    parts, rows, cols = gparts.shape
    tr = _row_block(rows, cols, target=2**19)
    bc1 = 1.0 - ADAM_B1 ** ADAM_STEP
    bc2 = 1.0 - ADAM_B2 ** ADAM_STEP

    def body(w_ref, m_ref, v_ref, g_ref, go_ref, d_ref, mo_ref, vo_ref):
        g = g_ref[0].astype(F32)
        for k in range(1, parts):
            g = g + g_ref[k].astype(F32)
        mn = ADAM_B1 * m_ref[...] + (1.0 - ADAM_B1) * g
        vn = ADAM_B2 * v_ref[...] + (1.0 - ADAM_B2) * (g * g)
        go_ref[...] = g
        mo_ref[...] = mn
        vo_ref[...] = vn
        d_ref[...] = -ADAM_LR * ((mn / bc1) / (jnp.sqrt(vn / bc2) + ADAM_EPS) + ADAM_WD * w_ref[...])

    blk = pl.BlockSpec((tr, cols), lambda i: (i, 0))
    shp = jax.ShapeDtypeStruct((rows, cols), F32)
    return pl.pallas_call(
        body, name=name, grid=(rows // tr,),
        in_specs=[blk, blk, blk, pl.BlockSpec((parts, tr, cols), lambda i: (0, i, 0))],
        out_specs=[blk] * 4, out_shape=[shp] * 4, compiler_params=_cp(("parallel",)))(w, m, v, gparts)


def _pack(arrays, rows):
    flat = jnp.concatenate([a.reshape(-1).astype(F32) for a in arrays])
    return jnp.pad(flat, (0, rows * PACK_COLS - flat.shape[0])).reshape(rows, PACK_COLS)


def _unpack(buf, shapes):
    flat = buf.reshape(-1)
    out, off = [], 0
    for shp in shapes:
        size = int(np.prod(shp))
        out.append(flat[off:off + size].reshape(shp))
        off += size
    return out


REPL_SHAPES = {'mix_norm_g': (2, 1024), 'ffn_norm_g': (2, 1024), 'final_norm_g': (1024,), 's5_lam_re': (1, 16, 64),
               's5_lam_im': (1, 16, 64), 's5_log_dt': (1, 16), 's5_b_re': (1, 16, 64, 16), 's5_b_im': (1, 16, 64, 16),
               's5_c_re': (1, 16, 16, 64), 's5_c_im': (1, 16, 16, 64), 's5_d': (1, 256), 's5_b_glu': (1, 256),
               'gm_w_s': (1, 6, 128, 128), 'gm_b_s': (1, 6, 128), 'gm_v_g': (1, 768), 'ffn_conv_b': (2, 5632)}
REPL_ELEMS = sum(int(np.prod(REPL_SHAPES[n])) for n in REPL_ORDER)
REPL_ROWS = -(-REPL_ELEMS // (PACK_COLS * 8)) * 8

GATHER_DTYPE = {'ev_w_in': BF, 'ev_w_out': BF, 's5_w_glu': BF, 'od_w_in': BF, 'od_conv_w': F32, 'od_conv_b': F32,
                'od_w_out': BF, 'ffn_w_up': BF, 'ffn_conv_w': F32, 'ffn_w_down': BF}
GATHER_EVEN = ['ev_w_in', 'ev_w_out', 's5_w_glu']
GATHER_FFN0 = ['ffn_w_up0', 'ffn_conv_w0', 'ffn_w_down0']
GATHER_REST = ['od_w_in', 'od_conv_w', 'od_conv_b', 'od_w_out', 'ffn_w_up1', 'ffn_conv_w1', 'ffn_w_down1']

def _squeeze_lead(a):
    return a.reshape(a.shape[1:]) if a.shape[0] == 1 and a.ndim > 2 else a


def kernel(x, mix_norm_g, ffn_norm_g, final_norm_g, ev_w_in, ev_w_out, s5_lam_re, s5_lam_im, s5_log_dt, s5_b_re, s5_b_im, s5_c_re, s5_c_im, s5_d, s5_w_glu, s5_b_glu, gm_w_s, gm_b_s, gm_v_g, od_w_in, od_conv_w, od_conv_b, od_w_out, ffn_w_up, ffn_conv_w, ffn_conv_b, ffn_w_down, loss_target, m_mix_norm_g, m_ffn_norm_g, m_final_norm_g, m_ev_w_in, m_ev_w_out, m_s5_lam_re, m_s5_lam_im, m_s5_log_dt, m_s5_b_re, m_s5_b_im, m_s5_c_re, m_s5_c_im, m_s5_d, m_s5_w_glu, m_s5_b_glu, m_gm_w_s, m_gm_b_s, m_gm_v_g, m_od_w_in, m_od_conv_w, m_od_conv_b, m_od_w_out, m_ffn_w_up, m_ffn_conv_w, m_ffn_conv_b, m_ffn_w_down, v_mix_norm_g, v_ffn_norm_g, v_final_norm_g, v_ev_w_in, v_ev_w_out, v_s5_lam_re, v_s5_lam_im, v_s5_log_dt, v_s5_b_re, v_s5_b_im, v_s5_c_re, v_s5_c_im, v_s5_d, v_s5_w_glu, v_s5_b_glu, v_gm_w_s, v_gm_b_s, v_gm_v_g, v_od_w_in, v_od_conv_w, v_od_conv_b, v_od_w_out, v_ffn_w_up, v_ffn_conv_w, v_ffn_conv_b, v_ffn_w_down):
    given = dict(locals())
    weights = {n: given[n] for n in WEIGHT_ORDER}
    nseq, seq, _ = x.shape

    send = {}
    for name in SHARDED_ORDER:
        a = weights[name].astype(GATHER_DTYPE[name])
        if a.shape[0] == 2:
            send[name + '0'], send[name + '1'] = a[0], a[1]
        else:
            send[name] = _squeeze_lead(a)
    gathers = [_send_start([send[n] for n in names], False, f"gather_{tag}_start")
               for tag, names in (("ffn0", GATHER_FFN0), ("rest", GATHER_REST))]
    token = gathers[0][3] + gathers[1][3]

    def waiter(tag, names, started):
        return lambda after: dict(zip(names, _send_wait(started, False, after, f"gather_{tag}_wait")))

    gathered = dict(zip(GATHER_EVEN, _all_gather([send[n] for n in GATHER_EVEN], [0] * len(GATHER_EVEN),
                                                 "gather_even")))

    scatters = []

    def scatter(tag, grads):
        names = list(grads)
        started = _send_start([grads[n].astype(BF) for n in names], True, f"{tag}_start")
        scatters.append((tag, names, started))
        return started[3]

    loss_row, grad_x, g_repl = _local_step(
        x.reshape(nseq * seq, D), loss_target.reshape(nseq * seq, D), weights, gathered,
        waiter("ffn0", GATHER_FFN0, gathers[0]), waiter("rest", GATHER_REST, gathers[1]), token, scatter, seq)
    loss = lax.psum(loss_row[0, 0], ("x", "y", "c"))

    parts = {}
    for tag, names, started in scatters:
        parts.update(zip(names, _send_wait(started, True, grad_x, f"{tag}_wait")))
    repl_parts = _all_gather([_pack([g_repl[n] for n in REPL_ORDER], REPL_ROWS)], [0], "gather_small_grads")[0]

    out = {}
    for name in SHARDED_ORDER:
        w = weights[name]
        if name + '0' in parts:
            gp = jnp.stack([parts[name + '0'], parts[name + '1']], axis=1)
        else:
            gp = parts[name]
        to_rows = lambda a: a.reshape(-1, w.shape[-1])
        res = _adamw(to_rows(w), to_rows(given["m_" + name]), to_rows(given["v_" + name]),
                     gp.reshape(NDEV, -1, w.shape[-1]), f"adamw_{name}")
        out[name] = [r.reshape(w.shape) for r in res]
    rp = _adamw(_pack([weights[n] for n in REPL_ORDER], REPL_ROWS),
                _pack([given["m_" + n] for n in REPL_ORDER], REPL_ROWS),
                _pack([given["v_" + n] for n in REPL_ORDER], REPL_ROWS), repl_parts, "adamw_replicated")
    rp_shapes = [weights[n].shape for n in REPL_ORDER]
    for k in range(4):
        for name, a in zip(REPL_ORDER, _unpack(rp[k], rp_shapes)):
            out.setdefault(name, [None] * 4)[k] = a
    results = [[out[n][k] for n in WEIGHT_ORDER] for k in range(4)]
    grad_w, delta_w, new_m, new_v = results
    return (loss, grad_x.reshape(nseq, seq, D), *grad_w, *delta_w, *new_m, *new_v)
```

```python
import math

import jax
import jax.numpy as jnp
import numpy as np
from jax import lax
from jax.experimental import pallas as pl
from jax.experimental.pallas import tpu as pltpu

F32 = jnp.float32
BF = jnp.bfloat16

D = 1024
DFF = 2816
NDEV = 8
SSM_W = 256
SSM_G = 16
SSM_H = 16
SSM_P = 64
NST = SSM_G * SSM_P
GM_W = 768
GM_HEADS = 6
CHUNK = 128
EPS = 1e-6
LAM_MAX = -1e-4
FB = 256
FSH = 2 * DFF // NDEV
ROW_BLOCK = 512
VMEM_LIMIT = 48 * 2**20
PACK_COLS = 1024
MESH_T = pl.DeviceIdType.MESH

ADAM_LR = 0.001
ADAM_B1 = 0.9
ADAM_B2 = 0.999
ADAM_EPS = 1e-08
ADAM_WD = 0.01
ADAM_STEP = 10

WEIGHT_ORDER = ['mix_norm_g', 'ffn_norm_g', 'final_norm_g', 'ev_w_in', 'ev_w_out', 's5_lam_re', 's5_lam_im',
                's5_log_dt', 's5_b_re', 's5_b_im', 's5_c_re', 's5_c_im', 's5_d', 's5_w_glu', 's5_b_glu', 'gm_w_s',
                'gm_b_s', 'gm_v_g', 'od_w_in', 'od_conv_w', 'od_conv_b', 'od_w_out', 'ffn_w_up', 'ffn_conv_w',
                'ffn_conv_b', 'ffn_w_down']
SHARDED = {'ev_w_in': ((1, 1024, 1792), 2), 'ev_w_out': ((1, 1024, 1024), 1), 's5_w_glu': ((1, 256, 256), 1),
           'od_w_in': ((1, 1024, 3072), 2), 'od_conv_w': ((1, 3, 1024), 2), 'od_conv_b': ((1, 1024), 1),
           'od_w_out': ((1, 1024, 1024), 1), 'ffn_w_up': ((2, 1024, 5632), 2), 'ffn_conv_w': ((2, 3, 5632), 2),
           'ffn_w_down': ((2, 2816, 1024), 1)}
SHARDED_ORDER = [n for n in WEIGHT_ORDER if n in SHARDED]
REPL_ORDER = [n for n in WEIGHT_ORDER if n not in SHARDED]


def _cp(sem):
    return pltpu.CompilerParams(dimension_semantics=sem, vmem_limit_bytes=VMEM_LIMIT)


def _sigmoid(x):
    return 1.0 / (1.0 + jnp.exp(-x))


_GELU_K = math.sqrt(2.0 / math.pi)


def _gelu(x):
    return 0.5 * x * (1.0 + jnp.tanh(_GELU_K * (x + 0.044715 * x * x * x)))


def _gelu_grad(x):
    t = jnp.tanh(_GELU_K * (x + 0.044715 * x * x * x))
    return 0.5 * (1.0 + t) + 0.5 * x * (1.0 - t * t) * _GELU_K * (1.0 + 3.0 * 0.044715 * x * x)


def _colsum(x):
    return jnp.sum(x, axis=0, keepdims=True)


def _accumulate(ref, first, part):
    @pl.when(first)
    def _():
        ref[...] = part

    @pl.when(jnp.logical_not(first))
    def _():
        ref[...] += part


_DIMS = {'nn': (((1,), (0,)), ((), ())), 'nt': (((1,), (1,)), ((), ())), 'tn': (((0,), (0,)), ((), ()))}


def _matmul(a, b, mode, tm, tn, tk, name, resid=None, out_dtype=F32):
    if mode == 'tn':
        kdim, m = a.shape
    else:
        m, kdim = a.shape
    n = b.shape[0] if mode == 'nt' else b.shape[1]
    tm, tn, tk = min(tm, m), min(tn, n), min(tk, kdim)
    assert m % tm == 0 and n % tn == 0 and kdim % tk == 0, (name, m, n, kdim, tm, tn, tk)
    a_spec = (pl.BlockSpec((tk, tm), lambda i, j, k: (k, i)) if mode == 'tn'
              else pl.BlockSpec((tm, tk), lambda i, j, k: (i, k)))
    b_spec = (pl.BlockSpec((tn, tk), lambda i, j, k: (j, k)) if mode == 'nt'
              else pl.BlockSpec((tk, tn), lambda i, j, k: (k, j)))
    o_spec = pl.BlockSpec((tm, tn), lambda i, j, k: (i, j))
    return _matmul_spec(a, b, mode, (m // tm, n // tn, kdim // tk), a_spec, b_spec, o_spec, (m, n), name,
                        resid=resid, out_dtype=out_dtype)


def _matmul_spec(a, b, mode, grid, a_spec, b_spec, o_spec, out_shape, name, resid=None, out_dtype=F32):
    nk = grid[2]
    tm, tn = o_spec.block_shape[-2:]
    dims = _DIMS[mode]
    has_resid = resid is not None

    def body(*refs):
        if has_resid:
            a_ref, b_ref, r_ref, o_ref = refs[:4]
        else:
            a_ref, b_ref, o_ref = refs[:3]
            r_ref = None
        part = lax.dot_general(a_ref[...].astype(BF), b_ref[...].astype(BF), dims, preferred_element_type=F32)
        if nk == 1:
            if has_resid:
                part = part + r_ref[...]
            o_ref[...] = part.astype(out_dtype)
        else:
            acc = refs[-1]
            k = pl.program_id(2)

            @pl.when(k == 0)
            def _():
                acc[...] = part

            @pl.when(k > 0)
            def _():
                acc[...] += part

            @pl.when(k == nk - 1)
            def _():
                tot = acc[...]
                if has_resid:
                    tot = tot + r_ref[...]
                o_ref[...] = tot.astype(out_dtype)

    operands = [a, b] + ([resid] if has_resid else [])
    in_specs = [a_spec, b_spec] + ([o_spec] if has_resid else [])
    return pl.pallas_call(
        body, name=name, grid=grid, in_specs=in_specs, out_specs=o_spec,
        out_shape=jax.ShapeDtypeStruct(out_shape, out_dtype),
        scratch_shapes=[pltpu.VMEM((tm, tn), F32)] if nk > 1 else [],
        compiler_params=_cp(("parallel", "parallel", "arbitrary")))(*operands)


def _matmul_shards(a, b, mode, tm, tn, name, resid=None, out_dtype=F32):
    shards, m, kdim = a.shape
    n = b.shape[2] if mode == 'nn' else b.shape[1]
    tm, tn = min(tm, m), min(tn, n)
    dims = _DIMS[mode]
    has_resid = resid is not None

    def body(*refs):
        a_ref, b_ref = refs[:2]
        acc = lax.dot_general(a_ref[0], b_ref[0], dims, preferred_element_type=F32)
        for s in range(1, shards):
            acc = acc + lax.dot_general(a_ref[s], b_ref[s], dims, preferred_element_type=F32)
        if has_resid:
            acc = acc + refs[2][...]
        refs[-1][...] = acc.astype(out_dtype)

    b_spec = (pl.BlockSpec((shards, kdim, tn), lambda i, j: (0, 0, j)) if mode == 'nn'
              else pl.BlockSpec((shards, tn, kdim), lambda i, j: (0, j, 0)))
    o_spec = pl.BlockSpec((tm, tn), lambda i, j: (i, j))
    return pl.pallas_call(
        body, name=name, grid=(m // tm, n // tn),
        in_specs=[pl.BlockSpec((shards, tm, kdim), lambda i, j: (0, i, 0)), b_spec] + ([o_spec] if has_resid else []),
        out_specs=o_spec, out_shape=jax.ShapeDtypeStruct((m, n), out_dtype),
        compiler_params=_cp(("parallel", "parallel")))(*([a, b] + ([resid] if has_resid else [])))


def _rmsnorm_fwd(x, g, name):
    n = x.shape[0]
    tm = min(512, n)

    def body(x_ref, g_ref, o_ref):
        xv = x_ref[...]
        r = lax.rsqrt(jnp.mean(xv * xv, axis=-1, keepdims=True) + EPS)
        o_ref[...] = (xv * r * g_ref[...]).astype(BF)

    return pl.pallas_call(
        body, name=name, grid=(n // tm,),
        in_specs=[pl.BlockSpec((tm, D), lambda i: (i, 0)), pl.BlockSpec((1, D), lambda i: (0, 0))],
        out_specs=pl.BlockSpec((tm, D), lambda i: (i, 0)),
        out_shape=jax.ShapeDtypeStruct((n, D), BF), compiler_params=_cp(("parallel",)))(x, g)


def _rmsnorm_bwd(x, g, dy, dres, name):
    n = x.shape[0]
    tm = min(512, n)

    def body(x_ref, g_ref, dy_ref, dr_ref, dx_ref, dxb_ref, dg_ref):
        xv = x_ref[...]
        r = lax.rsqrt(jnp.mean(xv * xv, axis=-1, keepdims=True) + EPS)
        xh = xv * r
        dyv = dy_ref[...]
        dyg = dyv * g_ref[...]
        dx = dr_ref[...] + r * (dyg - xh * jnp.mean(dyg * xh, axis=-1, keepdims=True))
        dx_ref[...] = dx
        dxb_ref[...] = dx.astype(BF)
        _accumulate(dg_ref, pl.program_id(0) == 0, _colsum(dyv * xh))

    row = pl.BlockSpec((tm, D), lambda i: (i, 0))
    vec = pl.BlockSpec((1, D), lambda i: (0, 0))
    return pl.pallas_call(
        body, name=name, grid=(n // tm,), in_specs=[row, vec, row, row], out_specs=[row, row, vec],
        out_shape=[jax.ShapeDtypeStruct((n, D), F32), jax.ShapeDtypeStruct((n, D), BF),
                   jax.ShapeDtypeStruct((1, D), F32)],
        compiler_params=_cp(("arbitrary",)))(x, g, dy, dres)


def _final_loss(h, g, tgt, name):
    n = h.shape[0]
    tm = min(512, n)

    def body(x_ref, g_ref, t_ref, loss_ref, dx_ref, dxb_ref, dg_ref):
        first = pl.program_id(0) == 0
        xv = x_ref[...]
        gv = g_ref[...]
        r = lax.rsqrt(jnp.mean(xv * xv, axis=-1, keepdims=True) + EPS)
        xh = xv * r
        err = xh * gv - t_ref[...]
        part = 0.5 * jnp.sum(jnp.mean(err * err, axis=-1, keepdims=True), axis=0, keepdims=True)
        _accumulate(loss_ref, first, jnp.broadcast_to(part, (1, 128)))
        dyv = err * (1.0 / D)
        dyg = dyv * gv
        dx = r * (dyg - xh * jnp.mean(dyg * xh, axis=-1, keepdims=True))
        dx_ref[...] = dx
        dxb_ref[...] = dx.astype(BF)
        _accumulate(dg_ref, first, _colsum(dyv * xh))

    row = pl.BlockSpec((tm, D), lambda i: (i, 0))
    vec = pl.BlockSpec((1, D), lambda i: (0, 0))
    return pl.pallas_call(
        body, name=name, grid=(n // tm,), in_specs=[row, vec, row],
        out_specs=[pl.BlockSpec((1, 128), lambda i: (0, 0)), row, row, vec],
        out_shape=[jax.ShapeDtypeStruct((1, 128), F32), jax.ShapeDtypeStruct((n, D), F32),
                   jax.ShapeDtypeStruct((n, D), BF), jax.ShapeDtypeStruct((1, D), F32)],
        compiler_params=_cp(("arbitrary",)))(h, g, tgt)


def _prev_rows(x, halo_ref, lanes, scale, row):
    h7 = halo_ref[7:8, lanes] * scale
    h6 = halo_ref[6:7, lanes] * scale
    p1 = jnp.where(row == 0, h7, pltpu.roll(x, 1, 0))
    p2 = jnp.where(row == 0, h6, jnp.where(row == 1, h7, pltpu.roll(x, 2, 0)))
    return p1, p2


def _halo_maps(tm, n_rows):
    r8 = tm // 8
    last = n_rows // 8 - 1
    prev = lambda i: jnp.maximum(i * r8 - 1, 0)
    nxt = lambda i: jnp.minimum((i + 1) * r8, last)
    return prev, nxt


def _lane_blocks(width):
    return [slice(lo, min(lo + 128, width)) for lo in range(0, width, 128)]


def _conv_taps(w_ref, b_ref, g, lanes):
    return w_ref[g, 0:1, lanes], w_ref[g, 1:2, lanes], w_ref[g, 2:3, lanes], b_ref[g, :, lanes]


def _conv_tile(x, prev1, prev2, taps, row):
    w0, w1, w2, b = taps
    r1 = pltpu.roll(x, 1, 0)
    r2 = pltpu.roll(x, 2, 0)
    x1 = jnp.where(row == 0, prev1, r1)
    x2 = jnp.where(row < 2, prev2, r2)
    return b + w0 * x2 + w1 * x1 + w2 * x, x1, x2, r1, r2


def _halo16_maps(tm, n_rows):
    r16 = tm // 16
    last = n_rows // 16 - 1
    return (lambda i: jnp.maximum(i * r16 - 1, 0)), (lambda i: jnp.minimum((i + 1) * r16, last))


def _ffn_conv_fwd(up, cw, cb, seq, name):
    n = up.shape[2]
    tm = min(ROW_BLOCK, seq)
    prev, _ = _halo16_maps(tm, n)

    def body(u_ref, h_ref, w_ref, b_ref, o_ref, d_ref):
        i = pl.program_id(1)
        scale = jnp.where(lax.rem(i * tm, seq) == 0, 0.0, 1.0)
        for lanes in _lane_blocks(FSH):
            lw = lanes.stop - lanes.start
            row = lax.broadcasted_iota(jnp.int32, (8, lw), 0)
            taps = [_conv_taps(w_ref, b_ref, g, lanes) for g in range(2)]

            def tile(xs, carry):
                hc, nxt = [], []
                for g in range(2):
                    conv, _, _, r1, r2 = _conv_tile(xs[g], carry[2 * g], carry[2 * g + 1], taps[g], row)
                    hc.append(conv)
                    nxt += [r1, r2]
                s = _sigmoid(hc[0])
                silu = hc[0] * s
                return (silu * hc[1], hc[1] * (s * (1.0 + hc[0] * (1.0 - s))), silu), tuple(nxt)

            carry = []
            for g in range(2):
                halo = h_ref[g, :, lanes].astype(F32)[8:] * scale
                carry += [pltpu.roll(halo, 1, 0), pltpu.roll(halo, 2, 0)]
            carry = tuple(carry)
            for m in range(tm // 16):
                rows = slice(m * 16, m * 16 + 16)
                x16 = [u_ref[g, rows, lanes].astype(F32) for g in range(2)]
                a, carry = tile([x[:8] for x in x16], carry)
                b, carry = tile([x[8:] for x in x16], carry)
                o_ref[rows, lanes] = jnp.concatenate([a[0], b[0]], axis=0).astype(BF)
                d_ref[0, rows, lanes] = jnp.concatenate([a[1], b[1]], axis=0).astype(BF)
                d_ref[1, rows, lanes] = jnp.concatenate([a[2], b[2]], axis=0).astype(BF)

    return pl.pallas_call(
        body, name=name, grid=(4, n // tm),
        in_specs=[pl.BlockSpec((2, None, tm, FSH), lambda j, i: (0, j, i, 0)),
                  pl.BlockSpec((2, None, 16, FSH), lambda j, i: (0, j, prev(i), 0)),
                  pl.BlockSpec((2, None, 3, FSH), lambda j, i: (0, j, 0, 0)),
                  pl.BlockSpec((2, None, 1, FSH), lambda j, i: (0, j, 0, 0))],
        out_specs=[pl.BlockSpec((None, tm, FSH), lambda j, i: (j, i, 0)),
                   pl.BlockSpec((2, None, tm, FSH), lambda j, i: (0, j, i, 0))],
        out_shape=[jax.ShapeDtypeStruct((4, n, FSH), BF), jax.ShapeDtypeStruct((2, 4, n, FSH), BF)],
        compiler_params=_cp(("parallel", "parallel")))(up, up, cw, cb)


def _ffn_conv_bwd(up, dgate, dact, cw, seq, name):
    n = up.shape[2]
    tm = min(ROW_BLOCK, seq)
    _, nxt = _halo16_maps(tm, n)

    def body(u_ref, g_ref, gn_ref, da_ref, dn_ref, w_ref, du_ref, dw_ref, db_ref):
        i = pl.program_id(1)
        sn = jnp.where(lax.rem((i + 1) * tm, seq) == 0, 0.0, 1.0)
        first = i == 0
        for lanes in _lane_blocks(FSH):
            lw = lanes.stop - lanes.start
            row = lax.broadcasted_iota(jnp.int32, (8, lw), 0)
            taps = [(w_ref[g, 0:1, lanes], w_ref[g, 1:2, lanes], w_ref[g, 2:3, lanes]) for g in range(2)]

            def dconv(gs, da):
                ds = [gs[g] * da for g in range(2)]
                return [(d, pltpu.roll(d, 7, 0), pltpu.roll(d, 6, 0)) for d in ds]

            def finish(cur, after, xs, sums):
                dups, new_sums = [], []
                for g in range(2):
                    w0, w1, w2 = taps[g]
                    s1 = jnp.where(row == 7, after[g][1], cur[g][1])
                    s2 = jnp.where(row >= 6, after[g][2], cur[g][2])
                    dups.append(w2 * cur[g][0] + w1 * s1 + w0 * s2)
                    acc = sums[g]
                    new_sums.append((acc[0] + xs[g] * s2, acc[1] + xs[g] * s1, acc[2] + xs[g] * cur[g][0],
                                     acc[3] + cur[g][0]))
                return dups, new_sums

            def emit(m, held, after, sums):
                (ta, xa), (tb, xb) = held
                dup_a, sums = finish(ta, tb, xa, sums)
                dup_b, sums = finish(tb, after, xb, sums)
                for g in range(2):
                    du_ref[g, m * 16:m * 16 + 16, lanes] = jnp.concatenate([dup_a[g], dup_b[g]], axis=0).astype(BF)
                return sums

            zero = jnp.zeros((8, lw), F32)
            sums = [(zero,) * 4, (zero,) * 4]
            held = None
            for m in range(tm // 16):
                rows = slice(m * 16, m * 16 + 16)
                x16 = [u_ref[g, rows, lanes].astype(F32) for g in range(2)]
                g16 = [g_ref[g, rows, lanes].astype(F32) for g in range(2)]
                d16 = da_ref[rows, lanes].astype(F32)
                ta = dconv([a[:8] for a in g16], d16[:8])
                tb = dconv([a[8:] for a in g16], d16[8:])
                if held is not None:
                    sums = emit(m - 1, held, ta, sums)
                held = ((ta, [x[:8] for x in x16]), (tb, [x[8:] for x in x16]))
            tn_ = dconv([gn_ref[g, :, lanes].astype(F32)[:8] for g in range(2)], dn_ref[:, lanes].astype(F32)[:8] * sn)
            sums = emit(tm // 16 - 1, held, tn_, sums)
            for g in range(2):
                for k in range(3):
                    _accumulate(dw_ref.at[g, k:k + 1, lanes], first, _colsum(sums[g][k]))
                _accumulate(db_ref.at[g, :, lanes], first, _colsum(sums[g][3]))

    return pl.pallas_call(
        body, name=name, grid=(4, n // tm),
        in_specs=[pl.BlockSpec((2, None, tm, FSH), lambda j, i: (0, j, i, 0)),
                  pl.BlockSpec((2, None, tm, FSH), lambda j, i: (0, j, i, 0)),
                  pl.BlockSpec((2, None, 16, FSH), lambda j, i: (0, j, nxt(i), 0)),
                  pl.BlockSpec((None, tm, FSH), lambda j, i: (j, i, 0)),
                  pl.BlockSpec((None, 16, FSH), lambda j, i: (j, nxt(i), 0)),
                  pl.BlockSpec((2, None, 3, FSH), lambda j, i: (0, j, 0, 0))],
        out_specs=[pl.BlockSpec((2, None, tm, FSH), lambda j, i: (0, j, i, 0)),
                   pl.BlockSpec((2, None, 3, FSH), lambda j, i: (0, j, 0, 0)),
                   pl.BlockSpec((2, None, 1, FSH), lambda j, i: (0, j, 0, 0))],
        out_shape=[jax.ShapeDtypeStruct((2, 4, n, FSH), BF), jax.ShapeDtypeStruct((2, 4, 3, FSH), F32),
                   jax.ShapeDtypeStruct((2, 4, 1, FSH), F32)],
        compiler_params=_cp(("parallel", "arbitrary")))(up, dgate, dgate, dact, dact, cw)


def _shortconv_fwd(p, cw, cb, seq, name):
    n = p.shape[0]
    tm = min(ROW_BLOCK, seq)
    prev, _ = _halo_maps(tm, n)

    def body(p_ref, h_ref, w_ref, b_ref, o_ref):
        i = pl.program_id(1)
        scale = jnp.where(lax.rem(i * tm, seq) == 0, 0.0, 1.0)
        q = p_ref[:, FB:2 * FB] * p_ref[:, 2 * FB:]
        row = lax.broadcasted_iota(jnp.int32, q.shape, 0)
        h7 = h_ref[7:8, FB:2 * FB] * h_ref[7:8, 2 * FB:] * scale
        h6 = h_ref[6:7, FB:2 * FB] * h_ref[6:7, 2 * FB:] * scale
        p1 = jnp.where(row == 0, h7, pltpu.roll(q, 1, 0))
        p2 = jnp.where(row == 0, h6, jnp.where(row == 1, h7, pltpu.roll(q, 2, 0)))
        conv = b_ref[...] + w_ref[0:1, :] * p2 + w_ref[1:2, :] * p1 + w_ref[2:3, :] * q
        o_ref[...] = (p_ref[:, :FB] * conv).astype(BF)

    return pl.pallas_call(
        body, name=name, grid=(D // FB, n // tm),
        in_specs=[pl.BlockSpec((tm, 3 * FB), lambda j, i: (i, j)),
                  pl.BlockSpec((8, 3 * FB), lambda j, i: (prev(i), j)),
                  pl.BlockSpec((3, FB), lambda j, i: (0, j)),
                  pl.BlockSpec((1, FB), lambda j, i: (0, j))],
        out_specs=pl.BlockSpec((tm, FB), lambda j, i: (i, j)),
        out_shape=jax.ShapeDtypeStruct((n, D), BF), compiler_params=_cp(("parallel", "parallel")))(p, p, cw, cb)


def _shortconv_bwd(p, dmix, cw, cb, seq, name):
    n = p.shape[0]
    tm = min(ROW_BLOCK, seq)
    ext = tm + 16
    prev, nxt = _halo_maps(tm, n)

    def body(p_ref, pp_ref, pn_ref, dm_ref, dn_ref, w_ref, b_ref, dp_ref, dw_ref, db_ref, qx, cx):
        i = pl.program_id(1)
        sp = jnp.where(lax.rem(i * tm, seq) == 0, 0.0, 1.0)
        sn = jnp.where(lax.rem((i + 1) * tm, seq) == 0, 0.0, 1.0)
        bg, cg, hx = p_ref[:, :FB], p_ref[:, FB:2 * FB], p_ref[:, 2 * FB:]
        dm = dm_ref[...]
        qx[0:8, :] = pp_ref[:, FB:2 * FB] * pp_ref[:, 2 * FB:] * sp
        qx[8:8 + tm, :] = cg * hx
        qx[8 + tm:, :] = jnp.zeros((8, FB), F32)
        cx[0:8, :] = jnp.zeros((8, FB), F32)
        cx[8:8 + tm, :] = dm * bg
        cx[8 + tm:, :] = dn_ref[...] * pn_ref[:, :FB] * sn
        q0 = qx[...]
        q1 = pltpu.roll(q0, 1, 0)
        q2 = pltpu.roll(q0, 2, 0)
        main = slice(8, 8 + tm)
        conv = b_ref[...] + w_ref[0:1, :] * q2[main] + w_ref[1:2, :] * q1[main] + w_ref[2:3, :] * q0[main]
        dc = cx[...]
        dq = (w_ref[2:3, :] * dc + w_ref[1:2, :] * pltpu.roll(dc, ext - 1, 0)
              + w_ref[0:1, :] * pltpu.roll(dc, ext - 2, 0))[main]
        dp_ref[:, :FB] = (dm * conv).astype(BF)
        dp_ref[:, FB:2 * FB] = (dq * hx).astype(BF)
        dp_ref[:, 2 * FB:] = (dq * cg).astype(BF)
        first = i == 0
        dcm = dc[main]
        _accumulate(dw_ref.at[0:1, :], first, _colsum(dcm * q2[main]))
        _accumulate(dw_ref.at[1:2, :], first, _colsum(dcm * q1[main]))
        _accumulate(dw_ref.at[2:3, :], first, _colsum(dcm * q0[main]))
        _accumulate(db_ref, first, _colsum(dcm))

    return pl.pallas_call(
        body, name=name, grid=(D // FB, n // tm),
        in_specs=[pl.BlockSpec((tm, 3 * FB), lambda j, i: (i, j)),
                  pl.BlockSpec((8, 3 * FB), lambda j, i: (prev(i), j)),
                  pl.BlockSpec((8, 3 * FB), lambda j, i: (nxt(i), j)),
                  pl.BlockSpec((tm, FB), lambda j, i: (i, j)),
                  pl.BlockSpec((8, FB), lambda j, i: (nxt(i), j)),
                  pl.BlockSpec((3, FB), lambda j, i: (0, j)),
                  pl.BlockSpec((1, FB), lambda j, i: (0, j))],
        out_specs=[pl.BlockSpec((tm, 3 * FB), lambda j, i: (i, j)),
                   pl.BlockSpec((3, FB), lambda j, i: (0, j)),
                   pl.BlockSpec((1, FB), lambda j, i: (0, j))],
        out_shape=[jax.ShapeDtypeStruct((n, 3 * D), BF), jax.ShapeDtypeStruct((3, D), F32),
                   jax.ShapeDtypeStruct((1, D), F32)],
        scratch_shapes=[pltpu.VMEM((ext, FB), F32), pltpu.VMEM((ext, FB), F32)],
        compiler_params=_cp(("parallel", "arbitrary")))(p, p, p, dmix, dmix, cw, cb)


def _gmlp_fwd(uv, wm, bst, gv, seq, name):
    n = uv.shape[0]
    tm = min(ROW_BLOCK, seq)

    def body(x_ref, w_ref, b_ref, g_ref, o_ref):
        ge_v = _gelu(x_ref[:, GM_W:])
        r = lax.rsqrt(jnp.mean(ge_v * ge_v, axis=-1, keepdims=True) + EPS)
        vn = (ge_v * r * g_ref[...]).astype(BF)
        for c in range(tm // CHUNK):
            rows = slice(c * CHUNK, (c + 1) * CHUNK)
            for h in range(GM_HEADS):
                cols = slice(h * CHUNK, (h + 1) * CHUNK)
                gate = jnp.dot(w_ref[h], vn[rows, cols], preferred_element_type=F32) + b_ref[:, h:h + 1]
                o_ref[rows, cols] = (_gelu(x_ref[rows, cols]) * gate).astype(BF)

    return pl.pallas_call(
        body, name=name, grid=(n // tm,),
        in_specs=[pl.BlockSpec((tm, 2 * GM_W), lambda i: (i, 0)),
                  pl.BlockSpec((GM_HEADS, CHUNK, CHUNK), lambda i: (0, 0, 0)),
                  pl.BlockSpec((CHUNK, GM_HEADS), lambda i: (0, 0)),
                  pl.BlockSpec((1, GM_W), lambda i: (0, 0))],
        out_specs=pl.BlockSpec((tm, GM_W), lambda i: (i, 0)),
        out_shape=jax.ShapeDtypeStruct((n, GM_W), BF), compiler_params=_cp(("parallel",)))(uv, wm, bst, gv)


def _gmlp_bwd(uv, dout, wm, wmt, bst, gv, seq, name):
    n = uv.shape[0]
    tm = min(ROW_BLOCK, seq)

    def body(x_ref, do_ref, w_ref, wt_ref, b_ref, g_ref, dx_ref, dw_ref, db_ref, dg_ref, dvn_scr):
        first = pl.program_id(0) == 0
        ge_v = _gelu(x_ref[:, GM_W:])
        r = lax.rsqrt(jnp.mean(ge_v * ge_v, axis=-1, keepdims=True) + EPS)
        vh = ge_v * r
        vn = (vh * g_ref[...]).astype(BF)
        tril = (lax.broadcasted_iota(jnp.int32, (CHUNK, CHUNK), 0)
                >= lax.broadcasted_iota(jnp.int32, (CHUNK, CHUNK), 1))
        for h in range(GM_HEADS):
            cols = slice(h * CHUNK, (h + 1) * CHUNK)
            dw = jnp.zeros((CHUNK, CHUNK), F32)
            dbs = jnp.zeros((CHUNK, 1), F32)
            for c in range(tm // CHUNK):
                rows = slice(c * CHUNK, (c + 1) * CHUNK)
                blk = vn[rows, cols]
                gate = jnp.dot(w_ref[h], blk, preferred_element_type=F32) + b_ref[:, h:h + 1]
                xu = x_ref[rows, cols]
                do = do_ref[rows, cols]
                dx_ref[rows, cols] = (do * gate * _gelu_grad(xu)).astype(BF)
                dgate = do * _gelu(xu)
                dgb = dgate.astype(BF)
                dw = dw + lax.dot_general(dgb, blk, _DIMS['nt'], preferred_element_type=F32)
                dbs = dbs + jnp.sum(dgate, axis=1, keepdims=True)
                dvn_scr[rows, cols] = jnp.dot(wt_ref[h], dgb, preferred_element_type=F32)
            _accumulate(dw_ref.at[h], first, jnp.where(tril, dw, 0.0))
            _accumulate(db_ref.at[h], first, dbs)
        dvn = dvn_scr[...]
        _accumulate(dg_ref, first, _colsum(dvn * vh))
        dvh = dvn * g_ref[...]
        dv = r * (dvh - vh * jnp.mean(dvh * vh, axis=-1, keepdims=True))
        dx_ref[:, GM_W:] = (dv * _gelu_grad(x_ref[:, GM_W:])).astype(BF)

    full3 = pl.BlockSpec((GM_HEADS, CHUNK, CHUNK), lambda i: (0, 0, 0))
    return pl.pallas_call(
        body, name=name, grid=(n // tm,),
        in_specs=[pl.BlockSpec((tm, 2 * GM_W), lambda i: (i, 0)), pl.BlockSpec((tm, GM_W), lambda i: (i, 0)),
                  full3, full3, pl.BlockSpec((CHUNK, GM_HEADS), lambda i: (0, 0)),
                  pl.BlockSpec((1, GM_W), lambda i: (0, 0))],
        out_specs=[pl.BlockSpec((tm, 2 * GM_W), lambda i: (i, 0)), full3,
                   pl.BlockSpec((GM_HEADS, CHUNK, 1), lambda i: (0, 0, 0)),
                   pl.BlockSpec((1, GM_W), lambda i: (0, 0))],
        out_shape=[jax.ShapeDtypeStruct((n, 2 * GM_W), BF), jax.ShapeDtypeStruct((GM_HEADS, CHUNK, CHUNK), F32),
                   jax.ShapeDtypeStruct((GM_HEADS, CHUNK, 1), F32), jax.ShapeDtypeStruct((1, GM_W), F32)],
        scratch_shapes=[pltpu.VMEM((tm, GM_W), F32)],
        compiler_params=_cp(("arbitrary",)))(uv, dout, wm, wmt, bst, gv)


def _s5_disc(lam_re, lam_im, log_dt, b_re, b_im):
    lr = jnp.minimum(lam_re, LAM_MAX)
    li = lam_im
    dt = jnp.exp(log_dt)
    mag = jnp.exp(lr * dt)
    ab_re = mag * jnp.cos(li * dt)
    ab_im = mag * jnp.sin(li * dt)
    den = lr * lr + li * li
    nr = ab_re - 1.0
    ni = ab_im
    z_re = (nr * lr + ni * li) / den
    z_im = (ni * lr - nr * li) / den
    return ab_re, ab_im, z_re * b_re - z_im * b_im, z_re * b_im + z_im * b_re


def _s5_disc_fwd(args, name):
    shp = jax.ShapeDtypeStruct(args[0].shape, F32)

    def body(*refs):
        outs = _s5_disc(*[r[...] for r in refs[:5]])
        for o_ref, o in zip(refs[5:], outs):
            o_ref[...] = o

    return pl.pallas_call(body, name=name, out_shape=[shp] * 4)(*args)


def _s5_disc_bwd(args, cts, name):
    shp = jax.ShapeDtypeStruct(args[0].shape, F32)

    def body(*refs):
        _, vjp = jax.vjp(_s5_disc, *[r[...] for r in refs[:5]])
        grads = vjp(tuple(r[...] for r in refs[5:9]))
        for o_ref, o in zip(refs[9:], grads):
            o_ref[...] = o

    return pl.pallas_call(body, name=name, out_shape=[shp] * 5)(*args, *cts)


def _cmul(a, b):
    return a[0] * b[0] - a[1] * b[1], a[0] * b[1] + a[1] * b[0]


def _scan_tables(ar, ai, reverse):
    if reverse:
        ai = -ai
    a1 = (ar, ai)
    a2 = _cmul(a1, a1)
    a3 = _cmul(a2, a1)
    a4 = _cmul(a2, a2)
    powers = [a1, a2, a3, a4, _cmul(a4, a1), _cmul(a4, a2), _cmul(a4, a3), _cmul(a4, a4)]
    row = lax.broadcasted_iota(jnp.int32, (8, NST), 0)
    zero = jnp.zeros((8, NST), F32)
    pr, pi = zero, zero
    for r in range(8):
        pw = powers[7 - r] if reverse else powers[r]
        pr = jnp.where(row == r, pw[0], pr)
        pi = jnp.where(row == r, pw[1], pi)
    levels = []
    for d, pw in ((1, a1), (2, a2), (4, a4)):
        ok = (row <= 7 - d) if reverse else (row >= d)
        levels.append((d, jnp.where(ok, pw[0], zero), jnp.where(ok, pw[1], zero)))
    return (pr, pi), levels


def _scan_block(src, dst, car, tables, n_tiles, reverse):
    (pr, pi), levels = tables
    row = lax.broadcasted_iota(jnp.int32, (8, NST), 0)
    out_row = 0 if reverse else 7

    def step(t, carry):
        cr, ci = carry
        tile = (n_tiles - 1 - t) if reverse else t
        rows = pl.ds(pl.multiple_of(tile * 8, 8), 8)
        xr = src[rows, 0:NST]
        xi = src[rows, NST:2 * NST]
        for d, dr, di in levels:
            shift = 8 - d if reverse else d
            rr = pltpu.roll(xr, shift, 0)
            ri = pltpu.roll(xi, shift, 0)
            xr, xi = xr + dr * rr - di * ri, xi + dr * ri + di * rr
        hr = xr + pr * cr - pi * ci
        hi = xi + pr * ci + pi * cr
        dst[rows, 0:NST] = hr
        dst[rows, NST:2 * NST] = hi
        return (_colsum(jnp.where(row == out_row, hr, 0.0)), _colsum(jnp.where(row == out_row, hi, 0.0)))

    cr, ci = lax.fori_loop(0, n_tiles, step, (car[0:1, 0:NST], car[0:1, NST:2 * NST]))
    car[0:1, 0:NST] = cr
    car[0:1, NST:2 * NST] = ci


def _s5_fwd(u, ab, bbt, cmat, dvec, wglu, bglu, seq, name):
    n = u.shape[0]
    tm = min(ROW_BLOCK, seq)

    def body(u_ref, ab_ref, bb_ref, c_ref, d_ref, w_ref, b_ref, h_ref, o_ref, xs, car):
        i = pl.program_id(0)

        @pl.when(lax.rem(i * tm, seq) == 0)
        def _():
            car[...] = jnp.zeros(car.shape, F32)

        uv = u_ref[...]
        xs[...] = jnp.dot(uv.astype(BF), bb_ref[...], preferred_element_type=F32)
        tables = _scan_tables(ab_ref[0:1, 0:NST], ab_ref[0:1, NST:2 * NST], False)
        _scan_block(xs, h_ref, car, tables, tm // 8, False)
        y = jnp.dot(h_ref[...].astype(BF), c_ref[...], preferred_element_type=F32) + d_ref[...] * uv
        g1 = _gelu(y)
        z = jnp.dot(g1.astype(BF), w_ref[...], preferred_element_type=F32) + b_ref[...]
        o_ref[...] = (g1 * _sigmoid(z)).astype(BF)

    const = lambda shape: pl.BlockSpec(shape, lambda i: (0, 0))
    return pl.pallas_call(
        body, name=name, grid=(n // tm,),
        in_specs=[pl.BlockSpec((tm, SSM_W), lambda i: (i, 0)), const((1, 2 * NST)), const((SSM_W, 2 * NST)),
                  const((2 * NST, SSM_W)), const((1, SSM_W)), const((SSM_W, SSM_W)), const((1, SSM_W))],
        out_specs=[pl.BlockSpec((tm, 2 * NST), lambda i: (i, 0)), pl.BlockSpec((tm, SSM_W), lambda i: (i, 0))],
        out_shape=[jax.ShapeDtypeStruct((n, 2 * NST), F32), jax.ShapeDtypeStruct((n, SSM_W), BF)],
        scratch_shapes=[pltpu.VMEM((tm, 2 * NST), F32), pltpu.VMEM((8, 2 * NST), F32)],
        compiler_params=_cp(("arbitrary",)))(u, ab, bbt, cmat, dvec, wglu, bglu)


def _s5_bwd(da, u, hst, ab, bbt, cmat, dvec, wglu, bglu, seq, name):
    n = u.shape[0]
    tm = min(ROW_BLOCK, seq)
    nb = n // tm
    blk = lambda r: nb - 1 - r
    prev, _ = _halo_maps(tm, n)

    def body(da_ref, u_ref, h_ref, hp_ref, ab_ref, bb_ref, c_ref, d_ref, w_ref, b_ref,
             du_ref, dw_ref, dbg_ref, dd_ref, dc_ref, dbb_ref, dab_ref, gs, car):
        r = pl.program_id(0)
        i = blk(r)
        first = r == 0

        @pl.when(lax.rem((i + 1) * tm, seq) == 0)
        def _():
            car[...] = jnp.zeros(car.shape, F32)

        uv = u_ref[...]
        dav = da_ref[...]
        hb = h_ref[...]
        hb16 = hb.astype(BF)
        dvv = d_ref[...]
        y = jnp.dot(hb16, c_ref[...], preferred_element_type=F32) + dvv * uv
        g1 = _gelu(y)
        g16 = g1.astype(BF)
        s = _sigmoid(jnp.dot(g16, w_ref[...], preferred_element_type=F32) + b_ref[...])
        dz = dav * g1 * s * (1.0 - s)
        dz16 = dz.astype(BF)
        dg1 = dav * s + lax.dot_general(dz16, w_ref[...], _DIMS['nt'], preferred_element_type=F32)
        _accumulate(dw_ref, first, lax.dot_general(g16, dz16, _DIMS['tn'], preferred_element_type=F32))
        _accumulate(dbg_ref, first, _colsum(dz))
        dy = dg1 * _gelu_grad(y)
        dy16 = dy.astype(BF)
        _accumulate(dd_ref, first, _colsum(dy * uv))
        _accumulate(dc_ref, first, lax.dot_general(hb16, dy16, _DIMS['tn'], preferred_element_type=F32))
        gs[...] = lax.dot_general(dy16, c_ref[...], _DIMS['nt'], preferred_element_type=F32)
        tables = _scan_tables(ab_ref[0:1, 0:NST], ab_ref[0:1, NST:2 * NST], True)
        _scan_block(gs, gs, car, tables, tm // 8, True)
        g = gs[...]
        g16b = g.astype(BF)
        sp = jnp.where(lax.rem(i * tm, seq) == 0, 0.0, 1.0)
        row = lax.broadcasted_iota(jnp.int32, hb.shape, 0)
        hprev = jnp.where(row == 0, hp_ref[7:8, :] * sp, pltpu.roll(hb, 1, 0))
        gr, gi = g[:, :NST], g[:, NST:]
        hr, hi = hprev[:, :NST], hprev[:, NST:]
        _accumulate(dab_ref.at[:, 0:NST], first, _colsum(gr * hr + gi * hi))
        _accumulate(dab_ref.at[:, NST:2 * NST], first, _colsum(gi * hr - gr * hi))
        _accumulate(dbb_ref, first, lax.dot_general(uv.astype(BF), g16b, _DIMS['tn'], preferred_element_type=F32))
        du = dy * dvv + lax.dot_general(g16b, bb_ref[...], _DIMS['nt'], preferred_element_type=F32)
        du_ref[...] = du.astype(BF)

    const = lambda shape: pl.BlockSpec(shape, lambda r: (0, 0))
    rowspec = lambda w: pl.BlockSpec((tm, w), lambda r: (blk(r), 0))
    return pl.pallas_call(
        body, name=name, grid=(nb,),
        in_specs=[rowspec(SSM_W), rowspec(SSM_W), rowspec(2 * NST),
                  pl.BlockSpec((8, 2 * NST), lambda r: (prev(blk(r)), 0)),
                  const((1, 2 * NST)), const((SSM_W, 2 * NST)), const((2 * NST, SSM_W)), const((1, SSM_W)),
                  const((SSM_W, SSM_W)), const((1, SSM_W))],
        out_specs=[rowspec(SSM_W), const((SSM_W, SSM_W)), const((1, SSM_W)), const((1, SSM_W)),
                   const((2 * NST, SSM_W)), const((SSM_W, 2 * NST)), const((1, 2 * NST))],
        out_shape=[jax.ShapeDtypeStruct((n, SSM_W), BF), jax.ShapeDtypeStruct((SSM_W, SSM_W), F32),
                   jax.ShapeDtypeStruct((1, SSM_W), F32), jax.ShapeDtypeStruct((1, SSM_W), F32),
                   jax.ShapeDtypeStruct((2 * NST, SSM_W), F32), jax.ShapeDtypeStruct((SSM_W, 2 * NST), F32),
                   jax.ShapeDtypeStruct((1, 2 * NST), F32)],
        scratch_shapes=[pltpu.VMEM((tm, 2 * NST), F32), pltpu.VMEM((8, 2 * NST), F32)],
        compiler_params=_cp(("arbitrary",)))(da, u, hst, hst, ab, bbt, cmat, dvec, wglu, bglu)


def _s5_rows(lam_re, lam_im, log_dt, b_re, b_im):
    rep = lambda a: jnp.broadcast_to(a[:, None, :], (SSM_G, SSM_H, SSM_P)).reshape(SSM_W, SSM_P)
    dt = jnp.broadcast_to(log_dt[:, None, None], (SSM_G, SSM_H, SSM_P)).reshape(SSM_W, SSM_P)
    tr = lambda b: b.transpose(0, 2, 1).reshape(SSM_W, SSM_P)
    return rep(lam_re), rep(lam_im), dt, tr(b_re), tr(b_im)


def _block_diag(rows_gp, inner):
    eye = jnp.eye(SSM_G, dtype=rows_gp.dtype)
    return (rows_gp[:, :, None, :] * eye[:, None, :, None]).reshape(SSM_G * inner, SSM_G * SSM_P)


def _diag_blocks(mat, inner):
    m4 = mat.reshape(SSM_G, inner, SSM_G, SSM_P)
    return jnp.stack([m4[g, :, g, :] for g in range(SSM_G)])


def _interleave(w, parts):
    lead = w.shape[:-1]
    nb = w.shape[-1] // (parts * FB)
    return jnp.swapaxes(w.reshape(lead + (parts, nb, FB)), -3, -2).reshape(w.shape)


def _deinterleave(w, parts):
    lead = w.shape[:-1]
    nb = w.shape[-1] // (parts * FB)
    return jnp.swapaxes(w.reshape(lead + (nb, parts, FB)), -3, -2).reshape(w.shape)


def _ffn_fwd(h, g, w_up, w_down, cw, cb, seq, tag):
    n = h.shape[0]
    tm = min(1024, n)
    ni = n // tm
    f = _rmsnorm_fwd(h, g, f"{tag}_norm")
    up = _matmul_spec(
        f, w_up, 'nn', (NDEV, ni, 1),
        pl.BlockSpec((tm, D), lambda s, i, k: (i, 0)),
        pl.BlockSpec((D, FSH), lambda s, i, k: (s, 0)),
        pl.BlockSpec((tm, FSH), lambda s, i, k: (s * ni + i, 0)), (NDEV * n, FSH), f"{tag}_up", out_dtype=BF)
    up = up.reshape(2, 4, n, FSH)
    act, dgate = _ffn_conv_fwd(up, cw, cb, seq, f"{tag}_conv")
    out = _matmul_shards(act, w_down.reshape(4, FSH, D), 'nn', 1024, 512, f"{tag}_down", resid=h)
    return out, (f, up, act, dgate)


def _ffn_bwd(dh, dhb, h, g, w_up, w_down, cw, cb, saved, seq, tag):
    f, up, act, dgate = saved
    n = h.shape[0]
    tm = min(1024, n)
    ni = n // tm
    tk = min(4096, n)
    nk = n // tk
    dact = _matmul_spec(
        dhb, w_down, 'nt', (4, ni, 1),
        pl.BlockSpec((tm, D), lambda j, i, k: (i, 0)),
        pl.BlockSpec((FSH, D), lambda j, i, k: (j, 0)),
        pl.BlockSpec((tm, FSH), lambda j, i, k: (j * ni + i, 0)), (4 * n, FSH), f"{tag}_ddown_x", out_dtype=BF)
    tn = 512
    dw_down = _matmul_spec(
        act.reshape(4 * n, FSH), dhb, 'tn', (4, D // tn, nk),
        pl.BlockSpec((tk, FSH), lambda j, c, k: (j * nk + k, 0)),
        pl.BlockSpec((tk, tn), lambda j, c, k: (k, c)),
        pl.BlockSpec((FSH, tn), lambda j, c, k: (j, c)), (DFF, D), f"{tag}_ddown_w", out_dtype=BF)
    dup, dcw, dcb = _ffn_conv_bwd(up, dgate, dact.reshape(4, n, FSH), cw, seq, f"{tag}_dconv")
    dup2 = dup.reshape(NDEV * n, FSH)
    df = _matmul_shards(dup.reshape(NDEV, n, FSH), w_up.reshape(NDEV, D, FSH), 'nt', 256, D, f"{tag}_dup_x")
    dw_up = _matmul_spec(
        f, dup2, 'tn', (NDEV, 1, nk),
        pl.BlockSpec((tk, D), lambda s, j, k: (k, 0)),
        pl.BlockSpec((tk, FSH), lambda s, j, k: (s * nk + k, 0)),
        pl.BlockSpec((D, FSH), lambda s, j, k: (s, 0)), (NDEV * D, FSH), f"{tag}_dup_w", out_dtype=BF)
    dh_in, dhb_in, dg = _rmsnorm_bwd(h, g, df, dh, f"{tag}_dnorm")
    grads = dict(g=dg, w_up=dw_up.reshape(NDEV, D, FSH), w_down=dw_down.reshape(NDEV, DFF // NDEV, D),
                 cw=dcw.reshape(NDEV, 3, FSH), cb=dcb.reshape(2 * DFF))
    return dh_in, dhb_in, grads


def _col_shards(w, width):
    return w.reshape(w.shape[0], NDEV, width).transpose(1, 0, 2)


def _local_step(x, tgt, w, gw, wait_ffn0, wait_rest, token, scatter, seq):
    bf = lambda a: a.astype(BF)
    row = lambda a: a.reshape(1, -1).astype(F32)
    w_ev = gw['ev_w_in'].transpose(1, 0, 2).reshape(D, 1792)
    w_ev_s5, w_ev_gm = w_ev[:, :SSM_W], w_ev[:, SSM_W:]
    w_evo = gw['ev_w_out'].reshape(D, D)
    f_cb = [w['ffn_conv_b'][l].reshape(2, 4, 1, FSH) for l in range(2)]
    tril = jnp.tril(jnp.ones((CHUNK, CHUNK), dtype=bool))
    gm_w = jnp.where(tril, w['gm_w_s'][0], 0.0)
    gm_wm, gm_wmt = bf(gm_w), bf(jnp.swapaxes(gm_w, 1, 2))
    gm_bt = w['gm_b_s'][0].T
    gm_gv = row(w['gm_v_g'][0])
    s5_in = _s5_rows(w['s5_lam_re'][0], w['s5_lam_im'][0], w['s5_log_dt'][0], w['s5_b_re'][0], w['s5_b_im'][0])
    ab_re, ab_im, bb_re, bb_im = _s5_disc_fwd(s5_in, "s5_disc")
    first_h = lambda a: a.reshape(SSM_G, SSM_H, SSM_P)[:, 0, :].reshape(1, NST)
    s5_ab = jnp.concatenate([first_h(ab_re), first_h(ab_im)], axis=1)
    to_gp = lambda a: a.reshape(SSM_G, SSM_H, SSM_P)
    s5_bbt = bf(jnp.concatenate([_block_diag(to_gp(bb_re), SSM_H), _block_diag(to_gp(bb_im), SSM_H)], axis=1))
    s5_cmat = bf(jnp.concatenate([_block_diag(w['s5_c_re'][0], SSM_H).T, -_block_diag(w['s5_c_im'][0], SSM_H).T],
                                 axis=0))
    s5_d, s5_bg, s5_wg = row(w['s5_d'][0]), row(w['s5_b_glu'][0]), gw['s5_w_glu'].reshape(SSM_W, SSM_W)
    g_mix = [row(w['mix_norm_g'][0]) + token[0:1, 0:1], row(w['mix_norm_g'][1])]
    g_ffn = [row(w['ffn_norm_g'][l]) for l in range(2)]
    g_fin = row(w['final_norm_g'])

    h0 = x
    y0 = _rmsnorm_fwd(h0, g_mix[0], "ev_norm")
    p_s5 = _matmul(y0, w_ev_s5, 'nn', 1024, 256, D, "ev_in_s5")
    p_gm = _matmul(y0, w_ev_gm, 'nn', 1024, 512, D, "ev_in_gm")
    hst, a_out = _s5_fwd(p_s5, s5_ab, s5_bbt, s5_cmat, s5_d, s5_wg, s5_bg, seq, "s5_fwd")
    b_out = _gmlp_fwd(p_gm, gm_wm, gm_bt, gm_gv, seq, "gmlp_fwd")
    mixcat = jnp.concatenate([a_out, b_out], axis=1)
    h1 = _matmul(mixcat, w_evo, 'nn', 1024, 512, D, "ev_out", resid=h0)
    g0 = wait_ffn0(mixcat)
    w_up0, w_dn0 = g0['ffn_w_up0'].reshape(NDEV * D, FSH), g0['ffn_w_down0'].reshape(DFF, D)
    f_cw0 = g0['ffn_conv_w0'].reshape(2, 4, 3, FSH)
    h2, ffn0 = _ffn_fwd(h1, g_ffn[0], w_up0, w_dn0, f_cw0, f_cb[0], seq, "ffn0")
    g1 = wait_rest(h2)
    w_od = _interleave(g1['od_w_in'].transpose(1, 0, 2).reshape(D, 3 * D), 3)
    w_odo = g1['od_w_out'].reshape(D, D)
    od_cw = g1['od_conv_w'].transpose(1, 0, 2).reshape(3, D)
    od_cb = g1['od_conv_b'].reshape(1, D)
    w_up1, w_dn1 = g1['ffn_w_up1'].reshape(NDEV * D, FSH), g1['ffn_w_down1'].reshape(DFF, D)
    f_cw1 = g1['ffn_conv_w1'].reshape(2, 4, 3, FSH)
    y1 = _rmsnorm_fwd(h2, g_mix[1], "od_norm")
    p_od = _matmul(y1, w_od, 'nn', 1024, 512, D, "od_in")
    mixin = _shortconv_fwd(p_od, od_cw, od_cb, seq, "od_conv")
    h3 = _matmul(mixin, w_odo, 'nn', 1024, 512, D, "od_out", resid=h2)
    h4, ffn1 = _ffn_fwd(h3, g_ffn[1], w_up1, w_dn1, f_cw1, f_cb[1], seq, "ffn1")
    loss, dh4, dh4b, dg_fin = _final_loss(h4, g_fin, tgt, "final_loss")

    dh3, dh3b, gf1 = _ffn_bwd(dh4, dh4b, h3, g_ffn[1], w_up1, w_dn1, f_cw1, f_cb[1], ffn1, seq, "ffn1")
    dmixin = _matmul(dh3b, w_odo, 'nt', 1024, 512, D, "od_dout_x")
    dw_odo = _matmul(mixin, dh3b, 'tn', D, 512, 4096, "od_dout_w", out_dtype=BF)
    dp_od, d_od_cw, d_od_cb = _shortconv_bwd(p_od, dmixin, od_cw, od_cb, seq, "od_dconv")
    dy1 = _matmul(dp_od, w_od, 'nt', 256, D, 3 * D, "od_din_x")
    dw_od = _matmul(y1, dp_od, 'tn', D, 512, 4096, "od_din_w", out_dtype=BF)
    sent = scatter("scatter_layer1", {
        'od_w_in': _col_shards(_deinterleave(dw_od, 3), 384), 'od_conv_w': _col_shards(d_od_cw, D // NDEV),
        'od_conv_b': d_od_cb.reshape(NDEV, 1, D // NDEV), 'od_w_out': dw_odo.reshape(NDEV, D // NDEV, D),
        'ffn_w_up1': gf1['w_up'], 'ffn_conv_w1': gf1['cw'], 'ffn_w_down1': gf1['w_down']})
    dh2, dh2b, dg_mix1 = _rmsnorm_bwd(h2, g_mix[1] + sent[0:1, 0:1], dy1, dh3, "od_dnorm")
    dh1, dh1b, gf0 = _ffn_bwd(dh2, dh2b, h1, g_ffn[0], w_up0, w_dn0, f_cw0, f_cb[0], ffn0, seq, "ffn0")
    dmix_a = _matmul(dh1b, w_evo[:SSM_W], 'nt', 1024, SSM_W, D, "ev_dout_xa")
    dmix_b = _matmul(dh1b, w_evo[SSM_W:], 'nt', 1024, GM_W, D, "ev_dout_xb")
    dw_evo = _matmul(mixcat, dh1b, 'tn', D, 512, 4096, "ev_dout_w", out_dtype=BF)
    sent = scatter("scatter_ffn0", {'ffn_w_up0': gf0['w_up'], 'ffn_conv_w0': gf0['cw'], 'ffn_w_down0': gf0['w_down'],
                                    'ev_w_out': dw_evo.reshape(NDEV, D // NDEV, D)})
    dp_s5, d_wg, d_bg, d_d, d_cmat, d_bbt, d_ab = _s5_bwd(dmix_a, p_s5, hst, s5_ab, s5_bbt, s5_cmat,
                                                           s5_d + sent[0:1, 0:1], s5_wg, s5_bg, seq, "s5_bwd")
    dp_gm, d_gmw, d_gmb, d_gmg = _gmlp_bwd(p_gm, dmix_b, gm_wm, gm_wmt, gm_bt, gm_gv, seq, "gmlp_bwd")
    dw_ev = jnp.concatenate([_matmul(y0, dp_s5, 'tn', D, SSM_W, 4096, "ev_din_wa", out_dtype=BF),
                             _matmul(y0, dp_gm, 'tn', D, 512, 4096, "ev_din_wb", out_dtype=BF)], axis=1)
    sent = scatter("scatter_even", {'ev_w_in': _col_shards(dw_ev, 224),
                                    's5_w_glu': d_wg.reshape(NDEV, SSM_W // NDEV, SSM_W)})
    dy0 = _matmul(dp_gm, w_ev_gm, 'nt', 512, D, 2 * GM_W, "ev_din_xb")
    dy0 = _matmul(dp_s5, w_ev_s5, 'nt', 1024, D, SSM_W, "ev_din_xa", resid=dy0)
    grad_x, _, dg_mix0 = _rmsnorm_bwd(h0, g_mix[0] + sent[0:1, 0:1], dy0, dh1, "ev_dnorm")

    put_h0 = lambda a: jnp.zeros((SSM_G, SSM_H, SSM_P), F32).at[:, 0, :].set(a.reshape(SSM_G, SSM_P)).reshape(
        SSM_W, SSM_P)
    ct = (put_h0(d_ab[:, :NST]), put_h0(d_ab[:, NST:]),
          _diag_blocks(d_bbt[:, :NST], SSM_H).reshape(SSM_W, SSM_P),
          _diag_blocks(d_bbt[:, NST:], SSM_H).reshape(SSM_W, SSM_P))
    d_lre, d_lim, d_ldt, d_bre, d_bim = _s5_disc_bwd(s5_in, ct, "s5_ddisc")
    over_h = lambda a: a.reshape(SSM_G, SSM_H, SSM_P).sum(axis=1)
    un_tr = lambda a: a.reshape(SSM_G, SSM_H, SSM_P).transpose(0, 2, 1)
    d_cre = _diag_blocks(d_cmat[:NST].T, SSM_H)
    d_cim = -_diag_blocks(d_cmat[NST:].T, SSM_H)

    repl = {
        'mix_norm_g': jnp.concatenate([dg_mix0, dg_mix1], axis=0),
        'ffn_norm_g': jnp.concatenate([gf0['g'], gf1['g']], axis=0),
        'final_norm_g': dg_fin.reshape(D),
        's5_lam_re': over_h(d_lre)[None], 's5_lam_im': over_h(d_lim)[None],
        's5_log_dt': over_h(d_ldt).sum(axis=1)[None],
        's5_b_re': un_tr(d_bre)[None], 's5_b_im': un_tr(d_bim)[None],
        's5_c_re': d_cre[None], 's5_c_im': d_cim[None],
        's5_d': d_d, 's5_b_glu': d_bg,
        'gm_w_s': d_gmw[None], 'gm_b_s': d_gmb.reshape(1, GM_HEADS, CHUNK), 'gm_v_g': d_gmg,
        'ffn_conv_b': jnp.stack([gf0['cb'], gf1['cb']]),
    }
    return loss, grad_x, repl


HBM_SPEC = pl.BlockSpec(memory_space=pltpu.HBM)


def _at_axis(ref, pos, index):
    return ref.at[(slice(None),) * pos + (index,)]


def _all_gather(shards, positions, name):
    n = len(shards)

    def body(*refs):
        xs, outs = refs[:n], refs[n:2 * n]
        send_sems, recv_sems, local_sems = refs[2 * n:]
        x, y, c = lax.axis_index("x"), lax.axis_index("y"), lax.axis_index("c")
        me, sibling = (x, y, c), (x, y, 1 - c)
        chips = [(1 - x, y), (x, 1 - y), (1 - x, 1 - y)]

        def block(p, dev):
            return _at_axis(outs[p], positions[p], 4 * dev[0] + 2 * dev[1] + dev[2])

        def copy(p, k, dev, to, src=None):
            return pltpu.make_async_remote_copy(
                src_ref=block(p, dev) if src is None else src, dst_ref=block(p, dev),
                send_sem=send_sems.at[p, k], recv_sem=recv_sems.at[p, k], device_id=to, device_id_type=MESH_T)

        mine = [pltpu.make_async_copy(xs[p], block(p, me), local_sems.at[p]) for p in range(n)]
        for cp in mine:
            cp.start()
        first = [copy(p, 0, me, sibling, src=xs[p]) for p in range(n)]
        first += [copy(p, 1 + j, me, (*chip, c), src=xs[p]) for j, chip in enumerate(chips) for p in range(n)]
        for cp in first:
            cp.start()
        passed = []
        for j, chip in enumerate(chips):
            for p in range(n):
                copy(p, 1 + j, (*chip, c), me).wait_recv()
                fwd = copy(p, 4 + j, (*chip, c), sibling)
                fwd.start()
                passed.append(fwd)
        for p in range(n):
            copy(p, 0, sibling, me).wait_recv()
        for j, chip in enumerate(chips):
            for p in range(n):
                copy(p, 4 + j, (*chip, 1 - c), me).wait_recv()
        for cp in first + passed:
            cp.wait_send()
        for cp in mine:
            cp.wait()

    out_shape = [jax.ShapeDtypeStruct(s.shape[:pos] + (NDEV,) + s.shape[pos:], s.dtype)
                 for s, pos in zip(shards, positions)]
    return pl.pallas_call(
        body, name=name, out_shape=out_shape, in_specs=[HBM_SPEC] * n, out_specs=[HBM_SPEC] * n,
        scratch_shapes=[pltpu.SemaphoreType.DMA((n, 7)), pltpu.SemaphoreType.DMA((n, 7)),
                        pltpu.SemaphoreType.DMA((n,))])(*shards)


def _other_devices(x, y, c):
    flip = lambda v, bit: 1 - v if bit else v
    return [(flip(x, k >> 2 & 1), flip(y, k >> 1 & 1), flip(c, k & 1)) for k in range(1, NDEV)]


SEM_SPEC = pl.BlockSpec(memory_space=pltpu.SEMAPHORE)
START_EFFECT = pltpu.SideEffectType.DATAFLOW_SIDE_EFFECTING


def _send_start(arrays, scatter, name):
    n = len(arrays)
    lands = [lax.empty((NDEV,) + (a.shape[1:] if scatter else a.shape), a.dtype) for a in arrays]

    def body(*refs):
        xs, ls = refs[:n], refs[n:2 * n]
        send_sems, recv_sems, own_sems, token = refs[2 * n], refs[2 * n + 1], refs[2 * n + 2], refs[4 * n + 3]
        x, y, c = lax.axis_index("x"), lax.axis_index("y"), lax.axis_index("c")
        me = 4 * x + 2 * y + c
        for k, peer in enumerate(_other_devices(x, y, c)):
            for p in range(n):
                src = xs[p].at[4 * peer[0] + 2 * peer[1] + peer[2]] if scatter else xs[p]
                pltpu.make_async_remote_copy(
                    src_ref=src, dst_ref=ls[p].at[me], send_sem=send_sems.at[p * (NDEV - 1) + k],
                    recv_sem=recv_sems.at[p * (NDEV - 1) + k], device_id=peer, device_id_type=MESH_T).start()
        for p in range(n):
            pltpu.make_async_copy(xs[p].at[me] if scatter else xs[p], ls[p].at[me], own_sems.at[p]).start()
        token[...] = jnp.zeros(token.shape, F32)

    sems = pltpu.SemaphoreType.DMA((n * (NDEV - 1),))
    out_shape = ([sems, sems, pltpu.SemaphoreType.DMA((n,))]
                 + [pltpu.HBM(a.shape, a.dtype) for a in list(arrays) + lands] + [jax.ShapeDtypeStruct((8, 128), F32)])
    res = pl.pallas_call(
        body, name=name, out_shape=out_shape, in_specs=[HBM_SPEC] * (2 * n),
        out_specs=[SEM_SPEC] * 3 + [HBM_SPEC] * (2 * n) + [pl.BlockSpec(memory_space=pltpu.VMEM)],
        input_output_aliases={i: 3 + i for i in range(2 * n)},
        compiler_params=pltpu.CompilerParams(has_side_effects=START_EFFECT))(
            *[pltpu.with_memory_space_constraint(a, pltpu.HBM) for a in list(arrays) + lands])
    return res[:3], res[3:3 + n], res[3 + n:3 + 2 * n], res[3 + 2 * n]


def _send_wait(started, scatter, after, name):
    sems, arrays, lands, _ = started
    n = len(arrays)

    def body(*refs):
        xs, ls = refs[:n], refs[n:2 * n]
        send, recv, own = refs[2 * n:2 * n + 3]
        x, y, c = lax.axis_index("x"), lax.axis_index("y"), lax.axis_index("c")
        me = 4 * x + 2 * y + c
        for p in range(n):
            pltpu.make_async_copy(xs[p].at[me] if scatter else xs[p], ls[p].at[me], own.at[p]).wait()
        for k, peer in enumerate(_other_devices(x, y, c)):
            slot = 4 * peer[0] + 2 * peer[1] + peer[2]
            for p in range(n):
                cp = pltpu.make_async_remote_copy(
                    src_ref=xs[p].at[slot] if scatter else xs[p], dst_ref=ls[p].at[slot],
                    send_sem=send.at[p * (NDEV - 1) + k], recv_sem=recv.at[p * (NDEV - 1) + k], device_id=peer,
                    device_id_type=MESH_T)
                cp.wait_send()
                cp.wait_recv()

    res = pl.pallas_call(
        body, name=name, out_shape=[pltpu.HBM(a.shape, a.dtype) for a in list(arrays) + list(lands)],
        in_specs=[HBM_SPEC] * (2 * n) + [SEM_SPEC] * 3 + [pl.BlockSpec(memory_space=pl.ANY)],
        out_specs=[HBM_SPEC] * (2 * n), input_output_aliases={i: i for i in range(2 * n)},
        compiler_params=pltpu.CompilerParams(has_side_effects=START_EFFECT))(
            *arrays, *lands, *sems, after)
    return res[n:]


def _row_block(rows, cols, itemsize=4, target=2**20):
    best = None
    for tr in range(16, rows + 1, 16):
        if rows % tr == 0 and tr * cols * itemsize <= target:
            best = tr
    return best or rows


def _adamw(w, m, v, gparts, name):
    parts, rows, cols = gparts.shape
    tr = _row_block(rows, cols, target=2**19)
    bc1 = 1.0 - ADAM_B1 ** ADAM_STEP
    bc2 = 1.0 - ADAM_B2 ** ADAM_STEP

    def body(w_ref, m_ref, v_ref, g_ref, go_ref, d_ref, mo_ref, vo_ref):
        g = g_ref[0].astype(F32)
        for k in range(1, parts):
            g = g + g_ref[k].astype(F32)
        mn = ADAM_B1 * m_ref[...] + (1.0 - ADAM_B1) * g
        vn = ADAM_B2 * v_ref[...] + (1.0 - ADAM_B2) * (g * g)
        go_ref[...] = g
        mo_ref[...] = mn
        vo_ref[...] = vn
        d_ref[...] = -ADAM_LR * ((mn / bc1) / (jnp.sqrt(vn / bc2) + ADAM_EPS) + ADAM_WD * w_ref[...])

    blk = pl.BlockSpec((tr, cols), lambda i: (i, 0))
    shp = jax.ShapeDtypeStruct((rows, cols), F32)
    return pl.pallas_call(
        body, name=name, grid=(rows // tr,),
        in_specs=[blk, blk, blk, pl.BlockSpec((parts, tr, cols), lambda i: (0, i, 0))],
        out_specs=[blk] * 4, out_shape=[shp] * 4, compiler_params=_cp(("parallel",)))(w, m, v, gparts)


def _pack(arrays, rows):
    flat = jnp.concatenate([a.reshape(-1).astype(F32) for a in arrays])
    return jnp.pad(flat, (0, rows * PACK_COLS - flat.shape[0])).reshape(rows, PACK_COLS)


def _unpack(buf, shapes):
    flat = buf.reshape(-1)
    out, off = [], 0
    for shp in shapes:
        size = int(np.prod(shp))
        out.append(flat[off:off + size].reshape(shp))
        off += size
    return out


REPL_SHAPES = {'mix_norm_g': (2, 1024), 'ffn_norm_g': (2, 1024), 'final_norm_g': (1024,), 's5_lam_re': (1, 16, 64),
               's5_lam_im': (1, 16, 64), 's5_log_dt': (1, 16), 's5_b_re': (1, 16, 64, 16), 's5_b_im': (1, 16, 64, 16),
               's5_c_re': (1, 16, 16, 64), 's5_c_im': (1, 16, 16, 64), 's5_d': (1, 256), 's5_b_glu': (1, 256),
               'gm_w_s': (1, 6, 128, 128), 'gm_b_s': (1, 6, 128), 'gm_v_g': (1, 768), 'ffn_conv_b': (2, 5632)}
REPL_ELEMS = sum(int(np.prod(REPL_SHAPES[n])) for n in REPL_ORDER)
REPL_ROWS = -(-REPL_ELEMS // (PACK_COLS * 8)) * 8

GATHER_DTYPE = {'ev_w_in': BF, 'ev_w_out': BF, 's5_w_glu': BF, 'od_w_in': BF, 'od_conv_w': F32, 'od_conv_b': F32,
                'od_w_out': BF, 'ffn_w_up': BF, 'ffn_conv_w': F32, 'ffn_w_down': BF}
GATHER_EVEN = ['ev_w_in', 'ev_w_out', 's5_w_glu']
GATHER_FFN0 = ['ffn_w_up0', 'ffn_conv_w0', 'ffn_w_down0']
GATHER_REST = ['od_w_in', 'od_conv_w', 'od_conv_b', 'od_w_out', 'ffn_w_up1', 'ffn_conv_w1', 'ffn_w_down1']

def _squeeze_lead(a):
    return a.reshape(a.shape[1:]) if a.shape[0] == 1 and a.ndim > 2 else a


def kernel(x, mix_norm_g, ffn_norm_g, final_norm_g, ev_w_in, ev_w_out, s5_lam_re, s5_lam_im, s5_log_dt, s5_b_re, s5_b_im, s5_c_re, s5_c_im, s5_d, s5_w_glu, s5_b_glu, gm_w_s, gm_b_s, gm_v_g, od_w_in, od_conv_w, od_conv_b, od_w_out, ffn_w_up, ffn_conv_w, ffn_conv_b, ffn_w_down, loss_target, m_mix_norm_g, m_ffn_norm_g, m_final_norm_g, m_ev_w_in, m_ev_w_out, m_s5_lam_re, m_s5_lam_im, m_s5_log_dt, m_s5_b_re, m_s5_b_im, m_s5_c_re, m_s5_c_im, m_s5_d, m_s5_w_glu, m_s5_b_glu, m_gm_w_s, m_gm_b_s, m_gm_v_g, m_od_w_in, m_od_conv_w, m_od_conv_b, m_od_w_out, m_ffn_w_up, m_ffn_conv_w, m_ffn_conv_b, m_ffn_w_down, v_mix_norm_g, v_ffn_norm_g, v_final_norm_g, v_ev_w_in, v_ev_w_out, v_s5_lam_re, v_s5_lam_im, v_s5_log_dt, v_s5_b_re, v_s5_b_im, v_s5_c_re, v_s5_c_im, v_s5_d, v_s5_w_glu, v_s5_b_glu, v_gm_w_s, v_gm_b_s, v_gm_v_g, v_od_w_in, v_od_conv_w, v_od_conv_b, v_od_w_out, v_ffn_w_up, v_ffn_conv_w, v_ffn_conv_b, v_ffn_w_down):
    given = dict(locals())
    weights = {n: given[n] for n in WEIGHT_ORDER}
    nseq, seq, _ = x.shape

    send = {}
    for name in SHARDED_ORDER:
        a = weights[name].astype(GATHER_DTYPE[name])
        if a.shape[0] == 2:
            send[name + '0'], send[name + '1'] = a[0], a[1]
        else:
            send[name] = _squeeze_lead(a)
    gathers = [_send_start([send[n] for n in names], False, f"gather_{tag}_start")
               for tag, names in (("ffn0", GATHER_FFN0), ("rest", GATHER_REST))]
    token = gathers[0][3] + gathers[1][3]

    def waiter(tag, names, started):
        return lambda after: dict(zip(names, _send_wait(started, False, after, f"gather_{tag}_wait")))

    gathered = dict(zip(GATHER_EVEN, _all_gather([send[n] for n in GATHER_EVEN], [0] * len(GATHER_EVEN),
                                                 "gather_even")))

    scatters = []

    def scatter(tag, grads):
        names = list(grads)
        started = _send_start([grads[n].astype(BF) for n in names], True, f"{tag}_start")
        scatters.append((tag, names, started))
        return started[3]

    loss_row, grad_x, g_repl = _local_step(
        x.reshape(nseq * seq, D), loss_target.reshape(nseq * seq, D), weights, gathered,
        waiter("ffn0", GATHER_FFN0, gathers[0]), waiter("rest", GATHER_REST, gathers[1]), token, scatter, seq)
    loss = lax.psum(loss_row[0, 0], ("x", "y", "c"))

    parts = {}
    for tag, names, started in scatters:
        parts.update(zip(names, _send_wait(started, True, grad_x, f"{tag}_wait")))
    repl_parts = _all_gather([_pack([g_repl[n] for n in REPL_ORDER], REPL_ROWS)], [0], "gather_small_grads")[0]

    out = {}
    for name in SHARDED_ORDER:
        w = weights[name]
        if name + '0' in parts:
            gp = jnp.stack([parts[name + '0'], parts[name + '1']], axis=1)
        else:
            gp = parts[name]
        to_rows = lambda a: a.reshape(-1, w.shape[-1])
        res = _adamw(to_rows(w), to_rows(given["m_" + name]), to_rows(given["v_" + name]),
                     gp.reshape(NDEV, -1, w.shape[-1]), f"adamw_{name}")
        out[name] = [r.reshape(w.shape) for r in res]
    rp = _adamw(_pack([weights[n] for n in REPL_ORDER], REPL_ROWS),
                _pack([given["m_" + n] for n in REPL_ORDER], REPL_ROWS),
                _pack([given["v_" + n] for n in REPL_ORDER], REPL_ROWS), repl_parts, "adamw_replicated")
    rp_shapes = [weights[n].shape for n in REPL_ORDER]
    for k in range(4):
        for name, a in zip(REPL_ORDER, _unpack(rp[k], rp_shapes)):
            out.setdefault(name, [None] * 4)[k] = a
    results = [[out[n][k] for n in WEIGHT_ORDER] for k in range(4)]
    grad_w, delta_w, new_m, new_v = results
    return (loss, grad_x.reshape(nseq, seq, D), *grad_w, *delta_w, *new_m, *new_v)
```

```python
import math

import jax
import jax.numpy as jnp
import numpy as np
from jax import lax
from jax.experimental import pallas as pl
from jax.experimental.pallas import tpu as pltpu

F32 = jnp.float32
BF = jnp.bfloat16

D = 1024
DFF = 2816
NDEV = 8
SSM_W = 256
SSM_G = 16
SSM_H = 16
SSM_P = 64
NST = SSM_G * SSM_P
GM_W = 768
GM_HEADS = 6
CHUNK = 128
EPS = 1e-6
LAM_MAX = -1e-4
FB = 256
FSH = 2 * DFF // NDEV
ROW_BLOCK = 512
CONV_ROW_BLOCK = 1024
VMEM_LIMIT = 48 * 2**20
PACK_COLS = 1024
MESH_T = pl.DeviceIdType.MESH

ADAM_LR = 0.001
ADAM_B1 = 0.9
ADAM_B2 = 0.999
ADAM_EPS = 1e-08
ADAM_WD = 0.01
ADAM_STEP = 10

WEIGHT_ORDER = ['mix_norm_g', 'ffn_norm_g', 'final_norm_g', 'ev_w_in', 'ev_w_out', 's5_lam_re', 's5_lam_im',
                's5_log_dt', 's5_b_re', 's5_b_im', 's5_c_re', 's5_c_im', 's5_d', 's5_w_glu', 's5_b_glu', 'gm_w_s',
                'gm_b_s', 'gm_v_g', 'od_w_in', 'od_conv_w', 'od_conv_b', 'od_w_out', 'ffn_w_up', 'ffn_conv_w',
                'ffn_conv_b', 'ffn_w_down']
SHARDED = {'ev_w_in': ((1, 1024, 1792), 2), 'ev_w_out': ((1, 1024, 1024), 1), 's5_w_glu': ((1, 256, 256), 1),
           'od_w_in': ((1, 1024, 3072), 2), 'od_conv_w': ((1, 3, 1024), 2), 'od_conv_b': ((1, 1024), 1),
           'od_w_out': ((1, 1024, 1024), 1), 'ffn_w_up': ((2, 1024, 5632), 2), 'ffn_conv_w': ((2, 3, 5632), 2),
           'ffn_w_down': ((2, 2816, 1024), 1)}
SHARDED_ORDER = [n for n in WEIGHT_ORDER if n in SHARDED]
REPL_ORDER = [n for n in WEIGHT_ORDER if n not in SHARDED]


def _cp(sem):
    return pltpu.CompilerParams(dimension_semantics=sem, vmem_limit_bytes=VMEM_LIMIT)


def _sigmoid(x):
    return 1.0 / (1.0 + jnp.exp(-x))


_GELU_K = math.sqrt(2.0 / math.pi)


def _gelu(x):
    return 0.5 * x * (1.0 + jnp.tanh(_GELU_K * (x + 0.044715 * x * x * x)))


def _gelu_grad(x):
    t = jnp.tanh(_GELU_K * (x + 0.044715 * x * x * x))
    return 0.5 * (1.0 + t) + 0.5 * x * (1.0 - t * t) * _GELU_K * (1.0 + 3.0 * 0.044715 * x * x)


def _colsum(x):
    return jnp.sum(x, axis=0, keepdims=True)


def _accumulate(ref, first, part):
    @pl.when(first)
    def _():
        ref[...] = part

    @pl.when(jnp.logical_not(first))
    def _():
        ref[...] += part


_DIMS = {'nn': (((1,), (0,)), ((), ())), 'nt': (((1,), (1,)), ((), ())), 'tn': (((0,), (0,)), ((), ()))}


def _matmul(a, b, mode, tm, tn, tk, name, resid=None, out_dtype=F32):
    if mode == 'tn':
        kdim, m = a.shape
    else:
        m, kdim = a.shape
    n = b.shape[0] if mode == 'nt' else b.shape[1]
    tm, tn, tk = min(tm, m), min(tn, n), min(tk, kdim)
    assert m % tm == 0 and n % tn == 0 and kdim % tk == 0, (name, m, n, kdim, tm, tn, tk)
    a_spec = (pl.BlockSpec((tk, tm), lambda i, j, k: (k, i)) if mode == 'tn'
              else pl.BlockSpec((tm, tk), lambda i, j, k: (i, k)))
    b_spec = (pl.BlockSpec((tn, tk), lambda i, j, k: (j, k)) if mode == 'nt'
              else pl.BlockSpec((tk, tn), lambda i, j, k: (k, j)))
    o_spec = pl.BlockSpec((tm, tn), lambda i, j, k: (i, j))
    return _matmul_spec(a, b, mode, (m // tm, n // tn, kdim // tk), a_spec, b_spec, o_spec, (m, n), name,
                        resid=resid, out_dtype=out_dtype)


def _matmul_spec(a, b, mode, grid, a_spec, b_spec, o_spec, out_shape, name, resid=None, out_dtype=F32):
    nk = grid[2]
    tm, tn = o_spec.block_shape[-2:]
    dims = _DIMS[mode]
    has_resid = resid is not None

    def body(*refs):
        if has_resid:
            a_ref, b_ref, r_ref, o_ref = refs[:4]
        else:
            a_ref, b_ref, o_ref = refs[:3]
            r_ref = None
        part = lax.dot_general(a_ref[...].astype(BF), b_ref[...].astype(BF), dims, preferred_element_type=F32)
        if nk == 1:
            if has_resid:
                part = part + r_ref[...]
            o_ref[...] = part.astype(out_dtype)
        else:
            acc = refs[-1]
            k = pl.program_id(2)

            @pl.when(k == 0)
            def _():
                acc[...] = part

            @pl.when(k > 0)
            def _():
                acc[...] += part

            @pl.when(k == nk - 1)
            def _():
                tot = acc[...]
                if has_resid:
                    tot = tot + r_ref[...]
                o_ref[...] = tot.astype(out_dtype)

    operands = [a, b] + ([resid] if has_resid else [])
    in_specs = [a_spec, b_spec] + ([o_spec] if has_resid else [])
    return pl.pallas_call(
        body, name=name, grid=grid, in_specs=in_specs, out_specs=o_spec,
        out_shape=jax.ShapeDtypeStruct(out_shape, out_dtype),
        scratch_shapes=[pltpu.VMEM((tm, tn), F32)] if nk > 1 else [],
        compiler_params=_cp(("parallel", "parallel", "arbitrary")))(*operands)


def _matmul_shards(a, b, mode, tm, tn, name, resid=None, out_dtype=F32):
    shards, m, kdim = a.shape
    n = b.shape[2] if mode == 'nn' else b.shape[1]
    tm, tn = min(tm, m), min(tn, n)
    dims = _DIMS[mode]
    has_resid = resid is not None

    def body(*refs):
        a_ref, b_ref = refs[:2]
        acc = lax.dot_general(a_ref[0], b_ref[0], dims, preferred_element_type=F32)
        for s in range(1, shards):
            acc = acc + lax.dot_general(a_ref[s], b_ref[s], dims, preferred_element_type=F32)
        if has_resid:
            acc = acc + refs[2][...]
        refs[-1][...] = acc.astype(out_dtype)

    b_spec = (pl.BlockSpec((shards, kdim, tn), lambda i, j: (0, 0, j)) if mode == 'nn'
              else pl.BlockSpec((shards, tn, kdim), lambda i, j: (0, j, 0)))
    o_spec = pl.BlockSpec((tm, tn), lambda i, j: (i, j))
    return pl.pallas_call(
        body, name=name, grid=(m // tm, n // tn),
        in_specs=[pl.BlockSpec((shards, tm, kdim), lambda i, j: (0, i, 0)), b_spec] + ([o_spec] if has_resid else []),
        out_specs=o_spec, out_shape=jax.ShapeDtypeStruct((m, n), out_dtype),
        compiler_params=_cp(("parallel", "parallel")))(*([a, b] + ([resid] if has_resid else [])))


def _rmsnorm_fwd(x, g, name):
    n = x.shape[0]
    tm = min(512, n)

    def body(x_ref, g_ref, o_ref):
        xv = x_ref[...]
        r = lax.rsqrt(jnp.mean(xv * xv, axis=-1, keepdims=True) + EPS)
        o_ref[...] = (xv * r * g_ref[...]).astype(BF)

    return pl.pallas_call(
        body, name=name, grid=(n // tm,),
        in_specs=[pl.BlockSpec((tm, D), lambda i: (i, 0)), pl.BlockSpec((1, D), lambda i: (0, 0))],
        out_specs=pl.BlockSpec((tm, D), lambda i: (i, 0)),
        out_shape=jax.ShapeDtypeStruct((n, D), BF), compiler_params=_cp(("parallel",)))(x, g)


def _rmsnorm_bwd(x, g, dy, dres, name):
    n = x.shape[0]
    tm = min(512, n)

    def body(x_ref, g_ref, dy_ref, dr_ref, dx_ref, dxb_ref, dg_ref):
        xv = x_ref[...]
        r = lax.rsqrt(jnp.mean(xv * xv, axis=-1, keepdims=True) + EPS)
        xh = xv * r
        dyv = dy_ref[...]
        dyg = dyv * g_ref[...]
        dx = dr_ref[...] + r * (dyg - xh * jnp.mean(dyg * xh, axis=-1, keepdims=True))
        dx_ref[...] = dx
        dxb_ref[...] = dx.astype(BF)
        _accumulate(dg_ref, pl.program_id(0) == 0, _colsum(dyv * xh))

    row = pl.BlockSpec((tm, D), lambda i: (i, 0))
    vec = pl.BlockSpec((1, D), lambda i: (0, 0))
    return pl.pallas_call(
        body, name=name, grid=(n // tm,), in_specs=[row, vec, row, row], out_specs=[row, row, vec],
        out_shape=[jax.ShapeDtypeStruct((n, D), F32), jax.ShapeDtypeStruct((n, D), BF),
                   jax.ShapeDtypeStruct((1, D), F32)],
        compiler_params=_cp(("arbitrary",)))(x, g, dy, dres)


def _final_loss(h, g, tgt, name):
    n = h.shape[0]
    tm = min(512, n)

    def body(x_ref, g_ref, t_ref, loss_ref, dx_ref, dxb_ref, dg_ref):
        first = pl.program_id(0) == 0
        xv = x_ref[...]
        gv = g_ref[...]
        r = lax.rsqrt(jnp.mean(xv * xv, axis=-1, keepdims=True) + EPS)
        xh = xv * r
        err = xh * gv - t_ref[...]
        part = 0.5 * jnp.sum(jnp.mean(err * err, axis=-1, keepdims=True), axis=0, keepdims=True)
        _accumulate(loss_ref, first, jnp.broadcast_to(part, (1, 128)))
        dyv = err * (1.0 / D)
        dyg = dyv * gv
        dx = r * (dyg - xh * jnp.mean(dyg * xh, axis=-1, keepdims=True))
        dx_ref[...] = dx
        dxb_ref[...] = dx.astype(BF)
        _accumulate(dg_ref, first, _colsum(dyv * xh))

    row = pl.BlockSpec((tm, D), lambda i: (i, 0))
    vec = pl.BlockSpec((1, D), lambda i: (0, 0))
    return pl.pallas_call(
        body, name=name, grid=(n // tm,), in_specs=[row, vec, row],
        out_specs=[pl.BlockSpec((1, 128), lambda i: (0, 0)), row, row, vec],
        out_shape=[jax.ShapeDtypeStruct((1, 128), F32), jax.ShapeDtypeStruct((n, D), F32),
                   jax.ShapeDtypeStruct((n, D), BF), jax.ShapeDtypeStruct((1, D), F32)],
        compiler_params=_cp(("arbitrary",)))(h, g, tgt)


def _prev_rows(x, halo_ref, lanes, scale, row):
    h7 = halo_ref[7:8, lanes] * scale
    h6 = halo_ref[6:7, lanes] * scale
    p1 = jnp.where(row == 0, h7, pltpu.roll(x, 1, 0))
    p2 = jnp.where(row == 0, h6, jnp.where(row == 1, h7, pltpu.roll(x, 2, 0)))
    return p1, p2


def _halo_maps(tm, n_rows):
    r8 = tm // 8
    last = n_rows // 8 - 1
    prev = lambda i: jnp.maximum(i * r8 - 1, 0)
    nxt = lambda i: jnp.minimum((i + 1) * r8, last)
    return prev, nxt


def _lane_blocks(width):
    return [slice(lo, min(lo + 128, width)) for lo in range(0, width, 128)]


def _conv_taps(w_ref, b_ref, g, lanes):
    return w_ref[g, 0:1, lanes], w_ref[g, 1:2, lanes], w_ref[g, 2:3, lanes], b_ref[g, :, lanes]


def _conv_tile(x, prev1, prev2, taps, row):
    w0, w1, w2, b = taps
    r1 = pltpu.roll(x, 1, 0)
    r2 = pltpu.roll(x, 2, 0)
    x1 = jnp.where(row == 0, prev1, r1)
    x2 = jnp.where(row < 2, prev2, r2)
    return b + w0 * x2 + w1 * x1 + w2 * x, x1, x2, r1, r2


def _halo16_maps(tm, n_rows):
    r16 = tm // 16
    last = n_rows // 16 - 1
    return (lambda i: jnp.maximum(i * r16 - 1, 0)), (lambda i: jnp.minimum((i + 1) * r16, last))


def _ffn_conv_fwd(up, cw, cb, seq, name):
    n = up.shape[2]
    tm = min(CONV_ROW_BLOCK, seq)
    prev, _ = _halo16_maps(tm, n)

    def body(u_ref, h_ref, w_ref, b_ref, o_ref, d_ref):
        i = pl.program_id(1)
        scale = jnp.where(lax.rem(i * tm, seq) == 0, 0.0, 1.0)
        for lanes in _lane_blocks(FSH):
            lw = lanes.stop - lanes.start
            row = lax.broadcasted_iota(jnp.int32, (8, lw), 0)
            taps = [_conv_taps(w_ref, b_ref, g, lanes) for g in range(2)]

            def tile(xs, carry):
                hc, nxt = [], []
                for g in range(2):
                    conv, _, _, r1, r2 = _conv_tile(xs[g], carry[2 * g], carry[2 * g + 1], taps[g], row)
                    hc.append(conv)
                    nxt += [r1, r2]
                s = _sigmoid(hc[0])
                silu = hc[0] * s
                return (silu * hc[1], hc[1] * (s * (1.0 + hc[0] * (1.0 - s))), silu), tuple(nxt)

            carry = []
            for g in range(2):
                halo = h_ref[g, :, lanes].astype(F32)[8:] * scale
                carry += [pltpu.roll(halo, 1, 0), pltpu.roll(halo, 2, 0)]
            carry = tuple(carry)
            for m in range(tm // 16):
                rows = slice(m * 16, m * 16 + 16)
                x16 = [u_ref[g, rows, lanes].astype(F32) for g in range(2)]
                a, carry = tile([x[:8] for x in x16], carry)
                b, carry = tile([x[8:] for x in x16], carry)
                o_ref[rows, lanes] = jnp.concatenate([a[0], b[0]], axis=0).astype(BF)
                d_ref[0, rows, lanes] = jnp.concatenate([a[1], b[1]], axis=0).astype(BF)
                d_ref[1, rows, lanes] = jnp.concatenate([a[2], b[2]], axis=0).astype(BF)

    return pl.pallas_call(
        body, name=name, grid=(4, n // tm),
        in_specs=[pl.BlockSpec((2, None, tm, FSH), lambda j, i: (0, j, i, 0)),
                  pl.BlockSpec((2, None, 16, FSH), lambda j, i: (0, j, prev(i), 0)),
                  pl.BlockSpec((2, None, 3, FSH), lambda j, i: (0, j, 0, 0)),
                  pl.BlockSpec((2, None, 1, FSH), lambda j, i: (0, j, 0, 0))],
        out_specs=[pl.BlockSpec((None, tm, FSH), lambda j, i: (j, i, 0)),
                   pl.BlockSpec((2, None, tm, FSH), lambda j, i: (0, j, i, 0))],
        out_shape=[jax.ShapeDtypeStruct((4, n, FSH), BF), jax.ShapeDtypeStruct((2, 4, n, FSH), BF)],
        compiler_params=_cp(("parallel", "parallel")))(up, up, cw, cb)


def _ffn_conv_bwd(up, dgate, dact, cw, seq, name):
    n = up.shape[2]
    tm = min(CONV_ROW_BLOCK, seq)
    _, nxt = _halo16_maps(tm, n)

    def body(u_ref, g_ref, gn_ref, da_ref, dn_ref, w_ref, du_ref, dw_ref, db_ref):
        i = pl.program_id(1)
        sn = jnp.where(lax.rem((i + 1) * tm, seq) == 0, 0.0, 1.0)
        first = i == 0
        for lanes in _lane_blocks(FSH):
            lw = lanes.stop - lanes.start
            row = lax.broadcasted_iota(jnp.int32, (8, lw), 0)
            taps = [(w_ref[g, 0:1, lanes], w_ref[g, 1:2, lanes], w_ref[g, 2:3, lanes]) for g in range(2)]

            def dconv(gs, da):
                ds = [gs[g] * da for g in range(2)]
                return [(d, pltpu.roll(d, 7, 0), pltpu.roll(d, 6, 0)) for d in ds]

            def finish(cur, after, xs, sums):
                dups, new_sums = [], []
                for g in range(2):
                    w0, w1, w2 = taps[g]
                    s1 = jnp.where(row == 7, after[g][1], cur[g][1])
                    s2 = jnp.where(row >= 6, after[g][2], cur[g][2])
                    dups.append(w2 * cur[g][0] + w1 * s1 + w0 * s2)
                    acc = sums[g]
                    new_sums.append((acc[0] + xs[g] * s2, acc[1] + xs[g] * s1, acc[2] + xs[g] * cur[g][0],
                                     acc[3] + cur[g][0]))
                return dups, new_sums

            def emit(m, held, after, sums):
                (ta, xa), (tb, xb) = held
                dup_a, sums = finish(ta, tb, xa, sums)
                dup_b, sums = finish(tb, after, xb, sums)
                for g in range(2):
                    du_ref[g, m * 16:m * 16 + 16, lanes] = jnp.concatenate([dup_a[g], dup_b[g]], axis=0).astype(BF)
                return sums

            zero = jnp.zeros((8, lw), F32)
            sums = [(zero,) * 4, (zero,) * 4]
            held = None
            for m in range(tm // 16):
                rows = slice(m * 16, m * 16 + 16)
                x16 = [u_ref[g, rows, lanes].astype(F32) for g in range(2)]
                g16 = [g_ref[g, rows, lanes].astype(F32) for g in range(2)]
                d16 = da_ref[rows, lanes].astype(F32)
                ta = dconv([a[:8] for a in g16], d16[:8])
                tb = dconv([a[8:] for a in g16], d16[8:])
                if held is not None:
                    sums = emit(m - 1, held, ta, sums)
                held = ((ta, [x[:8] for x in x16]), (tb, [x[8:] for x in x16]))
            tn_ = dconv([gn_ref[g, :, lanes].astype(F32)[:8] for g in range(2)], dn_ref[:, lanes].astype(F32)[:8] * sn)
            sums = emit(tm // 16 - 1, held, tn_, sums)
            for g in range(2):
                for k in range(3):
                    _accumulate(dw_ref.at[g, k:k + 1, lanes], first, _colsum(sums[g][k]))
                _accumulate(db_ref.at[g, :, lanes], first, _colsum(sums[g][3]))

    return pl.pallas_call(
        body, name=name, grid=(4, n // tm),
        in_specs=[pl.BlockSpec((2, None, tm, FSH), lambda j, i: (0, j, i, 0)),
                  pl.BlockSpec((2, None, tm, FSH), lambda j, i: (0, j, i, 0)),
                  pl.BlockSpec((2, None, 16, FSH), lambda j, i: (0, j, nxt(i), 0)),
                  pl.BlockSpec((None, tm, FSH), lambda j, i: (j, i, 0)),
                  pl.BlockSpec((None, 16, FSH), lambda j, i: (j, nxt(i), 0)),
                  pl.BlockSpec((2, None, 3, FSH), lambda j, i: (0, j, 0, 0))],
        out_specs=[pl.BlockSpec((2, None, tm, FSH), lambda j, i: (0, j, i, 0)),
                   pl.BlockSpec((2, None, 3, FSH), lambda j, i: (0, j, 0, 0)),
                   pl.BlockSpec((2, None, 1, FSH), lambda j, i: (0, j, 0, 0))],
        out_shape=[jax.ShapeDtypeStruct((2, 4, n, FSH), BF), jax.ShapeDtypeStruct((2, 4, 3, FSH), F32),
                   jax.ShapeDtypeStruct((2, 4, 1, FSH), F32)],
        compiler_params=_cp(("parallel", "arbitrary")))(up, dgate, dgate, dact, dact, cw)


def _shortconv_fwd(p, cw, cb, seq, name):
    n = p.shape[0]
    tm = min(CONV_ROW_BLOCK, seq)
    prev, _ = _halo_maps(tm, n)

    def body(p_ref, h_ref, w_ref, b_ref, o_ref):
        i = pl.program_id(1)
        scale = jnp.where(lax.rem(i * tm, seq) == 0, 0.0, 1.0)
        q = p_ref[:, FB:2 * FB] * p_ref[:, 2 * FB:]
        row = lax.broadcasted_iota(jnp.int32, q.shape, 0)
        h7 = h_ref[7:8, FB:2 * FB] * h_ref[7:8, 2 * FB:] * scale
        h6 = h_ref[6:7, FB:2 * FB] * h_ref[6:7, 2 * FB:] * scale
        p1 = jnp.where(row == 0, h7, pltpu.roll(q, 1, 0))
        p2 = jnp.where(row == 0, h6, jnp.where(row == 1, h7, pltpu.roll(q, 2, 0)))
        conv = b_ref[...] + w_ref[0:1, :] * p2 + w_ref[1:2, :] * p1 + w_ref[2:3, :] * q
        o_ref[...] = (p_ref[:, :FB] * conv).astype(BF)

    return pl.pallas_call(
        body, name=name, grid=(D // FB, n // tm),
        in_specs=[pl.BlockSpec((tm, 3 * FB), lambda j, i: (i, j)),
                  pl.BlockSpec((8, 3 * FB), lambda j, i: (prev(i), j)),
                  pl.BlockSpec((3, FB), lambda j, i: (0, j)),
                  pl.BlockSpec((1, FB), lambda j, i: (0, j))],
        out_specs=pl.BlockSpec((tm, FB), lambda j, i: (i, j)),
        out_shape=jax.ShapeDtypeStruct((n, D), BF), compiler_params=_cp(("parallel", "parallel")))(p, p, cw, cb)


def _shortconv_bwd(p, dmix, cw, cb, seq, name):
    n = p.shape[0]
    tm = min(CONV_ROW_BLOCK, seq)
    ext = tm + 16
    prev, nxt = _halo_maps(tm, n)

    def body(p_ref, pp_ref, pn_ref, dm_ref, dn_ref, w_ref, b_ref, dp_ref, dw_ref, db_ref, qx, cx):
        i = pl.program_id(1)
        sp = jnp.where(lax.rem(i * tm, seq) == 0, 0.0, 1.0)
        sn = jnp.where(lax.rem((i + 1) * tm, seq) == 0, 0.0, 1.0)
        bg, cg, hx = p_ref[:, :FB], p_ref[:, FB:2 * FB], p_ref[:, 2 * FB:]
        dm = dm_ref[...]
        qx[0:8, :] = pp_ref[:, FB:2 * FB] * pp_ref[:, 2 * FB:] * sp
        qx[8:8 + tm, :] = cg * hx
        qx[8 + tm:, :] = jnp.zeros((8, FB), F32)
        cx[0:8, :] = jnp.zeros((8, FB), F32)
        cx[8:8 + tm, :] = dm * bg
        cx[8 + tm:, :] = dn_ref[...] * pn_ref[:, :FB] * sn
        q0 = qx[...]
        q1 = pltpu.roll(q0, 1, 0)
        q2 = pltpu.roll(q0, 2, 0)
        main = slice(8, 8 + tm)
        conv = b_ref[...] + w_ref[0:1, :] * q2[main] + w_ref[1:2, :] * q1[main] + w_ref[2:3, :] * q0[main]
        dc = cx[...]
        dq = (w_ref[2:3, :] * dc + w_ref[1:2, :] * pltpu.roll(dc, ext - 1, 0)
              + w_ref[0:1, :] * pltpu.roll(dc, ext - 2, 0))[main]
        dp_ref[:, :FB] = (dm * conv).astype(BF)
        dp_ref[:, FB:2 * FB] = (dq * hx).astype(BF)
        dp_ref[:, 2 * FB:] = (dq * cg).astype(BF)
        first = i == 0
        dcm = dc[main]
        _accumulate(dw_ref.at[0:1, :], first, _colsum(dcm * q2[main]))
        _accumulate(dw_ref.at[1:2, :], first, _colsum(dcm * q1[main]))
        _accumulate(dw_ref.at[2:3, :], first, _colsum(dcm * q0[main]))
        _accumulate(db_ref, first, _colsum(dcm))

    return pl.pallas_call(
        body, name=name, grid=(D // FB, n // tm),
        in_specs=[pl.BlockSpec((tm, 3 * FB), lambda j, i: (i, j)),
                  pl.BlockSpec((8, 3 * FB), lambda j, i: (prev(i), j)),
                  pl.BlockSpec((8, 3 * FB), lambda j, i: (nxt(i), j)),
                  pl.BlockSpec((tm, FB), lambda j, i: (i, j)),
                  pl.BlockSpec((8, FB), lambda j, i: (nxt(i), j)),
                  pl.BlockSpec((3, FB), lambda j, i: (0, j)),
                  pl.BlockSpec((1, FB), lambda j, i: (0, j))],
        out_specs=[pl.BlockSpec((tm, 3 * FB), lambda j, i: (i, j)),
                   pl.BlockSpec((3, FB), lambda j, i: (0, j)),
                   pl.BlockSpec((1, FB), lambda j, i: (0, j))],
        out_shape=[jax.ShapeDtypeStruct((n, 3 * D), BF), jax.ShapeDtypeStruct((3, D), F32),
                   jax.ShapeDtypeStruct((1, D), F32)],
        scratch_shapes=[pltpu.VMEM((ext, FB), F32), pltpu.VMEM((ext, FB), F32)],
        compiler_params=_cp(("parallel", "arbitrary")))(p, p, p, dmix, dmix, cw, cb)


def _gmlp_fwd(uv, wm, bst, gv, seq, name):
    n = uv.shape[0]
    tm = min(ROW_BLOCK, seq)

    def body(x_ref, w_ref, b_ref, g_ref, o_ref):
        ge_v = _gelu(x_ref[:, GM_W:])
        r = lax.rsqrt(jnp.mean(ge_v * ge_v, axis=-1, keepdims=True) + EPS)
        vn = (ge_v * r * g_ref[...]).astype(BF)
        for c in range(tm // CHUNK):
            rows = slice(c * CHUNK, (c + 1) * CHUNK)
            for h in range(GM_HEADS):
                cols = slice(h * CHUNK, (h + 1) * CHUNK)
                gate = jnp.dot(w_ref[h], vn[rows, cols], preferred_element_type=F32) + b_ref[:, h:h + 1]
                o_ref[rows, cols] = (_gelu(x_ref[rows, cols]) * gate).astype(BF)

    return pl.pallas_call(
        body, name=name, grid=(n // tm,),
        in_specs=[pl.BlockSpec((tm, 2 * GM_W), lambda i: (i, 0)),
                  pl.BlockSpec((GM_HEADS, CHUNK, CHUNK), lambda i: (0, 0, 0)),
                  pl.BlockSpec((CHUNK, GM_HEADS), lambda i: (0, 0)),
                  pl.BlockSpec((1, GM_W), lambda i: (0, 0))],
        out_specs=pl.BlockSpec((tm, GM_W), lambda i: (i, 0)),
        out_shape=jax.ShapeDtypeStruct((n, GM_W), BF), compiler_params=_cp(("parallel",)))(uv, wm, bst, gv)


def _gmlp_bwd(uv, dout, wm, wmt, bst, gv, seq, name):
    n = uv.shape[0]
    tm = min(ROW_BLOCK, seq)

    def body(x_ref, do_ref, w_ref, wt_ref, b_ref, g_ref, dx_ref, dw_ref, db_ref, dg_ref, dvn_scr):
        first = pl.program_id(0) == 0
        ge_v = _gelu(x_ref[:, GM_W:])
        r = lax.rsqrt(jnp.mean(ge_v * ge_v, axis=-1, keepdims=True) + EPS)
        vh = ge_v * r
        vn = (vh * g_ref[...]).astype(BF)
        tril = (lax.broadcasted_iota(jnp.int32, (CHUNK, CHUNK), 0)
                >= lax.broadcasted_iota(jnp.int32, (CHUNK, CHUNK), 1))
        for h in range(GM_HEADS):
            cols = slice(h * CHUNK, (h + 1) * CHUNK)
            dw = jnp.zeros((CHUNK, CHUNK), F32)
            dbs = jnp.zeros((CHUNK, 1), F32)
            for c in range(tm // CHUNK):
                rows = slice(c * CHUNK, (c + 1) * CHUNK)
                blk = vn[rows, cols]
                gate = jnp.dot(w_ref[h], blk, preferred_element_type=F32) + b_ref[:, h:h + 1]
                xu = x_ref[rows, cols]
                do = do_ref[rows, cols]
                dx_ref[rows, cols] = (do * gate * _gelu_grad(xu)).astype(BF)
                dgate = do * _gelu(xu)
                dgb = dgate.astype(BF)
                dw = dw + lax.dot_general(dgb, blk, _DIMS['nt'], preferred_element_type=F32)
                dbs = dbs + jnp.sum(dgate, axis=1, keepdims=True)
                dvn_scr[rows, cols] = jnp.dot(wt_ref[h], dgb, preferred_element_type=F32)
            _accumulate(dw_ref.at[h], first, jnp.where(tril, dw, 0.0))
            _accumulate(db_ref.at[h], first, dbs)
        dvn = dvn_scr[...]
        _accumulate(dg_ref, first, _colsum(dvn * vh))
        dvh = dvn * g_ref[...]
        dv = r * (dvh - vh * jnp.mean(dvh * vh, axis=-1, keepdims=True))
        dx_ref[:, GM_W:] = (dv * _gelu_grad(x_ref[:, GM_W:])).astype(BF)

    full3 = pl.BlockSpec((GM_HEADS, CHUNK, CHUNK), lambda i: (0, 0, 0))
    return pl.pallas_call(
        body, name=name, grid=(n // tm,),
        in_specs=[pl.BlockSpec((tm, 2 * GM_W), lambda i: (i, 0)), pl.BlockSpec((tm, GM_W), lambda i: (i, 0)),
                  full3, full3, pl.BlockSpec((CHUNK, GM_HEADS), lambda i: (0, 0)),
                  pl.BlockSpec((1, GM_W), lambda i: (0, 0))],
        out_specs=[pl.BlockSpec((tm, 2 * GM_W), lambda i: (i, 0)), full3,
                   pl.BlockSpec((GM_HEADS, CHUNK, 1), lambda i: (0, 0, 0)),
                   pl.BlockSpec((1, GM_W), lambda i: (0, 0))],
        out_shape=[jax.ShapeDtypeStruct((n, 2 * GM_W), BF), jax.ShapeDtypeStruct((GM_HEADS, CHUNK, CHUNK), F32),
                   jax.ShapeDtypeStruct((GM_HEADS, CHUNK, 1), F32), jax.ShapeDtypeStruct((1, GM_W), F32)],
        scratch_shapes=[pltpu.VMEM((tm, GM_W), F32)],
        compiler_params=_cp(("arbitrary",)))(uv, dout, wm, wmt, bst, gv)


def _s5_disc(lam_re, lam_im, log_dt, b_re, b_im):
    lr = jnp.minimum(lam_re, LAM_MAX)
    li = lam_im
    dt = jnp.exp(log_dt)
    mag = jnp.exp(lr * dt)
    ab_re = mag * jnp.cos(li * dt)
    ab_im = mag * jnp.sin(li * dt)
    den = lr * lr + li * li
    nr = ab_re - 1.0
    ni = ab_im
    z_re = (nr * lr + ni * li) / den
    z_im = (ni * lr - nr * li) / den
    return ab_re, ab_im, z_re * b_re - z_im * b_im, z_re * b_im + z_im * b_re


def _s5_disc_fwd(args, name):
    shp = jax.ShapeDtypeStruct(args[0].shape, F32)

    def body(*refs):
        outs = _s5_disc(*[r[...] for r in refs[:5]])
        for o_ref, o in zip(refs[5:], outs):
            o_ref[...] = o

    return pl.pallas_call(body, name=name, out_shape=[shp] * 4)(*args)


def _s5_disc_bwd(args, cts, name):
    shp = jax.ShapeDtypeStruct(args[0].shape, F32)

    def body(*refs):
        _, vjp = jax.vjp(_s5_disc, *[r[...] for r in refs[:5]])
        grads = vjp(tuple(r[...] for r in refs[5:9]))
        for o_ref, o in zip(refs[9:], grads):
            o_ref[...] = o

    return pl.pallas_call(body, name=name, out_shape=[shp] * 5)(*args, *cts)


def _cmul(a, b):
    return a[0] * b[0] - a[1] * b[1], a[0] * b[1] + a[1] * b[0]


def _scan_tables(ar, ai, reverse):
    if reverse:
        ai = -ai
    a1 = (ar, ai)
    a2 = _cmul(a1, a1)
    a3 = _cmul(a2, a1)
    a4 = _cmul(a2, a2)
    powers = [a1, a2, a3, a4, _cmul(a4, a1), _cmul(a4, a2), _cmul(a4, a3), _cmul(a4, a4)]
    row = lax.broadcasted_iota(jnp.int32, (8, NST), 0)
    zero = jnp.zeros((8, NST), F32)
    pr, pi = zero, zero
    for r in range(8):
        pw = powers[7 - r] if reverse else powers[r]
        pr = jnp.where(row == r, pw[0], pr)
        pi = jnp.where(row == r, pw[1], pi)
    levels = []
    for d, pw in ((1, a1), (2, a2), (4, a4)):
        ok = (row <= 7 - d) if reverse else (row >= d)
        levels.append((d, jnp.where(ok, pw[0], zero), jnp.where(ok, pw[1], zero)))
    return (pr, pi), levels


def _scan_block(src, dst, car, tables, n_tiles, reverse):
    (pr, pi), levels = tables
    row = lax.broadcasted_iota(jnp.int32, (8, NST), 0)
    out_row = 0 if reverse else 7

    def step(t, carry):
        cr, ci = carry
        tile = (n_tiles - 1 - t) if reverse else t
        rows = pl.ds(pl.multiple_of(tile * 8, 8), 8)
        xr = src[rows, 0:NST]
        xi = src[rows, NST:2 * NST]
        for d, dr, di in levels:
            shift = 8 - d if reverse else d
            rr = pltpu.roll(xr, shift, 0)
            ri = pltpu.roll(xi, shift, 0)
            xr, xi = xr + dr * rr - di * ri, xi + dr * ri + di * rr
        hr = xr + pr * cr - pi * ci
        hi = xi + pr * ci + pi * cr
        dst[rows, 0:NST] = hr
        dst[rows, NST:2 * NST] = hi
        return (_colsum(jnp.where(row == out_row, hr, 0.0)), _colsum(jnp.where(row == out_row, hi, 0.0)))

    cr, ci = lax.fori_loop(0, n_tiles, step, (car[0:1, 0:NST], car[0:1, NST:2 * NST]))
    car[0:1, 0:NST] = cr
    car[0:1, NST:2 * NST] = ci


def _s5_fwd(u, ab, bbt, cmat, dvec, wglu, bglu, seq, name):
    n = u.shape[0]
    tm = min(ROW_BLOCK, seq)

    def body(u_ref, ab_ref, bb_ref, c_ref, d_ref, w_ref, b_ref, h_ref, o_ref, xs, car):
        i = pl.program_id(0)

        @pl.when(lax.rem(i * tm, seq) == 0)
        def _():
            car[...] = jnp.zeros(car.shape, F32)

        uv = u_ref[...]
        xs[...] = jnp.dot(uv.astype(BF), bb_ref[...], preferred_element_type=F32)
        tables = _scan_tables(ab_ref[0:1, 0:NST], ab_ref[0:1, NST:2 * NST], False)
        _scan_block(xs, h_ref, car, tables, tm // 8, False)
        y = jnp.dot(h_ref[...].astype(BF), c_ref[...], preferred_element_type=F32) + d_ref[...] * uv
        g1 = _gelu(y)
        z = jnp.dot(g1.astype(BF), w_ref[...], preferred_element_type=F32) + b_ref[...]
        o_ref[...] = (g1 * _sigmoid(z)).astype(BF)

    const = lambda shape: pl.BlockSpec(shape, lambda i: (0, 0))
    return pl.pallas_call(
        body, name=name, grid=(n // tm,),
        in_specs=[pl.BlockSpec((tm, SSM_W), lambda i: (i, 0)), const((1, 2 * NST)), const((SSM_W, 2 * NST)),
                  const((2 * NST, SSM_W)), const((1, SSM_W)), const((SSM_W, SSM_W)), const((1, SSM_W))],
        out_specs=[pl.BlockSpec((tm, 2 * NST), lambda i: (i, 0)), pl.BlockSpec((tm, SSM_W), lambda i: (i, 0))],
        out_shape=[jax.ShapeDtypeStruct((n, 2 * NST), F32), jax.ShapeDtypeStruct((n, SSM_W), BF)],
        scratch_shapes=[pltpu.VMEM((tm, 2 * NST), F32), pltpu.VMEM((8, 2 * NST), F32)],
        compiler_params=_cp(("arbitrary",)))(u, ab, bbt, cmat, dvec, wglu, bglu)


def _s5_bwd(da, u, hst, ab, bbt, cmat, dvec, wglu, bglu, seq, name):
    n = u.shape[0]
    tm = min(ROW_BLOCK, seq)
    nb = n // tm
    blk = lambda r: nb - 1 - r
    prev, _ = _halo_maps(tm, n)

    def body(da_ref, u_ref, h_ref, hp_ref, ab_ref, bb_ref, c_ref, d_ref, w_ref, b_ref,
             du_ref, dw_ref, dbg_ref, dd_ref, dc_ref, dbb_ref, dab_ref, gs, car):
        r = pl.program_id(0)
        i = blk(r)
        first = r == 0

        @pl.when(lax.rem((i + 1) * tm, seq) == 0)
        def _():
            car[...] = jnp.zeros(car.shape, F32)

        uv = u_ref[...]
        dav = da_ref[...]
        hb = h_ref[...]
        hb16 = hb.astype(BF)
        dvv = d_ref[...]
        y = jnp.dot(hb16, c_ref[...], preferred_element_type=F32) + dvv * uv
        g1 = _gelu(y)
        g16 = g1.astype(BF)
        s = _sigmoid(jnp.dot(g16, w_ref[...], preferred_element_type=F32) + b_ref[...])
        dz = dav * g1 * s * (1.0 - s)
        dz16 = dz.astype(BF)
        dg1 = dav * s + lax.dot_general(dz16, w_ref[...], _DIMS['nt'], preferred_element_type=F32)
        _accumulate(dw_ref, first, lax.dot_general(g16, dz16, _DIMS['tn'], preferred_element_type=F32))
        _accumulate(dbg_ref, first, _colsum(dz))
        dy = dg1 * _gelu_grad(y)
        dy16 = dy.astype(BF)
        _accumulate(dd_ref, first, _colsum(dy * uv))
        _accumulate(dc_ref, first, lax.dot_general(hb16, dy16, _DIMS['tn'], preferred_element_type=F32))
        gs[...] = lax.dot_general(dy16, c_ref[...], _DIMS['nt'], preferred_element_type=F32)
        tables = _scan_tables(ab_ref[0:1, 0:NST], ab_ref[0:1, NST:2 * NST], True)
        _scan_block(gs, gs, car, tables, tm // 8, True)
        g = gs[...]
        g16b = g.astype(BF)
        sp = jnp.where(lax.rem(i * tm, seq) == 0, 0.0, 1.0)
        row = lax.broadcasted_iota(jnp.int32, hb.shape, 0)
        hprev = jnp.where(row == 0, hp_ref[7:8, :] * sp, pltpu.roll(hb, 1, 0))
        gr, gi = g[:, :NST], g[:, NST:]
        hr, hi = hprev[:, :NST], hprev[:, NST:]
        _accumulate(dab_ref.at[:, 0:NST], first, _colsum(gr * hr + gi * hi))
        _accumulate(dab_ref.at[:, NST:2 * NST], first, _colsum(gi * hr - gr * hi))
        _accumulate(dbb_ref, first, lax.dot_general(uv.astype(BF), g16b, _DIMS['tn'], preferred_element_type=F32))
        du = dy * dvv + lax.dot_general(g16b, bb_ref[...], _DIMS['nt'], preferred_element_type=F32)
        du_ref[...] = du.astype(BF)

    const = lambda shape: pl.BlockSpec(shape, lambda r: (0, 0))
    rowspec = lambda w: pl.BlockSpec((tm, w), lambda r: (blk(r), 0))
    return pl.pallas_call(
        body, name=name, grid=(nb,),
        in_specs=[rowspec(SSM_W), rowspec(SSM_W), rowspec(2 * NST),
                  pl.BlockSpec((8, 2 * NST), lambda r: (prev(blk(r)), 0)),
                  const((1, 2 * NST)), const((SSM_W, 2 * NST)), const((2 * NST, SSM_W)), const((1, SSM_W)),
                  const((SSM_W, SSM_W)), const((1, SSM_W))],
        out_specs=[rowspec(SSM_W), const((SSM_W, SSM_W)), const((1, SSM_W)), const((1, SSM_W)),
                   const((2 * NST, SSM_W)), const((SSM_W, 2 * NST)), const((1, 2 * NST))],
        out_shape=[jax.ShapeDtypeStruct((n, SSM_W), BF), jax.ShapeDtypeStruct((SSM_W, SSM_W), F32),
                   jax.ShapeDtypeStruct((1, SSM_W), F32), jax.ShapeDtypeStruct((1, SSM_W), F32),
                   jax.ShapeDtypeStruct((2 * NST, SSM_W), F32), jax.ShapeDtypeStruct((SSM_W, 2 * NST), F32),
                   jax.ShapeDtypeStruct((1, 2 * NST), F32)],
        scratch_shapes=[pltpu.VMEM((tm, 2 * NST), F32), pltpu.VMEM((8, 2 * NST), F32)],
        compiler_params=_cp(("arbitrary",)))(da, u, hst, hst, ab, bbt, cmat, dvec, wglu, bglu)


def _s5_rows(lam_re, lam_im, log_dt, b_re, b_im):
    rep = lambda a: jnp.broadcast_to(a[:, None, :], (SSM_G, SSM_H, SSM_P)).reshape(SSM_W, SSM_P)
    dt = jnp.broadcast_to(log_dt[:, None, None], (SSM_G, SSM_H, SSM_P)).reshape(SSM_W, SSM_P)
    tr = lambda b: b.transpose(0, 2, 1).reshape(SSM_W, SSM_P)
    return rep(lam_re), rep(lam_im), dt, tr(b_re), tr(b_im)


def _block_diag(rows_gp, inner):
    eye = jnp.eye(SSM_G, dtype=rows_gp.dtype)
    return (rows_gp[:, :, None, :] * eye[:, None, :, None]).reshape(SSM_G * inner, SSM_G * SSM_P)


def _diag_blocks(mat, inner):
    m4 = mat.reshape(SSM_G, inner, SSM_G, SSM_P)
    return jnp.stack([m4[g, :, g, :] for g in range(SSM_G)])


def _interleave(w, parts):
    lead = w.shape[:-1]
    nb = w.shape[-1] // (parts * FB)
    return jnp.swapaxes(w.reshape(lead + (parts, nb, FB)), -3, -2).reshape(w.shape)


def _deinterleave(w, parts):
    lead = w.shape[:-1]
    nb = w.shape[-1] // (parts * FB)
    return jnp.swapaxes(w.reshape(lead + (nb, parts, FB)), -3, -2).reshape(w.shape)


def _ffn_fwd(h, g, w_up, w_down, cw, cb, seq, tag):
    n = h.shape[0]
    tm = min(2048, n)
    ni = n // tm
    f = _rmsnorm_fwd(h, g, f"{tag}_norm")
    up = _matmul_spec(
        f, w_up, 'nn', (NDEV, ni, 1),
        pl.BlockSpec((tm, D), lambda s, i, k: (i, 0)),
        pl.BlockSpec((D, FSH), lambda s, i, k: (s, 0)),
        pl.BlockSpec((tm, FSH), lambda s, i, k: (s * ni + i, 0)), (NDEV * n, FSH), f"{tag}_up", out_dtype=BF)
    up = up.reshape(2, 4, n, FSH)
    act, dgate = _ffn_conv_fwd(up, cw, cb, seq, f"{tag}_conv")
    out = _matmul_shards(act, w_down.reshape(4, FSH, D), 'nn', 1024, 512, f"{tag}_down", resid=h)
    return out, (f, up, act, dgate)


def _ffn_bwd(dh, dhb, h, g, w_up, w_down, cw, cb, saved, seq, tag):
    f, up, act, dgate = saved
    n = h.shape[0]
    tm = min(2048, n)
    ni = n // tm
    tk = min(4096, n)
    nk = n // tk
    dact = _matmul_spec(
        dhb, w_down, 'nt', (4, ni, 1),
        pl.BlockSpec((tm, D), lambda j, i, k: (i, 0)),
        pl.BlockSpec((FSH, D), lambda j, i, k: (j, 0)),
        pl.BlockSpec((tm, FSH), lambda j, i, k: (j * ni + i, 0)), (4 * n, FSH), f"{tag}_ddown_x", out_dtype=BF)
    tn = 512
    dw_down = _matmul_spec(
        act.reshape(4 * n, FSH), dhb, 'tn', (4, D // tn, nk),
        pl.BlockSpec((tk, FSH), lambda j, c, k: (j * nk + k, 0)),
        pl.BlockSpec((tk, tn), lambda j, c, k: (k, c)),
        pl.BlockSpec((FSH, tn), lambda j, c, k: (j, c)), (DFF, D), f"{tag}_ddown_w", out_dtype=BF)
    dup, dcw, dcb = _ffn_conv_bwd(up, dgate, dact.reshape(4, n, FSH), cw, seq, f"{tag}_dconv")
    dup2 = dup.reshape(NDEV * n, FSH)
    df = _matmul_shards(dup.reshape(NDEV, n, FSH), w_up.reshape(NDEV, D, FSH), 'nt', 512, D, f"{tag}_dup_x")
    dw_up = _matmul_spec(
        f, dup2, 'tn', (NDEV, 1, nk),
        pl.BlockSpec((tk, D), lambda s, j, k: (k, 0)),
        pl.BlockSpec((tk, FSH), lambda s, j, k: (s * nk + k, 0)),
        pl.BlockSpec((D, FSH), lambda s, j, k: (s, 0)), (NDEV * D, FSH), f"{tag}_dup_w", out_dtype=BF)
    dh_in, dhb_in, dg = _rmsnorm_bwd(h, g, df, dh, f"{tag}_dnorm")
    grads = dict(g=dg, w_up=dw_up.reshape(NDEV, D, FSH), w_down=dw_down.reshape(NDEV, DFF // NDEV, D),
                 cw=dcw.reshape(NDEV, 3, FSH), cb=dcb.reshape(2 * DFF))
    return dh_in, dhb_in, grads


def _col_shards(w, width):
    return w.reshape(w.shape[0], NDEV, width).transpose(1, 0, 2)


def _local_step(x, tgt, w, gw, wait_ffn0, wait_rest, token, scatter, seq):
    bf = lambda a: a.astype(BF)
    row = lambda a: a.reshape(1, -1).astype(F32)
    w_ev = gw['ev_w_in'].transpose(1, 0, 2).reshape(D, 1792)
    w_ev_s5, w_ev_gm = w_ev[:, :SSM_W], w_ev[:, SSM_W:]
    w_evo = gw['ev_w_out'].reshape(D, D)
    f_cb = [w['ffn_conv_b'][l].reshape(2, 4, 1, FSH) for l in range(2)]
    tril = jnp.tril(jnp.ones((CHUNK, CHUNK), dtype=bool))
    gm_w = jnp.where(tril, w['gm_w_s'][0], 0.0)
    gm_wm, gm_wmt = bf(gm_w), bf(jnp.swapaxes(gm_w, 1, 2))
    gm_bt = w['gm_b_s'][0].T
    gm_gv = row(w['gm_v_g'][0])
    s5_in = _s5_rows(w['s5_lam_re'][0], w['s5_lam_im'][0], w['s5_log_dt'][0], w['s5_b_re'][0], w['s5_b_im'][0])
    ab_re, ab_im, bb_re, bb_im = _s5_disc_fwd(s5_in, "s5_disc")
    first_h = lambda a: a.reshape(SSM_G, SSM_H, SSM_P)[:, 0, :].reshape(1, NST)
    s5_ab = jnp.concatenate([first_h(ab_re), first_h(ab_im)], axis=1)
    to_gp = lambda a: a.reshape(SSM_G, SSM_H, SSM_P)
    s5_bbt = bf(jnp.concatenate([_block_diag(to_gp(bb_re), SSM_H), _block_diag(to_gp(bb_im), SSM_H)], axis=1))
    s5_cmat = bf(jnp.concatenate([_block_diag(w['s5_c_re'][0], SSM_H).T, -_block_diag(w['s5_c_im'][0], SSM_H).T],
                                 axis=0))
    s5_d, s5_bg, s5_wg = row(w['s5_d'][0]), row(w['s5_b_glu'][0]), gw['s5_w_glu'].reshape(SSM_W, SSM_W)
    g_mix = [row(w['mix_norm_g'][0]) + token[0:1, 0:1], row(w['mix_norm_g'][1])]
    g_ffn = [row(w['ffn_norm_g'][l]) for l in range(2)]
    g_fin = row(w['final_norm_g'])

    h0 = x
    y0 = _rmsnorm_fwd(h0, g_mix[0], "ev_norm")
    p_s5 = _matmul(y0, w_ev_s5, 'nn', 1024, 256, D, "ev_in_s5")
    p_gm = _matmul(y0, w_ev_gm, 'nn', 1024, 2 * GM_W, D, "ev_in_gm")
    hst, a_out = _s5_fwd(p_s5, s5_ab, s5_bbt, s5_cmat, s5_d, s5_wg, s5_bg, seq, "s5_fwd")
    b_out = _gmlp_fwd(p_gm, gm_wm, gm_bt, gm_gv, seq, "gmlp_fwd")
    mixcat = jnp.concatenate([a_out, b_out], axis=1)
    h1 = _matmul(mixcat, w_evo, 'nn', 1024, D, D, "ev_out", resid=h0)
    g0 = wait_ffn0(mixcat)
    w_up0, w_dn0 = g0['ffn_w_up0'].reshape(NDEV * D, FSH), g0['ffn_w_down0'].reshape(DFF, D)
    f_cw0 = g0['ffn_conv_w0'].reshape(2, 4, 3, FSH)
    h2, ffn0 = _ffn_fwd(h1, g_ffn[0], w_up0, w_dn0, f_cw0, f_cb[0], seq, "ffn0")
    g1 = wait_rest(h2)
    w_od = _interleave(g1['od_w_in'].transpose(1, 0, 2).reshape(D, 3 * D), 3)
    w_odo = g1['od_w_out'].reshape(D, D)
    od_cw = g1['od_conv_w'].transpose(1, 0, 2).reshape(3, D)
    od_cb = g1['od_conv_b'].reshape(1, D)
    w_up1, w_dn1 = g1['ffn_w_up1'].reshape(NDEV * D, FSH), g1['ffn_w_down1'].reshape(DFF, D)
    f_cw1 = g1['ffn_conv_w1'].reshape(2, 4, 3, FSH)
    y1 = _rmsnorm_fwd(h2, g_mix[1], "od_norm")
    p_od = _matmul(y1, w_od, 'nn', 1024, 3 * D // 2, D, "od_in")
    mixin = _shortconv_fwd(p_od, od_cw, od_cb, seq, "od_conv")
    h3 = _matmul(mixin, w_odo, 'nn', 1024, D, D, "od_out", resid=h2)
    h4, ffn1 = _ffn_fwd(h3, g_ffn[1], w_up1, w_dn1, f_cw1, f_cb[1], seq, "ffn1")
    loss, dh4, dh4b, dg_fin = _final_loss(h4, g_fin, tgt, "final_loss")

    dh3, dh3b, gf1 = _ffn_bwd(dh4, dh4b, h3, g_ffn[1], w_up1, w_dn1, f_cw1, f_cb[1], ffn1, seq, "ffn1")
    dmixin = _matmul(dh3b, w_odo, 'nt', 1024, D, D, "od_dout_x")
    dw_odo = _matmul(mixin, dh3b, 'tn', D, 512, 4096, "od_dout_w", out_dtype=BF)
    dp_od, d_od_cw, d_od_cb = _shortconv_bwd(p_od, dmixin, od_cw, od_cb, seq, "od_dconv")
    dy1 = _matmul(dp_od, w_od, 'nt', 512, D, 3 * D, "od_din_x")
    dw_od = _matmul(y1, dp_od, 'tn', D, 512, 4096, "od_din_w", out_dtype=BF)
    sent = scatter("scatter_layer1", {
        'od_w_in': _col_shards(_deinterleave(dw_od, 3), 384), 'od_conv_w': _col_shards(d_od_cw, D // NDEV),
        'od_conv_b': d_od_cb.reshape(NDEV, 1, D // NDEV), 'od_w_out': dw_odo.reshape(NDEV, D // NDEV, D),
        'ffn_w_up1': gf1['w_up'], 'ffn_conv_w1': gf1['cw'], 'ffn_w_down1': gf1['w_down']})
    dh2, dh2b, dg_mix1 = _rmsnorm_bwd(h2, g_mix[1] + sent[0:1, 0:1], dy1, dh3, "od_dnorm")
    dh1, dh1b, gf0 = _ffn_bwd(dh2, dh2b, h1, g_ffn[0], w_up0, w_dn0, f_cw0, f_cb[0], ffn0, seq, "ffn0")
    dmix_a = _matmul(dh1b, w_evo[:SSM_W], 'nt', 1024, SSM_W, D, "ev_dout_xa")
    dmix_b = _matmul(dh1b, w_evo[SSM_W:], 'nt', 1024, GM_W, D, "ev_dout_xb")
    dw_evo = _matmul(mixcat, dh1b, 'tn', D, 512, 4096, "ev_dout_w", out_dtype=BF)
    sent = scatter("scatter_ffn0", {'ffn_w_up0': gf0['w_up'], 'ffn_conv_w0': gf0['cw'], 'ffn_w_down0': gf0['w_down'],
                                    'ev_w_out': dw_evo.reshape(NDEV, D // NDEV, D)})
    dp_s5, d_wg, d_bg, d_d, d_cmat, d_bbt, d_ab = _s5_bwd(dmix_a, p_s5, hst, s5_ab, s5_bbt, s5_cmat,
                                                           s5_d + sent[0:1, 0:1], s5_wg, s5_bg, seq, "s5_bwd")
    dp_gm, d_gmw, d_gmb, d_gmg = _gmlp_bwd(p_gm, dmix_b, gm_wm, gm_wmt, gm_bt, gm_gv, seq, "gmlp_bwd")
    dw_ev = jnp.concatenate([_matmul(y0, dp_s5, 'tn', D, SSM_W, 4096, "ev_din_wa", out_dtype=BF),
                             _matmul(y0, dp_gm, 'tn', D, 512, 4096, "ev_din_wb", out_dtype=BF)], axis=1)
    sent = scatter("scatter_even", {'ev_w_in': _col_shards(dw_ev, 224),
                                    's5_w_glu': d_wg.reshape(NDEV, SSM_W // NDEV, SSM_W)})
    dy0 = _matmul(dp_gm, w_ev_gm, 'nt', 512, D, 2 * GM_W, "ev_din_xb")
    dy0 = _matmul(dp_s5, w_ev_s5, 'nt', 1024, D, SSM_W, "ev_din_xa", resid=dy0)
    grad_x, _, dg_mix0 = _rmsnorm_bwd(h0, g_mix[0] + sent[0:1, 0:1], dy0, dh1, "ev_dnorm")

    put_h0 = lambda a: jnp.zeros((SSM_G, SSM_H, SSM_P), F32).at[:, 0, :].set(a.reshape(SSM_G, SSM_P)).reshape(
        SSM_W, SSM_P)
    ct = (put_h0(d_ab[:, :NST]), put_h0(d_ab[:, NST:]),
          _diag_blocks(d_bbt[:, :NST], SSM_H).reshape(SSM_W, SSM_P),
          _diag_blocks(d_bbt[:, NST:], SSM_H).reshape(SSM_W, SSM_P))
    d_lre, d_lim, d_ldt, d_bre, d_bim = _s5_disc_bwd(s5_in, ct, "s5_ddisc")
    over_h = lambda a: a.reshape(SSM_G, SSM_H, SSM_P).sum(axis=1)
    un_tr = lambda a: a.reshape(SSM_G, SSM_H, SSM_P).transpose(0, 2, 1)
    d_cre = _diag_blocks(d_cmat[:NST].T, SSM_H)
    d_cim = -_diag_blocks(d_cmat[NST:].T, SSM_H)

    repl = {
        'mix_norm_g': jnp.concatenate([dg_mix0, dg_mix1], axis=0),
        'ffn_norm_g': jnp.concatenate([gf0['g'], gf1['g']], axis=0),
        'final_norm_g': dg_fin.reshape(D),
        's5_lam_re': over_h(d_lre)[None], 's5_lam_im': over_h(d_lim)[None],
        's5_log_dt': over_h(d_ldt).sum(axis=1)[None],
        's5_b_re': un_tr(d_bre)[None], 's5_b_im': un_tr(d_bim)[None],
        's5_c_re': d_cre[None], 's5_c_im': d_cim[None],
        's5_d': d_d, 's5_b_glu': d_bg,
        'gm_w_s': d_gmw[None], 'gm_b_s': d_gmb.reshape(1, GM_HEADS, CHUNK), 'gm_v_g': d_gmg,
        'ffn_conv_b': jnp.stack([gf0['cb'], gf1['cb']]),
    }
    return loss, grad_x, repl


HBM_SPEC = pl.BlockSpec(memory_space=pltpu.HBM)


def _at_axis(ref, pos, index):
    return ref.at[(slice(None),) * pos + (index,)]


def _all_gather(shards, positions, name):
    n = len(shards)

    def body(*refs):
        xs, outs = refs[:n], refs[n:2 * n]
        send_sems, recv_sems, local_sems = refs[2 * n:]
        x, y, c = lax.axis_index("x"), lax.axis_index("y"), lax.axis_index("c")
        me, sibling = (x, y, c), (x, y, 1 - c)
        chips = [(1 - x, y), (x, 1 - y), (1 - x, 1 - y)]

        def block(p, dev):
            return _at_axis(outs[p], positions[p], 4 * dev[0] + 2 * dev[1] + dev[2])

        def copy(p, k, dev, to, src=None):
            return pltpu.make_async_remote_copy(
                src_ref=block(p, dev) if src is None else src, dst_ref=block(p, dev),
                send_sem=send_sems.at[p, k], recv_sem=recv_sems.at[p, k], device_id=to, device_id_type=MESH_T)

        mine = [pltpu.make_async_copy(xs[p], block(p, me), local_sems.at[p]) for p in range(n)]
        for cp in mine:
            cp.start()
        first = [copy(p, 0, me, sibling, src=xs[p]) for p in range(n)]
        first += [copy(p, 1 + j, me, (*chip, c), src=xs[p]) for j, chip in enumerate(chips) for p in range(n)]
        for cp in first:
            cp.start()
        passed = []
        for j, chip in enumerate(chips):
            for p in range(n):
                copy(p, 1 + j, (*chip, c), me).wait_recv()
                fwd = copy(p, 4 + j, (*chip, c), sibling)
                fwd.start()
                passed.append(fwd)
        for p in range(n):
            copy(p, 0, sibling, me).wait_recv()
        for j, chip in enumerate(chips):
            for p in range(n):
                copy(p, 4 + j, (*chip, 1 - c), me).wait_recv()
        for cp in first + passed:
            cp.wait_send()
        for cp in mine:
            cp.wait()

    out_shape = [jax.ShapeDtypeStruct(s.shape[:pos] + (NDEV,) + s.shape[pos:], s.dtype)
                 for s, pos in zip(shards, positions)]
    return pl.pallas_call(
        body, name=name, out_shape=out_shape, in_specs=[HBM_SPEC] * n, out_specs=[HBM_SPEC] * n,
        scratch_shapes=[pltpu.SemaphoreType.DMA((n, 7)), pltpu.SemaphoreType.DMA((n, 7)),
                        pltpu.SemaphoreType.DMA((n,))])(*shards)


def _other_devices(x, y, c):
    flip = lambda v, bit: 1 - v if bit else v
    return [(flip(x, k >> 2 & 1), flip(y, k >> 1 & 1), flip(c, k & 1)) for k in range(1, NDEV)]


SEM_SPEC = pl.BlockSpec(memory_space=pltpu.SEMAPHORE)
START_EFFECT = pltpu.SideEffectType.DATAFLOW_SIDE_EFFECTING


def _send_start(arrays, scatter, name):
    n = len(arrays)
    lands = [lax.empty((NDEV,) + (a.shape[1:] if scatter else a.shape), a.dtype) for a in arrays]

    def body(*refs):
        xs, ls = refs[:n], refs[n:2 * n]
        send_sems, recv_sems, own_sems, token = refs[2 * n], refs[2 * n + 1], refs[2 * n + 2], refs[4 * n + 3]
        x, y, c = lax.axis_index("x"), lax.axis_index("y"), lax.axis_index("c")
        me = 4 * x + 2 * y + c
        for k, peer in enumerate(_other_devices(x, y, c)):
            for p in range(n):
                src = xs[p].at[4 * peer[0] + 2 * peer[1] + peer[2]] if scatter else xs[p]
                pltpu.make_async_remote_copy(
                    src_ref=src, dst_ref=ls[p].at[me], send_sem=send_sems.at[p * (NDEV - 1) + k],
                    recv_sem=recv_sems.at[p * (NDEV - 1) + k], device_id=peer, device_id_type=MESH_T).start()
        for p in range(n):
            pltpu.make_async_copy(xs[p].at[me] if scatter else xs[p], ls[p].at[me], own_sems.at[p]).start()
        token[...] = jnp.zeros(token.shape, F32)

    sems = pltpu.SemaphoreType.DMA((n * (NDEV - 1),))
    out_shape = ([sems, sems, pltpu.SemaphoreType.DMA((n,))]
                 + [pltpu.HBM(a.shape, a.dtype) for a in list(arrays) + lands] + [jax.ShapeDtypeStruct((8, 128), F32)])
    res = pl.pallas_call(
        body, name=name, out_shape=out_shape, in_specs=[HBM_SPEC] * (2 * n),
        out_specs=[SEM_SPEC] * 3 + [HBM_SPEC] * (2 * n) + [pl.BlockSpec(memory_space=pltpu.VMEM)],
        input_output_aliases={i: 3 + i for i in range(2 * n)},
        compiler_params=pltpu.CompilerParams(has_side_effects=START_EFFECT))(
            *[pltpu.with_memory_space_constraint(a, pltpu.HBM) for a in list(arrays) + lands])
    return res[:3], res[3:3 + n], res[3 + n:3 + 2 * n], res[3 + 2 * n]


def _send_wait(started, scatter, after, name):
    sems, arrays, lands, _ = started
    n = len(arrays)

    def body(*refs):
        xs, ls = refs[:n], refs[n:2 * n]
        send, recv, own = refs[2 * n:2 * n + 3]
        x, y, c = lax.axis_index("x"), lax.axis_index("y"), lax.axis_index("c")
        me = 4 * x + 2 * y + c
        for p in range(n):
            pltpu.make_async_copy(xs[p].at[me] if scatter else xs[p], ls[p].at[me], own.at[p]).wait()
        for k, peer in enumerate(_other_devices(x, y, c)):
            slot = 4 * peer[0] + 2 * peer[1] + peer[2]
            for p in range(n):
                cp = pltpu.make_async_remote_copy(
                    src_ref=xs[p].at[slot] if scatter else xs[p], dst_ref=ls[p].at[slot],
                    send_sem=send.at[p * (NDEV - 1) + k], recv_sem=recv.at[p * (NDEV - 1) + k], device_id=peer,
                    device_id_type=MESH_T)
                cp.wait_send()
                cp.wait_recv()

    res = pl.pallas_call(
        body, name=name, out_shape=[pltpu.HBM(a.shape, a.dtype) for a in list(arrays) + list(lands)],
        in_specs=[HBM_SPEC] * (2 * n) + [SEM_SPEC] * 3 + [pl.BlockSpec(memory_space=pl.ANY)],
        out_specs=[HBM_SPEC] * (2 * n), input_output_aliases={i: i for i in range(2 * n)},
        compiler_params=pltpu.CompilerParams(has_side_effects=START_EFFECT))(
            *arrays, *lands, *sems, after)
    return res[n:]


def _row_block(rows, cols, itemsize=4, target=2**20):
    best = None
    for tr in range(16, rows + 1, 16):
        if rows % tr == 0 and tr * cols * itemsize <= target:
            best = tr
    return best or rows


def _adamw(w, m, v, gparts, name):
    parts, rows, cols = gparts.shape
    tr = _row_block(rows, cols, target=2**19)
    bc1 = 1.0 - ADAM_B1 ** ADAM_STEP
    bc2 = 1.0 - ADAM_B2 ** ADAM_STEP

    def body(w_ref, m_ref, v_ref, g_ref, go_ref, d_ref, mo_ref, vo_ref):
        g = g_ref[0].astype(F32)
        for k in range(1, parts):
            g = g + g_ref[k].astype(F32)
        mn = ADAM_B1 * m_ref[...] + (1.0 - ADAM_B1) * g
        vn = ADAM_B2 * v_ref[...] + (1.0 - ADAM_B2) * (g * g)
        go_ref[...] = g
        mo_ref[...] = mn
        vo_ref[...] = vn
        d_ref[...] = -ADAM_LR * ((mn / bc1) / (jnp.sqrt(vn / bc2) + ADAM_EPS) + ADAM_WD * w_ref[...])

    blk = pl.BlockSpec((tr, cols), lambda i: (i, 0))
    shp = jax.ShapeDtypeStruct((rows, cols), F32)
    return pl.pallas_call(
        body, name=name, grid=(rows // tr,),
        in_specs=[blk, blk, blk, pl.BlockSpec((parts, tr, cols), lambda i: (0, i, 0))],
        out_specs=[blk] * 4, out_shape=[shp] * 4, compiler_params=_cp(("parallel",)))(w, m, v, gparts)


def _pack(arrays, rows):
    flat = jnp.concatenate([a.reshape(-1).astype(F32) for a in arrays])
    return jnp.pad(flat, (0, rows * PACK_COLS - flat.shape[0])).reshape(rows, PACK_COLS)


def _unpack(buf, shapes):
    flat = buf.reshape(-1)
    out, off = [], 0
    for shp in shapes:
        size = int(np.prod(shp))
        out.append(flat[off:off + size].reshape(shp))
        off += size
    return out


REPL_SHAPES = {'mix_norm_g': (2, 1024), 'ffn_norm_g': (2, 1024), 'final_norm_g': (1024,), 's5_lam_re': (1, 16, 64),
               's5_lam_im': (1, 16, 64), 's5_log_dt': (1, 16), 's5_b_re': (1, 16, 64, 16), 's5_b_im': (1, 16, 64, 16),
               's5_c_re': (1, 16, 16, 64), 's5_c_im': (1, 16, 16, 64), 's5_d': (1, 256), 's5_b_glu': (1, 256),
               'gm_w_s': (1, 6, 128, 128), 'gm_b_s': (1, 6, 128), 'gm_v_g': (1, 768), 'ffn_conv_b': (2, 5632)}
REPL_ELEMS = sum(int(np.prod(REPL_SHAPES[n])) for n in REPL_ORDER)
REPL_ROWS = -(-REPL_ELEMS // (PACK_COLS * 8)) * 8

GATHER_DTYPE = {'ev_w_in': BF, 'ev_w_out': BF, 's5_w_glu': BF, 'od_w_in': BF, 'od_conv_w': F32, 'od_conv_b': F32,
                'od_w_out': BF, 'ffn_w_up': BF, 'ffn_conv_w': F32, 'ffn_w_down': BF}
GATHER_EVEN = ['ev_w_in', 'ev_w_out', 's5_w_glu']
GATHER_FFN0 = ['ffn_w_up0', 'ffn_conv_w0', 'ffn_w_down0']
GATHER_REST = ['od_w_in', 'od_conv_w', 'od_conv_b', 'od_w_out', 'ffn_w_up1', 'ffn_conv_w1', 'ffn_w_down1']

def _squeeze_lead(a):
    return a.reshape(a.shape[1:]) if a.shape[0] == 1 and a.ndim > 2 else a


def kernel(x, mix_norm_g, ffn_norm_g, final_norm_g, ev_w_in, ev_w_out, s5_lam_re, s5_lam_im, s5_log_dt, s5_b_re, s5_b_im, s5_c_re, s5_c_im, s5_d, s5_w_glu, s5_b_glu, gm_w_s, gm_b_s, gm_v_g, od_w_in, od_conv_w, od_conv_b, od_w_out, ffn_w_up, ffn_conv_w, ffn_conv_b, ffn_w_down, loss_target, m_mix_norm_g, m_ffn_norm_g, m_final_norm_g, m_ev_w_in, m_ev_w_out, m_s5_lam_re, m_s5_lam_im, m_s5_log_dt, m_s5_b_re, m_s5_b_im, m_s5_c_re, m_s5_c_im, m_s5_d, m_s5_w_glu, m_s5_b_glu, m_gm_w_s, m_gm_b_s, m_gm_v_g, m_od_w_in, m_od_conv_w, m_od_conv_b, m_od_w_out, m_ffn_w_up, m_ffn_conv_w, m_ffn_conv_b, m_ffn_w_down, v_mix_norm_g, v_ffn_norm_g, v_final_norm_g, v_ev_w_in, v_ev_w_out, v_s5_lam_re, v_s5_lam_im, v_s5_log_dt, v_s5_b_re, v_s5_b_im, v_s5_c_re, v_s5_c_im, v_s5_d, v_s5_w_glu, v_s5_b_glu, v_gm_w_s, v_gm_b_s, v_gm_v_g, v_od_w_in, v_od_conv_w, v_od_conv_b, v_od_w_out, v_ffn_w_up, v_ffn_conv_w, v_ffn_conv_b, v_ffn_w_down):
    given = dict(locals())
    weights = {n: given[n] for n in WEIGHT_ORDER}
    nseq, seq, _ = x.shape

    send = {}
    for name in SHARDED_ORDER:
        a = weights[name].astype(GATHER_DTYPE[name])
        if a.shape[0] == 2:
            send[name + '0'], send[name + '1'] = a[0], a[1]
        else:
            send[name] = _squeeze_lead(a)
    gathers = [_send_start([send[n] for n in names], False, f"gather_{tag}_start")
               for tag, names in (("ffn0", GATHER_FFN0), ("rest", GATHER_REST))]
    token = gathers[0][3] + gathers[1][3]

    def waiter(tag, names, started):
        return lambda after: dict(zip(names, _send_wait(started, False, after, f"gather_{tag}_wait")))

    gathered = dict(zip(GATHER_EVEN, _all_gather([send[n] for n in GATHER_EVEN], [0] * len(GATHER_EVEN),
                                                 "gather_even")))

    scatters = []

    def scatter(tag, grads):
        names = list(grads)
        started = _send_start([grads[n].astype(BF) for n in names], True, f"{tag}_start")
        scatters.append((tag, names, started))
        return started[3]

    loss_row, grad_x, g_repl = _local_step(
        x.reshape(nseq * seq, D), loss_target.reshape(nseq * seq, D), weights, gathered,
        waiter("ffn0", GATHER_FFN0, gathers[0]), waiter("rest", GATHER_REST, gathers[1]), token, scatter, seq)
    loss = lax.psum(loss_row[0, 0], ("x", "y", "c"))

    parts = {}
    for tag, names, started in scatters:
        parts.update(zip(names, _send_wait(started, True, grad_x, f"{tag}_wait")))
    repl_parts = _all_gather([_pack([g_repl[n] for n in REPL_ORDER], REPL_ROWS)], [0], "gather_small_grads")[0]

    out = {}
    for name in SHARDED_ORDER:
        w = weights[name]
        if name + '0' in parts:
            gp = jnp.stack([parts[name + '0'], parts[name + '1']], axis=1)
        else:
            gp = parts[name]
        to_rows = lambda a: a.reshape(-1, w.shape[-1])
        res = _adamw(to_rows(w), to_rows(given["m_" + name]), to_rows(given["v_" + name]),
                     gp.reshape(NDEV, -1, w.shape[-1]), f"adamw_{name}")
        out[name] = [r.reshape(w.shape) for r in res]
    rp = _adamw(_pack([weights[n] for n in REPL_ORDER], REPL_ROWS),
                _pack([given["m_" + n] for n in REPL_ORDER], REPL_ROWS),
                _pack([given["v_" + n] for n in REPL_ORDER], REPL_ROWS), repl_parts, "adamw_replicated")
    rp_shapes = [weights[n].shape for n in REPL_ORDER]
    for k in range(4):
        for name, a in zip(REPL_ORDER, _unpack(rp[k], rp_shapes)):
            out.setdefault(name, [None] * 4)[k] = a
    results = [[out[n][k] for n in WEIGHT_ORDER] for k in range(4)]
    grad_w, delta_w, new_m, new_v = results
    return (loss, grad_x.reshape(nseq, seq, D), *grad_w, *delta_w, *new_m, *new_v)
```

```python
import math

import jax
import jax.numpy as jnp
import numpy as np
from jax import lax
from jax.experimental import pallas as pl
from jax.experimental.pallas import tpu as pltpu

F32 = jnp.float32
BF = jnp.bfloat16

D = 1024
DFF = 2816
NDEV = 8
SSM_W = 256
SSM_G = 16
SSM_H = 16
SSM_P = 64
NST = SSM_G * SSM_P
GM_W = 768
GM_HEADS = 6
CHUNK = 128
EPS = 1e-6
LAM_MAX = -1e-4
FB = 256
FSH = 2 * DFF // NDEV
ROW_BLOCK = 512
CONV_ROW_BLOCK = 1024
VMEM_LIMIT = 48 * 2**20
PACK_COLS = 1024
MESH_T = pl.DeviceIdType.MESH

ADAM_LR = 0.001
ADAM_B1 = 0.9
ADAM_B2 = 0.999
ADAM_EPS = 1e-08
ADAM_WD = 0.01
ADAM_STEP = 10

WEIGHT_ORDER = ['mix_norm_g', 'ffn_norm_g', 'final_norm_g', 'ev_w_in', 'ev_w_out', 's5_lam_re', 's5_lam_im',
                's5_log_dt', 's5_b_re', 's5_b_im', 's5_c_re', 's5_c_im', 's5_d', 's5_w_glu', 's5_b_glu', 'gm_w_s',
                'gm_b_s', 'gm_v_g', 'od_w_in', 'od_conv_w', 'od_conv_b', 'od_w_out', 'ffn_w_up', 'ffn_conv_w',
                'ffn_conv_b', 'ffn_w_down']
SHARDED = {'ev_w_in': ((1, 1024, 1792), 2), 'ev_w_out': ((1, 1024, 1024), 1), 's5_w_glu': ((1, 256, 256), 1),
           'od_w_in': ((1, 1024, 3072), 2), 'od_conv_w': ((1, 3, 1024), 2), 'od_conv_b': ((1, 1024), 1),
           'od_w_out': ((1, 1024, 1024), 1), 'ffn_w_up': ((2, 1024, 5632), 2), 'ffn_conv_w': ((2, 3, 5632), 2),
           'ffn_w_down': ((2, 2816, 1024), 1)}
SHARDED_ORDER = [n for n in WEIGHT_ORDER if n in SHARDED]
REPL_ORDER = [n for n in WEIGHT_ORDER if n not in SHARDED]


def _cp(sem):
    return pltpu.CompilerParams(dimension_semantics=sem, vmem_limit_bytes=VMEM_LIMIT)


def _sigmoid(x):
    return 1.0 / (1.0 + jnp.exp(-x))


_GELU_K = math.sqrt(2.0 / math.pi)


def _gelu(x):
    return 0.5 * x * (1.0 + jnp.tanh(_GELU_K * (x + 0.044715 * x * x * x)))


def _gelu_grad(x):
    t = jnp.tanh(_GELU_K * (x + 0.044715 * x * x * x))
    return 0.5 * (1.0 + t) + 0.5 * x * (1.0 - t * t) * _GELU_K * (1.0 + 3.0 * 0.044715 * x * x)


def _colsum(x):
    return jnp.sum(x, axis=0, keepdims=True)


def _accumulate(ref, first, part):
    @pl.when(first)
    def _():
        ref[...] = part

    @pl.when(jnp.logical_not(first))
    def _():
        ref[...] += part


_DIMS = {'nn': (((1,), (0,)), ((), ())), 'nt': (((1,), (1,)), ((), ())), 'tn': (((0,), (0,)), ((), ()))}


def _norm_bwd_tail(dyv, x_ref, g_ref, dr_ref, dx_ref, dxb_ref, dg_ref, first):
    xv = x_ref[...]
    r = lax.rsqrt(jnp.mean(xv * xv, axis=-1, keepdims=True) + EPS)
    xh = xv * r
    dyg = dyv * g_ref[...]
    dx = dr_ref[...] + r * (dyg - xh * jnp.mean(dyg * xh, axis=-1, keepdims=True))
    dx_ref[...] = dx
    dxb_ref[...] = dx.astype(BF)
    _accumulate(dg_ref, first, _colsum(dyv * xh))


def _norm_bwd_io(norm, tm, index):
    x, g, dres = norm
    rows = pl.BlockSpec((tm, D), index)
    vec = pl.BlockSpec((1, D), lambda *_: (0, 0))
    n = x.shape[0]
    return ([x, g, dres], [rows, vec, rows], [rows, rows, vec],
            [jax.ShapeDtypeStruct((n, D), F32), jax.ShapeDtypeStruct((n, D), BF), jax.ShapeDtypeStruct((1, D), F32)])


def _matmul(a, b, mode, tm, tn, tk, name, resid=None, out_dtype=F32, norm=None):
    if mode == 'tn':
        kdim, m = a.shape
    else:
        m, kdim = a.shape
    n = b.shape[0] if mode == 'nt' else b.shape[1]
    tm, tn, tk = min(tm, m), min(tn, n), min(tk, kdim)
    assert m % tm == 0 and n % tn == 0 and kdim % tk == 0, (name, m, n, kdim, tm, tn, tk)
    a_spec = (pl.BlockSpec((tk, tm), lambda i, j, k: (k, i)) if mode == 'tn'
              else pl.BlockSpec((tm, tk), lambda i, j, k: (i, k)))
    b_spec = (pl.BlockSpec((tn, tk), lambda i, j, k: (j, k)) if mode == 'nt'
              else pl.BlockSpec((tk, tn), lambda i, j, k: (k, j)))
    o_spec = pl.BlockSpec((tm, tn), lambda i, j, k: (i, j))
    return _matmul_spec(a, b, mode, (m // tm, n // tn, kdim // tk), a_spec, b_spec, o_spec, (m, n), name,
                        resid=resid, out_dtype=out_dtype, norm=norm)


def _matmul_spec(a, b, mode, grid, a_spec, b_spec, o_spec, out_shape, name, resid=None, out_dtype=F32, norm=None):
    nk = grid[2]
    tm, tn = o_spec.block_shape[-2:]
    dims = _DIMS[mode]
    has_resid = resid is not None
    operands = [a, b] + ([resid] if has_resid else [])
    in_specs = [a_spec, b_spec] + ([o_spec] if has_resid else [])
    out_specs, out_shapes = [o_spec], [jax.ShapeDtypeStruct(out_shape, out_dtype)]
    if norm is not None:
        assert tn == D and grid[1] == 1, name
        extra, extra_specs, out_specs, out_shapes = _norm_bwd_io(norm, tm, lambda i, j, k: (i, 0))
        operands, in_specs = operands + extra, in_specs + extra_specs
    n_in, n_out = len(operands), len(out_specs)

    def body(*refs):
        ins, outs = refs[:n_in], refs[n_in:n_in + n_out]
        a_ref, b_ref = ins[:2]
        part = lax.dot_general(a_ref[...].astype(BF), b_ref[...].astype(BF), dims, preferred_element_type=F32)

        def finish(tot):
            if has_resid:
                tot = tot + ins[2][...]
            if norm is not None:
                _norm_bwd_tail(tot, *ins[-3:], *outs, first=pl.program_id(0) == 0)
            else:
                outs[0][...] = tot.astype(out_dtype)

        if nk == 1:
            finish(part)
        else:
            acc = refs[-1]
            k = pl.program_id(2)

            @pl.when(k == 0)
            def _():
                acc[...] = part

            @pl.when(k > 0)
            def _():
                acc[...] += part

            @pl.when(k == nk - 1)
            def _():
                finish(acc[...])

    res = pl.pallas_call(
        body, name=name, grid=grid, in_specs=in_specs, out_specs=out_specs, out_shape=out_shapes,
        scratch_shapes=[pltpu.VMEM((tm, tn), F32)] if nk > 1 else [],
        compiler_params=_cp(("arbitrary",) * 3 if norm is not None else ("parallel", "parallel", "arbitrary")))(*operands)
    return res if norm is not None else res[0]


def _matmul_shards(a, b, mode, tm, tn, name, resid=None, out_dtype=F32, norm=None):
    shards, m, kdim = a.shape
    n = b.shape[2] if mode == 'nn' else b.shape[1]
    tm, tn = min(tm, m), min(tn, n)
    dims = _DIMS[mode]
    has_resid = resid is not None
    b_spec = (pl.BlockSpec((shards, kdim, tn), lambda i, j: (0, 0, j)) if mode == 'nn'
              else pl.BlockSpec((shards, tn, kdim), lambda i, j: (0, j, 0)))
    o_spec = pl.BlockSpec((tm, tn), lambda i, j: (i, j))
    operands = [a, b] + ([resid] if has_resid else [])
    in_specs = [pl.BlockSpec((shards, tm, kdim), lambda i, j: (0, i, 0)), b_spec] + ([o_spec] if has_resid else [])
    out_specs, out_shapes = [o_spec], [jax.ShapeDtypeStruct((m, n), out_dtype)]
    if norm is not None:
        assert tn == D and n == D, name
        extra, extra_specs, out_specs, out_shapes = _norm_bwd_io(norm, tm, lambda i, j: (i, 0))
        operands, in_specs = operands + extra, in_specs + extra_specs
    n_in, n_out = len(operands), len(out_specs)

    def body(*refs):
        ins, outs = refs[:n_in], refs[n_in:n_in + n_out]
        acc = lax.dot_general(ins[0][0], ins[1][0], dims, preferred_element_type=F32)
        for s in range(1, shards):
            acc = acc + lax.dot_general(ins[0][s], ins[1][s], dims, preferred_element_type=F32)
        if has_resid:
            acc = acc + ins[2][...]
        if norm is not None:
            _norm_bwd_tail(acc, *ins[-3:], *outs, first=pl.program_id(0) == 0)
        else:
            outs[0][...] = acc.astype(out_dtype)

    res = pl.pallas_call(
        body, name=name, grid=(m // tm, n // tn), in_specs=in_specs, out_specs=out_specs, out_shape=out_shapes,
        compiler_params=_cp(("arbitrary", "arbitrary") if norm is not None else ("parallel", "parallel")))(*operands)
    return res if norm is not None else res[0]


def _rmsnorm_fwd(x, g, name):
    n = x.shape[0]
    tm = min(512, n)

    def body(x_ref, g_ref, o_ref):
        xv = x_ref[...]
        r = lax.rsqrt(jnp.mean(xv * xv, axis=-1, keepdims=True) + EPS)
        o_ref[...] = (xv * r * g_ref[...]).astype(BF)

    return pl.pallas_call(
        body, name=name, grid=(n // tm,),
        in_specs=[pl.BlockSpec((tm, D), lambda i: (i, 0)), pl.BlockSpec((1, D), lambda i: (0, 0))],
        out_specs=pl.BlockSpec((tm, D), lambda i: (i, 0)),
        out_shape=jax.ShapeDtypeStruct((n, D), BF), compiler_params=_cp(("parallel",)))(x, g)


def _final_loss(h, g, tgt, name):
    n = h.shape[0]
    tm = min(512, n)

    def body(x_ref, g_ref, t_ref, loss_ref, dx_ref, dxb_ref, dg_ref):
        first = pl.program_id(0) == 0
        xv = x_ref[...]
        gv = g_ref[...]
        r = lax.rsqrt(jnp.mean(xv * xv, axis=-1, keepdims=True) + EPS)
        xh = xv * r
        err = xh * gv - t_ref[...]
        part = 0.5 * jnp.sum(jnp.mean(err * err, axis=-1, keepdims=True), axis=0, keepdims=True)
        _accumulate(loss_ref, first, jnp.broadcast_to(part, (1, 128)))
        dyv = err * (1.0 / D)
        dyg = dyv * gv
        dx = r * (dyg - xh * jnp.mean(dyg * xh, axis=-1, keepdims=True))
        dx_ref[...] = dx
        dxb_ref[...] = dx.astype(BF)
        _accumulate(dg_ref, first, _colsum(dyv * xh))

    row = pl.BlockSpec((tm, D), lambda i: (i, 0))
    vec = pl.BlockSpec((1, D), lambda i: (0, 0))
    return pl.pallas_call(
        body, name=name, grid=(n // tm,), in_specs=[row, vec, row],
        out_specs=[pl.BlockSpec((1, 128), lambda i: (0, 0)), row, row, vec],
        out_shape=[jax.ShapeDtypeStruct((1, 128), F32), jax.ShapeDtypeStruct((n, D), F32),
                   jax.ShapeDtypeStruct((n, D), BF), jax.ShapeDtypeStruct((1, D), F32)],
        compiler_params=_cp(("arbitrary",)))(h, g, tgt)


def _prev_rows(x, halo_ref, lanes, scale, row):
    h7 = halo_ref[7:8, lanes] * scale
    h6 = halo_ref[6:7, lanes] * scale
    p1 = jnp.where(row == 0, h7, pltpu.roll(x, 1, 0))
    p2 = jnp.where(row == 0, h6, jnp.where(row == 1, h7, pltpu.roll(x, 2, 0)))
    return p1, p2


def _halo_maps(tm, n_rows):
    r8 = tm // 8
    last = n_rows // 8 - 1
    prev = lambda i: jnp.maximum(i * r8 - 1, 0)
    nxt = lambda i: jnp.minimum((i + 1) * r8, last)
    return prev, nxt


def _lane_blocks(width):
    return [slice(lo, min(lo + 128, width)) for lo in range(0, width, 128)]


def _conv_taps(w_ref, b_ref, g, lanes):
    return w_ref[g, 0:1, lanes], w_ref[g, 1:2, lanes], w_ref[g, 2:3, lanes], b_ref[g, :, lanes]


def _conv_tile(x, prev1, prev2, taps, row):
    w0, w1, w2, b = taps
    r1 = pltpu.roll(x, 1, 0)
    r2 = pltpu.roll(x, 2, 0)
    x1 = jnp.where(row == 0, prev1, r1)
    x2 = jnp.where(row < 2, prev2, r2)
    return b + w0 * x2 + w1 * x1 + w2 * x, x1, x2, r1, r2


def _halo16_maps(tm, n_rows):
    r16 = tm // 16
    last = n_rows // 16 - 1
    return (lambda i: jnp.maximum(i * r16 - 1, 0)), (lambda i: jnp.minimum((i + 1) * r16, last))


def _ffn_conv_fwd(up, cw, cb, seq, name):
    n = up.shape[2]
    tm = min(CONV_ROW_BLOCK, seq)
    prev, _ = _halo16_maps(tm, n)

    def body(u_ref, h_ref, w_ref, b_ref, o_ref, d_ref):
        i = pl.program_id(1)
        scale = jnp.where(lax.rem(i * tm, seq) == 0, 0.0, 1.0)
        for lanes in _lane_blocks(FSH):
            lw = lanes.stop - lanes.start
            row = lax.broadcasted_iota(jnp.int32, (8, lw), 0)
            taps = [_conv_taps(w_ref, b_ref, g, lanes) for g in range(2)]

            def tile(xs, carry):
                hc, nxt = [], []
                for g in range(2):
                    conv, _, _, r1, r2 = _conv_tile(xs[g], carry[2 * g], carry[2 * g + 1], taps[g], row)
                    hc.append(conv)
                    nxt += [r1, r2]
                s = _sigmoid(hc[0])
                silu = hc[0] * s
                return (silu * hc[1], hc[1] * (s * (1.0 + hc[0] * (1.0 - s))), silu), tuple(nxt)

            carry = []
            for g in range(2):
                halo = h_ref[g, :, lanes].astype(F32)[8:] * scale
                carry += [pltpu.roll(halo, 1, 0), pltpu.roll(halo, 2, 0)]
            carry = tuple(carry)
            for m in range(tm // 16):
                rows = slice(m * 16, m * 16 + 16)
                x16 = [u_ref[g, rows, lanes].astype(F32) for g in range(2)]
                a, carry = tile([x[:8] for x in x16], carry)
                b, carry = tile([x[8:] for x in x16], carry)
                o_ref[rows, lanes] = jnp.concatenate([a[0], b[0]], axis=0).astype(BF)
                d_ref[0, rows, lanes] = jnp.concatenate([a[1], b[1]], axis=0).astype(BF)
                d_ref[1, rows, lanes] = jnp.concatenate([a[2], b[2]], axis=0).astype(BF)

    return pl.pallas_call(
        body, name=name, grid=(4, n // tm),
        in_specs=[pl.BlockSpec((2, None, tm, FSH), lambda j, i: (0, j, i, 0)),
                  pl.BlockSpec((2, None, 16, FSH), lambda j, i: (0, j, prev(i), 0)),
                  pl.BlockSpec((2, None, 3, FSH), lambda j, i: (0, j, 0, 0)),
                  pl.BlockSpec((2, None, 1, FSH), lambda j, i: (0, j, 0, 0))],
        out_specs=[pl.BlockSpec((None, tm, FSH), lambda j, i: (j, i, 0)),
                   pl.BlockSpec((2, None, tm, FSH), lambda j, i: (0, j, i, 0))],
        out_shape=[jax.ShapeDtypeStruct((4, n, FSH), BF), jax.ShapeDtypeStruct((2, 4, n, FSH), BF)],
        compiler_params=_cp(("parallel", "parallel")))(up, up, cw, cb)


def _ffn_conv_bwd(up, dgate, dact, cw, seq, name):
    n = up.shape[2]
    tm = min(CONV_ROW_BLOCK, seq)
    _, nxt = _halo16_maps(tm, n)

    def body(u_ref, g_ref, gn_ref, da_ref, dn_ref, w_ref, du_ref, dw_ref, db_ref):
        i = pl.program_id(1)
        sn = jnp.where(lax.rem((i + 1) * tm, seq) == 0, 0.0, 1.0)
        first = i == 0
        for lanes in _lane_blocks(FSH):
            lw = lanes.stop - lanes.start
            row = lax.broadcasted_iota(jnp.int32, (8, lw), 0)
            taps = [(w_ref[g, 0:1, lanes], w_ref[g, 1:2, lanes], w_ref[g, 2:3, lanes]) for g in range(2)]

            def dconv(gs, da):
                ds = [gs[g] * da for g in range(2)]
                return [(d, pltpu.roll(d, 7, 0), pltpu.roll(d, 6, 0)) for d in ds]

            def finish(cur, after, xs, sums):
                dups, new_sums = [], []
                for g in range(2):
                    w0, w1, w2 = taps[g]
                    s1 = jnp.where(row == 7, after[g][1], cur[g][1])
                    s2 = jnp.where(row >= 6, after[g][2], cur[g][2])
                    dups.append(w2 * cur[g][0] + w1 * s1 + w0 * s2)
                    acc = sums[g]
                    new_sums.append((acc[0] + xs[g] * s2, acc[1] + xs[g] * s1, acc[2] + xs[g] * cur[g][0],
                                     acc[3] + cur[g][0]))
                return dups, new_sums

            def emit(m, held, after, sums):
                (ta, xa), (tb, xb) = held
                dup_a, sums = finish(ta, tb, xa, sums)
                dup_b, sums = finish(tb, after, xb, sums)
                for g in range(2):
                    du_ref[g, m * 16:m * 16 + 16, lanes] = jnp.concatenate([dup_a[g], dup_b[g]], axis=0).astype(BF)
                return sums

            zero = jnp.zeros((8, lw), F32)
            sums = [(zero,) * 4, (zero,) * 4]
            held = None
            for m in range(tm // 16):
                rows = slice(m * 16, m * 16 + 16)
                x16 = [u_ref[g, rows, lanes].astype(F32) for g in range(2)]
                g16 = [g_ref[g, rows, lanes].astype(F32) for g in range(2)]
                d16 = da_ref[rows, lanes].astype(F32)
                ta = dconv([a[:8] for a in g16], d16[:8])
                tb = dconv([a[8:] for a in g16], d16[8:])
                if held is not None:
                    sums = emit(m - 1, held, ta, sums)
                held = ((ta, [x[:8] for x in x16]), (tb, [x[8:] for x in x16]))
            tn_ = dconv([gn_ref[g, :, lanes].astype(F32)[:8] for g in range(2)], dn_ref[:, lanes].astype(F32)[:8] * sn)
            sums = emit(tm // 16 - 1, held, tn_, sums)
            for g in range(2):
                for k in range(3):
                    _accumulate(dw_ref.at[g, k:k + 1, lanes], first, _colsum(sums[g][k]))
                _accumulate(db_ref.at[g, :, lanes], first, _colsum(sums[g][3]))

    return pl.pallas_call(
        body, name=name, grid=(4, n // tm),
        in_specs=[pl.BlockSpec((2, None, tm, FSH), lambda j, i: (0, j, i, 0)),
                  pl.BlockSpec((2, None, tm, FSH), lambda j, i: (0, j, i, 0)),
                  pl.BlockSpec((2, None, 16, FSH), lambda j, i: (0, j, nxt(i), 0)),
                  pl.BlockSpec((None, tm, FSH), lambda j, i: (j, i, 0)),
                  pl.BlockSpec((None, 16, FSH), lambda j, i: (j, nxt(i), 0)),
                  pl.BlockSpec((2, None, 3, FSH), lambda j, i: (0, j, 0, 0))],
        out_specs=[pl.BlockSpec((2, None, tm, FSH), lambda j, i: (0, j, i, 0)),
                   pl.BlockSpec((2, None, 3, FSH), lambda j, i: (0, j, 0, 0)),
                   pl.BlockSpec((2, None, 1, FSH), lambda j, i: (0, j, 0, 0))],
        out_shape=[jax.ShapeDtypeStruct((2, 4, n, FSH), BF), jax.ShapeDtypeStruct((2, 4, 3, FSH), F32),
                   jax.ShapeDtypeStruct((2, 4, 1, FSH), F32)],
        compiler_params=_cp(("parallel", "arbitrary")))(up, dgate, dgate, dact, dact, cw)


def _shortconv_fwd(p, cw, cb, seq, name):
    n = p.shape[0]
    tm = min(CONV_ROW_BLOCK, seq)
    prev, _ = _halo_maps(tm, n)

    def body(p_ref, h_ref, w_ref, b_ref, o_ref):
        i = pl.program_id(1)
        scale = jnp.where(lax.rem(i * tm, seq) == 0, 0.0, 1.0)
        q = p_ref[:, FB:2 * FB] * p_ref[:, 2 * FB:]
        row = lax.broadcasted_iota(jnp.int32, q.shape, 0)
        h7 = h_ref[7:8, FB:2 * FB] * h_ref[7:8, 2 * FB:] * scale
        h6 = h_ref[6:7, FB:2 * FB] * h_ref[6:7, 2 * FB:] * scale
        p1 = jnp.where(row == 0, h7, pltpu.roll(q, 1, 0))
        p2 = jnp.where(row == 0, h6, jnp.where(row == 1, h7, pltpu.roll(q, 2, 0)))
        conv = b_ref[...] + w_ref[0:1, :] * p2 + w_ref[1:2, :] * p1 + w_ref[2:3, :] * q
        o_ref[...] = (p_ref[:, :FB] * conv).astype(BF)

    return pl.pallas_call(
        body, name=name, grid=(D // FB, n // tm),
        in_specs=[pl.BlockSpec((tm, 3 * FB), lambda j, i: (i, j)),
                  pl.BlockSpec((8, 3 * FB), lambda j, i: (prev(i), j)),
                  pl.BlockSpec((3, FB), lambda j, i: (0, j)),
                  pl.BlockSpec((1, FB), lambda j, i: (0, j))],
        out_specs=pl.BlockSpec((tm, FB), lambda j, i: (i, j)),
        out_shape=jax.ShapeDtypeStruct((n, D), BF), compiler_params=_cp(("parallel", "parallel")))(p, p, cw, cb)


def _shortconv_bwd(p, dmix, cw, cb, seq, name):
    n = p.shape[0]
    tm = min(CONV_ROW_BLOCK, seq)
    ext = tm + 16
    prev, nxt = _halo_maps(tm, n)

    def body(p_ref, pp_ref, pn_ref, dm_ref, dn_ref, w_ref, b_ref, dp_ref, dw_ref, db_ref, qx, cx):
        i = pl.program_id(1)
        sp = jnp.where(lax.rem(i * tm, seq) == 0, 0.0, 1.0)
        sn = jnp.where(lax.rem((i + 1) * tm, seq) == 0, 0.0, 1.0)
        bg, cg, hx = p_ref[:, :FB], p_ref[:, FB:2 * FB], p_ref[:, 2 * FB:]
        dm = dm_ref[...]
        qx[0:8, :] = pp_ref[:, FB:2 * FB] * pp_ref[:, 2 * FB:] * sp
        qx[8:8 + tm, :] = cg * hx
        qx[8 + tm:, :] = jnp.zeros((8, FB), F32)
        cx[0:8, :] = jnp.zeros((8, FB), F32)
        cx[8:8 + tm, :] = dm * bg
        cx[8 + tm:, :] = dn_ref[...] * pn_ref[:, :FB] * sn
        q0 = qx[...]
        q1 = pltpu.roll(q0, 1, 0)
        q2 = pltpu.roll(q0, 2, 0)
        main = slice(8, 8 + tm)
        conv = b_ref[...] + w_ref[0:1, :] * q2[main] + w_ref[1:2, :] * q1[main] + w_ref[2:3, :] * q0[main]
        dc = cx[...]
        dq = (w_ref[2:3, :] * dc + w_ref[1:2, :] * pltpu.roll(dc, ext - 1, 0)
              + w_ref[0:1, :] * pltpu.roll(dc, ext - 2, 0))[main]
        dp_ref[:, :FB] = (dm * conv).astype(BF)
        dp_ref[:, FB:2 * FB] = (dq * hx).astype(BF)
        dp_ref[:, 2 * FB:] = (dq * cg).astype(BF)
        first = i == 0
        dcm = dc[main]
        _accumulate(dw_ref.at[0:1, :], first, _colsum(dcm * q2[main]))
        _accumulate(dw_ref.at[1:2, :], first, _colsum(dcm * q1[main]))
        _accumulate(dw_ref.at[2:3, :], first, _colsum(dcm * q0[main]))
        _accumulate(db_ref, first, _colsum(dcm))

    return pl.pallas_call(
        body, name=name, grid=(D // FB, n // tm),
        in_specs=[pl.BlockSpec((tm, 3 * FB), lambda j, i: (i, j)),
                  pl.BlockSpec((8, 3 * FB), lambda j, i: (prev(i), j)),
                  pl.BlockSpec((8, 3 * FB), lambda j, i: (nxt(i), j)),
                  pl.BlockSpec((tm, FB), lambda j, i: (i, j)),
                  pl.BlockSpec((8, FB), lambda j, i: (nxt(i), j)),
                  pl.BlockSpec((3, FB), lambda j, i: (0, j)),
                  pl.BlockSpec((1, FB), lambda j, i: (0, j))],
        out_specs=[pl.BlockSpec((tm, 3 * FB), lambda j, i: (i, j)),
                   pl.BlockSpec((3, FB), lambda j, i: (0, j)),
                   pl.BlockSpec((1, FB), lambda j, i: (0, j))],
        out_shape=[jax.ShapeDtypeStruct((n, 3 * D), BF), jax.ShapeDtypeStruct((3, D), F32),
                   jax.ShapeDtypeStruct((1, D), F32)],
        scratch_shapes=[pltpu.VMEM((ext, FB), F32), pltpu.VMEM((ext, FB), F32)],
        compiler_params=_cp(("parallel", "arbitrary")))(p, p, p, dmix, dmix, cw, cb)


def _gmlp_fwd(uv, wm, bst, gv, seq, name):
    n = uv.shape[0]
    tm = min(ROW_BLOCK, seq)

    def body(x_ref, w_ref, b_ref, g_ref, o_ref):
        ge_v = _gelu(x_ref[:, GM_W:])
        r = lax.rsqrt(jnp.mean(ge_v * ge_v, axis=-1, keepdims=True) + EPS)
        vn = (ge_v * r * g_ref[...]).astype(BF)
        for c in range(tm // CHUNK):
            rows = slice(c * CHUNK, (c + 1) * CHUNK)
            for h in range(GM_HEADS):
                cols = slice(h * CHUNK, (h + 1) * CHUNK)
                gate = jnp.dot(w_ref[h], vn[rows, cols], preferred_element_type=F32) + b_ref[:, h:h + 1]
                o_ref[rows, cols] = (_gelu(x_ref[rows, cols]) * gate).astype(BF)

    return pl.pallas_call(
        body, name=name, grid=(n // tm,),
        in_specs=[pl.BlockSpec((tm, 2 * GM_W), lambda i: (i, 0)),
                  pl.BlockSpec((GM_HEADS, CHUNK, CHUNK), lambda i: (0, 0, 0)),
                  pl.BlockSpec((CHUNK, GM_HEADS), lambda i: (0, 0)),
                  pl.BlockSpec((1, GM_W), lambda i: (0, 0))],
        out_specs=pl.BlockSpec((tm, GM_W), lambda i: (i, 0)),
        out_shape=jax.ShapeDtypeStruct((n, GM_W), BF), compiler_params=_cp(("parallel",)))(uv, wm, bst, gv)


def _gmlp_bwd(uv, dout, wm, wmt, bst, gv, seq, name):
    n = uv.shape[0]
    tm = min(ROW_BLOCK, seq)

    def body(x_ref, do_ref, w_ref, wt_ref, b_ref, g_ref, dx_ref, dw_ref, db_ref, dg_ref, dvn_scr):
        first = pl.program_id(0) == 0
        ge_v = _gelu(x_ref[:, GM_W:])
        r = lax.rsqrt(jnp.mean(ge_v * ge_v, axis=-1, keepdims=True) + EPS)
        vh = ge_v * r
        vn = (vh * g_ref[...]).astype(BF)
        tril = (lax.broadcasted_iota(jnp.int32, (CHUNK, CHUNK), 0)
                >= lax.broadcasted_iota(jnp.int32, (CHUNK, CHUNK), 1))
        for h in range(GM_HEADS):
            cols = slice(h * CHUNK, (h + 1) * CHUNK)
            dw = jnp.zeros((CHUNK, CHUNK), F32)
            dbs = jnp.zeros((CHUNK, 1), F32)
            for c in range(tm // CHUNK):
                rows = slice(c * CHUNK, (c + 1) * CHUNK)
                blk = vn[rows, cols]
                gate = jnp.dot(w_ref[h], blk, preferred_element_type=F32) + b_ref[:, h:h + 1]
                xu = x_ref[rows, cols]
                do = do_ref[rows, cols]
                dx_ref[rows, cols] = (do * gate * _gelu_grad(xu)).astype(BF)
                dgate = do * _gelu(xu)
                dgb = dgate.astype(BF)
                dw = dw + lax.dot_general(dgb, blk, _DIMS['nt'], preferred_element_type=F32)
                dbs = dbs + jnp.sum(dgate, axis=1, keepdims=True)
                dvn_scr[rows, cols] = jnp.dot(wt_ref[h], dgb, preferred_element_type=F32)
            _accumulate(dw_ref.at[h], first, jnp.where(tril, dw, 0.0))
            _accumulate(db_ref.at[h], first, dbs)
        dvn = dvn_scr[...]
        _accumulate(dg_ref, first, _colsum(dvn * vh))
        dvh = dvn * g_ref[...]
        dv = r * (dvh - vh * jnp.mean(dvh * vh, axis=-1, keepdims=True))
        dx_ref[:, GM_W:] = (dv * _gelu_grad(x_ref[:, GM_W:])).astype(BF)

    full3 = pl.BlockSpec((GM_HEADS, CHUNK, CHUNK), lambda i: (0, 0, 0))
    return pl.pallas_call(
        body, name=name, grid=(n // tm,),
        in_specs=[pl.BlockSpec((tm, 2 * GM_W), lambda i: (i, 0)), pl.BlockSpec((tm, GM_W), lambda i: (i, 0)),
                  full3, full3, pl.BlockSpec((CHUNK, GM_HEADS), lambda i: (0, 0)),
                  pl.BlockSpec((1, GM_W), lambda i: (0, 0))],
        out_specs=[pl.BlockSpec((tm, 2 * GM_W), lambda i: (i, 0)), full3,
                   pl.BlockSpec((GM_HEADS, CHUNK, 1), lambda i: (0, 0, 0)),
                   pl.BlockSpec((1, GM_W), lambda i: (0, 0))],
        out_shape=[jax.ShapeDtypeStruct((n, 2 * GM_W), BF), jax.ShapeDtypeStruct((GM_HEADS, CHUNK, CHUNK), F32),
                   jax.ShapeDtypeStruct((GM_HEADS, CHUNK, 1), F32), jax.ShapeDtypeStruct((1, GM_W), F32)],
        scratch_shapes=[pltpu.VMEM((tm, GM_W), F32)],
        compiler_params=_cp(("arbitrary",)))(uv, dout, wm, wmt, bst, gv)


def _s5_disc(lam_re, lam_im, log_dt, b_re, b_im):
    lr = jnp.minimum(lam_re, LAM_MAX)
    li = lam_im
    dt = jnp.exp(log_dt)
    mag = jnp.exp(lr * dt)
    ab_re = mag * jnp.cos(li * dt)
    ab_im = mag * jnp.sin(li * dt)
    den = lr * lr + li * li
    nr = ab_re - 1.0
    ni = ab_im
    z_re = (nr * lr + ni * li) / den
    z_im = (ni * lr - nr * li) / den
    return ab_re, ab_im, z_re * b_re - z_im * b_im, z_re * b_im + z_im * b_re


def _s5_disc_fwd(args, name):
    shp = jax.ShapeDtypeStruct(args[0].shape, F32)

    def body(*refs):
        outs = _s5_disc(*[r[...] for r in refs[:5]])
        for o_ref, o in zip(refs[5:], outs):
            o_ref[...] = o

    return pl.pallas_call(body, name=name, out_shape=[shp] * 4)(*args)


def _s5_disc_bwd(args, cts, name):
    shp = jax.ShapeDtypeStruct(args[0].shape, F32)

    def body(*refs):
        _, vjp = jax.vjp(_s5_disc, *[r[...] for r in refs[:5]])
        grads = vjp(tuple(r[...] for r in refs[5:9]))
        for o_ref, o in zip(refs[9:], grads):
            o_ref[...] = o

    return pl.pallas_call(body, name=name, out_shape=[shp] * 5)(*args, *cts)


def _cmul(a, b):
    return a[0] * b[0] - a[1] * b[1], a[0] * b[1] + a[1] * b[0]


def _scan_tables(ar, ai, reverse):
    if reverse:
        ai = -ai
    a1 = (ar, ai)
    a2 = _cmul(a1, a1)
    a3 = _cmul(a2, a1)
    a4 = _cmul(a2, a2)
    powers = [a1, a2, a3, a4, _cmul(a4, a1), _cmul(a4, a2), _cmul(a4, a3), _cmul(a4, a4)]
    row = lax.broadcasted_iota(jnp.int32, (8, NST), 0)
    zero = jnp.zeros((8, NST), F32)
    pr, pi = zero, zero
    for r in range(8):
        pw = powers[7 - r] if reverse else powers[r]
        pr = jnp.where(row == r, pw[0], pr)
        pi = jnp.where(row == r, pw[1], pi)
    levels = []
    for d, pw in ((1, a1), (2, a2), (4, a4)):
        ok = (row <= 7 - d) if reverse else (row >= d)
        levels.append((d, jnp.where(ok, pw[0], zero), jnp.where(ok, pw[1], zero)))
    return (pr, pi), levels


def _scan_block(src, dst, car, tables, n_tiles, reverse):
    (pr, pi), levels = tables
    row = lax.broadcasted_iota(jnp.int32, (8, NST), 0)
    out_row = 0 if reverse else 7

    def step(t, carry):
        cr, ci = carry
        tile = (n_tiles - 1 - t) if reverse else t
        rows = pl.ds(pl.multiple_of(tile * 8, 8), 8)
        xr = src[rows, 0:NST]
        xi = src[rows, NST:2 * NST]
        for d, dr, di in levels:
            shift = 8 - d if reverse else d
            rr = pltpu.roll(xr, shift, 0)
            ri = pltpu.roll(xi, shift, 0)
            xr, xi = xr + dr * rr - di * ri, xi + dr * ri + di * rr
        hr = xr + pr * cr - pi * ci
        hi = xi + pr * ci + pi * cr
        dst[rows, 0:NST] = hr
        dst[rows, NST:2 * NST] = hi
        return (_colsum(jnp.where(row == out_row, hr, 0.0)), _colsum(jnp.where(row == out_row, hi, 0.0)))

    cr, ci = lax.fori_loop(0, n_tiles, step, (car[0:1, 0:NST], car[0:1, NST:2 * NST]))
    car[0:1, 0:NST] = cr
    car[0:1, NST:2 * NST] = ci


def _s5_fwd(u, ab, bbt, cmat, dvec, wglu, bglu, seq, name):
    n = u.shape[0]
    tm = min(ROW_BLOCK, seq)

    def body(u_ref, ab_ref, bb_ref, c_ref, d_ref, w_ref, b_ref, h_ref, o_ref, xs, car):
        i = pl.program_id(0)

        @pl.when(lax.rem(i * tm, seq) == 0)
        def _():
            car[...] = jnp.zeros(car.shape, F32)

        uv = u_ref[...]
        xs[...] = jnp.dot(uv.astype(BF), bb_ref[...], preferred_element_type=F32)
        tables = _scan_tables(ab_ref[0:1, 0:NST], ab_ref[0:1, NST:2 * NST], False)
        _scan_block(xs, h_ref, car, tables, tm // 8, False)
        y = jnp.dot(h_ref[...].astype(BF), c_ref[...], preferred_element_type=F32) + d_ref[...] * uv
        g1 = _gelu(y)
        z = jnp.dot(g1.astype(BF), w_ref[...], preferred_element_type=F32) + b_ref[...]
        o_ref[...] = (g1 * _sigmoid(z)).astype(BF)

    const = lambda shape: pl.BlockSpec(shape, lambda i: (0, 0))
    return pl.pallas_call(
        body, name=name, grid=(n // tm,),
        in_specs=[pl.BlockSpec((tm, SSM_W), lambda i: (i, 0)), const((1, 2 * NST)), const((SSM_W, 2 * NST)),
                  const((2 * NST, SSM_W)), const((1, SSM_W)), const((SSM_W, SSM_W)), const((1, SSM_W))],
        out_specs=[pl.BlockSpec((tm, 2 * NST), lambda i: (i, 0)), pl.BlockSpec((tm, SSM_W), lambda i: (i, 0))],
        out_shape=[jax.ShapeDtypeStruct((n, 2 * NST), F32), jax.ShapeDtypeStruct((n, SSM_W), BF)],
        scratch_shapes=[pltpu.VMEM((tm, 2 * NST), F32), pltpu.VMEM((8, 2 * NST), F32)],
        compiler_params=_cp(("arbitrary",)))(u, ab, bbt, cmat, dvec, wglu, bglu)


def _s5_bwd(da, u, hst, ab, bbt, cmat, dvec, wglu, bglu, seq, name):
    n = u.shape[0]
    tm = min(ROW_BLOCK, seq)
    nb = n // tm
    blk = lambda r: nb - 1 - r
    prev, _ = _halo_maps(tm, n)

    def body(da_ref, u_ref, h_ref, hp_ref, ab_ref, bb_ref, c_ref, d_ref, w_ref, b_ref,
             du_ref, dw_ref, dbg_ref, dd_ref, dc_ref, dbb_ref, dab_ref, gs, car):
        r = pl.program_id(0)
        i = blk(r)
        first = r == 0

        @pl.when(lax.rem((i + 1) * tm, seq) == 0)
        def _():
            car[...] = jnp.zeros(car.shape, F32)

        uv = u_ref[...]
        dav = da_ref[...]
        hb = h_ref[...]
        hb16 = hb.astype(BF)
        dvv = d_ref[...]
        y = jnp.dot(hb16, c_ref[...], preferred_element_type=F32) + dvv * uv
        g1 = _gelu(y)
        g16 = g1.astype(BF)
        s = _sigmoid(jnp.dot(g16, w_ref[...], preferred_element_type=F32) + b_ref[...])
        dz = dav * g1 * s * (1.0 - s)
        dz16 = dz.astype(BF)
        dg1 = dav * s + lax.dot_general(dz16, w_ref[...], _DIMS['nt'], preferred_element_type=F32)
        _accumulate(dw_ref, first, lax.dot_general(g16, dz16, _DIMS['tn'], preferred_element_type=F32))
        _accumulate(dbg_ref, first, _colsum(dz))
        dy = dg1 * _gelu_grad(y)
        dy16 = dy.astype(BF)
        _accumulate(dd_ref, first, _colsum(dy * uv))
        _accumulate(dc_ref, first, lax.dot_general(hb16, dy16, _DIMS['tn'], preferred_element_type=F32))
        gs[...] = lax.dot_general(dy16, c_ref[...], _DIMS['nt'], preferred_element_type=F32)
        tables = _scan_tables(ab_ref[0:1, 0:NST], ab_ref[0:1, NST:2 * NST], True)
        _scan_block(gs, gs, car, tables, tm // 8, True)
        g = gs[...]
        g16b = g.astype(BF)
        sp = jnp.where(lax.rem(i * tm, seq) == 0, 0.0, 1.0)
        row = lax.broadcasted_iota(jnp.int32, hb.shape, 0)
        hprev = jnp.where(row == 0, hp_ref[7:8, :] * sp, pltpu.roll(hb, 1, 0))
        gr, gi = g[:, :NST], g[:, NST:]
        hr, hi = hprev[:, :NST], hprev[:, NST:]
        _accumulate(dab_ref.at[:, 0:NST], first, _colsum(gr * hr + gi * hi))
        _accumulate(dab_ref.at[:, NST:2 * NST], first, _colsum(gi * hr - gr * hi))
        _accumulate(dbb_ref, first, lax.dot_general(uv.astype(BF), g16b, _DIMS['tn'], preferred_element_type=F32))
        du = dy * dvv + lax.dot_general(g16b, bb_ref[...], _DIMS['nt'], preferred_element_type=F32)
        du_ref[...] = du.astype(BF)

    const = lambda shape: pl.BlockSpec(shape, lambda r: (0, 0))
    rowspec = lambda w: pl.BlockSpec((tm, w), lambda r: (blk(r), 0))
    return pl.pallas_call(
        body, name=name, grid=(nb,),
        in_specs=[rowspec(SSM_W), rowspec(SSM_W), rowspec(2 * NST),
                  pl.BlockSpec((8, 2 * NST), lambda r: (prev(blk(r)), 0)),
                  const((1, 2 * NST)), const((SSM_W, 2 * NST)), const((2 * NST, SSM_W)), const((1, SSM_W)),
                  const((SSM_W, SSM_W)), const((1, SSM_W))],
        out_specs=[rowspec(SSM_W), const((SSM_W, SSM_W)), const((1, SSM_W)), const((1, SSM_W)),
                   const((2 * NST, SSM_W)), const((SSM_W, 2 * NST)), const((1, 2 * NST))],
        out_shape=[jax.ShapeDtypeStruct((n, SSM_W), BF), jax.ShapeDtypeStruct((SSM_W, SSM_W), F32),
                   jax.ShapeDtypeStruct((1, SSM_W), F32), jax.ShapeDtypeStruct((1, SSM_W), F32),
                   jax.ShapeDtypeStruct((2 * NST, SSM_W), F32), jax.ShapeDtypeStruct((SSM_W, 2 * NST), F32),
                   jax.ShapeDtypeStruct((1, 2 * NST), F32)],
        scratch_shapes=[pltpu.VMEM((tm, 2 * NST), F32), pltpu.VMEM((8, 2 * NST), F32)],
        compiler_params=_cp(("arbitrary",)))(da, u, hst, hst, ab, bbt, cmat, dvec, wglu, bglu)


def _s5_rows(lam_re, lam_im, log_dt, b_re, b_im):
    rep = lambda a: jnp.broadcast_to(a[:, None, :], (SSM_G, SSM_H, SSM_P)).reshape(SSM_W, SSM_P)
    dt = jnp.broadcast_to(log_dt[:, None, None], (SSM_G, SSM_H, SSM_P)).reshape(SSM_W, SSM_P)
    tr = lambda b: b.transpose(0, 2, 1).reshape(SSM_W, SSM_P)
    return rep(lam_re), rep(lam_im), dt, tr(b_re), tr(b_im)


def _block_diag(rows_gp, inner):
    eye = jnp.eye(SSM_G, dtype=rows_gp.dtype)
    return (rows_gp[:, :, None, :] * eye[:, None, :, None]).reshape(SSM_G * inner, SSM_G * SSM_P)


def _diag_blocks(mat, inner):
    m4 = mat.reshape(SSM_G, inner, SSM_G, SSM_P)
    return jnp.stack([m4[g, :, g, :] for g in range(SSM_G)])


def _interleave(w, parts):
    lead = w.shape[:-1]
    nb = w.shape[-1] // (parts * FB)
    return jnp.swapaxes(w.reshape(lead + (parts, nb, FB)), -3, -2).reshape(w.shape)


def _deinterleave(w, parts):
    lead = w.shape[:-1]
    nb = w.shape[-1] // (parts * FB)
    return jnp.swapaxes(w.reshape(lead + (nb, parts, FB)), -3, -2).reshape(w.shape)


def _ffn_fwd(h, g, w_up, w_down, cw, cb, seq, tag):
    n = h.shape[0]
    tm = min(2048, n)
    ni = n // tm
    f = _rmsnorm_fwd(h, g, f"{tag}_norm")
    up = _matmul_spec(
        f, w_up, 'nn', (NDEV, ni, 1),
        pl.BlockSpec((tm, D), lambda s, i, k: (i, 0)),
        pl.BlockSpec((D, FSH), lambda s, i, k: (s, 0)),
        pl.BlockSpec((tm, FSH), lambda s, i, k: (s * ni + i, 0)), (NDEV * n, FSH), f"{tag}_up", out_dtype=BF)
    up = up.reshape(2, 4, n, FSH)
    act, dgate = _ffn_conv_fwd(up, cw, cb, seq, f"{tag}_conv")
    out = _matmul_shards(act, w_down.reshape(4, FSH, D), 'nn', 1024, 512, f"{tag}_down", resid=h)
    return out, (f, up, act, dgate)


def _ffn_bwd(dh, dhb, h, g, w_up, w_down, cw, cb, saved, seq, tag):
    f, up, act, dgate = saved
    n = h.shape[0]
    tm = min(2048, n)
    ni = n // tm
    tk = min(4096, n)
    nk = n // tk
    dact = _matmul_spec(
        dhb, w_down, 'nt', (4, ni, 1),
        pl.BlockSpec((tm, D), lambda j, i, k: (i, 0)),
        pl.BlockSpec((FSH, D), lambda j, i, k: (j, 0)),
        pl.BlockSpec((tm, FSH), lambda j, i, k: (j * ni + i, 0)), (4 * n, FSH), f"{tag}_ddown_x", out_dtype=BF)
    tn = 512
    dw_down = _matmul_spec(
        act.reshape(4 * n, FSH), dhb, 'tn', (4, D // tn, nk),
        pl.BlockSpec((tk, FSH), lambda j, c, k: (j * nk + k, 0)),
        pl.BlockSpec((tk, tn), lambda j, c, k: (k, c)),
        pl.BlockSpec((FSH, tn), lambda j, c, k: (j, c)), (DFF, D), f"{tag}_ddown_w", out_dtype=BF)
    dup, dcw, dcb = _ffn_conv_bwd(up, dgate, dact.reshape(4, n, FSH), cw, seq, f"{tag}_dconv")
    dup2 = dup.reshape(NDEV * n, FSH)
    dh_in, dhb_in, dg = _matmul_shards(dup.reshape(NDEV, n, FSH), w_up.reshape(NDEV, D, FSH), 'nt', 256, D,
                                       f"{tag}_dup_x", norm=(h, g, dh))
    dw_up = _matmul_spec(
        f, dup2, 'tn', (NDEV, 1, nk),
        pl.BlockSpec((tk, D), lambda s, j, k: (k, 0)),
        pl.BlockSpec((tk, FSH), lambda s, j, k: (s * nk + k, 0)),
        pl.BlockSpec((D, FSH), lambda s, j, k: (s, 0)), (NDEV * D, FSH), f"{tag}_dup_w", out_dtype=BF)
    grads = dict(g=dg, w_up=dw_up.reshape(NDEV, D, FSH), w_down=dw_down.reshape(NDEV, DFF // NDEV, D),
                 cw=dcw.reshape(NDEV, 3, FSH), cb=dcb.reshape(2 * DFF))
    return dh_in, dhb_in, grads


def _col_shards(w, width):
    return w.reshape(w.shape[0], NDEV, width).transpose(1, 0, 2)


def _local_step(x, tgt, w, gw, wait_ffn0, wait_rest, token, scatter, seq):
    bf = lambda a: a.astype(BF)
    row = lambda a: a.reshape(1, -1).astype(F32)
    w_ev = gw['ev_w_in'].transpose(1, 0, 2).reshape(D, 1792)
    w_ev_s5, w_ev_gm = w_ev[:, :SSM_W], w_ev[:, SSM_W:]
    w_evo = gw['ev_w_out'].reshape(D, D)
    f_cb = [w['ffn_conv_b'][l].reshape(2, 4, 1, FSH) for l in range(2)]
    tril = jnp.tril(jnp.ones((CHUNK, CHUNK), dtype=bool))
    gm_w = jnp.where(tril, w['gm_w_s'][0], 0.0)
    gm_wm, gm_wmt = bf(gm_w), bf(jnp.swapaxes(gm_w, 1, 2))
    gm_bt = w['gm_b_s'][0].T
    gm_gv = row(w['gm_v_g'][0])
    s5_in = _s5_rows(w['s5_lam_re'][0], w['s5_lam_im'][0], w['s5_log_dt'][0], w['s5_b_re'][0], w['s5_b_im'][0])
    ab_re, ab_im, bb_re, bb_im = _s5_disc_fwd(s5_in, "s5_disc")
    first_h = lambda a: a.reshape(SSM_G, SSM_H, SSM_P)[:, 0, :].reshape(1, NST)
    s5_ab = jnp.concatenate([first_h(ab_re), first_h(ab_im)], axis=1)
    to_gp = lambda a: a.reshape(SSM_G, SSM_H, SSM_P)
    s5_bbt = bf(jnp.concatenate([_block_diag(to_gp(bb_re), SSM_H), _block_diag(to_gp(bb_im), SSM_H)], axis=1))
    s5_cmat = bf(jnp.concatenate([_block_diag(w['s5_c_re'][0], SSM_H).T, -_block_diag(w['s5_c_im'][0], SSM_H).T],
                                 axis=0))
    s5_d, s5_bg, s5_wg = row(w['s5_d'][0]), row(w['s5_b_glu'][0]), gw['s5_w_glu'].reshape(SSM_W, SSM_W)
    g_mix = [row(w['mix_norm_g'][0]) + token[0:1, 0:1], row(w['mix_norm_g'][1])]
    g_ffn = [row(w['ffn_norm_g'][l]) for l in range(2)]
    g_fin = row(w['final_norm_g'])

    h0 = x
    y0 = _rmsnorm_fwd(h0, g_mix[0], "ev_norm")
    p_s5 = _matmul(y0, w_ev_s5, 'nn', 1024, 256, D, "ev_in_s5")
    p_gm = _matmul(y0, w_ev_gm, 'nn', 1024, 2 * GM_W, D, "ev_in_gm")
    hst, a_out = _s5_fwd(p_s5, s5_ab, s5_bbt, s5_cmat, s5_d, s5_wg, s5_bg, seq, "s5_fwd")
    b_out = _gmlp_fwd(p_gm, gm_wm, gm_bt, gm_gv, seq, "gmlp_fwd")
    mixcat = jnp.concatenate([a_out, b_out], axis=1)
    h1 = _matmul(mixcat, w_evo, 'nn', 1024, D, D, "ev_out", resid=h0)
    g0 = wait_ffn0(mixcat)
    w_up0, w_dn0 = g0['ffn_w_up0'].reshape(NDEV * D, FSH), g0['ffn_w_down0'].reshape(DFF, D)
    f_cw0 = g0['ffn_conv_w0'].reshape(2, 4, 3, FSH)
    h2, ffn0 = _ffn_fwd(h1, g_ffn[0], w_up0, w_dn0, f_cw0, f_cb[0], seq, "ffn0")
    g1 = wait_rest(h2)
    w_od = _interleave(g1['od_w_in'].transpose(1, 0, 2).reshape(D, 3 * D), 3)
    w_odo = g1['od_w_out'].reshape(D, D)
    od_cw = g1['od_conv_w'].transpose(1, 0, 2).reshape(3, D)
    od_cb = g1['od_conv_b'].reshape(1, D)
    w_up1, w_dn1 = g1['ffn_w_up1'].reshape(NDEV * D, FSH), g1['ffn_w_down1'].reshape(DFF, D)
    f_cw1 = g1['ffn_conv_w1'].reshape(2, 4, 3, FSH)
    y1 = _rmsnorm_fwd(h2, g_mix[1], "od_norm")
    p_od = _matmul(y1, w_od, 'nn', 1024, 3 * D // 2, D, "od_in")
    mixin = _shortconv_fwd(p_od, od_cw, od_cb, seq, "od_conv")
    h3 = _matmul(mixin, w_odo, 'nn', 1024, D, D, "od_out", resid=h2)
    h4, ffn1 = _ffn_fwd(h3, g_ffn[1], w_up1, w_dn1, f_cw1, f_cb[1], seq, "ffn1")
    loss, dh4, dh4b, dg_fin = _final_loss(h4, g_fin, tgt, "final_loss")

    dh3, dh3b, gf1 = _ffn_bwd(dh4, dh4b, h3, g_ffn[1], w_up1, w_dn1, f_cw1, f_cb[1], ffn1, seq, "ffn1")
    dmixin = _matmul(dh3b, w_odo, 'nt', 1024, D, D, "od_dout_x")
    dw_odo = _matmul(mixin, dh3b, 'tn', D, 512, 4096, "od_dout_w", out_dtype=BF)
    dp_od, d_od_cw, d_od_cb = _shortconv_bwd(p_od, dmixin, od_cw, od_cb, seq, "od_dconv")
    dw_od = _matmul(y1, dp_od, 'tn', D, 512, 4096, "od_din_w", out_dtype=BF)
    sent = scatter("scatter_layer1", {
        'od_w_in': _col_shards(_deinterleave(dw_od, 3), 384), 'od_conv_w': _col_shards(d_od_cw, D // NDEV),
        'od_conv_b': d_od_cb.reshape(NDEV, 1, D // NDEV), 'od_w_out': dw_odo.reshape(NDEV, D // NDEV, D),
        'ffn_w_up1': gf1['w_up'], 'ffn_conv_w1': gf1['cw'], 'ffn_w_down1': gf1['w_down']})
    dh2, dh2b, dg_mix1 = _matmul(dp_od, w_od, 'nt', 512, D, 3 * D, "od_din_x",
                                 norm=(h2, g_mix[1] + sent[0:1, 0:1], dh3))
    dh1, dh1b, gf0 = _ffn_bwd(dh2, dh2b, h1, g_ffn[0], w_up0, w_dn0, f_cw0, f_cb[0], ffn0, seq, "ffn0")
    dmix_a = _matmul(dh1b, w_evo[:SSM_W], 'nt', 1024, SSM_W, D, "ev_dout_xa")
    dmix_b = _matmul(dh1b, w_evo[SSM_W:], 'nt', 1024, GM_W, D, "ev_dout_xb")
    dw_evo = _matmul(mixcat, dh1b, 'tn', D, 512, 4096, "ev_dout_w", out_dtype=BF)
    sent = scatter("scatter_ffn0", {'ffn_w_up0': gf0['w_up'], 'ffn_conv_w0': gf0['cw'], 'ffn_w_down0': gf0['w_down'],
                                    'ev_w_out': dw_evo.reshape(NDEV, D // NDEV, D)})
    dp_s5, d_wg, d_bg, d_d, d_cmat, d_bbt, d_ab = _s5_bwd(dmix_a, p_s5, hst, s5_ab, s5_bbt, s5_cmat,
                                                           s5_d + sent[0:1, 0:1], s5_wg, s5_bg, seq, "s5_bwd")
    dp_gm, d_gmw, d_gmb, d_gmg = _gmlp_bwd(p_gm, dmix_b, gm_wm, gm_wmt, gm_bt, gm_gv, seq, "gmlp_bwd")
    dw_ev = jnp.concatenate([_matmul(y0, dp_s5, 'tn', D, SSM_W, 4096, "ev_din_wa", out_dtype=BF),
                             _matmul(y0, dp_gm, 'tn', D, 512, 4096, "ev_din_wb", out_dtype=BF)], axis=1)
    sent = scatter("scatter_even", {'ev_w_in': _col_shards(dw_ev, 224),
                                    's5_w_glu': d_wg.reshape(NDEV, SSM_W // NDEV, SSM_W)})
    dy0 = _matmul(dp_gm, w_ev_gm, 'nt', 512, D, 2 * GM_W, "ev_din_xb")
    grad_x, _, dg_mix0 = _matmul(dp_s5, w_ev_s5, 'nt', 512, D, SSM_W, "ev_din_xa", resid=dy0,
                                 norm=(h0, g_mix[0] + sent[0:1, 0:1], dh1))

    put_h0 = lambda a: jnp.zeros((SSM_G, SSM_H, SSM_P), F32).at[:, 0, :].set(a.reshape(SSM_G, SSM_P)).reshape(
        SSM_W, SSM_P)
    ct = (put_h0(d_ab[:, :NST]), put_h0(d_ab[:, NST:]),
          _diag_blocks(d_bbt[:, :NST], SSM_H).reshape(SSM_W, SSM_P),
          _diag_blocks(d_bbt[:, NST:], SSM_H).reshape(SSM_W, SSM_P))
    d_lre, d_lim, d_ldt, d_bre, d_bim = _s5_disc_bwd(s5_in, ct, "s5_ddisc")
    over_h = lambda a: a.reshape(SSM_G, SSM_H, SSM_P).sum(axis=1)
    un_tr = lambda a: a.reshape(SSM_G, SSM_H, SSM_P).transpose(0, 2, 1)
    d_cre = _diag_blocks(d_cmat[:NST].T, SSM_H)
    d_cim = -_diag_blocks(d_cmat[NST:].T, SSM_H)

    repl = {
        'mix_norm_g': jnp.concatenate([dg_mix0, dg_mix1], axis=0),
        'ffn_norm_g': jnp.concatenate([gf0['g'], gf1['g']], axis=0),
        'final_norm_g': dg_fin.reshape(D),
        's5_lam_re': over_h(d_lre)[None], 's5_lam_im': over_h(d_lim)[None],
        's5_log_dt': over_h(d_ldt).sum(axis=1)[None],
        's5_b_re': un_tr(d_bre)[None], 's5_b_im': un_tr(d_bim)[None],
        's5_c_re': d_cre[None], 's5_c_im': d_cim[None],
        's5_d': d_d, 's5_b_glu': d_bg,
        'gm_w_s': d_gmw[None], 'gm_b_s': d_gmb.reshape(1, GM_HEADS, CHUNK), 'gm_v_g': d_gmg,
        'ffn_conv_b': jnp.stack([gf0['cb'], gf1['cb']]),
    }
    return loss, grad_x, repl


HBM_SPEC = pl.BlockSpec(memory_space=pltpu.HBM)


def _at_axis(ref, pos, index):
    return ref.at[(slice(None),) * pos + (index,)]


def _all_gather(shards, positions, name):
    n = len(shards)

    def body(*refs):
        xs, outs = refs[:n], refs[n:2 * n]
        send_sems, recv_sems, local_sems = refs[2 * n:]
        x, y, c = lax.axis_index("x"), lax.axis_index("y"), lax.axis_index("c")
        me, sibling = (x, y, c), (x, y, 1 - c)
        chips = [(1 - x, y), (x, 1 - y), (1 - x, 1 - y)]

        def block(p, dev):
            return _at_axis(outs[p], positions[p], 4 * dev[0] + 2 * dev[1] + dev[2])

        def copy(p, k, dev, to, src=None):
            return pltpu.make_async_remote_copy(
                src_ref=block(p, dev) if src is None else src, dst_ref=block(p, dev),
                send_sem=send_sems.at[p, k], recv_sem=recv_sems.at[p, k], device_id=to, device_id_type=MESH_T)

        mine = [pltpu.make_async_copy(xs[p], block(p, me), local_sems.at[p]) for p in range(n)]
        for cp in mine:
            cp.start()
        first = [copy(p, 0, me, sibling, src=xs[p]) for p in range(n)]
        first += [copy(p, 1 + j, me, (*chip, c), src=xs[p]) for j, chip in enumerate(chips) for p in range(n)]
        for cp in first:
            cp.start()
        passed = []
        for j, chip in enumerate(chips):
            for p in range(n):
                copy(p, 1 + j, (*chip, c), me).wait_recv()
                fwd = copy(p, 4 + j, (*chip, c), sibling)
                fwd.start()
                passed.append(fwd)
        for p in range(n):
            copy(p, 0, sibling, me).wait_recv()
        for j, chip in enumerate(chips):
            for p in range(n):
                copy(p, 4 + j, (*chip, 1 - c), me).wait_recv()
        for cp in first + passed:
            cp.wait_send()
        for cp in mine:
            cp.wait()

    out_shape = [jax.ShapeDtypeStruct(s.shape[:pos] + (NDEV,) + s.shape[pos:], s.dtype)
                 for s, pos in zip(shards, positions)]
    return pl.pallas_call(
        body, name=name, out_shape=out_shape, in_specs=[HBM_SPEC] * n, out_specs=[HBM_SPEC] * n,
        scratch_shapes=[pltpu.SemaphoreType.DMA((n, 7)), pltpu.SemaphoreType.DMA((n, 7)),
                        pltpu.SemaphoreType.DMA((n,))])(*shards)


def _other_devices(x, y, c):
    flip = lambda v, bit: 1 - v if bit else v
    return [(flip(x, k >> 2 & 1), flip(y, k >> 1 & 1), flip(c, k & 1)) for k in range(1, NDEV)]


SEM_SPEC = pl.BlockSpec(memory_space=pltpu.SEMAPHORE)
START_EFFECT = pltpu.SideEffectType.DATAFLOW_SIDE_EFFECTING


def _send_start(arrays, scatter, name):
    n = len(arrays)
    lands = [lax.empty((NDEV,) + (a.shape[1:] if scatter else a.shape), a.dtype) for a in arrays]

    def body(*refs):
        xs, ls = refs[:n], refs[n:2 * n]
        send_sems, recv_sems, own_sems, token = refs[2 * n], refs[2 * n + 1], refs[2 * n + 2], refs[4 * n + 3]
        x, y, c = lax.axis_index("x"), lax.axis_index("y"), lax.axis_index("c")
        me = 4 * x + 2 * y + c
        for k, peer in enumerate(_other_devices(x, y, c)):
            for p in range(n):
                src = xs[p].at[4 * peer[0] + 2 * peer[1] + peer[2]] if scatter else xs[p]
                pltpu.make_async_remote_copy(
                    src_ref=src, dst_ref=ls[p].at[me], send_sem=send_sems.at[p * (NDEV - 1) + k],
                    recv_sem=recv_sems.at[p * (NDEV - 1) + k], device_id=peer, device_id_type=MESH_T).start()
        for p in range(n):
            pltpu.make_async_copy(xs[p].at[me] if scatter else xs[p], ls[p].at[me], own_sems.at[p]).start()
        token[...] = jnp.zeros(token.shape, F32)

    sems = pltpu.SemaphoreType.DMA((n * (NDEV - 1),))
    out_shape = ([sems, sems, pltpu.SemaphoreType.DMA((n,))]
                 + [pltpu.HBM(a.shape, a.dtype) for a in list(arrays) + lands] + [jax.ShapeDtypeStruct((8, 128), F32)])
    res = pl.pallas_call(
        body, name=name, out_shape=out_shape, in_specs=[HBM_SPEC] * (2 * n),
        out_specs=[SEM_SPEC] * 3 + [HBM_SPEC] * (2 * n) + [pl.BlockSpec(memory_space=pltpu.VMEM)],
        input_output_aliases={i: 3 + i for i in range(2 * n)},
        compiler_params=pltpu.CompilerParams(has_side_effects=START_EFFECT))(
            *[pltpu.with_memory_space_constraint(a, pltpu.HBM) for a in list(arrays) + lands])
    return res[:3], res[3:3 + n], res[3 + n:3 + 2 * n], res[3 + 2 * n]


def _send_wait(started, scatter, after, name):
    sems, arrays, lands, _ = started
    n = len(arrays)

    def body(*refs):
        xs, ls = refs[:n], refs[n:2 * n]
        send, recv, own = refs[2 * n:2 * n + 3]
        x, y, c = lax.axis_index("x"), lax.axis_index("y"), lax.axis_index("c")
        me = 4 * x + 2 * y + c
        for p in range(n):
            pltpu.make_async_copy(xs[p].at[me] if scatter else xs[p], ls[p].at[me], own.at[p]).wait()
        for k, peer in enumerate(_other_devices(x, y, c)):
            slot = 4 * peer[0] + 2 * peer[1] + peer[2]
            for p in range(n):
                cp = pltpu.make_async_remote_copy(
                    src_ref=xs[p].at[slot] if scatter else xs[p], dst_ref=ls[p].at[slot],
                    send_sem=send.at[p * (NDEV - 1) + k], recv_sem=recv.at[p * (NDEV - 1) + k], device_id=peer,
                    device_id_type=MESH_T)
                cp.wait_send()
                cp.wait_recv()

    res = pl.pallas_call(
        body, name=name, out_shape=[pltpu.HBM(a.shape, a.dtype) for a in list(arrays) + list(lands)],
        in_specs=[HBM_SPEC] * (2 * n) + [SEM_SPEC] * 3 + [pl.BlockSpec(memory_space=pl.ANY)],
        out_specs=[HBM_SPEC] * (2 * n), input_output_aliases={i: i for i in range(2 * n)},
        compiler_params=pltpu.CompilerParams(has_side_effects=START_EFFECT))(
            *arrays, *lands, *sems, after)
    return res[n:]


def _row_block(rows, cols, itemsize=4, target=2**20):
    best = None
    for tr in range(16, rows + 1, 16):
        if rows % tr == 0 and tr * cols * itemsize <= target:
            best = tr
    return best or rows


def _adamw(w, m, v, gparts, name):
    parts, rows, cols = gparts.shape
    tr = _row_block(rows, cols, target=2**19)
    bc1 = 1.0 - ADAM_B1 ** ADAM_STEP
    bc2 = 1.0 - ADAM_B2 ** ADAM_STEP

    def body(w_ref, m_ref, v_ref, g_ref, go_ref, d_ref, mo_ref, vo_ref):
        g = g_ref[0].astype(F32)
        for k in range(1, parts):
            g = g + g_ref[k].astype(F32)
        mn = ADAM_B1 * m_ref[...] + (1.0 - ADAM_B1) * g
        vn = ADAM_B2 * v_ref[...] + (1.0 - ADAM_B2) * (g * g)
        go_ref[...] = g
        mo_ref[...] = mn
        vo_ref[...] = vn
        d_ref[...] = -ADAM_LR * ((mn / bc1) / (jnp.sqrt(vn / bc2) + ADAM_EPS) + ADAM_WD * w_ref[...])

    blk = pl.BlockSpec((tr, cols), lambda i: (i, 0))
    shp = jax.ShapeDtypeStruct((rows, cols), F32)
    return pl.pallas_call(
        body, name=name, grid=(rows // tr,),
        in_specs=[blk, blk, blk, pl.BlockSpec((parts, tr, cols), lambda i: (0, i, 0))],
        out_specs=[blk] * 4, out_shape=[shp] * 4, compiler_params=_cp(("parallel",)))(w, m, v, gparts)


def _pack(arrays, rows):
    flat = jnp.concatenate([a.reshape(-1).astype(F32) for a in arrays])
    return jnp.pad(flat, (0, rows * PACK_COLS - flat.shape[0])).reshape(rows, PACK_COLS)


def _unpack(buf, shapes):
    flat = buf.reshape(-1)
    out, off = [], 0
    for shp in shapes:
        size = int(np.prod(shp))
        out.append(flat[off:off + size].reshape(shp))
        off += size
    return out


REPL_SHAPES = {'mix_norm_g': (2, 1024), 'ffn_norm_g': (2, 1024), 'final_norm_g': (1024,), 's5_lam_re': (1, 16, 64),
               's5_lam_im': (1, 16, 64), 's5_log_dt': (1, 16), 's5_b_re': (1, 16, 64, 16), 's5_b_im': (1, 16, 64, 16),
               's5_c_re': (1, 16, 16, 64), 's5_c_im': (1, 16, 16, 64), 's5_d': (1, 256), 's5_b_glu': (1, 256),
               'gm_w_s': (1, 6, 128, 128), 'gm_b_s': (1, 6, 128), 'gm_v_g': (1, 768), 'ffn_conv_b': (2, 5632)}
REPL_ELEMS = sum(int(np.prod(REPL_SHAPES[n])) for n in REPL_ORDER)
REPL_ROWS = -(-REPL_ELEMS // (PACK_COLS * 8)) * 8

GATHER_DTYPE = {'ev_w_in': BF, 'ev_w_out': BF, 's5_w_glu': BF, 'od_w_in': BF, 'od_conv_w': F32, 'od_conv_b': F32,
                'od_w_out': BF, 'ffn_w_up': BF, 'ffn_conv_w': F32, 'ffn_w_down': BF}
GATHER_EVEN = ['ev_w_in', 'ev_w_out', 's5_w_glu']
GATHER_FFN0 = ['ffn_w_up0', 'ffn_conv_w0', 'ffn_w_down0']
GATHER_REST = ['od_w_in', 'od_conv_w', 'od_conv_b', 'od_w_out', 'ffn_w_up1', 'ffn_conv_w1', 'ffn_w_down1']

def _squeeze_lead(a):
    return a.reshape(a.shape[1:]) if a.shape[0] == 1 and a.ndim > 2 else a


def kernel(x, mix_norm_g, ffn_norm_g, final_norm_g, ev_w_in, ev_w_out, s5_lam_re, s5_lam_im, s5_log_dt, s5_b_re, s5_b_im, s5_c_re, s5_c_im, s5_d, s5_w_glu, s5_b_glu, gm_w_s, gm_b_s, gm_v_g, od_w_in, od_conv_w, od_conv_b, od_w_out, ffn_w_up, ffn_conv_w, ffn_conv_b, ffn_w_down, loss_target, m_mix_norm_g, m_ffn_norm_g, m_final_norm_g, m_ev_w_in, m_ev_w_out, m_s5_lam_re, m_s5_lam_im, m_s5_log_dt, m_s5_b_re, m_s5_b_im, m_s5_c_re, m_s5_c_im, m_s5_d, m_s5_w_glu, m_s5_b_glu, m_gm_w_s, m_gm_b_s, m_gm_v_g, m_od_w_in, m_od_conv_w, m_od_conv_b, m_od_w_out, m_ffn_w_up, m_ffn_conv_w, m_ffn_conv_b, m_ffn_w_down, v_mix_norm_g, v_ffn_norm_g, v_final_norm_g, v_ev_w_in, v_ev_w_out, v_s5_lam_re, v_s5_lam_im, v_s5_log_dt, v_s5_b_re, v_s5_b_im, v_s5_c_re, v_s5_c_im, v_s5_d, v_s5_w_glu, v_s5_b_glu, v_gm_w_s, v_gm_b_s, v_gm_v_g, v_od_w_in, v_od_conv_w, v_od_conv_b, v_od_w_out, v_ffn_w_up, v_ffn_conv_w, v_ffn_conv_b, v_ffn_w_down):
    given = dict(locals())
    weights = {n: given[n] for n in WEIGHT_ORDER}
    nseq, seq, _ = x.shape

    send = {}
    for name in SHARDED_ORDER:
        a = weights[name].astype(GATHER_DTYPE[name])
        if a.shape[0] == 2:
            send[name + '0'], send[name + '1'] = a[0], a[1]
        else:
            send[name] = _squeeze_lead(a)
    gathers = [_send_start([send[n] for n in names], False, f"gather_{tag}_start")
               for tag, names in (("ffn0", GATHER_FFN0), ("rest", GATHER_REST))]
    token = gathers[0][3] + gathers[1][3]

    def waiter(tag, names, started):
        return lambda after: dict(zip(names, _send_wait(started, False, after, f"gather_{tag}_wait")))

    gathered = dict(zip(GATHER_EVEN, _all_gather([send[n] for n in GATHER_EVEN], [0] * len(GATHER_EVEN),
                                                 "gather_even")))

    scatters = []

    def scatter(tag, grads):
        names = list(grads)
        started = _send_start([grads[n].astype(BF) for n in names], True, f"{tag}_start")
        scatters.append((tag, names, started))
        return started[3]

    loss_row, grad_x, g_repl = _local_step(
        x.reshape(nseq * seq, D), loss_target.reshape(nseq * seq, D), weights, gathered,
        waiter("ffn0", GATHER_FFN0, gathers[0]), waiter("rest", GATHER_REST, gathers[1]), token, scatter, seq)
    loss = lax.psum(loss_row[0, 0], ("x", "y", "c"))

    parts = {}
    for tag, names, started in scatters:
        parts.update(zip(names, _send_wait(started, True, grad_x, f"{tag}_wait")))
    repl_parts = _all_gather([_pack([g_repl[n] for n in REPL_ORDER], REPL_ROWS)], [0], "gather_small_grads")[0]

    out = {}
    for name in SHARDED_ORDER:
        w = weights[name]
        if name + '0' in parts:
            gp = jnp.stack([parts[name + '0'], parts[name + '1']], axis=1)
        else:
            gp = parts[name]
        to_rows = lambda a: a.reshape(-1, w.shape[-1])
        res = _adamw(to_rows(w), to_rows(given["m_" + name]), to_rows(given["v_" + name]),
                     gp.reshape(NDEV, -1, w.shape[-1]), f"adamw_{name}")
        out[name] = [r.reshape(w.shape) for r in res]
    rp = _adamw(_pack([weights[n] for n in REPL_ORDER], REPL_ROWS),
                _pack([given["m_" + n] for n in REPL_ORDER], REPL_ROWS),
                _pack([given["v_" + n] for n in REPL_ORDER], REPL_ROWS), repl_parts, "adamw_replicated")
    rp_shapes = [weights[n].shape for n in REPL_ORDER]
    for k in range(4):
        for name, a in zip(REPL_ORDER, _unpack(rp[k], rp_shapes)):
            out.setdefault(name, [None] * 4)[k] = a
    results = [[out[n][k] for n in WEIGHT_ORDER] for k in range(4)]
    grad_w, delta_w, new_m, new_v = results
    return (loss, grad_x.reshape(nseq, seq, D), *grad_w, *delta_w, *new_m, *new_v)
```

```python
import math

import jax
import jax.numpy as jnp
import numpy as np
from jax import lax
from jax.experimental import pallas as pl
from jax.experimental.pallas import tpu as pltpu

F32 = jnp.float32
BF = jnp.bfloat16

D = 1024
DFF = 2816
NDEV = 8
SSM_W = 256
SSM_G = 16
SSM_H = 16
SSM_P = 64
NST = SSM_G * SSM_P
GM_W = 768
GM_HEADS = 6
CHUNK = 128
EPS = 1e-6
LAM_MAX = -1e-4
FB = 256
FSH = 2 * DFF // NDEV
ROW_BLOCK = 512
CONV_ROW_BLOCK = 1024
VMEM_LIMIT = 48 * 2**20
PACK_COLS = 1024
MESH_T = pl.DeviceIdType.MESH

ADAM_LR = 0.001
ADAM_B1 = 0.9
ADAM_B2 = 0.999
ADAM_EPS = 1e-08
ADAM_WD = 0.01
ADAM_STEP = 10

WEIGHT_ORDER = ['mix_norm_g', 'ffn_norm_g', 'final_norm_g', 'ev_w_in', 'ev_w_out', 's5_lam_re', 's5_lam_im',
                's5_log_dt', 's5_b_re', 's5_b_im', 's5_c_re', 's5_c_im', 's5_d', 's5_w_glu', 's5_b_glu', 'gm_w_s',
                'gm_b_s', 'gm_v_g', 'od_w_in', 'od_conv_w', 'od_conv_b', 'od_w_out', 'ffn_w_up', 'ffn_conv_w',
                'ffn_conv_b', 'ffn_w_down']
SHARDED = {'ev_w_in': ((1, 1024, 1792), 2), 'ev_w_out': ((1, 1024, 1024), 1), 's5_w_glu': ((1, 256, 256), 1),
           'od_w_in': ((1, 1024, 3072), 2), 'od_conv_w': ((1, 3, 1024), 2), 'od_conv_b': ((1, 1024), 1),
           'od_w_out': ((1, 1024, 1024), 1), 'ffn_w_up': ((2, 1024, 5632), 2), 'ffn_conv_w': ((2, 3, 5632), 2),
           'ffn_w_down': ((2, 2816, 1024), 1)}
SHARDED_ORDER = [n for n in WEIGHT_ORDER if n in SHARDED]
REPL_ORDER = [n for n in WEIGHT_ORDER if n not in SHARDED]


def _cp(sem):
    return pltpu.CompilerParams(dimension_semantics=sem, vmem_limit_bytes=VMEM_LIMIT)


def _sigmoid(x):
    return 1.0 / (1.0 + jnp.exp(-x))


_GELU_K = math.sqrt(2.0 / math.pi)


def _gelu(x):
    return 0.5 * x * (1.0 + jnp.tanh(_GELU_K * (x + 0.044715 * x * x * x)))


def _gelu_grad(x):
    t = jnp.tanh(_GELU_K * (x + 0.044715 * x * x * x))
    return 0.5 * (1.0 + t) + 0.5 * x * (1.0 - t * t) * _GELU_K * (1.0 + 3.0 * 0.044715 * x * x)


def _colsum(x):
    return jnp.sum(x, axis=0, keepdims=True)


def _accumulate(ref, first, part):
    @pl.when(first)
    def _():
        ref[...] = part

    @pl.when(jnp.logical_not(first))
    def _():
        ref[...] += part


_DIMS = {'nn': (((1,), (0,)), ((), ())), 'nt': (((1,), (1,)), ((), ())), 'tn': (((0,), (0,)), ((), ()))}


def _rms(xv):
    r = lax.rsqrt(jnp.mean(xv * xv, axis=-1, keepdims=True) + EPS)
    return r, xv * r


def _norm_grad(dyv, gv, r, xh):
    dyg = dyv * gv
    return r * (dyg - xh * jnp.mean(dyg * xh, axis=-1, keepdims=True))


def _tail_io(tail, tm, index, n):
    rows = pl.BlockSpec((tm, D), index)
    vec = pl.BlockSpec((1, D), lambda *_: (0, 0))
    full, half, gain = (jax.ShapeDtypeStruct((n, D), F32), jax.ShapeDtypeStruct((n, D), BF),
                        jax.ShapeDtypeStruct((1, D), F32))
    if tail[0] == 'norm_fwd':
        return [tail[1]], [vec], [rows, rows], [full, half]
    if tail[0] == 'norm_bwd':
        return list(tail[1:]), [rows, vec, rows], [rows, rows, vec], [full, half, gain]
    return (list(tail[1:]), [vec, rows], [pl.BlockSpec((1, 128), lambda *_: (0, 0)), rows, rows, vec],
            [jax.ShapeDtypeStruct((1, 128), F32), full, half, gain])


def _tail_apply(kind, tot, tail_refs, outs, first):
    if kind == 'norm_fwd':
        r, xh = _rms(tot)
        outs[0][...] = tot
        outs[1][...] = (xh * tail_refs[0][...]).astype(BF)
    elif kind == 'norm_bwd':
        x_ref, g_ref, dr_ref = tail_refs
        r, xh = _rms(x_ref[...])
        dx = dr_ref[...] + _norm_grad(tot, g_ref[...], r, xh)
        outs[0][...] = dx
        outs[1][...] = dx.astype(BF)
        _accumulate(outs[2], first, _colsum(tot * xh))
    else:
        gv = tail_refs[0][...]
        r, xh = _rms(tot)
        err = xh * gv - tail_refs[1][...]
        part = 0.5 * jnp.sum(jnp.mean(err * err, axis=-1, keepdims=True), axis=0, keepdims=True)
        _accumulate(outs[0], first, jnp.broadcast_to(part, (1, 128)))
        dyv = err * (1.0 / D)
        dx = _norm_grad(dyv, gv, r, xh)
        outs[1][...] = dx
        outs[2][...] = dx.astype(BF)
        _accumulate(outs[3], first, _colsum(dyv * xh))


def _matmul(a, b, mode, tm, tn, tk, name, resid=None, out_dtype=F32, tail=None):
    if mode == 'tn':
        kdim, m = a.shape
    else:
        m, kdim = a.shape
    n = b.shape[0] if mode == 'nt' else b.shape[1]
    tm, tn, tk = min(tm, m), min(tn, n), min(tk, kdim)
    assert m % tm == 0 and n % tn == 0 and kdim % tk == 0, (name, m, n, kdim, tm, tn, tk)
    a_spec = (pl.BlockSpec((tk, tm), lambda i, j, k: (k, i)) if mode == 'tn'
              else pl.BlockSpec((tm, tk), lambda i, j, k: (i, k)))
    b_spec = (pl.BlockSpec((tn, tk), lambda i, j, k: (j, k)) if mode == 'nt'
              else pl.BlockSpec((tk, tn), lambda i, j, k: (k, j)))
    o_spec = pl.BlockSpec((tm, tn), lambda i, j, k: (i, j))
    return _matmul_spec(a, b, mode, (m // tm, n // tn, kdim // tk), a_spec, b_spec, o_spec, (m, n), name,
                        resid=resid, out_dtype=out_dtype, tail=tail)


def _matmul_spec(a, b, mode, grid, a_spec, b_spec, o_spec, out_shape, name, resid=None, out_dtype=F32, tail=None):
    nk = grid[2]
    tm, tn = o_spec.block_shape[-2:]
    dims = _DIMS[mode]
    has_resid = resid is not None
    operands = [a, b] + ([resid] if has_resid else [])
    in_specs = [a_spec, b_spec] + ([o_spec] if has_resid else [])
    out_specs, out_shapes = [o_spec], [jax.ShapeDtypeStruct(out_shape, out_dtype)]
    n_tail = 0
    if tail is not None:
        assert tn == D and grid[1] == 1, name
        extra, extra_specs, out_specs, out_shapes = _tail_io(tail, tm, lambda i, j, k: (i, 0), out_shape[0])
        operands, in_specs, n_tail = operands + extra, in_specs + extra_specs, len(extra)
    n_in, n_out = len(operands), len(out_specs)

    def body(*refs):
        ins, outs = refs[:n_in], refs[n_in:n_in + n_out]
        a_ref, b_ref = ins[:2]
        part = lax.dot_general(a_ref[...].astype(BF), b_ref[...].astype(BF), dims, preferred_element_type=F32)

        def finish(tot):
            if has_resid:
                tot = tot + ins[2][...]
            if tail is not None:
                _tail_apply(tail[0], tot, ins[n_in - n_tail:], outs, pl.program_id(0) == 0)
            else:
                outs[0][...] = tot.astype(out_dtype)

        if nk == 1:
            finish(part)
        else:
            acc = refs[-1]
            k = pl.program_id(2)

            @pl.when(k == 0)
            def _():
                acc[...] = part

            @pl.when(k > 0)
            def _():
                acc[...] += part

            @pl.when(k == nk - 1)
            def _():
                finish(acc[...])

    res = pl.pallas_call(
        body, name=name, grid=grid, in_specs=in_specs, out_specs=out_specs, out_shape=out_shapes,
        scratch_shapes=[pltpu.VMEM((tm, tn), F32)] if nk > 1 else [],
        compiler_params=_cp(("arbitrary",) * 3 if tail is not None else ("parallel", "parallel", "arbitrary")))(*operands)
    return res if tail is not None else res[0]


def _matmul_shards(a, b, mode, tm, tn, name, resid=None, out_dtype=F32, tail=None):
    shards, m, kdim = a.shape
    n = b.shape[2] if mode == 'nn' else b.shape[1]
    tm, tn = min(tm, m), min(tn, n)
    dims = _DIMS[mode]
    has_resid = resid is not None
    b_spec = (pl.BlockSpec((shards, kdim, tn), lambda i, j: (0, 0, j)) if mode == 'nn'
              else pl.BlockSpec((shards, tn, kdim), lambda i, j: (0, j, 0)))
    o_spec = pl.BlockSpec((tm, tn), lambda i, j: (i, j))
    operands = [a, b] + ([resid] if has_resid else [])
    in_specs = [pl.BlockSpec((shards, tm, kdim), lambda i, j: (0, i, 0)), b_spec] + ([o_spec] if has_resid else [])
    out_specs, out_shapes = [o_spec], [jax.ShapeDtypeStruct((m, n), out_dtype)]
    n_tail = 0
    if tail is not None:
        assert tn == D and n == D, name
        extra, extra_specs, out_specs, out_shapes = _tail_io(tail, tm, lambda i, j: (i, 0), m)
        operands, in_specs, n_tail = operands + extra, in_specs + extra_specs, len(extra)
    n_in, n_out = len(operands), len(out_specs)

    def body(*refs):
        ins, outs = refs[:n_in], refs[n_in:n_in + n_out]
        acc = lax.dot_general(ins[0][0], ins[1][0], dims, preferred_element_type=F32)
        for s in range(1, shards):
            acc = acc + lax.dot_general(ins[0][s], ins[1][s], dims, preferred_element_type=F32)
        if has_resid:
            acc = acc + ins[2][...]
        if tail is not None:
            _tail_apply(tail[0], acc, ins[n_in - n_tail:], outs, pl.program_id(0) == 0)
        else:
            outs[0][...] = acc.astype(out_dtype)

    res = pl.pallas_call(
        body, name=name, grid=(m // tm, n // tn), in_specs=in_specs, out_specs=out_specs, out_shape=out_shapes,
        compiler_params=_cp(("arbitrary", "arbitrary") if tail is not None else ("parallel", "parallel")))(*operands)
    return res if tail is not None else res[0]


def _rmsnorm_fwd(x, g, name):
    n = x.shape[0]
    tm = min(512, n)

    def body(x_ref, g_ref, o_ref):
        xv = x_ref[...]
        r = lax.rsqrt(jnp.mean(xv * xv, axis=-1, keepdims=True) + EPS)
        o_ref[...] = (xv * r * g_ref[...]).astype(BF)

    return pl.pallas_call(
        body, name=name, grid=(n // tm,),
        in_specs=[pl.BlockSpec((tm, D), lambda i: (i, 0)), pl.BlockSpec((1, D), lambda i: (0, 0))],
        out_specs=pl.BlockSpec((tm, D), lambda i: (i, 0)),
        out_shape=jax.ShapeDtypeStruct((n, D), BF), compiler_params=_cp(("parallel",)))(x, g)


def _prev_rows(x, halo_ref, lanes, scale, row):
    h7 = halo_ref[7:8, lanes] * scale
    h6 = halo_ref[6:7, lanes] * scale
    p1 = jnp.where(row == 0, h7, pltpu.roll(x, 1, 0))
    p2 = jnp.where(row == 0, h6, jnp.where(row == 1, h7, pltpu.roll(x, 2, 0)))
    return p1, p2


def _halo_maps(tm, n_rows):
    r8 = tm // 8
    last = n_rows // 8 - 1
    prev = lambda i: jnp.maximum(i * r8 - 1, 0)
    nxt = lambda i: jnp.minimum((i + 1) * r8, last)
    return prev, nxt


def _lane_blocks(width):
    return [slice(lo, min(lo + 128, width)) for lo in range(0, width, 128)]


def _conv_taps(w_ref, b_ref, g, lanes):
    return w_ref[g, 0:1, lanes], w_ref[g, 1:2, lanes], w_ref[g, 2:3, lanes], b_ref[g, :, lanes]


def _conv_tile(x, prev1, prev2, taps, row):
    w0, w1, w2, b = taps
    r1 = pltpu.roll(x, 1, 0)
    r2 = pltpu.roll(x, 2, 0)
    x1 = jnp.where(row == 0, prev1, r1)
    x2 = jnp.where(row < 2, prev2, r2)
    return b + w0 * x2 + w1 * x1 + w2 * x, x1, x2, r1, r2


def _halo16_maps(tm, n_rows):
    r16 = tm // 16
    last = n_rows // 16 - 1
    return (lambda i: jnp.maximum(i * r16 - 1, 0)), (lambda i: jnp.minimum((i + 1) * r16, last))


def _ffn_conv_fwd(up, cw, cb, seq, name):
    n = up.shape[2]
    tm = min(CONV_ROW_BLOCK, seq)
    prev, _ = _halo16_maps(tm, n)

    def body(u_ref, h_ref, w_ref, b_ref, o_ref, d_ref):
        i = pl.program_id(1)
        scale = jnp.where(lax.rem(i * tm, seq) == 0, 0.0, 1.0)
        for lanes in _lane_blocks(FSH):
            lw = lanes.stop - lanes.start
            row = lax.broadcasted_iota(jnp.int32, (8, lw), 0)
            taps = [_conv_taps(w_ref, b_ref, g, lanes) for g in range(2)]

            def tile(xs, carry):
                hc, nxt = [], []
                for g in range(2):
                    conv, _, _, r1, r2 = _conv_tile(xs[g], carry[2 * g], carry[2 * g + 1], taps[g], row)
                    hc.append(conv)
                    nxt += [r1, r2]
                s = _sigmoid(hc[0])
                silu = hc[0] * s
                return (silu * hc[1], hc[1] * (s * (1.0 + hc[0] * (1.0 - s))), silu), tuple(nxt)

            carry = []
            for g in range(2):
                halo = h_ref[g, :, lanes].astype(F32)[8:] * scale
                carry += [pltpu.roll(halo, 1, 0), pltpu.roll(halo, 2, 0)]
            carry = tuple(carry)
            for m in range(tm // 16):
                rows = slice(m * 16, m * 16 + 16)
                x16 = [u_ref[g, rows, lanes].astype(F32) for g in range(2)]
                a, carry = tile([x[:8] for x in x16], carry)
                b, carry = tile([x[8:] for x in x16], carry)
                o_ref[rows, lanes] = jnp.concatenate([a[0], b[0]], axis=0).astype(BF)
                d_ref[0, rows, lanes] = jnp.concatenate([a[1], b[1]], axis=0).astype(BF)
                d_ref[1, rows, lanes] = jnp.concatenate([a[2], b[2]], axis=0).astype(BF)

    return pl.pallas_call(
        body, name=name, grid=(4, n // tm),
        in_specs=[pl.BlockSpec((2, None, tm, FSH), lambda j, i: (0, j, i, 0)),
                  pl.BlockSpec((2, None, 16, FSH), lambda j, i: (0, j, prev(i), 0)),
                  pl.BlockSpec((2, None, 3, FSH), lambda j, i: (0, j, 0, 0)),
                  pl.BlockSpec((2, None, 1, FSH), lambda j, i: (0, j, 0, 0))],
        out_specs=[pl.BlockSpec((None, tm, FSH), lambda j, i: (j, i, 0)),
                   pl.BlockSpec((2, None, tm, FSH), lambda j, i: (0, j, i, 0))],
        out_shape=[jax.ShapeDtypeStruct((4, n, FSH), BF), jax.ShapeDtypeStruct((2, 4, n, FSH), BF)],
        compiler_params=_cp(("parallel", "parallel")))(up, up, cw, cb)


def _ffn_conv_bwd(up, dgate, dact, cw, seq, name):
    n = up.shape[2]
    tm = min(CONV_ROW_BLOCK, seq)
    _, nxt = _halo16_maps(tm, n)

    def body(u_ref, g_ref, gn_ref, da_ref, dn_ref, w_ref, du_ref, dw_ref, db_ref):
        i = pl.program_id(1)
        sn = jnp.where(lax.rem((i + 1) * tm, seq) == 0, 0.0, 1.0)
        first = i == 0
        for lanes in _lane_blocks(FSH):
            lw = lanes.stop - lanes.start
            row = lax.broadcasted_iota(jnp.int32, (8, lw), 0)
            taps = [(w_ref[g, 0:1, lanes], w_ref[g, 1:2, lanes], w_ref[g, 2:3, lanes]) for g in range(2)]

            def dconv(gs, da):
                ds = [gs[g] * da for g in range(2)]
                return [(d, pltpu.roll(d, 7, 0), pltpu.roll(d, 6, 0)) for d in ds]

            def finish(cur, after, xs, sums):
                dups, new_sums = [], []
                for g in range(2):
                    w0, w1, w2 = taps[g]
                    s1 = jnp.where(row == 7, after[g][1], cur[g][1])
                    s2 = jnp.where(row >= 6, after[g][2], cur[g][2])
                    dups.append(w2 * cur[g][0] + w1 * s1 + w0 * s2)
                    acc = sums[g]
                    new_sums.append((acc[0] + xs[g] * s2, acc[1] + xs[g] * s1, acc[2] + xs[g] * cur[g][0],
                                     acc[3] + cur[g][0]))
                return dups, new_sums

            def emit(m, held, after, sums):
                (ta, xa), (tb, xb) = held
                dup_a, sums = finish(ta, tb, xa, sums)
                dup_b, sums = finish(tb, after, xb, sums)
                for g in range(2):
                    du_ref[g, m * 16:m * 16 + 16, lanes] = jnp.concatenate([dup_a[g], dup_b[g]], axis=0).astype(BF)
                return sums

            zero = jnp.zeros((8, lw), F32)
            sums = [(zero,) * 4, (zero,) * 4]
            held = None
            for m in range(tm // 16):
                rows = slice(m * 16, m * 16 + 16)
                x16 = [u_ref[g, rows, lanes].astype(F32) for g in range(2)]
                g16 = [g_ref[g, rows, lanes].astype(F32) for g in range(2)]
                d16 = da_ref[rows, lanes].astype(F32)
                ta = dconv([a[:8] for a in g16], d16[:8])
                tb = dconv([a[8:] for a in g16], d16[8:])
                if held is not None:
                    sums = emit(m - 1, held, ta, sums)
                held = ((ta, [x[:8] for x in x16]), (tb, [x[8:] for x in x16]))
            tn_ = dconv([gn_ref[g, :, lanes].astype(F32)[:8] for g in range(2)], dn_ref[:, lanes].astype(F32)[:8] * sn)
            sums = emit(tm // 16 - 1, held, tn_, sums)
            for g in range(2):
                for k in range(3):
                    _accumulate(dw_ref.at[g, k:k + 1, lanes], first, _colsum(sums[g][k]))
                _accumulate(db_ref.at[g, :, lanes], first, _colsum(sums[g][3]))

    return pl.pallas_call(
        body, name=name, grid=(4, n // tm),
        in_specs=[pl.BlockSpec((2, None, tm, FSH), lambda j, i: (0, j, i, 0)),
                  pl.BlockSpec((2, None, tm, FSH), lambda j, i: (0, j, i, 0)),
                  pl.BlockSpec((2, None, 16, FSH), lambda j, i: (0, j, nxt(i), 0)),
                  pl.BlockSpec((None, tm, FSH), lambda j, i: (j, i, 0)),
                  pl.BlockSpec((None, 16, FSH), lambda j, i: (j, nxt(i), 0)),
                  pl.BlockSpec((2, None, 3, FSH), lambda j, i: (0, j, 0, 0))],
        out_specs=[pl.BlockSpec((2, None, tm, FSH), lambda j, i: (0, j, i, 0)),
                   pl.BlockSpec((2, None, 3, FSH), lambda j, i: (0, j, 0, 0)),
                   pl.BlockSpec((2, None, 1, FSH), lambda j, i: (0, j, 0, 0))],
        out_shape=[jax.ShapeDtypeStruct((2, 4, n, FSH), BF), jax.ShapeDtypeStruct((2, 4, 3, FSH), F32),
                   jax.ShapeDtypeStruct((2, 4, 1, FSH), F32)],
        compiler_params=_cp(("parallel", "arbitrary")))(up, dgate, dgate, dact, dact, cw)


def _shortconv_fwd(p, cw, cb, seq, name):
    n = p.shape[0]
    tm = min(CONV_ROW_BLOCK, seq)
    prev, _ = _halo_maps(tm, n)

    def body(p_ref, h_ref, w_ref, b_ref, o_ref):
        i = pl.program_id(1)
        scale = jnp.where(lax.rem(i * tm, seq) == 0, 0.0, 1.0)
        q = p_ref[:, FB:2 * FB] * p_ref[:, 2 * FB:]
        row = lax.broadcasted_iota(jnp.int32, q.shape, 0)
        h7 = h_ref[7:8, FB:2 * FB] * h_ref[7:8, 2 * FB:] * scale
        h6 = h_ref[6:7, FB:2 * FB] * h_ref[6:7, 2 * FB:] * scale
        p1 = jnp.where(row == 0, h7, pltpu.roll(q, 1, 0))
        p2 = jnp.where(row == 0, h6, jnp.where(row == 1, h7, pltpu.roll(q, 2, 0)))
        conv = b_ref[...] + w_ref[0:1, :] * p2 + w_ref[1:2, :] * p1 + w_ref[2:3, :] * q
        o_ref[...] = (p_ref[:, :FB] * conv).astype(BF)

    return pl.pallas_call(
        body, name=name, grid=(D // FB, n // tm),
        in_specs=[pl.BlockSpec((tm, 3 * FB), lambda j, i: (i, j)),
                  pl.BlockSpec((8, 3 * FB), lambda j, i: (prev(i), j)),
                  pl.BlockSpec((3, FB), lambda j, i: (0, j)),
                  pl.BlockSpec((1, FB), lambda j, i: (0, j))],
        out_specs=pl.BlockSpec((tm, FB), lambda j, i: (i, j)),
        out_shape=jax.ShapeDtypeStruct((n, D), BF), compiler_params=_cp(("parallel", "parallel")))(p, p, cw, cb)


def _shortconv_bwd(p, dmix, cw, cb, seq, name):
    n = p.shape[0]
    tm = min(CONV_ROW_BLOCK, seq)
    ext = tm + 16
    prev, nxt = _halo_maps(tm, n)

    def body(p_ref, pp_ref, pn_ref, dm_ref, dn_ref, w_ref, b_ref, dp_ref, dw_ref, db_ref, qx, cx):
        i = pl.program_id(1)
        sp = jnp.where(lax.rem(i * tm, seq) == 0, 0.0, 1.0)
        sn = jnp.where(lax.rem((i + 1) * tm, seq) == 0, 0.0, 1.0)
        bg, cg, hx = p_ref[:, :FB], p_ref[:, FB:2 * FB], p_ref[:, 2 * FB:]
        dm = dm_ref[...]
        qx[0:8, :] = pp_ref[:, FB:2 * FB] * pp_ref[:, 2 * FB:] * sp
        qx[8:8 + tm, :] = cg * hx
        qx[8 + tm:, :] = jnp.zeros((8, FB), F32)
        cx[0:8, :] = jnp.zeros((8, FB), F32)
        cx[8:8 + tm, :] = dm * bg
        cx[8 + tm:, :] = dn_ref[...] * pn_ref[:, :FB] * sn
        q0 = qx[...]
        q1 = pltpu.roll(q0, 1, 0)
        q2 = pltpu.roll(q0, 2, 0)
        main = slice(8, 8 + tm)
        conv = b_ref[...] + w_ref[0:1, :] * q2[main] + w_ref[1:2, :] * q1[main] + w_ref[2:3, :] * q0[main]
        dc = cx[...]
        dq = (w_ref[2:3, :] * dc + w_ref[1:2, :] * pltpu.roll(dc, ext - 1, 0)
              + w_ref[0:1, :] * pltpu.roll(dc, ext - 2, 0))[main]
        dp_ref[:, :FB] = (dm * conv).astype(BF)
        dp_ref[:, FB:2 * FB] = (dq * hx).astype(BF)
        dp_ref[:, 2 * FB:] = (dq * cg).astype(BF)
        first = i == 0
        dcm = dc[main]
        _accumulate(dw_ref.at[0:1, :], first, _colsum(dcm * q2[main]))
        _accumulate(dw_ref.at[1:2, :], first, _colsum(dcm * q1[main]))
        _accumulate(dw_ref.at[2:3, :], first, _colsum(dcm * q0[main]))
        _accumulate(db_ref, first, _colsum(dcm))

    return pl.pallas_call(
        body, name=name, grid=(D // FB, n // tm),
        in_specs=[pl.BlockSpec((tm, 3 * FB), lambda j, i: (i, j)),
                  pl.BlockSpec((8, 3 * FB), lambda j, i: (prev(i), j)),
                  pl.BlockSpec((8, 3 * FB), lambda j, i: (nxt(i), j)),
                  pl.BlockSpec((tm, FB), lambda j, i: (i, j)),
                  pl.BlockSpec((8, FB), lambda j, i: (nxt(i), j)),
                  pl.BlockSpec((3, FB), lambda j, i: (0, j)),
                  pl.BlockSpec((1, FB), lambda j, i: (0, j))],
        out_specs=[pl.BlockSpec((tm, 3 * FB), lambda j, i: (i, j)),
                   pl.BlockSpec((3, FB), lambda j, i: (0, j)),
                   pl.BlockSpec((1, FB), lambda j, i: (0, j))],
        out_shape=[jax.ShapeDtypeStruct((n, 3 * D), BF), jax.ShapeDtypeStruct((3, D), F32),
                   jax.ShapeDtypeStruct((1, D), F32)],
        scratch_shapes=[pltpu.VMEM((ext, FB), F32), pltpu.VMEM((ext, FB), F32)],
        compiler_params=_cp(("parallel", "arbitrary")))(p, p, p, dmix, dmix, cw, cb)


def _gmlp_fwd(uv, wm, bst, gv, seq, name):
    n = uv.shape[0]
    tm = min(ROW_BLOCK, seq)

    def body(x_ref, w_ref, b_ref, g_ref, o_ref):
        ge_v = _gelu(x_ref[:, GM_W:])
        r = lax.rsqrt(jnp.mean(ge_v * ge_v, axis=-1, keepdims=True) + EPS)
        vn = (ge_v * r * g_ref[...]).astype(BF)
        for c in range(tm // CHUNK):
            rows = slice(c * CHUNK, (c + 1) * CHUNK)
            for h in range(GM_HEADS):
                cols = slice(h * CHUNK, (h + 1) * CHUNK)
                gate = jnp.dot(w_ref[h], vn[rows, cols], preferred_element_type=F32) + b_ref[:, h:h + 1]
                o_ref[rows, cols] = (_gelu(x_ref[rows, cols]) * gate).astype(BF)

    return pl.pallas_call(
        body, name=name, grid=(n // tm,),
        in_specs=[pl.BlockSpec((tm, 2 * GM_W), lambda i: (i, 0)),
                  pl.BlockSpec((GM_HEADS, CHUNK, CHUNK), lambda i: (0, 0, 0)),
                  pl.BlockSpec((CHUNK, GM_HEADS), lambda i: (0, 0)),
                  pl.BlockSpec((1, GM_W), lambda i: (0, 0))],
        out_specs=pl.BlockSpec((tm, GM_W), lambda i: (i, 0)),
        out_shape=jax.ShapeDtypeStruct((n, GM_W), BF), compiler_params=_cp(("parallel",)))(uv, wm, bst, gv)


def _gmlp_bwd(uv, dout, wm, wmt, bst, gv, seq, name):
    n = uv.shape[0]
    tm = min(ROW_BLOCK, seq)

    def body(x_ref, do_ref, w_ref, wt_ref, b_ref, g_ref, dx_ref, dw_ref, db_ref, dg_ref, dvn_scr):
        first = pl.program_id(0) == 0
        ge_v = _gelu(x_ref[:, GM_W:])
        r = lax.rsqrt(jnp.mean(ge_v * ge_v, axis=-1, keepdims=True) + EPS)
        vh = ge_v * r
        vn = (vh * g_ref[...]).astype(BF)
        tril = (lax.broadcasted_iota(jnp.int32, (CHUNK, CHUNK), 0)
                >= lax.broadcasted_iota(jnp.int32, (CHUNK, CHUNK), 1))
        for h in range(GM_HEADS):
            cols = slice(h * CHUNK, (h + 1) * CHUNK)
            dw = jnp.zeros((CHUNK, CHUNK), F32)
            dbs = jnp.zeros((CHUNK, 1), F32)
            for c in range(tm // CHUNK):
                rows = slice(c * CHUNK, (c + 1) * CHUNK)
                blk = vn[rows, cols]
                gate = jnp.dot(w_ref[h], blk, preferred_element_type=F32) + b_ref[:, h:h + 1]
                xu = x_ref[rows, cols]
                do = do_ref[rows, cols]
                dx_ref[rows, cols] = (do * gate * _gelu_grad(xu)).astype(BF)
                dgate = do * _gelu(xu)
                dgb = dgate.astype(BF)
                dw = dw + lax.dot_general(dgb, blk, _DIMS['nt'], preferred_element_type=F32)
                dbs = dbs + jnp.sum(dgate, axis=1, keepdims=True)
                dvn_scr[rows, cols] = jnp.dot(wt_ref[h], dgb, preferred_element_type=F32)
            _accumulate(dw_ref.at[h], first, jnp.where(tril, dw, 0.0))
            _accumulate(db_ref.at[h], first, dbs)
        dvn = dvn_scr[...]
        _accumulate(dg_ref, first, _colsum(dvn * vh))
        dvh = dvn * g_ref[...]
        dv = r * (dvh - vh * jnp.mean(dvh * vh, axis=-1, keepdims=True))
        dx_ref[:, GM_W:] = (dv * _gelu_grad(x_ref[:, GM_W:])).astype(BF)

    full3 = pl.BlockSpec((GM_HEADS, CHUNK, CHUNK), lambda i: (0, 0, 0))
    return pl.pallas_call(
        body, name=name, grid=(n // tm,),
        in_specs=[pl.BlockSpec((tm, 2 * GM_W), lambda i: (i, 0)), pl.BlockSpec((tm, GM_W), lambda i: (i, 0)),
                  full3, full3, pl.BlockSpec((CHUNK, GM_HEADS), lambda i: (0, 0)),
                  pl.BlockSpec((1, GM_W), lambda i: (0, 0))],
        out_specs=[pl.BlockSpec((tm, 2 * GM_W), lambda i: (i, 0)), full3,
                   pl.BlockSpec((GM_HEADS, CHUNK, 1), lambda i: (0, 0, 0)),
                   pl.BlockSpec((1, GM_W), lambda i: (0, 0))],
        out_shape=[jax.ShapeDtypeStruct((n, 2 * GM_W), BF), jax.ShapeDtypeStruct((GM_HEADS, CHUNK, CHUNK), F32),
                   jax.ShapeDtypeStruct((GM_HEADS, CHUNK, 1), F32), jax.ShapeDtypeStruct((1, GM_W), F32)],
        scratch_shapes=[pltpu.VMEM((tm, GM_W), F32)],
        compiler_params=_cp(("arbitrary",)))(uv, dout, wm, wmt, bst, gv)


def _s5_disc(lam_re, lam_im, log_dt, b_re, b_im):
    lr = jnp.minimum(lam_re, LAM_MAX)
    li = lam_im
    dt = jnp.exp(log_dt)
    mag = jnp.exp(lr * dt)
    ab_re = mag * jnp.cos(li * dt)
    ab_im = mag * jnp.sin(li * dt)
    den = lr * lr + li * li
    nr = ab_re - 1.0
    ni = ab_im
    z_re = (nr * lr + ni * li) / den
    z_im = (ni * lr - nr * li) / den
    return ab_re, ab_im, z_re * b_re - z_im * b_im, z_re * b_im + z_im * b_re


def _s5_disc_fwd(args, name):
    shp = jax.ShapeDtypeStruct(args[0].shape, F32)

    def body(*refs):
        outs = _s5_disc(*[r[...] for r in refs[:5]])
        for o_ref, o in zip(refs[5:], outs):
            o_ref[...] = o

    return pl.pallas_call(body, name=name, out_shape=[shp] * 4)(*args)


def _s5_disc_bwd(args, cts, name):
    shp = jax.ShapeDtypeStruct(args[0].shape, F32)

    def body(*refs):
        _, vjp = jax.vjp(_s5_disc, *[r[...] for r in refs[:5]])
        grads = vjp(tuple(r[...] for r in refs[5:9]))
        for o_ref, o in zip(refs[9:], grads):
            o_ref[...] = o

    return pl.pallas_call(body, name=name, out_shape=[shp] * 5)(*args, *cts)


def _cmul(a, b):
    return a[0] * b[0] - a[1] * b[1], a[0] * b[1] + a[1] * b[0]


def _scan_tables(ar, ai, reverse):
    if reverse:
        ai = -ai
    a1 = (ar, ai)
    a2 = _cmul(a1, a1)
    a3 = _cmul(a2, a1)
    a4 = _cmul(a2, a2)
    powers = [a1, a2, a3, a4, _cmul(a4, a1), _cmul(a4, a2), _cmul(a4, a3), _cmul(a4, a4)]
    row = lax.broadcasted_iota(jnp.int32, (8, NST), 0)
    zero = jnp.zeros((8, NST), F32)
    pr, pi = zero, zero
    for r in range(8):
        pw = powers[7 - r] if reverse else powers[r]
        pr = jnp.where(row == r, pw[0], pr)
        pi = jnp.where(row == r, pw[1], pi)
    levels = []
    for d, pw in ((1, a1), (2, a2), (4, a4)):
        ok = (row <= 7 - d) if reverse else (row >= d)
        levels.append((d, jnp.where(ok, pw[0], zero), jnp.where(ok, pw[1], zero)))
    return (pr, pi), levels


def _scan_block(src, dst, car, tables, n_tiles, reverse):
    (pr, pi), levels = tables
    row = lax.broadcasted_iota(jnp.int32, (8, NST), 0)
    out_row = 0 if reverse else 7

    def step(t, carry):
        cr, ci = carry
        tile = (n_tiles - 1 - t) if reverse else t
        rows = pl.ds(pl.multiple_of(tile * 8, 8), 8)
        xr = src[rows, 0:NST]
        xi = src[rows, NST:2 * NST]
        for d, dr, di in levels:
            shift = 8 - d if reverse else d
            rr = pltpu.roll(xr, shift, 0)
            ri = pltpu.roll(xi, shift, 0)
            xr, xi = xr + dr * rr - di * ri, xi + dr * ri + di * rr
        hr = xr + pr * cr - pi * ci
        hi = xi + pr * ci + pi * cr
        dst[rows, 0:NST] = hr
        dst[rows, NST:2 * NST] = hi
        return (_colsum(jnp.where(row == out_row, hr, 0.0)), _colsum(jnp.where(row == out_row, hi, 0.0)))

    cr, ci = lax.fori_loop(0, n_tiles, step, (car[0:1, 0:NST], car[0:1, NST:2 * NST]))
    car[0:1, 0:NST] = cr
    car[0:1, NST:2 * NST] = ci


def _s5_fwd(u, ab, bbt, cmat, dvec, wglu, bglu, seq, name):
    n = u.shape[0]
    tm = min(ROW_BLOCK, seq)

    def body(u_ref, ab_ref, bb_ref, c_ref, d_ref, w_ref, b_ref, h_ref, o_ref, xs, car):
        i = pl.program_id(0)

        @pl.when(lax.rem(i * tm, seq) == 0)
        def _():
            car[...] = jnp.zeros(car.shape, F32)

        uv = u_ref[...]
        xs[...] = jnp.dot(uv.astype(BF), bb_ref[...], preferred_element_type=F32)
        tables = _scan_tables(ab_ref[0:1, 0:NST], ab_ref[0:1, NST:2 * NST], False)
        _scan_block(xs, h_ref, car, tables, tm // 8, False)
        y = jnp.dot(h_ref[...].astype(BF), c_ref[...], preferred_element_type=F32) + d_ref[...] * uv
        g1 = _gelu(y)
        z = jnp.dot(g1.astype(BF), w_ref[...], preferred_element_type=F32) + b_ref[...]
        o_ref[...] = (g1 * _sigmoid(z)).astype(BF)

    const = lambda shape: pl.BlockSpec(shape, lambda i: (0, 0))
    return pl.pallas_call(
        body, name=name, grid=(n // tm,),
        in_specs=[pl.BlockSpec((tm, SSM_W), lambda i: (i, 0)), const((1, 2 * NST)), const((SSM_W, 2 * NST)),
                  const((2 * NST, SSM_W)), const((1, SSM_W)), const((SSM_W, SSM_W)), const((1, SSM_W))],
        out_specs=[pl.BlockSpec((tm, 2 * NST), lambda i: (i, 0)), pl.BlockSpec((tm, SSM_W), lambda i: (i, 0))],
        out_shape=[jax.ShapeDtypeStruct((n, 2 * NST), F32), jax.ShapeDtypeStruct((n, SSM_W), BF)],
        scratch_shapes=[pltpu.VMEM((tm, 2 * NST), F32), pltpu.VMEM((8, 2 * NST), F32)],
        compiler_params=_cp(("arbitrary",)))(u, ab, bbt, cmat, dvec, wglu, bglu)


def _s5_bwd(da, u, hst, ab, bbt, cmat, dvec, wglu, bglu, seq, name):
    n = u.shape[0]
    tm = min(ROW_BLOCK, seq)
    nb = n // tm
    blk = lambda r: nb - 1 - r
    prev, _ = _halo_maps(tm, n)

    def body(da_ref, u_ref, h_ref, hp_ref, ab_ref, bb_ref, c_ref, d_ref, w_ref, b_ref,
             du_ref, dw_ref, dbg_ref, dd_ref, dc_ref, dbb_ref, dab_ref, gs, car):
        r = pl.program_id(0)
        i = blk(r)
        first = r == 0

        @pl.when(lax.rem((i + 1) * tm, seq) == 0)
        def _():
            car[...] = jnp.zeros(car.shape, F32)

        uv = u_ref[...]
        dav = da_ref[...]
        hb = h_ref[...]
        hb16 = hb.astype(BF)
        dvv = d_ref[...]
        y = jnp.dot(hb16, c_ref[...], preferred_element_type=F32) + dvv * uv
        g1 = _gelu(y)
        g16 = g1.astype(BF)
        s = _sigmoid(jnp.dot(g16, w_ref[...], preferred_element_type=F32) + b_ref[...])
        dz = dav * g1 * s * (1.0 - s)
        dz16 = dz.astype(BF)
        dg1 = dav * s + lax.dot_general(dz16, w_ref[...], _DIMS['nt'], preferred_element_type=F32)
        _accumulate(dw_ref, first, lax.dot_general(g16, dz16, _DIMS['tn'], preferred_element_type=F32))
        _accumulate(dbg_ref, first, _colsum(dz))
        dy = dg1 * _gelu_grad(y)
        dy16 = dy.astype(BF)
        _accumulate(dd_ref, first, _colsum(dy * uv))
        _accumulate(dc_ref, first, lax.dot_general(hb16, dy16, _DIMS['tn'], preferred_element_type=F32))
        gs[...] = lax.dot_general(dy16, c_ref[...], _DIMS['nt'], preferred_element_type=F32)
        tables = _scan_tables(ab_ref[0:1, 0:NST], ab_ref[0:1, NST:2 * NST], True)
        _scan_block(gs, gs, car, tables, tm // 8, True)
        g = gs[...]
        g16b = g.astype(BF)
        sp = jnp.where(lax.rem(i * tm, seq) == 0, 0.0, 1.0)
        row = lax.broadcasted_iota(jnp.int32, hb.shape, 0)
        hprev = jnp.where(row == 0, hp_ref[7:8, :] * sp, pltpu.roll(hb, 1, 0))
        gr, gi = g[:, :NST], g[:, NST:]
        hr, hi = hprev[:, :NST], hprev[:, NST:]
        _accumulate(dab_ref.at[:, 0:NST], first, _colsum(gr * hr + gi * hi))
        _accumulate(dab_ref.at[:, NST:2 * NST], first, _colsum(gi * hr - gr * hi))
        _accumulate(dbb_ref, first, lax.dot_general(uv.astype(BF), g16b, _DIMS['tn'], preferred_element_type=F32))
        du = dy * dvv + lax.dot_general(g16b, bb_ref[...], _DIMS['nt'], preferred_element_type=F32)
        du_ref[...] = du.astype(BF)

    const = lambda shape: pl.BlockSpec(shape, lambda r: (0, 0))
    rowspec = lambda w: pl.BlockSpec((tm, w), lambda r: (blk(r), 0))
    return pl.pallas_call(
        body, name=name, grid=(nb,),
        in_specs=[rowspec(SSM_W), rowspec(SSM_W), rowspec(2 * NST),
                  pl.BlockSpec((8, 2 * NST), lambda r: (prev(blk(r)), 0)),
                  const((1, 2 * NST)), const((SSM_W, 2 * NST)), const((2 * NST, SSM_W)), const((1, SSM_W)),
                  const((SSM_W, SSM_W)), const((1, SSM_W))],
        out_specs=[rowspec(SSM_W), const((SSM_W, SSM_W)), const((1, SSM_W)), const((1, SSM_W)),
                   const((2 * NST, SSM_W)), const((SSM_W, 2 * NST)), const((1, 2 * NST))],
        out_shape=[jax.ShapeDtypeStruct((n, SSM_W), BF), jax.ShapeDtypeStruct((SSM_W, SSM_W), F32),
                   jax.ShapeDtypeStruct((1, SSM_W), F32), jax.ShapeDtypeStruct((1, SSM_W), F32),
                   jax.ShapeDtypeStruct((2 * NST, SSM_W), F32), jax.ShapeDtypeStruct((SSM_W, 2 * NST), F32),
                   jax.ShapeDtypeStruct((1, 2 * NST), F32)],
        scratch_shapes=[pltpu.VMEM((tm, 2 * NST), F32), pltpu.VMEM((8, 2 * NST), F32)],
        compiler_params=_cp(("arbitrary",)))(da, u, hst, hst, ab, bbt, cmat, dvec, wglu, bglu)


def _s5_rows(lam_re, lam_im, log_dt, b_re, b_im):
    rep = lambda a: jnp.broadcast_to(a[:, None, :], (SSM_G, SSM_H, SSM_P)).reshape(SSM_W, SSM_P)
    dt = jnp.broadcast_to(log_dt[:, None, None], (SSM_G, SSM_H, SSM_P)).reshape(SSM_W, SSM_P)
    tr = lambda b: b.transpose(0, 2, 1).reshape(SSM_W, SSM_P)
    return rep(lam_re), rep(lam_im), dt, tr(b_re), tr(b_im)


def _block_diag(rows_gp, inner):
    eye = jnp.eye(SSM_G, dtype=rows_gp.dtype)
    return (rows_gp[:, :, None, :] * eye[:, None, :, None]).reshape(SSM_G * inner, SSM_G * SSM_P)


def _diag_blocks(mat, inner):
    m4 = mat.reshape(SSM_G, inner, SSM_G, SSM_P)
    return jnp.stack([m4[g, :, g, :] for g in range(SSM_G)])


def _interleave(w, parts):
    lead = w.shape[:-1]
    nb = w.shape[-1] // (parts * FB)
    return jnp.swapaxes(w.reshape(lead + (parts, nb, FB)), -3, -2).reshape(w.shape)


def _deinterleave(w, parts):
    lead = w.shape[:-1]
    nb = w.shape[-1] // (parts * FB)
    return jnp.swapaxes(w.reshape(lead + (nb, parts, FB)), -3, -2).reshape(w.shape)


def _ffn_fwd(h, f, w_up, w_down, cw, cb, seq, tag, tail):
    n = h.shape[0]
    tm = min(2048, n)
    ni = n // tm
    up = _matmul_spec(
        f, w_up, 'nn', (NDEV, ni, 1),
        pl.BlockSpec((tm, D), lambda s, i, k: (i, 0)),
        pl.BlockSpec((D, FSH), lambda s, i, k: (s, 0)),
        pl.BlockSpec((tm, FSH), lambda s, i, k: (s * ni + i, 0)), (NDEV * n, FSH), f"{tag}_up", out_dtype=BF)
    up = up.reshape(2, 4, n, FSH)
    act, dgate = _ffn_conv_fwd(up, cw, cb, seq, f"{tag}_conv")
    out = _matmul_shards(act, w_down.reshape(4, FSH, D), 'nn', 512, D, f"{tag}_down", resid=h, tail=tail)
    return out, (f, up, act, dgate)


def _ffn_bwd(dh, dhb, h, g, w_up, w_down, cw, cb, saved, seq, tag):
    f, up, act, dgate = saved
    n = h.shape[0]
    tm = min(2048, n)
    ni = n // tm
    tk = min(4096, n)
    nk = n // tk
    dact = _matmul_spec(
        dhb, w_down, 'nt', (4, ni, 1),
        pl.BlockSpec((tm, D), lambda j, i, k: (i, 0)),
        pl.BlockSpec((FSH, D), lambda j, i, k: (j, 0)),
        pl.BlockSpec((tm, FSH), lambda j, i, k: (j * ni + i, 0)), (4 * n, FSH), f"{tag}_ddown_x", out_dtype=BF)
    tn = 512
    dw_down = _matmul_spec(
        act.reshape(4 * n, FSH), dhb, 'tn', (4, D // tn, nk),
        pl.BlockSpec((tk, FSH), lambda j, c, k: (j * nk + k, 0)),
        pl.BlockSpec((tk, tn), lambda j, c, k: (k, c)),
        pl.BlockSpec((FSH, tn), lambda j, c, k: (j, c)), (DFF, D), f"{tag}_ddown_w", out_dtype=BF)
    dup, dcw, dcb = _ffn_conv_bwd(up, dgate, dact.reshape(4, n, FSH), cw, seq, f"{tag}_dconv")
    dup2 = dup.reshape(NDEV * n, FSH)
    dh_in, dhb_in, dg = _matmul_shards(dup.reshape(NDEV, n, FSH), w_up.reshape(NDEV, D, FSH), 'nt', 256, D,
                                       f"{tag}_dup_x", tail=('norm_bwd', h, g, dh))
    dw_up = _matmul_spec(
        f, dup2, 'tn', (NDEV, 1, nk),
        pl.BlockSpec((tk, D), lambda s, j, k: (k, 0)),
        pl.BlockSpec((tk, FSH), lambda s, j, k: (s * nk + k, 0)),
        pl.BlockSpec((D, FSH), lambda s, j, k: (s, 0)), (NDEV * D, FSH), f"{tag}_dup_w", out_dtype=BF)
    grads = dict(g=dg, w_up=dw_up.reshape(NDEV, D, FSH), w_down=dw_down.reshape(NDEV, DFF // NDEV, D),
                 cw=dcw.reshape(NDEV, 3, FSH), cb=dcb.reshape(2 * DFF))
    return dh_in, dhb_in, grads


def _col_shards(w, width):
    return w.reshape(w.shape[0], NDEV, width).transpose(1, 0, 2)


def _local_step(x, tgt, w, gw, wait_ffn0, wait_rest, token, scatter, seq):
    bf = lambda a: a.astype(BF)
    row = lambda a: a.reshape(1, -1).astype(F32)
    w_ev = gw['ev_w_in'].transpose(1, 0, 2).reshape(D, 1792)
    w_ev_s5, w_ev_gm = w_ev[:, :SSM_W], w_ev[:, SSM_W:]
    w_evo = gw['ev_w_out'].reshape(D, D)
    f_cb = [w['ffn_conv_b'][l].reshape(2, 4, 1, FSH) for l in range(2)]
    tril = jnp.tril(jnp.ones((CHUNK, CHUNK), dtype=bool))
    gm_w = jnp.where(tril, w['gm_w_s'][0], 0.0)
    gm_wm, gm_wmt = bf(gm_w), bf(jnp.swapaxes(gm_w, 1, 2))
    gm_bt = w['gm_b_s'][0].T
    gm_gv = row(w['gm_v_g'][0])
    s5_in = _s5_rows(w['s5_lam_re'][0], w['s5_lam_im'][0], w['s5_log_dt'][0], w['s5_b_re'][0], w['s5_b_im'][0])
    ab_re, ab_im, bb_re, bb_im = _s5_disc_fwd(s5_in, "s5_disc")
    first_h = lambda a: a.reshape(SSM_G, SSM_H, SSM_P)[:, 0, :].reshape(1, NST)
    s5_ab = jnp.concatenate([first_h(ab_re), first_h(ab_im)], axis=1)
    to_gp = lambda a: a.reshape(SSM_G, SSM_H, SSM_P)
    s5_bbt = bf(jnp.concatenate([_block_diag(to_gp(bb_re), SSM_H), _block_diag(to_gp(bb_im), SSM_H)], axis=1))
    s5_cmat = bf(jnp.concatenate([_block_diag(w['s5_c_re'][0], SSM_H).T, -_block_diag(w['s5_c_im'][0], SSM_H).T],
                                 axis=0))
    s5_d, s5_bg, s5_wg = row(w['s5_d'][0]), row(w['s5_b_glu'][0]), gw['s5_w_glu'].reshape(SSM_W, SSM_W)
    g_mix = [row(w['mix_norm_g'][0]) + token[0:1, 0:1], row(w['mix_norm_g'][1])]
    g_ffn = [row(w['ffn_norm_g'][l]) for l in range(2)]
    g_fin = row(w['final_norm_g'])

    h0 = x
    y0 = _rmsnorm_fwd(h0, g_mix[0], "ev_norm")
    p_s5 = _matmul(y0, w_ev_s5, 'nn', 1024, 256, D, "ev_in_s5")
    p_gm = _matmul(y0, w_ev_gm, 'nn', 1024, 2 * GM_W, D, "ev_in_gm")
    hst, a_out = _s5_fwd(p_s5, s5_ab, s5_bbt, s5_cmat, s5_d, s5_wg, s5_bg, seq, "s5_fwd")
    b_out = _gmlp_fwd(p_gm, gm_wm, gm_bt, gm_gv, seq, "gmlp_fwd")
    mixcat = jnp.concatenate([a_out, b_out], axis=1)
    h1, f0 = _matmul(mixcat, w_evo, 'nn', 1024, D, D, "ev_out", resid=h0, tail=('norm_fwd', g_ffn[0]))
    g0 = wait_ffn0(mixcat)
    w_up0, w_dn0 = g0['ffn_w_up0'].reshape(NDEV * D, FSH), g0['ffn_w_down0'].reshape(DFF, D)
    f_cw0 = g0['ffn_conv_w0'].reshape(2, 4, 3, FSH)
    (h2, y1), ffn0 = _ffn_fwd(h1, f0, w_up0, w_dn0, f_cw0, f_cb[0], seq, "ffn0", ('norm_fwd', g_mix[1]))
    g1 = wait_rest(h2)
    w_od = _interleave(g1['od_w_in'].transpose(1, 0, 2).reshape(D, 3 * D), 3)
    w_odo = g1['od_w_out'].reshape(D, D)
    od_cw = g1['od_conv_w'].transpose(1, 0, 2).reshape(3, D)
    od_cb = g1['od_conv_b'].reshape(1, D)
    w_up1, w_dn1 = g1['ffn_w_up1'].reshape(NDEV * D, FSH), g1['ffn_w_down1'].reshape(DFF, D)
    f_cw1 = g1['ffn_conv_w1'].reshape(2, 4, 3, FSH)
    p_od = _matmul(y1, w_od, 'nn', 1024, 3 * D // 2, D, "od_in")
    mixin = _shortconv_fwd(p_od, od_cw, od_cb, seq, "od_conv")
    h3, f1 = _matmul(mixin, w_odo, 'nn', 1024, D, D, "od_out", resid=h2, tail=('norm_fwd', g_ffn[1]))
    (loss, dh4, dh4b, dg_fin), ffn1 = _ffn_fwd(h3, f1, w_up1, w_dn1, f_cw1, f_cb[1], seq, "ffn1",
                                                ('loss', g_fin, tgt))

    dh3, dh3b, gf1 = _ffn_bwd(dh4, dh4b, h3, g_ffn[1], w_up1, w_dn1, f_cw1, f_cb[1], ffn1, seq, "ffn1")
    dmixin = _matmul(dh3b, w_odo, 'nt', 1024, D, D, "od_dout_x")
    dw_odo = _matmul(mixin, dh3b, 'tn', D, 512, 4096, "od_dout_w", out_dtype=BF)
    dp_od, d_od_cw, d_od_cb = _shortconv_bwd(p_od, dmixin, od_cw, od_cb, seq, "od_dconv")
    dw_od = _matmul(y1, dp_od, 'tn', D, 512, 4096, "od_din_w", out_dtype=BF)
    sent = scatter("scatter_layer1", {
        'od_w_in': _col_shards(_deinterleave(dw_od, 3), 384), 'od_conv_w': _col_shards(d_od_cw, D // NDEV),
        'od_conv_b': d_od_cb.reshape(NDEV, 1, D // NDEV), 'od_w_out': dw_odo.reshape(NDEV, D // NDEV, D),
        'ffn_w_up1': gf1['w_up'], 'ffn_conv_w1': gf1['cw'], 'ffn_w_down1': gf1['w_down']})
    dh2, dh2b, dg_mix1 = _matmul(dp_od, w_od, 'nt', 512, D, 3 * D, "od_din_x",
                                 tail=('norm_bwd', h2, g_mix[1] + sent[0:1, 0:1], dh3))
    dh1, dh1b, gf0 = _ffn_bwd(dh2, dh2b, h1, g_ffn[0], w_up0, w_dn0, f_cw0, f_cb[0], ffn0, seq, "ffn0")
    dmix_a = _matmul(dh1b, w_evo[:SSM_W], 'nt', 1024, SSM_W, D, "ev_dout_xa")
    dmix_b = _matmul(dh1b, w_evo[SSM_W:], 'nt', 1024, GM_W, D, "ev_dout_xb")
    dw_evo = _matmul(mixcat, dh1b, 'tn', D, 512, 4096, "ev_dout_w", out_dtype=BF)
    sent = scatter("scatter_ffn0", {'ffn_w_up0': gf0['w_up'], 'ffn_conv_w0': gf0['cw'], 'ffn_w_down0': gf0['w_down'],
                                    'ev_w_out': dw_evo.reshape(NDEV, D // NDEV, D)})
    dp_s5, d_wg, d_bg, d_d, d_cmat, d_bbt, d_ab = _s5_bwd(dmix_a, p_s5, hst, s5_ab, s5_bbt, s5_cmat,
                                                           s5_d + sent[0:1, 0:1], s5_wg, s5_bg, seq, "s5_bwd")
    dp_gm, d_gmw, d_gmb, d_gmg = _gmlp_bwd(p_gm, dmix_b, gm_wm, gm_wmt, gm_bt, gm_gv, seq, "gmlp_bwd")
    dw_ev = jnp.concatenate([_matmul(y0, dp_s5, 'tn', D, SSM_W, 4096, "ev_din_wa", out_dtype=BF),
                             _matmul(y0, dp_gm, 'tn', D, 512, 4096, "ev_din_wb", out_dtype=BF)], axis=1)
    sent = scatter("scatter_even", {'ev_w_in': _col_shards(dw_ev, 224),
                                    's5_w_glu': d_wg.reshape(NDEV, SSM_W // NDEV, SSM_W)})
    dy0 = _matmul(dp_gm, w_ev_gm, 'nt', 512, D, 2 * GM_W, "ev_din_xb")
    grad_x, _, dg_mix0 = _matmul(dp_s5, w_ev_s5, 'nt', 512, D, SSM_W, "ev_din_xa", resid=dy0,
                                 tail=('norm_bwd', h0, g_mix[0] + sent[0:1, 0:1], dh1))

    put_h0 = lambda a: jnp.zeros((SSM_G, SSM_H, SSM_P), F32).at[:, 0, :].set(a.reshape(SSM_G, SSM_P)).reshape(
        SSM_W, SSM_P)
    ct = (put_h0(d_ab[:, :NST]), put_h0(d_ab[:, NST:]),
          _diag_blocks(d_bbt[:, :NST], SSM_H).reshape(SSM_W, SSM_P),
          _diag_blocks(d_bbt[:, NST:], SSM_H).reshape(SSM_W, SSM_P))
    d_lre, d_lim, d_ldt, d_bre, d_bim = _s5_disc_bwd(s5_in, ct, "s5_ddisc")
    over_h = lambda a: a.reshape(SSM_G, SSM_H, SSM_P).sum(axis=1)
    un_tr = lambda a: a.reshape(SSM_G, SSM_H, SSM_P).transpose(0, 2, 1)
    d_cre = _diag_blocks(d_cmat[:NST].T, SSM_H)
    d_cim = -_diag_blocks(d_cmat[NST:].T, SSM_H)

    repl = {
        'mix_norm_g': jnp.concatenate([dg_mix0, dg_mix1], axis=0),
        'ffn_norm_g': jnp.concatenate([gf0['g'], gf1['g']], axis=0),
        'final_norm_g': dg_fin.reshape(D),
        's5_lam_re': over_h(d_lre)[None], 's5_lam_im': over_h(d_lim)[None],
        's5_log_dt': over_h(d_ldt).sum(axis=1)[None],
        's5_b_re': un_tr(d_bre)[None], 's5_b_im': un_tr(d_bim)[None],
        's5_c_re': d_cre[None], 's5_c_im': d_cim[None],
        's5_d': d_d, 's5_b_glu': d_bg,
        'gm_w_s': d_gmw[None], 'gm_b_s': d_gmb.reshape(1, GM_HEADS, CHUNK), 'gm_v_g': d_gmg,
        'ffn_conv_b': jnp.stack([gf0['cb'], gf1['cb']]),
    }
    return loss, grad_x, repl


HBM_SPEC = pl.BlockSpec(memory_space=pltpu.HBM)


def _at_axis(ref, pos, index):
    return ref.at[(slice(None),) * pos + (index,)]


def _all_gather(shards, positions, name):
    n = len(shards)

    def body(*refs):
        xs, outs = refs[:n], refs[n:2 * n]
        send_sems, recv_sems, local_sems = refs[2 * n:]
        x, y, c = lax.axis_index("x"), lax.axis_index("y"), lax.axis_index("c")
        me, sibling = (x, y, c), (x, y, 1 - c)
        chips = [(1 - x, y), (x, 1 - y), (1 - x, 1 - y)]

        def block(p, dev):
            return _at_axis(outs[p], positions[p], 4 * dev[0] + 2 * dev[1] + dev[2])

        def copy(p, k, dev, to, src=None):
            return pltpu.make_async_remote_copy(
                src_ref=block(p, dev) if src is None else src, dst_ref=block(p, dev),
                send_sem=send_sems.at[p, k], recv_sem=recv_sems.at[p, k], device_id=to, device_id_type=MESH_T)

        mine = [pltpu.make_async_copy(xs[p], block(p, me), local_sems.at[p]) for p in range(n)]
        for cp in mine:
            cp.start()
        first = [copy(p, 0, me, sibling, src=xs[p]) for p in range(n)]
        first += [copy(p, 1 + j, me, (*chip, c), src=xs[p]) for j, chip in enumerate(chips) for p in range(n)]
        for cp in first:
            cp.start()
        passed = []
        for j, chip in enumerate(chips):
            for p in range(n):
                copy(p, 1 + j, (*chip, c), me).wait_recv()
                fwd = copy(p, 4 + j, (*chip, c), sibling)
                fwd.start()
                passed.append(fwd)
        for p in range(n):
            copy(p, 0, sibling, me).wait_recv()
        for j, chip in enumerate(chips):
            for p in range(n):
                copy(p, 4 + j, (*chip, 1 - c), me).wait_recv()
        for cp in first + passed:
            cp.wait_send()
        for cp in mine:
            cp.wait()

    out_shape = [jax.ShapeDtypeStruct(s.shape[:pos] + (NDEV,) + s.shape[pos:], s.dtype)
                 for s, pos in zip(shards, positions)]
    return pl.pallas_call(
        body, name=name, out_shape=out_shape, in_specs=[HBM_SPEC] * n, out_specs=[HBM_SPEC] * n,
        scratch_shapes=[pltpu.SemaphoreType.DMA((n, 7)), pltpu.SemaphoreType.DMA((n, 7)),
                        pltpu.SemaphoreType.DMA((n,))])(*shards)


def _other_devices(x, y, c):
    flip = lambda v, bit: 1 - v if bit else v
    return [(flip(x, k >> 2 & 1), flip(y, k >> 1 & 1), flip(c, k & 1)) for k in range(1, NDEV)]


SEM_SPEC = pl.BlockSpec(memory_space=pltpu.SEMAPHORE)
START_EFFECT = pltpu.SideEffectType.DATAFLOW_SIDE_EFFECTING


def _send_start(arrays, scatter, name):
    n = len(arrays)
    lands = [lax.empty((NDEV,) + (a.shape[1:] if scatter else a.shape), a.dtype) for a in arrays]

    def body(*refs):
        xs, ls = refs[:n], refs[n:2 * n]
        send_sems, recv_sems, own_sems, token = refs[2 * n], refs[2 * n + 1], refs[2 * n + 2], refs[4 * n + 3]
        x, y, c = lax.axis_index("x"), lax.axis_index("y"), lax.axis_index("c")
        me = 4 * x + 2 * y + c
        for k, peer in enumerate(_other_devices(x, y, c)):
            for p in range(n):
                src = xs[p].at[4 * peer[0] + 2 * peer[1] + peer[2]] if scatter else xs[p]
                pltpu.make_async_remote_copy(
                    src_ref=src, dst_ref=ls[p].at[me], send_sem=send_sems.at[p * (NDEV - 1) + k],
                    recv_sem=recv_sems.at[p * (NDEV - 1) + k], device_id=peer, device_id_type=MESH_T).start()
        for p in range(n):
            pltpu.make_async_copy(xs[p].at[me] if scatter else xs[p], ls[p].at[me], own_sems.at[p]).start()
        token[...] = jnp.zeros(token.shape, F32)

    sems = pltpu.SemaphoreType.DMA((n * (NDEV - 1),))
    out_shape = ([sems, sems, pltpu.SemaphoreType.DMA((n,))]
                 + [pltpu.HBM(a.shape, a.dtype) for a in list(arrays) + lands] + [jax.ShapeDtypeStruct((8, 128), F32)])
    res = pl.pallas_call(
        body, name=name, out_shape=out_shape, in_specs=[HBM_SPEC] * (2 * n),
        out_specs=[SEM_SPEC] * 3 + [HBM_SPEC] * (2 * n) + [pl.BlockSpec(memory_space=pltpu.VMEM)],
        input_output_aliases={i: 3 + i for i in range(2 * n)},
        compiler_params=pltpu.CompilerParams(has_side_effects=START_EFFECT))(
            *[pltpu.with_memory_space_constraint(a, pltpu.HBM) for a in list(arrays) + lands])
    return res[:3], res[3:3 + n], res[3 + n:3 + 2 * n], res[3 + 2 * n]


def _send_wait(started, scatter, after, name):
    sems, arrays, lands, _ = started
    n = len(arrays)

    def body(*refs):
        xs, ls = refs[:n], refs[n:2 * n]
        send, recv, own = refs[2 * n:2 * n + 3]
        x, y, c = lax.axis_index("x"), lax.axis_index("y"), lax.axis_index("c")
        me = 4 * x + 2 * y + c
        for p in range(n):
            pltpu.make_async_copy(xs[p].at[me] if scatter else xs[p], ls[p].at[me], own.at[p]).wait()
        for k, peer in enumerate(_other_devices(x, y, c)):
            slot = 4 * peer[0] + 2 * peer[1] + peer[2]
            for p in range(n):
                cp = pltpu.make_async_remote_copy(
                    src_ref=xs[p].at[slot] if scatter else xs[p], dst_ref=ls[p].at[slot],
                    send_sem=send.at[p * (NDEV - 1) + k], recv_sem=recv.at[p * (NDEV - 1) + k], device_id=peer,
                    device_id_type=MESH_T)
                cp.wait_send()
                cp.wait_recv()

    res = pl.pallas_call(
        body, name=name, out_shape=[pltpu.HBM(a.shape, a.dtype) for a in list(arrays) + list(lands)],
        in_specs=[HBM_SPEC] * (2 * n) + [SEM_SPEC] * 3 + [pl.BlockSpec(memory_space=pl.ANY)],
        out_specs=[HBM_SPEC] * (2 * n), input_output_aliases={i: i for i in range(2 * n)},
        compiler_params=pltpu.CompilerParams(has_side_effects=START_EFFECT))(
            *arrays, *lands, *sems, after)
    return res[n:]


def _row_block(rows, cols, itemsize=4, target=2**20):
    best = None
    for tr in range(16, rows + 1, 16):
        if rows % tr == 0 and tr * cols * itemsize <= target:
            best = tr
    return best or rows


def _adamw(w, m, v, gparts, name):
    parts, rows, cols = gparts.shape
    tr = _row_block(rows, cols, target=2**19)
    bc1 = 1.0 - ADAM_B1 ** ADAM_STEP
    bc2 = 1.0 - ADAM_B2 ** ADAM_STEP

    def body(w_ref, m_ref, v_ref, g_ref, go_ref, d_ref, mo_ref, vo_ref):
        g = g_ref[0].astype(F32)
        for k in range(1, parts):
            g = g + g_ref[k].astype(F32)
        mn = ADAM_B1 * m_ref[...] + (1.0 - ADAM_B1) * g
        vn = ADAM_B2 * v_ref[...] + (1.0 - ADAM_B2) * (g * g)
        go_ref[...] = g
        mo_ref[...] = mn
        vo_ref[...] = vn
        d_ref[...] = -ADAM_LR * ((mn / bc1) / (jnp.sqrt(vn / bc2) + ADAM_EPS) + ADAM_WD * w_ref[...])

    blk = pl.BlockSpec((tr, cols), lambda i: (i, 0))
    shp = jax.ShapeDtypeStruct((rows, cols), F32)
    return pl.pallas_call(
        body, name=name, grid=(rows // tr,),
        in_specs=[blk, blk, blk, pl.BlockSpec((parts, tr, cols), lambda i: (0, i, 0))],
        out_specs=[blk] * 4, out_shape=[shp] * 4, compiler_params=_cp(("parallel",)))(w, m, v, gparts)


def _pack(arrays, rows):
    flat = jnp.concatenate([a.reshape(-1).astype(F32) for a in arrays])
    return jnp.pad(flat, (0, rows * PACK_COLS - flat.shape[0])).reshape(rows, PACK_COLS)


def _unpack(buf, shapes):
    flat = buf.reshape(-1)
    out, off = [], 0
    for shp in shapes:
        size = int(np.prod(shp))
        out.append(flat[off:off + size].reshape(shp))
        off += size
    return out


REPL_SHAPES = {'mix_norm_g': (2, 1024), 'ffn_norm_g': (2, 1024), 'final_norm_g': (1024,), 's5_lam_re': (1, 16, 64),
               's5_lam_im': (1, 16, 64), 's5_log_dt': (1, 16), 's5_b_re': (1, 16, 64, 16), 's5_b_im': (1, 16, 64, 16),
               's5_c_re': (1, 16, 16, 64), 's5_c_im': (1, 16, 16, 64), 's5_d': (1, 256), 's5_b_glu': (1, 256),
               'gm_w_s': (1, 6, 128, 128), 'gm_b_s': (1, 6, 128), 'gm_v_g': (1, 768), 'ffn_conv_b': (2, 5632)}
REPL_ELEMS = sum(int(np.prod(REPL_SHAPES[n])) for n in REPL_ORDER)
REPL_ROWS = -(-REPL_ELEMS // (PACK_COLS * 8)) * 8

GATHER_DTYPE = {'ev_w_in': BF, 'ev_w_out': BF, 's5_w_glu': BF, 'od_w_in': BF, 'od_conv_w': F32, 'od_conv_b': F32,
                'od_w_out': BF, 'ffn_w_up': BF, 'ffn_conv_w': F32, 'ffn_w_down': BF}
GATHER_EVEN = ['ev_w_in', 'ev_w_out', 's5_w_glu']
GATHER_FFN0 = ['ffn_w_up0', 'ffn_conv_w0', 'ffn_w_down0']
GATHER_REST = ['od_w_in', 'od_conv_w', 'od_conv_b', 'od_w_out', 'ffn_w_up1', 'ffn_conv_w1', 'ffn_w_down1']

def _squeeze_lead(a):
    return a.reshape(a.shape[1:]) if a.shape[0] == 1 and a.ndim > 2 else a


def kernel(x, mix_norm_g, ffn_norm_g, final_norm_g, ev_w_in, ev_w_out, s5_lam_re, s5_lam_im, s5_log_dt, s5_b_re, s5_b_im, s5_c_re, s5_c_im, s5_d, s5_w_glu, s5_b_glu, gm_w_s, gm_b_s, gm_v_g, od_w_in, od_conv_w, od_conv_b, od_w_out, ffn_w_up, ffn_conv_w, ffn_conv_b, ffn_w_down, loss_target, m_mix_norm_g, m_ffn_norm_g, m_final_norm_g, m_ev_w_in, m_ev_w_out, m_s5_lam_re, m_s5_lam_im, m_s5_log_dt, m_s5_b_re, m_s5_b_im, m_s5_c_re, m_s5_c_im, m_s5_d, m_s5_w_glu, m_s5_b_glu, m_gm_w_s, m_gm_b_s, m_gm_v_g, m_od_w_in, m_od_conv_w, m_od_conv_b, m_od_w_out, m_ffn_w_up, m_ffn_conv_w, m_ffn_conv_b, m_ffn_w_down, v_mix_norm_g, v_ffn_norm_g, v_final_norm_g, v_ev_w_in, v_ev_w_out, v_s5_lam_re, v_s5_lam_im, v_s5_log_dt, v_s5_b_re, v_s5_b_im, v_s5_c_re, v_s5_c_im, v_s5_d, v_s5_w_glu, v_s5_b_glu, v_gm_w_s, v_gm_b_s, v_gm_v_g, v_od_w_in, v_od_conv_w, v_od_conv_b, v_od_w_out, v_ffn_w_up, v_ffn_conv_w, v_ffn_conv_b, v_ffn_w_down):
    given = dict(locals())
    weights = {n: given[n] for n in WEIGHT_ORDER}
    nseq, seq, _ = x.shape

    send = {}
    for name in SHARDED_ORDER:
        a = weights[name].astype(GATHER_DTYPE[name])
        if a.shape[0] == 2:
            send[name + '0'], send[name + '1'] = a[0], a[1]
        else:
            send[name] = _squeeze_lead(a)
    gathers = [_send_start([send[n] for n in names], False, f"gather_{tag}_start")
               for tag, names in (("ffn0", GATHER_FFN0), ("rest", GATHER_REST))]
    token = gathers[0][3] + gathers[1][3]

    def waiter(tag, names, started):
        return lambda after: dict(zip(names, _send_wait(started, False, after, f"gather_{tag}_wait")))

    gathered = dict(zip(GATHER_EVEN, _all_gather([send[n] for n in GATHER_EVEN], [0] * len(GATHER_EVEN),
                                                 "gather_even")))

    scatters = []

    def scatter(tag, grads):
        names = list(grads)
        started = _send_start([grads[n].astype(BF) for n in names], True, f"{tag}_start")
        scatters.append((tag, names, started))
        return started[3]

    loss_row, grad_x, g_repl = _local_step(
        x.reshape(nseq * seq, D), loss_target.reshape(nseq * seq, D), weights, gathered,
        waiter("ffn0", GATHER_FFN0, gathers[0]), waiter("rest", GATHER_REST, gathers[1]), token, scatter, seq)
    loss = lax.psum(loss_row[0, 0], ("x", "y", "c"))

    parts = {}
    for tag, names, started in scatters:
        parts.update(zip(names, _send_wait(started, True, grad_x, f"{tag}_wait")))
    repl_parts = _all_gather([_pack([g_repl[n] for n in REPL_ORDER], REPL_ROWS)], [0], "gather_small_grads")[0]

    out = {}
    for name in SHARDED_ORDER:
        w = weights[name]
        if name + '0' in parts:
            gp = jnp.stack([parts[name + '0'], parts[name + '1']], axis=1)
        else:
            gp = parts[name]
        to_rows = lambda a: a.reshape(-1, w.shape[-1])
        res = _adamw(to_rows(w), to_rows(given["m_" + name]), to_rows(given["v_" + name]),
                     gp.reshape(NDEV, -1, w.shape[-1]), f"adamw_{name}")
        out[name] = [r.reshape(w.shape) for r in res]
    rp = _adamw(_pack([weights[n] for n in REPL_ORDER], REPL_ROWS),
                _pack([given["m_" + n] for n in REPL_ORDER], REPL_ROWS),
                _pack([given["v_" + n] for n in REPL_ORDER], REPL_ROWS), repl_parts, "adamw_replicated")
    rp_shapes = [weights[n].shape for n in REPL_ORDER]
    for k in range(4):
        for name, a in zip(REPL_ORDER, _unpack(rp[k], rp_shapes)):
            out.setdefault(name, [None] * 4)[k] = a
    results = [[out[n][k] for n in WEIGHT_ORDER] for k in range(4)]
    grad_w, delta_w, new_m, new_v = results
    return (loss, grad_x.reshape(nseq, seq, D), *grad_w, *delta_w, *new_m, *new_v)
```

```python
import math

import jax
import jax.numpy as jnp
import numpy as np
from jax import lax
from jax.experimental import pallas as pl
from jax.experimental.pallas import tpu as pltpu

F32 = jnp.float32
BF = jnp.bfloat16

D = 1024
DFF = 2816
NDEV = 8
SSM_W = 256
SSM_G = 16
SSM_H = 16
SSM_P = 64
NST = SSM_G * SSM_P
GM_W = 768
GM_HEADS = 6
CHUNK = 128
EPS = 1e-6
LAM_MAX = -1e-4
FB = 256
FSH = 2 * DFF // NDEV
ROW_BLOCK = 512
CONV_ROW_BLOCK = 1024
VMEM_LIMIT = 48 * 2**20
PACK_COLS = 1024
MESH_T = pl.DeviceIdType.MESH

ADAM_LR = 0.001
ADAM_B1 = 0.9
ADAM_B2 = 0.999
ADAM_EPS = 1e-08
ADAM_WD = 0.01
ADAM_STEP = 10

WEIGHT_ORDER = ['mix_norm_g', 'ffn_norm_g', 'final_norm_g', 'ev_w_in', 'ev_w_out', 's5_lam_re', 's5_lam_im',
                's5_log_dt', 's5_b_re', 's5_b_im', 's5_c_re', 's5_c_im', 's5_d', 's5_w_glu', 's5_b_glu', 'gm_w_s',
                'gm_b_s', 'gm_v_g', 'od_w_in', 'od_conv_w', 'od_conv_b', 'od_w_out', 'ffn_w_up', 'ffn_conv_w',
                'ffn_conv_b', 'ffn_w_down']
SHARDED = {'ev_w_in': ((1, 1024, 1792), 2), 'ev_w_out': ((1, 1024, 1024), 1), 's5_w_glu': ((1, 256, 256), 1),
           'od_w_in': ((1, 1024, 3072), 2), 'od_conv_w': ((1, 3, 1024), 2), 'od_conv_b': ((1, 1024), 1),
           'od_w_out': ((1, 1024, 1024), 1), 'ffn_w_up': ((2, 1024, 5632), 2), 'ffn_conv_w': ((2, 3, 5632), 2),
           'ffn_w_down': ((2, 2816, 1024), 1)}
SHARDED_ORDER = [n for n in WEIGHT_ORDER if n in SHARDED]
REPL_ORDER = [n for n in WEIGHT_ORDER if n not in SHARDED]


def _cp(sem):
    return pltpu.CompilerParams(dimension_semantics=sem, vmem_limit_bytes=VMEM_LIMIT)


def _sigmoid(x):
    return 1.0 / (1.0 + jnp.exp(-x))


_GELU_K = math.sqrt(2.0 / math.pi)


def _gelu(x):
    return 0.5 * x * (1.0 + jnp.tanh(_GELU_K * (x + 0.044715 * x * x * x)))


def _gelu_grad(x):
    t = jnp.tanh(_GELU_K * (x + 0.044715 * x * x * x))
    return 0.5 * (1.0 + t) + 0.5 * x * (1.0 - t * t) * _GELU_K * (1.0 + 3.0 * 0.044715 * x * x)


def _colsum(x):
    return jnp.sum(x, axis=0, keepdims=True)


def _accumulate(ref, first, part):
    @pl.when(first)
    def _():
        ref[...] = part

    @pl.when(jnp.logical_not(first))
    def _():
        ref[...] += part


_DIMS = {'nn': (((1,), (0,)), ((), ())), 'nt': (((1,), (1,)), ((), ())), 'tn': (((0,), (0,)), ((), ()))}


def _rms(xv):
    r = lax.rsqrt(jnp.mean(xv * xv, axis=-1, keepdims=True) + EPS)
    return r, xv * r


def _norm_grad(dyv, gv, r, xh):
    dyg = dyv * gv
    return r * (dyg - xh * jnp.mean(dyg * xh, axis=-1, keepdims=True))


def _tail_io(tail, tm, index, n):
    rows = pl.BlockSpec((tm, D), index)
    vec = pl.BlockSpec((1, D), lambda *_: (0, 0))
    full, half, gain = (jax.ShapeDtypeStruct((n, D), F32), jax.ShapeDtypeStruct((n, D), BF),
                        jax.ShapeDtypeStruct((1, D), F32))
    if tail[0] == 'norm_fwd':
        return [tail[1]], [vec], [rows, rows], [full, half]
    if tail[0] == 'norm_bwd':
        return list(tail[1:]), [rows, vec, rows], [rows, rows, vec], [full, half, gain]
    return (list(tail[1:]), [vec, rows], [pl.BlockSpec((1, 128), lambda *_: (0, 0)), rows, rows, vec],
            [jax.ShapeDtypeStruct((1, 128), F32), full, half, gain])


def _tail_apply(kind, tot, tail_refs, outs, first):
    if kind == 'norm_fwd':
        r, xh = _rms(tot)
        outs[0][...] = tot
        outs[1][...] = (xh * tail_refs[0][...]).astype(BF)
    elif kind == 'norm_bwd':
        x_ref, g_ref, dr_ref = tail_refs
        r, xh = _rms(x_ref[...])
        dx = dr_ref[...] + _norm_grad(tot, g_ref[...], r, xh)
        outs[0][...] = dx
        outs[1][...] = dx.astype(BF)
        _accumulate(outs[2], first, _colsum(tot * xh))
    else:
        gv = tail_refs[0][...]
        r, xh = _rms(tot)
        err = xh * gv - tail_refs[1][...]
        part = 0.5 * jnp.sum(jnp.mean(err * err, axis=-1, keepdims=True), axis=0, keepdims=True)
        _accumulate(outs[0], first, jnp.broadcast_to(part, (1, 128)))
        dyv = err * (1.0 / D)
        dx = _norm_grad(dyv, gv, r, xh)
        outs[1][...] = dx
        outs[2][...] = dx.astype(BF)
        _accumulate(outs[3], first, _colsum(dyv * xh))


def _matmul(a, b, mode, tm, tn, tk, name, resid=None, out_dtype=F32, tail=None):
    if mode == 'tn':
        kdim, m = a.shape
    else:
        m, kdim = a.shape
    n = b.shape[0] if mode == 'nt' else b.shape[1]
    tm, tn, tk = min(tm, m), min(tn, n), min(tk, kdim)
    assert m % tm == 0 and n % tn == 0 and kdim % tk == 0, (name, m, n, kdim, tm, tn, tk)
    a_spec = (pl.BlockSpec((tk, tm), lambda i, j, k: (k, i)) if mode == 'tn'
              else pl.BlockSpec((tm, tk), lambda i, j, k: (i, k)))
    b_spec = (pl.BlockSpec((tn, tk), lambda i, j, k: (j, k)) if mode == 'nt'
              else pl.BlockSpec((tk, tn), lambda i, j, k: (k, j)))
    o_spec = pl.BlockSpec((tm, tn), lambda i, j, k: (i, j))
    return _matmul_spec(a, b, mode, (m // tm, n // tn, kdim // tk), a_spec, b_spec, o_spec, (m, n), name,
                        resid=resid, out_dtype=out_dtype, tail=tail)


def _matmul_spec(a, b, mode, grid, a_spec, b_spec, o_spec, out_shape, name, resid=None, out_dtype=F32, tail=None):
    nk = grid[2]
    tm, tn = o_spec.block_shape[-2:]
    dims = _DIMS[mode]
    has_resid = resid is not None
    operands = [a, b] + ([resid] if has_resid else [])
    in_specs = [a_spec, b_spec] + ([o_spec] if has_resid else [])
    out_specs, out_shapes = [o_spec], [jax.ShapeDtypeStruct(out_shape, out_dtype)]
    n_tail = 0
    if tail is not None:
        assert tn == D and grid[1] == 1, name
        extra, extra_specs, out_specs, out_shapes = _tail_io(tail, tm, lambda i, j, k: (i, 0), out_shape[0])
        operands, in_specs, n_tail = operands + extra, in_specs + extra_specs, len(extra)
    n_in, n_out = len(operands), len(out_specs)

    def body(*refs):
        ins, outs = refs[:n_in], refs[n_in:n_in + n_out]
        a_ref, b_ref = ins[:2]
        part = lax.dot_general(a_ref[...].astype(BF), b_ref[...].astype(BF), dims, preferred_element_type=F32)

        def finish(tot):
            if has_resid:
                tot = tot + ins[2][...]
            if tail is not None:
                _tail_apply(tail[0], tot, ins[n_in - n_tail:], outs, pl.program_id(0) == 0)
            else:
                outs[0][...] = tot.astype(out_dtype)

        if nk == 1:
            finish(part)
        else:
            acc = refs[-1]
            k = pl.program_id(2)

            @pl.when(k == 0)
            def _():
                acc[...] = part

            @pl.when(k > 0)
            def _():
                acc[...] += part

            @pl.when(k == nk - 1)
            def _():
                finish(acc[...])

    res = pl.pallas_call(
        body, name=name, grid=grid, in_specs=in_specs, out_specs=out_specs, out_shape=out_shapes,
        scratch_shapes=[pltpu.VMEM((tm, tn), F32)] if nk > 1 else [],
        compiler_params=_cp(("arbitrary",) * 3 if tail is not None else ("parallel", "parallel", "arbitrary")))(*operands)
    return res if tail is not None else res[0]


def _matmul_shards(a, b, mode, tm, tn, name, resid=None, out_dtype=F32, tail=None):
    shards, m, kdim = a.shape
    n = b.shape[2] if mode == 'nn' else b.shape[1]
    tm, tn = min(tm, m), min(tn, n)
    dims = _DIMS[mode]
    has_resid = resid is not None
    b_spec = (pl.BlockSpec((shards, kdim, tn), lambda i, j: (0, 0, j)) if mode == 'nn'
              else pl.BlockSpec((shards, tn, kdim), lambda i, j: (0, j, 0)))
    o_spec = pl.BlockSpec((tm, tn), lambda i, j: (i, j))
    operands = [a, b] + ([resid] if has_resid else [])
    in_specs = [pl.BlockSpec((shards, tm, kdim), lambda i, j: (0, i, 0)), b_spec] + ([o_spec] if has_resid else [])
    out_specs, out_shapes = [o_spec], [jax.ShapeDtypeStruct((m, n), out_dtype)]
    n_tail = 0
    if tail is not None:
        assert tn == D and n == D, name
        extra, extra_specs, out_specs, out_shapes = _tail_io(tail, tm, lambda i, j: (i, 0), m)
        operands, in_specs, n_tail = operands + extra, in_specs + extra_specs, len(extra)
    n_in, n_out = len(operands), len(out_specs)

    def body(*refs):
        ins, outs = refs[:n_in], refs[n_in:n_in + n_out]
        acc = lax.dot_general(ins[0][0], ins[1][0], dims, preferred_element_type=F32)
        for s in range(1, shards):
            acc = acc + lax.dot_general(ins[0][s], ins[1][s], dims, preferred_element_type=F32)
        if has_resid:
            acc = acc + ins[2][...]
        if tail is not None:
            _tail_apply(tail[0], acc, ins[n_in - n_tail:], outs, pl.program_id(0) == 0)
        else:
            outs[0][...] = acc.astype(out_dtype)

    res = pl.pallas_call(
        body, name=name, grid=(m // tm, n // tn), in_specs=in_specs, out_specs=out_specs, out_shape=out_shapes,
        compiler_params=_cp(("arbitrary", "arbitrary") if tail is not None else ("parallel", "parallel")))(*operands)
    return res if tail is not None else res[0]


def _matmul_tn_shards(a, b, sp, tk, name, out_dtype=BF):
    a_sh, b_sh = a.ndim == 3, b.ndim == 3
    shards = a.shape[0] if a_sh else b.shape[0]
    kdim, m, n = a.shape[-2], a.shape[-1], b.shape[-1]
    nk = kdim // tk

    def body(a_ref, b_ref, o_ref, acc):
        k = pl.program_id(1)
        parts = [lax.dot_general(a_ref[q] if a_sh else a_ref[...], b_ref[q] if b_sh else b_ref[...], _DIMS['tn'],
                                 preferred_element_type=F32) for q in range(sp)]

        @pl.when(k == 0)
        def _():
            for q in range(sp):
                acc[q] = parts[q]

        @pl.when(k > 0)
        def _():
            for q in range(sp):
                acc[q] += parts[q]

        @pl.when(k == nk - 1)
        def _():
            o_ref[...] = acc[...].astype(out_dtype)

    spec = lambda sharded, cols: (pl.BlockSpec((sp, tk, cols), lambda s, k: (s, k, 0)) if sharded
                                  else pl.BlockSpec((tk, cols), lambda s, k: (k, 0)))
    return pl.pallas_call(
        body, name=name, grid=(shards // sp, nk), in_specs=[spec(a_sh, m), spec(b_sh, n)],
        out_specs=pl.BlockSpec((sp, m, n), lambda s, k: (s, 0, 0)),
        out_shape=jax.ShapeDtypeStruct((shards, m, n), out_dtype),
        scratch_shapes=[pltpu.VMEM((sp, m, n), F32)],
        compiler_params=_cp(("parallel", "arbitrary")))(a, b)


def _rmsnorm_fwd(x, g, name):
    n = x.shape[0]
    tm = min(512, n)

    def body(x_ref, g_ref, o_ref):
        xv = x_ref[...]
        r = lax.rsqrt(jnp.mean(xv * xv, axis=-1, keepdims=True) + EPS)
        o_ref[...] = (xv * r * g_ref[...]).astype(BF)

    return pl.pallas_call(
        body, name=name, grid=(n // tm,),
        in_specs=[pl.BlockSpec((tm, D), lambda i: (i, 0)), pl.BlockSpec((1, D), lambda i: (0, 0))],
        out_specs=pl.BlockSpec((tm, D), lambda i: (i, 0)),
        out_shape=jax.ShapeDtypeStruct((n, D), BF), compiler_params=_cp(("parallel",)))(x, g)


def _prev_rows(x, halo_ref, lanes, scale, row):
    h7 = halo_ref[7:8, lanes] * scale
    h6 = halo_ref[6:7, lanes] * scale
    p1 = jnp.where(row == 0, h7, pltpu.roll(x, 1, 0))
    p2 = jnp.where(row == 0, h6, jnp.where(row == 1, h7, pltpu.roll(x, 2, 0)))
    return p1, p2


def _halo_maps(tm, n_rows):
    r8 = tm // 8
    last = n_rows // 8 - 1
    prev = lambda i: jnp.maximum(i * r8 - 1, 0)
    nxt = lambda i: jnp.minimum((i + 1) * r8, last)
    return prev, nxt


def _lane_blocks(width):
    return [slice(lo, min(lo + 128, width)) for lo in range(0, width, 128)]


def _conv_taps(w_ref, b_ref, g, lanes):
    return w_ref[g, 0:1, lanes], w_ref[g, 1:2, lanes], w_ref[g, 2:3, lanes], b_ref[g, :, lanes]


def _conv_tile(x, prev1, prev2, taps, row):
    w0, w1, w2, b = taps
    r1 = pltpu.roll(x, 1, 0)
    r2 = pltpu.roll(x, 2, 0)
    x1 = jnp.where(row == 0, prev1, r1)
    x2 = jnp.where(row < 2, prev2, r2)
    return b + w0 * x2 + w1 * x1 + w2 * x, x1, x2, r1, r2


def _halo16_maps(tm, n_rows):
    r16 = tm // 16
    last = n_rows // 16 - 1
    return (lambda i: jnp.maximum(i * r16 - 1, 0)), (lambda i: jnp.minimum((i + 1) * r16, last))


def _ffn_conv_fwd(up, cw, cb, seq, name):
    n = up.shape[2]
    tm = min(CONV_ROW_BLOCK, seq)
    prev, _ = _halo16_maps(tm, n)

    def body(u_ref, h_ref, w_ref, b_ref, o_ref, d_ref):
        i = pl.program_id(1)
        scale = jnp.where(lax.rem(i * tm, seq) == 0, 0.0, 1.0)
        for lanes in _lane_blocks(FSH):
            lw = lanes.stop - lanes.start
            row = lax.broadcasted_iota(jnp.int32, (8, lw), 0)
            taps = [_conv_taps(w_ref, b_ref, g, lanes) for g in range(2)]

            def tile(xs, carry):
                hc, nxt = [], []
                for g in range(2):
                    conv, _, _, r1, r2 = _conv_tile(xs[g], carry[2 * g], carry[2 * g + 1], taps[g], row)
                    hc.append(conv)
                    nxt += [r1, r2]
                s = _sigmoid(hc[0])
                silu = hc[0] * s
                return (silu * hc[1], hc[1] * (s * (1.0 + hc[0] * (1.0 - s))), silu), tuple(nxt)

            carry = []
            for g in range(2):
                halo = h_ref[g, :, lanes].astype(F32)[8:] * scale
                carry += [pltpu.roll(halo, 1, 0), pltpu.roll(halo, 2, 0)]
            carry = tuple(carry)
            for m in range(tm // 16):
                rows = slice(m * 16, m * 16 + 16)
                x16 = [u_ref[g, rows, lanes].astype(F32) for g in range(2)]
                a, carry = tile([x[:8] for x in x16], carry)
                b, carry = tile([x[8:] for x in x16], carry)
                o_ref[rows, lanes] = jnp.concatenate([a[0], b[0]], axis=0).astype(BF)
                d_ref[0, rows, lanes] = jnp.concatenate([a[1], b[1]], axis=0).astype(BF)
                d_ref[1, rows, lanes] = jnp.concatenate([a[2], b[2]], axis=0).astype(BF)

    return pl.pallas_call(
        body, name=name, grid=(4, n // tm),
        in_specs=[pl.BlockSpec((2, None, tm, FSH), lambda j, i: (0, j, i, 0)),
                  pl.BlockSpec((2, None, 16, FSH), lambda j, i: (0, j, prev(i), 0)),
                  pl.BlockSpec((2, None, 3, FSH), lambda j, i: (0, j, 0, 0)),
                  pl.BlockSpec((2, None, 1, FSH), lambda j, i: (0, j, 0, 0))],
        out_specs=[pl.BlockSpec((None, tm, FSH), lambda j, i: (j, i, 0)),
                   pl.BlockSpec((2, None, tm, FSH), lambda j, i: (0, j, i, 0))],
        out_shape=[jax.ShapeDtypeStruct((4, n, FSH), BF), jax.ShapeDtypeStruct((2, 4, n, FSH), BF)],
        compiler_params=_cp(("parallel", "parallel")))(up, up, cw, cb)


def _ffn_conv_bwd(up, dgate, dact, cw, seq, name):
    n = up.shape[2]
    tm = min(CONV_ROW_BLOCK, seq)
    _, nxt = _halo16_maps(tm, n)

    def body(u_ref, g_ref, gn_ref, da_ref, dn_ref, w_ref, du_ref, dw_ref, db_ref):
        i = pl.program_id(1)
        sn = jnp.where(lax.rem((i + 1) * tm, seq) == 0, 0.0, 1.0)
        first = i == 0
        for lanes in _lane_blocks(FSH):
            lw = lanes.stop - lanes.start
            row = lax.broadcasted_iota(jnp.int32, (8, lw), 0)
            taps = [(w_ref[g, 0:1, lanes], w_ref[g, 1:2, lanes], w_ref[g, 2:3, lanes]) for g in range(2)]

            def dconv(gs, da):
                ds = [gs[g] * da for g in range(2)]
                return [(d, pltpu.roll(d, 7, 0), pltpu.roll(d, 6, 0)) for d in ds]

            def finish(cur, after, xs, sums):
                dups, new_sums = [], []
                for g in range(2):
                    w0, w1, w2 = taps[g]
                    s1 = jnp.where(row == 7, after[g][1], cur[g][1])
                    s2 = jnp.where(row >= 6, after[g][2], cur[g][2])
                    dups.append(w2 * cur[g][0] + w1 * s1 + w0 * s2)
                    acc = sums[g]
                    new_sums.append((acc[0] + xs[g] * s2, acc[1] + xs[g] * s1, acc[2] + xs[g] * cur[g][0],
                                     acc[3] + cur[g][0]))
                return dups, new_sums

            def emit(m, held, after, sums):
                (ta, xa), (tb, xb) = held
                dup_a, sums = finish(ta, tb, xa, sums)
                dup_b, sums = finish(tb, after, xb, sums)
                for g in range(2):
                    du_ref[g, m * 16:m * 16 + 16, lanes] = jnp.concatenate([dup_a[g], dup_b[g]], axis=0).astype(BF)
                return sums

            zero = jnp.zeros((8, lw), F32)
            sums = [(zero,) * 4, (zero,) * 4]
            held = None
            for m in range(tm // 16):
                rows = slice(m * 16, m * 16 + 16)
                x16 = [u_ref[g, rows, lanes].astype(F32) for g in range(2)]
                g16 = [g_ref[g, rows, lanes].astype(F32) for g in range(2)]
                d16 = da_ref[rows, lanes].astype(F32)
                ta = dconv([a[:8] for a in g16], d16[:8])
                tb = dconv([a[8:] for a in g16], d16[8:])
                if held is not None:
                    sums = emit(m - 1, held, ta, sums)
                held = ((ta, [x[:8] for x in x16]), (tb, [x[8:] for x in x16]))
            tn_ = dconv([gn_ref[g, :, lanes].astype(F32)[:8] for g in range(2)], dn_ref[:, lanes].astype(F32)[:8] * sn)
            sums = emit(tm // 16 - 1, held, tn_, sums)
            for g in range(2):
                for k in range(3):
                    _accumulate(dw_ref.at[g, k:k + 1, lanes], first, _colsum(sums[g][k]))
                _accumulate(db_ref.at[g, :, lanes], first, _colsum(sums[g][3]))

    return pl.pallas_call(
        body, name=name, grid=(4, n // tm),
        in_specs=[pl.BlockSpec((2, None, tm, FSH), lambda j, i: (0, j, i, 0)),
                  pl.BlockSpec((2, None, tm, FSH), lambda j, i: (0, j, i, 0)),
                  pl.BlockSpec((2, None, 16, FSH), lambda j, i: (0, j, nxt(i), 0)),
                  pl.BlockSpec((None, tm, FSH), lambda j, i: (j, i, 0)),
                  pl.BlockSpec((None, 16, FSH), lambda j, i: (j, nxt(i), 0)),
                  pl.BlockSpec((2, None, 3, FSH), lambda j, i: (0, j, 0, 0))],
        out_specs=[pl.BlockSpec((2, None, tm, FSH), lambda j, i: (0, j, i, 0)),
                   pl.BlockSpec((2, None, 3, FSH), lambda j, i: (0, j, 0, 0)),
                   pl.BlockSpec((2, None, 1, FSH), lambda j, i: (0, j, 0, 0))],
        out_shape=[jax.ShapeDtypeStruct((2, 4, n, FSH), BF), jax.ShapeDtypeStruct((2, 4, 3, FSH), F32),
                   jax.ShapeDtypeStruct((2, 4, 1, FSH), F32)],
        compiler_params=_cp(("parallel", "arbitrary")))(up, dgate, dgate, dact, dact, cw)


def _shortconv_fwd(p, cw, cb, seq, name):
    n = p.shape[0]
    tm = min(CONV_ROW_BLOCK, seq)
    prev, _ = _halo_maps(tm, n)

    def body(p_ref, h_ref, w_ref, b_ref, o_ref):
        i = pl.program_id(1)
        scale = jnp.where(lax.rem(i * tm, seq) == 0, 0.0, 1.0)
        q = p_ref[:, FB:2 * FB] * p_ref[:, 2 * FB:]
        row = lax.broadcasted_iota(jnp.int32, q.shape, 0)
        h7 = h_ref[7:8, FB:2 * FB] * h_ref[7:8, 2 * FB:] * scale
        h6 = h_ref[6:7, FB:2 * FB] * h_ref[6:7, 2 * FB:] * scale
        p1 = jnp.where(row == 0, h7, pltpu.roll(q, 1, 0))
        p2 = jnp.where(row == 0, h6, jnp.where(row == 1, h7, pltpu.roll(q, 2, 0)))
        conv = b_ref[...] + w_ref[0:1, :] * p2 + w_ref[1:2, :] * p1 + w_ref[2:3, :] * q
        o_ref[...] = (p_ref[:, :FB] * conv).astype(BF)

    return pl.pallas_call(
        body, name=name, grid=(D // FB, n // tm),
        in_specs=[pl.BlockSpec((tm, 3 * FB), lambda j, i: (i, j)),
                  pl.BlockSpec((8, 3 * FB), lambda j, i: (prev(i), j)),
                  pl.BlockSpec((3, FB), lambda j, i: (0, j)),
                  pl.BlockSpec((1, FB), lambda j, i: (0, j))],
        out_specs=pl.BlockSpec((tm, FB), lambda j, i: (i, j)),
        out_shape=jax.ShapeDtypeStruct((n, D), BF), compiler_params=_cp(("parallel", "parallel")))(p, p, cw, cb)


def _shortconv_bwd(p, dmix, cw, cb, seq, name):
    n = p.shape[0]
    tm = min(CONV_ROW_BLOCK, seq)
    ext = tm + 16
    prev, nxt = _halo_maps(tm, n)

    def body(p_ref, pp_ref, pn_ref, dm_ref, dn_ref, w_ref, b_ref, dp_ref, dw_ref, db_ref, qx, cx):
        i = pl.program_id(1)
        sp = jnp.where(lax.rem(i * tm, seq) == 0, 0.0, 1.0)
        sn = jnp.where(lax.rem((i + 1) * tm, seq) == 0, 0.0, 1.0)
        bg, cg, hx = p_ref[:, :FB], p_ref[:, FB:2 * FB], p_ref[:, 2 * FB:]
        dm = dm_ref[...]
        qx[0:8, :] = pp_ref[:, FB:2 * FB] * pp_ref[:, 2 * FB:] * sp
        qx[8:8 + tm, :] = cg * hx
        qx[8 + tm:, :] = jnp.zeros((8, FB), F32)
        cx[0:8, :] = jnp.zeros((8, FB), F32)
        cx[8:8 + tm, :] = dm * bg
        cx[8 + tm:, :] = dn_ref[...] * pn_ref[:, :FB] * sn
        q0 = qx[...]
        q1 = pltpu.roll(q0, 1, 0)
        q2 = pltpu.roll(q0, 2, 0)
        main = slice(8, 8 + tm)
        conv = b_ref[...] + w_ref[0:1, :] * q2[main] + w_ref[1:2, :] * q1[main] + w_ref[2:3, :] * q0[main]
        dc = cx[...]
        dq = (w_ref[2:3, :] * dc + w_ref[1:2, :] * pltpu.roll(dc, ext - 1, 0)
              + w_ref[0:1, :] * pltpu.roll(dc, ext - 2, 0))[main]
        dp_ref[:, :FB] = (dm * conv).astype(BF)
        dp_ref[:, FB:2 * FB] = (dq * hx).astype(BF)
        dp_ref[:, 2 * FB:] = (dq * cg).astype(BF)
        first = i == 0
        dcm = dc[main]
        _accumulate(dw_ref.at[0:1, :], first, _colsum(dcm * q2[main]))
        _accumulate(dw_ref.at[1:2, :], first, _colsum(dcm * q1[main]))
        _accumulate(dw_ref.at[2:3, :], first, _colsum(dcm * q0[main]))
        _accumulate(db_ref, first, _colsum(dcm))

    return pl.pallas_call(
        body, name=name, grid=(D // FB, n // tm),
        in_specs=[pl.BlockSpec((tm, 3 * FB), lambda j, i: (i, j)),
                  pl.BlockSpec((8, 3 * FB), lambda j, i: (prev(i), j)),
                  pl.BlockSpec((8, 3 * FB), lambda j, i: (nxt(i), j)),
                  pl.BlockSpec((tm, FB), lambda j, i: (i, j)),
                  pl.BlockSpec((8, FB), lambda j, i: (nxt(i), j)),
                  pl.BlockSpec((3, FB), lambda j, i: (0, j)),
                  pl.BlockSpec((1, FB), lambda j, i: (0, j))],
        out_specs=[pl.BlockSpec((tm, 3 * FB), lambda j, i: (i, j)),
                   pl.BlockSpec((3, FB), lambda j, i: (0, j)),
                   pl.BlockSpec((1, FB), lambda j, i: (0, j))],
        out_shape=[jax.ShapeDtypeStruct((n, 3 * D), BF), jax.ShapeDtypeStruct((3, D), F32),
                   jax.ShapeDtypeStruct((1, D), F32)],
        scratch_shapes=[pltpu.VMEM((ext, FB), F32), pltpu.VMEM((ext, FB), F32)],
        compiler_params=_cp(("parallel", "arbitrary")))(p, p, p, dmix, dmix, cw, cb)


def _gmlp_fwd(uv, wm, bst, gv, seq, name):
    n = uv.shape[0]
    tm = min(ROW_BLOCK, seq)

    def body(x_ref, w_ref, b_ref, g_ref, o_ref):
        ge_v = _gelu(x_ref[:, GM_W:])
        r = lax.rsqrt(jnp.mean(ge_v * ge_v, axis=-1, keepdims=True) + EPS)
        vn = (ge_v * r * g_ref[...]).astype(BF)
        for c in range(tm // CHUNK):
            rows = slice(c * CHUNK, (c + 1) * CHUNK)
            for h in range(GM_HEADS):
                cols = slice(h * CHUNK, (h + 1) * CHUNK)
                gate = jnp.dot(w_ref[h], vn[rows, cols], preferred_element_type=F32) + b_ref[:, h:h + 1]
                o_ref[rows, cols] = (_gelu(x_ref[rows, cols]) * gate).astype(BF)

    return pl.pallas_call(
        body, name=name, grid=(n // tm,),
        in_specs=[pl.BlockSpec((tm, 2 * GM_W), lambda i: (i, 0)),
                  pl.BlockSpec((GM_HEADS, CHUNK, CHUNK), lambda i: (0, 0, 0)),
                  pl.BlockSpec((CHUNK, GM_HEADS), lambda i: (0, 0)),
                  pl.BlockSpec((1, GM_W), lambda i: (0, 0))],
        out_specs=pl.BlockSpec((tm, GM_W), lambda i: (i, 0)),
        out_shape=jax.ShapeDtypeStruct((n, GM_W), BF), compiler_params=_cp(("parallel",)))(uv, wm, bst, gv)


def _gmlp_bwd(uv, dout, wm, wmt, bst, gv, seq, name):
    n = uv.shape[0]
    tm = min(ROW_BLOCK, seq)

    def body(x_ref, do_ref, w_ref, wt_ref, b_ref, g_ref, dx_ref, dw_ref, db_ref, dg_ref, dvn_scr):
        first = pl.program_id(0) == 0
        ge_v = _gelu(x_ref[:, GM_W:])
        r = lax.rsqrt(jnp.mean(ge_v * ge_v, axis=-1, keepdims=True) + EPS)
        vh = ge_v * r
        vn = (vh * g_ref[...]).astype(BF)
        tril = (lax.broadcasted_iota(jnp.int32, (CHUNK, CHUNK), 0)
                >= lax.broadcasted_iota(jnp.int32, (CHUNK, CHUNK), 1))
        for h in range(GM_HEADS):
            cols = slice(h * CHUNK, (h + 1) * CHUNK)
            dw = jnp.zeros((CHUNK, CHUNK), F32)
            dbs = jnp.zeros((CHUNK, 1), F32)
            for c in range(tm // CHUNK):
                rows = slice(c * CHUNK, (c + 1) * CHUNK)
                blk = vn[rows, cols]
                gate = jnp.dot(w_ref[h], blk, preferred_element_type=F32) + b_ref[:, h:h + 1]
                xu = x_ref[rows, cols]
                do = do_ref[rows, cols]
                dx_ref[rows, cols] = (do * gate * _gelu_grad(xu)).astype(BF)
                dgate = do * _gelu(xu)
                dgb = dgate.astype(BF)
                dw = dw + lax.dot_general(dgb, blk, _DIMS['nt'], preferred_element_type=F32)
                dbs = dbs + jnp.sum(dgate, axis=1, keepdims=True)
                dvn_scr[rows, cols] = jnp.dot(wt_ref[h], dgb, preferred_element_type=F32)
            _accumulate(dw_ref.at[h], first, jnp.where(tril, dw, 0.0))
            _accumulate(db_ref.at[h], first, dbs)
        dvn = dvn_scr[...]
        _accumulate(dg_ref, first, _colsum(dvn * vh))
        dvh = dvn * g_ref[...]
        dv = r * (dvh - vh * jnp.mean(dvh * vh, axis=-1, keepdims=True))
        dx_ref[:, GM_W:] = (dv * _gelu_grad(x_ref[:, GM_W:])).astype(BF)

    full3 = pl.BlockSpec((GM_HEADS, CHUNK, CHUNK), lambda i: (0, 0, 0))
    return pl.pallas_call(
        body, name=name, grid=(n // tm,),
        in_specs=[pl.BlockSpec((tm, 2 * GM_W), lambda i: (i, 0)), pl.BlockSpec((tm, GM_W), lambda i: (i, 0)),
                  full3, full3, pl.BlockSpec((CHUNK, GM_HEADS), lambda i: (0, 0)),
                  pl.BlockSpec((1, GM_W), lambda i: (0, 0))],
        out_specs=[pl.BlockSpec((tm, 2 * GM_W), lambda i: (i, 0)), full3,
                   pl.BlockSpec((GM_HEADS, CHUNK, 1), lambda i: (0, 0, 0)),
                   pl.BlockSpec((1, GM_W), lambda i: (0, 0))],
        out_shape=[jax.ShapeDtypeStruct((n, 2 * GM_W), BF), jax.ShapeDtypeStruct((GM_HEADS, CHUNK, CHUNK), F32),
                   jax.ShapeDtypeStruct((GM_HEADS, CHUNK, 1), F32), jax.ShapeDtypeStruct((1, GM_W), F32)],
        scratch_shapes=[pltpu.VMEM((tm, GM_W), F32)],
        compiler_params=_cp(("arbitrary",)))(uv, dout, wm, wmt, bst, gv)


def _s5_disc(lam_re, lam_im, log_dt, b_re, b_im):
    lr = jnp.minimum(lam_re, LAM_MAX)
    li = lam_im
    dt = jnp.exp(log_dt)
    mag = jnp.exp(lr * dt)
    ab_re = mag * jnp.cos(li * dt)
    ab_im = mag * jnp.sin(li * dt)
    den = lr * lr + li * li
    nr = ab_re - 1.0
    ni = ab_im
    z_re = (nr * lr + ni * li) / den
    z_im = (ni * lr - nr * li) / den
    return ab_re, ab_im, z_re * b_re - z_im * b_im, z_re * b_im + z_im * b_re


def _s5_disc_fwd(args, name):
    shp = jax.ShapeDtypeStruct(args[0].shape, F32)

    def body(*refs):
        outs = _s5_disc(*[r[...] for r in refs[:5]])
        for o_ref, o in zip(refs[5:], outs):
            o_ref[...] = o

    return pl.pallas_call(body, name=name, out_shape=[shp] * 4)(*args)


def _s5_disc_bwd(args, cts, name):
    shp = jax.ShapeDtypeStruct(args[0].shape, F32)

    def body(*refs):
        _, vjp = jax.vjp(_s5_disc, *[r[...] for r in refs[:5]])
        grads = vjp(tuple(r[...] for r in refs[5:9]))
        for o_ref, o in zip(refs[9:], grads):
            o_ref[...] = o

    return pl.pallas_call(body, name=name, out_shape=[shp] * 5)(*args, *cts)


def _cmul(a, b):
    return a[0] * b[0] - a[1] * b[1], a[0] * b[1] + a[1] * b[0]


def _scan_tables(ar, ai, reverse):
    if reverse:
        ai = -ai
    a1 = (ar, ai)
    a2 = _cmul(a1, a1)
    a3 = _cmul(a2, a1)
    a4 = _cmul(a2, a2)
    powers = [a1, a2, a3, a4, _cmul(a4, a1), _cmul(a4, a2), _cmul(a4, a3), _cmul(a4, a4)]
    row = lax.broadcasted_iota(jnp.int32, (8, NST), 0)
    zero = jnp.zeros((8, NST), F32)
    pr, pi = zero, zero
    for r in range(8):
        pw = powers[7 - r] if reverse else powers[r]
        pr = jnp.where(row == r, pw[0], pr)
        pi = jnp.where(row == r, pw[1], pi)
    levels = []
    for d, pw in ((1, a1), (2, a2), (4, a4)):
        ok = (row <= 7 - d) if reverse else (row >= d)
        levels.append((d, jnp.where(ok, pw[0], zero), jnp.where(ok, pw[1], zero)))
    return (pr, pi), levels


def _scan_block(src, dst, car, tables, n_tiles, reverse):
    (pr, pi), levels = tables
    row = lax.broadcasted_iota(jnp.int32, (8, NST), 0)
    out_row = 0 if reverse else 7

    def step(t, carry):
        cr, ci = carry
        tile = (n_tiles - 1 - t) if reverse else t
        rows = pl.ds(pl.multiple_of(tile * 8, 8), 8)
        xr = src[rows, 0:NST]
        xi = src[rows, NST:2 * NST]
        for d, dr, di in levels:
            shift = 8 - d if reverse else d
            rr = pltpu.roll(xr, shift, 0)
            ri = pltpu.roll(xi, shift, 0)
            xr, xi = xr + dr * rr - di * ri, xi + dr * ri + di * rr
        hr = xr + pr * cr - pi * ci
        hi = xi + pr * ci + pi * cr
        dst[rows, 0:NST] = hr
        dst[rows, NST:2 * NST] = hi
        return (_colsum(jnp.where(row == out_row, hr, 0.0)), _colsum(jnp.where(row == out_row, hi, 0.0)))

    cr, ci = lax.fori_loop(0, n_tiles, step, (car[0:1, 0:NST], car[0:1, NST:2 * NST]))
    car[0:1, 0:NST] = cr
    car[0:1, NST:2 * NST] = ci


def _s5_fwd(u, ab, bbt, cmat, dvec, wglu, bglu, seq, name):
    n = u.shape[0]
    tm = min(ROW_BLOCK, seq)

    def body(u_ref, ab_ref, bb_ref, c_ref, d_ref, w_ref, b_ref, h_ref, o_ref, xs, car):
        i = pl.program_id(0)

        @pl.when(lax.rem(i * tm, seq) == 0)
        def _():
            car[...] = jnp.zeros(car.shape, F32)

        uv = u_ref[...]
        xs[...] = jnp.dot(uv.astype(BF), bb_ref[...], preferred_element_type=F32)
        tables = _scan_tables(ab_ref[0:1, 0:NST], ab_ref[0:1, NST:2 * NST], False)
        _scan_block(xs, h_ref, car, tables, tm // 8, False)
        y = jnp.dot(h_ref[...].astype(BF), c_ref[...], preferred_element_type=F32) + d_ref[...] * uv
        g1 = _gelu(y)
        z = jnp.dot(g1.astype(BF), w_ref[...], preferred_element_type=F32) + b_ref[...]
        o_ref[...] = (g1 * _sigmoid(z)).astype(BF)

    const = lambda shape: pl.BlockSpec(shape, lambda i: (0, 0))
    return pl.pallas_call(
        body, name=name, grid=(n // tm,),
        in_specs=[pl.BlockSpec((tm, SSM_W), lambda i: (i, 0)), const((1, 2 * NST)), const((SSM_W, 2 * NST)),
                  const((2 * NST, SSM_W)), const((1, SSM_W)), const((SSM_W, SSM_W)), const((1, SSM_W))],
        out_specs=[pl.BlockSpec((tm, 2 * NST), lambda i: (i, 0)), pl.BlockSpec((tm, SSM_W), lambda i: (i, 0))],
        out_shape=[jax.ShapeDtypeStruct((n, 2 * NST), F32), jax.ShapeDtypeStruct((n, SSM_W), BF)],
        scratch_shapes=[pltpu.VMEM((tm, 2 * NST), F32), pltpu.VMEM((8, 2 * NST), F32)],
        compiler_params=_cp(("arbitrary",)))(u, ab, bbt, cmat, dvec, wglu, bglu)


def _s5_bwd(da, u, hst, ab, bbt, cmat, dvec, wglu, bglu, seq, name):
    n = u.shape[0]
    tm = min(ROW_BLOCK, seq)
    nb = n // tm
    blk = lambda r: nb - 1 - r
    prev, _ = _halo_maps(tm, n)

    def body(da_ref, u_ref, h_ref, hp_ref, ab_ref, bb_ref, c_ref, d_ref, w_ref, b_ref,
             du_ref, dw_ref, dbg_ref, dd_ref, dc_ref, dbb_ref, dab_ref, gs, car):
        r = pl.program_id(0)
        i = blk(r)
        first = r == 0

        @pl.when(lax.rem((i + 1) * tm, seq) == 0)
        def _():
            car[...] = jnp.zeros(car.shape, F32)

        uv = u_ref[...]
        dav = da_ref[...]
        hb = h_ref[...]
        hb16 = hb.astype(BF)
        dvv = d_ref[...]
        y = jnp.dot(hb16, c_ref[...], preferred_element_type=F32) + dvv * uv
        g1 = _gelu(y)
        g16 = g1.astype(BF)
        s = _sigmoid(jnp.dot(g16, w_ref[...], preferred_element_type=F32) + b_ref[...])
        dz = dav * g1 * s * (1.0 - s)
        dz16 = dz.astype(BF)
        dg1 = dav * s + lax.dot_general(dz16, w_ref[...], _DIMS['nt'], preferred_element_type=F32)
        _accumulate(dw_ref, first, lax.dot_general(g16, dz16, _DIMS['tn'], preferred_element_type=F32))
        _accumulate(dbg_ref, first, _colsum(dz))
        dy = dg1 * _gelu_grad(y)
        dy16 = dy.astype(BF)
        _accumulate(dd_ref, first, _colsum(dy * uv))
        _accumulate(dc_ref, first, lax.dot_general(hb16, dy16, _DIMS['tn'], preferred_element_type=F32))
        gs[...] = lax.dot_general(dy16, c_ref[...], _DIMS['nt'], preferred_element_type=F32)
        tables = _scan_tables(ab_ref[0:1, 0:NST], ab_ref[0:1, NST:2 * NST], True)
        _scan_block(gs, gs, car, tables, tm // 8, True)
        g = gs[...]
        g16b = g.astype(BF)
        sp = jnp.where(lax.rem(i * tm, seq) == 0, 0.0, 1.0)
        row = lax.broadcasted_iota(jnp.int32, hb.shape, 0)
        hprev = jnp.where(row == 0, hp_ref[7:8, :] * sp, pltpu.roll(hb, 1, 0))
        gr, gi = g[:, :NST], g[:, NST:]
        hr, hi = hprev[:, :NST], hprev[:, NST:]
        _accumulate(dab_ref.at[:, 0:NST], first, _colsum(gr * hr + gi * hi))
        _accumulate(dab_ref.at[:, NST:2 * NST], first, _colsum(gi * hr - gr * hi))
        _accumulate(dbb_ref, first, lax.dot_general(uv.astype(BF), g16b, _DIMS['tn'], preferred_element_type=F32))
        du = dy * dvv + lax.dot_general(g16b, bb_ref[...], _DIMS['nt'], preferred_element_type=F32)
        du_ref[...] = du.astype(BF)

    const = lambda shape: pl.BlockSpec(shape, lambda r: (0, 0))
    rowspec = lambda w: pl.BlockSpec((tm, w), lambda r: (blk(r), 0))
    return pl.pallas_call(
        body, name=name, grid=(nb,),
        in_specs=[rowspec(SSM_W), rowspec(SSM_W), rowspec(2 * NST),
                  pl.BlockSpec((8, 2 * NST), lambda r: (prev(blk(r)), 0)),
                  const((1, 2 * NST)), const((SSM_W, 2 * NST)), const((2 * NST, SSM_W)), const((1, SSM_W)),
                  const((SSM_W, SSM_W)), const((1, SSM_W))],
        out_specs=[rowspec(SSM_W), const((SSM_W, SSM_W)), const((1, SSM_W)), const((1, SSM_W)),
                   const((2 * NST, SSM_W)), const((SSM_W, 2 * NST)), const((1, 2 * NST))],
        out_shape=[jax.ShapeDtypeStruct((n, SSM_W), BF), jax.ShapeDtypeStruct((SSM_W, SSM_W), F32),
                   jax.ShapeDtypeStruct((1, SSM_W), F32), jax.ShapeDtypeStruct((1, SSM_W), F32),
                   jax.ShapeDtypeStruct((2 * NST, SSM_W), F32), jax.ShapeDtypeStruct((SSM_W, 2 * NST), F32),
                   jax.ShapeDtypeStruct((1, 2 * NST), F32)],
        scratch_shapes=[pltpu.VMEM((tm, 2 * NST), F32), pltpu.VMEM((8, 2 * NST), F32)],
        compiler_params=_cp(("arbitrary",)))(da, u, hst, hst, ab, bbt, cmat, dvec, wglu, bglu)


def _s5_rows(lam_re, lam_im, log_dt, b_re, b_im):
    rep = lambda a: jnp.broadcast_to(a[:, None, :], (SSM_G, SSM_H, SSM_P)).reshape(SSM_W, SSM_P)
    dt = jnp.broadcast_to(log_dt[:, None, None], (SSM_G, SSM_H, SSM_P)).reshape(SSM_W, SSM_P)
    tr = lambda b: b.transpose(0, 2, 1).reshape(SSM_W, SSM_P)
    return rep(lam_re), rep(lam_im), dt, tr(b_re), tr(b_im)


def _block_diag(rows_gp, inner):
    eye = jnp.eye(SSM_G, dtype=rows_gp.dtype)
    return (rows_gp[:, :, None, :] * eye[:, None, :, None]).reshape(SSM_G * inner, SSM_G * SSM_P)


def _diag_blocks(mat, inner):
    m4 = mat.reshape(SSM_G, inner, SSM_G, SSM_P)
    return jnp.stack([m4[g, :, g, :] for g in range(SSM_G)])


def _interleave(w, parts):
    lead = w.shape[:-1]
    nb = w.shape[-1] // (parts * FB)
    return jnp.swapaxes(w.reshape(lead + (parts, nb, FB)), -3, -2).reshape(w.shape)


def _deinterleave(w, parts):
    lead = w.shape[:-1]
    nb = w.shape[-1] // (parts * FB)
    return jnp.swapaxes(w.reshape(lead + (nb, parts, FB)), -3, -2).reshape(w.shape)


def _ffn_fwd(h, f, w_up, w_down, cw, cb, seq, tag, tail):
    n = h.shape[0]
    tm = min(2048, n)
    ni = n // tm
    up = _matmul_spec(
        f, w_up, 'nn', (ni, NDEV, 1),
        pl.BlockSpec((tm, D), lambda i, s, k: (i, 0)),
        pl.BlockSpec((D, FSH), lambda i, s, k: (s, 0)),
        pl.BlockSpec((tm, FSH), lambda i, s, k: (s * ni + i, 0)), (NDEV * n, FSH), f"{tag}_up", out_dtype=BF)
    up = up.reshape(2, 4, n, FSH)
    act, dgate = _ffn_conv_fwd(up, cw, cb, seq, f"{tag}_conv")
    out = _matmul_shards(act, w_down.reshape(4, FSH, D), 'nn', 512, D, f"{tag}_down", resid=h, tail=tail)
    return out, (f, up, act, dgate)


def _ffn_bwd(dh, dhb, h, g, w_up, w_down, cw, cb, saved, seq, tag):
    f, up, act, dgate = saved
    n = h.shape[0]
    tm = min(2048, n)
    ni = n // tm
    tk = min(2048, n)
    dact = _matmul_spec(
        dhb, w_down, 'nt', (ni, 4, 1),
        pl.BlockSpec((tm, D), lambda i, j, k: (i, 0)),
        pl.BlockSpec((FSH, D), lambda i, j, k: (j, 0)),
        pl.BlockSpec((tm, FSH), lambda i, j, k: (j * ni + i, 0)), (4 * n, FSH), f"{tag}_ddown_x", out_dtype=BF)
    dw_down = _matmul_tn_shards(act, dhb, 2, tk, f"{tag}_ddown_w")
    dup, dcw, dcb = _ffn_conv_bwd(up, dgate, dact.reshape(4, n, FSH), cw, seq, f"{tag}_dconv")
    dh_in, dhb_in, dg = _matmul_shards(dup.reshape(NDEV, n, FSH), w_up.reshape(NDEV, D, FSH), 'nt', 256, D,
                                       f"{tag}_dup_x", tail=('norm_bwd', h, g, dh))
    dw_up = _matmul_tn_shards(f, dup.reshape(NDEV, n, FSH), 2, tk, f"{tag}_dup_w")
    grads = dict(g=dg, w_up=dw_up, w_down=dw_down.reshape(NDEV, DFF // NDEV, D),
                 cw=dcw.reshape(NDEV, 3, FSH), cb=dcb.reshape(2 * DFF))
    return dh_in, dhb_in, grads


def _col_shards(w, width):
    return w.reshape(w.shape[0], NDEV, width).transpose(1, 0, 2)


def _local_step(x, tgt, w, gw, wait_ffn0, wait_rest, token, scatter, seq):
    bf = lambda a: a.astype(BF)
    row = lambda a: a.reshape(1, -1).astype(F32)
    w_ev = gw['ev_w_in'].transpose(1, 0, 2).reshape(D, 1792)
    w_ev_s5, w_ev_gm = w_ev[:, :SSM_W], w_ev[:, SSM_W:]
    w_evo = gw['ev_w_out'].reshape(D, D)
    f_cb = [w['ffn_conv_b'][l].reshape(2, 4, 1, FSH) for l in range(2)]
    tril = jnp.tril(jnp.ones((CHUNK, CHUNK), dtype=bool))
    gm_w = jnp.where(tril, w['gm_w_s'][0], 0.0)
    gm_wm, gm_wmt = bf(gm_w), bf(jnp.swapaxes(gm_w, 1, 2))
    gm_bt = w['gm_b_s'][0].T
    gm_gv = row(w['gm_v_g'][0])
    s5_in = _s5_rows(w['s5_lam_re'][0], w['s5_lam_im'][0], w['s5_log_dt'][0], w['s5_b_re'][0], w['s5_b_im'][0])
    ab_re, ab_im, bb_re, bb_im = _s5_disc_fwd(s5_in, "s5_disc")
    first_h = lambda a: a.reshape(SSM_G, SSM_H, SSM_P)[:, 0, :].reshape(1, NST)
    s5_ab = jnp.concatenate([first_h(ab_re), first_h(ab_im)], axis=1)
    to_gp = lambda a: a.reshape(SSM_G, SSM_H, SSM_P)
    s5_bbt = bf(jnp.concatenate([_block_diag(to_gp(bb_re), SSM_H), _block_diag(to_gp(bb_im), SSM_H)], axis=1))
    s5_cmat = bf(jnp.concatenate([_block_diag(w['s5_c_re'][0], SSM_H).T, -_block_diag(w['s5_c_im'][0], SSM_H).T],
                                 axis=0))
    s5_d, s5_bg, s5_wg = row(w['s5_d'][0]), row(w['s5_b_glu'][0]), gw['s5_w_glu'].reshape(SSM_W, SSM_W)
    g_mix = [row(w['mix_norm_g'][0]) + token[0:1, 0:1], row(w['mix_norm_g'][1])]
    g_ffn = [row(w['ffn_norm_g'][l]) for l in range(2)]
    g_fin = row(w['final_norm_g'])

    h0 = x
    y0 = _rmsnorm_fwd(h0, g_mix[0], "ev_norm")
    p_s5 = _matmul(y0, w_ev_s5, 'nn', 1024, 256, D, "ev_in_s5")
    p_gm = _matmul(y0, w_ev_gm, 'nn', 1024, 2 * GM_W, D, "ev_in_gm")
    hst, a_out = _s5_fwd(p_s5, s5_ab, s5_bbt, s5_cmat, s5_d, s5_wg, s5_bg, seq, "s5_fwd")
    b_out = _gmlp_fwd(p_gm, gm_wm, gm_bt, gm_gv, seq, "gmlp_fwd")
    mixcat = jnp.concatenate([a_out, b_out], axis=1)
    h1, f0 = _matmul(mixcat, w_evo, 'nn', 1024, D, D, "ev_out", resid=h0, tail=('norm_fwd', g_ffn[0]))
    g0 = wait_ffn0(mixcat)
    w_up0, w_dn0 = g0['ffn_w_up0'].reshape(NDEV * D, FSH), g0['ffn_w_down0'].reshape(DFF, D)
    f_cw0 = g0['ffn_conv_w0'].reshape(2, 4, 3, FSH)
    (h2, y1), ffn0 = _ffn_fwd(h1, f0, w_up0, w_dn0, f_cw0, f_cb[0], seq, "ffn0", ('norm_fwd', g_mix[1]))
    g1 = wait_rest(h2)
    w_od = _interleave(g1['od_w_in'].transpose(1, 0, 2).reshape(D, 3 * D), 3)
    w_odo = g1['od_w_out'].reshape(D, D)
    od_cw = g1['od_conv_w'].transpose(1, 0, 2).reshape(3, D)
    od_cb = g1['od_conv_b'].reshape(1, D)
    w_up1, w_dn1 = g1['ffn_w_up1'].reshape(NDEV * D, FSH), g1['ffn_w_down1'].reshape(DFF, D)
    f_cw1 = g1['ffn_conv_w1'].reshape(2, 4, 3, FSH)
    p_od = _matmul(y1, w_od, 'nn', 1024, 3 * D // 2, D, "od_in")
    mixin = _shortconv_fwd(p_od, od_cw, od_cb, seq, "od_conv")
    h3, f1 = _matmul(mixin, w_odo, 'nn', 1024, D, D, "od_out", resid=h2, tail=('norm_fwd', g_ffn[1]))
    (loss, dh4, dh4b, dg_fin), ffn1 = _ffn_fwd(h3, f1, w_up1, w_dn1, f_cw1, f_cb[1], seq, "ffn1",
                                                ('loss', g_fin, tgt))

    dh3, dh3b, gf1 = _ffn_bwd(dh4, dh4b, h3, g_ffn[1], w_up1, w_dn1, f_cw1, f_cb[1], ffn1, seq, "ffn1")
    dmixin = _matmul(dh3b, w_odo, 'nt', 1024, D, D, "od_dout_x")
    dw_odo = _matmul(mixin, dh3b, 'tn', D, 512, 4096, "od_dout_w", out_dtype=BF)
    dp_od, d_od_cw, d_od_cb = _shortconv_bwd(p_od, dmixin, od_cw, od_cb, seq, "od_dconv")
    dw_od = _matmul(y1, dp_od, 'tn', D, 3 * D // 2, 2048, "od_din_w", out_dtype=BF)
    sent = scatter("scatter_layer1", {
        'od_w_in': _col_shards(_deinterleave(dw_od, 3), 384), 'od_conv_w': _col_shards(d_od_cw, D // NDEV),
        'od_conv_b': d_od_cb.reshape(NDEV, 1, D // NDEV), 'od_w_out': dw_odo.reshape(NDEV, D // NDEV, D),
        'ffn_w_up1': gf1['w_up'], 'ffn_conv_w1': gf1['cw'], 'ffn_w_down1': gf1['w_down']})
    dh2, dh2b, dg_mix1 = _matmul(dp_od, w_od, 'nt', 512, D, 3 * D, "od_din_x",
                                 tail=('norm_bwd', h2, g_mix[1] + sent[0:1, 0:1], dh3))
    dh1, dh1b, gf0 = _ffn_bwd(dh2, dh2b, h1, g_ffn[0], w_up0, w_dn0, f_cw0, f_cb[0], ffn0, seq, "ffn0")
    dmix_a = _matmul(dh1b, w_evo[:SSM_W], 'nt', 1024, SSM_W, D, "ev_dout_xa")
    dmix_b = _matmul(dh1b, w_evo[SSM_W:], 'nt', 1024, GM_W, D, "ev_dout_xb")
    dw_evo = _matmul(mixcat, dh1b, 'tn', D, 512, 4096, "ev_dout_w", out_dtype=BF)
    sent = scatter("scatter_ffn0", {'ffn_w_up0': gf0['w_up'], 'ffn_conv_w0': gf0['cw'], 'ffn_w_down0': gf0['w_down'],
                                    'ev_w_out': dw_evo.reshape(NDEV, D // NDEV, D)})
    dp_s5, d_wg, d_bg, d_d, d_cmat, d_bbt, d_ab = _s5_bwd(dmix_a, p_s5, hst, s5_ab, s5_bbt, s5_cmat,
                                                           s5_d + sent[0:1, 0:1], s5_wg, s5_bg, seq, "s5_bwd")
    dp_gm, d_gmw, d_gmb, d_gmg = _gmlp_bwd(p_gm, dmix_b, gm_wm, gm_wmt, gm_bt, gm_gv, seq, "gmlp_bwd")
    dw_ev = jnp.concatenate([_matmul(y0, dp_s5, 'tn', D, SSM_W, 4096, "ev_din_wa", out_dtype=BF),
                             _matmul(y0, dp_gm, 'tn', D, 2 * GM_W, 2048, "ev_din_wb", out_dtype=BF)], axis=1)
    sent = scatter("scatter_even", {'ev_w_in': _col_shards(dw_ev, 224),
                                    's5_w_glu': d_wg.reshape(NDEV, SSM_W // NDEV, SSM_W)})
    dy0 = _matmul(dp_gm, w_ev_gm, 'nt', 512, D, 2 * GM_W, "ev_din_xb")
    grad_x, _, dg_mix0 = _matmul(dp_s5, w_ev_s5, 'nt', 512, D, SSM_W, "ev_din_xa", resid=dy0,
                                 tail=('norm_bwd', h0, g_mix[0] + sent[0:1, 0:1], dh1))

    put_h0 = lambda a: jnp.zeros((SSM_G, SSM_H, SSM_P), F32).at[:, 0, :].set(a.reshape(SSM_G, SSM_P)).reshape(
        SSM_W, SSM_P)
    ct = (put_h0(d_ab[:, :NST]), put_h0(d_ab[:, NST:]),
          _diag_blocks(d_bbt[:, :NST], SSM_H).reshape(SSM_W, SSM_P),
          _diag_blocks(d_bbt[:, NST:], SSM_H).reshape(SSM_W, SSM_P))
    d_lre, d_lim, d_ldt, d_bre, d_bim = _s5_disc_bwd(s5_in, ct, "s5_ddisc")
    over_h = lambda a: a.reshape(SSM_G, SSM_H, SSM_P).sum(axis=1)
    un_tr = lambda a: a.reshape(SSM_G, SSM_H, SSM_P).transpose(0, 2, 1)
    d_cre = _diag_blocks(d_cmat[:NST].T, SSM_H)
    d_cim = -_diag_blocks(d_cmat[NST:].T, SSM_H)

    repl = {
        'mix_norm_g': jnp.concatenate([dg_mix0, dg_mix1], axis=0),
        'ffn_norm_g': jnp.concatenate([gf0['g'], gf1['g']], axis=0),
        'final_norm_g': dg_fin.reshape(D),
        's5_lam_re': over_h(d_lre)[None], 's5_lam_im': over_h(d_lim)[None],
        's5_log_dt': over_h(d_ldt).sum(axis=1)[None],
        's5_b_re': un_tr(d_bre)[None], 's5_b_im': un_tr(d_bim)[None],
        's5_c_re': d_cre[None], 's5_c_im': d_cim[None],
        's5_d': d_d, 's5_b_glu': d_bg,
        'gm_w_s': d_gmw[None], 'gm_b_s': d_gmb.reshape(1, GM_HEADS, CHUNK), 'gm_v_g': d_gmg,
        'ffn_conv_b': jnp.stack([gf0['cb'], gf1['cb']]),
    }
    return loss, grad_x, repl


HBM_SPEC = pl.BlockSpec(memory_space=pltpu.HBM)


def _at_axis(ref, pos, index):
    return ref.at[(slice(None),) * pos + (index,)]


def _all_gather(shards, positions, name):
    n = len(shards)

    def body(*refs):
        xs, outs = refs[:n], refs[n:2 * n]
        send_sems, recv_sems, local_sems = refs[2 * n:]
        x, y, c = lax.axis_index("x"), lax.axis_index("y"), lax.axis_index("c")
        me, sibling = (x, y, c), (x, y, 1 - c)
        chips = [(1 - x, y), (x, 1 - y), (1 - x, 1 - y)]

        def block(p, dev):
            return _at_axis(outs[p], positions[p], 4 * dev[0] + 2 * dev[1] + dev[2])

        def copy(p, k, dev, to, src=None):
            return pltpu.make_async_remote_copy(
                src_ref=block(p, dev) if src is None else src, dst_ref=block(p, dev),
                send_sem=send_sems.at[p, k], recv_sem=recv_sems.at[p, k], device_id=to, device_id_type=MESH_T)

        mine = [pltpu.make_async_copy(xs[p], block(p, me), local_sems.at[p]) for p in range(n)]
        for cp in mine:
            cp.start()
        first = [copy(p, 0, me, sibling, src=xs[p]) for p in range(n)]
        first += [copy(p, 1 + j, me, (*chip, c), src=xs[p]) for j, chip in enumerate(chips) for p in range(n)]
        for cp in first:
            cp.start()
        passed = []
        for j, chip in enumerate(chips):
            for p in range(n):
                copy(p, 1 + j, (*chip, c), me).wait_recv()
                fwd = copy(p, 4 + j, (*chip, c), sibling)
                fwd.start()
                passed.append(fwd)
        for p in range(n):
            copy(p, 0, sibling, me).wait_recv()
        for j, chip in enumerate(chips):
            for p in range(n):
                copy(p, 4 + j, (*chip, 1 - c), me).wait_recv()
        for cp in first + passed:
            cp.wait_send()
        for cp in mine:
            cp.wait()

    out_shape = [jax.ShapeDtypeStruct(s.shape[:pos] + (NDEV,) + s.shape[pos:], s.dtype)
                 for s, pos in zip(shards, positions)]
    return pl.pallas_call(
        body, name=name, out_shape=out_shape, in_specs=[HBM_SPEC] * n, out_specs=[HBM_SPEC] * n,
        scratch_shapes=[pltpu.SemaphoreType.DMA((n, 7)), pltpu.SemaphoreType.DMA((n, 7)),
                        pltpu.SemaphoreType.DMA((n,))])(*shards)


def _other_devices(x, y, c):
    flip = lambda v, bit: 1 - v if bit else v
    return [(flip(x, k >> 2 & 1), flip(y, k >> 1 & 1), flip(c, k & 1)) for k in range(1, NDEV)]


SEM_SPEC = pl.BlockSpec(memory_space=pltpu.SEMAPHORE)
START_EFFECT = pltpu.SideEffectType.DATAFLOW_SIDE_EFFECTING


def _send_start(arrays, scatter, name):
    n = len(arrays)
    lands = [lax.empty((NDEV,) + (a.shape[1:] if scatter else a.shape), a.dtype) for a in arrays]

    def body(*refs):
        xs, ls = refs[:n], refs[n:2 * n]
        send_sems, recv_sems, own_sems, token = refs[2 * n], refs[2 * n + 1], refs[2 * n + 2], refs[4 * n + 3]
        x, y, c = lax.axis_index("x"), lax.axis_index("y"), lax.axis_index("c")
        me = 4 * x + 2 * y + c
        for k, peer in enumerate(_other_devices(x, y, c)):
            for p in range(n):
                src = xs[p].at[4 * peer[0] + 2 * peer[1] + peer[2]] if scatter else xs[p]
                pltpu.make_async_remote_copy(
                    src_ref=src, dst_ref=ls[p].at[me], send_sem=send_sems.at[p * (NDEV - 1) + k],
                    recv_sem=recv_sems.at[p * (NDEV - 1) + k], device_id=peer, device_id_type=MESH_T).start()
        for p in range(n):
            pltpu.make_async_copy(xs[p].at[me] if scatter else xs[p], ls[p].at[me], own_sems.at[p]).start()
        token[...] = jnp.zeros(token.shape, F32)

    sems = pltpu.SemaphoreType.DMA((n * (NDEV - 1),))
    out_shape = ([sems, sems, pltpu.SemaphoreType.DMA((n,))]
                 + [pltpu.HBM(a.shape, a.dtype) for a in list(arrays) + lands] + [jax.ShapeDtypeStruct((8, 128), F32)])
    res = pl.pallas_call(
        body, name=name, out_shape=out_shape, in_specs=[HBM_SPEC] * (2 * n),
        out_specs=[SEM_SPEC] * 3 + [HBM_SPEC] * (2 * n) + [pl.BlockSpec(memory_space=pltpu.VMEM)],
        input_output_aliases={i: 3 + i for i in range(2 * n)},
        compiler_params=pltpu.CompilerParams(has_side_effects=START_EFFECT))(
            *[pltpu.with_memory_space_constraint(a, pltpu.HBM) for a in list(arrays) + lands])
    return res[:3], res[3:3 + n], res[3 + n:3 + 2 * n], res[3 + 2 * n]


def _send_wait(started, scatter, after, name):
    sems, arrays, lands, _ = started
    n = len(arrays)

    def body(*refs):
        xs, ls = refs[:n], refs[n:2 * n]
        send, recv, own = refs[2 * n:2 * n + 3]
        x, y, c = lax.axis_index("x"), lax.axis_index("y"), lax.axis_index("c")
        me = 4 * x + 2 * y + c
        for p in range(n):
            pltpu.make_async_copy(xs[p].at[me] if scatter else xs[p], ls[p].at[me], own.at[p]).wait()
        for k, peer in enumerate(_other_devices(x, y, c)):
            slot = 4 * peer[0] + 2 * peer[1] + peer[2]
            for p in range(n):
                cp = pltpu.make_async_remote_copy(
                    src_ref=xs[p].at[slot] if scatter else xs[p], dst_ref=ls[p].at[slot],
                    send_sem=send.at[p * (NDEV - 1) + k], recv_sem=recv.at[p * (NDEV - 1) + k], device_id=peer,
                    device_id_type=MESH_T)
                cp.wait_send()
                cp.wait_recv()

    res = pl.pallas_call(
        body, name=name, out_shape=[pltpu.HBM(a.shape, a.dtype) for a in list(arrays) + list(lands)],
        in_specs=[HBM_SPEC] * (2 * n) + [SEM_SPEC] * 3 + [pl.BlockSpec(memory_space=pl.ANY)],
        out_specs=[HBM_SPEC] * (2 * n), input_output_aliases={i: i for i in range(2 * n)},
        compiler_params=pltpu.CompilerParams(has_side_effects=START_EFFECT))(
            *arrays, *lands, *sems, after)
    return res[n:]


def _row_block(rows, cols, itemsize=4, target=2**20):
    best = None
    for tr in range(16, rows + 1, 16):
        if rows % tr == 0 and tr * cols * itemsize <= target:
            best = tr
    return best or rows


def _adamw(w, m, v, gparts, name):
    parts, rows, cols = gparts.shape
    tr = _row_block(rows, cols, target=2**19)
    bc1 = 1.0 - ADAM_B1 ** ADAM_STEP
    bc2 = 1.0 - ADAM_B2 ** ADAM_STEP

    def body(w_ref, m_ref, v_ref, g_ref, go_ref, d_ref, mo_ref, vo_ref):
        g = g_ref[0].astype(F32)
        for k in range(1, parts):
            g = g + g_ref[k].astype(F32)
        mn = ADAM_B1 * m_ref[...] + (1.0 - ADAM_B1) * g
        vn = ADAM_B2 * v_ref[...] + (1.0 - ADAM_B2) * (g * g)
        go_ref[...] = g
        mo_ref[...] = mn
        vo_ref[...] = vn
        d_ref[...] = -ADAM_LR * ((mn / bc1) / (jnp.sqrt(vn / bc2) + ADAM_EPS) + ADAM_WD * w_ref[...])

    blk = pl.BlockSpec((tr, cols), lambda i: (i, 0))
    shp = jax.ShapeDtypeStruct((rows, cols), F32)
    return pl.pallas_call(
        body, name=name, grid=(rows // tr,),
        in_specs=[blk, blk, blk, pl.BlockSpec((parts, tr, cols), lambda i: (0, i, 0))],
        out_specs=[blk] * 4, out_shape=[shp] * 4, compiler_params=_cp(("parallel",)))(w, m, v, gparts)


def _pack(arrays, rows):
    flat = jnp.concatenate([a.reshape(-1).astype(F32) for a in arrays])
    return jnp.pad(flat, (0, rows * PACK_COLS - flat.shape[0])).reshape(rows, PACK_COLS)


def _unpack(buf, shapes):
    flat = buf.reshape(-1)
    out, off = [], 0
    for shp in shapes:
        size = int(np.prod(shp))
        out.append(flat[off:off + size].reshape(shp))
        off += size
    return out


REPL_SHAPES = {'mix_norm_g': (2, 1024), 'ffn_norm_g': (2, 1024), 'final_norm_g': (1024,), 's5_lam_re': (1, 16, 64),
               's5_lam_im': (1, 16, 64), 's5_log_dt': (1, 16), 's5_b_re': (1, 16, 64, 16), 's5_b_im': (1, 16, 64, 16),
               's5_c_re': (1, 16, 16, 64), 's5_c_im': (1, 16, 16, 64), 's5_d': (1, 256), 's5_b_glu': (1, 256),
               'gm_w_s': (1, 6, 128, 128), 'gm_b_s': (1, 6, 128), 'gm_v_g': (1, 768), 'ffn_conv_b': (2, 5632)}
REPL_ELEMS = sum(int(np.prod(REPL_SHAPES[n])) for n in REPL_ORDER)
REPL_ROWS = -(-REPL_ELEMS // (PACK_COLS * 8)) * 8

GATHER_DTYPE = {'ev_w_in': BF, 'ev_w_out': BF, 's5_w_glu': BF, 'od_w_in': BF, 'od_conv_w': F32, 'od_conv_b': F32,
                'od_w_out': BF, 'ffn_w_up': BF, 'ffn_conv_w': F32, 'ffn_w_down': BF}
GATHER_EVEN = ['ev_w_in', 'ev_w_out', 's5_w_glu']
GATHER_FFN0 = ['ffn_w_up0', 'ffn_conv_w0', 'ffn_w_down0']
GATHER_REST = ['od_w_in', 'od_conv_w', 'od_conv_b', 'od_w_out', 'ffn_w_up1', 'ffn_conv_w1', 'ffn_w_down1']

def _squeeze_lead(a):
    return a.reshape(a.shape[1:]) if a.shape[0] == 1 and a.ndim > 2 else a


def kernel(x, mix_norm_g, ffn_norm_g, final_norm_g, ev_w_in, ev_w_out, s5_lam_re, s5_lam_im, s5_log_dt, s5_b_re, s5_b_im, s5_c_re, s5_c_im, s5_d, s5_w_glu, s5_b_glu, gm_w_s, gm_b_s, gm_v_g, od_w_in, od_conv_w, od_conv_b, od_w_out, ffn_w_up, ffn_conv_w, ffn_conv_b, ffn_w_down, loss_target, m_mix_norm_g, m_ffn_norm_g, m_final_norm_g, m_ev_w_in, m_ev_w_out, m_s5_lam_re, m_s5_lam_im, m_s5_log_dt, m_s5_b_re, m_s5_b_im, m_s5_c_re, m_s5_c_im, m_s5_d, m_s5_w_glu, m_s5_b_glu, m_gm_w_s, m_gm_b_s, m_gm_v_g, m_od_w_in, m_od_conv_w, m_od_conv_b, m_od_w_out, m_ffn_w_up, m_ffn_conv_w, m_ffn_conv_b, m_ffn_w_down, v_mix_norm_g, v_ffn_norm_g, v_final_norm_g, v_ev_w_in, v_ev_w_out, v_s5_lam_re, v_s5_lam_im, v_s5_log_dt, v_s5_b_re, v_s5_b_im, v_s5_c_re, v_s5_c_im, v_s5_d, v_s5_w_glu, v_s5_b_glu, v_gm_w_s, v_gm_b_s, v_gm_v_g, v_od_w_in, v_od_conv_w, v_od_conv_b, v_od_w_out, v_ffn_w_up, v_ffn_conv_w, v_ffn_conv_b, v_ffn_w_down):
    given = dict(locals())
    weights = {n: given[n] for n in WEIGHT_ORDER}
    nseq, seq, _ = x.shape

    send = {}
    for name in SHARDED_ORDER:
        a = weights[name].astype(GATHER_DTYPE[name])
        if a.shape[0] == 2:
            send[name + '0'], send[name + '1'] = a[0], a[1]
        else:
            send[name] = _squeeze_lead(a)
    gathers = [_send_start([send[n] for n in names], False, f"gather_{tag}_start")
               for tag, names in (("ffn0", GATHER_FFN0), ("rest", GATHER_REST))]
    token = gathers[0][3] + gathers[1][3]

    def waiter(tag, names, started):
        return lambda after: dict(zip(names, _send_wait(started, False, after, f"gather_{tag}_wait")))

    gathered = dict(zip(GATHER_EVEN, _all_gather([send[n] for n in GATHER_EVEN], [0] * len(GATHER_EVEN),
                                                 "gather_even")))

    scatters = []

    def scatter(tag, grads):
        names = list(grads)
        started = _send_start([grads[n].astype(BF) for n in names], True, f"{tag}_start")
        scatters.append((tag, names, started))
        return started[3]

    loss_row, grad_x, g_repl = _local_step(
        x.reshape(nseq * seq, D), loss_target.reshape(nseq * seq, D), weights, gathered,
        waiter("ffn0", GATHER_FFN0, gathers[0]), waiter("rest", GATHER_REST, gathers[1]), token, scatter, seq)
    loss = lax.psum(loss_row[0, 0], ("x", "y", "c"))

    parts = {}
    for tag, names, started in scatters:
        parts.update(zip(names, _send_wait(started, True, grad_x, f"{tag}_wait")))
    repl_parts = _all_gather([_pack([g_repl[n] for n in REPL_ORDER], REPL_ROWS)], [0], "gather_small_grads")[0]

    out = {}
    for name in SHARDED_ORDER:
        w = weights[name]
        if name + '0' in parts:
            gp = jnp.stack([parts[name + '0'], parts[name + '1']], axis=1)
        else:
            gp = parts[name]
        to_rows = lambda a: a.reshape(-1, w.shape[-1])
        res = _adamw(to_rows(w), to_rows(given["m_" + name]), to_rows(given["v_" + name]),
                     gp.reshape(NDEV, -1, w.shape[-1]), f"adamw_{name}")
        out[name] = [r.reshape(w.shape) for r in res]
    rp = _adamw(_pack([weights[n] for n in REPL_ORDER], REPL_ROWS),
                _pack([given["m_" + n] for n in REPL_ORDER], REPL_ROWS),
                _pack([given["v_" + n] for n in REPL_ORDER], REPL_ROWS), repl_parts, "adamw_replicated")
    rp_shapes = [weights[n].shape for n in REPL_ORDER]
    for k in range(4):
        for name, a in zip(REPL_ORDER, _unpack(rp[k], rp_shapes)):
            out.setdefault(name, [None] * 4)[k] = a
    results = [[out[n][k] for n in WEIGHT_ORDER] for k in range(4)]
    grad_w, delta_w, new_m, new_v = results
    return (loss, grad_x.reshape(nseq, seq, D), *grad_w, *delta_w, *new_m, *new_v)
```

```python
import math

import jax
import jax.numpy as jnp
import numpy as np
from jax import lax
from jax.experimental import pallas as pl
from jax.experimental.pallas import tpu as pltpu

F32 = jnp.float32
BF = jnp.bfloat16

D = 1024
DFF = 2816
NDEV = 8
SSM_W = 256
SSM_G = 16
SSM_H = 16
SSM_P = 64
NST = SSM_G * SSM_P
GM_W = 768
GM_HEADS = 6
CHUNK = 128
EPS = 1e-6
LAM_MAX = -1e-4
FB = 256
FSH = 2 * DFF // NDEV
ROW_BLOCK = 512
CONV_ROW_BLOCK = 1024
VMEM_LIMIT = 48 * 2**20
PACK_COLS = 1024
MESH_T = pl.DeviceIdType.MESH

ADAM_LR = 0.001
ADAM_B1 = 0.9
ADAM_B2 = 0.999
ADAM_EPS = 1e-08
ADAM_WD = 0.01
ADAM_STEP = 10

WEIGHT_ORDER = ['mix_norm_g', 'ffn_norm_g', 'final_norm_g', 'ev_w_in', 'ev_w_out', 's5_lam_re', 's5_lam_im',
                's5_log_dt', 's5_b_re', 's5_b_im', 's5_c_re', 's5_c_im', 's5_d', 's5_w_glu', 's5_b_glu', 'gm_w_s',
                'gm_b_s', 'gm_v_g', 'od_w_in', 'od_conv_w', 'od_conv_b', 'od_w_out', 'ffn_w_up', 'ffn_conv_w',
                'ffn_conv_b', 'ffn_w_down']
SHARDED = {'ev_w_in': ((1, 1024, 1792), 2), 'ev_w_out': ((1, 1024, 1024), 1), 's5_w_glu': ((1, 256, 256), 1),
           'od_w_in': ((1, 1024, 3072), 2), 'od_conv_w': ((1, 3, 1024), 2), 'od_conv_b': ((1, 1024), 1),
           'od_w_out': ((1, 1024, 1024), 1), 'ffn_w_up': ((2, 1024, 5632), 2), 'ffn_conv_w': ((2, 3, 5632), 2),
           'ffn_w_down': ((2, 2816, 1024), 1)}
SHARDED_ORDER = [n for n in WEIGHT_ORDER if n in SHARDED]
REPL_ORDER = [n for n in WEIGHT_ORDER if n not in SHARDED]


def _cp(sem):
    return pltpu.CompilerParams(dimension_semantics=sem, vmem_limit_bytes=VMEM_LIMIT)


def _sigmoid(x):
    return 1.0 / (1.0 + jnp.exp(-x))


_GELU_K = math.sqrt(2.0 / math.pi)


def _gelu(x):
    return 0.5 * x * (1.0 + jnp.tanh(_GELU_K * (x + 0.044715 * x * x * x)))


def _gelu_grad(x):
    t = jnp.tanh(_GELU_K * (x + 0.044715 * x * x * x))
    return 0.5 * (1.0 + t) + 0.5 * x * (1.0 - t * t) * _GELU_K * (1.0 + 3.0 * 0.044715 * x * x)


def _colsum(x):
    return jnp.sum(x, axis=0, keepdims=True)


def _accumulate(ref, first, part):
    @pl.when(first)
    def _():
        ref[...] = part

    @pl.when(jnp.logical_not(first))
    def _():
        ref[...] += part


_DIMS = {'nn': (((1,), (0,)), ((), ())), 'nt': (((1,), (1,)), ((), ())), 'tn': (((0,), (0,)), ((), ()))}


def _rms(xv):
    r = lax.rsqrt(jnp.mean(xv * xv, axis=-1, keepdims=True) + EPS)
    return r, xv * r


def _norm_grad(dyv, gv, r, xh):
    dyg = dyv * gv
    return r * (dyg - xh * jnp.mean(dyg * xh, axis=-1, keepdims=True))


def _tail_io(tail, tm, index, n):
    rows = pl.BlockSpec((tm, D), index)
    vec = pl.BlockSpec((1, D), lambda *_: (0, 0))
    full, half, gain = (jax.ShapeDtypeStruct((n, D), F32), jax.ShapeDtypeStruct((n, D), BF),
                        jax.ShapeDtypeStruct((1, D), F32))
    if tail[0] == 'norm_fwd':
        return [tail[1]], [vec], [rows, rows], [full, half]
    if tail[0] == 'norm_bwd':
        return list(tail[1:]), [rows, vec, rows], [rows, rows, vec], [full, half, gain]
    return (list(tail[1:]), [vec, rows], [pl.BlockSpec((1, 128), lambda *_: (0, 0)), rows, rows, vec],
            [jax.ShapeDtypeStruct((1, 128), F32), full, half, gain])


def _tail_apply(kind, tot, tail_refs, outs, first):
    if kind == 'norm_fwd':
        r, xh = _rms(tot)
        outs[0][...] = tot
        outs[1][...] = (xh * tail_refs[0][...]).astype(BF)
    elif kind == 'norm_bwd':
        x_ref, g_ref, dr_ref = tail_refs
        r, xh = _rms(x_ref[...])
        dx = dr_ref[...] + _norm_grad(tot, g_ref[...], r, xh)
        outs[0][...] = dx
        outs[1][...] = dx.astype(BF)
        _accumulate(outs[2], first, _colsum(tot * xh))
    else:
        gv = tail_refs[0][...]
        r, xh = _rms(tot)
        err = xh * gv - tail_refs[1][...]
        part = 0.5 * jnp.sum(jnp.mean(err * err, axis=-1, keepdims=True), axis=0, keepdims=True)
        _accumulate(outs[0], first, jnp.broadcast_to(part, (1, 128)))
        dyv = err * (1.0 / D)
        dx = _norm_grad(dyv, gv, r, xh)
        outs[1][...] = dx
        outs[2][...] = dx.astype(BF)
        _accumulate(outs[3], first, _colsum(dyv * xh))


def _matmul(a, b, mode, tm, tn, tk, name, resid=None, out_dtype=F32, tail=None):
    if mode == 'tn':
        kdim, m = a.shape
    else:
        m, kdim = a.shape
    n = b.shape[0] if mode == 'nt' else b.shape[1]
    tm, tn, tk = min(tm, m), min(tn, n), min(tk, kdim)
    assert m % tm == 0 and n % tn == 0 and kdim % tk == 0, (name, m, n, kdim, tm, tn, tk)
    a_spec = (pl.BlockSpec((tk, tm), lambda i, j, k: (k, i)) if mode == 'tn'
              else pl.BlockSpec((tm, tk), lambda i, j, k: (i, k)))
    b_spec = (pl.BlockSpec((tn, tk), lambda i, j, k: (j, k)) if mode == 'nt'
              else pl.BlockSpec((tk, tn), lambda i, j, k: (k, j)))
    o_spec = pl.BlockSpec((tm, tn), lambda i, j, k: (i, j))
    return _matmul_spec(a, b, mode, (m // tm, n // tn, kdim // tk), a_spec, b_spec, o_spec, (m, n), name,
                        resid=resid, out_dtype=out_dtype, tail=tail)


def _matmul_spec(a, b, mode, grid, a_spec, b_spec, o_spec, out_shape, name, resid=None, out_dtype=F32, tail=None):
    nk = grid[2]
    tm, tn = o_spec.block_shape[-2:]
    dims = _DIMS[mode]
    has_resid = resid is not None
    operands = [a, b] + ([resid] if has_resid else [])
    in_specs = [a_spec, b_spec] + ([o_spec] if has_resid else [])
    out_specs, out_shapes = [o_spec], [jax.ShapeDtypeStruct(out_shape, out_dtype)]
    n_tail = 0
    if tail is not None:
        assert tn == D and grid[1] == 1, name
        extra, extra_specs, out_specs, out_shapes = _tail_io(tail, tm, lambda i, j, k: (i, 0), out_shape[0])
        operands, in_specs, n_tail = operands + extra, in_specs + extra_specs, len(extra)
    n_in, n_out = len(operands), len(out_specs)

    def body(*refs):
        ins, outs = refs[:n_in], refs[n_in:n_in + n_out]
        a_ref, b_ref = ins[:2]
        part = lax.dot_general(a_ref[...].astype(BF), b_ref[...].astype(BF), dims, preferred_element_type=F32)

        def finish(tot):
            if has_resid:
                tot = tot + ins[2][...]
            if tail is not None:
                _tail_apply(tail[0], tot, ins[n_in - n_tail:], outs, pl.program_id(0) == 0)
            else:
                outs[0][...] = tot.astype(out_dtype)

        if nk == 1:
            finish(part)
        else:
            acc = refs[-1]
            k = pl.program_id(2)

            @pl.when(k == 0)
            def _():
                acc[...] = part

            @pl.when(k > 0)
            def _():
                acc[...] += part

            @pl.when(k == nk - 1)
            def _():
                finish(acc[...])

    res = pl.pallas_call(
        body, name=name, grid=grid, in_specs=in_specs, out_specs=out_specs, out_shape=out_shapes,
        scratch_shapes=[pltpu.VMEM((tm, tn), F32)] if nk > 1 else [],
        compiler_params=_cp(("arbitrary",) * 3 if tail is not None else ("parallel", "parallel", "arbitrary")))(*operands)
    return res if tail is not None else res[0]


def _matmul_shards(a, b, mode, tm, tn, name, resid=None, out_dtype=F32, tail=None):
    shards, m, kdim = a.shape
    n = b.shape[2] if mode == 'nn' else b.shape[1]
    tm, tn = min(tm, m), min(tn, n)
    dims = _DIMS[mode]
    has_resid = resid is not None
    b_spec = (pl.BlockSpec((shards, kdim, tn), lambda i, j: (0, 0, j)) if mode == 'nn'
              else pl.BlockSpec((shards, tn, kdim), lambda i, j: (0, j, 0)))
    o_spec = pl.BlockSpec((tm, tn), lambda i, j: (i, j))
    operands = [a, b] + ([resid] if has_resid else [])
    in_specs = [pl.BlockSpec((shards, tm, kdim), lambda i, j: (0, i, 0)), b_spec] + ([o_spec] if has_resid else [])
    out_specs, out_shapes = [o_spec], [jax.ShapeDtypeStruct((m, n), out_dtype)]
    n_tail = 0
    if tail is not None:
        assert tn == D and n == D, name
        extra, extra_specs, out_specs, out_shapes = _tail_io(tail, tm, lambda i, j: (i, 0), m)
        operands, in_specs, n_tail = operands + extra, in_specs + extra_specs, len(extra)
    n_in, n_out = len(operands), len(out_specs)

    def body(*refs):
        ins, outs = refs[:n_in], refs[n_in:n_in + n_out]
        acc = lax.dot_general(ins[0][0], ins[1][0], dims, preferred_element_type=F32)
        for s in range(1, shards):
            acc = acc + lax.dot_general(ins[0][s], ins[1][s], dims, preferred_element_type=F32)
        if has_resid:
            acc = acc + ins[2][...]
        if tail is not None:
            _tail_apply(tail[0], acc, ins[n_in - n_tail:], outs, pl.program_id(0) == 0)
        else:
            outs[0][...] = acc.astype(out_dtype)

    res = pl.pallas_call(
        body, name=name, grid=(m // tm, n // tn), in_specs=in_specs, out_specs=out_specs, out_shape=out_shapes,
        compiler_params=_cp(("arbitrary", "arbitrary") if tail is not None else ("parallel", "parallel")))(*operands)
    return res if tail is not None else res[0]


def _matmul_tn_shards(a, b, sp, tk, name, out_dtype=BF):
    a_sh, b_sh = a.ndim == 3, b.ndim == 3
    shards = a.shape[0] if a_sh else b.shape[0]
    kdim, m, n = a.shape[-2], a.shape[-1], b.shape[-1]
    nk = kdim // tk

    def body(a_ref, b_ref, o_ref, acc):
        k = pl.program_id(1)
        parts = [lax.dot_general(a_ref[q] if a_sh else a_ref[...], b_ref[q] if b_sh else b_ref[...], _DIMS['tn'],
                                 preferred_element_type=F32) for q in range(sp)]

        @pl.when(k == 0)
        def _():
            for q in range(sp):
                acc[q] = parts[q]

        @pl.when(k > 0)
        def _():
            for q in range(sp):
                acc[q] += parts[q]

        @pl.when(k == nk - 1)
        def _():
            o_ref[...] = acc[...].astype(out_dtype)

    spec = lambda sharded, cols: (pl.BlockSpec((sp, tk, cols), lambda s, k: (s, k, 0)) if sharded
                                  else pl.BlockSpec((tk, cols), lambda s, k: (k, 0)))
    return pl.pallas_call(
        body, name=name, grid=(shards // sp, nk), in_specs=[spec(a_sh, m), spec(b_sh, n)],
        out_specs=pl.BlockSpec((sp, m, n), lambda s, k: (s, 0, 0)),
        out_shape=jax.ShapeDtypeStruct((shards, m, n), out_dtype),
        scratch_shapes=[pltpu.VMEM((sp, m, n), F32)],
        compiler_params=_cp(("parallel", "arbitrary")))(a, b)


def _rmsnorm_fwd(x, g, name):
    n = x.shape[0]
    tm = min(512, n)

    def body(x_ref, g_ref, o_ref):
        xv = x_ref[...]
        r = lax.rsqrt(jnp.mean(xv * xv, axis=-1, keepdims=True) + EPS)
        o_ref[...] = (xv * r * g_ref[...]).astype(BF)

    return pl.pallas_call(
        body, name=name, grid=(n // tm,),
        in_specs=[pl.BlockSpec((tm, D), lambda i: (i, 0)), pl.BlockSpec((1, D), lambda i: (0, 0))],
        out_specs=pl.BlockSpec((tm, D), lambda i: (i, 0)),
        out_shape=jax.ShapeDtypeStruct((n, D), BF), compiler_params=_cp(("parallel",)))(x, g)


def _halo_maps(tm, n_rows):
    r8 = tm // 8
    last = n_rows // 8 - 1
    prev = lambda i: jnp.maximum(i * r8 - 1, 0)
    nxt = lambda i: jnp.minimum((i + 1) * r8, last)
    return prev, nxt


def _lane_blocks(width):
    return [slice(lo, min(lo + 128, width)) for lo in range(0, width, 128)]


def _conv_taps(w_ref, b_ref, g, lanes):
    return w_ref[g, 0:1, lanes], w_ref[g, 1:2, lanes], w_ref[g, 2:3, lanes], b_ref[g, :, lanes]


def _conv_tile(x, prev1, prev2, taps, row):
    w0, w1, w2, b = taps
    r1 = pltpu.roll(x, 1, 0)
    r2 = pltpu.roll(x, 2, 0)
    x1 = jnp.where(row == 0, prev1, r1)
    x2 = jnp.where(row < 2, prev2, r2)
    return b + w0 * x2 + w1 * x1 + w2 * x, x1, x2, r1, r2


def _halo16_maps(tm, n_rows):
    r16 = tm // 16
    last = n_rows // 16 - 1
    return (lambda i: jnp.maximum(i * r16 - 1, 0)), (lambda i: jnp.minimum((i + 1) * r16, last))


def _ffn_conv_fwd(up, cw, cb, seq, name):
    n = up.shape[2]
    tm = min(CONV_ROW_BLOCK, seq)
    prev, _ = _halo16_maps(tm, n)

    def body(u_ref, h_ref, w_ref, b_ref, o_ref, d_ref):
        i = pl.program_id(1)
        scale = jnp.where(lax.rem(i * tm, seq) == 0, 0.0, 1.0)
        for lanes in _lane_blocks(FSH):
            lw = lanes.stop - lanes.start
            row = lax.broadcasted_iota(jnp.int32, (8, lw), 0)
            taps = [_conv_taps(w_ref, b_ref, g, lanes) for g in range(2)]

            def tile(xs, carry):
                hc, nxt = [], []
                for g in range(2):
                    conv, _, _, r1, r2 = _conv_tile(xs[g], carry[2 * g], carry[2 * g + 1], taps[g], row)
                    hc.append(conv)
                    nxt += [r1, r2]
                s = _sigmoid(hc[0])
                silu = hc[0] * s
                return (silu * hc[1], hc[1] * (s * (1.0 + hc[0] * (1.0 - s))), silu), tuple(nxt)

            carry = []
            for g in range(2):
                halo = h_ref[g, :, lanes].astype(F32)[8:] * scale
                carry += [pltpu.roll(halo, 1, 0), pltpu.roll(halo, 2, 0)]
            carry = tuple(carry)
            for m in range(tm // 16):
                rows = slice(m * 16, m * 16 + 16)
                x16 = [u_ref[g, rows, lanes].astype(F32) for g in range(2)]
                a, carry = tile([x[:8] for x in x16], carry)
                b, carry = tile([x[8:] for x in x16], carry)
                o_ref[rows, lanes] = jnp.concatenate([a[0], b[0]], axis=0).astype(BF)
                d_ref[0, rows, lanes] = jnp.concatenate([a[1], b[1]], axis=0).astype(BF)
                d_ref[1, rows, lanes] = jnp.concatenate([a[2], b[2]], axis=0).astype(BF)

    return pl.pallas_call(
        body, name=name, grid=(4, n // tm),
        in_specs=[pl.BlockSpec((2, None, tm, FSH), lambda j, i: (0, j, i, 0)),
                  pl.BlockSpec((2, None, 16, FSH), lambda j, i: (0, j, prev(i), 0)),
                  pl.BlockSpec((2, None, 3, FSH), lambda j, i: (0, j, 0, 0)),
                  pl.BlockSpec((2, None, 1, FSH), lambda j, i: (0, j, 0, 0))],
        out_specs=[pl.BlockSpec((None, tm, FSH), lambda j, i: (j, i, 0)),
                   pl.BlockSpec((2, None, tm, FSH), lambda j, i: (0, j, i, 0))],
        out_shape=[jax.ShapeDtypeStruct((4, n, FSH), BF), jax.ShapeDtypeStruct((2, 4, n, FSH), BF)],
        compiler_params=_cp(("parallel", "parallel")))(up, up, cw, cb)


def _ffn_conv_bwd(up, dgate, dact, cw, seq, name):
    n = up.shape[2]
    tm = min(CONV_ROW_BLOCK, seq)
    _, nxt = _halo16_maps(tm, n)

    def body(u_ref, g_ref, gn_ref, da_ref, dn_ref, w_ref, du_ref, dw_ref, db_ref):
        i = pl.program_id(1)
        sn = jnp.where(lax.rem((i + 1) * tm, seq) == 0, 0.0, 1.0)
        first = i == 0
        for lanes in _lane_blocks(FSH):
            lw = lanes.stop - lanes.start
            row = lax.broadcasted_iota(jnp.int32, (8, lw), 0)
            taps = [(w_ref[g, 0:1, lanes], w_ref[g, 1:2, lanes], w_ref[g, 2:3, lanes]) for g in range(2)]

            def dconv(gs, da):
                ds = [gs[g] * da for g in range(2)]
                return [(d, pltpu.roll(d, 7, 0), pltpu.roll(d, 6, 0)) for d in ds]

            def finish(cur, after, xs, sums):
                dups, new_sums = [], []
                for g in range(2):
                    w0, w1, w2 = taps[g]
                    s1 = jnp.where(row == 7, after[g][1], cur[g][1])
                    s2 = jnp.where(row >= 6, after[g][2], cur[g][2])
                    dups.append(w2 * cur[g][0] + w1 * s1 + w0 * s2)
                    acc = sums[g]
                    new_sums.append((acc[0] + xs[g] * s2, acc[1] + xs[g] * s1, acc[2] + xs[g] * cur[g][0],
                                     acc[3] + cur[g][0]))
                return dups, new_sums

            def emit(m, held, after, sums):
                (ta, xa), (tb, xb) = held
                dup_a, sums = finish(ta, tb, xa, sums)
                dup_b, sums = finish(tb, after, xb, sums)
                for g in range(2):
                    du_ref[g, m * 16:m * 16 + 16, lanes] = jnp.concatenate([dup_a[g], dup_b[g]], axis=0).astype(BF)
                return sums

            zero = jnp.zeros((8, lw), F32)
            sums = [(zero,) * 4, (zero,) * 4]
            held = None
            for m in range(tm // 16):
                rows = slice(m * 16, m * 16 + 16)
                x16 = [u_ref[g, rows, lanes].astype(F32) for g in range(2)]
                g16 = [g_ref[g, rows, lanes].astype(F32) for g in range(2)]
                d16 = da_ref[rows, lanes].astype(F32)
                ta = dconv([a[:8] for a in g16], d16[:8])
                tb = dconv([a[8:] for a in g16], d16[8:])
                if held is not None:
                    sums = emit(m - 1, held, ta, sums)
                held = ((ta, [x[:8] for x in x16]), (tb, [x[8:] for x in x16]))
            tn_ = dconv([gn_ref[g, :, lanes].astype(F32)[:8] for g in range(2)], dn_ref[:, lanes].astype(F32)[:8] * sn)
            sums = emit(tm // 16 - 1, held, tn_, sums)
            for g in range(2):
                for k in range(3):
                    _accumulate(dw_ref.at[g, k:k + 1, lanes], first, _colsum(sums[g][k]))
                _accumulate(db_ref.at[g, :, lanes], first, _colsum(sums[g][3]))

    return pl.pallas_call(
        body, name=name, grid=(4, n // tm),
        in_specs=[pl.BlockSpec((2, None, tm, FSH), lambda j, i: (0, j, i, 0)),
                  pl.BlockSpec((2, None, tm, FSH), lambda j, i: (0, j, i, 0)),
                  pl.BlockSpec((2, None, 16, FSH), lambda j, i: (0, j, nxt(i), 0)),
                  pl.BlockSpec((None, tm, FSH), lambda j, i: (j, i, 0)),
                  pl.BlockSpec((None, 16, FSH), lambda j, i: (j, nxt(i), 0)),
                  pl.BlockSpec((2, None, 3, FSH), lambda j, i: (0, j, 0, 0))],
        out_specs=[pl.BlockSpec((2, None, tm, FSH), lambda j, i: (0, j, i, 0)),
                   pl.BlockSpec((2, None, 3, FSH), lambda j, i: (0, j, 0, 0)),
                   pl.BlockSpec((2, None, 1, FSH), lambda j, i: (0, j, 0, 0))],
        out_shape=[jax.ShapeDtypeStruct((2, 4, n, FSH), BF), jax.ShapeDtypeStruct((2, 4, 3, FSH), F32),
                   jax.ShapeDtypeStruct((2, 4, 1, FSH), F32)],
        compiler_params=_cp(("parallel", "arbitrary")))(up, dgate, dgate, dact, dact, cw)


def _shortconv_fwd(p, cw, cb, seq, name):
    n = p.shape[0]
    tm = min(CONV_ROW_BLOCK, seq)
    prev, _ = _halo16_maps(tm, n)

    def body(p_ref, h_ref, w_ref, b_ref, o_ref):
        i = pl.program_id(1)
        scale = jnp.where(lax.rem(i * tm, seq) == 0, 0.0, 1.0)
        q = p_ref[:, FB:2 * FB].astype(F32) * p_ref[:, 2 * FB:].astype(F32)
        row = lax.broadcasted_iota(jnp.int32, q.shape, 0)
        hq = (h_ref[:, FB:2 * FB].astype(F32) * h_ref[:, 2 * FB:].astype(F32))[8:] * scale
        hrow = lax.broadcasted_iota(jnp.int32, hq.shape, 0)
        h7 = _colsum(jnp.where(hrow == 7, hq, 0.0))
        h6 = _colsum(jnp.where(hrow == 6, hq, 0.0))
        p1 = jnp.where(row == 0, h7, pltpu.roll(q, 1, 0))
        p2 = jnp.where(row == 0, h6, jnp.where(row == 1, h7, pltpu.roll(q, 2, 0)))
        conv = b_ref[...] + w_ref[0:1, :] * p2 + w_ref[1:2, :] * p1 + w_ref[2:3, :] * q
        o_ref[...] = (p_ref[:, :FB].astype(F32) * conv).astype(BF)

    return pl.pallas_call(
        body, name=name, grid=(D // FB, n // tm),
        in_specs=[pl.BlockSpec((tm, 3 * FB), lambda j, i: (i, j)),
                  pl.BlockSpec((16, 3 * FB), lambda j, i: (prev(i), j)),
                  pl.BlockSpec((3, FB), lambda j, i: (0, j)),
                  pl.BlockSpec((1, FB), lambda j, i: (0, j))],
        out_specs=pl.BlockSpec((tm, FB), lambda j, i: (i, j)),
        out_shape=jax.ShapeDtypeStruct((n, D), BF), compiler_params=_cp(("parallel", "parallel")))(p, p, cw, cb)


def _shortconv_bwd(p, dmix, cw, cb, seq, name):
    n = p.shape[0]
    tm = min(CONV_ROW_BLOCK, seq)
    ext = tm + 16
    prev, nxt = _halo16_maps(tm, n)

    def body(p_ref, pp_ref, pn_ref, dm_ref, dn_ref, w_ref, b_ref, dp_ref, dw_ref, db_ref, qx, cx):
        i = pl.program_id(1)
        sp = jnp.where(lax.rem(i * tm, seq) == 0, 0.0, 1.0)
        sn = jnp.where(lax.rem((i + 1) * tm, seq) == 0, 0.0, 1.0)
        bg, cg, hx = (p_ref[:, :FB].astype(F32), p_ref[:, FB:2 * FB].astype(F32), p_ref[:, 2 * FB:].astype(F32))
        dm = dm_ref[...].astype(F32)
        qx[0:8, :] = (pp_ref[:, FB:2 * FB].astype(F32) * pp_ref[:, 2 * FB:].astype(F32))[8:] * sp
        qx[8:8 + tm, :] = cg * hx
        qx[8 + tm:, :] = jnp.zeros((8, FB), F32)
        cx[0:8, :] = jnp.zeros((8, FB), F32)
        cx[8:8 + tm, :] = dm * bg
        cx[8 + tm:, :] = (dn_ref[...].astype(F32) * pn_ref[:, :FB].astype(F32))[:8] * sn
        q0 = qx[...]
        q1 = pltpu.roll(q0, 1, 0)
        q2 = pltpu.roll(q0, 2, 0)
        main = slice(8, 8 + tm)
        conv = b_ref[...] + w_ref[0:1, :] * q2[main] + w_ref[1:2, :] * q1[main] + w_ref[2:3, :] * q0[main]
        dc = cx[...]
        dq = (w_ref[2:3, :] * dc + w_ref[1:2, :] * pltpu.roll(dc, ext - 1, 0)
              + w_ref[0:1, :] * pltpu.roll(dc, ext - 2, 0))[main]
        dp_ref[:, :FB] = (dm * conv).astype(BF)
        dp_ref[:, FB:2 * FB] = (dq * hx).astype(BF)
        dp_ref[:, 2 * FB:] = (dq * cg).astype(BF)
        first = i == 0
        dcm = dc[main]
        _accumulate(dw_ref.at[0:1, :], first, _colsum(dcm * q2[main]))
        _accumulate(dw_ref.at[1:2, :], first, _colsum(dcm * q1[main]))
        _accumulate(dw_ref.at[2:3, :], first, _colsum(dcm * q0[main]))
        _accumulate(db_ref, first, _colsum(dcm))

    return pl.pallas_call(
        body, name=name, grid=(D // FB, n // tm),
        in_specs=[pl.BlockSpec((tm, 3 * FB), lambda j, i: (i, j)),
                  pl.BlockSpec((16, 3 * FB), lambda j, i: (prev(i), j)),
                  pl.BlockSpec((16, 3 * FB), lambda j, i: (nxt(i), j)),
                  pl.BlockSpec((tm, FB), lambda j, i: (i, j)),
                  pl.BlockSpec((16, FB), lambda j, i: (nxt(i), j)),
                  pl.BlockSpec((3, FB), lambda j, i: (0, j)),
                  pl.BlockSpec((1, FB), lambda j, i: (0, j))],
        out_specs=[pl.BlockSpec((tm, 3 * FB), lambda j, i: (i, j)),
                   pl.BlockSpec((3, FB), lambda j, i: (0, j)),
                   pl.BlockSpec((1, FB), lambda j, i: (0, j))],
        out_shape=[jax.ShapeDtypeStruct((n, 3 * D), BF), jax.ShapeDtypeStruct((3, D), F32),
                   jax.ShapeDtypeStruct((1, D), F32)],
        scratch_shapes=[pltpu.VMEM((ext, FB), F32), pltpu.VMEM((ext, FB), F32)],
        compiler_params=_cp(("parallel", "arbitrary")))(p, p, p, dmix, dmix, cw, cb)


def _gmlp_fwd(uv, wm, bst, gv, seq, name):
    n = uv.shape[0]
    tm = min(ROW_BLOCK, seq)

    def body(x_ref, w_ref, b_ref, g_ref, o_ref):
        ge_v = _gelu(x_ref[:, GM_W:])
        r = lax.rsqrt(jnp.mean(ge_v * ge_v, axis=-1, keepdims=True) + EPS)
        vn = (ge_v * r * g_ref[...]).astype(BF)
        for c in range(tm // CHUNK):
            rows = slice(c * CHUNK, (c + 1) * CHUNK)
            for h in range(GM_HEADS):
                cols = slice(h * CHUNK, (h + 1) * CHUNK)
                gate = jnp.dot(w_ref[h], vn[rows, cols], preferred_element_type=F32) + b_ref[:, h:h + 1]
                o_ref[rows, cols] = (_gelu(x_ref[rows, cols]) * gate).astype(BF)

    return pl.pallas_call(
        body, name=name, grid=(n // tm,),
        in_specs=[pl.BlockSpec((tm, 2 * GM_W), lambda i: (i, 0)),
                  pl.BlockSpec((GM_HEADS, CHUNK, CHUNK), lambda i: (0, 0, 0)),
                  pl.BlockSpec((CHUNK, GM_HEADS), lambda i: (0, 0)),
                  pl.BlockSpec((1, GM_W), lambda i: (0, 0))],
        out_specs=pl.BlockSpec((tm, GM_W), lambda i: (i, 0)),
        out_shape=jax.ShapeDtypeStruct((n, GM_W), BF), compiler_params=_cp(("parallel",)))(uv, wm, bst, gv)


def _gmlp_bwd(uv, dout, wm, wmt, bst, gv, seq, name):
    n = uv.shape[0]
    tm = min(ROW_BLOCK, seq)

    def body(x_ref, do_ref, w_ref, wt_ref, b_ref, g_ref, dx_ref, dw_ref, db_ref, dg_ref, dvn_scr):
        first = pl.program_id(0) == 0
        ge_v = _gelu(x_ref[:, GM_W:])
        r = lax.rsqrt(jnp.mean(ge_v * ge_v, axis=-1, keepdims=True) + EPS)
        vh = ge_v * r
        vn = (vh * g_ref[...]).astype(BF)
        tril = (lax.broadcasted_iota(jnp.int32, (CHUNK, CHUNK), 0)
                >= lax.broadcasted_iota(jnp.int32, (CHUNK, CHUNK), 1))
        for h in range(GM_HEADS):
            cols = slice(h * CHUNK, (h + 1) * CHUNK)
            dw = jnp.zeros((CHUNK, CHUNK), F32)
            dbs = jnp.zeros((CHUNK, 1), F32)
            for c in range(tm // CHUNK):
                rows = slice(c * CHUNK, (c + 1) * CHUNK)
                blk = vn[rows, cols]
                gate = jnp.dot(w_ref[h], blk, preferred_element_type=F32) + b_ref[:, h:h + 1]
                xu = x_ref[rows, cols]
                do = do_ref[rows, cols]
                dx_ref[rows, cols] = (do * gate * _gelu_grad(xu)).astype(BF)
                dgate = do * _gelu(xu)
                dgb = dgate.astype(BF)
                dw = dw + lax.dot_general(dgb, blk, _DIMS['nt'], preferred_element_type=F32)
                dbs = dbs + jnp.sum(dgate, axis=1, keepdims=True)
                dvn_scr[rows, cols] = jnp.dot(wt_ref[h], dgb, preferred_element_type=F32)
            _accumulate(dw_ref.at[h], first, jnp.where(tril, dw, 0.0))
            _accumulate(db_ref.at[h], first, dbs)
        dvn = dvn_scr[...]
        _accumulate(dg_ref, first, _colsum(dvn * vh))
        dvh = dvn * g_ref[...]
        dv = r * (dvh - vh * jnp.mean(dvh * vh, axis=-1, keepdims=True))
        dx_ref[:, GM_W:] = (dv * _gelu_grad(x_ref[:, GM_W:])).astype(BF)

    full3 = pl.BlockSpec((GM_HEADS, CHUNK, CHUNK), lambda i: (0, 0, 0))
    return pl.pallas_call(
        body, name=name, grid=(n // tm,),
        in_specs=[pl.BlockSpec((tm, 2 * GM_W), lambda i: (i, 0)), pl.BlockSpec((tm, GM_W), lambda i: (i, 0)),
                  full3, full3, pl.BlockSpec((CHUNK, GM_HEADS), lambda i: (0, 0)),
                  pl.BlockSpec((1, GM_W), lambda i: (0, 0))],
        out_specs=[pl.BlockSpec((tm, 2 * GM_W), lambda i: (i, 0)), full3,
                   pl.BlockSpec((GM_HEADS, CHUNK, 1), lambda i: (0, 0, 0)),
                   pl.BlockSpec((1, GM_W), lambda i: (0, 0))],
        out_shape=[jax.ShapeDtypeStruct((n, 2 * GM_W), BF), jax.ShapeDtypeStruct((GM_HEADS, CHUNK, CHUNK), F32),
                   jax.ShapeDtypeStruct((GM_HEADS, CHUNK, 1), F32), jax.ShapeDtypeStruct((1, GM_W), F32)],
        scratch_shapes=[pltpu.VMEM((tm, GM_W), F32)],
        compiler_params=_cp(("arbitrary",)))(uv, dout, wm, wmt, bst, gv)


def _s5_disc(lam_re, lam_im, log_dt, b_re, b_im):
    lr = jnp.minimum(lam_re, LAM_MAX)
    li = lam_im
    dt = jnp.exp(log_dt)
    mag = jnp.exp(lr * dt)
    ab_re = mag * jnp.cos(li * dt)
    ab_im = mag * jnp.sin(li * dt)
    den = lr * lr + li * li
    nr = ab_re - 1.0
    ni = ab_im
    z_re = (nr * lr + ni * li) / den
    z_im = (ni * lr - nr * li) / den
    return ab_re, ab_im, z_re * b_re - z_im * b_im, z_re * b_im + z_im * b_re


def _s5_disc_fwd(args, name):
    shp = jax.ShapeDtypeStruct(args[0].shape, F32)

    def body(*refs):
        outs = _s5_disc(*[r[...] for r in refs[:5]])
        for o_ref, o in zip(refs[5:], outs):
            o_ref[...] = o

    return pl.pallas_call(body, name=name, out_shape=[shp] * 4)(*args)


def _s5_disc_bwd(args, cts, name):
    shp = jax.ShapeDtypeStruct(args[0].shape, F32)

    def body(*refs):
        _, vjp = jax.vjp(_s5_disc, *[r[...] for r in refs[:5]])
        grads = vjp(tuple(r[...] for r in refs[5:9]))
        for o_ref, o in zip(refs[9:], grads):
            o_ref[...] = o

    return pl.pallas_call(body, name=name, out_shape=[shp] * 5)(*args, *cts)


def _cmul(a, b):
    return a[0] * b[0] - a[1] * b[1], a[0] * b[1] + a[1] * b[0]


def _scan_tables(ar, ai, reverse):
    if reverse:
        ai = -ai
    a1 = (ar, ai)
    a2 = _cmul(a1, a1)
    a3 = _cmul(a2, a1)
    a4 = _cmul(a2, a2)
    powers = [a1, a2, a3, a4, _cmul(a4, a1), _cmul(a4, a2), _cmul(a4, a3), _cmul(a4, a4)]
    row = lax.broadcasted_iota(jnp.int32, (8, NST), 0)
    zero = jnp.zeros((8, NST), F32)
    pr, pi = zero, zero
    for r in range(8):
        pw = powers[7 - r] if reverse else powers[r]
        pr = jnp.where(row == r, pw[0], pr)
        pi = jnp.where(row == r, pw[1], pi)
    levels = []
    for d, pw in ((1, a1), (2, a2), (4, a4)):
        ok = (row <= 7 - d) if reverse else (row >= d)
        levels.append((d, jnp.where(ok, pw[0], zero), jnp.where(ok, pw[1], zero)))
    return (pr, pi), levels


def _scan_block(src, dst, car, tables, n_tiles, reverse):
    (pr, pi), levels = tables
    row = lax.broadcasted_iota(jnp.int32, (8, NST), 0)
    out_row = 0 if reverse else 7

    def step(t, carry):
        cr, ci = carry
        tile = (n_tiles - 1 - t) if reverse else t
        rows = pl.ds(pl.multiple_of(tile * 8, 8), 8)
        xr = src[rows, 0:NST]
        xi = src[rows, NST:2 * NST]
        for d, dr, di in levels:
            shift = 8 - d if reverse else d
            rr = pltpu.roll(xr, shift, 0)
            ri = pltpu.roll(xi, shift, 0)
            xr, xi = xr + dr * rr - di * ri, xi + dr * ri + di * rr
        hr = xr + pr * cr - pi * ci
        hi = xi + pr * ci + pi * cr
        dst[rows, 0:NST] = hr
        dst[rows, NST:2 * NST] = hi
        return (_colsum(jnp.where(row == out_row, hr, 0.0)), _colsum(jnp.where(row == out_row, hi, 0.0)))

    cr, ci = lax.fori_loop(0, n_tiles, step, (car[0:1, 0:NST], car[0:1, NST:2 * NST]))
    car[0:1, 0:NST] = cr
    car[0:1, NST:2 * NST] = ci


def _s5_fwd(u, ab, bbt, cmat, dvec, wglu, bglu, seq, name):
    n = u.shape[0]
    tm = min(ROW_BLOCK, seq)

    def body(u_ref, ab_ref, bb_ref, c_ref, d_ref, w_ref, b_ref, h_ref, o_ref, xs, car):
        i = pl.program_id(0)

        @pl.when(lax.rem(i * tm, seq) == 0)
        def _():
            car[...] = jnp.zeros(car.shape, F32)

        uv = u_ref[...]
        xs[...] = jnp.dot(uv.astype(BF), bb_ref[...], preferred_element_type=F32)
        tables = _scan_tables(ab_ref[0:1, 0:NST], ab_ref[0:1, NST:2 * NST], False)
        _scan_block(xs, h_ref, car, tables, tm // 8, False)
        y = jnp.dot(h_ref[...].astype(BF), c_ref[...], preferred_element_type=F32) + d_ref[...] * uv
        g1 = _gelu(y)
        z = jnp.dot(g1.astype(BF), w_ref[...], preferred_element_type=F32) + b_ref[...]
        o_ref[...] = (g1 * _sigmoid(z)).astype(BF)

    const = lambda shape: pl.BlockSpec(shape, lambda i: (0, 0))
    return pl.pallas_call(
        body, name=name, grid=(n // tm,),
        in_specs=[pl.BlockSpec((tm, SSM_W), lambda i: (i, 0)), const((1, 2 * NST)), const((SSM_W, 2 * NST)),
                  const((2 * NST, SSM_W)), const((1, SSM_W)), const((SSM_W, SSM_W)), const((1, SSM_W))],
        out_specs=[pl.BlockSpec((tm, 2 * NST), lambda i: (i, 0)), pl.BlockSpec((tm, SSM_W), lambda i: (i, 0))],
        out_shape=[jax.ShapeDtypeStruct((n, 2 * NST), F32), jax.ShapeDtypeStruct((n, SSM_W), BF)],
        scratch_shapes=[pltpu.VMEM((tm, 2 * NST), F32), pltpu.VMEM((8, 2 * NST), F32)],
        compiler_params=_cp(("arbitrary",)))(u, ab, bbt, cmat, dvec, wglu, bglu)


def _s5_bwd(da, u, hst, ab, bbt, cmat, dvec, wglu, bglu, seq, name):
    n = u.shape[0]
    tm = min(ROW_BLOCK, seq)
    nb = n // tm
    blk = lambda r: nb - 1 - r
    prev, _ = _halo_maps(tm, n)

    def body(da_ref, u_ref, h_ref, hp_ref, ab_ref, bb_ref, c_ref, d_ref, w_ref, b_ref,
             du_ref, dw_ref, dbg_ref, dd_ref, dc_ref, dbb_ref, dab_ref, gs, car):
        r = pl.program_id(0)
        i = blk(r)
        first = r == 0

        @pl.when(lax.rem((i + 1) * tm, seq) == 0)
        def _():
            car[...] = jnp.zeros(car.shape, F32)

        uv = u_ref[...]
        dav = da_ref[...]
        hb = h_ref[...]
        hb16 = hb.astype(BF)
        dvv = d_ref[...]
        y = jnp.dot(hb16, c_ref[...], preferred_element_type=F32) + dvv * uv
        g1 = _gelu(y)
        g16 = g1.astype(BF)
        s = _sigmoid(jnp.dot(g16, w_ref[...], preferred_element_type=F32) + b_ref[...])
        dz = dav * g1 * s * (1.0 - s)
        dz16 = dz.astype(BF)
        dg1 = dav * s + lax.dot_general(dz16, w_ref[...], _DIMS['nt'], preferred_element_type=F32)
        _accumulate(dw_ref, first, lax.dot_general(g16, dz16, _DIMS['tn'], preferred_element_type=F32))
        _accumulate(dbg_ref, first, _colsum(dz))
        dy = dg1 * _gelu_grad(y)
        dy16 = dy.astype(BF)
        _accumulate(dd_ref, first, _colsum(dy * uv))
        _accumulate(dc_ref, first, lax.dot_general(hb16, dy16, _DIMS['tn'], preferred_element_type=F32))
        gs[...] = lax.dot_general(dy16, c_ref[...], _DIMS['nt'], preferred_element_type=F32)
        tables = _scan_tables(ab_ref[0:1, 0:NST], ab_ref[0:1, NST:2 * NST], True)
        _scan_block(gs, gs, car, tables, tm // 8, True)
        g = gs[...]
        g16b = g.astype(BF)
        sp = jnp.where(lax.rem(i * tm, seq) == 0, 0.0, 1.0)
        row = lax.broadcasted_iota(jnp.int32, hb.shape, 0)
        hprev = jnp.where(row == 0, hp_ref[7:8, :] * sp, pltpu.roll(hb, 1, 0))
        gr, gi = g[:, :NST], g[:, NST:]
        hr, hi = hprev[:, :NST], hprev[:, NST:]
        _accumulate(dab_ref.at[:, 0:NST], first, _colsum(gr * hr + gi * hi))
        _accumulate(dab_ref.at[:, NST:2 * NST], first, _colsum(gi * hr - gr * hi))
        _accumulate(dbb_ref, first, lax.dot_general(uv.astype(BF), g16b, _DIMS['tn'], preferred_element_type=F32))
        du = dy * dvv + lax.dot_general(g16b, bb_ref[...], _DIMS['nt'], preferred_element_type=F32)
        du_ref[...] = du.astype(BF)

    const = lambda shape: pl.BlockSpec(shape, lambda r: (0, 0))
    rowspec = lambda w: pl.BlockSpec((tm, w), lambda r: (blk(r), 0))
    return pl.pallas_call(
        body, name=name, grid=(nb,),
        in_specs=[rowspec(SSM_W), rowspec(SSM_W), rowspec(2 * NST),
                  pl.BlockSpec((8, 2 * NST), lambda r: (prev(blk(r)), 0)),
                  const((1, 2 * NST)), const((SSM_W, 2 * NST)), const((2 * NST, SSM_W)), const((1, SSM_W)),
                  const((SSM_W, SSM_W)), const((1, SSM_W))],
        out_specs=[rowspec(SSM_W), const((SSM_W, SSM_W)), const((1, SSM_W)), const((1, SSM_W)),
                   const((2 * NST, SSM_W)), const((SSM_W, 2 * NST)), const((1, 2 * NST))],
        out_shape=[jax.ShapeDtypeStruct((n, SSM_W), BF), jax.ShapeDtypeStruct((SSM_W, SSM_W), F32),
                   jax.ShapeDtypeStruct((1, SSM_W), F32), jax.ShapeDtypeStruct((1, SSM_W), F32),
                   jax.ShapeDtypeStruct((2 * NST, SSM_W), F32), jax.ShapeDtypeStruct((SSM_W, 2 * NST), F32),
                   jax.ShapeDtypeStruct((1, 2 * NST), F32)],
        scratch_shapes=[pltpu.VMEM((tm, 2 * NST), F32), pltpu.VMEM((8, 2 * NST), F32)],
        compiler_params=_cp(("arbitrary",)))(da, u, hst, hst, ab, bbt, cmat, dvec, wglu, bglu)


def _s5_rows(lam_re, lam_im, log_dt, b_re, b_im):
    rep = lambda a: jnp.broadcast_to(a[:, None, :], (SSM_G, SSM_H, SSM_P)).reshape(SSM_W, SSM_P)
    dt = jnp.broadcast_to(log_dt[:, None, None], (SSM_G, SSM_H, SSM_P)).reshape(SSM_W, SSM_P)
    tr = lambda b: b.transpose(0, 2, 1).reshape(SSM_W, SSM_P)
    return rep(lam_re), rep(lam_im), dt, tr(b_re), tr(b_im)


def _block_diag(rows_gp, inner):
    eye = jnp.eye(SSM_G, dtype=rows_gp.dtype)
    return (rows_gp[:, :, None, :] * eye[:, None, :, None]).reshape(SSM_G * inner, SSM_G * SSM_P)


def _diag_blocks(mat, inner):
    m4 = mat.reshape(SSM_G, inner, SSM_G, SSM_P)
    return jnp.stack([m4[g, :, g, :] for g in range(SSM_G)])


def _interleave(w, parts):
    lead = w.shape[:-1]
    nb = w.shape[-1] // (parts * FB)
    return jnp.swapaxes(w.reshape(lead + (parts, nb, FB)), -3, -2).reshape(w.shape)


def _deinterleave(w, parts):
    lead = w.shape[:-1]
    nb = w.shape[-1] // (parts * FB)
    return jnp.swapaxes(w.reshape(lead + (nb, parts, FB)), -3, -2).reshape(w.shape)


def _ffn_fwd(h, f, w_up, w_down, cw, cb, seq, tag, tail):
    n = h.shape[0]
    tm = min(2048, n)
    ni = n // tm
    up = _matmul_spec(
        f, w_up, 'nn', (ni, NDEV, 1),
        pl.BlockSpec((tm, D), lambda i, s, k: (i, 0)),
        pl.BlockSpec((D, FSH), lambda i, s, k: (s, 0)),
        pl.BlockSpec((tm, FSH), lambda i, s, k: (s * ni + i, 0)), (NDEV * n, FSH), f"{tag}_up", out_dtype=BF)
    up = up.reshape(2, 4, n, FSH)
    act, dgate = _ffn_conv_fwd(up, cw, cb, seq, f"{tag}_conv")
    out = _matmul_shards(act, w_down.reshape(4, FSH, D), 'nn', 512, D, f"{tag}_down", resid=h, tail=tail)
    return out, (f, up, act, dgate)


def _ffn_bwd(dh, dhb, h, g, w_up, w_down, cw, cb, saved, seq, tag):
    f, up, act, dgate = saved
    n = h.shape[0]
    tm = min(2048, n)
    ni = n // tm
    tk = min(2048, n)
    dact = _matmul_spec(
        dhb, w_down, 'nt', (ni, 4, 1),
        pl.BlockSpec((tm, D), lambda i, j, k: (i, 0)),
        pl.BlockSpec((FSH, D), lambda i, j, k: (j, 0)),
        pl.BlockSpec((tm, FSH), lambda i, j, k: (j * ni + i, 0)), (4 * n, FSH), f"{tag}_ddown_x", out_dtype=BF)
    dw_down = _matmul_tn_shards(act, dhb, 2, tk, f"{tag}_ddown_w")
    dup, dcw, dcb = _ffn_conv_bwd(up, dgate, dact.reshape(4, n, FSH), cw, seq, f"{tag}_dconv")
    dh_in, dhb_in, dg = _matmul_shards(dup.reshape(NDEV, n, FSH), w_up.reshape(NDEV, D, FSH), 'nt', 256, D,
                                       f"{tag}_dup_x", tail=('norm_bwd', h, g, dh))
    dw_up = _matmul_tn_shards(f, dup.reshape(NDEV, n, FSH), 2, tk, f"{tag}_dup_w")
    grads = dict(g=dg, w_up=dw_up, w_down=dw_down.reshape(NDEV, DFF // NDEV, D),
                 cw=dcw.reshape(NDEV, 3, FSH), cb=dcb.reshape(2 * DFF))
    return dh_in, dhb_in, grads


def _col_shards(w, width):
    return w.reshape(w.shape[0], NDEV, width).transpose(1, 0, 2)


def _local_step(x, tgt, w, gw, wait_ffn0, wait_rest, token, scatter, seq):
    bf = lambda a: a.astype(BF)
    row = lambda a: a.reshape(1, -1).astype(F32)
    w_ev = gw['ev_w_in'].transpose(1, 0, 2).reshape(D, 1792)
    w_ev_s5, w_ev_gm = w_ev[:, :SSM_W], w_ev[:, SSM_W:]
    w_evo = gw['ev_w_out'].reshape(D, D)
    f_cb = [w['ffn_conv_b'][l].reshape(2, 4, 1, FSH) for l in range(2)]
    tril = jnp.tril(jnp.ones((CHUNK, CHUNK), dtype=bool))
    gm_w = jnp.where(tril, w['gm_w_s'][0], 0.0)
    gm_wm, gm_wmt = bf(gm_w), bf(jnp.swapaxes(gm_w, 1, 2))
    gm_bt = w['gm_b_s'][0].T
    gm_gv = row(w['gm_v_g'][0])
    s5_in = _s5_rows(w['s5_lam_re'][0], w['s5_lam_im'][0], w['s5_log_dt'][0], w['s5_b_re'][0], w['s5_b_im'][0])
    ab_re, ab_im, bb_re, bb_im = _s5_disc_fwd(s5_in, "s5_disc")
    first_h = lambda a: a.reshape(SSM_G, SSM_H, SSM_P)[:, 0, :].reshape(1, NST)
    s5_ab = jnp.concatenate([first_h(ab_re), first_h(ab_im)], axis=1)
    to_gp = lambda a: a.reshape(SSM_G, SSM_H, SSM_P)
    s5_bbt = bf(jnp.concatenate([_block_diag(to_gp(bb_re), SSM_H), _block_diag(to_gp(bb_im), SSM_H)], axis=1))
    s5_cmat = bf(jnp.concatenate([_block_diag(w['s5_c_re'][0], SSM_H).T, -_block_diag(w['s5_c_im'][0], SSM_H).T],
                                 axis=0))
    s5_d, s5_bg, s5_wg = row(w['s5_d'][0]), row(w['s5_b_glu'][0]), gw['s5_w_glu'].reshape(SSM_W, SSM_W)
    g_mix = [row(w['mix_norm_g'][0]) + token[0:1, 0:1], row(w['mix_norm_g'][1])]
    g_ffn = [row(w['ffn_norm_g'][l]) for l in range(2)]
    g_fin = row(w['final_norm_g'])

    h0 = x
    y0 = _rmsnorm_fwd(h0, g_mix[0], "ev_norm")
    p_s5 = _matmul(y0, w_ev_s5, 'nn', 1024, 256, D, "ev_in_s5")
    p_gm = _matmul(y0, w_ev_gm, 'nn', 1024, 2 * GM_W, D, "ev_in_gm")
    hst, a_out = _s5_fwd(p_s5, s5_ab, s5_bbt, s5_cmat, s5_d, s5_wg, s5_bg, seq, "s5_fwd")
    b_out = _gmlp_fwd(p_gm, gm_wm, gm_bt, gm_gv, seq, "gmlp_fwd")
    mixcat = jnp.concatenate([a_out, b_out], axis=1)
    h1, f0 = _matmul(mixcat, w_evo, 'nn', 1024, D, D, "ev_out", resid=h0, tail=('norm_fwd', g_ffn[0]))
    g0 = wait_ffn0(mixcat)
    w_up0, w_dn0 = g0['ffn_w_up0'].reshape(NDEV * D, FSH), g0['ffn_w_down0'].reshape(DFF, D)
    f_cw0 = g0['ffn_conv_w0'].reshape(2, 4, 3, FSH)
    (h2, y1), ffn0 = _ffn_fwd(h1, f0, w_up0, w_dn0, f_cw0, f_cb[0], seq, "ffn0", ('norm_fwd', g_mix[1]))
    g1 = wait_rest(h2)
    w_od = _interleave(g1['od_w_in'].transpose(1, 0, 2).reshape(D, 3 * D), 3)
    w_odo = g1['od_w_out'].reshape(D, D)
    od_cw = g1['od_conv_w'].transpose(1, 0, 2).reshape(3, D)
    od_cb = g1['od_conv_b'].reshape(1, D)
    w_up1, w_dn1 = g1['ffn_w_up1'].reshape(NDEV * D, FSH), g1['ffn_w_down1'].reshape(DFF, D)
    f_cw1 = g1['ffn_conv_w1'].reshape(2, 4, 3, FSH)
    p_od = _matmul(y1, w_od, 'nn', 1024, 3 * D // 2, D, "od_in", out_dtype=BF)
    mixin = _shortconv_fwd(p_od, od_cw, od_cb, seq, "od_conv")
    h3, f1 = _matmul(mixin, w_odo, 'nn', 1024, D, D, "od_out", resid=h2, tail=('norm_fwd', g_ffn[1]))
    (loss, dh4, dh4b, dg_fin), ffn1 = _ffn_fwd(h3, f1, w_up1, w_dn1, f_cw1, f_cb[1], seq, "ffn1",
                                                ('loss', g_fin, tgt))

    dh3, dh3b, gf1 = _ffn_bwd(dh4, dh4b, h3, g_ffn[1], w_up1, w_dn1, f_cw1, f_cb[1], ffn1, seq, "ffn1")
    dmixin = _matmul(dh3b, w_odo, 'nt', 1024, D, D, "od_dout_x", out_dtype=BF)
    dw_odo = _matmul(mixin, dh3b, 'tn', D, 512, 4096, "od_dout_w", out_dtype=BF)
    dp_od, d_od_cw, d_od_cb = _shortconv_bwd(p_od, dmixin, od_cw, od_cb, seq, "od_dconv")
    dw_od = _matmul(y1, dp_od, 'tn', D, 3 * D // 2, 2048, "od_din_w", out_dtype=BF)
    sent = scatter("scatter_layer1", {
        'od_w_in': _col_shards(_deinterleave(dw_od, 3), 384), 'od_conv_w': _col_shards(d_od_cw, D // NDEV),
        'od_conv_b': d_od_cb.reshape(NDEV, 1, D // NDEV), 'od_w_out': dw_odo.reshape(NDEV, D // NDEV, D),
        'ffn_w_up1': gf1['w_up'], 'ffn_conv_w1': gf1['cw'], 'ffn_w_down1': gf1['w_down']})
    dh2, dh2b, dg_mix1 = _matmul(dp_od, w_od, 'nt', 512, D, 3 * D, "od_din_x",
                                 tail=('norm_bwd', h2, g_mix[1] + sent[0:1, 0:1], dh3))
    dh1, dh1b, gf0 = _ffn_bwd(dh2, dh2b, h1, g_ffn[0], w_up0, w_dn0, f_cw0, f_cb[0], ffn0, seq, "ffn0")
    dmix_a = _matmul(dh1b, w_evo[:SSM_W], 'nt', 1024, SSM_W, D, "ev_dout_xa")
    dmix_b = _matmul(dh1b, w_evo[SSM_W:], 'nt', 1024, GM_W, D, "ev_dout_xb")
    dw_evo = _matmul(mixcat, dh1b, 'tn', D, 512, 4096, "ev_dout_w", out_dtype=BF)
    sent = scatter("scatter_ffn0", {'ffn_w_up0': gf0['w_up'], 'ffn_conv_w0': gf0['cw'], 'ffn_w_down0': gf0['w_down'],
                                    'ev_w_out': dw_evo.reshape(NDEV, D // NDEV, D)})
    dp_s5, d_wg, d_bg, d_d, d_cmat, d_bbt, d_ab = _s5_bwd(dmix_a, p_s5, hst, s5_ab, s5_bbt, s5_cmat,
                                                           s5_d + sent[0:1, 0:1], s5_wg, s5_bg, seq, "s5_bwd")
    dp_gm, d_gmw, d_gmb, d_gmg = _gmlp_bwd(p_gm, dmix_b, gm_wm, gm_wmt, gm_bt, gm_gv, seq, "gmlp_bwd")
    dw_ev = jnp.concatenate([_matmul(y0, dp_s5, 'tn', D, SSM_W, 4096, "ev_din_wa", out_dtype=BF),
                             _matmul(y0, dp_gm, 'tn', D, 2 * GM_W, 2048, "ev_din_wb", out_dtype=BF)], axis=1)
    sent = scatter("scatter_even", {'ev_w_in': _col_shards(dw_ev, 224),
                                    's5_w_glu': d_wg.reshape(NDEV, SSM_W // NDEV, SSM_W)})
    dy0 = _matmul(dp_gm, w_ev_gm, 'nt', 512, D, 2 * GM_W, "ev_din_xb")
    grad_x, _, dg_mix0 = _matmul(dp_s5, w_ev_s5, 'nt', 512, D, SSM_W, "ev_din_xa", resid=dy0,
                                 tail=('norm_bwd', h0, g_mix[0] + sent[0:1, 0:1], dh1))

    put_h0 = lambda a: jnp.zeros((SSM_G, SSM_H, SSM_P), F32).at[:, 0, :].set(a.reshape(SSM_G, SSM_P)).reshape(
        SSM_W, SSM_P)
    ct = (put_h0(d_ab[:, :NST]), put_h0(d_ab[:, NST:]),
          _diag_blocks(d_bbt[:, :NST], SSM_H).reshape(SSM_W, SSM_P),
          _diag_blocks(d_bbt[:, NST:], SSM_H).reshape(SSM_W, SSM_P))
    d_lre, d_lim, d_ldt, d_bre, d_bim = _s5_disc_bwd(s5_in, ct, "s5_ddisc")
    over_h = lambda a: a.reshape(SSM_G, SSM_H, SSM_P).sum(axis=1)
    un_tr = lambda a: a.reshape(SSM_G, SSM_H, SSM_P).transpose(0, 2, 1)
    d_cre = _diag_blocks(d_cmat[:NST].T, SSM_H)
    d_cim = -_diag_blocks(d_cmat[NST:].T, SSM_H)

    repl = {
        'mix_norm_g': jnp.concatenate([dg_mix0, dg_mix1], axis=0),
        'ffn_norm_g': jnp.concatenate([gf0['g'], gf1['g']], axis=0),
        'final_norm_g': dg_fin.reshape(D),
        's5_lam_re': over_h(d_lre)[None], 's5_lam_im': over_h(d_lim)[None],
        's5_log_dt': over_h(d_ldt).sum(axis=1)[None],
        's5_b_re': un_tr(d_bre)[None], 's5_b_im': un_tr(d_bim)[None],
        's5_c_re': d_cre[None], 's5_c_im': d_cim[None],
        's5_d': d_d, 's5_b_glu': d_bg,
        'gm_w_s': d_gmw[None], 'gm_b_s': d_gmb.reshape(1, GM_HEADS, CHUNK), 'gm_v_g': d_gmg,
        'ffn_conv_b': jnp.stack([gf0['cb'], gf1['cb']]),
    }
    return loss, grad_x, repl


HBM_SPEC = pl.BlockSpec(memory_space=pltpu.HBM)


def _at_axis(ref, pos, index):
    return ref.at[(slice(None),) * pos + (index,)]


def _all_gather(shards, positions, name):
    n = len(shards)

    def body(*refs):
        xs, outs = refs[:n], refs[n:2 * n]
        send_sems, recv_sems, local_sems = refs[2 * n:]
        x, y, c = lax.axis_index("x"), lax.axis_index("y"), lax.axis_index("c")
        me, sibling = (x, y, c), (x, y, 1 - c)
        chips = [(1 - x, y), (x, 1 - y), (1 - x, 1 - y)]

        def block(p, dev):
            return _at_axis(outs[p], positions[p], 4 * dev[0] + 2 * dev[1] + dev[2])

        def copy(p, k, dev, to, src=None):
            return pltpu.make_async_remote_copy(
                src_ref=block(p, dev) if src is None else src, dst_ref=block(p, dev),
                send_sem=send_sems.at[p, k], recv_sem=recv_sems.at[p, k], device_id=to, device_id_type=MESH_T)

        mine = [pltpu.make_async_copy(xs[p], block(p, me), local_sems.at[p]) for p in range(n)]
        for cp in mine:
            cp.start()
        first = [copy(p, 0, me, sibling, src=xs[p]) for p in range(n)]
        first += [copy(p, 1 + j, me, (*chip, c), src=xs[p]) for j, chip in enumerate(chips) for p in range(n)]
        for cp in first:
            cp.start()
        passed = []
        for j, chip in enumerate(chips):
            for p in range(n):
                copy(p, 1 + j, (*chip, c), me).wait_recv()
                fwd = copy(p, 4 + j, (*chip, c), sibling)
                fwd.start()
                passed.append(fwd)
        for p in range(n):
            copy(p, 0, sibling, me).wait_recv()
        for j, chip in enumerate(chips):
            for p in range(n):
                copy(p, 4 + j, (*chip, 1 - c), me).wait_recv()
        for cp in first + passed:
            cp.wait_send()
        for cp in mine:
            cp.wait()

    out_shape = [jax.ShapeDtypeStruct(s.shape[:pos] + (NDEV,) + s.shape[pos:], s.dtype)
                 for s, pos in zip(shards, positions)]
    return pl.pallas_call(
        body, name=name, out_shape=out_shape, in_specs=[HBM_SPEC] * n, out_specs=[HBM_SPEC] * n,
        scratch_shapes=[pltpu.SemaphoreType.DMA((n, 7)), pltpu.SemaphoreType.DMA((n, 7)),
                        pltpu.SemaphoreType.DMA((n,))])(*shards)


def _other_devices(x, y, c):
    flip = lambda v, bit: 1 - v if bit else v
    return [(flip(x, k >> 2 & 1), flip(y, k >> 1 & 1), flip(c, k & 1)) for k in range(1, NDEV)]


SEM_SPEC = pl.BlockSpec(memory_space=pltpu.SEMAPHORE)
START_EFFECT = pltpu.SideEffectType.DATAFLOW_SIDE_EFFECTING


def _send_start(arrays, scatter, name):
    n = len(arrays)
    lands = [lax.empty((NDEV,) + (a.shape[1:] if scatter else a.shape), a.dtype) for a in arrays]

    def body(*refs):
        xs, ls = refs[:n], refs[n:2 * n]
        send_sems, recv_sems, own_sems, token = refs[2 * n], refs[2 * n + 1], refs[2 * n + 2], refs[4 * n + 3]
        x, y, c = lax.axis_index("x"), lax.axis_index("y"), lax.axis_index("c")
        me = 4 * x + 2 * y + c
        for k, peer in enumerate(_other_devices(x, y, c)):
            for p in range(n):
                src = xs[p].at[4 * peer[0] + 2 * peer[1] + peer[2]] if scatter else xs[p]
                pltpu.make_async_remote_copy(
                    src_ref=src, dst_ref=ls[p].at[me], send_sem=send_sems.at[p * (NDEV - 1) + k],
                    recv_sem=recv_sems.at[p * (NDEV - 1) + k], device_id=peer, device_id_type=MESH_T).start()
        for p in range(n):
            pltpu.make_async_copy(xs[p].at[me] if scatter else xs[p], ls[p].at[me], own_sems.at[p]).start()
        token[...] = jnp.zeros(token.shape, F32)

    sems = pltpu.SemaphoreType.DMA((n * (NDEV - 1),))
    out_shape = ([sems, sems, pltpu.SemaphoreType.DMA((n,))]
                 + [pltpu.HBM(a.shape, a.dtype) for a in list(arrays) + lands] + [jax.ShapeDtypeStruct((8, 128), F32)])
    res = pl.pallas_call(
        body, name=name, out_shape=out_shape, in_specs=[HBM_SPEC] * (2 * n),
        out_specs=[SEM_SPEC] * 3 + [HBM_SPEC] * (2 * n) + [pl.BlockSpec(memory_space=pltpu.VMEM)],
        input_output_aliases={i: 3 + i for i in range(2 * n)},
        compiler_params=pltpu.CompilerParams(has_side_effects=START_EFFECT))(
            *[pltpu.with_memory_space_constraint(a, pltpu.HBM) for a in list(arrays) + lands])
    return res[:3], res[3:3 + n], res[3 + n:3 + 2 * n], res[3 + 2 * n]


def _send_wait(started, scatter, after, name):
    sems, arrays, lands, _ = started
    n = len(arrays)

    def body(*refs):
        xs, ls = refs[:n], refs[n:2 * n]
        send, recv, own = refs[2 * n:2 * n + 3]
        x, y, c = lax.axis_index("x"), lax.axis_index("y"), lax.axis_index("c")
        me = 4 * x + 2 * y + c
        for p in range(n):
            pltpu.make_async_copy(xs[p].at[me] if scatter else xs[p], ls[p].at[me], own.at[p]).wait()
        for k, peer in enumerate(_other_devices(x, y, c)):
            slot = 4 * peer[0] + 2 * peer[1] + peer[2]
            for p in range(n):
                cp = pltpu.make_async_remote_copy(
                    src_ref=xs[p].at[slot] if scatter else xs[p], dst_ref=ls[p].at[slot],
                    send_sem=send.at[p * (NDEV - 1) + k], recv_sem=recv.at[p * (NDEV - 1) + k], device_id=peer,
                    device_id_type=MESH_T)
                cp.wait_send()
                cp.wait_recv()

    res = pl.pallas_call(
        body, name=name, out_shape=[pltpu.HBM(a.shape, a.dtype) for a in list(arrays) + list(lands)],
        in_specs=[HBM_SPEC] * (2 * n) + [SEM_SPEC] * 3 + [pl.BlockSpec(memory_space=pl.ANY)],
        out_specs=[HBM_SPEC] * (2 * n), input_output_aliases={i: i for i in range(2 * n)},
        compiler_params=pltpu.CompilerParams(has_side_effects=START_EFFECT))(
            *arrays, *lands, *sems, after)
    return res[n:]


def _row_block(rows, cols, itemsize=4, target=2**20):
    best = None
    for tr in range(16, rows + 1, 16):
        if rows % tr == 0 and tr * cols * itemsize <= target:
            best = tr
    return best or rows


def _adamw(w, m, v, gparts, name):
    parts, rows, cols = gparts.shape
    tr = _row_block(rows, cols, target=2**19)
    bc1 = 1.0 - ADAM_B1 ** ADAM_STEP
    bc2 = 1.0 - ADAM_B2 ** ADAM_STEP

    def body(w_ref, m_ref, v_ref, g_ref, go_ref, d_ref, mo_ref, vo_ref):
        g = g_ref[0].astype(F32)
        for k in range(1, parts):
            g = g + g_ref[k].astype(F32)
        mn = ADAM_B1 * m_ref[...] + (1.0 - ADAM_B1) * g
        vn = ADAM_B2 * v_ref[...] + (1.0 - ADAM_B2) * (g * g)
        go_ref[...] = g
        mo_ref[...] = mn
        vo_ref[...] = vn
        d_ref[...] = -ADAM_LR * ((mn / bc1) / (jnp.sqrt(vn / bc2) + ADAM_EPS) + ADAM_WD * w_ref[...])

    blk = pl.BlockSpec((tr, cols), lambda i: (i, 0))
    shp = jax.ShapeDtypeStruct((rows, cols), F32)
    return pl.pallas_call(
        body, name=name, grid=(rows // tr,),
        in_specs=[blk, blk, blk, pl.BlockSpec((parts, tr, cols), lambda i: (0, i, 0))],
        out_specs=[blk] * 4, out_shape=[shp] * 4, compiler_params=_cp(("parallel",)))(w, m, v, gparts)


def _pack(arrays, rows):
    flat = jnp.concatenate([a.reshape(-1).astype(F32) for a in arrays])
    return jnp.pad(flat, (0, rows * PACK_COLS - flat.shape[0])).reshape(rows, PACK_COLS)


def _unpack(buf, shapes):
    flat = buf.reshape(-1)
    out, off = [], 0
    for shp in shapes:
        size = int(np.prod(shp))
        out.append(flat[off:off + size].reshape(shp))
        off += size
    return out


REPL_SHAPES = {'mix_norm_g': (2, 1024), 'ffn_norm_g': (2, 1024), 'final_norm_g': (1024,), 's5_lam_re': (1, 16, 64),
               's5_lam_im': (1, 16, 64), 's5_log_dt': (1, 16), 's5_b_re': (1, 16, 64, 16), 's5_b_im': (1, 16, 64, 16),
               's5_c_re': (1, 16, 16, 64), 's5_c_im': (1, 16, 16, 64), 's5_d': (1, 256), 's5_b_glu': (1, 256),
               'gm_w_s': (1, 6, 128, 128), 'gm_b_s': (1, 6, 128), 'gm_v_g': (1, 768), 'ffn_conv_b': (2, 5632)}
REPL_ELEMS = sum(int(np.prod(REPL_SHAPES[n])) for n in REPL_ORDER)
REPL_ROWS = -(-REPL_ELEMS // (PACK_COLS * 8)) * 8

GATHER_DTYPE = {'ev_w_in': BF, 'ev_w_out': BF, 's5_w_glu': BF, 'od_w_in': BF, 'od_conv_w': F32, 'od_conv_b': F32,
                'od_w_out': BF, 'ffn_w_up': BF, 'ffn_conv_w': F32, 'ffn_w_down': BF}
GATHER_EVEN = ['ev_w_in', 'ev_w_out', 's5_w_glu']
GATHER_FFN0 = ['ffn_w_up0', 'ffn_conv_w0', 'ffn_w_down0']
GATHER_REST = ['od_w_in', 'od_conv_w', 'od_conv_b', 'od_w_out', 'ffn_w_up1', 'ffn_conv_w1', 'ffn_w_down1']

def _squeeze_lead(a):
    return a.reshape(a.shape[1:]) if a.shape[0] == 1 and a.ndim > 2 else a


def kernel(x, mix_norm_g, ffn_norm_g, final_norm_g, ev_w_in, ev_w_out, s5_lam_re, s5_lam_im, s5_log_dt, s5_b_re, s5_b_im, s5_c_re, s5_c_im, s5_d, s5_w_glu, s5_b_glu, gm_w_s, gm_b_s, gm_v_g, od_w_in, od_conv_w, od_conv_b, od_w_out, ffn_w_up, ffn_conv_w, ffn_conv_b, ffn_w_down, loss_target, m_mix_norm_g, m_ffn_norm_g, m_final_norm_g, m_ev_w_in, m_ev_w_out, m_s5_lam_re, m_s5_lam_im, m_s5_log_dt, m_s5_b_re, m_s5_b_im, m_s5_c_re, m_s5_c_im, m_s5_d, m_s5_w_glu, m_s5_b_glu, m_gm_w_s, m_gm_b_s, m_gm_v_g, m_od_w_in, m_od_conv_w, m_od_conv_b, m_od_w_out, m_ffn_w_up, m_ffn_conv_w, m_ffn_conv_b, m_ffn_w_down, v_mix_norm_g, v_ffn_norm_g, v_final_norm_g, v_ev_w_in, v_ev_w_out, v_s5_lam_re, v_s5_lam_im, v_s5_log_dt, v_s5_b_re, v_s5_b_im, v_s5_c_re, v_s5_c_im, v_s5_d, v_s5_w_glu, v_s5_b_glu, v_gm_w_s, v_gm_b_s, v_gm_v_g, v_od_w_in, v_od_conv_w, v_od_conv_b, v_od_w_out, v_ffn_w_up, v_ffn_conv_w, v_ffn_conv_b, v_ffn_w_down):
    given = dict(locals())
    weights = {n: given[n] for n in WEIGHT_ORDER}
    nseq, seq, _ = x.shape

    send = {}
    for name in SHARDED_ORDER:
        a = weights[name].astype(GATHER_DTYPE[name])
        if a.shape[0] == 2:
            send[name + '0'], send[name + '1'] = a[0], a[1]
        else:
            send[name] = _squeeze_lead(a)
    gathers = [_send_start([send[n] for n in names], False, f"gather_{tag}_start")
               for tag, names in (("ffn0", GATHER_FFN0), ("rest", GATHER_REST))]
    token = gathers[0][3] + gathers[1][3]

    def waiter(tag, names, started):
        return lambda after: dict(zip(names, _send_wait(started, False, after, f"gather_{tag}_wait")))

    gathered = dict(zip(GATHER_EVEN, _all_gather([send[n] for n in GATHER_EVEN], [0] * len(GATHER_EVEN),
                                                 "gather_even")))

    scatters = []

    def scatter(tag, grads):
        names = list(grads)
        started = _send_start([grads[n].astype(BF) for n in names], True, f"{tag}_start")
        scatters.append((tag, names, started))
        return started[3]

    loss_row, grad_x, g_repl = _local_step(
        x.reshape(nseq * seq, D), loss_target.reshape(nseq * seq, D), weights, gathered,
        waiter("ffn0", GATHER_FFN0, gathers[0]), waiter("rest", GATHER_REST, gathers[1]), token, scatter, seq)
    loss = lax.psum(loss_row[0, 0], ("x", "y", "c"))

    parts = {}
    for tag, names, started in scatters:
        parts.update(zip(names, _send_wait(started, True, grad_x, f"{tag}_wait")))
    repl_parts = _all_gather([_pack([g_repl[n] for n in REPL_ORDER], REPL_ROWS)], [0], "gather_small_grads")[0]

    out = {}
    for name in SHARDED_ORDER:
        w = weights[name]
        if name + '0' in parts:
            gp = jnp.stack([parts[name + '0'], parts[name + '1']], axis=1)
        else:
            gp = parts[name]
        to_rows = lambda a: a.reshape(-1, w.shape[-1])
        res = _adamw(to_rows(w), to_rows(given["m_" + name]), to_rows(given["v_" + name]),
                     gp.reshape(NDEV, -1, w.shape[-1]), f"adamw_{name}")
        out[name] = [r.reshape(w.shape) for r in res]
    rp = _adamw(_pack([weights[n] for n in REPL_ORDER], REPL_ROWS),
                _pack([given["m_" + n] for n in REPL_ORDER], REPL_ROWS),
                _pack([given["v_" + n] for n in REPL_ORDER], REPL_ROWS), repl_parts, "adamw_replicated")
    rp_shapes = [weights[n].shape for n in REPL_ORDER]
    for k in range(4):
        for name, a in zip(REPL_ORDER, _unpack(rp[k], rp_shapes)):
            out.setdefault(name, [None] * 4)[k] = a
    results = [[out[n][k] for n in WEIGHT_ORDER] for k in range(4)]
    grad_w, delta_w, new_m, new_v = results
    return (loss, grad_x.reshape(nseq, seq, D), *grad_w, *delta_w, *new_m, *new_v)
```

```python
import math

import jax
import jax.numpy as jnp
import numpy as np
from jax import lax
from jax.experimental import pallas as pl
from jax.experimental.pallas import tpu as pltpu

F32 = jnp.float32
BF = jnp.bfloat16

D = 1024
DFF = 2816
NDEV = 8
SSM_W = 256
SSM_G = 16
SSM_H = 16
SSM_P = 64
NST = SSM_G * SSM_P
GM_W = 768
GM_HEADS = 6
CHUNK = 128
EPS = 1e-6
LAM_MAX = -1e-4
FB = 256
FSH = 2 * DFF // NDEV
ROW_BLOCK = 512
CONV_ROW_BLOCK = 1024
VMEM_LIMIT = 48 * 2**20
PACK_COLS = 1024
MESH_T = pl.DeviceIdType.MESH

ADAM_LR = 0.001
ADAM_B1 = 0.9
ADAM_B2 = 0.999
ADAM_EPS = 1e-08
ADAM_WD = 0.01
ADAM_STEP = 10

WEIGHT_ORDER = ['mix_norm_g', 'ffn_norm_g', 'final_norm_g', 'ev_w_in', 'ev_w_out', 's5_lam_re', 's5_lam_im',
                's5_log_dt', 's5_b_re', 's5_b_im', 's5_c_re', 's5_c_im', 's5_d', 's5_w_glu', 's5_b_glu', 'gm_w_s',
                'gm_b_s', 'gm_v_g', 'od_w_in', 'od_conv_w', 'od_conv_b', 'od_w_out', 'ffn_w_up', 'ffn_conv_w',
                'ffn_conv_b', 'ffn_w_down']
SHARDED = {'ev_w_in': ((1, 1024, 1792), 2), 'ev_w_out': ((1, 1024, 1024), 1), 's5_w_glu': ((1, 256, 256), 1),
           'od_w_in': ((1, 1024, 3072), 2), 'od_conv_w': ((1, 3, 1024), 2), 'od_conv_b': ((1, 1024), 1),
           'od_w_out': ((1, 1024, 1024), 1), 'ffn_w_up': ((2, 1024, 5632), 2), 'ffn_conv_w': ((2, 3, 5632), 2),
           'ffn_w_down': ((2, 2816, 1024), 1)}
SHARDED_ORDER = [n for n in WEIGHT_ORDER if n in SHARDED]
REPL_ORDER = [n for n in WEIGHT_ORDER if n not in SHARDED]


def _cp(sem):
    return pltpu.CompilerParams(dimension_semantics=sem, vmem_limit_bytes=VMEM_LIMIT)


def _sigmoid(x):
    return 1.0 / (1.0 + jnp.exp(-x))


_GELU_K = math.sqrt(2.0 / math.pi)


def _gelu(x):
    return 0.5 * x * (1.0 + jnp.tanh(_GELU_K * (x + 0.044715 * x * x * x)))


def _gelu_grad(x):
    t = jnp.tanh(_GELU_K * (x + 0.044715 * x * x * x))
    return 0.5 * (1.0 + t) + 0.5 * x * (1.0 - t * t) * _GELU_K * (1.0 + 3.0 * 0.044715 * x * x)


def _colsum(x):
    return jnp.sum(x, axis=0, keepdims=True)


def _accumulate(ref, first, part):
    @pl.when(first)
    def _():
        ref[...] = part

    @pl.when(jnp.logical_not(first))
    def _():
        ref[...] += part


_DIMS = {'nn': (((1,), (0,)), ((), ())), 'nt': (((1,), (1,)), ((), ())), 'tn': (((0,), (0,)), ((), ()))}


def _rms(xv):
    r = lax.rsqrt(jnp.mean(xv * xv, axis=-1, keepdims=True) + EPS)
    return r, xv * r


def _norm_grad(dyv, gv, r, xh):
    dyg = dyv * gv
    return r * (dyg - xh * jnp.mean(dyg * xh, axis=-1, keepdims=True))


def _tail_io(tail, tm, index, n):
    rows = pl.BlockSpec((tm, D), index)
    vec = pl.BlockSpec((1, D), lambda *_: (0, 0))
    full, half, gain = (jax.ShapeDtypeStruct((n, D), F32), jax.ShapeDtypeStruct((n, D), BF),
                        jax.ShapeDtypeStruct((1, D), F32))
    if tail[0] == 'norm_fwd':
        return [tail[1]], [vec], [rows, rows], [full, half]
    if tail[0] == 'norm_bwd':
        return list(tail[1:]), [rows, vec, rows], [rows, rows, vec], [full, half, gain]
    return (list(tail[1:]), [vec, rows], [pl.BlockSpec((1, 128), lambda *_: (0, 0)), rows, rows, vec],
            [jax.ShapeDtypeStruct((1, 128), F32), full, half, gain])


def _tail_apply(kind, tot, tail_refs, outs, first):
    if kind == 'norm_fwd':
        r, xh = _rms(tot)
        outs[0][...] = tot
        outs[1][...] = (xh * tail_refs[0][...]).astype(BF)
    elif kind == 'norm_bwd':
        x_ref, g_ref, dr_ref = tail_refs
        r, xh = _rms(x_ref[...])
        dx = dr_ref[...] + _norm_grad(tot, g_ref[...], r, xh)
        outs[0][...] = dx
        outs[1][...] = dx.astype(BF)
        _accumulate(outs[2], first, _colsum(tot * xh))
    else:
        gv = tail_refs[0][...]
        r, xh = _rms(tot)
        err = xh * gv - tail_refs[1][...]
        part = 0.5 * jnp.sum(jnp.mean(err * err, axis=-1, keepdims=True), axis=0, keepdims=True)
        _accumulate(outs[0], first, jnp.broadcast_to(part, (1, 128)))
        dyv = err * (1.0 / D)
        dx = _norm_grad(dyv, gv, r, xh)
        outs[1][...] = dx
        outs[2][...] = dx.astype(BF)
        _accumulate(outs[3], first, _colsum(dyv * xh))


def _matmul(a, b, mode, tm, tn, tk, name, resid=None, out_dtype=F32, tail=None):
    if mode == 'tn':
        kdim, m = a.shape
    else:
        m, kdim = a.shape
    n = b.shape[0] if mode == 'nt' else b.shape[1]
    tm, tn, tk = min(tm, m), min(tn, n), min(tk, kdim)
    assert m % tm == 0 and n % tn == 0 and kdim % tk == 0, (name, m, n, kdim, tm, tn, tk)
    a_spec = (pl.BlockSpec((tk, tm), lambda i, j, k: (k, i)) if mode == 'tn'
              else pl.BlockSpec((tm, tk), lambda i, j, k: (i, k)))
    b_spec = (pl.BlockSpec((tn, tk), lambda i, j, k: (j, k)) if mode == 'nt'
              else pl.BlockSpec((tk, tn), lambda i, j, k: (k, j)))
    o_spec = pl.BlockSpec((tm, tn), lambda i, j, k: (i, j))
    return _matmul_spec(a, b, mode, (m // tm, n // tn, kdim // tk), a_spec, b_spec, o_spec, (m, n), name,
                        resid=resid, out_dtype=out_dtype, tail=tail)


def _matmul_spec(a, b, mode, grid, a_spec, b_spec, o_spec, out_shape, name, resid=None, out_dtype=F32, tail=None):
    nk = grid[2]
    tm, tn = o_spec.block_shape[-2:]
    dims = _DIMS[mode]
    has_resid = resid is not None
    operands = [a, b] + ([resid] if has_resid else [])
    in_specs = [a_spec, b_spec] + ([o_spec] if has_resid else [])
    out_specs, out_shapes = [o_spec], [jax.ShapeDtypeStruct(out_shape, out_dtype)]
    n_tail = 0
    if tail is not None:
        assert tn == D and grid[1] == 1, name
        extra, extra_specs, out_specs, out_shapes = _tail_io(tail, tm, lambda i, j, k: (i, 0), out_shape[0])
        operands, in_specs, n_tail = operands + extra, in_specs + extra_specs, len(extra)
    n_in, n_out = len(operands), len(out_specs)

    def body(*refs):
        ins, outs = refs[:n_in], refs[n_in:n_in + n_out]
        a_ref, b_ref = ins[:2]
        part = lax.dot_general(a_ref[...].astype(BF), b_ref[...].astype(BF), dims, preferred_element_type=F32)

        def finish(tot):
            if has_resid:
                tot = tot + ins[2][...]
            if tail is not None:
                _tail_apply(tail[0], tot, ins[n_in - n_tail:], outs, pl.program_id(0) == 0)
            else:
                outs[0][...] = tot.astype(out_dtype)

        if nk == 1:
            finish(part)
        else:
            acc = refs[-1]
            k = pl.program_id(2)

            @pl.when(k == 0)
            def _():
                acc[...] = part

            @pl.when(k > 0)
            def _():
                acc[...] += part

            @pl.when(k == nk - 1)
            def _():
                finish(acc[...])

    res = pl.pallas_call(
        body, name=name, grid=grid, in_specs=in_specs, out_specs=out_specs, out_shape=out_shapes,
        scratch_shapes=[pltpu.VMEM((tm, tn), F32)] if nk > 1 else [],
        compiler_params=_cp(("arbitrary",) * 3 if tail is not None else ("parallel", "parallel", "arbitrary")))(*operands)
    return res if tail is not None else res[0]


def _matmul_shards(a, b, mode, tm, tn, name, resid=None, out_dtype=F32, tail=None):
    shards, m, kdim = a.shape
    n = b.shape[2] if mode == 'nn' else b.shape[1]
    tm, tn = min(tm, m), min(tn, n)
    dims = _DIMS[mode]
    has_resid = resid is not None
    b_spec = (pl.BlockSpec((shards, kdim, tn), lambda i, j: (0, 0, j)) if mode == 'nn'
              else pl.BlockSpec((shards, tn, kdim), lambda i, j: (0, j, 0)))
    o_spec = pl.BlockSpec((tm, tn), lambda i, j: (i, j))
    operands = [a, b] + ([resid] if has_resid else [])
    in_specs = [pl.BlockSpec((shards, tm, kdim), lambda i, j: (0, i, 0)), b_spec] + ([o_spec] if has_resid else [])
    out_specs, out_shapes = [o_spec], [jax.ShapeDtypeStruct((m, n), out_dtype)]
    n_tail = 0
    if tail is not None:
        assert tn == D and n == D, name
        extra, extra_specs, out_specs, out_shapes = _tail_io(tail, tm, lambda i, j: (i, 0), m)
        operands, in_specs, n_tail = operands + extra, in_specs + extra_specs, len(extra)
    n_in, n_out = len(operands), len(out_specs)

    def body(*refs):
        ins, outs = refs[:n_in], refs[n_in:n_in + n_out]
        acc = lax.dot_general(ins[0][0], ins[1][0], dims, preferred_element_type=F32)
        for s in range(1, shards):
            acc = acc + lax.dot_general(ins[0][s], ins[1][s], dims, preferred_element_type=F32)
        if has_resid:
            acc = acc + ins[2][...]
        if tail is not None:
            _tail_apply(tail[0], acc, ins[n_in - n_tail:], outs, pl.program_id(0) == 0)
        else:
            outs[0][...] = acc.astype(out_dtype)

    res = pl.pallas_call(
        body, name=name, grid=(m // tm, n // tn), in_specs=in_specs, out_specs=out_specs, out_shape=out_shapes,
        compiler_params=_cp(("arbitrary", "arbitrary") if tail is not None else ("parallel", "parallel")))(*operands)
    return res if tail is not None else res[0]


def _matmul_tn_shards(a, b, sp, tk, name, out_dtype=BF):
    a_sh, b_sh = a.ndim == 3, b.ndim == 3
    shards = a.shape[0] if a_sh else b.shape[0]
    kdim, m, n = a.shape[-2], a.shape[-1], b.shape[-1]
    nk = kdim // tk

    def body(a_ref, b_ref, o_ref, acc):
        k = pl.program_id(1)
        parts = [lax.dot_general(a_ref[q] if a_sh else a_ref[...], b_ref[q] if b_sh else b_ref[...], _DIMS['tn'],
                                 preferred_element_type=F32) for q in range(sp)]

        @pl.when(k == 0)
        def _():
            for q in range(sp):
                acc[q] = parts[q]

        @pl.when(k > 0)
        def _():
            for q in range(sp):
                acc[q] += parts[q]

        @pl.when(k == nk - 1)
        def _():
            o_ref[...] = acc[...].astype(out_dtype)

    spec = lambda sharded, cols: (pl.BlockSpec((sp, tk, cols), lambda s, k: (s, k, 0)) if sharded
                                  else pl.BlockSpec((tk, cols), lambda s, k: (k, 0)))
    return pl.pallas_call(
        body, name=name, grid=(shards // sp, nk), in_specs=[spec(a_sh, m), spec(b_sh, n)],
        out_specs=pl.BlockSpec((sp, m, n), lambda s, k: (s, 0, 0)),
        out_shape=jax.ShapeDtypeStruct((shards, m, n), out_dtype),
        scratch_shapes=[pltpu.VMEM((sp, m, n), F32)],
        compiler_params=_cp(("parallel", "arbitrary")))(a, b)


def _rmsnorm_fwd(x, g, name):
    n = x.shape[0]
    tm = min(512, n)

    def body(x_ref, g_ref, o_ref):
        xv = x_ref[...]
        r = lax.rsqrt(jnp.mean(xv * xv, axis=-1, keepdims=True) + EPS)
        o_ref[...] = (xv * r * g_ref[...]).astype(BF)

    return pl.pallas_call(
        body, name=name, grid=(n // tm,),
        in_specs=[pl.BlockSpec((tm, D), lambda i: (i, 0)), pl.BlockSpec((1, D), lambda i: (0, 0))],
        out_specs=pl.BlockSpec((tm, D), lambda i: (i, 0)),
        out_shape=jax.ShapeDtypeStruct((n, D), BF), compiler_params=_cp(("parallel",)))(x, g)


def _halo_maps(tm, n_rows):
    r8 = tm // 8
    last = n_rows // 8 - 1
    prev = lambda i: jnp.maximum(i * r8 - 1, 0)
    nxt = lambda i: jnp.minimum((i + 1) * r8, last)
    return prev, nxt


def _lane_blocks(width):
    return [slice(lo, min(lo + 128, width)) for lo in range(0, width, 128)]


def _conv_taps(w_ref, b_ref, g, lanes):
    return w_ref[g, 0:1, lanes], w_ref[g, 1:2, lanes], w_ref[g, 2:3, lanes], b_ref[g, :, lanes]


def _conv_tile(x, prev1, prev2, taps, row):
    w0, w1, w2, b = taps
    r1 = pltpu.roll(x, 1, 0)
    r2 = pltpu.roll(x, 2, 0)
    x1 = jnp.where(row == 0, prev1, r1)
    x2 = jnp.where(row < 2, prev2, r2)
    return b + w0 * x2 + w1 * x1 + w2 * x, x1, x2, r1, r2


def _halo16_maps(tm, n_rows):
    r16 = tm // 16
    last = n_rows // 16 - 1
    return (lambda i: jnp.maximum(i * r16 - 1, 0)), (lambda i: jnp.minimum((i + 1) * r16, last))


def _ffn_conv_fwd(up, cw, cb, seq, name):
    n = up.shape[2]
    tm = min(CONV_ROW_BLOCK, seq)
    prev, _ = _halo16_maps(tm, n)

    def body(u_ref, h_ref, w_ref, b_ref, o_ref, d_ref):
        i = pl.program_id(1)
        scale = jnp.where(lax.rem(i * tm, seq) == 0, 0.0, 1.0)
        for lanes in _lane_blocks(FSH):
            lw = lanes.stop - lanes.start
            row = lax.broadcasted_iota(jnp.int32, (8, lw), 0)
            taps = [_conv_taps(w_ref, b_ref, g, lanes) for g in range(2)]

            def tile(xs, carry):
                hc, nxt = [], []
                for g in range(2):
                    conv, _, _, r1, r2 = _conv_tile(xs[g], carry[2 * g], carry[2 * g + 1], taps[g], row)
                    hc.append(conv)
                    nxt += [r1, r2]
                s = _sigmoid(hc[0])
                silu = hc[0] * s
                return (silu * hc[1], hc[1] * (s * (1.0 + hc[0] * (1.0 - s))), silu), tuple(nxt)

            carry = []
            for g in range(2):
                halo = h_ref[g, :, lanes].astype(F32)[8:] * scale
                carry += [pltpu.roll(halo, 1, 0), pltpu.roll(halo, 2, 0)]
            carry = tuple(carry)
            for m in range(tm // 16):
                rows = slice(m * 16, m * 16 + 16)
                x16 = [u_ref[g, rows, lanes].astype(F32) for g in range(2)]
                a, carry = tile([x[:8] for x in x16], carry)
                b, carry = tile([x[8:] for x in x16], carry)
                o_ref[rows, lanes] = jnp.concatenate([a[0], b[0]], axis=0).astype(BF)
                d_ref[0, rows, lanes] = jnp.concatenate([a[1], b[1]], axis=0).astype(BF)
                d_ref[1, rows, lanes] = jnp.concatenate([a[2], b[2]], axis=0).astype(BF)

    return pl.pallas_call(
        body, name=name, grid=(4, n // tm),
        in_specs=[pl.BlockSpec((2, None, tm, FSH), lambda j, i: (0, j, i, 0)),
                  pl.BlockSpec((2, None, 16, FSH), lambda j, i: (0, j, prev(i), 0)),
                  pl.BlockSpec((2, None, 3, FSH), lambda j, i: (0, j, 0, 0)),
                  pl.BlockSpec((2, None, 1, FSH), lambda j, i: (0, j, 0, 0))],
        out_specs=[pl.BlockSpec((None, tm, FSH), lambda j, i: (j, i, 0)),
                   pl.BlockSpec((2, None, tm, FSH), lambda j, i: (0, j, i, 0))],
        out_shape=[jax.ShapeDtypeStruct((4, n, FSH), BF), jax.ShapeDtypeStruct((2, 4, n, FSH), BF)],
        compiler_params=_cp(("parallel", "parallel")))(up, up, cw, cb)


def _ffn_conv_bwd(up, dgate, dact, cw, seq, name):
    n = up.shape[2]
    tm = min(CONV_ROW_BLOCK, seq)
    _, nxt = _halo16_maps(tm, n)

    def body(u_ref, g_ref, gn_ref, da_ref, dn_ref, w_ref, du_ref, dw_ref, db_ref):
        i = pl.program_id(1)
        sn = jnp.where(lax.rem((i + 1) * tm, seq) == 0, 0.0, 1.0)
        first = i == 0
        for lanes in _lane_blocks(FSH):
            lw = lanes.stop - lanes.start
            row = lax.broadcasted_iota(jnp.int32, (8, lw), 0)
            taps = [(w_ref[g, 0:1, lanes], w_ref[g, 1:2, lanes], w_ref[g, 2:3, lanes]) for g in range(2)]

            def dconv(gs, da):
                ds = [gs[g] * da for g in range(2)]
                return [(d, pltpu.roll(d, 7, 0), pltpu.roll(d, 6, 0)) for d in ds]

            def finish(cur, after, xs, sums):
                dups, new_sums = [], []
                for g in range(2):
                    w0, w1, w2 = taps[g]
                    s1 = jnp.where(row == 7, after[g][1], cur[g][1])
                    s2 = jnp.where(row >= 6, after[g][2], cur[g][2])
                    dups.append(w2 * cur[g][0] + w1 * s1 + w0 * s2)
                    acc = sums[g]
                    new_sums.append((acc[0] + xs[g] * s2, acc[1] + xs[g] * s1, acc[2] + xs[g] * cur[g][0],
                                     acc[3] + cur[g][0]))
                return dups, new_sums

            def emit(m, held, after, sums):
                (ta, xa), (tb, xb) = held
                dup_a, sums = finish(ta, tb, xa, sums)
                dup_b, sums = finish(tb, after, xb, sums)
                for g in range(2):
                    du_ref[g, m * 16:m * 16 + 16, lanes] = jnp.concatenate([dup_a[g], dup_b[g]], axis=0).astype(BF)
                return sums

            zero = jnp.zeros((8, lw), F32)
            sums = [(zero,) * 4, (zero,) * 4]
            held = None
            for m in range(tm // 16):
                rows = slice(m * 16, m * 16 + 16)
                x16 = [u_ref[g, rows, lanes].astype(F32) for g in range(2)]
                g16 = [g_ref[g, rows, lanes].astype(F32) for g in range(2)]
                d16 = da_ref[rows, lanes].astype(F32)
                ta = dconv([a[:8] for a in g16], d16[:8])
                tb = dconv([a[8:] for a in g16], d16[8:])
                if held is not None:
                    sums = emit(m - 1, held, ta, sums)
                held = ((ta, [x[:8] for x in x16]), (tb, [x[8:] for x in x16]))
            tn_ = dconv([gn_ref[g, :, lanes].astype(F32)[:8] for g in range(2)], dn_ref[:, lanes].astype(F32)[:8] * sn)
            sums = emit(tm // 16 - 1, held, tn_, sums)
            for g in range(2):
                for k in range(3):
                    _accumulate(dw_ref.at[g, k:k + 1, lanes], first, _colsum(sums[g][k]))
                _accumulate(db_ref.at[g, :, lanes], first, _colsum(sums[g][3]))

    return pl.pallas_call(
        body, name=name, grid=(4, n // tm),
        in_specs=[pl.BlockSpec((2, None, tm, FSH), lambda j, i: (0, j, i, 0)),
                  pl.BlockSpec((2, None, tm, FSH), lambda j, i: (0, j, i, 0)),
                  pl.BlockSpec((2, None, 16, FSH), lambda j, i: (0, j, nxt(i), 0)),
                  pl.BlockSpec((None, tm, FSH), lambda j, i: (j, i, 0)),
                  pl.BlockSpec((None, 16, FSH), lambda j, i: (j, nxt(i), 0)),
                  pl.BlockSpec((2, None, 3, FSH), lambda j, i: (0, j, 0, 0))],
        out_specs=[pl.BlockSpec((2, None, tm, FSH), lambda j, i: (0, j, i, 0)),
                   pl.BlockSpec((2, None, 3, FSH), lambda j, i: (0, j, 0, 0)),
                   pl.BlockSpec((2, None, 1, FSH), lambda j, i: (0, j, 0, 0))],
        out_shape=[jax.ShapeDtypeStruct((2, 4, n, FSH), BF), jax.ShapeDtypeStruct((2, 4, 3, FSH), F32),
                   jax.ShapeDtypeStruct((2, 4, 1, FSH), F32)],
        compiler_params=_cp(("parallel", "arbitrary")))(up, dgate, dgate, dact, dact, cw)


def _shortconv_fwd(p, cw, cb, seq, name):
    n = p.shape[0]
    tm = min(CONV_ROW_BLOCK, seq)
    prev, _ = _halo16_maps(tm, n)

    def body(p_ref, h_ref, w_ref, b_ref, o_ref):
        i = pl.program_id(1)
        scale = jnp.where(lax.rem(i * tm, seq) == 0, 0.0, 1.0)
        q = p_ref[:, FB:2 * FB].astype(F32) * p_ref[:, 2 * FB:].astype(F32)
        row = lax.broadcasted_iota(jnp.int32, q.shape, 0)
        hq = (h_ref[:, FB:2 * FB].astype(F32) * h_ref[:, 2 * FB:].astype(F32))[8:] * scale
        hrow = lax.broadcasted_iota(jnp.int32, hq.shape, 0)
        h7 = _colsum(jnp.where(hrow == 7, hq, 0.0))
        h6 = _colsum(jnp.where(hrow == 6, hq, 0.0))
        p1 = jnp.where(row == 0, h7, pltpu.roll(q, 1, 0))
        p2 = jnp.where(row == 0, h6, jnp.where(row == 1, h7, pltpu.roll(q, 2, 0)))
        conv = b_ref[...] + w_ref[0:1, :] * p2 + w_ref[1:2, :] * p1 + w_ref[2:3, :] * q
        o_ref[...] = (p_ref[:, :FB].astype(F32) * conv).astype(BF)

    return pl.pallas_call(
        body, name=name, grid=(D // FB, n // tm),
        in_specs=[pl.BlockSpec((tm, 3 * FB), lambda j, i: (i, j)),
                  pl.BlockSpec((16, 3 * FB), lambda j, i: (prev(i), j)),
                  pl.BlockSpec((3, FB), lambda j, i: (0, j)),
                  pl.BlockSpec((1, FB), lambda j, i: (0, j))],
        out_specs=pl.BlockSpec((tm, FB), lambda j, i: (i, j)),
        out_shape=jax.ShapeDtypeStruct((n, D), BF), compiler_params=_cp(("parallel", "parallel")))(p, p, cw, cb)


def _shortconv_bwd(p, dmix, cw, cb, seq, name):
    n = p.shape[0]
    tm = min(CONV_ROW_BLOCK, seq)
    ext = tm + 16
    prev, nxt = _halo16_maps(tm, n)

    def body(p_ref, pp_ref, pn_ref, dm_ref, dn_ref, w_ref, b_ref, dp_ref, dw_ref, db_ref, qx, cx):
        i = pl.program_id(1)
        sp = jnp.where(lax.rem(i * tm, seq) == 0, 0.0, 1.0)
        sn = jnp.where(lax.rem((i + 1) * tm, seq) == 0, 0.0, 1.0)
        bg, cg, hx = (p_ref[:, :FB].astype(F32), p_ref[:, FB:2 * FB].astype(F32), p_ref[:, 2 * FB:].astype(F32))
        dm = dm_ref[...].astype(F32)
        qx[0:8, :] = (pp_ref[:, FB:2 * FB].astype(F32) * pp_ref[:, 2 * FB:].astype(F32))[8:] * sp
        qx[8:8 + tm, :] = cg * hx
        qx[8 + tm:, :] = jnp.zeros((8, FB), F32)
        cx[0:8, :] = jnp.zeros((8, FB), F32)
        cx[8:8 + tm, :] = dm * bg
        cx[8 + tm:, :] = (dn_ref[...].astype(F32) * pn_ref[:, :FB].astype(F32))[:8] * sn
        q0 = qx[...]
        q1 = pltpu.roll(q0, 1, 0)
        q2 = pltpu.roll(q0, 2, 0)
        main = slice(8, 8 + tm)
        conv = b_ref[...] + w_ref[0:1, :] * q2[main] + w_ref[1:2, :] * q1[main] + w_ref[2:3, :] * q0[main]
        dc = cx[...]
        dq = (w_ref[2:3, :] * dc + w_ref[1:2, :] * pltpu.roll(dc, ext - 1, 0)
              + w_ref[0:1, :] * pltpu.roll(dc, ext - 2, 0))[main]
        dp_ref[:, :FB] = (dm * conv).astype(BF)
        dp_ref[:, FB:2 * FB] = (dq * hx).astype(BF)
        dp_ref[:, 2 * FB:] = (dq * cg).astype(BF)
        first = i == 0
        dcm = dc[main]
        _accumulate(dw_ref.at[0:1, :], first, _colsum(dcm * q2[main]))
        _accumulate(dw_ref.at[1:2, :], first, _colsum(dcm * q1[main]))
        _accumulate(dw_ref.at[2:3, :], first, _colsum(dcm * q0[main]))
        _accumulate(db_ref, first, _colsum(dcm))

    return pl.pallas_call(
        body, name=name, grid=(D // FB, n // tm),
        in_specs=[pl.BlockSpec((tm, 3 * FB), lambda j, i: (i, j)),
                  pl.BlockSpec((16, 3 * FB), lambda j, i: (prev(i), j)),
                  pl.BlockSpec((16, 3 * FB), lambda j, i: (nxt(i), j)),
                  pl.BlockSpec((tm, FB), lambda j, i: (i, j)),
                  pl.BlockSpec((16, FB), lambda j, i: (nxt(i), j)),
                  pl.BlockSpec((3, FB), lambda j, i: (0, j)),
                  pl.BlockSpec((1, FB), lambda j, i: (0, j))],
        out_specs=[pl.BlockSpec((tm, 3 * FB), lambda j, i: (i, j)),
                   pl.BlockSpec((3, FB), lambda j, i: (0, j)),
                   pl.BlockSpec((1, FB), lambda j, i: (0, j))],
        out_shape=[jax.ShapeDtypeStruct((n, 3 * D), BF), jax.ShapeDtypeStruct((3, D), F32),
                   jax.ShapeDtypeStruct((1, D), F32)],
        scratch_shapes=[pltpu.VMEM((ext, FB), F32), pltpu.VMEM((ext, FB), F32)],
        compiler_params=_cp(("parallel", "arbitrary")))(p, p, p, dmix, dmix, cw, cb)


def _gmlp_fwd(uv, wm, bst, gv, seq, name):
    n = uv.shape[0]
    tm = min(ROW_BLOCK, seq)

    def body(x_ref, w_ref, b_ref, g_ref, o_ref):
        ge_v = _gelu(x_ref[:, GM_W:].astype(F32))
        r = lax.rsqrt(jnp.mean(ge_v * ge_v, axis=-1, keepdims=True) + EPS)
        vn = (ge_v * r * g_ref[...]).astype(BF)
        for c in range(tm // CHUNK):
            rows = slice(c * CHUNK, (c + 1) * CHUNK)
            for h in range(GM_HEADS):
                cols = slice(h * CHUNK, (h + 1) * CHUNK)
                gate = jnp.dot(w_ref[h], vn[rows, cols], preferred_element_type=F32) + b_ref[:, h:h + 1]
                o_ref[rows, cols] = (_gelu(x_ref[rows, cols].astype(F32)) * gate).astype(BF)

    return pl.pallas_call(
        body, name=name, grid=(n // tm,),
        in_specs=[pl.BlockSpec((tm, 2 * GM_W), lambda i: (i, 0)),
                  pl.BlockSpec((GM_HEADS, CHUNK, CHUNK), lambda i: (0, 0, 0)),
                  pl.BlockSpec((CHUNK, GM_HEADS), lambda i: (0, 0)),
                  pl.BlockSpec((1, GM_W), lambda i: (0, 0))],
        out_specs=pl.BlockSpec((tm, GM_W), lambda i: (i, 0)),
        out_shape=jax.ShapeDtypeStruct((n, GM_W), BF), compiler_params=_cp(("parallel",)))(uv, wm, bst, gv)


def _gmlp_bwd(uv, dout, wm, wmt, bst, gv, seq, name):
    n = uv.shape[0]
    tm = min(ROW_BLOCK, seq)

    def body(x_ref, do_ref, w_ref, wt_ref, b_ref, g_ref, dx_ref, dw_ref, db_ref, dg_ref, dvn_scr):
        first = pl.program_id(0) == 0
        ge_v = _gelu(x_ref[:, GM_W:].astype(F32))
        r = lax.rsqrt(jnp.mean(ge_v * ge_v, axis=-1, keepdims=True) + EPS)
        vh = ge_v * r
        vn = (vh * g_ref[...]).astype(BF)
        tril = (lax.broadcasted_iota(jnp.int32, (CHUNK, CHUNK), 0)
                >= lax.broadcasted_iota(jnp.int32, (CHUNK, CHUNK), 1))
        for h in range(GM_HEADS):
            cols = slice(h * CHUNK, (h + 1) * CHUNK)
            dw = jnp.zeros((CHUNK, CHUNK), F32)
            dbs = jnp.zeros((CHUNK, 1), F32)
            for c in range(tm // CHUNK):
                rows = slice(c * CHUNK, (c + 1) * CHUNK)
                blk = vn[rows, cols]
                gate = jnp.dot(w_ref[h], blk, preferred_element_type=F32) + b_ref[:, h:h + 1]
                xu = x_ref[rows, cols].astype(F32)
                do = do_ref[rows, cols].astype(F32)
                dx_ref[rows, cols] = (do * gate * _gelu_grad(xu)).astype(BF)
                dgate = do * _gelu(xu)
                dgb = dgate.astype(BF)
                dw = dw + lax.dot_general(dgb, blk, _DIMS['nt'], preferred_element_type=F32)
                dbs = dbs + jnp.sum(dgate, axis=1, keepdims=True)
                dvn_scr[rows, cols] = jnp.dot(wt_ref[h], dgb, preferred_element_type=F32)
            _accumulate(dw_ref.at[h], first, jnp.where(tril, dw, 0.0))
            _accumulate(db_ref.at[h], first, dbs)
        dvn = dvn_scr[...]
        _accumulate(dg_ref, first, _colsum(dvn * vh))
        dvh = dvn * g_ref[...]
        dv = r * (dvh - vh * jnp.mean(dvh * vh, axis=-1, keepdims=True))
        dx_ref[:, GM_W:] = (dv * _gelu_grad(x_ref[:, GM_W:].astype(F32))).astype(BF)

    full3 = pl.BlockSpec((GM_HEADS, CHUNK, CHUNK), lambda i: (0, 0, 0))
    return pl.pallas_call(
        body, name=name, grid=(n // tm,),
        in_specs=[pl.BlockSpec((tm, 2 * GM_W), lambda i: (i, 0)), pl.BlockSpec((tm, GM_W), lambda i: (i, 0)),
                  full3, full3, pl.BlockSpec((CHUNK, GM_HEADS), lambda i: (0, 0)),
                  pl.BlockSpec((1, GM_W), lambda i: (0, 0))],
        out_specs=[pl.BlockSpec((tm, 2 * GM_W), lambda i: (i, 0)), full3,
                   pl.BlockSpec((GM_HEADS, CHUNK, 1), lambda i: (0, 0, 0)),
                   pl.BlockSpec((1, GM_W), lambda i: (0, 0))],
        out_shape=[jax.ShapeDtypeStruct((n, 2 * GM_W), BF), jax.ShapeDtypeStruct((GM_HEADS, CHUNK, CHUNK), F32),
                   jax.ShapeDtypeStruct((GM_HEADS, CHUNK, 1), F32), jax.ShapeDtypeStruct((1, GM_W), F32)],
        scratch_shapes=[pltpu.VMEM((tm, GM_W), F32)],
        compiler_params=_cp(("arbitrary",)))(uv, dout, wm, wmt, bst, gv)


def _s5_disc(lam_re, lam_im, log_dt, b_re, b_im):
    lr = jnp.minimum(lam_re, LAM_MAX)
    li = lam_im
    dt = jnp.exp(log_dt)
    mag = jnp.exp(lr * dt)
    ab_re = mag * jnp.cos(li * dt)
    ab_im = mag * jnp.sin(li * dt)
    den = lr * lr + li * li
    nr = ab_re - 1.0
    ni = ab_im
    z_re = (nr * lr + ni * li) / den
    z_im = (ni * lr - nr * li) / den
    return ab_re, ab_im, z_re * b_re - z_im * b_im, z_re * b_im + z_im * b_re


def _s5_disc_fwd(args, name):
    shp = jax.ShapeDtypeStruct(args[0].shape, F32)

    def body(*refs):
        outs = _s5_disc(*[r[...] for r in refs[:5]])
        for o_ref, o in zip(refs[5:], outs):
            o_ref[...] = o

    return pl.pallas_call(body, name=name, out_shape=[shp] * 4)(*args)


def _s5_disc_bwd(args, cts, name):
    shp = jax.ShapeDtypeStruct(args[0].shape, F32)

    def body(*refs):
        _, vjp = jax.vjp(_s5_disc, *[r[...] for r in refs[:5]])
        grads = vjp(tuple(r[...] for r in refs[5:9]))
        for o_ref, o in zip(refs[9:], grads):
            o_ref[...] = o

    return pl.pallas_call(body, name=name, out_shape=[shp] * 5)(*args, *cts)


def _cmul(a, b):
    return a[0] * b[0] - a[1] * b[1], a[0] * b[1] + a[1] * b[0]


def _scan_tables(ar, ai, reverse):
    if reverse:
        ai = -ai
    a1 = (ar, ai)
    a2 = _cmul(a1, a1)
    a3 = _cmul(a2, a1)
    a4 = _cmul(a2, a2)
    powers = [a1, a2, a3, a4, _cmul(a4, a1), _cmul(a4, a2), _cmul(a4, a3), _cmul(a4, a4)]
    row = lax.broadcasted_iota(jnp.int32, (8, NST), 0)
    zero = jnp.zeros((8, NST), F32)
    pr, pi = zero, zero
    for r in range(8):
        pw = powers[7 - r] if reverse else powers[r]
        pr = jnp.where(row == r, pw[0], pr)
        pi = jnp.where(row == r, pw[1], pi)
    levels = []
    for d, pw in ((1, a1), (2, a2), (4, a4)):
        ok = (row <= 7 - d) if reverse else (row >= d)
        levels.append((d, jnp.where(ok, pw[0], zero), jnp.where(ok, pw[1], zero)))
    return (pr, pi), levels


def _scan_block(src, dst, car, tables, n_tiles, reverse):
    (pr, pi), levels = tables
    row = lax.broadcasted_iota(jnp.int32, (8, NST), 0)
    out_row = 0 if reverse else 7

    def step(t, carry):
        cr, ci = carry
        tile = (n_tiles - 1 - t) if reverse else t
        rows = pl.ds(pl.multiple_of(tile * 8, 8), 8)
        xr = src[rows, 0:NST]
        xi = src[rows, NST:2 * NST]
        for d, dr, di in levels:
            shift = 8 - d if reverse else d
            rr = pltpu.roll(xr, shift, 0)
            ri = pltpu.roll(xi, shift, 0)
            xr, xi = xr + dr * rr - di * ri, xi + dr * ri + di * rr
        hr = xr + pr * cr - pi * ci
        hi = xi + pr * ci + pi * cr
        dst[rows, 0:NST] = hr
        dst[rows, NST:2 * NST] = hi
        return (_colsum(jnp.where(row == out_row, hr, 0.0)), _colsum(jnp.where(row == out_row, hi, 0.0)))

    cr, ci = lax.fori_loop(0, n_tiles, step, (car[0:1, 0:NST], car[0:1, NST:2 * NST]))
    car[0:1, 0:NST] = cr
    car[0:1, NST:2 * NST] = ci


def _s5_fwd(u, ab, bbt, cmat, dvec, wglu, bglu, seq, name):
    n = u.shape[0]
    tm = min(ROW_BLOCK, seq)

    def body(u_ref, ab_ref, bb_ref, c_ref, d_ref, w_ref, b_ref, h_ref, o_ref, xs, car):
        i = pl.program_id(0)

        @pl.when(lax.rem(i * tm, seq) == 0)
        def _():
            car[...] = jnp.zeros(car.shape, F32)

        uv = u_ref[...]
        xs[...] = jnp.dot(uv.astype(BF), bb_ref[...], preferred_element_type=F32)
        tables = _scan_tables(ab_ref[0:1, 0:NST], ab_ref[0:1, NST:2 * NST], False)
        _scan_block(xs, h_ref, car, tables, tm // 8, False)
        y = jnp.dot(h_ref[...].astype(BF), c_ref[...], preferred_element_type=F32) + d_ref[...] * uv
        g1 = _gelu(y)
        z = jnp.dot(g1.astype(BF), w_ref[...], preferred_element_type=F32) + b_ref[...]
        o_ref[...] = (g1 * _sigmoid(z)).astype(BF)

    const = lambda shape: pl.BlockSpec(shape, lambda i: (0, 0))
    return pl.pallas_call(
        body, name=name, grid=(n // tm,),
        in_specs=[pl.BlockSpec((tm, SSM_W), lambda i: (i, 0)), const((1, 2 * NST)), const((SSM_W, 2 * NST)),
                  const((2 * NST, SSM_W)), const((1, SSM_W)), const((SSM_W, SSM_W)), const((1, SSM_W))],
        out_specs=[pl.BlockSpec((tm, 2 * NST), lambda i: (i, 0)), pl.BlockSpec((tm, SSM_W), lambda i: (i, 0))],
        out_shape=[jax.ShapeDtypeStruct((n, 2 * NST), F32), jax.ShapeDtypeStruct((n, SSM_W), BF)],
        scratch_shapes=[pltpu.VMEM((tm, 2 * NST), F32), pltpu.VMEM((8, 2 * NST), F32)],
        compiler_params=_cp(("arbitrary",)))(u, ab, bbt, cmat, dvec, wglu, bglu)


def _s5_bwd(da, u, hst, ab, bbt, cmat, dvec, wglu, bglu, seq, name):
    n = u.shape[0]
    tm = min(ROW_BLOCK, seq)
    nb = n // tm
    blk = lambda r: nb - 1 - r
    prev, _ = _halo_maps(tm, n)

    def body(da_ref, u_ref, h_ref, hp_ref, ab_ref, bb_ref, c_ref, d_ref, w_ref, b_ref,
             du_ref, dw_ref, dbg_ref, dd_ref, dc_ref, dbb_ref, dab_ref, gs, car):
        r = pl.program_id(0)
        i = blk(r)
        first = r == 0

        @pl.when(lax.rem((i + 1) * tm, seq) == 0)
        def _():
            car[...] = jnp.zeros(car.shape, F32)

        uv = u_ref[...]
        dav = da_ref[...].astype(F32)
        hb = h_ref[...]
        hb16 = hb.astype(BF)
        dvv = d_ref[...]
        y = jnp.dot(hb16, c_ref[...], preferred_element_type=F32) + dvv * uv
        g1 = _gelu(y)
        g16 = g1.astype(BF)
        s = _sigmoid(jnp.dot(g16, w_ref[...], preferred_element_type=F32) + b_ref[...])
        dz = dav * g1 * s * (1.0 - s)
        dz16 = dz.astype(BF)
        dg1 = dav * s + lax.dot_general(dz16, w_ref[...], _DIMS['nt'], preferred_element_type=F32)
        _accumulate(dw_ref, first, lax.dot_general(g16, dz16, _DIMS['tn'], preferred_element_type=F32))
        _accumulate(dbg_ref, first, _colsum(dz))
        dy = dg1 * _gelu_grad(y)
        dy16 = dy.astype(BF)
        _accumulate(dd_ref, first, _colsum(dy * uv))
        _accumulate(dc_ref, first, lax.dot_general(hb16, dy16, _DIMS['tn'], preferred_element_type=F32))
        gs[...] = lax.dot_general(dy16, c_ref[...], _DIMS['nt'], preferred_element_type=F32)
        tables = _scan_tables(ab_ref[0:1, 0:NST], ab_ref[0:1, NST:2 * NST], True)
        _scan_block(gs, gs, car, tables, tm // 8, True)
        g = gs[...]
        g16b = g.astype(BF)
        sp = jnp.where(lax.rem(i * tm, seq) == 0, 0.0, 1.0)
        row = lax.broadcasted_iota(jnp.int32, hb.shape, 0)
        hprev = jnp.where(row == 0, hp_ref[7:8, :] * sp, pltpu.roll(hb, 1, 0))
        gr, gi = g[:, :NST], g[:, NST:]
        hr, hi = hprev[:, :NST], hprev[:, NST:]
        _accumulate(dab_ref.at[:, 0:NST], first, _colsum(gr * hr + gi * hi))
        _accumulate(dab_ref.at[:, NST:2 * NST], first, _colsum(gi * hr - gr * hi))
        _accumulate(dbb_ref, first, lax.dot_general(uv.astype(BF), g16b, _DIMS['tn'], preferred_element_type=F32))
        du = dy * dvv + lax.dot_general(g16b, bb_ref[...], _DIMS['nt'], preferred_element_type=F32)
        du_ref[...] = du.astype(BF)

    const = lambda shape: pl.BlockSpec(shape, lambda r: (0, 0))
    rowspec = lambda w: pl.BlockSpec((tm, w), lambda r: (blk(r), 0))
    return pl.pallas_call(
        body, name=name, grid=(nb,),
        in_specs=[rowspec(SSM_W), rowspec(SSM_W), rowspec(2 * NST),
                  pl.BlockSpec((8, 2 * NST), lambda r: (prev(blk(r)), 0)),
                  const((1, 2 * NST)), const((SSM_W, 2 * NST)), const((2 * NST, SSM_W)), const((1, SSM_W)),
                  const((SSM_W, SSM_W)), const((1, SSM_W))],
        out_specs=[rowspec(SSM_W), const((SSM_W, SSM_W)), const((1, SSM_W)), const((1, SSM_W)),
                   const((2 * NST, SSM_W)), const((SSM_W, 2 * NST)), const((1, 2 * NST))],
        out_shape=[jax.ShapeDtypeStruct((n, SSM_W), BF), jax.ShapeDtypeStruct((SSM_W, SSM_W), F32),
                   jax.ShapeDtypeStruct((1, SSM_W), F32), jax.ShapeDtypeStruct((1, SSM_W), F32),
                   jax.ShapeDtypeStruct((2 * NST, SSM_W), F32), jax.ShapeDtypeStruct((SSM_W, 2 * NST), F32),
                   jax.ShapeDtypeStruct((1, 2 * NST), F32)],
        scratch_shapes=[pltpu.VMEM((tm, 2 * NST), F32), pltpu.VMEM((8, 2 * NST), F32)],
        compiler_params=_cp(("arbitrary",)))(da, u, hst, hst, ab, bbt, cmat, dvec, wglu, bglu)


def _s5_rows(lam_re, lam_im, log_dt, b_re, b_im):
    rep = lambda a: jnp.broadcast_to(a[:, None, :], (SSM_G, SSM_H, SSM_P)).reshape(SSM_W, SSM_P)
    dt = jnp.broadcast_to(log_dt[:, None, None], (SSM_G, SSM_H, SSM_P)).reshape(SSM_W, SSM_P)
    tr = lambda b: b.transpose(0, 2, 1).reshape(SSM_W, SSM_P)
    return rep(lam_re), rep(lam_im), dt, tr(b_re), tr(b_im)


def _block_diag(rows_gp, inner):
    eye = jnp.eye(SSM_G, dtype=rows_gp.dtype)
    return (rows_gp[:, :, None, :] * eye[:, None, :, None]).reshape(SSM_G * inner, SSM_G * SSM_P)


def _diag_blocks(mat, inner):
    m4 = mat.reshape(SSM_G, inner, SSM_G, SSM_P)
    return jnp.stack([m4[g, :, g, :] for g in range(SSM_G)])


def _interleave(w, parts):
    lead = w.shape[:-1]
    nb = w.shape[-1] // (parts * FB)
    return jnp.swapaxes(w.reshape(lead + (parts, nb, FB)), -3, -2).reshape(w.shape)


def _deinterleave(w, parts):
    lead = w.shape[:-1]
    nb = w.shape[-1] // (parts * FB)
    return jnp.swapaxes(w.reshape(lead + (nb, parts, FB)), -3, -2).reshape(w.shape)


def _ffn_fwd(h, f, w_up, w_down, cw, cb, seq, tag, tail):
    n = h.shape[0]
    tm = min(2048, n)
    ni = n // tm
    up = _matmul_spec(
        f, w_up, 'nn', (ni, NDEV, 1),
        pl.BlockSpec((tm, D), lambda i, s, k: (i, 0)),
        pl.BlockSpec((D, FSH), lambda i, s, k: (s, 0)),
        pl.BlockSpec((tm, FSH), lambda i, s, k: (s * ni + i, 0)), (NDEV * n, FSH), f"{tag}_up", out_dtype=BF)
    up = up.reshape(2, 4, n, FSH)
    act, dgate = _ffn_conv_fwd(up, cw, cb, seq, f"{tag}_conv")
    out = _matmul_shards(act, w_down.reshape(4, FSH, D), 'nn', 512, D, f"{tag}_down", resid=h, tail=tail)
    return out, (f, up, act, dgate)


def _ffn_bwd(dh, dhb, h, g, w_up, w_down, cw, cb, saved, seq, tag):
    f, up, act, dgate = saved
    n = h.shape[0]
    tm = min(2048, n)
    ni = n // tm
    tk = min(2048, n)
    dact = _matmul_spec(
        dhb, w_down, 'nt', (ni, 4, 1),
        pl.BlockSpec((tm, D), lambda i, j, k: (i, 0)),
        pl.BlockSpec((FSH, D), lambda i, j, k: (j, 0)),
        pl.BlockSpec((tm, FSH), lambda i, j, k: (j * ni + i, 0)), (4 * n, FSH), f"{tag}_ddown_x", out_dtype=BF)
    dw_down = _matmul_tn_shards(act, dhb, 2, tk, f"{tag}_ddown_w")
    dup, dcw, dcb = _ffn_conv_bwd(up, dgate, dact.reshape(4, n, FSH), cw, seq, f"{tag}_dconv")
    dh_in, dhb_in, dg = _matmul_shards(dup.reshape(NDEV, n, FSH), w_up.reshape(NDEV, D, FSH), 'nt', 256, D,
                                       f"{tag}_dup_x", tail=('norm_bwd', h, g, dh))
    dw_up = _matmul_tn_shards(f, dup.reshape(NDEV, n, FSH), 2, tk, f"{tag}_dup_w")
    grads = dict(g=dg, w_up=dw_up, w_down=dw_down.reshape(NDEV, DFF // NDEV, D),
                 cw=dcw.reshape(NDEV, 3, FSH), cb=dcb.reshape(2 * DFF))
    return dh_in, dhb_in, grads


def _col_shards(w, width):
    return w.reshape(w.shape[0], NDEV, width).transpose(1, 0, 2)


def _local_step(x, tgt, w, gw, wait_ffn0, wait_rest, token, scatter, seq):
    bf = lambda a: a.astype(BF)
    row = lambda a: a.reshape(1, -1).astype(F32)
    w_ev = gw['ev_w_in'].transpose(1, 0, 2).reshape(D, 1792)
    w_ev_s5, w_ev_gm = w_ev[:, :SSM_W], w_ev[:, SSM_W:]
    w_evo = gw['ev_w_out'].reshape(D, D)
    f_cb = [w['ffn_conv_b'][l].reshape(2, 4, 1, FSH) for l in range(2)]
    tril = jnp.tril(jnp.ones((CHUNK, CHUNK), dtype=bool))
    gm_w = jnp.where(tril, w['gm_w_s'][0], 0.0)
    gm_wm, gm_wmt = bf(gm_w), bf(jnp.swapaxes(gm_w, 1, 2))
    gm_bt = w['gm_b_s'][0].T
    gm_gv = row(w['gm_v_g'][0])
    s5_in = _s5_rows(w['s5_lam_re'][0], w['s5_lam_im'][0], w['s5_log_dt'][0], w['s5_b_re'][0], w['s5_b_im'][0])
    ab_re, ab_im, bb_re, bb_im = _s5_disc_fwd(s5_in, "s5_disc")
    first_h = lambda a: a.reshape(SSM_G, SSM_H, SSM_P)[:, 0, :].reshape(1, NST)
    s5_ab = jnp.concatenate([first_h(ab_re), first_h(ab_im)], axis=1)
    to_gp = lambda a: a.reshape(SSM_G, SSM_H, SSM_P)
    s5_bbt = bf(jnp.concatenate([_block_diag(to_gp(bb_re), SSM_H), _block_diag(to_gp(bb_im), SSM_H)], axis=1))
    s5_cmat = bf(jnp.concatenate([_block_diag(w['s5_c_re'][0], SSM_H).T, -_block_diag(w['s5_c_im'][0], SSM_H).T],
                                 axis=0))
    s5_d, s5_bg, s5_wg = row(w['s5_d'][0]), row(w['s5_b_glu'][0]), gw['s5_w_glu'].reshape(SSM_W, SSM_W)
    g_mix = [row(w['mix_norm_g'][0]) + token[0:1, 0:1], row(w['mix_norm_g'][1])]
    g_ffn = [row(w['ffn_norm_g'][l]) for l in range(2)]
    g_fin = row(w['final_norm_g'])

    h0 = x
    y0 = _rmsnorm_fwd(h0, g_mix[0], "ev_norm")
    p_s5 = _matmul(y0, w_ev_s5, 'nn', 1024, 256, D, "ev_in_s5")
    p_gm = _matmul(y0, w_ev_gm, 'nn', 1024, 2 * GM_W, D, "ev_in_gm", out_dtype=BF)
    hst, a_out = _s5_fwd(p_s5, s5_ab, s5_bbt, s5_cmat, s5_d, s5_wg, s5_bg, seq, "s5_fwd")
    b_out = _gmlp_fwd(p_gm, gm_wm, gm_bt, gm_gv, seq, "gmlp_fwd")
    mixcat = jnp.concatenate([a_out, b_out], axis=1)
    h1, f0 = _matmul(mixcat, w_evo, 'nn', 1024, D, D, "ev_out", resid=h0, tail=('norm_fwd', g_ffn[0]))
    g0 = wait_ffn0(mixcat)
    w_up0, w_dn0 = g0['ffn_w_up0'].reshape(NDEV * D, FSH), g0['ffn_w_down0'].reshape(DFF, D)
    f_cw0 = g0['ffn_conv_w0'].reshape(2, 4, 3, FSH)
    (h2, y1), ffn0 = _ffn_fwd(h1, f0, w_up0, w_dn0, f_cw0, f_cb[0], seq, "ffn0", ('norm_fwd', g_mix[1]))
    g1 = wait_rest(h2)
    w_od = _interleave(g1['od_w_in'].transpose(1, 0, 2).reshape(D, 3 * D), 3)
    w_odo = g1['od_w_out'].reshape(D, D)
    od_cw = g1['od_conv_w'].transpose(1, 0, 2).reshape(3, D)
    od_cb = g1['od_conv_b'].reshape(1, D)
    w_up1, w_dn1 = g1['ffn_w_up1'].reshape(NDEV * D, FSH), g1['ffn_w_down1'].reshape(DFF, D)
    f_cw1 = g1['ffn_conv_w1'].reshape(2, 4, 3, FSH)
    p_od = _matmul(y1, w_od, 'nn', 1024, 3 * D // 2, D, "od_in", out_dtype=BF)
    mixin = _shortconv_fwd(p_od, od_cw, od_cb, seq, "od_conv")
    h3, f1 = _matmul(mixin, w_odo, 'nn', 1024, D, D, "od_out", resid=h2, tail=('norm_fwd', g_ffn[1]))
    (loss, dh4, dh4b, dg_fin), ffn1 = _ffn_fwd(h3, f1, w_up1, w_dn1, f_cw1, f_cb[1], seq, "ffn1",
                                                ('loss', g_fin, tgt))

    dh3, dh3b, gf1 = _ffn_bwd(dh4, dh4b, h3, g_ffn[1], w_up1, w_dn1, f_cw1, f_cb[1], ffn1, seq, "ffn1")
    dmixin = _matmul(dh3b, w_odo, 'nt', 1024, D, D, "od_dout_x", out_dtype=BF)
    dw_odo = _matmul(mixin, dh3b, 'tn', D, 512, 4096, "od_dout_w", out_dtype=BF)
    dp_od, d_od_cw, d_od_cb = _shortconv_bwd(p_od, dmixin, od_cw, od_cb, seq, "od_dconv")
    dw_od = _matmul(y1, dp_od, 'tn', D, 3 * D // 2, 2048, "od_din_w", out_dtype=BF)
    sent = scatter("scatter_layer1", {
        'od_w_in': _col_shards(_deinterleave(dw_od, 3), 384), 'od_conv_w': _col_shards(d_od_cw, D // NDEV),
        'od_conv_b': d_od_cb.reshape(NDEV, 1, D // NDEV), 'od_w_out': dw_odo.reshape(NDEV, D // NDEV, D),
        'ffn_w_up1': gf1['w_up'], 'ffn_conv_w1': gf1['cw'], 'ffn_w_down1': gf1['w_down']})
    dh2, dh2b, dg_mix1 = _matmul(dp_od, w_od, 'nt', 512, D, 3 * D, "od_din_x",
                                 tail=('norm_bwd', h2, g_mix[1] + sent[0:1, 0:1], dh3))
    dh1, dh1b, gf0 = _ffn_bwd(dh2, dh2b, h1, g_ffn[0], w_up0, w_dn0, f_cw0, f_cb[0], ffn0, seq, "ffn0")
    dmix_a = _matmul(dh1b, w_evo[:SSM_W], 'nt', 1024, SSM_W, D, "ev_dout_xa", out_dtype=BF)
    dmix_b = _matmul(dh1b, w_evo[SSM_W:], 'nt', 1024, GM_W, D, "ev_dout_xb", out_dtype=BF)
    dw_evo = _matmul(mixcat, dh1b, 'tn', D, 512, 4096, "ev_dout_w", out_dtype=BF)
    sent = scatter("scatter_ffn0", {'ffn_w_up0': gf0['w_up'], 'ffn_conv_w0': gf0['cw'], 'ffn_w_down0': gf0['w_down'],
                                    'ev_w_out': dw_evo.reshape(NDEV, D // NDEV, D)})
    dp_s5, d_wg, d_bg, d_d, d_cmat, d_bbt, d_ab = _s5_bwd(dmix_a, p_s5, hst, s5_ab, s5_bbt, s5_cmat,
                                                           s5_d + sent[0:1, 0:1], s5_wg, s5_bg, seq, "s5_bwd")
    dp_gm, d_gmw, d_gmb, d_gmg = _gmlp_bwd(p_gm, dmix_b, gm_wm, gm_wmt, gm_bt, gm_gv, seq, "gmlp_bwd")
    dw_ev = jnp.concatenate([_matmul(y0, dp_s5, 'tn', D, SSM_W, 4096, "ev_din_wa", out_dtype=BF),
                             _matmul(y0, dp_gm, 'tn', D, 2 * GM_W, 2048, "ev_din_wb", out_dtype=BF)], axis=1)
    sent = scatter("scatter_even", {'ev_w_in': _col_shards(dw_ev, 224),
                                    's5_w_glu': d_wg.reshape(NDEV, SSM_W // NDEV, SSM_W)})
    dy0 = _matmul(dp_gm, w_ev_gm, 'nt', 512, D, 2 * GM_W, "ev_din_xb")
    grad_x, _, dg_mix0 = _matmul(dp_s5, w_ev_s5, 'nt', 512, D, SSM_W, "ev_din_xa", resid=dy0,
                                 tail=('norm_bwd', h0, g_mix[0] + sent[0:1, 0:1], dh1))

    put_h0 = lambda a: jnp.zeros((SSM_G, SSM_H, SSM_P), F32).at[:, 0, :].set(a.reshape(SSM_G, SSM_P)).reshape(
        SSM_W, SSM_P)
    ct = (put_h0(d_ab[:, :NST]), put_h0(d_ab[:, NST:]),
          _diag_blocks(d_bbt[:, :NST], SSM_H).reshape(SSM_W, SSM_P),
          _diag_blocks(d_bbt[:, NST:], SSM_H).reshape(SSM_W, SSM_P))
    d_lre, d_lim, d_ldt, d_bre, d_bim = _s5_disc_bwd(s5_in, ct, "s5_ddisc")
    over_h = lambda a: a.reshape(SSM_G, SSM_H, SSM_P).sum(axis=1)
    un_tr = lambda a: a.reshape(SSM_G, SSM_H, SSM_P).transpose(0, 2, 1)
    d_cre = _diag_blocks(d_cmat[:NST].T, SSM_H)
    d_cim = -_diag_blocks(d_cmat[NST:].T, SSM_H)

    repl = {
        'mix_norm_g': jnp.concatenate([dg_mix0, dg_mix1], axis=0),
        'ffn_norm_g': jnp.concatenate([gf0['g'], gf1['g']], axis=0),
        'final_norm_g': dg_fin.reshape(D),
        's5_lam_re': over_h(d_lre)[None], 's5_lam_im': over_h(d_lim)[None],
        's5_log_dt': over_h(d_ldt).sum(axis=1)[None],
        's5_b_re': un_tr(d_bre)[None], 's5_b_im': un_tr(d_bim)[None],
        's5_c_re': d_cre[None], 's5_c_im': d_cim[None],
        's5_d': d_d, 's5_b_glu': d_bg,
        'gm_w_s': d_gmw[None], 'gm_b_s': d_gmb.reshape(1, GM_HEADS, CHUNK), 'gm_v_g': d_gmg,
        'ffn_conv_b': jnp.stack([gf0['cb'], gf1['cb']]),
    }
    return loss, grad_x, repl


HBM_SPEC = pl.BlockSpec(memory_space=pltpu.HBM)


def _at_axis(ref, pos, index):
    return ref.at[(slice(None),) * pos + (index,)]


def _all_gather(shards, positions, name):
    n = len(shards)

    def body(*refs):
        xs, outs = refs[:n], refs[n:2 * n]
        send_sems, recv_sems, local_sems = refs[2 * n:]
        x, y, c = lax.axis_index("x"), lax.axis_index("y"), lax.axis_index("c")
        me, sibling = (x, y, c), (x, y, 1 - c)
        chips = [(1 - x, y), (x, 1 - y), (1 - x, 1 - y)]

        def block(p, dev):
            return _at_axis(outs[p], positions[p], 4 * dev[0] + 2 * dev[1] + dev[2])

        def copy(p, k, dev, to, src=None):
            return pltpu.make_async_remote_copy(
                src_ref=block(p, dev) if src is None else src, dst_ref=block(p, dev),
                send_sem=send_sems.at[p, k], recv_sem=recv_sems.at[p, k], device_id=to, device_id_type=MESH_T)

        mine = [pltpu.make_async_copy(xs[p], block(p, me), local_sems.at[p]) for p in range(n)]
        for cp in mine:
            cp.start()
        first = [copy(p, 0, me, sibling, src=xs[p]) for p in range(n)]
        first += [copy(p, 1 + j, me, (*chip, c), src=xs[p]) for j, chip in enumerate(chips) for p in range(n)]
        for cp in first:
            cp.start()
        passed = []
        for j, chip in enumerate(chips):
            for p in range(n):
                copy(p, 1 + j, (*chip, c), me).wait_recv()
                fwd = copy(p, 4 + j, (*chip, c), sibling)
                fwd.start()
                passed.append(fwd)
        for p in range(n):
            copy(p, 0, sibling, me).wait_recv()
        for j, chip in enumerate(chips):
            for p in range(n):
                copy(p, 4 + j, (*chip, 1 - c), me).wait_recv()
        for cp in first + passed:
            cp.wait_send()
        for cp in mine:
            cp.wait()

    out_shape = [jax.ShapeDtypeStruct(s.shape[:pos] + (NDEV,) + s.shape[pos:], s.dtype)
                 for s, pos in zip(shards, positions)]
    return pl.pallas_call(
        body, name=name, out_shape=out_shape, in_specs=[HBM_SPEC] * n, out_specs=[HBM_SPEC] * n,
        scratch_shapes=[pltpu.SemaphoreType.DMA((n, 7)), pltpu.SemaphoreType.DMA((n, 7)),
                        pltpu.SemaphoreType.DMA((n,))])(*shards)


def _other_devices(x, y, c):
    flip = lambda v, bit: 1 - v if bit else v
    return [(flip(x, k >> 2 & 1), flip(y, k >> 1 & 1), flip(c, k & 1)) for k in range(1, NDEV)]


SEM_SPEC = pl.BlockSpec(memory_space=pltpu.SEMAPHORE)
START_EFFECT = pltpu.SideEffectType.DATAFLOW_SIDE_EFFECTING


def _send_start(arrays, scatter, name):
    n = len(arrays)
    lands = [lax.empty((NDEV,) + (a.shape[1:] if scatter else a.shape), a.dtype) for a in arrays]

    def body(*refs):
        xs, ls = refs[:n], refs[n:2 * n]
        send_sems, recv_sems, own_sems, token = refs[2 * n], refs[2 * n + 1], refs[2 * n + 2], refs[4 * n + 3]
        x, y, c = lax.axis_index("x"), lax.axis_index("y"), lax.axis_index("c")
        me = 4 * x + 2 * y + c
        for k, peer in enumerate(_other_devices(x, y, c)):
            for p in range(n):
                src = xs[p].at[4 * peer[0] + 2 * peer[1] + peer[2]] if scatter else xs[p]
                pltpu.make_async_remote_copy(
                    src_ref=src, dst_ref=ls[p].at[me], send_sem=send_sems.at[p * (NDEV - 1) + k],
                    recv_sem=recv_sems.at[p * (NDEV - 1) + k], device_id=peer, device_id_type=MESH_T).start()
        for p in range(n):
            pltpu.make_async_copy(xs[p].at[me] if scatter else xs[p], ls[p].at[me], own_sems.at[p]).start()
        token[...] = jnp.zeros(token.shape, F32)

    sems = pltpu.SemaphoreType.DMA((n * (NDEV - 1),))
    out_shape = ([sems, sems, pltpu.SemaphoreType.DMA((n,))]
                 + [pltpu.HBM(a.shape, a.dtype) for a in list(arrays) + lands] + [jax.ShapeDtypeStruct((8, 128), F32)])
    res = pl.pallas_call(
        body, name=name, out_shape=out_shape, in_specs=[HBM_SPEC] * (2 * n),
        out_specs=[SEM_SPEC] * 3 + [HBM_SPEC] * (2 * n) + [pl.BlockSpec(memory_space=pltpu.VMEM)],
        input_output_aliases={i: 3 + i for i in range(2 * n)},
        compiler_params=pltpu.CompilerParams(has_side_effects=START_EFFECT))(
            *[pltpu.with_memory_space_constraint(a, pltpu.HBM) for a in list(arrays) + lands])
    return res[:3], res[3:3 + n], res[3 + n:3 + 2 * n], res[3 + 2 * n]


def _send_wait(started, scatter, after, name):
    sems, arrays, lands, _ = started
    n = len(arrays)

    def body(*refs):
        xs, ls = refs[:n], refs[n:2 * n]
        send, recv, own = refs[2 * n:2 * n + 3]
        x, y, c = lax.axis_index("x"), lax.axis_index("y"), lax.axis_index("c")
        me = 4 * x + 2 * y + c
        for p in range(n):
            pltpu.make_async_copy(xs[p].at[me] if scatter else xs[p], ls[p].at[me], own.at[p]).wait()
        for k, peer in enumerate(_other_devices(x, y, c)):
            slot = 4 * peer[0] + 2 * peer[1] + peer[2]
            for p in range(n):
                cp = pltpu.make_async_remote_copy(
                    src_ref=xs[p].at[slot] if scatter else xs[p], dst_ref=ls[p].at[slot],
                    send_sem=send.at[p * (NDEV - 1) + k], recv_sem=recv.at[p * (NDEV - 1) + k], device_id=peer,
                    device_id_type=MESH_T)
                cp.wait_send()
                cp.wait_recv()

    res = pl.pallas_call(
        body, name=name, out_shape=[pltpu.HBM(a.shape, a.dtype) for a in list(arrays) + list(lands)],
        in_specs=[HBM_SPEC] * (2 * n) + [SEM_SPEC] * 3 + [pl.BlockSpec(memory_space=pl.ANY)],
        out_specs=[HBM_SPEC] * (2 * n), input_output_aliases={i: i for i in range(2 * n)},
        compiler_params=pltpu.CompilerParams(has_side_effects=START_EFFECT))(
            *arrays, *lands, *sems, after)
    return res[n:]


def _row_block(rows, cols, itemsize=4, target=2**20):
    best = None
    for tr in range(16, rows + 1, 16):
        if rows % tr == 0 and tr * cols * itemsize <= target:
            best = tr
    return best or rows


def _adamw(w, m, v, gparts, name):
    parts, rows, cols = gparts.shape
    tr = _row_block(rows, cols, target=2**19)
    bc1 = 1.0 - ADAM_B1 ** ADAM_STEP
    bc2 = 1.0 - ADAM_B2 ** ADAM_STEP

    def body(w_ref, m_ref, v_ref, g_ref, go_ref, d_ref, mo_ref, vo_ref):
        g = g_ref[0].astype(F32)
        for k in range(1, parts):
            g = g + g_ref[k].astype(F32)
        mn = ADAM_B1 * m_ref[...] + (1.0 - ADAM_B1) * g
        vn = ADAM_B2 * v_ref[...] + (1.0 - ADAM_B2) * (g * g)
        go_ref[...] = g
        mo_ref[...] = mn
        vo_ref[...] = vn
        d_ref[...] = -ADAM_LR * ((mn / bc1) / (jnp.sqrt(vn / bc2) + ADAM_EPS) + ADAM_WD * w_ref[...])

    blk = pl.BlockSpec((tr, cols), lambda i: (i, 0))
    shp = jax.ShapeDtypeStruct((rows, cols), F32)
    return pl.pallas_call(
        body, name=name, grid=(rows // tr,),
        in_specs=[blk, blk, blk, pl.BlockSpec((parts, tr, cols), lambda i: (0, i, 0))],
        out_specs=[blk] * 4, out_shape=[shp] * 4, compiler_params=_cp(("parallel",)))(w, m, v, gparts)


def _pack(arrays, rows):
    flat = jnp.concatenate([a.reshape(-1).astype(F32) for a in arrays])
    return jnp.pad(flat, (0, rows * PACK_COLS - flat.shape[0])).reshape(rows, PACK_COLS)


def _unpack(buf, shapes):
    flat = buf.reshape(-1)
    out, off = [], 0
    for shp in shapes:
        size = int(np.prod(shp))
        out.append(flat[off:off + size].reshape(shp))
        off += size
    return out


REPL_SHAPES = {'mix_norm_g': (2, 1024), 'ffn_norm_g': (2, 1024), 'final_norm_g': (1024,), 's5_lam_re': (1, 16, 64),
               's5_lam_im': (1, 16, 64), 's5_log_dt': (1, 16), 's5_b_re': (1, 16, 64, 16), 's5_b_im': (1, 16, 64, 16),
               's5_c_re': (1, 16, 16, 64), 's5_c_im': (1, 16, 16, 64), 's5_d': (1, 256), 's5_b_glu': (1, 256),
               'gm_w_s': (1, 6, 128, 128), 'gm_b_s': (1, 6, 128), 'gm_v_g': (1, 768), 'ffn_conv_b': (2, 5632)}
REPL_ELEMS = sum(int(np.prod(REPL_SHAPES[n])) for n in REPL_ORDER)
REPL_ROWS = -(-REPL_ELEMS // (PACK_COLS * 8)) * 8

GATHER_DTYPE = {'ev_w_in': BF, 'ev_w_out': BF, 's5_w_glu': BF, 'od_w_in': BF, 'od_conv_w': F32, 'od_conv_b': F32,
                'od_w_out': BF, 'ffn_w_up': BF, 'ffn_conv_w': F32, 'ffn_w_down': BF}
GATHER_EVEN = ['ev_w_in', 'ev_w_out', 's5_w_glu']
GATHER_FFN0 = ['ffn_w_up0', 'ffn_conv_w0', 'ffn_w_down0']
GATHER_REST = ['od_w_in', 'od_conv_w', 'od_conv_b', 'od_w_out', 'ffn_w_up1', 'ffn_conv_w1', 'ffn_w_down1']

def _squeeze_lead(a):
    return a.reshape(a.shape[1:]) if a.shape[0] == 1 and a.ndim > 2 else a


def kernel(x, mix_norm_g, ffn_norm_g, final_norm_g, ev_w_in, ev_w_out, s5_lam_re, s5_lam_im, s5_log_dt, s5_b_re, s5_b_im, s5_c_re, s5_c_im, s5_d, s5_w_glu, s5_b_glu, gm_w_s, gm_b_s, gm_v_g, od_w_in, od_conv_w, od_conv_b, od_w_out, ffn_w_up, ffn_conv_w, ffn_conv_b, ffn_w_down, loss_target, m_mix_norm_g, m_ffn_norm_g, m_final_norm_g, m_ev_w_in, m_ev_w_out, m_s5_lam_re, m_s5_lam_im, m_s5_log_dt, m_s5_b_re, m_s5_b_im, m_s5_c_re, m_s5_c_im, m_s5_d, m_s5_w_glu, m_s5_b_glu, m_gm_w_s, m_gm_b_s, m_gm_v_g, m_od_w_in, m_od_conv_w, m_od_conv_b, m_od_w_out, m_ffn_w_up, m_ffn_conv_w, m_ffn_conv_b, m_ffn_w_down, v_mix_norm_g, v_ffn_norm_g, v_final_norm_g, v_ev_w_in, v_ev_w_out, v_s5_lam_re, v_s5_lam_im, v_s5_log_dt, v_s5_b_re, v_s5_b_im, v_s5_c_re, v_s5_c_im, v_s5_d, v_s5_w_glu, v_s5_b_glu, v_gm_w_s, v_gm_b_s, v_gm_v_g, v_od_w_in, v_od_conv_w, v_od_conv_b, v_od_w_out, v_ffn_w_up, v_ffn_conv_w, v_ffn_conv_b, v_ffn_w_down):
    given = dict(locals())
    weights = {n: given[n] for n in WEIGHT_ORDER}
    nseq, seq, _ = x.shape

    send = {}
    for name in SHARDED_ORDER:
        a = weights[name].astype(GATHER_DTYPE[name])
        if a.shape[0] == 2:
            send[name + '0'], send[name + '1'] = a[0], a[1]
        else:
            send[name] = _squeeze_lead(a)
    gathers = [_send_start([send[n] for n in names], False, f"gather_{tag}_start")
               for tag, names in (("ffn0", GATHER_FFN0), ("rest", GATHER_REST))]
    token = gathers[0][3] + gathers[1][3]

    def waiter(tag, names, started):
        return lambda after: dict(zip(names, _send_wait(started, False, after, f"gather_{tag}_wait")))

    gathered = dict(zip(GATHER_EVEN, _all_gather([send[n] for n in GATHER_EVEN], [0] * len(GATHER_EVEN),
                                                 "gather_even")))

    scatters = []

    def scatter(tag, grads):
        names = list(grads)
        started = _send_start([grads[n].astype(BF) for n in names], True, f"{tag}_start")
        scatters.append((tag, names, started))
        return started[3]

    loss_row, grad_x, g_repl = _local_step(
        x.reshape(nseq * seq, D), loss_target.reshape(nseq * seq, D), weights, gathered,
        waiter("ffn0", GATHER_FFN0, gathers[0]), waiter("rest", GATHER_REST, gathers[1]), token, scatter, seq)
    loss = lax.psum(loss_row[0, 0], ("x", "y", "c"))

    parts = {}
    for tag, names, started in scatters:
        parts.update(zip(names, _send_wait(started, True, grad_x, f"{tag}_wait")))
    repl_parts = _all_gather([_pack([g_repl[n] for n in REPL_ORDER], REPL_ROWS)], [0], "gather_small_grads")[0]

    out = {}
    for name in SHARDED_ORDER:
        w = weights[name]
        if name + '0' in parts:
            gp = jnp.stack([parts[name + '0'], parts[name + '1']], axis=1)
        else:
            gp = parts[name]
        to_rows = lambda a: a.reshape(-1, w.shape[-1])
        res = _adamw(to_rows(w), to_rows(given["m_" + name]), to_rows(given["v_" + name]),
                     gp.reshape(NDEV, -1, w.shape[-1]), f"adamw_{name}")
        out[name] = [r.reshape(w.shape) for r in res]
    rp = _adamw(_pack([weights[n] for n in REPL_ORDER], REPL_ROWS),
                _pack([given["m_" + n] for n in REPL_ORDER], REPL_ROWS),
                _pack([given["v_" + n] for n in REPL_ORDER], REPL_ROWS), repl_parts, "adamw_replicated")
    rp_shapes = [weights[n].shape for n in REPL_ORDER]
    for k in range(4):
        for name, a in zip(REPL_ORDER, _unpack(rp[k], rp_shapes)):
            out.setdefault(name, [None] * 4)[k] = a
    results = [[out[n][k] for n in WEIGHT_ORDER] for k in range(4)]
    grad_w, delta_w, new_m, new_v = results
    return (loss, grad_x.reshape(nseq, seq, D), *grad_w, *delta_w, *new_m, *new_v)
```

```python
import math

import jax
import jax.numpy as jnp
import numpy as np
from jax import lax
from jax.experimental import pallas as pl
from jax.experimental.pallas import tpu as pltpu

F32 = jnp.float32
BF = jnp.bfloat16

D = 1024
DFF = 2816
NDEV = 8
SSM_W = 256
SSM_G = 16
SSM_H = 16
SSM_P = 64
NST = SSM_G * SSM_P
GM_W = 768
GM_HEADS = 6
CHUNK = 128
EPS = 1e-6
LAM_MAX = -1e-4
FB = 256
FSH = 2 * DFF // NDEV
ROW_BLOCK = 512
CONV_ROW_BLOCK = 1024
VMEM_LIMIT = 48 * 2**20
PACK_COLS = 1024
MESH_T = pl.DeviceIdType.MESH

ADAM_LR = 0.001
ADAM_B1 = 0.9
ADAM_B2 = 0.999
ADAM_EPS = 1e-08
ADAM_WD = 0.01
ADAM_STEP = 10

WEIGHT_ORDER = ['mix_norm_g', 'ffn_norm_g', 'final_norm_g', 'ev_w_in', 'ev_w_out', 's5_lam_re', 's5_lam_im',
                's5_log_dt', 's5_b_re', 's5_b_im', 's5_c_re', 's5_c_im', 's5_d', 's5_w_glu', 's5_b_glu', 'gm_w_s',
                'gm_b_s', 'gm_v_g', 'od_w_in', 'od_conv_w', 'od_conv_b', 'od_w_out', 'ffn_w_up', 'ffn_conv_w',
                'ffn_conv_b', 'ffn_w_down']
SHARDED = {'ev_w_in': ((1, 1024, 1792), 2), 'ev_w_out': ((1, 1024, 1024), 1), 's5_w_glu': ((1, 256, 256), 1),
           'od_w_in': ((1, 1024, 3072), 2), 'od_conv_w': ((1, 3, 1024), 2), 'od_conv_b': ((1, 1024), 1),
           'od_w_out': ((1, 1024, 1024), 1), 'ffn_w_up': ((2, 1024, 5632), 2), 'ffn_conv_w': ((2, 3, 5632), 2),
           'ffn_w_down': ((2, 2816, 1024), 1)}
SHARDED_ORDER = [n for n in WEIGHT_ORDER if n in SHARDED]
REPL_ORDER = [n for n in WEIGHT_ORDER if n not in SHARDED]


def _cp(sem):
    return pltpu.CompilerParams(dimension_semantics=sem, vmem_limit_bytes=VMEM_LIMIT)


def _sigmoid(x):
    return 1.0 / (1.0 + jnp.exp(-x))


_GELU_K = math.sqrt(2.0 / math.pi)


def _gelu(x):
    return 0.5 * x * (1.0 + jnp.tanh(_GELU_K * (x + 0.044715 * x * x * x)))


def _gelu_grad(x):
    t = jnp.tanh(_GELU_K * (x + 0.044715 * x * x * x))
    return 0.5 * (1.0 + t) + 0.5 * x * (1.0 - t * t) * _GELU_K * (1.0 + 3.0 * 0.044715 * x * x)


def _colsum(x):
    return jnp.sum(x, axis=0, keepdims=True)


def _accumulate(ref, first, part):
    @pl.when(first)
    def _():
        ref[...] = part

    @pl.when(jnp.logical_not(first))
    def _():
        ref[...] += part


_DIMS = {'nn': (((1,), (0,)), ((), ())), 'nt': (((1,), (1,)), ((), ())), 'tn': (((0,), (0,)), ((), ()))}


def _rms(xv):
    r = lax.rsqrt(jnp.mean(xv * xv, axis=-1, keepdims=True) + EPS)
    return r, xv * r


def _norm_grad(dyv, gv, r, xh):
    dyg = dyv * gv
    return r * (dyg - xh * jnp.mean(dyg * xh, axis=-1, keepdims=True))


def _tail_io(tail, tm, index, n):
    rows = pl.BlockSpec((tm, D), index)
    vec = pl.BlockSpec((1, D), lambda *_: (0, 0))
    full, half, gain = (jax.ShapeDtypeStruct((n, D), F32), jax.ShapeDtypeStruct((n, D), BF),
                        jax.ShapeDtypeStruct((1, D), F32))
    if tail[0] == 'norm_fwd':
        return [tail[1]], [vec], [rows, rows], [full, half]
    if tail[0] == 'norm_bwd':
        return list(tail[1:]), [rows, vec, rows], [rows, rows, vec], [full, half, gain]
    return (list(tail[1:]), [vec, rows], [pl.BlockSpec((1, 128), lambda *_: (0, 0)), rows, rows, vec],
            [jax.ShapeDtypeStruct((1, 128), F32), full, half, gain])


def _tail_apply(kind, tot, tail_refs, outs, first):
    if kind == 'norm_fwd':
        r, xh = _rms(tot)
        outs[0][...] = tot
        outs[1][...] = (xh * tail_refs[0][...]).astype(BF)
    elif kind == 'norm_bwd':
        x_ref, g_ref, dr_ref = tail_refs
        r, xh = _rms(x_ref[...])
        dx = dr_ref[...] + _norm_grad(tot, g_ref[...], r, xh)
        outs[0][...] = dx
        outs[1][...] = dx.astype(BF)
        _accumulate(outs[2], first, _colsum(tot * xh))
    else:
        gv = tail_refs[0][...]
        r, xh = _rms(tot)
        err = xh * gv - tail_refs[1][...]
        part = 0.5 * jnp.sum(jnp.mean(err * err, axis=-1, keepdims=True), axis=0, keepdims=True)
        _accumulate(outs[0], first, jnp.broadcast_to(part, (1, 128)))
        dyv = err * (1.0 / D)
        dx = _norm_grad(dyv, gv, r, xh)
        outs[1][...] = dx
        outs[2][...] = dx.astype(BF)
        _accumulate(outs[3], first, _colsum(dyv * xh))


def _matmul(a, b, mode, tm, tn, tk, name, resid=None, out_dtype=F32, tail=None):
    if mode == 'tn':
        kdim, m = a.shape
    else:
        m, kdim = a.shape
    n = b.shape[0] if mode == 'nt' else b.shape[1]
    tm, tn, tk = min(tm, m), min(tn, n), min(tk, kdim)
    assert m % tm == 0 and n % tn == 0 and kdim % tk == 0, (name, m, n, kdim, tm, tn, tk)
    a_spec = (pl.BlockSpec((tk, tm), lambda i, j, k: (k, i)) if mode == 'tn'
              else pl.BlockSpec((tm, tk), lambda i, j, k: (i, k)))
    b_spec = (pl.BlockSpec((tn, tk), lambda i, j, k: (j, k)) if mode == 'nt'
              else pl.BlockSpec((tk, tn), lambda i, j, k: (k, j)))
    o_spec = pl.BlockSpec((tm, tn), lambda i, j, k: (i, j))
    return _matmul_spec(a, b, mode, (m // tm, n // tn, kdim // tk), a_spec, b_spec, o_spec, (m, n), name,
                        resid=resid, out_dtype=out_dtype, tail=tail)


def _matmul_spec(a, b, mode, grid, a_spec, b_spec, o_spec, out_shape, name, resid=None, out_dtype=F32, tail=None):
    nk = grid[2]
    tm, tn = o_spec.block_shape[-2:]
    dims = _DIMS[mode]
    has_resid = resid is not None
    operands = [a, b] + ([resid] if has_resid else [])
    in_specs = [a_spec, b_spec] + ([o_spec] if has_resid else [])
    out_specs, out_shapes = [o_spec], [jax.ShapeDtypeStruct(out_shape, out_dtype)]
    n_tail = 0
    if tail is not None:
        assert tn == D and grid[1] == 1, name
        extra, extra_specs, out_specs, out_shapes = _tail_io(tail, tm, lambda i, j, k: (i, 0), out_shape[0])
        operands, in_specs, n_tail = operands + extra, in_specs + extra_specs, len(extra)
    n_in, n_out = len(operands), len(out_specs)

    def body(*refs):
        ins, outs = refs[:n_in], refs[n_in:n_in + n_out]
        a_ref, b_ref = ins[:2]
        part = lax.dot_general(a_ref[...].astype(BF), b_ref[...].astype(BF), dims, preferred_element_type=F32)

        def finish(tot):
            if has_resid:
                tot = tot + ins[2][...]
            if tail is not None:
                _tail_apply(tail[0], tot, ins[n_in - n_tail:], outs, pl.program_id(0) == 0)
            else:
                outs[0][...] = tot.astype(out_dtype)

        if nk == 1:
            finish(part)
        else:
            acc = refs[-1]
            k = pl.program_id(2)

            @pl.when(k == 0)
            def _():
                acc[...] = part

            @pl.when(k > 0)
            def _():
                acc[...] += part

            @pl.when(k == nk - 1)
            def _():
                finish(acc[...])

    res = pl.pallas_call(
        body, name=name, grid=grid, in_specs=in_specs, out_specs=out_specs, out_shape=out_shapes,
        scratch_shapes=[pltpu.VMEM((tm, tn), F32)] if nk > 1 else [],
        compiler_params=_cp(("arbitrary",) * 3 if tail is not None else ("parallel", "parallel", "arbitrary")))(*operands)
    return res if tail is not None else res[0]


def _matmul_shards(a, b, mode, tm, tn, name, resid=None, out_dtype=F32, tail=None):
    shards, m, kdim = a.shape
    n = b.shape[2] if mode == 'nn' else b.shape[1]
    tm, tn = min(tm, m), min(tn, n)
    dims = _DIMS[mode]
    has_resid = resid is not None
    b_spec = (pl.BlockSpec((shards, kdim, tn), lambda i, j: (0, 0, j)) if mode == 'nn'
              else pl.BlockSpec((shards, tn, kdim), lambda i, j: (0, j, 0)))
    o_spec = pl.BlockSpec((tm, tn), lambda i, j: (i, j))
    operands = [a, b] + ([resid] if has_resid else [])
    in_specs = [pl.BlockSpec((shards, tm, kdim), lambda i, j: (0, i, 0)), b_spec] + ([o_spec] if has_resid else [])
    out_specs, out_shapes = [o_spec], [jax.ShapeDtypeStruct((m, n), out_dtype)]
    n_tail = 0
    if tail is not None:
        assert tn == D and n == D, name
        extra, extra_specs, out_specs, out_shapes = _tail_io(tail, tm, lambda i, j: (i, 0), m)
        operands, in_specs, n_tail = operands + extra, in_specs + extra_specs, len(extra)
    n_in, n_out = len(operands), len(out_specs)

    def body(*refs):
        ins, outs = refs[:n_in], refs[n_in:n_in + n_out]
        acc = lax.dot_general(ins[0][0], ins[1][0], dims, preferred_element_type=F32)
        for s in range(1, shards):
            acc = acc + lax.dot_general(ins[0][s], ins[1][s], dims, preferred_element_type=F32)
        if has_resid:
            acc = acc + ins[2][...]
        if tail is not None:
            _tail_apply(tail[0], acc, ins[n_in - n_tail:], outs, pl.program_id(0) == 0)
        else:
            outs[0][...] = acc.astype(out_dtype)

    res = pl.pallas_call(
        body, name=name, grid=(m // tm, n // tn), in_specs=in_specs, out_specs=out_specs, out_shape=out_shapes,
        compiler_params=_cp(("arbitrary", "arbitrary") if tail is not None else ("parallel", "parallel")))(*operands)
    return res if tail is not None else res[0]


def _matmul_tn_shards(a, b, sp, tk, name, out_dtype=BF):
    a_sh, b_sh = a.ndim == 3, b.ndim == 3
    shards = a.shape[0] if a_sh else b.shape[0]
    kdim, m, n = a.shape[-2], a.shape[-1], b.shape[-1]
    nk = kdim // tk

    def body(a_ref, b_ref, o_ref, acc):
        k = pl.program_id(1)
        parts = [lax.dot_general(a_ref[q] if a_sh else a_ref[...], b_ref[q] if b_sh else b_ref[...], _DIMS['tn'],
                                 preferred_element_type=F32) for q in range(sp)]

        @pl.when(k == 0)
        def _():
            for q in range(sp):
                acc[q] = parts[q]

        @pl.when(k > 0)
        def _():
            for q in range(sp):
                acc[q] += parts[q]

        @pl.when(k == nk - 1)
        def _():
            o_ref[...] = acc[...].astype(out_dtype)

    spec = lambda sharded, cols: (pl.BlockSpec((sp, tk, cols), lambda s, k: (s, k, 0)) if sharded
                                  else pl.BlockSpec((tk, cols), lambda s, k: (k, 0)))
    return pl.pallas_call(
        body, name=name, grid=(shards // sp, nk), in_specs=[spec(a_sh, m), spec(b_sh, n)],
        out_specs=pl.BlockSpec((sp, m, n), lambda s, k: (s, 0, 0)),
        out_shape=jax.ShapeDtypeStruct((shards, m, n), out_dtype),
        scratch_shapes=[pltpu.VMEM((sp, m, n), F32)],
        compiler_params=_cp(("parallel", "arbitrary")))(a, b)


def _rmsnorm_fwd(x, g, name):
    n = x.shape[0]
    tm = min(512, n)

    def body(x_ref, g_ref, o_ref):
        xv = x_ref[...]
        r = lax.rsqrt(jnp.mean(xv * xv, axis=-1, keepdims=True) + EPS)
        o_ref[...] = (xv * r * g_ref[...]).astype(BF)

    return pl.pallas_call(
        body, name=name, grid=(n // tm,),
        in_specs=[pl.BlockSpec((tm, D), lambda i: (i, 0)), pl.BlockSpec((1, D), lambda i: (0, 0))],
        out_specs=pl.BlockSpec((tm, D), lambda i: (i, 0)),
        out_shape=jax.ShapeDtypeStruct((n, D), BF), compiler_params=_cp(("parallel",)))(x, g)


def _halo_maps(tm, n_rows):
    r8 = tm // 8
    last = n_rows // 8 - 1
    prev = lambda i: jnp.maximum(i * r8 - 1, 0)
    nxt = lambda i: jnp.minimum((i + 1) * r8, last)
    return prev, nxt


def _lane_blocks(width):
    return [slice(lo, min(lo + 128, width)) for lo in range(0, width, 128)]


def _conv_taps(w_ref, b_ref, g, lanes):
    return w_ref[g, 0:1, lanes], w_ref[g, 1:2, lanes], w_ref[g, 2:3, lanes], b_ref[g, :, lanes]


def _conv_tile(x, prev1, prev2, taps, row):
    w0, w1, w2, b = taps
    r1 = pltpu.roll(x, 1, 0)
    r2 = pltpu.roll(x, 2, 0)
    x1 = jnp.where(row == 0, prev1, r1)
    x2 = jnp.where(row < 2, prev2, r2)
    return b + w0 * x2 + w1 * x1 + w2 * x, x1, x2, r1, r2


def _halo16_maps(tm, n_rows):
    r16 = tm // 16
    last = n_rows // 16 - 1
    return (lambda i: jnp.maximum(i * r16 - 1, 0)), (lambda i: jnp.minimum((i + 1) * r16, last))


def _ffn_conv_fwd(up, cw, cb, seq, name):
    n = up.shape[2]
    tm = min(CONV_ROW_BLOCK, seq)
    prev, _ = _halo16_maps(tm, n)

    def body(u_ref, h_ref, w_ref, b_ref, o_ref, d_ref):
        i = pl.program_id(1)
        scale = jnp.where(lax.rem(i * tm, seq) == 0, 0.0, 1.0)
        for lanes in _lane_blocks(FSH):
            lw = lanes.stop - lanes.start
            row = lax.broadcasted_iota(jnp.int32, (8, lw), 0)
            taps = [_conv_taps(w_ref, b_ref, g, lanes) for g in range(2)]

            def tile(xs, carry):
                hc, nxt = [], []
                for g in range(2):
                    conv, _, _, r1, r2 = _conv_tile(xs[g], carry[2 * g], carry[2 * g + 1], taps[g], row)
                    hc.append(conv)
                    nxt += [r1, r2]
                s = _sigmoid(hc[0])
                silu = hc[0] * s
                return (silu * hc[1], hc[1] * (s * (1.0 + hc[0] * (1.0 - s))), silu), tuple(nxt)

            carry = []
            for g in range(2):
                halo = h_ref[g, :, lanes].astype(F32)[8:] * scale
                carry += [pltpu.roll(halo, 1, 0), pltpu.roll(halo, 2, 0)]
            carry = tuple(carry)
            for m in range(tm // 16):
                rows = slice(m * 16, m * 16 + 16)
                x16 = [u_ref[g, rows, lanes].astype(F32) for g in range(2)]
                a, carry = tile([x[:8] for x in x16], carry)
                b, carry = tile([x[8:] for x in x16], carry)
                o_ref[rows, lanes] = jnp.concatenate([a[0], b[0]], axis=0).astype(BF)
                d_ref[0, rows, lanes] = jnp.concatenate([a[1], b[1]], axis=0).astype(BF)
                d_ref[1, rows, lanes] = jnp.concatenate([a[2], b[2]], axis=0).astype(BF)

    return pl.pallas_call(
        body, name=name, grid=(4, n // tm),
        in_specs=[pl.BlockSpec((2, None, tm, FSH), lambda j, i: (0, j, i, 0)),
                  pl.BlockSpec((2, None, 16, FSH), lambda j, i: (0, j, prev(i), 0)),
                  pl.BlockSpec((2, None, 3, FSH), lambda j, i: (0, j, 0, 0)),
                  pl.BlockSpec((2, None, 1, FSH), lambda j, i: (0, j, 0, 0))],
        out_specs=[pl.BlockSpec((None, tm, FSH), lambda j, i: (j, i, 0)),
                   pl.BlockSpec((2, None, tm, FSH), lambda j, i: (0, j, i, 0))],
        out_shape=[jax.ShapeDtypeStruct((4, n, FSH), BF), jax.ShapeDtypeStruct((2, 4, n, FSH), BF)],
        compiler_params=_cp(("parallel", "parallel")))(up, up, cw, cb)


def _ffn_conv_bwd(up, dgate, dact, cw, seq, name):
    n = up.shape[2]
    tm = min(CONV_ROW_BLOCK, seq)
    _, nxt = _halo16_maps(tm, n)

    def body(u_ref, g_ref, gn_ref, da_ref, dn_ref, w_ref, du_ref, dw_ref, db_ref):
        i = pl.program_id(1)
        sn = jnp.where(lax.rem((i + 1) * tm, seq) == 0, 0.0, 1.0)
        first = i == 0
        for lanes in _lane_blocks(FSH):
            lw = lanes.stop - lanes.start
            row = lax.broadcasted_iota(jnp.int32, (8, lw), 0)
            taps = [(w_ref[g, 0:1, lanes], w_ref[g, 1:2, lanes], w_ref[g, 2:3, lanes]) for g in range(2)]

            def dconv(gs, da):
                ds = [gs[g] * da for g in range(2)]
                return [(d, pltpu.roll(d, 7, 0), pltpu.roll(d, 6, 0)) for d in ds]

            def finish(cur, after, xs, sums):
                dups, new_sums = [], []
                for g in range(2):
                    w0, w1, w2 = taps[g]
                    s1 = jnp.where(row == 7, after[g][1], cur[g][1])
                    s2 = jnp.where(row >= 6, after[g][2], cur[g][2])
                    dups.append(w2 * cur[g][0] + w1 * s1 + w0 * s2)
                    acc = sums[g]
                    new_sums.append((acc[0] + xs[g] * s2, acc[1] + xs[g] * s1, acc[2] + xs[g] * cur[g][0],
                                     acc[3] + cur[g][0]))
                return dups, new_sums

            def emit(m, held, after, sums):
                (ta, xa), (tb, xb) = held
                dup_a, sums = finish(ta, tb, xa, sums)
                dup_b, sums = finish(tb, after, xb, sums)
                for g in range(2):
                    du_ref[g, m * 16:m * 16 + 16, lanes] = jnp.concatenate([dup_a[g], dup_b[g]], axis=0).astype(BF)
                return sums

            zero = jnp.zeros((8, lw), F32)
            sums = [(zero,) * 4, (zero,) * 4]
            held = None
            for m in range(tm // 16):
                rows = slice(m * 16, m * 16 + 16)
                x16 = [u_ref[g, rows, lanes].astype(F32) for g in range(2)]
                g16 = [g_ref[g, rows, lanes].astype(F32) for g in range(2)]
                d16 = da_ref[rows, lanes].astype(F32)
                ta = dconv([a[:8] for a in g16], d16[:8])
                tb = dconv([a[8:] for a in g16], d16[8:])
                if held is not None:
                    sums = emit(m - 1, held, ta, sums)
                held = ((ta, [x[:8] for x in x16]), (tb, [x[8:] for x in x16]))
            tn_ = dconv([gn_ref[g, :, lanes].astype(F32)[:8] for g in range(2)], dn_ref[:, lanes].astype(F32)[:8] * sn)
            sums = emit(tm // 16 - 1, held, tn_, sums)
            for g in range(2):
                for k in range(3):
                    _accumulate(dw_ref.at[g, k:k + 1, lanes], first, _colsum(sums[g][k]))
                _accumulate(db_ref.at[g, :, lanes], first, _colsum(sums[g][3]))

    return pl.pallas_call(
        body, name=name, grid=(4, n // tm),
        in_specs=[pl.BlockSpec((2, None, tm, FSH), lambda j, i: (0, j, i, 0)),
                  pl.BlockSpec((2, None, tm, FSH), lambda j, i: (0, j, i, 0)),
                  pl.BlockSpec((2, None, 16, FSH), lambda j, i: (0, j, nxt(i), 0)),
                  pl.BlockSpec((None, tm, FSH), lambda j, i: (j, i, 0)),
                  pl.BlockSpec((None, 16, FSH), lambda j, i: (j, nxt(i), 0)),
                  pl.BlockSpec((2, None, 3, FSH), lambda j, i: (0, j, 0, 0))],
        out_specs=[pl.BlockSpec((2, None, tm, FSH), lambda j, i: (0, j, i, 0)),
                   pl.BlockSpec((2, None, 3, FSH), lambda j, i: (0, j, 0, 0)),
                   pl.BlockSpec((2, None, 1, FSH), lambda j, i: (0, j, 0, 0))],
        out_shape=[jax.ShapeDtypeStruct((2, 4, n, FSH), BF), jax.ShapeDtypeStruct((2, 4, 3, FSH), F32),
                   jax.ShapeDtypeStruct((2, 4, 1, FSH), F32)],
        compiler_params=_cp(("parallel", "arbitrary")))(up, dgate, dgate, dact, dact, cw)


def _shortconv_fwd(p, cw, cb, seq, name):
    n = p.shape[0]
    tm = min(CONV_ROW_BLOCK, seq)
    prev, _ = _halo16_maps(tm, n)

    def body(p_ref, h_ref, w_ref, b_ref, o_ref):
        i = pl.program_id(1)
        scale = jnp.where(lax.rem(i * tm, seq) == 0, 0.0, 1.0)
        q = p_ref[:, FB:2 * FB].astype(F32) * p_ref[:, 2 * FB:].astype(F32)
        row = lax.broadcasted_iota(jnp.int32, q.shape, 0)
        hq = (h_ref[:, FB:2 * FB].astype(F32) * h_ref[:, 2 * FB:].astype(F32))[8:] * scale
        hrow = lax.broadcasted_iota(jnp.int32, hq.shape, 0)
        h7 = _colsum(jnp.where(hrow == 7, hq, 0.0))
        h6 = _colsum(jnp.where(hrow == 6, hq, 0.0))
        p1 = jnp.where(row == 0, h7, pltpu.roll(q, 1, 0))
        p2 = jnp.where(row == 0, h6, jnp.where(row == 1, h7, pltpu.roll(q, 2, 0)))
        conv = b_ref[...] + w_ref[0:1, :] * p2 + w_ref[1:2, :] * p1 + w_ref[2:3, :] * q
        o_ref[...] = (p_ref[:, :FB].astype(F32) * conv).astype(BF)

    return pl.pallas_call(
        body, name=name, grid=(D // FB, n // tm),
        in_specs=[pl.BlockSpec((tm, 3 * FB), lambda j, i: (i, j)),
                  pl.BlockSpec((16, 3 * FB), lambda j, i: (prev(i), j)),
                  pl.BlockSpec((3, FB), lambda j, i: (0, j)),
                  pl.BlockSpec((1, FB), lambda j, i: (0, j))],
        out_specs=pl.BlockSpec((tm, FB), lambda j, i: (i, j)),
        out_shape=jax.ShapeDtypeStruct((n, D), BF), compiler_params=_cp(("parallel", "parallel")))(p, p, cw, cb)


def _shortconv_bwd(p, dmix, cw, cb, seq, name):
    n = p.shape[0]
    tm = min(CONV_ROW_BLOCK, seq)
    ext = tm + 16
    prev, nxt = _halo16_maps(tm, n)

    def body(p_ref, pp_ref, pn_ref, dm_ref, dn_ref, w_ref, b_ref, dp_ref, dw_ref, db_ref, qx, cx):
        i = pl.program_id(1)
        sp = jnp.where(lax.rem(i * tm, seq) == 0, 0.0, 1.0)
        sn = jnp.where(lax.rem((i + 1) * tm, seq) == 0, 0.0, 1.0)
        bg, cg, hx = (p_ref[:, :FB].astype(F32), p_ref[:, FB:2 * FB].astype(F32), p_ref[:, 2 * FB:].astype(F32))
        dm = dm_ref[...].astype(F32)
        qx[0:8, :] = (pp_ref[:, FB:2 * FB].astype(F32) * pp_ref[:, 2 * FB:].astype(F32))[8:] * sp
        qx[8:8 + tm, :] = cg * hx
        qx[8 + tm:, :] = jnp.zeros((8, FB), F32)
        cx[0:8, :] = jnp.zeros((8, FB), F32)
        cx[8:8 + tm, :] = dm * bg
        cx[8 + tm:, :] = (dn_ref[...].astype(F32) * pn_ref[:, :FB].astype(F32))[:8] * sn
        q0 = qx[...]
        q1 = pltpu.roll(q0, 1, 0)
        q2 = pltpu.roll(q0, 2, 0)
        main = slice(8, 8 + tm)
        conv = b_ref[...] + w_ref[0:1, :] * q2[main] + w_ref[1:2, :] * q1[main] + w_ref[2:3, :] * q0[main]
        dc = cx[...]
        dq = (w_ref[2:3, :] * dc + w_ref[1:2, :] * pltpu.roll(dc, ext - 1, 0)
              + w_ref[0:1, :] * pltpu.roll(dc, ext - 2, 0))[main]
        dp_ref[:, :FB] = (dm * conv).astype(BF)
        dp_ref[:, FB:2 * FB] = (dq * hx).astype(BF)
        dp_ref[:, 2 * FB:] = (dq * cg).astype(BF)
        first = i == 0
        dcm = dc[main]
        _accumulate(dw_ref.at[0:1, :], first, _colsum(dcm * q2[main]))
        _accumulate(dw_ref.at[1:2, :], first, _colsum(dcm * q1[main]))
        _accumulate(dw_ref.at[2:3, :], first, _colsum(dcm * q0[main]))
        _accumulate(db_ref, first, _colsum(dcm))

    return pl.pallas_call(
        body, name=name, grid=(D // FB, n // tm),
        in_specs=[pl.BlockSpec((tm, 3 * FB), lambda j, i: (i, j)),
                  pl.BlockSpec((16, 3 * FB), lambda j, i: (prev(i), j)),
                  pl.BlockSpec((16, 3 * FB), lambda j, i: (nxt(i), j)),
                  pl.BlockSpec((tm, FB), lambda j, i: (i, j)),
                  pl.BlockSpec((16, FB), lambda j, i: (nxt(i), j)),
                  pl.BlockSpec((3, FB), lambda j, i: (0, j)),
                  pl.BlockSpec((1, FB), lambda j, i: (0, j))],
        out_specs=[pl.BlockSpec((tm, 3 * FB), lambda j, i: (i, j)),
                   pl.BlockSpec((3, FB), lambda j, i: (0, j)),
                   pl.BlockSpec((1, FB), lambda j, i: (0, j))],
        out_shape=[jax.ShapeDtypeStruct((n, 3 * D), BF), jax.ShapeDtypeStruct((3, D), F32),
                   jax.ShapeDtypeStruct((1, D), F32)],
        scratch_shapes=[pltpu.VMEM((ext, FB), F32), pltpu.VMEM((ext, FB), F32)],
        compiler_params=_cp(("parallel", "arbitrary")))(p, p, p, dmix, dmix, cw, cb)


def _gmlp_fwd(uv, wm, bst, gv, seq, name):
    n = uv.shape[0]
    tm = min(ROW_BLOCK, seq)

    def body(x_ref, w_ref, b_ref, g_ref, o_ref):
        ge_v = _gelu(x_ref[:, GM_W:].astype(F32))
        r = lax.rsqrt(jnp.mean(ge_v * ge_v, axis=-1, keepdims=True) + EPS)
        vn = (ge_v * r * g_ref[...]).astype(BF)
        for c in range(tm // CHUNK):
            rows = slice(c * CHUNK, (c + 1) * CHUNK)
            for h in range(GM_HEADS):
                cols = slice(h * CHUNK, (h + 1) * CHUNK)
                gate = jnp.dot(w_ref[h], vn[rows, cols], preferred_element_type=F32) + b_ref[:, h:h + 1]
                o_ref[rows, cols] = (_gelu(x_ref[rows, cols].astype(F32)) * gate).astype(BF)

    return pl.pallas_call(
        body, name=name, grid=(n // tm,),
        in_specs=[pl.BlockSpec((tm, 2 * GM_W), lambda i: (i, 0)),
                  pl.BlockSpec((GM_HEADS, CHUNK, CHUNK), lambda i: (0, 0, 0)),
                  pl.BlockSpec((CHUNK, GM_HEADS), lambda i: (0, 0)),
                  pl.BlockSpec((1, GM_W), lambda i: (0, 0))],
        out_specs=pl.BlockSpec((tm, GM_W), lambda i: (i, 0)),
        out_shape=jax.ShapeDtypeStruct((n, GM_W), BF), compiler_params=_cp(("parallel",)))(uv, wm, bst, gv)


def _gmlp_bwd(uv, dout, wm, wmt, bst, gv, seq, name):
    n = uv.shape[0]
    tm = min(ROW_BLOCK, seq)

    def body(x_ref, do_ref, w_ref, wt_ref, b_ref, g_ref, dx_ref, dw_ref, db_ref, dg_ref, dvn_scr):
        first = pl.program_id(0) == 0
        ge_v = _gelu(x_ref[:, GM_W:].astype(F32))
        r = lax.rsqrt(jnp.mean(ge_v * ge_v, axis=-1, keepdims=True) + EPS)
        vh = ge_v * r
        vn = (vh * g_ref[...]).astype(BF)
        tril = (lax.broadcasted_iota(jnp.int32, (CHUNK, CHUNK), 0)
                >= lax.broadcasted_iota(jnp.int32, (CHUNK, CHUNK), 1))
        for h in range(GM_HEADS):
            cols = slice(h * CHUNK, (h + 1) * CHUNK)
            dw = jnp.zeros((CHUNK, CHUNK), F32)
            dbs = jnp.zeros((CHUNK, 1), F32)
            for c in range(tm // CHUNK):
                rows = slice(c * CHUNK, (c + 1) * CHUNK)
                blk = vn[rows, cols]
                gate = jnp.dot(w_ref[h], blk, preferred_element_type=F32) + b_ref[:, h:h + 1]
                xu = x_ref[rows, cols].astype(F32)
                do = do_ref[rows, cols].astype(F32)
                dx_ref[rows, cols] = (do * gate * _gelu_grad(xu)).astype(BF)
                dgate = do * _gelu(xu)
                dgb = dgate.astype(BF)
                dw = dw + lax.dot_general(dgb, blk, _DIMS['nt'], preferred_element_type=F32)
                dbs = dbs + jnp.sum(dgate, axis=1, keepdims=True)
                dvn_scr[rows, cols] = jnp.dot(wt_ref[h], dgb, preferred_element_type=F32)
            _accumulate(dw_ref.at[h], first, jnp.where(tril, dw, 0.0))
            _accumulate(db_ref.at[h], first, dbs)
        dvn = dvn_scr[...]
        _accumulate(dg_ref, first, _colsum(dvn * vh))
        dvh = dvn * g_ref[...]
        dv = r * (dvh - vh * jnp.mean(dvh * vh, axis=-1, keepdims=True))
        dx_ref[:, GM_W:] = (dv * _gelu_grad(x_ref[:, GM_W:].astype(F32))).astype(BF)

    full3 = pl.BlockSpec((GM_HEADS, CHUNK, CHUNK), lambda i: (0, 0, 0))
    return pl.pallas_call(
        body, name=name, grid=(n // tm,),
        in_specs=[pl.BlockSpec((tm, 2 * GM_W), lambda i: (i, 0)), pl.BlockSpec((tm, GM_W), lambda i: (i, 0)),
                  full3, full3, pl.BlockSpec((CHUNK, GM_HEADS), lambda i: (0, 0)),
                  pl.BlockSpec((1, GM_W), lambda i: (0, 0))],
        out_specs=[pl.BlockSpec((tm, 2 * GM_W), lambda i: (i, 0)), full3,
                   pl.BlockSpec((GM_HEADS, CHUNK, 1), lambda i: (0, 0, 0)),
                   pl.BlockSpec((1, GM_W), lambda i: (0, 0))],
        out_shape=[jax.ShapeDtypeStruct((n, 2 * GM_W), BF), jax.ShapeDtypeStruct((GM_HEADS, CHUNK, CHUNK), F32),
                   jax.ShapeDtypeStruct((GM_HEADS, CHUNK, 1), F32), jax.ShapeDtypeStruct((1, GM_W), F32)],
        scratch_shapes=[pltpu.VMEM((tm, GM_W), F32)],
        compiler_params=_cp(("arbitrary",)))(uv, dout, wm, wmt, bst, gv)


def _s5_disc(lam_re, lam_im, log_dt, b_re, b_im):
    lr = jnp.minimum(lam_re, LAM_MAX)
    li = lam_im
    dt = jnp.exp(log_dt)
    mag = jnp.exp(lr * dt)
    ab_re = mag * jnp.cos(li * dt)
    ab_im = mag * jnp.sin(li * dt)
    den = lr * lr + li * li
    nr = ab_re - 1.0
    ni = ab_im
    z_re = (nr * lr + ni * li) / den
    z_im = (ni * lr - nr * li) / den
    return ab_re, ab_im, z_re * b_re - z_im * b_im, z_re * b_im + z_im * b_re


def _s5_disc_fwd(args, name):
    shp = jax.ShapeDtypeStruct(args[0].shape, F32)

    def body(*refs):
        outs = _s5_disc(*[r[...] for r in refs[:5]])
        for o_ref, o in zip(refs[5:], outs):
            o_ref[...] = o

    return pl.pallas_call(body, name=name, out_shape=[shp] * 4)(*args)


def _s5_disc_bwd(args, cts, name):
    shp = jax.ShapeDtypeStruct(args[0].shape, F32)

    def body(*refs):
        _, vjp = jax.vjp(_s5_disc, *[r[...] for r in refs[:5]])
        grads = vjp(tuple(r[...] for r in refs[5:9]))
        for o_ref, o in zip(refs[9:], grads):
            o_ref[...] = o

    return pl.pallas_call(body, name=name, out_shape=[shp] * 5)(*args, *cts)


def _cmul(a, b):
    return a[0] * b[0] - a[1] * b[1], a[0] * b[1] + a[1] * b[0]


def _scan_tables(ar, ai, reverse):
    if reverse:
        ai = -ai
    a1 = (ar, ai)
    a2 = _cmul(a1, a1)
    a3 = _cmul(a2, a1)
    a4 = _cmul(a2, a2)
    powers = [a1, a2, a3, a4, _cmul(a4, a1), _cmul(a4, a2), _cmul(a4, a3), _cmul(a4, a4)]
    row = lax.broadcasted_iota(jnp.int32, (8, NST), 0)
    zero = jnp.zeros((8, NST), F32)
    pr, pi = zero, zero
    for r in range(8):
        pw = powers[7 - r] if reverse else powers[r]
        pr = jnp.where(row == r, pw[0], pr)
        pi = jnp.where(row == r, pw[1], pi)
    levels = []
    for d, pw in ((1, a1), (2, a2), (4, a4)):
        ok = (row <= 7 - d) if reverse else (row >= d)
        levels.append((d, jnp.where(ok, pw[0], zero), jnp.where(ok, pw[1], zero)))
    return (pr, pi), levels


def _scan_block(src, dst, car, tables, n_tiles, reverse):
    (pr, pi), levels = tables
    row = lax.broadcasted_iota(jnp.int32, (8, NST), 0)
    out_row = 0 if reverse else 7

    def step(t, carry):
        cr, ci = carry
        tile = (n_tiles - 1 - t) if reverse else t
        rows = pl.ds(pl.multiple_of(tile * 8, 8), 8)
        xr = src[rows, 0:NST]
        xi = src[rows, NST:2 * NST]
        for d, dr, di in levels:
            shift = 8 - d if reverse else d
            rr = pltpu.roll(xr, shift, 0)
            ri = pltpu.roll(xi, shift, 0)
            xr, xi = xr + dr * rr - di * ri, xi + dr * ri + di * rr
        hr = xr + pr * cr - pi * ci
        hi = xi + pr * ci + pi * cr
        dst[rows, 0:NST] = hr
        dst[rows, NST:2 * NST] = hi
        return (_colsum(jnp.where(row == out_row, hr, 0.0)), _colsum(jnp.where(row == out_row, hi, 0.0)))

    cr, ci = lax.fori_loop(0, n_tiles, step, (car[0:1, 0:NST], car[0:1, NST:2 * NST]))
    car[0:1, 0:NST] = cr
    car[0:1, NST:2 * NST] = ci


def _s5_fwd(u, ab, bbt, cmat, dvec, wglu, bglu, seq, name):
    n = u.shape[0]
    tm = min(ROW_BLOCK, seq)

    def body(u_ref, ab_ref, bb_ref, c_ref, d_ref, w_ref, b_ref, h_ref, o_ref, xs, car):
        i = pl.program_id(0)

        @pl.when(lax.rem(i * tm, seq) == 0)
        def _():
            car[...] = jnp.zeros(car.shape, F32)

        uv = u_ref[...]
        xs[...] = jnp.dot(uv.astype(BF), bb_ref[...], preferred_element_type=F32)
        tables = _scan_tables(ab_ref[0:1, 0:NST], ab_ref[0:1, NST:2 * NST], False)
        _scan_block(xs, h_ref, car, tables, tm // 8, False)
        y = jnp.dot(h_ref[...].astype(BF), c_ref[...], preferred_element_type=F32) + d_ref[...] * uv
        g1 = _gelu(y)
        z = jnp.dot(g1.astype(BF), w_ref[...], preferred_element_type=F32) + b_ref[...]
        o_ref[...] = (g1 * _sigmoid(z)).astype(BF)

    const = lambda shape: pl.BlockSpec(shape, lambda i: (0, 0))
    return pl.pallas_call(
        body, name=name, grid=(n // tm,),
        in_specs=[pl.BlockSpec((tm, SSM_W), lambda i: (i, 0)), const((1, 2 * NST)), const((SSM_W, 2 * NST)),
                  const((2 * NST, SSM_W)), const((1, SSM_W)), const((SSM_W, SSM_W)), const((1, SSM_W))],
        out_specs=[pl.BlockSpec((tm, 2 * NST), lambda i: (i, 0)), pl.BlockSpec((tm, SSM_W), lambda i: (i, 0))],
        out_shape=[jax.ShapeDtypeStruct((n, 2 * NST), F32), jax.ShapeDtypeStruct((n, SSM_W), BF)],
        scratch_shapes=[pltpu.VMEM((tm, 2 * NST), F32), pltpu.VMEM((8, 2 * NST), F32)],
        compiler_params=_cp(("arbitrary",)))(u, ab, bbt, cmat, dvec, wglu, bglu)


def _s5_bwd(da, u, hst, ab, bbt, cmat, dvec, wglu, bglu, seq, name):
    n = u.shape[0]
    tm = min(ROW_BLOCK, seq)
    nb = n // tm
    blk = lambda r: nb - 1 - r
    prev, _ = _halo_maps(tm, n)

    def body(da_ref, u_ref, h_ref, hp_ref, ab_ref, bb_ref, c_ref, d_ref, w_ref, b_ref,
             du_ref, dw_ref, dbg_ref, dd_ref, dc_ref, dbb_ref, dab_ref, gs, car):
        r = pl.program_id(0)
        i = blk(r)
        first = r == 0

        @pl.when(lax.rem((i + 1) * tm, seq) == 0)
        def _():
            car[...] = jnp.zeros(car.shape, F32)

        uv = u_ref[...]
        dav = da_ref[...].astype(F32)
        hb = h_ref[...]
        hb16 = hb.astype(BF)
        dvv = d_ref[...]
        y = jnp.dot(hb16, c_ref[...], preferred_element_type=F32) + dvv * uv
        g1 = _gelu(y)
        g16 = g1.astype(BF)
        s = _sigmoid(jnp.dot(g16, w_ref[...], preferred_element_type=F32) + b_ref[...])
        dz = dav * g1 * s * (1.0 - s)
        dz16 = dz.astype(BF)
        dg1 = dav * s + lax.dot_general(dz16, w_ref[...], _DIMS['nt'], preferred_element_type=F32)
        _accumulate(dw_ref, first, lax.dot_general(g16, dz16, _DIMS['tn'], preferred_element_type=F32))
        _accumulate(dbg_ref, first, _colsum(dz))
        dy = dg1 * _gelu_grad(y)
        dy16 = dy.astype(BF)
        _accumulate(dd_ref, first, _colsum(dy * uv))
        _accumulate(dc_ref, first, lax.dot_general(hb16, dy16, _DIMS['tn'], preferred_element_type=F32))
        gs[...] = lax.dot_general(dy16, c_ref[...], _DIMS['nt'], preferred_element_type=F32)
        tables = _scan_tables(ab_ref[0:1, 0:NST], ab_ref[0:1, NST:2 * NST], True)
        _scan_block(gs, gs, car, tables, tm // 8, True)
        g = gs[...]
        g16b = g.astype(BF)
        sp = jnp.where(lax.rem(i * tm, seq) == 0, 0.0, 1.0)
        row = lax.broadcasted_iota(jnp.int32, hb.shape, 0)
        hprev = jnp.where(row == 0, hp_ref[7:8, :] * sp, pltpu.roll(hb, 1, 0))
        gr, gi = g[:, :NST], g[:, NST:]
        hr, hi = hprev[:, :NST], hprev[:, NST:]
        _accumulate(dab_ref.at[:, 0:NST], first, _colsum(gr * hr + gi * hi))
        _accumulate(dab_ref.at[:, NST:2 * NST], first, _colsum(gi * hr - gr * hi))
        _accumulate(dbb_ref, first, lax.dot_general(uv.astype(BF), g16b, _DIMS['tn'], preferred_element_type=F32))
        du = dy * dvv + lax.dot_general(g16b, bb_ref[...], _DIMS['nt'], preferred_element_type=F32)
        du_ref[...] = du.astype(BF)

    const = lambda shape: pl.BlockSpec(shape, lambda r: (0, 0))
    rowspec = lambda w: pl.BlockSpec((tm, w), lambda r: (blk(r), 0))
    return pl.pallas_call(
        body, name=name, grid=(nb,),
        in_specs=[rowspec(SSM_W), rowspec(SSM_W), rowspec(2 * NST),
                  pl.BlockSpec((8, 2 * NST), lambda r: (prev(blk(r)), 0)),
                  const((1, 2 * NST)), const((SSM_W, 2 * NST)), const((2 * NST, SSM_W)), const((1, SSM_W)),
                  const((SSM_W, SSM_W)), const((1, SSM_W))],
        out_specs=[rowspec(SSM_W), const((SSM_W, SSM_W)), const((1, SSM_W)), const((1, SSM_W)),
                   const((2 * NST, SSM_W)), const((SSM_W, 2 * NST)), const((1, 2 * NST))],
        out_shape=[jax.ShapeDtypeStruct((n, SSM_W), BF), jax.ShapeDtypeStruct((SSM_W, SSM_W), F32),
                   jax.ShapeDtypeStruct((1, SSM_W), F32), jax.ShapeDtypeStruct((1, SSM_W), F32),
                   jax.ShapeDtypeStruct((2 * NST, SSM_W), F32), jax.ShapeDtypeStruct((SSM_W, 2 * NST), F32),
                   jax.ShapeDtypeStruct((1, 2 * NST), F32)],
        scratch_shapes=[pltpu.VMEM((tm, 2 * NST), F32), pltpu.VMEM((8, 2 * NST), F32)],
        compiler_params=_cp(("arbitrary",)))(da, u, hst, hst, ab, bbt, cmat, dvec, wglu, bglu)


def _s5_rows(lam_re, lam_im, log_dt, b_re, b_im):
    rep = lambda a: jnp.broadcast_to(a[:, None, :], (SSM_G, SSM_H, SSM_P)).reshape(SSM_W, SSM_P)
    dt = jnp.broadcast_to(log_dt[:, None, None], (SSM_G, SSM_H, SSM_P)).reshape(SSM_W, SSM_P)
    tr = lambda b: b.transpose(0, 2, 1).reshape(SSM_W, SSM_P)
    return rep(lam_re), rep(lam_im), dt, tr(b_re), tr(b_im)


def _block_diag(rows_gp, inner):
    eye = jnp.eye(SSM_G, dtype=rows_gp.dtype)
    return (rows_gp[:, :, None, :] * eye[:, None, :, None]).reshape(SSM_G * inner, SSM_G * SSM_P)


def _diag_blocks(mat, inner):
    m4 = mat.reshape(SSM_G, inner, SSM_G, SSM_P)
    return jnp.stack([m4[g, :, g, :] for g in range(SSM_G)])


def _interleave(w, parts):
    lead = w.shape[:-1]
    nb = w.shape[-1] // (parts * FB)
    return jnp.swapaxes(w.reshape(lead + (parts, nb, FB)), -3, -2).reshape(w.shape)


def _deinterleave(w, parts):
    lead = w.shape[:-1]
    nb = w.shape[-1] // (parts * FB)
    return jnp.swapaxes(w.reshape(lead + (nb, parts, FB)), -3, -2).reshape(w.shape)


def _ffn_fwd(h, f, w_up, w_down, cw, cb, seq, tag, tail):
    n = h.shape[0]
    tm = min(2048, n)
    ni = n // tm
    up = _matmul_spec(
        f, w_up, 'nn', (ni, NDEV, 1),
        pl.BlockSpec((tm, D), lambda i, s, k: (i, 0)),
        pl.BlockSpec((D, FSH), lambda i, s, k: (s, 0)),
        pl.BlockSpec((tm, FSH), lambda i, s, k: (s * ni + i, 0)), (NDEV * n, FSH), f"{tag}_up", out_dtype=BF)
    up = up.reshape(2, 4, n, FSH)
    act, dgate = _ffn_conv_fwd(up, cw, cb, seq, f"{tag}_conv")
    out = _matmul_shards(act, w_down.reshape(4, FSH, D), 'nn', 512, D, f"{tag}_down", resid=h, tail=tail)
    return out, (f, up, act, dgate)


def _ffn_bwd(dh, dhb, h, g, w_up, w_down, cw, cb, saved, seq, tag):
    f, up, act, dgate = saved
    n = h.shape[0]
    tm = min(2048, n)
    ni = n // tm
    tk = min(2048, n)
    dact = _matmul_spec(
        dhb, w_down, 'nt', (ni, 4, 1),
        pl.BlockSpec((tm, D), lambda i, j, k: (i, 0)),
        pl.BlockSpec((FSH, D), lambda i, j, k: (j, 0)),
        pl.BlockSpec((tm, FSH), lambda i, j, k: (j * ni + i, 0)), (4 * n, FSH), f"{tag}_ddown_x", out_dtype=BF)
    dw_down = _matmul_tn_shards(act, dhb, 2, tk, f"{tag}_ddown_w")
    dup, dcw, dcb = _ffn_conv_bwd(up, dgate, dact.reshape(4, n, FSH), cw, seq, f"{tag}_dconv")
    dh_in, dhb_in, dg = _matmul_shards(dup.reshape(NDEV, n, FSH), w_up.reshape(NDEV, D, FSH), 'nt', 256, D,
                                       f"{tag}_dup_x", tail=('norm_bwd', h, g, dh))
    dw_up = _matmul_tn_shards(f, dup.reshape(NDEV, n, FSH), 2, tk, f"{tag}_dup_w")
    grads = dict(g=dg, w_up=dw_up, w_down=dw_down.reshape(NDEV, DFF // NDEV, D),
                 cw=dcw.reshape(NDEV, 3, FSH), cb=dcb.reshape(2 * DFF))
    return dh_in, dhb_in, grads


def _col_shards(w, width):
    return w.reshape(w.shape[0], NDEV, width).transpose(1, 0, 2)


def _local_step(x, tgt, w, gw, wait_ffn0, wait_rest, token, scatter, seq):
    bf = lambda a: a.astype(BF)
    row = lambda a: a.reshape(1, -1).astype(F32)
    w_ev = gw['ev_w_in'].transpose(1, 0, 2).reshape(D, 1792)
    w_ev_s5, w_ev_gm = w_ev[:, :SSM_W], w_ev[:, SSM_W:]
    w_evo = gw['ev_w_out'].reshape(D, D)
    f_cb = [w['ffn_conv_b'][l].reshape(2, 4, 1, FSH) for l in range(2)]
    tril = jnp.tril(jnp.ones((CHUNK, CHUNK), dtype=bool))
    gm_w = jnp.where(tril, w['gm_w_s'][0], 0.0)
    gm_wm, gm_wmt = bf(gm_w), bf(jnp.swapaxes(gm_w, 1, 2))
    gm_bt = w['gm_b_s'][0].T
    gm_gv = row(w['gm_v_g'][0])
    s5_in = _s5_rows(w['s5_lam_re'][0], w['s5_lam_im'][0], w['s5_log_dt'][0], w['s5_b_re'][0], w['s5_b_im'][0])
    ab_re, ab_im, bb_re, bb_im = _s5_disc_fwd(s5_in, "s5_disc")
    first_h = lambda a: a.reshape(SSM_G, SSM_H, SSM_P)[:, 0, :].reshape(1, NST)
    s5_ab = jnp.concatenate([first_h(ab_re), first_h(ab_im)], axis=1)
    to_gp = lambda a: a.reshape(SSM_G, SSM_H, SSM_P)
    s5_bbt = bf(jnp.concatenate([_block_diag(to_gp(bb_re), SSM_H), _block_diag(to_gp(bb_im), SSM_H)], axis=1))
    s5_cmat = bf(jnp.concatenate([_block_diag(w['s5_c_re'][0], SSM_H).T, -_block_diag(w['s5_c_im'][0], SSM_H).T],
                                 axis=0))
    s5_d, s5_bg, s5_wg = row(w['s5_d'][0]), row(w['s5_b_glu'][0]), gw['s5_w_glu'].reshape(SSM_W, SSM_W)
    g_mix = [row(w['mix_norm_g'][0]) + token[0:1, 0:1], row(w['mix_norm_g'][1])]
    g_ffn = [row(w['ffn_norm_g'][l]) for l in range(2)]
    g_fin = row(w['final_norm_g'])

    h0 = x
    y0 = _rmsnorm_fwd(h0, g_mix[0], "ev_norm")
    p_s5 = _matmul(y0, w_ev_s5, 'nn', 1024, 256, D, "ev_in_s5")
    p_gm = _matmul(y0, w_ev_gm, 'nn', 1024, 2 * GM_W, D, "ev_in_gm", out_dtype=BF)
    hst, a_out = _s5_fwd(p_s5, s5_ab, s5_bbt, s5_cmat, s5_d, s5_wg, s5_bg, seq, "s5_fwd")
    b_out = _gmlp_fwd(p_gm, gm_wm, gm_bt, gm_gv, seq, "gmlp_fwd")
    mixcat = jnp.concatenate([a_out, b_out], axis=1)
    h1, f0 = _matmul(mixcat, w_evo, 'nn', 1024, D, D, "ev_out", resid=h0, tail=('norm_fwd', g_ffn[0]))
    g0 = wait_ffn0(mixcat)
    w_up0, w_dn0 = g0['ffn_w_up0'].reshape(NDEV * D, FSH), g0['ffn_w_down0'].reshape(DFF, D)
    f_cw0 = g0['ffn_conv_w0'].reshape(2, 4, 3, FSH)
    (h2, y1), ffn0 = _ffn_fwd(h1, f0, w_up0, w_dn0, f_cw0, f_cb[0], seq, "ffn0", ('norm_fwd', g_mix[1]))
    g1 = wait_rest(h2)
    w_od = _interleave(g1['od_w_in'].transpose(1, 0, 2).reshape(D, 3 * D), 3)
    w_odo = g1['od_w_out'].reshape(D, D)
    od_cw = g1['od_conv_w'].transpose(1, 0, 2).reshape(3, D)
    od_cb = g1['od_conv_b'].reshape(1, D)
    w_up1, w_dn1 = g1['ffn_w_up1'].reshape(NDEV * D, FSH), g1['ffn_w_down1'].reshape(DFF, D)
    f_cw1 = g1['ffn_conv_w1'].reshape(2, 4, 3, FSH)
    p_od = _matmul(y1, w_od, 'nn', 1024, 3 * D // 2, D, "od_in", out_dtype=BF)
    mixin = _shortconv_fwd(p_od, od_cw, od_cb, seq, "od_conv")
    h3, f1 = _matmul(mixin, w_odo, 'nn', 1024, D, D, "od_out", resid=h2, tail=('norm_fwd', g_ffn[1]))
    (loss, dh4, dh4b, dg_fin), ffn1 = _ffn_fwd(h3, f1, w_up1, w_dn1, f_cw1, f_cb[1], seq, "ffn1",
                                                ('loss', g_fin, tgt))

    dh3, dh3b, gf1 = _ffn_bwd(dh4, dh4b, h3, g_ffn[1], w_up1, w_dn1, f_cw1, f_cb[1], ffn1, seq, "ffn1")
    dmixin = _matmul(dh3b, w_odo, 'nt', 1024, D, D, "od_dout_x", out_dtype=BF)
    dw_odo = _matmul(mixin, dh3b, 'tn', D, 512, 4096, "od_dout_w", out_dtype=BF)
    dp_od, d_od_cw, d_od_cb = _shortconv_bwd(p_od, dmixin, od_cw, od_cb, seq, "od_dconv")
    dw_od = _matmul(y1, dp_od, 'tn', D, 3 * D // 2, 2048, "od_din_w", out_dtype=BF)
    sent = scatter("scatter_layer1", {
        'od_w_in': _col_shards(_deinterleave(dw_od, 3), 384), 'od_conv_w': _col_shards(d_od_cw, D // NDEV),
        'od_conv_b': d_od_cb.reshape(NDEV, 1, D // NDEV), 'od_w_out': dw_odo.reshape(NDEV, D // NDEV, D),
        'ffn_w_up1': gf1['w_up'], 'ffn_conv_w1': gf1['cw'], 'ffn_w_down1': gf1['w_down']})
    dh2, dh2b, dg_mix1 = _matmul(dp_od, w_od, 'nt', 512, D, 3 * D, "od_din_x",
                                 tail=('norm_bwd', h2, g_mix[1] + sent[0:1, 0:1], dh3))
    dh1, dh1b, gf0 = _ffn_bwd(dh2, dh2b, h1, g_ffn[0], w_up0, w_dn0, f_cw0, f_cb[0], ffn0, seq, "ffn0")
    dmix_a = _matmul(dh1b, w_evo[:SSM_W], 'nt', 1024, SSM_W, D, "ev_dout_xa", out_dtype=BF)
    dmix_b = _matmul(dh1b, w_evo[SSM_W:], 'nt', 1024, GM_W, D, "ev_dout_xb", out_dtype=BF)
    dw_evo = _matmul(mixcat, dh1b, 'tn', D, 512, 4096, "ev_dout_w", out_dtype=BF)
    sent = scatter("scatter_ffn0", {'ffn_w_up0': gf0['w_up'], 'ffn_conv_w0': gf0['cw'], 'ffn_w_down0': gf0['w_down'],
                                    'ev_w_out': dw_evo.reshape(NDEV, D // NDEV, D)})
    dp_s5, d_wg, d_bg, d_d, d_cmat, d_bbt, d_ab = _s5_bwd(dmix_a, p_s5, hst, s5_ab, s5_bbt, s5_cmat,
                                                           s5_d + sent[0:1, 0:1], s5_wg, s5_bg, seq, "s5_bwd")
    dp_gm, d_gmw, d_gmb, d_gmg = _gmlp_bwd(p_gm, dmix_b, gm_wm, gm_wmt, gm_bt, gm_gv, seq, "gmlp_bwd")
    dp_ev = jnp.concatenate([dp_s5, dp_gm], axis=1)
    dw_ev = _matmul(y0, dp_ev, 'tn', D, SSM_W + 2 * GM_W, 2048, "ev_din_w", out_dtype=BF)
    sent = scatter("scatter_even", {'ev_w_in': _col_shards(dw_ev, 224),
                                    's5_w_glu': d_wg.reshape(NDEV, SSM_W // NDEV, SSM_W)})
    grad_x, _, dg_mix0 = _matmul(dp_ev, w_ev, 'nt', 512, D, SSM_W + 2 * GM_W, "ev_din_x",
                                 tail=('norm_bwd', h0, g_mix[0] + sent[0:1, 0:1], dh1))

    put_h0 = lambda a: jnp.zeros((SSM_G, SSM_H, SSM_P), F32).at[:, 0, :].set(a.reshape(SSM_G, SSM_P)).reshape(
        SSM_W, SSM_P)
    ct = (put_h0(d_ab[:, :NST]), put_h0(d_ab[:, NST:]),
          _diag_blocks(d_bbt[:, :NST], SSM_H).reshape(SSM_W, SSM_P),
          _diag_blocks(d_bbt[:, NST:], SSM_H).reshape(SSM_W, SSM_P))
    d_lre, d_lim, d_ldt, d_bre, d_bim = _s5_disc_bwd(s5_in, ct, "s5_ddisc")
    over_h = lambda a: a.reshape(SSM_G, SSM_H, SSM_P).sum(axis=1)
    un_tr = lambda a: a.reshape(SSM_G, SSM_H, SSM_P).transpose(0, 2, 1)
    d_cre = _diag_blocks(d_cmat[:NST].T, SSM_H)
    d_cim = -_diag_blocks(d_cmat[NST:].T, SSM_H)

    repl = {
        'mix_norm_g': jnp.concatenate([dg_mix0, dg_mix1], axis=0),
        'ffn_norm_g': jnp.concatenate([gf0['g'], gf1['g']], axis=0),
        'final_norm_g': dg_fin.reshape(D),
        's5_lam_re': over_h(d_lre)[None], 's5_lam_im': over_h(d_lim)[None],
        's5_log_dt': over_h(d_ldt).sum(axis=1)[None],
        's5_b_re': un_tr(d_bre)[None], 's5_b_im': un_tr(d_bim)[None],
        's5_c_re': d_cre[None], 's5_c_im': d_cim[None],
        's5_d': d_d, 's5_b_glu': d_bg,
        'gm_w_s': d_gmw[None], 'gm_b_s': d_gmb.reshape(1, GM_HEADS, CHUNK), 'gm_v_g': d_gmg,
        'ffn_conv_b': jnp.stack([gf0['cb'], gf1['cb']]),
    }
    return loss, grad_x, repl


HBM_SPEC = pl.BlockSpec(memory_space=pltpu.HBM)


def _at_axis(ref, pos, index):
    return ref.at[(slice(None),) * pos + (index,)]


def _all_gather(shards, positions, name):
    n = len(shards)

    def body(*refs):
        xs, outs = refs[:n], refs[n:2 * n]
        send_sems, recv_sems, local_sems = refs[2 * n:]
        x, y, c = lax.axis_index("x"), lax.axis_index("y"), lax.axis_index("c")
        me, sibling = (x, y, c), (x, y, 1 - c)
        chips = [(1 - x, y), (x, 1 - y), (1 - x, 1 - y)]

        def block(p, dev):
            return _at_axis(outs[p], positions[p], 4 * dev[0] + 2 * dev[1] + dev[2])

        def copy(p, k, dev, to, src=None):
            return pltpu.make_async_remote_copy(
                src_ref=block(p, dev) if src is None else src, dst_ref=block(p, dev),
                send_sem=send_sems.at[p, k], recv_sem=recv_sems.at[p, k], device_id=to, device_id_type=MESH_T)

        mine = [pltpu.make_async_copy(xs[p], block(p, me), local_sems.at[p]) for p in range(n)]
        for cp in mine:
            cp.start()
        first = [copy(p, 0, me, sibling, src=xs[p]) for p in range(n)]
        first += [copy(p, 1 + j, me, (*chip, c), src=xs[p]) for j, chip in enumerate(chips) for p in range(n)]
        for cp in first:
            cp.start()
        passed = []
        for j, chip in enumerate(chips):
            for p in range(n):
                copy(p, 1 + j, (*chip, c), me).wait_recv()
                fwd = copy(p, 4 + j, (*chip, c), sibling)
                fwd.start()
                passed.append(fwd)
        for p in range(n):
            copy(p, 0, sibling, me).wait_recv()
        for j, chip in enumerate(chips):
            for p in range(n):
                copy(p, 4 + j, (*chip, 1 - c), me).wait_recv()
        for cp in first + passed:
            cp.wait_send()
        for cp in mine:
            cp.wait()

    out_shape = [jax.ShapeDtypeStruct(s.shape[:pos] + (NDEV,) + s.shape[pos:], s.dtype)
                 for s, pos in zip(shards, positions)]
    return pl.pallas_call(
        body, name=name, out_shape=out_shape, in_specs=[HBM_SPEC] * n, out_specs=[HBM_SPEC] * n,
        scratch_shapes=[pltpu.SemaphoreType.DMA((n, 7)), pltpu.SemaphoreType.DMA((n, 7)),
                        pltpu.SemaphoreType.DMA((n,))])(*shards)


def _other_devices(x, y, c):
    flip = lambda v, bit: 1 - v if bit else v
    return [(flip(x, k >> 2 & 1), flip(y, k >> 1 & 1), flip(c, k & 1)) for k in range(1, NDEV)]


SEM_SPEC = pl.BlockSpec(memory_space=pltpu.SEMAPHORE)
START_EFFECT = pltpu.SideEffectType.DATAFLOW_SIDE_EFFECTING


def _send_start(arrays, scatter, name):
    n = len(arrays)
    lands = [lax.empty((NDEV,) + (a.shape[1:] if scatter else a.shape), a.dtype) for a in arrays]

    def body(*refs):
        xs, ls = refs[:n], refs[n:2 * n]
        send_sems, recv_sems, own_sems, token = refs[2 * n], refs[2 * n + 1], refs[2 * n + 2], refs[4 * n + 3]
        x, y, c = lax.axis_index("x"), lax.axis_index("y"), lax.axis_index("c")
        me = 4 * x + 2 * y + c
        for k, peer in enumerate(_other_devices(x, y, c)):
            for p in range(n):
                src = xs[p].at[4 * peer[0] + 2 * peer[1] + peer[2]] if scatter else xs[p]
                pltpu.make_async_remote_copy(
                    src_ref=src, dst_ref=ls[p].at[me], send_sem=send_sems.at[p * (NDEV - 1) + k],
                    recv_sem=recv_sems.at[p * (NDEV - 1) + k], device_id=peer, device_id_type=MESH_T).start()
        for p in range(n):
            pltpu.make_async_copy(xs[p].at[me] if scatter else xs[p], ls[p].at[me], own_sems.at[p]).start()
        token[...] = jnp.zeros(token.shape, F32)

    sems = pltpu.SemaphoreType.DMA((n * (NDEV - 1),))
    out_shape = ([sems, sems, pltpu.SemaphoreType.DMA((n,))]
                 + [pltpu.HBM(a.shape, a.dtype) for a in list(arrays) + lands] + [jax.ShapeDtypeStruct((8, 128), F32)])
    res = pl.pallas_call(
        body, name=name, out_shape=out_shape, in_specs=[HBM_SPEC] * (2 * n),
        out_specs=[SEM_SPEC] * 3 + [HBM_SPEC] * (2 * n) + [pl.BlockSpec(memory_space=pltpu.VMEM)],
        input_output_aliases={i: 3 + i for i in range(2 * n)},
        compiler_params=pltpu.CompilerParams(has_side_effects=START_EFFECT))(
            *[pltpu.with_memory_space_constraint(a, pltpu.HBM) for a in list(arrays) + lands])
    return res[:3], res[3:3 + n], res[3 + n:3 + 2 * n], res[3 + 2 * n]


def _send_wait(started, scatter, after, name):
    sems, arrays, lands, _ = started
    n = len(arrays)

    def body(*refs):
        xs, ls = refs[:n], refs[n:2 * n]
        send, recv, own = refs[2 * n:2 * n + 3]
        x, y, c = lax.axis_index("x"), lax.axis_index("y"), lax.axis_index("c")
        me = 4 * x + 2 * y + c
        for p in range(n):
            pltpu.make_async_copy(xs[p].at[me] if scatter else xs[p], ls[p].at[me], own.at[p]).wait()
        for k, peer in enumerate(_other_devices(x, y, c)):
            slot = 4 * peer[0] + 2 * peer[1] + peer[2]
            for p in range(n):
                cp = pltpu.make_async_remote_copy(
                    src_ref=xs[p].at[slot] if scatter else xs[p], dst_ref=ls[p].at[slot],
                    send_sem=send.at[p * (NDEV - 1) + k], recv_sem=recv.at[p * (NDEV - 1) + k], device_id=peer,
                    device_id_type=MESH_T)
                cp.wait_send()
                cp.wait_recv()

    res = pl.pallas_call(
        body, name=name, out_shape=[pltpu.HBM(a.shape, a.dtype) for a in list(arrays) + list(lands)],
        in_specs=[HBM_SPEC] * (2 * n) + [SEM_SPEC] * 3 + [pl.BlockSpec(memory_space=pl.ANY)],
        out_specs=[HBM_SPEC] * (2 * n), input_output_aliases={i: i for i in range(2 * n)},
        compiler_params=pltpu.CompilerParams(has_side_effects=START_EFFECT))(
            *arrays, *lands, *sems, after)
    return res[n:]


def _row_block(rows, cols, itemsize=4, target=2**20):
    best = None
    for tr in range(16, rows + 1, 16):
        if rows % tr == 0 and tr * cols * itemsize <= target:
            best = tr
    return best or rows


def _adamw(w, m, v, gparts, name):
    parts, rows, cols = gparts.shape
    tr = _row_block(rows, cols, target=2**19)
    bc1 = 1.0 - ADAM_B1 ** ADAM_STEP
    bc2 = 1.0 - ADAM_B2 ** ADAM_STEP

    def body(w_ref, m_ref, v_ref, g_ref, go_ref, d_ref, mo_ref, vo_ref):
        g = g_ref[0].astype(F32)
        for k in range(1, parts):
            g = g + g_ref[k].astype(F32)
        mn = ADAM_B1 * m_ref[...] + (1.0 - ADAM_B1) * g
        vn = ADAM_B2 * v_ref[...] + (1.0 - ADAM_B2) * (g * g)
        go_ref[...] = g
        mo_ref[...] = mn
        vo_ref[...] = vn
        d_ref[...] = -ADAM_LR * ((mn / bc1) / (jnp.sqrt(vn / bc2) + ADAM_EPS) + ADAM_WD * w_ref[...])

    blk = pl.BlockSpec((tr, cols), lambda i: (i, 0))
    shp = jax.ShapeDtypeStruct((rows, cols), F32)
    return pl.pallas_call(
        body, name=name, grid=(rows // tr,),
        in_specs=[blk, blk, blk, pl.BlockSpec((parts, tr, cols), lambda i: (0, i, 0))],
        out_specs=[blk] * 4, out_shape=[shp] * 4, compiler_params=_cp(("parallel",)))(w, m, v, gparts)


def _pack(arrays, rows):
    flat = jnp.concatenate([a.reshape(-1).astype(F32) for a in arrays])
    return jnp.pad(flat, (0, rows * PACK_COLS - flat.shape[0])).reshape(rows, PACK_COLS)


def _unpack(buf, shapes):
    flat = buf.reshape(-1)
    out, off = [], 0
    for shp in shapes:
        size = int(np.prod(shp))
        out.append(flat[off:off + size].reshape(shp))
        off += size
    return out


REPL_SHAPES = {'mix_norm_g': (2, 1024), 'ffn_norm_g': (2, 1024), 'final_norm_g': (1024,), 's5_lam_re': (1, 16, 64),
               's5_lam_im': (1, 16, 64), 's5_log_dt': (1, 16), 's5_b_re': (1, 16, 64, 16), 's5_b_im': (1, 16, 64, 16),
               's5_c_re': (1, 16, 16, 64), 's5_c_im': (1, 16, 16, 64), 's5_d': (1, 256), 's5_b_glu': (1, 256),
               'gm_w_s': (1, 6, 128, 128), 'gm_b_s': (1, 6, 128), 'gm_v_g': (1, 768), 'ffn_conv_b': (2, 5632)}
REPL_ELEMS = sum(int(np.prod(REPL_SHAPES[n])) for n in REPL_ORDER)
REPL_ROWS = -(-REPL_ELEMS // (PACK_COLS * 8)) * 8

GATHER_DTYPE = {'ev_w_in': BF, 'ev_w_out': BF, 's5_w_glu': BF, 'od_w_in': BF, 'od_conv_w': F32, 'od_conv_b': F32,
                'od_w_out': BF, 'ffn_w_up': BF, 'ffn_conv_w': F32, 'ffn_w_down': BF}
GATHER_EVEN = ['ev_w_in', 'ev_w_out', 's5_w_glu']
GATHER_FFN0 = ['ffn_w_up0', 'ffn_conv_w0', 'ffn_w_down0']
GATHER_REST = ['od_w_in', 'od_conv_w', 'od_conv_b', 'od_w_out', 'ffn_w_up1', 'ffn_conv_w1', 'ffn_w_down1']

def _squeeze_lead(a):
    return a.reshape(a.shape[1:]) if a.shape[0] == 1 and a.ndim > 2 else a


def kernel(x, mix_norm_g, ffn_norm_g, final_norm_g, ev_w_in, ev_w_out, s5_lam_re, s5_lam_im, s5_log_dt, s5_b_re, s5_b_im, s5_c_re, s5_c_im, s5_d, s5_w_glu, s5_b_glu, gm_w_s, gm_b_s, gm_v_g, od_w_in, od_conv_w, od_conv_b, od_w_out, ffn_w_up, ffn_conv_w, ffn_conv_b, ffn_w_down, loss_target, m_mix_norm_g, m_ffn_norm_g, m_final_norm_g, m_ev_w_in, m_ev_w_out, m_s5_lam_re, m_s5_lam_im, m_s5_log_dt, m_s5_b_re, m_s5_b_im, m_s5_c_re, m_s5_c_im, m_s5_d, m_s5_w_glu, m_s5_b_glu, m_gm_w_s, m_gm_b_s, m_gm_v_g, m_od_w_in, m_od_conv_w, m_od_conv_b, m_od_w_out, m_ffn_w_up, m_ffn_conv_w, m_ffn_conv_b, m_ffn_w_down, v_mix_norm_g, v_ffn_norm_g, v_final_norm_g, v_ev_w_in, v_ev_w_out, v_s5_lam_re, v_s5_lam_im, v_s5_log_dt, v_s5_b_re, v_s5_b_im, v_s5_c_re, v_s5_c_im, v_s5_d, v_s5_w_glu, v_s5_b_glu, v_gm_w_s, v_gm_b_s, v_gm_v_g, v_od_w_in, v_od_conv_w, v_od_conv_b, v_od_w_out, v_ffn_w_up, v_ffn_conv_w, v_ffn_conv_b, v_ffn_w_down):
    given = dict(locals())
    weights = {n: given[n] for n in WEIGHT_ORDER}
    nseq, seq, _ = x.shape

    send = {}
    for name in SHARDED_ORDER:
        a = weights[name].astype(GATHER_DTYPE[name])
        if a.shape[0] == 2:
            send[name + '0'], send[name + '1'] = a[0], a[1]
        else:
            send[name] = _squeeze_lead(a)
    gathers = [_send_start([send[n] for n in names], False, f"gather_{tag}_start")
               for tag, names in (("ffn0", GATHER_FFN0), ("rest", GATHER_REST))]
    token = gathers[0][3] + gathers[1][3]

    def waiter(tag, names, started):
        return lambda after: dict(zip(names, _send_wait(started, False, after, f"gather_{tag}_wait")))

    gathered = dict(zip(GATHER_EVEN, _all_gather([send[n] for n in GATHER_EVEN], [0] * len(GATHER_EVEN),
                                                 "gather_even")))

    scatters = []

    def scatter(tag, grads):
        names = list(grads)
        started = _send_start([grads[n].astype(BF) for n in names], True, f"{tag}_start")
        scatters.append((tag, names, started))
        return started[3]

    loss_row, grad_x, g_repl = _local_step(
        x.reshape(nseq * seq, D), loss_target.reshape(nseq * seq, D), weights, gathered,
        waiter("ffn0", GATHER_FFN0, gathers[0]), waiter("rest", GATHER_REST, gathers[1]), token, scatter, seq)
    loss = lax.psum(loss_row[0, 0], ("x", "y", "c"))

    parts = {}
    for tag, names, started in scatters:
        parts.update(zip(names, _send_wait(started, True, grad_x, f"{tag}_wait")))
    repl_parts = _all_gather([_pack([g_repl[n] for n in REPL_ORDER], REPL_ROWS)], [0], "gather_small_grads")[0]

    out = {}
    for name in SHARDED_ORDER:
        w = weights[name]
        if name + '0' in parts:
            gp = jnp.stack([parts[name + '0'], parts[name + '1']], axis=1)
        else:
            gp = parts[name]
        to_rows = lambda a: a.reshape(-1, w.shape[-1])
        res = _adamw(to_rows(w), to_rows(given["m_" + name]), to_rows(given["v_" + name]),
                     gp.reshape(NDEV, -1, w.shape[-1]), f"adamw_{name}")
        out[name] = [r.reshape(w.shape) for r in res]
    rp = _adamw(_pack([weights[n] for n in REPL_ORDER], REPL_ROWS),
                _pack([given["m_" + n] for n in REPL_ORDER], REPL_ROWS),
                _pack([given["v_" + n] for n in REPL_ORDER], REPL_ROWS), repl_parts, "adamw_replicated")
    rp_shapes = [weights[n].shape for n in REPL_ORDER]
    for k in range(4):
        for name, a in zip(REPL_ORDER, _unpack(rp[k], rp_shapes)):
            out.setdefault(name, [None] * 4)[k] = a
    results = [[out[n][k] for n in WEIGHT_ORDER] for k in range(4)]
    grad_w, delta_w, new_m, new_v = results
    return (loss, grad_x.reshape(nseq, seq, D), *grad_w, *delta_w, *new_m, *new_v)
```

```python
import math

import jax
import jax.numpy as jnp
import numpy as np
from jax import lax
from jax.experimental import pallas as pl
from jax.experimental.pallas import tpu as pltpu

F32 = jnp.float32
BF = jnp.bfloat16

D = 1024
DFF = 2816
NDEV = 8
SSM_W = 256
SSM_G = 16
SSM_H = 16
SSM_P = 64
NST = SSM_G * SSM_P
GM_W = 768
GM_HEADS = 6
CHUNK = 128
EPS = 1e-6
LAM_MAX = -1e-4
FB = 256
FSH = 2 * DFF // NDEV
ROW_BLOCK = 512
CONV_ROW_BLOCK = 1024
VMEM_LIMIT = 48 * 2**20
PACK_COLS = 1024
MESH_T = pl.DeviceIdType.MESH

ADAM_LR = 0.001
ADAM_B1 = 0.9
ADAM_B2 = 0.999
ADAM_EPS = 1e-08
ADAM_WD = 0.01
ADAM_STEP = 10

WEIGHT_ORDER = ['mix_norm_g', 'ffn_norm_g', 'final_norm_g', 'ev_w_in', 'ev_w_out', 's5_lam_re', 's5_lam_im',
                's5_log_dt', 's5_b_re', 's5_b_im', 's5_c_re', 's5_c_im', 's5_d', 's5_w_glu', 's5_b_glu', 'gm_w_s',
                'gm_b_s', 'gm_v_g', 'od_w_in', 'od_conv_w', 'od_conv_b', 'od_w_out', 'ffn_w_up', 'ffn_conv_w',
                'ffn_conv_b', 'ffn_w_down']
SHARDED = {'ev_w_in': ((1, 1024, 1792), 2), 'ev_w_out': ((1, 1024, 1024), 1), 's5_w_glu': ((1, 256, 256), 1),
           'od_w_in': ((1, 1024, 3072), 2), 'od_conv_w': ((1, 3, 1024), 2), 'od_conv_b': ((1, 1024), 1),
           'od_w_out': ((1, 1024, 1024), 1), 'ffn_w_up': ((2, 1024, 5632), 2), 'ffn_conv_w': ((2, 3, 5632), 2),
           'ffn_w_down': ((2, 2816, 1024), 1)}
SHARDED_ORDER = [n for n in WEIGHT_ORDER if n in SHARDED]
REPL_ORDER = [n for n in WEIGHT_ORDER if n not in SHARDED]


def _cp(sem):
    return pltpu.CompilerParams(dimension_semantics=sem, vmem_limit_bytes=VMEM_LIMIT)


def _sigmoid(x):
    return 1.0 / (1.0 + jnp.exp(-x))


_GELU_K = math.sqrt(2.0 / math.pi)


def _gelu(x):
    return 0.5 * x * (1.0 + jnp.tanh(_GELU_K * (x + 0.044715 * x * x * x)))


def _gelu_grad(x):
    t = jnp.tanh(_GELU_K * (x + 0.044715 * x * x * x))
    return 0.5 * (1.0 + t) + 0.5 * x * (1.0 - t * t) * _GELU_K * (1.0 + 3.0 * 0.044715 * x * x)


def _colsum(x):
    return jnp.sum(x, axis=0, keepdims=True)


def _accumulate(ref, first, part):
    @pl.when(first)
    def _():
        ref[...] = part

    @pl.when(jnp.logical_not(first))
    def _():
        ref[...] += part


_DIMS = {'nn': (((1,), (0,)), ((), ())), 'nt': (((1,), (1,)), ((), ())), 'tn': (((0,), (0,)), ((), ()))}


def _rms(xv):
    r = lax.rsqrt(jnp.mean(xv * xv, axis=-1, keepdims=True) + EPS)
    return r, xv * r


def _norm_grad(dyv, gv, r, xh):
    dyg = dyv * gv
    return r * (dyg - xh * jnp.mean(dyg * xh, axis=-1, keepdims=True))


def _tail_io(tail, tm, index, n):
    rows = pl.BlockSpec((tm, D), index)
    vec = pl.BlockSpec((1, D), lambda *_: (0, 0))
    full, half, gain = (jax.ShapeDtypeStruct((n, D), F32), jax.ShapeDtypeStruct((n, D), BF),
                        jax.ShapeDtypeStruct((1, D), F32))
    if tail[0] == 'norm_fwd':
        return [tail[1]], [vec], [rows, rows], [full, half]
    if tail[0] == 'norm_bwd':
        return list(tail[1:]), [rows, vec, rows], [rows, rows, vec], [full, half, gain]
    return (list(tail[1:]), [vec, rows], [pl.BlockSpec((1, 128), lambda *_: (0, 0)), rows, rows, vec],
            [jax.ShapeDtypeStruct((1, 128), F32), full, half, gain])


def _tail_apply(kind, tot, tail_refs, outs, first):
    if kind == 'norm_fwd':
        r, xh = _rms(tot)
        outs[0][...] = tot
        outs[1][...] = (xh * tail_refs[0][...]).astype(BF)
    elif kind == 'norm_bwd':
        x_ref, g_ref, dr_ref = tail_refs
        r, xh = _rms(x_ref[...])
        dx = dr_ref[...] + _norm_grad(tot, g_ref[...], r, xh)
        outs[0][...] = dx
        outs[1][...] = dx.astype(BF)
        _accumulate(outs[2], first, _colsum(tot * xh))
    else:
        gv = tail_refs[0][...]
        r, xh = _rms(tot)
        err = xh * gv - tail_refs[1][...]
        part = 0.5 * jnp.sum(jnp.mean(err * err, axis=-1, keepdims=True), axis=0, keepdims=True)
        _accumulate(outs[0], first, jnp.broadcast_to(part, (1, 128)))
        dyv = err * (1.0 / D)
        dx = _norm_grad(dyv, gv, r, xh)
        outs[1][...] = dx
        outs[2][...] = dx.astype(BF)
        _accumulate(outs[3], first, _colsum(dyv * xh))


def _matmul(a, b, mode, tm, tn, tk, name, resid=None, out_dtype=F32, tail=None):
    if mode == 'tn':
        kdim, m = a.shape
    else:
        m, kdim = a.shape
    n = b.shape[0] if mode == 'nt' else b.shape[1]
    tm, tn, tk = min(tm, m), min(tn, n), min(tk, kdim)
    assert m % tm == 0 and n % tn == 0 and kdim % tk == 0, (name, m, n, kdim, tm, tn, tk)
    a_spec = (pl.BlockSpec((tk, tm), lambda i, j, k: (k, i)) if mode == 'tn'
              else pl.BlockSpec((tm, tk), lambda i, j, k: (i, k)))
    b_spec = (pl.BlockSpec((tn, tk), lambda i, j, k: (j, k)) if mode == 'nt'
              else pl.BlockSpec((tk, tn), lambda i, j, k: (k, j)))
    o_spec = pl.BlockSpec((tm, tn), lambda i, j, k: (i, j))
    return _matmul_spec(a, b, mode, (m // tm, n // tn, kdim // tk), a_spec, b_spec, o_spec, (m, n), name,
                        resid=resid, out_dtype=out_dtype, tail=tail)


def _matmul_spec(a, b, mode, grid, a_spec, b_spec, o_spec, out_shape, name, resid=None, out_dtype=F32, tail=None):
    nk = grid[2]
    tm, tn = o_spec.block_shape[-2:]
    dims = _DIMS[mode]
    has_resid = resid is not None
    operands = [a, b] + ([resid] if has_resid else [])
    in_specs = [a_spec, b_spec] + ([o_spec] if has_resid else [])
    out_specs, out_shapes = [o_spec], [jax.ShapeDtypeStruct(out_shape, out_dtype)]
    n_tail = 0
    if tail is not None:
        assert tn == D and grid[1] == 1, name
        extra, extra_specs, out_specs, out_shapes = _tail_io(tail, tm, lambda i, j, k: (i, 0), out_shape[0])
        operands, in_specs, n_tail = operands + extra, in_specs + extra_specs, len(extra)
    n_in, n_out = len(operands), len(out_specs)

    def body(*refs):
        ins, outs = refs[:n_in], refs[n_in:n_in + n_out]
        a_ref, b_ref = ins[:2]
        part = lax.dot_general(a_ref[...].astype(BF), b_ref[...].astype(BF), dims, preferred_element_type=F32)

        def finish(tot):
            if has_resid:
                tot = tot + ins[2][...]
            if tail is not None:
                _tail_apply(tail[0], tot, ins[n_in - n_tail:], outs, pl.program_id(0) == 0)
            else:
                outs[0][...] = tot.astype(out_dtype)

        if nk == 1:
            finish(part)
        else:
            acc = refs[-1]
            k = pl.program_id(2)

            @pl.when(k == 0)
            def _():
                acc[...] = part

            @pl.when(k > 0)
            def _():
                acc[...] += part

            @pl.when(k == nk - 1)
            def _():
                finish(acc[...])

    res = pl.pallas_call(
        body, name=name, grid=grid, in_specs=in_specs, out_specs=out_specs, out_shape=out_shapes,
        scratch_shapes=[pltpu.VMEM((tm, tn), F32)] if nk > 1 else [],
        compiler_params=_cp(("arbitrary",) * 3 if tail is not None else ("parallel", "parallel", "arbitrary")))(*operands)
    return res if tail is not None else res[0]


def _matmul_shards(a, b, mode, tm, tn, name, resid=None, out_dtype=F32, tail=None):
    shards, m, kdim = a.shape
    n = b.shape[2] if mode == 'nn' else b.shape[1]
    tm, tn = min(tm, m), min(tn, n)
    dims = _DIMS[mode]
    has_resid = resid is not None
    b_spec = (pl.BlockSpec((shards, kdim, tn), lambda i, j: (0, 0, j)) if mode == 'nn'
              else pl.BlockSpec((shards, tn, kdim), lambda i, j: (0, j, 0)))
    o_spec = pl.BlockSpec((tm, tn), lambda i, j: (i, j))
    operands = [a, b] + ([resid] if has_resid else [])
    in_specs = [pl.BlockSpec((shards, tm, kdim), lambda i, j: (0, i, 0)), b_spec] + ([o_spec] if has_resid else [])
    out_specs, out_shapes = [o_spec], [jax.ShapeDtypeStruct((m, n), out_dtype)]
    n_tail = 0
    if tail is not None:
        assert tn == D and n == D, name
        extra, extra_specs, out_specs, out_shapes = _tail_io(tail, tm, lambda i, j: (i, 0), m)
        operands, in_specs, n_tail = operands + extra, in_specs + extra_specs, len(extra)
    n_in, n_out = len(operands), len(out_specs)

    def body(*refs):
        ins, outs = refs[:n_in], refs[n_in:n_in + n_out]
        acc = lax.dot_general(ins[0][0], ins[1][0], dims, preferred_element_type=F32)
        for s in range(1, shards):
            acc = acc + lax.dot_general(ins[0][s], ins[1][s], dims, preferred_element_type=F32)
        if has_resid:
            acc = acc + ins[2][...]
        if tail is not None:
            _tail_apply(tail[0], acc, ins[n_in - n_tail:], outs, pl.program_id(0) == 0)
        else:
            outs[0][...] = acc.astype(out_dtype)

    res = pl.pallas_call(
        body, name=name, grid=(m // tm, n // tn), in_specs=in_specs, out_specs=out_specs, out_shape=out_shapes,
        compiler_params=_cp(("arbitrary", "arbitrary") if tail is not None else ("parallel", "parallel")))(*operands)
    return res if tail is not None else res[0]


def _matmul_tn_shards(a, b, sp, tk, name, out_dtype=BF):
    a_sh, b_sh = a.ndim == 3, b.ndim == 3
    shards = a.shape[0] if a_sh else b.shape[0]
    kdim, m, n = a.shape[-2], a.shape[-1], b.shape[-1]
    nk = kdim // tk

    def body(a_ref, b_ref, o_ref, acc):
        k = pl.program_id(1)
        parts = [lax.dot_general(a_ref[q] if a_sh else a_ref[...], b_ref[q] if b_sh else b_ref[...], _DIMS['tn'],
                                 preferred_element_type=F32) for q in range(sp)]

        @pl.when(k == 0)
        def _():
            for q in range(sp):
                acc[q] = parts[q]

        @pl.when(k > 0)
        def _():
            for q in range(sp):
                acc[q] += parts[q]

        @pl.when(k == nk - 1)
        def _():
            o_ref[...] = acc[...].astype(out_dtype)

    spec = lambda sharded, cols: (pl.BlockSpec((sp, tk, cols), lambda s, k: (s, k, 0)) if sharded
                                  else pl.BlockSpec((tk, cols), lambda s, k: (k, 0)))
    return pl.pallas_call(
        body, name=name, grid=(shards // sp, nk), in_specs=[spec(a_sh, m), spec(b_sh, n)],
        out_specs=pl.BlockSpec((sp, m, n), lambda s, k: (s, 0, 0)),
        out_shape=jax.ShapeDtypeStruct((shards, m, n), out_dtype),
        scratch_shapes=[pltpu.VMEM((sp, m, n), F32)],
        compiler_params=_cp(("parallel", "arbitrary")))(a, b)


def _rmsnorm_fwd(x, g, name):
    n = x.shape[0]
    tm = min(512, n)

    def body(x_ref, g_ref, o_ref):
        xv = x_ref[...]
        r = lax.rsqrt(jnp.mean(xv * xv, axis=-1, keepdims=True) + EPS)
        o_ref[...] = (xv * r * g_ref[...]).astype(BF)

    return pl.pallas_call(
        body, name=name, grid=(n // tm,),
        in_specs=[pl.BlockSpec((tm, D), lambda i: (i, 0)), pl.BlockSpec((1, D), lambda i: (0, 0))],
        out_specs=pl.BlockSpec((tm, D), lambda i: (i, 0)),
        out_shape=jax.ShapeDtypeStruct((n, D), BF), compiler_params=_cp(("parallel",)))(x, g)


def _halo_maps(tm, n_rows):
    r8 = tm // 8
    last = n_rows // 8 - 1
    prev = lambda i: jnp.maximum(i * r8 - 1, 0)
    nxt = lambda i: jnp.minimum((i + 1) * r8, last)
    return prev, nxt


def _lane_blocks(width):
    return [slice(lo, min(lo + 128, width)) for lo in range(0, width, 128)]


def _conv_taps(w_ref, b_ref, g, lanes):
    return w_ref[g, 0:1, lanes], w_ref[g, 1:2, lanes], w_ref[g, 2:3, lanes], b_ref[g, :, lanes]


def _conv_tile(x, prev1, prev2, taps, row):
    w0, w1, w2, b = taps
    r1 = pltpu.roll(x, 1, 0)
    r2 = pltpu.roll(x, 2, 0)
    x1 = jnp.where(row == 0, prev1, r1)
    x2 = jnp.where(row < 2, prev2, r2)
    return b + w0 * x2 + w1 * x1 + w2 * x, x1, x2, r1, r2


def _halo16_maps(tm, n_rows):
    r16 = tm // 16
    last = n_rows // 16 - 1
    return (lambda i: jnp.maximum(i * r16 - 1, 0)), (lambda i: jnp.minimum((i + 1) * r16, last))


def _ffn_conv_fwd(up, cw, cb, seq, name):
    n = up.shape[2]
    tm = min(CONV_ROW_BLOCK, seq)
    prev, _ = _halo16_maps(tm, n)

    def body(u_ref, h_ref, w_ref, b_ref, o_ref, d_ref):
        i = pl.program_id(1)
        scale = jnp.where(lax.rem(i * tm, seq) == 0, 0.0, 1.0)
        for lanes in _lane_blocks(FSH):
            lw = lanes.stop - lanes.start
            row = lax.broadcasted_iota(jnp.int32, (8, lw), 0)
            taps = [_conv_taps(w_ref, b_ref, g, lanes) for g in range(2)]

            def tile(xs, carry):
                hc, nxt = [], []
                for g in range(2):
                    conv, _, _, r1, r2 = _conv_tile(xs[g], carry[2 * g], carry[2 * g + 1], taps[g], row)
                    hc.append(conv)
                    nxt += [r1, r2]
                s = _sigmoid(hc[0])
                silu = hc[0] * s
                return (silu * hc[1], hc[1] * (s * (1.0 + hc[0] * (1.0 - s))), silu), tuple(nxt)

            carry = []
            for g in range(2):
                halo = h_ref[g, :, lanes].astype(F32)[8:] * scale
                carry += [pltpu.roll(halo, 1, 0), pltpu.roll(halo, 2, 0)]
            carry = tuple(carry)
            for m in range(tm // 16):
                rows = slice(m * 16, m * 16 + 16)
                x16 = [u_ref[g, rows, lanes].astype(F32) for g in range(2)]
                a, carry = tile([x[:8] for x in x16], carry)
                b, carry = tile([x[8:] for x in x16], carry)
                o_ref[rows, lanes] = jnp.concatenate([a[0], b[0]], axis=0).astype(BF)
                d_ref[0, rows, lanes] = jnp.concatenate([a[1], b[1]], axis=0).astype(BF)
                d_ref[1, rows, lanes] = jnp.concatenate([a[2], b[2]], axis=0).astype(BF)

    return pl.pallas_call(
        body, name=name, grid=(4, n // tm),
        in_specs=[pl.BlockSpec((2, None, tm, FSH), lambda j, i: (0, j, i, 0)),
                  pl.BlockSpec((2, None, 16, FSH), lambda j, i: (0, j, prev(i), 0)),
                  pl.BlockSpec((2, None, 3, FSH), lambda j, i: (0, j, 0, 0)),
                  pl.BlockSpec((2, None, 1, FSH), lambda j, i: (0, j, 0, 0))],
        out_specs=[pl.BlockSpec((None, tm, FSH), lambda j, i: (j, i, 0)),
                   pl.BlockSpec((2, None, tm, FSH), lambda j, i: (0, j, i, 0))],
        out_shape=[jax.ShapeDtypeStruct((4, n, FSH), BF), jax.ShapeDtypeStruct((2, 4, n, FSH), BF)],
        compiler_params=_cp(("parallel", "parallel")))(up, up, cw, cb)


def _ffn_conv_bwd(up, dgate, dact, cw, seq, name):
    n = up.shape[2]
    tm = min(CONV_ROW_BLOCK, seq)
    _, nxt = _halo16_maps(tm, n)

    def body(u_ref, g_ref, gn_ref, da_ref, dn_ref, w_ref, du_ref, dw_ref, db_ref):
        i = pl.program_id(1)
        sn = jnp.where(lax.rem((i + 1) * tm, seq) == 0, 0.0, 1.0)
        first = i == 0
        for lanes in _lane_blocks(FSH):
            lw = lanes.stop - lanes.start
            row = lax.broadcasted_iota(jnp.int32, (8, lw), 0)
            taps = [(w_ref[g, 0:1, lanes], w_ref[g, 1:2, lanes], w_ref[g, 2:3, lanes]) for g in range(2)]

            def dconv(gs, da):
                ds = [gs[g] * da for g in range(2)]
                return [(d, pltpu.roll(d, 7, 0), pltpu.roll(d, 6, 0)) for d in ds]

            def finish(cur, after, xs, sums):
                dups, new_sums = [], []
                for g in range(2):
                    w0, w1, w2 = taps[g]
                    s1 = jnp.where(row == 7, after[g][1], cur[g][1])
                    s2 = jnp.where(row >= 6, after[g][2], cur[g][2])
                    dups.append(w2 * cur[g][0] + w1 * s1 + w0 * s2)
                    acc = sums[g]
                    new_sums.append((acc[0] + xs[g] * s2, acc[1] + xs[g] * s1, acc[2] + xs[g] * cur[g][0],
                                     acc[3] + cur[g][0]))
                return dups, new_sums

            def emit(m, held, after, sums):
                (ta, xa), (tb, xb) = held
                dup_a, sums = finish(ta, tb, xa, sums)
                dup_b, sums = finish(tb, after, xb, sums)
                for g in range(2):
                    du_ref[g, m * 16:m * 16 + 16, lanes] = jnp.concatenate([dup_a[g], dup_b[g]], axis=0).astype(BF)
                return sums

            zero = jnp.zeros((8, lw), F32)
            sums = [(zero,) * 4, (zero,) * 4]
            held = None
            for m in range(tm // 16):
                rows = slice(m * 16, m * 16 + 16)
                x16 = [u_ref[g, rows, lanes].astype(F32) for g in range(2)]
                g16 = [g_ref[g, rows, lanes].astype(F32) for g in range(2)]
                d16 = da_ref[rows, lanes].astype(F32)
                ta = dconv([a[:8] for a in g16], d16[:8])
                tb = dconv([a[8:] for a in g16], d16[8:])
                if held is not None:
                    sums = emit(m - 1, held, ta, sums)
                held = ((ta, [x[:8] for x in x16]), (tb, [x[8:] for x in x16]))
            tn_ = dconv([gn_ref[g, :, lanes].astype(F32)[:8] for g in range(2)], dn_ref[:, lanes].astype(F32)[:8] * sn)
            sums = emit(tm // 16 - 1, held, tn_, sums)
            for g in range(2):
                for k in range(3):
                    _accumulate(dw_ref.at[g, k:k + 1, lanes], first, _colsum(sums[g][k]))
                _accumulate(db_ref.at[g, :, lanes], first, _colsum(sums[g][3]))

    return pl.pallas_call(
        body, name=name, grid=(4, n // tm),
        in_specs=[pl.BlockSpec((2, None, tm, FSH), lambda j, i: (0, j, i, 0)),
                  pl.BlockSpec((2, None, tm, FSH), lambda j, i: (0, j, i, 0)),
                  pl.BlockSpec((2, None, 16, FSH), lambda j, i: (0, j, nxt(i), 0)),
                  pl.BlockSpec((None, tm, FSH), lambda j, i: (j, i, 0)),
                  pl.BlockSpec((None, 16, FSH), lambda j, i: (j, nxt(i), 0)),
                  pl.BlockSpec((2, None, 3, FSH), lambda j, i: (0, j, 0, 0))],
        out_specs=[pl.BlockSpec((2, None, tm, FSH), lambda j, i: (0, j, i, 0)),
                   pl.BlockSpec((2, None, 3, FSH), lambda j, i: (0, j, 0, 0)),
                   pl.BlockSpec((2, None, 1, FSH), lambda j, i: (0, j, 0, 0))],
        out_shape=[jax.ShapeDtypeStruct((2, 4, n, FSH), BF), jax.ShapeDtypeStruct((2, 4, 3, FSH), F32),
                   jax.ShapeDtypeStruct((2, 4, 1, FSH), F32)],
        compiler_params=_cp(("parallel", "arbitrary")))(up, dgate, dgate, dact, dact, cw)


def _shortconv_fwd(p, cw, cb, seq, name):
    n = p.shape[0]
    tm = min(CONV_ROW_BLOCK, seq)
    prev, _ = _halo16_maps(tm, n)

    def body(p_ref, h_ref, w_ref, b_ref, o_ref):
        i = pl.program_id(1)
        scale = jnp.where(lax.rem(i * tm, seq) == 0, 0.0, 1.0)
        q = p_ref[:, FB:2 * FB].astype(F32) * p_ref[:, 2 * FB:].astype(F32)
        row = lax.broadcasted_iota(jnp.int32, q.shape, 0)
        hq = (h_ref[:, FB:2 * FB].astype(F32) * h_ref[:, 2 * FB:].astype(F32))[8:] * scale
        hrow = lax.broadcasted_iota(jnp.int32, hq.shape, 0)
        h7 = _colsum(jnp.where(hrow == 7, hq, 0.0))
        h6 = _colsum(jnp.where(hrow == 6, hq, 0.0))
        p1 = jnp.where(row == 0, h7, pltpu.roll(q, 1, 0))
        p2 = jnp.where(row == 0, h6, jnp.where(row == 1, h7, pltpu.roll(q, 2, 0)))
        conv = b_ref[...] + w_ref[0:1, :] * p2 + w_ref[1:2, :] * p1 + w_ref[2:3, :] * q
        o_ref[...] = (p_ref[:, :FB].astype(F32) * conv).astype(BF)

    return pl.pallas_call(
        body, name=name, grid=(D // FB, n // tm),
        in_specs=[pl.BlockSpec((tm, 3 * FB), lambda j, i: (i, j)),
                  pl.BlockSpec((16, 3 * FB), lambda j, i: (prev(i), j)),
                  pl.BlockSpec((3, FB), lambda j, i: (0, j)),
                  pl.BlockSpec((1, FB), lambda j, i: (0, j))],
        out_specs=pl.BlockSpec((tm, FB), lambda j, i: (i, j)),
        out_shape=jax.ShapeDtypeStruct((n, D), BF), compiler_params=_cp(("parallel", "parallel")))(p, p, cw, cb)


def _shortconv_bwd(p, dmix, cw, cb, seq, name):
    n = p.shape[0]
    tm = min(CONV_ROW_BLOCK, seq)
    ext = tm + 16
    prev, nxt = _halo16_maps(tm, n)

    def body(p_ref, pp_ref, pn_ref, dm_ref, dn_ref, w_ref, b_ref, dp_ref, dw_ref, db_ref, qx, cx):
        i = pl.program_id(1)
        sp = jnp.where(lax.rem(i * tm, seq) == 0, 0.0, 1.0)
        sn = jnp.where(lax.rem((i + 1) * tm, seq) == 0, 0.0, 1.0)
        bg, cg, hx = (p_ref[:, :FB].astype(F32), p_ref[:, FB:2 * FB].astype(F32), p_ref[:, 2 * FB:].astype(F32))
        dm = dm_ref[...].astype(F32)
        qx[0:8, :] = (pp_ref[:, FB:2 * FB].astype(F32) * pp_ref[:, 2 * FB:].astype(F32))[8:] * sp
        qx[8:8 + tm, :] = cg * hx
        qx[8 + tm:, :] = jnp.zeros((8, FB), F32)
        cx[0:8, :] = jnp.zeros((8, FB), F32)
        cx[8:8 + tm, :] = dm * bg
        cx[8 + tm:, :] = (dn_ref[...].astype(F32) * pn_ref[:, :FB].astype(F32))[:8] * sn
        q0 = qx[...]
        q1 = pltpu.roll(q0, 1, 0)
        q2 = pltpu.roll(q0, 2, 0)
        main = slice(8, 8 + tm)
        conv = b_ref[...] + w_ref[0:1, :] * q2[main] + w_ref[1:2, :] * q1[main] + w_ref[2:3, :] * q0[main]
        dc = cx[...]
        dq = (w_ref[2:3, :] * dc + w_ref[1:2, :] * pltpu.roll(dc, ext - 1, 0)
              + w_ref[0:1, :] * pltpu.roll(dc, ext - 2, 0))[main]
        dp_ref[:, :FB] = (dm * conv).astype(BF)
        dp_ref[:, FB:2 * FB] = (dq * hx).astype(BF)
        dp_ref[:, 2 * FB:] = (dq * cg).astype(BF)
        first = i == 0
        dcm = dc[main]
        _accumulate(dw_ref.at[0:1, :], first, _colsum(dcm * q2[main]))
        _accumulate(dw_ref.at[1:2, :], first, _colsum(dcm * q1[main]))
        _accumulate(dw_ref.at[2:3, :], first, _colsum(dcm * q0[main]))
        _accumulate(db_ref, first, _colsum(dcm))

    return pl.pallas_call(
        body, name=name, grid=(D // FB, n // tm),
        in_specs=[pl.BlockSpec((tm, 3 * FB), lambda j, i: (i, j)),
                  pl.BlockSpec((16, 3 * FB), lambda j, i: (prev(i), j)),
                  pl.BlockSpec((16, 3 * FB), lambda j, i: (nxt(i), j)),
                  pl.BlockSpec((tm, FB), lambda j, i: (i, j)),
                  pl.BlockSpec((16, FB), lambda j, i: (nxt(i), j)),
                  pl.BlockSpec((3, FB), lambda j, i: (0, j)),
                  pl.BlockSpec((1, FB), lambda j, i: (0, j))],
        out_specs=[pl.BlockSpec((tm, 3 * FB), lambda j, i: (i, j)),
                   pl.BlockSpec((3, FB), lambda j, i: (0, j)),
                   pl.BlockSpec((1, FB), lambda j, i: (0, j))],
        out_shape=[jax.ShapeDtypeStruct((n, 3 * D), BF), jax.ShapeDtypeStruct((3, D), F32),
                   jax.ShapeDtypeStruct((1, D), F32)],
        scratch_shapes=[pltpu.VMEM((ext, FB), F32), pltpu.VMEM((ext, FB), F32)],
        compiler_params=_cp(("parallel", "arbitrary")))(p, p, p, dmix, dmix, cw, cb)


def _gmlp_fwd(uv, wm, bst, gv, seq, name):
    n = uv.shape[0]
    tm = min(ROW_BLOCK, seq)

    def body(x_ref, w_ref, b_ref, g_ref, o_ref):
        ge_v = _gelu(x_ref[:, GM_W:].astype(F32))
        r = lax.rsqrt(jnp.mean(ge_v * ge_v, axis=-1, keepdims=True) + EPS)
        vn = (ge_v * r * g_ref[...]).astype(BF)
        for c in range(tm // CHUNK):
            rows = slice(c * CHUNK, (c + 1) * CHUNK)
            for h in range(GM_HEADS):
                cols = slice(h * CHUNK, (h + 1) * CHUNK)
                gate = jnp.dot(w_ref[h], vn[rows, cols], preferred_element_type=F32) + b_ref[:, h:h + 1]
                o_ref[rows, cols] = (_gelu(x_ref[rows, cols].astype(F32)) * gate).astype(BF)

    return pl.pallas_call(
        body, name=name, grid=(n // tm,),
        in_specs=[pl.BlockSpec((tm, 2 * GM_W), lambda i: (i, 0)),
                  pl.BlockSpec((GM_HEADS, CHUNK, CHUNK), lambda i: (0, 0, 0)),
                  pl.BlockSpec((CHUNK, GM_HEADS), lambda i: (0, 0)),
                  pl.BlockSpec((1, GM_W), lambda i: (0, 0))],
        out_specs=pl.BlockSpec((tm, GM_W), lambda i: (i, 0)),
        out_shape=jax.ShapeDtypeStruct((n, GM_W), BF), compiler_params=_cp(("parallel",)))(uv, wm, bst, gv)


def _gmlp_bwd(uv, dout, wm, wmt, bst, gv, seq, name):
    n = uv.shape[0]
    tm = min(ROW_BLOCK, seq)

    def body(x_ref, do_ref, w_ref, wt_ref, b_ref, g_ref, dx_ref, dw_ref, db_ref, dg_ref, dvn_scr):
        first = pl.program_id(0) == 0
        ge_v = _gelu(x_ref[:, GM_W:].astype(F32))
        r = lax.rsqrt(jnp.mean(ge_v * ge_v, axis=-1, keepdims=True) + EPS)
        vh = ge_v * r
        vn = (vh * g_ref[...]).astype(BF)
        tril = (lax.broadcasted_iota(jnp.int32, (CHUNK, CHUNK), 0)
                >= lax.broadcasted_iota(jnp.int32, (CHUNK, CHUNK), 1))
        for h in range(GM_HEADS):
            cols = slice(h * CHUNK, (h + 1) * CHUNK)
            dw = jnp.zeros((CHUNK, CHUNK), F32)
            dbs = jnp.zeros((CHUNK, 1), F32)
            for c in range(tm // CHUNK):
                rows = slice(c * CHUNK, (c + 1) * CHUNK)
                blk = vn[rows, cols]
                gate = jnp.dot(w_ref[h], blk, preferred_element_type=F32) + b_ref[:, h:h + 1]
                xu = x_ref[rows, cols].astype(F32)
                do = do_ref[rows, cols].astype(F32)
                dx_ref[rows, cols] = (do * gate * _gelu_grad(xu)).astype(BF)
                dgate = do * _gelu(xu)
                dgb = dgate.astype(BF)
                dw = dw + lax.dot_general(dgb, blk, _DIMS['nt'], preferred_element_type=F32)
                dbs = dbs + jnp.sum(dgate, axis=1, keepdims=True)
                dvn_scr[rows, cols] = jnp.dot(wt_ref[h], dgb, preferred_element_type=F32)
            _accumulate(dw_ref.at[h], first, jnp.where(tril, dw, 0.0))
            _accumulate(db_ref.at[h], first, dbs)
        dvn = dvn_scr[...]
        _accumulate(dg_ref, first, _colsum(dvn * vh))
        dvh = dvn * g_ref[...]
        dv = r * (dvh - vh * jnp.mean(dvh * vh, axis=-1, keepdims=True))
        dx_ref[:, GM_W:] = (dv * _gelu_grad(x_ref[:, GM_W:].astype(F32))).astype(BF)

    full3 = pl.BlockSpec((GM_HEADS, CHUNK, CHUNK), lambda i: (0, 0, 0))
    return pl.pallas_call(
        body, name=name, grid=(n // tm,),
        in_specs=[pl.BlockSpec((tm, 2 * GM_W), lambda i: (i, 0)), pl.BlockSpec((tm, GM_W), lambda i: (i, 0)),
                  full3, full3, pl.BlockSpec((CHUNK, GM_HEADS), lambda i: (0, 0)),
                  pl.BlockSpec((1, GM_W), lambda i: (0, 0))],
        out_specs=[pl.BlockSpec((tm, 2 * GM_W), lambda i: (i, 0)), full3,
                   pl.BlockSpec((GM_HEADS, CHUNK, 1), lambda i: (0, 0, 0)),
                   pl.BlockSpec((1, GM_W), lambda i: (0, 0))],
        out_shape=[jax.ShapeDtypeStruct((n, 2 * GM_W), BF), jax.ShapeDtypeStruct((GM_HEADS, CHUNK, CHUNK), F32),
                   jax.ShapeDtypeStruct((GM_HEADS, CHUNK, 1), F32), jax.ShapeDtypeStruct((1, GM_W), F32)],
        scratch_shapes=[pltpu.VMEM((tm, GM_W), F32)],
        compiler_params=_cp(("arbitrary",)))(uv, dout, wm, wmt, bst, gv)


def _s5_disc(lam_re, lam_im, log_dt, b_re, b_im):
    lr = jnp.minimum(lam_re, LAM_MAX)
    li = lam_im
    dt = jnp.exp(log_dt)
    mag = jnp.exp(lr * dt)
    ab_re = mag * jnp.cos(li * dt)
    ab_im = mag * jnp.sin(li * dt)
    den = lr * lr + li * li
    nr = ab_re - 1.0
    ni = ab_im
    z_re = (nr * lr + ni * li) / den
    z_im = (ni * lr - nr * li) / den
    return ab_re, ab_im, z_re * b_re - z_im * b_im, z_re * b_im + z_im * b_re


def _s5_disc_fwd(args, name):
    shp = jax.ShapeDtypeStruct(args[0].shape, F32)

    def body(*refs):
        outs = _s5_disc(*[r[...] for r in refs[:5]])
        for o_ref, o in zip(refs[5:], outs):
            o_ref[...] = o

    return pl.pallas_call(body, name=name, out_shape=[shp] * 4)(*args)


def _s5_disc_bwd(args, cts, name):
    shp = jax.ShapeDtypeStruct(args[0].shape, F32)

    def body(*refs):
        _, vjp = jax.vjp(_s5_disc, *[r[...] for r in refs[:5]])
        grads = vjp(tuple(r[...] for r in refs[5:9]))
        for o_ref, o in zip(refs[9:], grads):
            o_ref[...] = o

    return pl.pallas_call(body, name=name, out_shape=[shp] * 5)(*args, *cts)


def _cmul(a, b):
    return a[0] * b[0] - a[1] * b[1], a[0] * b[1] + a[1] * b[0]


def _scan_tables(ar, ai, reverse):
    if reverse:
        ai = -ai
    a1 = (ar, ai)
    a2 = _cmul(a1, a1)
    a3 = _cmul(a2, a1)
    a4 = _cmul(a2, a2)
    powers = [a1, a2, a3, a4, _cmul(a4, a1), _cmul(a4, a2), _cmul(a4, a3), _cmul(a4, a4)]
    row = lax.broadcasted_iota(jnp.int32, (8, NST), 0)
    zero = jnp.zeros((8, NST), F32)
    pr, pi = zero, zero
    for r in range(8):
        pw = powers[7 - r] if reverse else powers[r]
        pr = jnp.where(row == r, pw[0], pr)
        pi = jnp.where(row == r, pw[1], pi)
    levels = []
    for d, pw in ((1, a1), (2, a2), (4, a4)):
        ok = (row <= 7 - d) if reverse else (row >= d)
        levels.append((d, jnp.where(ok, pw[0], zero), jnp.where(ok, pw[1], zero)))
    return (pr, pi), levels


def _scan_block(src, dst, car, tables, n_tiles, reverse):
    (pr, pi), levels = tables
    row = lax.broadcasted_iota(jnp.int32, (8, NST), 0)
    out_row = 0 if reverse else 7

    def step(t, carry):
        cr, ci = carry
        tile = (n_tiles - 1 - t) if reverse else t
        rows = pl.ds(pl.multiple_of(tile * 8, 8), 8)
        xr = src[rows, 0:NST]
        xi = src[rows, NST:2 * NST]
        for d, dr, di in levels:
            shift = 8 - d if reverse else d
            rr = pltpu.roll(xr, shift, 0)
            ri = pltpu.roll(xi, shift, 0)
            xr, xi = xr + dr * rr - di * ri, xi + dr * ri + di * rr
        hr = xr + pr * cr - pi * ci
        hi = xi + pr * ci + pi * cr
        dst[rows, 0:NST] = hr
        dst[rows, NST:2 * NST] = hi
        return (_colsum(jnp.where(row == out_row, hr, 0.0)), _colsum(jnp.where(row == out_row, hi, 0.0)))

    cr, ci = lax.fori_loop(0, n_tiles, step, (car[0:1, 0:NST], car[0:1, NST:2 * NST]))
    car[0:1, 0:NST] = cr
    car[0:1, NST:2 * NST] = ci


def _s5_fwd(u, ab, bbt, cmat, dvec, wglu, bglu, seq, name):
    n = u.shape[0]
    tm = min(ROW_BLOCK, seq)

    def body(u_ref, ab_ref, bb_ref, c_ref, d_ref, w_ref, b_ref, h_ref, o_ref, xs, car):
        i = pl.program_id(0)

        @pl.when(lax.rem(i * tm, seq) == 0)
        def _():
            car[...] = jnp.zeros(car.shape, F32)

        uv = u_ref[...]
        xs[...] = jnp.dot(uv.astype(BF), bb_ref[...], preferred_element_type=F32)
        tables = _scan_tables(ab_ref[0:1, 0:NST], ab_ref[0:1, NST:2 * NST], False)
        _scan_block(xs, h_ref, car, tables, tm // 8, False)
        y = jnp.dot(h_ref[...].astype(BF), c_ref[...], preferred_element_type=F32) + d_ref[...] * uv
        g1 = _gelu(y)
        z = jnp.dot(g1.astype(BF), w_ref[...], preferred_element_type=F32) + b_ref[...]
        o_ref[...] = (g1 * _sigmoid(z)).astype(BF)

    const = lambda shape: pl.BlockSpec(shape, lambda i: (0, 0))
    return pl.pallas_call(
        body, name=name, grid=(n // tm,),
        in_specs=[pl.BlockSpec((tm, SSM_W), lambda i: (i, 0)), const((1, 2 * NST)), const((SSM_W, 2 * NST)),
                  const((2 * NST, SSM_W)), const((1, SSM_W)), const((SSM_W, SSM_W)), const((1, SSM_W))],
        out_specs=[pl.BlockSpec((tm, 2 * NST), lambda i: (i, 0)), pl.BlockSpec((tm, SSM_W), lambda i: (i, 0))],
        out_shape=[jax.ShapeDtypeStruct((n, 2 * NST), F32), jax.ShapeDtypeStruct((n, SSM_W), BF)],
        scratch_shapes=[pltpu.VMEM((tm, 2 * NST), F32), pltpu.VMEM((8, 2 * NST), F32)],
        compiler_params=_cp(("arbitrary",)))(u, ab, bbt, cmat, dvec, wglu, bglu)


def _s5_bwd(da, u, hst, ab, bbt, cmat, dvec, wglu, bglu, seq, name):
    n = u.shape[0]
    tm = min(ROW_BLOCK, seq)
    nb = n // tm
    blk = lambda r: nb - 1 - r
    prev, _ = _halo_maps(tm, n)

    def body(da_ref, u_ref, h_ref, hp_ref, ab_ref, bb_ref, c_ref, d_ref, w_ref, b_ref,
             du_ref, dw_ref, dbg_ref, dd_ref, dc_ref, dbb_ref, dab_ref, gs, car):
        r = pl.program_id(0)
        i = blk(r)
        first = r == 0

        @pl.when(lax.rem((i + 1) * tm, seq) == 0)
        def _():
            car[...] = jnp.zeros(car.shape, F32)

        uv = u_ref[...]
        dav = da_ref[...].astype(F32)
        hb = h_ref[...]
        hb16 = hb.astype(BF)
        dvv = d_ref[...]
        y = jnp.dot(hb16, c_ref[...], preferred_element_type=F32) + dvv * uv
        g1 = _gelu(y)
        g16 = g1.astype(BF)
        s = _sigmoid(jnp.dot(g16, w_ref[...], preferred_element_type=F32) + b_ref[...])
        dz = dav * g1 * s * (1.0 - s)
        dz16 = dz.astype(BF)
        dg1 = dav * s + lax.dot_general(dz16, w_ref[...], _DIMS['nt'], preferred_element_type=F32)
        _accumulate(dw_ref, first, lax.dot_general(g16, dz16, _DIMS['tn'], preferred_element_type=F32))
        _accumulate(dbg_ref, first, _colsum(dz))
        dy = dg1 * _gelu_grad(y)
        dy16 = dy.astype(BF)
        _accumulate(dd_ref, first, _colsum(dy * uv))
        _accumulate(dc_ref, first, lax.dot_general(hb16, dy16, _DIMS['tn'], preferred_element_type=F32))
        gs[...] = lax.dot_general(dy16, c_ref[...], _DIMS['nt'], preferred_element_type=F32)
        tables = _scan_tables(ab_ref[0:1, 0:NST], ab_ref[0:1, NST:2 * NST], True)
        _scan_block(gs, gs, car, tables, tm // 8, True)
        g = gs[...]
        g16b = g.astype(BF)
        sp = jnp.where(lax.rem(i * tm, seq) == 0, 0.0, 1.0)
        row = lax.broadcasted_iota(jnp.int32, hb.shape, 0)
        hprev = jnp.where(row == 0, hp_ref[7:8, :] * sp, pltpu.roll(hb, 1, 0))
        gr, gi = g[:, :NST], g[:, NST:]
        hr, hi = hprev[:, :NST], hprev[:, NST:]
        _accumulate(dab_ref.at[:, 0:NST], first, _colsum(gr * hr + gi * hi))
        _accumulate(dab_ref.at[:, NST:2 * NST], first, _colsum(gi * hr - gr * hi))
        _accumulate(dbb_ref, first, lax.dot_general(uv.astype(BF), g16b, _DIMS['tn'], preferred_element_type=F32))
        du = dy * dvv + lax.dot_general(g16b, bb_ref[...], _DIMS['nt'], preferred_element_type=F32)
        du_ref[...] = du.astype(BF)

    const = lambda shape: pl.BlockSpec(shape, lambda r: (0, 0))
    rowspec = lambda w: pl.BlockSpec((tm, w), lambda r: (blk(r), 0))
    return pl.pallas_call(
        body, name=name, grid=(nb,),
        in_specs=[rowspec(SSM_W), rowspec(SSM_W), rowspec(2 * NST),
                  pl.BlockSpec((8, 2 * NST), lambda r: (prev(blk(r)), 0)),
                  const((1, 2 * NST)), const((SSM_W, 2 * NST)), const((2 * NST, SSM_W)), const((1, SSM_W)),
                  const((SSM_W, SSM_W)), const((1, SSM_W))],
        out_specs=[rowspec(SSM_W), const((SSM_W, SSM_W)), const((1, SSM_W)), const((1, SSM_W)),
                   const((2 * NST, SSM_W)), const((SSM_W, 2 * NST)), const((1, 2 * NST))],
        out_shape=[jax.ShapeDtypeStruct((n, SSM_W), BF), jax.ShapeDtypeStruct((SSM_W, SSM_W), F32),
                   jax.ShapeDtypeStruct((1, SSM_W), F32), jax.ShapeDtypeStruct((1, SSM_W), F32),
                   jax.ShapeDtypeStruct((2 * NST, SSM_W), F32), jax.ShapeDtypeStruct((SSM_W, 2 * NST), F32),
                   jax.ShapeDtypeStruct((1, 2 * NST), F32)],
        scratch_shapes=[pltpu.VMEM((tm, 2 * NST), F32), pltpu.VMEM((8, 2 * NST), F32)],
        compiler_params=_cp(("arbitrary",)))(da, u, hst, hst, ab, bbt, cmat, dvec, wglu, bglu)


def _s5_rows(lam_re, lam_im, log_dt, b_re, b_im):
    rep = lambda a: jnp.broadcast_to(a[:, None, :], (SSM_G, SSM_H, SSM_P)).reshape(SSM_W, SSM_P)
    dt = jnp.broadcast_to(log_dt[:, None, None], (SSM_G, SSM_H, SSM_P)).reshape(SSM_W, SSM_P)
    tr = lambda b: b.transpose(0, 2, 1).reshape(SSM_W, SSM_P)
    return rep(lam_re), rep(lam_im), dt, tr(b_re), tr(b_im)


def _block_diag(rows_gp, inner):
    eye = jnp.eye(SSM_G, dtype=rows_gp.dtype)
    return (rows_gp[:, :, None, :] * eye[:, None, :, None]).reshape(SSM_G * inner, SSM_G * SSM_P)


def _diag_blocks(mat, inner):
    m4 = mat.reshape(SSM_G, inner, SSM_G, SSM_P)
    return jnp.stack([m4[g, :, g, :] for g in range(SSM_G)])


def _interleave(w, parts):
    lead = w.shape[:-1]
    nb = w.shape[-1] // (parts * FB)
    return jnp.swapaxes(w.reshape(lead + (parts, nb, FB)), -3, -2).reshape(w.shape)


def _deinterleave(w, parts):
    lead = w.shape[:-1]
    nb = w.shape[-1] // (parts * FB)
    return jnp.swapaxes(w.reshape(lead + (nb, parts, FB)), -3, -2).reshape(w.shape)


def _ffn_fwd(h, f, w_up, w_down, cw, cb, seq, tag, tail):
    n = h.shape[0]
    tm = min(4096, n)
    ni = n // tm
    up = _matmul_spec(
        f, w_up, 'nn', (ni, NDEV, 1),
        pl.BlockSpec((tm, D), lambda i, s, k: (i, 0)),
        pl.BlockSpec((D, FSH), lambda i, s, k: (s, 0)),
        pl.BlockSpec((tm, FSH), lambda i, s, k: (s * ni + i, 0)), (NDEV * n, FSH), f"{tag}_up", out_dtype=BF)
    up = up.reshape(2, 4, n, FSH)
    act, dgate = _ffn_conv_fwd(up, cw, cb, seq, f"{tag}_conv")
    out = _matmul_shards(act, w_down.reshape(4, FSH, D), 'nn', 512, D, f"{tag}_down", resid=h, tail=tail)
    return out, (f, up, act, dgate)


def _ffn_bwd(dh, dhb, h, g, w_up, w_down, cw, cb, saved, seq, tag):
    f, up, act, dgate = saved
    n = h.shape[0]
    tm = min(4096, n)
    ni = n // tm
    tk = min(2048, n)
    dact = _matmul_spec(
        dhb, w_down, 'nt', (ni, 4, 1),
        pl.BlockSpec((tm, D), lambda i, j, k: (i, 0)),
        pl.BlockSpec((FSH, D), lambda i, j, k: (j, 0)),
        pl.BlockSpec((tm, FSH), lambda i, j, k: (j * ni + i, 0)), (4 * n, FSH), f"{tag}_ddown_x", out_dtype=BF)
    dw_down = _matmul_tn_shards(act, dhb, 2, tk, f"{tag}_ddown_w")
    dup, dcw, dcb = _ffn_conv_bwd(up, dgate, dact.reshape(4, n, FSH), cw, seq, f"{tag}_dconv")
    dh_in, dhb_in, dg = _matmul_shards(dup.reshape(NDEV, n, FSH), w_up.reshape(NDEV, D, FSH), 'nt', 256, D,
                                       f"{tag}_dup_x", tail=('norm_bwd', h, g, dh))
    dw_up = _matmul_tn_shards(f, dup.reshape(NDEV, n, FSH), 2, tk, f"{tag}_dup_w")
    grads = dict(g=dg, w_up=dw_up, w_down=dw_down.reshape(NDEV, DFF // NDEV, D),
                 cw=dcw.reshape(NDEV, 3, FSH), cb=dcb.reshape(2 * DFF))
    return dh_in, dhb_in, grads


def _col_shards(w, width):
    return w.reshape(w.shape[0], NDEV, width).transpose(1, 0, 2)


def _local_step(x, tgt, w, gw, wait_ffn0, wait_rest, token, scatter, seq):
    bf = lambda a: a.astype(BF)
    row = lambda a: a.reshape(1, -1).astype(F32)
    w_ev = gw['ev_w_in'].transpose(1, 0, 2).reshape(D, 1792)
    w_ev_s5, w_ev_gm = w_ev[:, :SSM_W], w_ev[:, SSM_W:]
    w_evo = gw['ev_w_out'].reshape(D, D)
    f_cb = [w['ffn_conv_b'][l].reshape(2, 4, 1, FSH) for l in range(2)]
    tril = jnp.tril(jnp.ones((CHUNK, CHUNK), dtype=bool))
    gm_w = jnp.where(tril, w['gm_w_s'][0], 0.0)
    gm_wm, gm_wmt = bf(gm_w), bf(jnp.swapaxes(gm_w, 1, 2))
    gm_bt = w['gm_b_s'][0].T
    gm_gv = row(w['gm_v_g'][0])
    s5_in = _s5_rows(w['s5_lam_re'][0], w['s5_lam_im'][0], w['s5_log_dt'][0], w['s5_b_re'][0], w['s5_b_im'][0])
    ab_re, ab_im, bb_re, bb_im = _s5_disc_fwd(s5_in, "s5_disc")
    first_h = lambda a: a.reshape(SSM_G, SSM_H, SSM_P)[:, 0, :].reshape(1, NST)
    s5_ab = jnp.concatenate([first_h(ab_re), first_h(ab_im)], axis=1)
    to_gp = lambda a: a.reshape(SSM_G, SSM_H, SSM_P)
    s5_bbt = bf(jnp.concatenate([_block_diag(to_gp(bb_re), SSM_H), _block_diag(to_gp(bb_im), SSM_H)], axis=1))
    s5_cmat = bf(jnp.concatenate([_block_diag(w['s5_c_re'][0], SSM_H).T, -_block_diag(w['s5_c_im'][0], SSM_H).T],
                                 axis=0))
    s5_d, s5_bg, s5_wg = row(w['s5_d'][0]), row(w['s5_b_glu'][0]), gw['s5_w_glu'].reshape(SSM_W, SSM_W)
    g_mix = [row(w['mix_norm_g'][0]) + token[0:1, 0:1], row(w['mix_norm_g'][1])]
    g_ffn = [row(w['ffn_norm_g'][l]) for l in range(2)]
    g_fin = row(w['final_norm_g'])

    h0 = x
    y0 = _rmsnorm_fwd(h0, g_mix[0], "ev_norm")
    p_s5 = _matmul(y0, w_ev_s5, 'nn', 1024, 256, D, "ev_in_s5")
    p_gm = _matmul(y0, w_ev_gm, 'nn', 1024, 2 * GM_W, D, "ev_in_gm", out_dtype=BF)
    hst, a_out = _s5_fwd(p_s5, s5_ab, s5_bbt, s5_cmat, s5_d, s5_wg, s5_bg, seq, "s5_fwd")
    b_out = _gmlp_fwd(p_gm, gm_wm, gm_bt, gm_gv, seq, "gmlp_fwd")
    mixcat = jnp.concatenate([a_out, b_out], axis=1)
    h1, f0 = _matmul(mixcat, w_evo, 'nn', 1024, D, D, "ev_out", resid=h0, tail=('norm_fwd', g_ffn[0]))
    g0 = wait_ffn0(mixcat)
    w_up0, w_dn0 = g0['ffn_w_up0'].reshape(NDEV * D, FSH), g0['ffn_w_down0'].reshape(DFF, D)
    f_cw0 = g0['ffn_conv_w0'].reshape(2, 4, 3, FSH)
    (h2, y1), ffn0 = _ffn_fwd(h1, f0, w_up0, w_dn0, f_cw0, f_cb[0], seq, "ffn0", ('norm_fwd', g_mix[1]))
    g1 = wait_rest(h2)
    w_od = _interleave(g1['od_w_in'].transpose(1, 0, 2).reshape(D, 3 * D), 3)
    w_odo = g1['od_w_out'].reshape(D, D)
    od_cw = g1['od_conv_w'].transpose(1, 0, 2).reshape(3, D)
    od_cb = g1['od_conv_b'].reshape(1, D)
    w_up1, w_dn1 = g1['ffn_w_up1'].reshape(NDEV * D, FSH), g1['ffn_w_down1'].reshape(DFF, D)
    f_cw1 = g1['ffn_conv_w1'].reshape(2, 4, 3, FSH)
    p_od = _matmul(y1, w_od, 'nn', 1024, 3 * D // 2, D, "od_in", out_dtype=BF)
    mixin = _shortconv_fwd(p_od, od_cw, od_cb, seq, "od_conv")
    h3, f1 = _matmul(mixin, w_odo, 'nn', 1024, D, D, "od_out", resid=h2, tail=('norm_fwd', g_ffn[1]))
    (loss, dh4, dh4b, dg_fin), ffn1 = _ffn_fwd(h3, f1, w_up1, w_dn1, f_cw1, f_cb[1], seq, "ffn1",
                                                ('loss', g_fin, tgt))

    dh3, dh3b, gf1 = _ffn_bwd(dh4, dh4b, h3, g_ffn[1], w_up1, w_dn1, f_cw1, f_cb[1], ffn1, seq, "ffn1")
    dmixin = _matmul(dh3b, w_odo, 'nt', 1024, D, D, "od_dout_x", out_dtype=BF)
    dw_odo = _matmul(mixin, dh3b, 'tn', D, 512, 4096, "od_dout_w", out_dtype=BF)
    dp_od, d_od_cw, d_od_cb = _shortconv_bwd(p_od, dmixin, od_cw, od_cb, seq, "od_dconv")
    dw_od = _matmul(y1, dp_od, 'tn', D, 3 * D // 2, 2048, "od_din_w", out_dtype=BF)
    sent = scatter("scatter_layer1", {
        'od_w_in': _col_shards(_deinterleave(dw_od, 3), 384), 'od_conv_w': _col_shards(d_od_cw, D // NDEV),
        'od_conv_b': d_od_cb.reshape(NDEV, 1, D // NDEV), 'od_w_out': dw_odo.reshape(NDEV, D // NDEV, D),
        'ffn_w_up1': gf1['w_up'], 'ffn_conv_w1': gf1['cw'], 'ffn_w_down1': gf1['w_down']})
    dh2, dh2b, dg_mix1 = _matmul(dp_od, w_od, 'nt', 512, D, 3 * D, "od_din_x",
                                 tail=('norm_bwd', h2, g_mix[1] + sent[0:1, 0:1], dh3))
    dh1, dh1b, gf0 = _ffn_bwd(dh2, dh2b, h1, g_ffn[0], w_up0, w_dn0, f_cw0, f_cb[0], ffn0, seq, "ffn0")
    dmix_a = _matmul(dh1b, w_evo[:SSM_W], 'nt', 1024, SSM_W, D, "ev_dout_xa", out_dtype=BF)
    dmix_b = _matmul(dh1b, w_evo[SSM_W:], 'nt', 1024, GM_W, D, "ev_dout_xb", out_dtype=BF)
    dw_evo = _matmul(mixcat, dh1b, 'tn', D, 512, 4096, "ev_dout_w", out_dtype=BF)
    sent = scatter("scatter_ffn0", {'ffn_w_up0': gf0['w_up'], 'ffn_conv_w0': gf0['cw'], 'ffn_w_down0': gf0['w_down'],
                                    'ev_w_out': dw_evo.reshape(NDEV, D // NDEV, D)})
    dp_s5, d_wg, d_bg, d_d, d_cmat, d_bbt, d_ab = _s5_bwd(dmix_a, p_s5, hst, s5_ab, s5_bbt, s5_cmat,
                                                           s5_d + sent[0:1, 0:1], s5_wg, s5_bg, seq, "s5_bwd")
    dp_gm, d_gmw, d_gmb, d_gmg = _gmlp_bwd(p_gm, dmix_b, gm_wm, gm_wmt, gm_bt, gm_gv, seq, "gmlp_bwd")
    dp_ev = jnp.concatenate([dp_s5, dp_gm], axis=1)
    dw_ev = _matmul(y0, dp_ev, 'tn', D, SSM_W + 2 * GM_W, 2048, "ev_din_w", out_dtype=BF)
    sent = scatter("scatter_even", {'ev_w_in': _col_shards(dw_ev, 224),
                                    's5_w_glu': d_wg.reshape(NDEV, SSM_W // NDEV, SSM_W)})
    grad_x, _, dg_mix0 = _matmul(dp_ev, w_ev, 'nt', 512, D, SSM_W + 2 * GM_W, "ev_din_x",
                                 tail=('norm_bwd', h0, g_mix[0] + sent[0:1, 0:1], dh1))

    put_h0 = lambda a: jnp.zeros((SSM_G, SSM_H, SSM_P), F32).at[:, 0, :].set(a.reshape(SSM_G, SSM_P)).reshape(
        SSM_W, SSM_P)
    ct = (put_h0(d_ab[:, :NST]), put_h0(d_ab[:, NST:]),
          _diag_blocks(d_bbt[:, :NST], SSM_H).reshape(SSM_W, SSM_P),
          _diag_blocks(d_bbt[:, NST:], SSM_H).reshape(SSM_W, SSM_P))
    d_lre, d_lim, d_ldt, d_bre, d_bim = _s5_disc_bwd(s5_in, ct, "s5_ddisc")
    over_h = lambda a: a.reshape(SSM_G, SSM_H, SSM_P).sum(axis=1)
    un_tr = lambda a: a.reshape(SSM_G, SSM_H, SSM_P).transpose(0, 2, 1)
    d_cre = _diag_blocks(d_cmat[:NST].T, SSM_H)
    d_cim = -_diag_blocks(d_cmat[NST:].T, SSM_H)

    repl = {
        'mix_norm_g': jnp.concatenate([dg_mix0, dg_mix1], axis=0),
        'ffn_norm_g': jnp.concatenate([gf0['g'], gf1['g']], axis=0),
        'final_norm_g': dg_fin.reshape(D),
        's5_lam_re': over_h(d_lre)[None], 's5_lam_im': over_h(d_lim)[None],
        's5_log_dt': over_h(d_ldt).sum(axis=1)[None],
        's5_b_re': un_tr(d_bre)[None], 's5_b_im': un_tr(d_bim)[None],
        's5_c_re': d_cre[None], 's5_c_im': d_cim[None],
        's5_d': d_d, 's5_b_glu': d_bg,
        'gm_w_s': d_gmw[None], 'gm_b_s': d_gmb.reshape(1, GM_HEADS, CHUNK), 'gm_v_g': d_gmg,
        'ffn_conv_b': jnp.stack([gf0['cb'], gf1['cb']]),
    }
    return loss, grad_x, repl


HBM_SPEC = pl.BlockSpec(memory_space=pltpu.HBM)


def _at_axis(ref, pos, index):
    return ref.at[(slice(None),) * pos + (index,)]


def _all_gather(shards, positions, name):
    n = len(shards)

    def body(*refs):
        xs, outs = refs[:n], refs[n:2 * n]
        send_sems, recv_sems, local_sems = refs[2 * n:]
        x, y, c = lax.axis_index("x"), lax.axis_index("y"), lax.axis_index("c")
        me, sibling = (x, y, c), (x, y, 1 - c)
        chips = [(1 - x, y), (x, 1 - y), (1 - x, 1 - y)]

        def block(p, dev):
            return _at_axis(outs[p], positions[p], 4 * dev[0] + 2 * dev[1] + dev[2])

        def copy(p, k, dev, to, src=None):
            return pltpu.make_async_remote_copy(
                src_ref=block(p, dev) if src is None else src, dst_ref=block(p, dev),
                send_sem=send_sems.at[p, k], recv_sem=recv_sems.at[p, k], device_id=to, device_id_type=MESH_T)

        mine = [pltpu.make_async_copy(xs[p], block(p, me), local_sems.at[p]) for p in range(n)]
        for cp in mine:
            cp.start()
        first = [copy(p, 0, me, sibling, src=xs[p]) for p in range(n)]
        first += [copy(p, 1 + j, me, (*chip, c), src=xs[p]) for j, chip in enumerate(chips) for p in range(n)]
        for cp in first:
            cp.start()
        passed = []
        for j, chip in enumerate(chips):
            for p in range(n):
                copy(p, 1 + j, (*chip, c), me).wait_recv()
                fwd = copy(p, 4 + j, (*chip, c), sibling)
                fwd.start()
                passed.append(fwd)
        for p in range(n):
            copy(p, 0, sibling, me).wait_recv()
        for j, chip in enumerate(chips):
            for p in range(n):
                copy(p, 4 + j, (*chip, 1 - c), me).wait_recv()
        for cp in first + passed:
            cp.wait_send()
        for cp in mine:
            cp.wait()

    out_shape = [jax.ShapeDtypeStruct(s.shape[:pos] + (NDEV,) + s.shape[pos:], s.dtype)
                 for s, pos in zip(shards, positions)]
    return pl.pallas_call(
        body, name=name, out_shape=out_shape, in_specs=[HBM_SPEC] * n, out_specs=[HBM_SPEC] * n,
        scratch_shapes=[pltpu.SemaphoreType.DMA((n, 7)), pltpu.SemaphoreType.DMA((n, 7)),
                        pltpu.SemaphoreType.DMA((n,))])(*shards)


def _other_devices(x, y, c):
    flip = lambda v, bit: 1 - v if bit else v
    return [(flip(x, k >> 2 & 1), flip(y, k >> 1 & 1), flip(c, k & 1)) for k in range(1, NDEV)]


SEM_SPEC = pl.BlockSpec(memory_space=pltpu.SEMAPHORE)
START_EFFECT = pltpu.SideEffectType.DATAFLOW_SIDE_EFFECTING


def _send_start(arrays, scatter, name):
    n = len(arrays)
    lands = [lax.empty((NDEV,) + (a.shape[1:] if scatter else a.shape), a.dtype) for a in arrays]

    def body(*refs):
        xs, ls = refs[:n], refs[n:2 * n]
        send_sems, recv_sems, own_sems, token = refs[2 * n], refs[2 * n + 1], refs[2 * n + 2], refs[4 * n + 3]
        x, y, c = lax.axis_index("x"), lax.axis_index("y"), lax.axis_index("c")
        me = 4 * x + 2 * y + c
        for k, peer in enumerate(_other_devices(x, y, c)):
            for p in range(n):
                src = xs[p].at[4 * peer[0] + 2 * peer[1] + peer[2]] if scatter else xs[p]
                pltpu.make_async_remote_copy(
                    src_ref=src, dst_ref=ls[p].at[me], send_sem=send_sems.at[p * (NDEV - 1) + k],
                    recv_sem=recv_sems.at[p * (NDEV - 1) + k], device_id=peer, device_id_type=MESH_T).start()
        for p in range(n):
            pltpu.make_async_copy(xs[p].at[me] if scatter else xs[p], ls[p].at[me], own_sems.at[p]).start()
        token[...] = jnp.zeros(token.shape, F32)

    sems = pltpu.SemaphoreType.DMA((n * (NDEV - 1),))
    out_shape = ([sems, sems, pltpu.SemaphoreType.DMA((n,))]
                 + [pltpu.HBM(a.shape, a.dtype) for a in list(arrays) + lands] + [jax.ShapeDtypeStruct((8, 128), F32)])
    res = pl.pallas_call(
        body, name=name, out_shape=out_shape, in_specs=[HBM_SPEC] * (2 * n),
        out_specs=[SEM_SPEC] * 3 + [HBM_SPEC] * (2 * n) + [pl.BlockSpec(memory_space=pltpu.VMEM)],
        input_output_aliases={i: 3 + i for i in range(2 * n)},
        compiler_params=pltpu.CompilerParams(has_side_effects=START_EFFECT))(
            *[pltpu.with_memory_space_constraint(a, pltpu.HBM) for a in list(arrays) + lands])
    return res[:3], res[3:3 + n], res[3 + n:3 + 2 * n], res[3 + 2 * n]


def _send_wait(started, scatter, after, name):
    sems, arrays, lands, _ = started
    n = len(arrays)

    def body(*refs):
        xs, ls = refs[:n], refs[n:2 * n]
        send, recv, own = refs[2 * n:2 * n + 3]
        x, y, c = lax.axis_index("x"), lax.axis_index("y"), lax.axis_index("c")
        me = 4 * x + 2 * y + c
        for p in range(n):
            pltpu.make_async_copy(xs[p].at[me] if scatter else xs[p], ls[p].at[me], own.at[p]).wait()
        for k, peer in enumerate(_other_devices(x, y, c)):
            slot = 4 * peer[0] + 2 * peer[1] + peer[2]
            for p in range(n):
                cp = pltpu.make_async_remote_copy(
                    src_ref=xs[p].at[slot] if scatter else xs[p], dst_ref=ls[p].at[slot],
                    send_sem=send.at[p * (NDEV - 1) + k], recv_sem=recv.at[p * (NDEV - 1) + k], device_id=peer,
                    device_id_type=MESH_T)
                cp.wait_send()
                cp.wait_recv()

    res = pl.pallas_call(
        body, name=name, out_shape=[pltpu.HBM(a.shape, a.dtype) for a in list(arrays) + list(lands)],
        in_specs=[HBM_SPEC] * (2 * n) + [SEM_SPEC] * 3 + [pl.BlockSpec(memory_space=pl.ANY)],
        out_specs=[HBM_SPEC] * (2 * n), input_output_aliases={i: i for i in range(2 * n)},
        compiler_params=pltpu.CompilerParams(has_side_effects=START_EFFECT))(
            *arrays, *lands, *sems, after)
    return res[n:]


def _row_block(rows, cols, itemsize=4, target=2**20):
    best = None
    for tr in range(16, rows + 1, 16):
        if rows % tr == 0 and tr * cols * itemsize <= target:
            best = tr
    return best or rows


def _adamw(w, m, v, gparts, name):
    parts, rows, cols = gparts.shape
    tr = _row_block(rows, cols, target=2**19)
    bc1 = 1.0 - ADAM_B1 ** ADAM_STEP
    bc2 = 1.0 - ADAM_B2 ** ADAM_STEP

    def body(w_ref, m_ref, v_ref, g_ref, go_ref, d_ref, mo_ref, vo_ref):
        g = g_ref[0].astype(F32)
        for k in range(1, parts):
            g = g + g_ref[k].astype(F32)
        mn = ADAM_B1 * m_ref[...] + (1.0 - ADAM_B1) * g
        vn = ADAM_B2 * v_ref[...] + (1.0 - ADAM_B2) * (g * g)
        go_ref[...] = g
        mo_ref[...] = mn
        vo_ref[...] = vn
        d_ref[...] = -ADAM_LR * ((mn / bc1) / (jnp.sqrt(vn / bc2) + ADAM_EPS) + ADAM_WD * w_ref[...])

    blk = pl.BlockSpec((tr, cols), lambda i: (i, 0))
    shp = jax.ShapeDtypeStruct((rows, cols), F32)
    return pl.pallas_call(
        body, name=name, grid=(rows // tr,),
        in_specs=[blk, blk, blk, pl.BlockSpec((parts, tr, cols), lambda i: (0, i, 0))],
        out_specs=[blk] * 4, out_shape=[shp] * 4, compiler_params=_cp(("parallel",)))(w, m, v, gparts)


def _pack(arrays, rows):
    flat = jnp.concatenate([a.reshape(-1).astype(F32) for a in arrays])
    return jnp.pad(flat, (0, rows * PACK_COLS - flat.shape[0])).reshape(rows, PACK_COLS)


def _unpack(buf, shapes):
    flat = buf.reshape(-1)
    out, off = [], 0
    for shp in shapes:
        size = int(np.prod(shp))
        out.append(flat[off:off + size].reshape(shp))
        off += size
    return out


REPL_SHAPES = {'mix_norm_g': (2, 1024), 'ffn_norm_g': (2, 1024), 'final_norm_g': (1024,), 's5_lam_re': (1, 16, 64),
               's5_lam_im': (1, 16, 64), 's5_log_dt': (1, 16), 's5_b_re': (1, 16, 64, 16), 's5_b_im': (1, 16, 64, 16),
               's5_c_re': (1, 16, 16, 64), 's5_c_im': (1, 16, 16, 64), 's5_d': (1, 256), 's5_b_glu': (1, 256),
               'gm_w_s': (1, 6, 128, 128), 'gm_b_s': (1, 6, 128), 'gm_v_g': (1, 768), 'ffn_conv_b': (2, 5632)}
REPL_ELEMS = sum(int(np.prod(REPL_SHAPES[n])) for n in REPL_ORDER)
REPL_ROWS = -(-REPL_ELEMS // (PACK_COLS * 8)) * 8

GATHER_DTYPE = {'ev_w_in': BF, 'ev_w_out': BF, 's5_w_glu': BF, 'od_w_in': BF, 'od_conv_w': F32, 'od_conv_b': F32,
                'od_w_out': BF, 'ffn_w_up': BF, 'ffn_conv_w': F32, 'ffn_w_down': BF}
GATHER_EVEN = ['ev_w_in', 'ev_w_out', 's5_w_glu']
GATHER_FFN0 = ['ffn_w_up0', 'ffn_conv_w0', 'ffn_w_down0']
GATHER_REST = ['od_w_in', 'od_conv_w', 'od_conv_b', 'od_w_out', 'ffn_w_up1', 'ffn_conv_w1', 'ffn_w_down1']

def _squeeze_lead(a):
    return a.reshape(a.shape[1:]) if a.shape[0] == 1 and a.ndim > 2 else a


def kernel(x, mix_norm_g, ffn_norm_g, final_norm_g, ev_w_in, ev_w_out, s5_lam_re, s5_lam_im, s5_log_dt, s5_b_re, s5_b_im, s5_c_re, s5_c_im, s5_d, s5_w_glu, s5_b_glu, gm_w_s, gm_b_s, gm_v_g, od_w_in, od_conv_w, od_conv_b, od_w_out, ffn_w_up, ffn_conv_w, ffn_conv_b, ffn_w_down, loss_target, m_mix_norm_g, m_ffn_norm_g, m_final_norm_g, m_ev_w_in, m_ev_w_out, m_s5_lam_re, m_s5_lam_im, m_s5_log_dt, m_s5_b_re, m_s5_b_im, m_s5_c_re, m_s5_c_im, m_s5_d, m_s5_w_glu, m_s5_b_glu, m_gm_w_s, m_gm_b_s, m_gm_v_g, m_od_w_in, m_od_conv_w, m_od_conv_b, m_od_w_out, m_ffn_w_up, m_ffn_conv_w, m_ffn_conv_b, m_ffn_w_down, v_mix_norm_g, v_ffn_norm_g, v_final_norm_g, v_ev_w_in, v_ev_w_out, v_s5_lam_re, v_s5_lam_im, v_s5_log_dt, v_s5_b_re, v_s5_b_im, v_s5_c_re, v_s5_c_im, v_s5_d, v_s5_w_glu, v_s5_b_glu, v_gm_w_s, v_gm_b_s, v_gm_v_g, v_od_w_in, v_od_conv_w, v_od_conv_b, v_od_w_out, v_ffn_w_up, v_ffn_conv_w, v_ffn_conv_b, v_ffn_w_down):
    given = dict(locals())
    weights = {n: given[n] for n in WEIGHT_ORDER}
    nseq, seq, _ = x.shape

    send = {}
    for name in SHARDED_ORDER:
        a = weights[name].astype(GATHER_DTYPE[name])
        if a.shape[0] == 2:
            send[name + '0'], send[name + '1'] = a[0], a[1]
        else:
            send[name] = _squeeze_lead(a)
    gathers = [_send_start([send[n] for n in names], False, f"gather_{tag}_start")
               for tag, names in (("ffn0", GATHER_FFN0), ("rest", GATHER_REST))]
    token = gathers[0][3] + gathers[1][3]

    def waiter(tag, names, started):
        return lambda after: dict(zip(names, _send_wait(started, False, after, f"gather_{tag}_wait")))

    gathered = dict(zip(GATHER_EVEN, _all_gather([send[n] for n in GATHER_EVEN], [0] * len(GATHER_EVEN),
                                                 "gather_even")))

    scatters = []

    def scatter(tag, grads):
        names = list(grads)
        started = _send_start([grads[n].astype(BF) for n in names], True, f"{tag}_start")
        scatters.append((tag, names, started))
        return started[3]

    loss_row, grad_x, g_repl = _local_step(
        x.reshape(nseq * seq, D), loss_target.reshape(nseq * seq, D), weights, gathered,
        waiter("ffn0", GATHER_FFN0, gathers[0]), waiter("rest", GATHER_REST, gathers[1]), token, scatter, seq)
    loss = lax.psum(loss_row[0, 0], ("x", "y", "c"))

    parts = {}
    for tag, names, started in scatters:
        parts.update(zip(names, _send_wait(started, True, grad_x, f"{tag}_wait")))
    repl_parts = _all_gather([_pack([g_repl[n] for n in REPL_ORDER], REPL_ROWS)], [0], "gather_small_grads")[0]

    out = {}
    for name in SHARDED_ORDER:
        w = weights[name]
        if name + '0' in parts:
            gp = jnp.stack([parts[name + '0'], parts[name + '1']], axis=1)
        else:
            gp = parts[name]
        to_rows = lambda a: a.reshape(-1, w.shape[-1])
        res = _adamw(to_rows(w), to_rows(given["m_" + name]), to_rows(given["v_" + name]),
                     gp.reshape(NDEV, -1, w.shape[-1]), f"adamw_{name}")
        out[name] = [r.reshape(w.shape) for r in res]
    rp = _adamw(_pack([weights[n] for n in REPL_ORDER], REPL_ROWS),
                _pack([given["m_" + n] for n in REPL_ORDER], REPL_ROWS),
                _pack([given["v_" + n] for n in REPL_ORDER], REPL_ROWS), repl_parts, "adamw_replicated")
    rp_shapes = [weights[n].shape for n in REPL_ORDER]
    for k in range(4):
        for name, a in zip(REPL_ORDER, _unpack(rp[k], rp_shapes)):
            out.setdefault(name, [None] * 4)[k] = a
    results = [[out[n][k] for n in WEIGHT_ORDER] for k in range(4)]
    grad_w, delta_w, new_m, new_v = results
    return (loss, grad_x.reshape(nseq, seq, D), *grad_w, *delta_w, *new_m, *new_v)
```
